```python
import jax
import jax.numpy as jnp
from jax import lax
import numpy as np


D_MODEL = 1024
BATCH = 8
SEQ = 4096
DEPTH = 2

N_META = 16
BLOCK = 128
EPS = 1e-6
FOX_HEADS = 8
FOX_HEAD_DIM = 64
MLA_HEADS = 8
MLA_Q_RANK = 256
MLA_KV_RANK = 128
MLA_NOPE_DIM = 64
MLA_ROPE_DIM = 32
MLA_V_DIM = 64
ROPE_THETA = 10000.0
SWA_Q_HEADS = 8
SWA_KV_HEADS = 2
SWA_HEAD_DIM = 64
WINDOW = 128
N_BRANCH = 3
BRANCH_WIDTH = FOX_HEADS * FOX_HEAD_DIM
D_FF = 2816
CONV_WIDTH = 3
IN_SPLITS = (FOX_HEADS * FOX_HEAD_DIM, FOX_HEADS * FOX_HEAD_DIM, FOX_HEADS * FOX_HEAD_DIM, FOX_HEADS,
             MLA_Q_RANK, MLA_KV_RANK, MLA_ROPE_DIM,
             SWA_Q_HEADS * SWA_HEAD_DIM, SWA_KV_HEADS * SWA_HEAD_DIM, SWA_KV_HEADS * SWA_HEAD_DIM,
             N_BRANCH * D_MODEL)
IN_WIDTH = sum(IN_SPLITS)

kernel_name = 'hybrid_fox_mla_swa_convffn'


def rms_norm(x, g):
    xf = x.astype(jnp.float32)
    y = xf * lax.rsqrt(jnp.mean(xf * xf, axis=-1, keepdims=True) + EPS)
    return (y * g.astype(jnp.float32)).astype(x.dtype)


def rope(x, pos):
    half = x.shape[-1] // 2
    freqs = ROPE_THETA ** (-jnp.arange(half, dtype=jnp.float32) / half)
    ang = pos.astype(jnp.float32)[:, None] * freqs[None, :]
    cos = jnp.cos(ang)[:, None, :]
    sin = jnp.sin(ang)[:, None, :]
    xf = x.astype(jnp.float32)
    x1, x2 = xf[..., :half], xf[..., half:]
    return jnp.concatenate([x1 * cos - x2 * sin, x2 * cos + x1 * sin], axis=-1).astype(x.dtype)


def alibi_slopes(n_heads):
    return jnp.exp2(-8.0 * jnp.arange(1, n_heads + 1, dtype=jnp.float32) / n_heads)


def causal_block_attention(q, k, v, log_decay=None):
    B, L, H, dk = q.shape
    M = N_META
    nb = (L - M) // BLOCK
    scale = dk ** -0.5
    pos = jnp.arange(L)

    def attend(qb, qpos, kk, vv, kpos, qdec=None, kdec=None):
        s = jnp.einsum('bqhd,bkhd->bhqk', qb, kk, preferred_element_type=jnp.float32) * scale
        if qdec is not None:
            s = s + jnp.swapaxes(qdec, 1, 2)[..., :, None] - jnp.swapaxes(kdec, 1, 2)[..., None, :]
        s = jnp.where(kpos[None, :] <= qpos[:, None], s, -jnp.inf)
        p = jax.nn.softmax(s, axis=-1).astype(vv.dtype)
        return jnp.einsum('bhqk,bkhd->bqhd', p, vv)

    qb = q[:, M:].reshape(B, nb, BLOCK, H, dk).swapaxes(0, 1)
    posb = pos[M:].reshape(nb, BLOCK)
    if log_decay is not None:
        meta = attend(q[:, :M], pos[:M], k[:, :M], v[:, :M], pos[:M], log_decay[:, :M], log_decay[:, :M])
        decb = log_decay[:, M:].reshape(B, nb, BLOCK, H).swapaxes(0, 1)
        real = lax.map(lambda a: attend(a[0], a[1], k, v, pos, a[2], log_decay), (qb, posb, decb))
    else:
        meta = attend(q[:, :M], pos[:M], k[:, :M], v[:, :M], pos[:M])
        real = lax.map(lambda a: attend(a[0], a[1], k, v, pos), (qb, posb))
    real = real.swapaxes(0, 1).reshape(B, L - M, H, v.shape[-1])
    return jnp.concatenate([meta, real], axis=1)


def sliding_window_attention(q, k, v, sinks, slopes):
    B, L, Hq, d = q.shape
    Hkv = k.shape[2]
    G = Hq // Hkv
    M = N_META
    nb = (L - M) // BLOCK
    scale = d ** -0.5
    q = q.reshape(B, L, Hkv, G, d)
    sinks = sinks.astype(jnp.float32).reshape(Hkv, G)
    slopes = slopes.reshape(Hkv, G, 1, 1)

    def attend(qq, kk, vv, valid, dist):
        s = jnp.einsum('...qhgd,...khd->...hgqk', qq, kk, preferred_element_type=jnp.float32) * scale - slopes * dist
        s = jnp.where(valid, s, -jnp.inf)
        sink = jnp.broadcast_to(sinks[:, :, None, None], s.shape[:-1] + (1,))
        p = jax.nn.softmax(jnp.concatenate([s, sink], axis=-1), axis=-1)[..., :-1].astype(vv.dtype)
        return jnp.einsum('...hgqk,...khd->...qhgd', p, vv)

    mpos = jnp.arange(M)
    mdist = (mpos[:, None] - mpos[None, :]).astype(jnp.float32)
    meta = attend(q[:, :M], k[:, :M], v[:, :M], mdist >= 0, mdist)

    def band(t):
        tr = t[:, M:].reshape(B, nb, BLOCK, Hkv, d)
        prev = jnp.concatenate([jnp.zeros_like(tr[:, :1]), tr[:, :-1]], axis=1)
        meta_t = jnp.broadcast_to(t[:, None, :M], (B, nb, M, Hkv, d))
        return jnp.concatenate([meta_t, prev, tr], axis=2)

    blk = jnp.arange(nb)
    qpos = M + blk[:, None] * BLOCK + jnp.arange(BLOCK)[None, :]
    band_pos = M + (blk[:, None] - 1) * BLOCK + jnp.arange(2 * BLOCK)[None, :]
    band_dist = qpos[:, :, None] - band_pos[:, None, :]
    band_ok = (band_pos[:, None, :] >= M) & (band_dist >= 0) & (band_dist < WINDOW)
    meta_dist = qpos[:, :, None] - mpos[None, None, :]
    valid = jnp.concatenate([jnp.ones((nb, BLOCK, M), dtype=bool), band_ok], axis=-1)
    dist = jnp.concatenate([meta_dist, band_dist], axis=-1).astype(jnp.float32)
    qr = q[:, M:].reshape(B, nb, BLOCK, Hkv, G, d)
    real = attend(qr, band(k), band(v), valid[:, None, None], dist[:, None, None])
    return jnp.concatenate([meta.reshape(B, M, Hq, d), real.reshape(B, L - M, Hq, d)], axis=1)


def mixer_block(h, norm1_g, w_in, fox_forget_b, fox_q_g, fox_k_g, mla_q_a_g, mla_w_q_up, mla_kv_a_g,
                mla_w_kv_up, mla_q_g, mla_k_g, swa_q_g, swa_k_g, swa_sinks, w_branch, w_o):
    B, L, _ = h.shape
    pos = jnp.arange(L)
    xn = rms_norm(h, norm1_g)
    proj = xn @ w_in
    offsets = [int(o) for o in np.cumsum(IN_SPLITS)[:-1]]
    fq, fk, fv, ff, cq, ckv, krope, sq, sk, sv, gates = jnp.split(proj, offsets, axis=-1)

    fq = rms_norm(fq.reshape(B, L, FOX_HEADS, FOX_HEAD_DIM), fox_q_g)
    fk = rms_norm(fk.reshape(B, L, FOX_HEADS, FOX_HEAD_DIM), fox_k_g)
    fv = fv.reshape(B, L, FOX_HEADS, FOX_HEAD_DIM)
    log_decay = jnp.cumsum(jax.nn.log_sigmoid((ff + fox_forget_b).astype(jnp.float32)), axis=1)
    out_a = causal_block_attention(fq, fk, fv, log_decay)

    q = (rms_norm(cq, mla_q_a_g) @ mla_w_q_up).reshape(B, L, MLA_HEADS, MLA_NOPE_DIM + MLA_ROPE_DIM)
    q = rms_norm(q, mla_q_g)
    q = jnp.concatenate([q[..., :MLA_NOPE_DIM], rope(q[..., MLA_NOPE_DIM:], pos)], axis=-1)
    kv = (rms_norm(ckv, mla_kv_a_g) @ mla_w_kv_up).reshape(B, L, MLA_HEADS, MLA_NOPE_DIM + MLA_V_DIM)
    k_nope, v_mla = kv[..., :MLA_NOPE_DIM], kv[..., MLA_NOPE_DIM:]
    k = jnp.concatenate([k_nope, jnp.broadcast_to(krope[:, :, None, :], (B, L, MLA_HEADS, MLA_ROPE_DIM))], axis=-1)
    k = rms_norm(k, mla_k_g)
    k = jnp.concatenate([k[..., :MLA_NOPE_DIM], rope(k[..., MLA_NOPE_DIM:], pos)], axis=-1)
    out_b = causal_block_attention(q, k, v_mla)

    sq = rms_norm(sq.reshape(B, L, SWA_Q_HEADS, SWA_HEAD_DIM), swa_q_g)
    sk = rms_norm(sk.reshape(B, L, SWA_KV_HEADS, SWA_HEAD_DIM), swa_k_g)
    sv = sv.reshape(B, L, SWA_KV_HEADS, SWA_HEAD_DIM)
    out_c = sliding_window_attention(sq, sk, sv, swa_sinks, alibi_slopes(SWA_Q_HEADS))

    branches = jnp.stack([out_a.reshape(B, L, BRANCH_WIDTH), out_b.reshape(B, L, BRANCH_WIDTH),
                          out_c.reshape(B, L, BRANCH_WIDTH)], axis=2)
    g = jax.nn.sigmoid(gates.reshape(B, L, N_BRANCH, D_MODEL))
    y = jnp.einsum('blnc,ncd->blnd', branches, w_branch)
    merged = jnp.sum(g * y, axis=2)
    return h + merged @ w_o


def conv_ffn(h, norm2_g, w_up, conv_w, conv_b, w_down):
    L = h.shape[1]
    u = rms_norm(h, norm2_g) @ w_up
    up = jnp.pad(u, ((0, 0), (CONV_WIDTH - 1, 0), (0, 0)))
    c = conv_b
    for i in range(CONV_WIDTH):
        c = c + conv_w[i] * up[:, i:i + L]
    gate, val = jnp.split(c, 2, axis=-1)
    return h + (jax.nn.silu(gate) * val) @ w_down


def _fwd_setup_inputs(seed: int = 0) -> dict:
    key = jax.random.key(seed)
    ks = jax.random.split(key, 23)
    f32 = jnp.float32
    nrm = lambda k, shape, s: jax.random.normal(k, shape, f32) * s
    gain = lambda k, shape: 1.0 + 0.1 * jax.random.normal(k, shape, f32)
    return {
        'x': nrm(ks[0], (BATCH, SEQ, D_MODEL), 1.0),
        'meta_tokens': nrm(ks[1], (N_META, D_MODEL), 1.0),
        'norm1_g': gain(ks[2], (DEPTH, D_MODEL)),
        'w_in': nrm(ks[3], (DEPTH, D_MODEL, IN_WIDTH), D_MODEL ** -0.5),
        'fox_forget_b': 3.0 + 0.1 * jax.random.normal(ks[4], (DEPTH, FOX_HEADS), f32),
        'fox_q_g': gain(ks[5], (DEPTH, FOX_HEAD_DIM)),
        'fox_k_g': gain(ks[6], (DEPTH, FOX_HEAD_DIM)),
        'mla_q_a_g': gain(ks[7], (DEPTH, MLA_Q_RANK)),
        'mla_w_q_up': nrm(ks[8], (DEPTH, MLA_Q_RANK, MLA_HEADS * (MLA_NOPE_DIM + MLA_ROPE_DIM)), MLA_Q_RANK ** -0.5),
        'mla_kv_a_g': gain(ks[9], (DEPTH, MLA_KV_RANK)),
        'mla_w_kv_up': nrm(ks[10], (DEPTH, MLA_KV_RANK, MLA_HEADS * (MLA_NOPE_DIM + MLA_V_DIM)), MLA_KV_RANK ** -0.5),
        'mla_q_g': gain(ks[11], (DEPTH, MLA_NOPE_DIM + MLA_ROPE_DIM)),
        'mla_k_g': gain(ks[12], (DEPTH, MLA_NOPE_DIM + MLA_ROPE_DIM)),
        'swa_q_g': gain(ks[13], (DEPTH, SWA_HEAD_DIM)),
        'swa_k_g': gain(ks[14], (DEPTH, SWA_HEAD_DIM)),
        'swa_sinks': nrm(ks[15], (DEPTH, SWA_Q_HEADS), 0.5),
        'w_branch': nrm(ks[16], (DEPTH, N_BRANCH, BRANCH_WIDTH, D_MODEL), BRANCH_WIDTH ** -0.5),
        'w_o': nrm(ks[17], (DEPTH, D_MODEL, D_MODEL), D_MODEL ** -0.5),
        'norm2_g': gain(ks[18], (DEPTH, D_MODEL)),
        'ffn_w_up': nrm(ks[19], (DEPTH, D_MODEL, 2 * D_FF), D_MODEL ** -0.5),
        'ffn_conv_w': nrm(ks[20], (DEPTH, CONV_WIDTH, 2 * D_FF), CONV_WIDTH ** -0.5),
        'ffn_conv_b': nrm(ks[21], (DEPTH, 2 * D_FF), 0.02),
        'ffn_w_down': nrm(ks[22], (DEPTH, D_FF, D_MODEL), D_FF ** -0.5),
    }


def _fwd_reference(x, meta_tokens, norm1_g, w_in, fox_forget_b, fox_q_g, fox_k_g, mla_q_a_g, mla_w_q_up,
              mla_kv_a_g, mla_w_kv_up, mla_q_g, mla_k_g, swa_q_g, swa_k_g, swa_sinks, w_branch, w_o,
              norm2_g, ffn_w_up, ffn_conv_w, ffn_conv_b, ffn_w_down):
    B = x.shape[0]
    meta = jnp.broadcast_to(meta_tokens[None].astype(x.dtype), (B, N_META, x.shape[-1]))
    h = jnp.concatenate([meta, x], axis=1)
    for l in range(DEPTH):
        h = mixer_block(h, norm1_g[l], w_in[l], fox_forget_b[l], fox_q_g[l], fox_k_g[l], mla_q_a_g[l],
                        mla_w_q_up[l], mla_kv_a_g[l], mla_w_kv_up[l], mla_q_g[l], mla_k_g[l], swa_q_g[l],
                        swa_k_g[l], swa_sinks[l], w_branch[l], w_o[l])
        h = conv_ffn(h, norm2_g[l], ffn_w_up[l], ffn_conv_w[l], ffn_conv_b[l], ffn_w_down[l])
    return h[:, N_META:]


import jax as _jax
import jax.numpy as _jnp

TWIN_FORMAT = 'train_step'
FWD_PARAMS = ['x', 'meta_tokens', 'norm1_g', 'w_in', 'fox_forget_b', 'fox_q_g', 'fox_k_g', 'mla_q_a_g', 'mla_w_q_up', 'mla_kv_a_g', 'mla_w_kv_up', 'mla_q_g', 'mla_k_g', 'swa_q_g', 'swa_k_g', 'swa_sinks', 'w_branch', 'w_o', 'norm2_g', 'ffn_w_up', 'ffn_conv_w', 'ffn_conv_b', 'ffn_w_down']
TWIN_WEIGHTS = ['meta_tokens', 'norm1_g', 'w_in', 'fox_forget_b', 'fox_q_g', 'fox_k_g', 'mla_q_a_g', 'mla_w_q_up', 'mla_kv_a_g', 'mla_w_kv_up', 'mla_q_g', 'mla_k_g', 'swa_q_g', 'swa_k_g', 'swa_sinks', 'w_branch', 'w_o', 'norm2_g', 'ffn_w_up', 'ffn_conv_w', 'ffn_conv_b', 'ffn_w_down']
TWIN_DIFF_INPUT = 'x'
TWIN_INPUTS = ['x', 'meta_tokens', 'norm1_g', 'w_in', 'fox_forget_b', 'fox_q_g', 'fox_k_g', 'mla_q_a_g', 'mla_w_q_up', 'mla_kv_a_g', 'mla_w_kv_up', 'mla_q_g', 'mla_k_g', 'swa_q_g', 'swa_k_g', 'swa_sinks', 'w_branch', 'w_o', 'norm2_g', 'ffn_w_up', 'ffn_conv_w', 'ffn_conv_b', 'ffn_w_down', 'loss_target', 'm_meta_tokens', 'm_norm1_g', 'm_w_in', 'm_fox_forget_b', 'm_fox_q_g', 'm_fox_k_g', 'm_mla_q_a_g', 'm_mla_w_q_up', 'm_mla_kv_a_g', 'm_mla_w_kv_up', 'm_mla_q_g', 'm_mla_k_g', 'm_swa_q_g', 'm_swa_k_g', 'm_swa_sinks', 'm_w_branch', 'm_w_o', 'm_norm2_g', 'm_ffn_w_up', 'm_ffn_conv_w', 'm_ffn_conv_b', 'm_ffn_w_down', 'v_meta_tokens', 'v_norm1_g', 'v_w_in', 'v_fox_forget_b', 'v_fox_q_g', 'v_fox_k_g', 'v_mla_q_a_g', 'v_mla_w_q_up', 'v_mla_kv_a_g', 'v_mla_w_kv_up', 'v_mla_q_g', 'v_mla_k_g', 'v_swa_q_g', 'v_swa_k_g', 'v_swa_sinks', 'v_w_branch', 'v_w_o', 'v_norm2_g', 'v_ffn_w_up', 'v_ffn_conv_w', 'v_ffn_conv_b', 'v_ffn_w_down']
TWIN_OUTPUTS = ['loss', 'grad_x', 'grad_meta_tokens', 'grad_norm1_g', 'grad_w_in', 'grad_fox_forget_b', 'grad_fox_q_g', 'grad_fox_k_g', 'grad_mla_q_a_g', 'grad_mla_w_q_up', 'grad_mla_kv_a_g', 'grad_mla_w_kv_up', 'grad_mla_q_g', 'grad_mla_k_g', 'grad_swa_q_g', 'grad_swa_k_g', 'grad_swa_sinks', 'grad_w_branch', 'grad_w_o', 'grad_norm2_g', 'grad_ffn_w_up', 'grad_ffn_conv_w', 'grad_ffn_conv_b', 'grad_ffn_w_down', 'delta_meta_tokens', 'delta_norm1_g', 'delta_w_in', 'delta_fox_forget_b', 'delta_fox_q_g', 'delta_fox_k_g', 'delta_mla_q_a_g', 'delta_mla_w_q_up', 'delta_mla_kv_a_g', 'delta_mla_w_kv_up', 'delta_mla_q_g', 'delta_mla_k_g', 'delta_swa_q_g', 'delta_swa_k_g', 'delta_swa_sinks', 'delta_w_branch', 'delta_w_o', 'delta_norm2_g', 'delta_ffn_w_up', 'delta_ffn_conv_w', 'delta_ffn_conv_b', 'delta_ffn_w_down', 'new_m_meta_tokens', 'new_m_norm1_g', 'new_m_w_in', 'new_m_fox_forget_b', 'new_m_fox_q_g', 'new_m_fox_k_g', 'new_m_mla_q_a_g', 'new_m_mla_w_q_up', 'new_m_mla_kv_a_g', 'new_m_mla_w_kv_up', 'new_m_mla_q_g', 'new_m_mla_k_g', 'new_m_swa_q_g', 'new_m_swa_k_g', 'new_m_swa_sinks', 'new_m_w_branch', 'new_m_w_o', 'new_m_norm2_g', 'new_m_ffn_w_up', 'new_m_ffn_conv_w', 'new_m_ffn_conv_b', 'new_m_ffn_w_down', 'new_v_meta_tokens', 'new_v_norm1_g', 'new_v_w_in', 'new_v_fox_forget_b', 'new_v_fox_q_g', 'new_v_fox_k_g', 'new_v_mla_q_a_g', 'new_v_mla_w_q_up', 'new_v_mla_kv_a_g', 'new_v_mla_w_kv_up', 'new_v_mla_q_g', 'new_v_mla_k_g', 'new_v_swa_q_g', 'new_v_swa_k_g', 'new_v_swa_sinks', 'new_v_w_branch', 'new_v_w_o', 'new_v_norm2_g', 'new_v_ffn_w_up', 'new_v_ffn_conv_w', 'new_v_ffn_conv_b', 'new_v_ffn_w_down']
TWIN_LEAF_KINDS = {'loss': 'loss', 'grad_x': 'grad_x', 'grad_meta_tokens': 'grad_w', 'grad_norm1_g': 'grad_w', 'grad_w_in': 'grad_w', 'grad_fox_forget_b': 'grad_w', 'grad_fox_q_g': 'grad_w', 'grad_fox_k_g': 'grad_w', 'grad_mla_q_a_g': 'grad_w', 'grad_mla_w_q_up': 'grad_w', 'grad_mla_kv_a_g': 'grad_w', 'grad_mla_w_kv_up': 'grad_w', 'grad_mla_q_g': 'grad_w', 'grad_mla_k_g': 'grad_w', 'grad_swa_q_g': 'grad_w', 'grad_swa_k_g': 'grad_w', 'grad_swa_sinks': 'grad_w', 'grad_w_branch': 'grad_w', 'grad_w_o': 'grad_w', 'grad_norm2_g': 'grad_w', 'grad_ffn_w_up': 'grad_w', 'grad_ffn_conv_w': 'grad_w', 'grad_ffn_conv_b': 'grad_w', 'grad_ffn_w_down': 'grad_w', 'delta_meta_tokens': 'delta_w', 'delta_norm1_g': 'delta_w', 'delta_w_in': 'delta_w', 'delta_fox_forget_b': 'delta_w', 'delta_fox_q_g': 'delta_w', 'delta_fox_k_g': 'delta_w', 'delta_mla_q_a_g': 'delta_w', 'delta_mla_w_q_up': 'delta_w', 'delta_mla_kv_a_g': 'delta_w', 'delta_mla_w_kv_up': 'delta_w', 'delta_mla_q_g': 'delta_w', 'delta_mla_k_g': 'delta_w', 'delta_swa_q_g': 'delta_w', 'delta_swa_k_g': 'delta_w', 'delta_swa_sinks': 'delta_w', 'delta_w_branch': 'delta_w', 'delta_w_o': 'delta_w', 'delta_norm2_g': 'delta_w', 'delta_ffn_w_up': 'delta_w', 'delta_ffn_conv_w': 'delta_w', 'delta_ffn_conv_b': 'delta_w', 'delta_ffn_w_down': 'delta_w', 'new_m_meta_tokens': 'new_m', 'new_m_norm1_g': 'new_m', 'new_m_w_in': 'new_m', 'new_m_fox_forget_b': 'new_m', 'new_m_fox_q_g': 'new_m', 'new_m_fox_k_g': 'new_m', 'new_m_mla_q_a_g': 'new_m', 'new_m_mla_w_q_up': 'new_m', 'new_m_mla_kv_a_g': 'new_m', 'new_m_mla_w_kv_up': 'new_m', 'new_m_mla_q_g': 'new_m', 'new_m_mla_k_g': 'new_m', 'new_m_swa_q_g': 'new_m', 'new_m_swa_k_g': 'new_m', 'new_m_swa_sinks': 'new_m', 'new_m_w_branch': 'new_m', 'new_m_w_o': 'new_m', 'new_m_norm2_g': 'new_m', 'new_m_ffn_w_up': 'new_m', 'new_m_ffn_conv_w': 'new_m', 'new_m_ffn_conv_b': 'new_m', 'new_m_ffn_w_down': 'new_m', 'new_v_meta_tokens': 'new_v', 'new_v_norm1_g': 'new_v', 'new_v_w_in': 'new_v', 'new_v_fox_forget_b': 'new_v', 'new_v_fox_q_g': 'new_v', 'new_v_fox_k_g': 'new_v', 'new_v_mla_q_a_g': 'new_v', 'new_v_mla_w_q_up': 'new_v', 'new_v_mla_kv_a_g': 'new_v', 'new_v_mla_w_kv_up': 'new_v', 'new_v_mla_q_g': 'new_v', 'new_v_mla_k_g': 'new_v', 'new_v_swa_q_g': 'new_v', 'new_v_swa_k_g': 'new_v', 'new_v_swa_sinks': 'new_v', 'new_v_w_branch': 'new_v', 'new_v_w_o': 'new_v', 'new_v_norm2_g': 'new_v', 'new_v_ffn_w_up': 'new_v', 'new_v_ffn_conv_w': 'new_v', 'new_v_ffn_conv_b': 'new_v', 'new_v_ffn_w_down': 'new_v'}


def _forward(args):
    return _fwd_reference(*[args[k] for k in FWD_PARAMS])


def _output_shape():
    def fwd():
        inp = _fwd_setup_inputs(0)
        return _fwd_reference(*[inp[k] for k in FWD_PARAMS])
    out = _jax.eval_shape(fwd)
    return out.shape, out.dtype

N_MICROBATCH = 1
ADAM_LR = 0.001
ADAM_B1 = 0.9
ADAM_B2 = 0.999
ADAM_EPS = 1e-08
ADAM_WD = 0.01
ADAM_STEP = 10
PER_EXAMPLE_BATCH_AXIS = {'x': 0, 'loss_target': 0}
SHARED_INPUTS = []
_WEIGHT_DTYPES = {'meta_tokens': _jnp.float32, 'norm1_g': _jnp.float32, 'w_in': _jnp.float32, 'fox_forget_b': _jnp.float32, 'fox_q_g': _jnp.float32, 'fox_k_g': _jnp.float32, 'mla_q_a_g': _jnp.float32, 'mla_w_q_up': _jnp.float32, 'mla_kv_a_g': _jnp.float32, 'mla_w_kv_up': _jnp.float32, 'mla_q_g': _jnp.float32, 'mla_k_g': _jnp.float32, 'swa_q_g': _jnp.float32, 'swa_k_g': _jnp.float32, 'swa_sinks': _jnp.float32, 'w_branch': _jnp.float32, 'w_o': _jnp.float32, 'norm2_g': _jnp.float32, 'ffn_w_up': _jnp.float32, 'ffn_conv_w': _jnp.float32, 'ffn_conv_b': _jnp.float32, 'ffn_w_down': _jnp.float32}
MOMENT_SCALE = {'meta_tokens': 5.748946e-02, 'norm1_g': 1.876743e+00, 'w_in': 1.317482e-01, 'fox_forget_b': 3.812588e+01, 'fox_q_g': 7.246283e+00, 'fox_k_g': 7.242996e+00, 'mla_q_a_g': 1.236010e-01, 'mla_w_q_up': 6.900168e-02, 'mla_kv_a_g': 9.409152e-01, 'mla_w_kv_up': 1.469727e-01, 'mla_q_g': 8.287560e-01, 'mla_k_g': 8.228594e-01, 'swa_q_g': 6.220561e+00, 'swa_k_g': 6.248779e+00, 'swa_sinks': 2.050373e+01, 'w_branch': 1.312976e-01, 'w_o': 2.241139e-01, 'norm2_g': 2.674513e+01, 'ffn_w_up': 2.946655e-01, 'ffn_conv_w': 3.758632e+00, 'ffn_conv_b': 3.402279e+00, 'ffn_w_down': 4.351883e-01}


def _to_microbatches(a, axis):
    t = _jnp.moveaxis(a, axis, 0)
    t = t.reshape((N_MICROBATCH, t.shape[0] // N_MICROBATCH) + t.shape[1:])
    return _jnp.moveaxis(t, 1, axis + 1)


def setup_inputs(seed: int = 0) -> dict:
    inp = _fwd_setup_inputs(seed)
    key = _jax.random.fold_in(_jax.random.key(seed), 7919)
    shape, _ = _output_shape()
    out = dict(inp)
    out["loss_target"] = _jax.random.normal(_jax.random.fold_in(key, 0), shape, _jnp.float32)
    for i, name in enumerate(TWIN_WEIGHTS):
        w = inp[name].astype(_jnp.float32)
        if MOMENT_SCALE is None:
            s = _jnp.sqrt(_jnp.mean(_jnp.square(w)) + 1e-30)
        else:
            s = MOMENT_SCALE[name]
        km, kv = _jax.random.split(_jax.random.fold_in(key, i + 1))
        out[name] = w
        out["m_" + name] = s * _jax.random.normal(km, w.shape, _jnp.float32)
        out["v_" + name] = (s * s) * _jax.random.uniform(kv, w.shape, _jnp.float32, 0.5, 1.5)
    if N_MICROBATCH > 1:
        for name, axis in PER_EXAMPLE_BATCH_AXIS.items():
            out[name] = _to_microbatches(out[name], axis)
    return {'x': out['x'], 'meta_tokens': out['meta_tokens'], 'norm1_g': out['norm1_g'], 'w_in': out['w_in'], 'fox_forget_b': out['fox_forget_b'], 'fox_q_g': out['fox_q_g'], 'fox_k_g': out['fox_k_g'], 'mla_q_a_g': out['mla_q_a_g'], 'mla_w_q_up': out['mla_w_q_up'], 'mla_kv_a_g': out['mla_kv_a_g'], 'mla_w_kv_up': out['mla_w_kv_up'], 'mla_q_g': out['mla_q_g'], 'mla_k_g': out['mla_k_g'], 'swa_q_g': out['swa_q_g'], 'swa_k_g': out['swa_k_g'], 'swa_sinks': out['swa_sinks'], 'w_branch': out['w_branch'], 'w_o': out['w_o'], 'norm2_g': out['norm2_g'], 'ffn_w_up': out['ffn_w_up'], 'ffn_conv_w': out['ffn_conv_w'], 'ffn_conv_b': out['ffn_conv_b'], 'ffn_w_down': out['ffn_w_down'], 'loss_target': out['loss_target'], 'm_meta_tokens': out['m_meta_tokens'], 'm_norm1_g': out['m_norm1_g'], 'm_w_in': out['m_w_in'], 'm_fox_forget_b': out['m_fox_forget_b'], 'm_fox_q_g': out['m_fox_q_g'], 'm_fox_k_g': out['m_fox_k_g'], 'm_mla_q_a_g': out['m_mla_q_a_g'], 'm_mla_w_q_up': out['m_mla_w_q_up'], 'm_mla_kv_a_g': out['m_mla_kv_a_g'], 'm_mla_w_kv_up': out['m_mla_w_kv_up'], 'm_mla_q_g': out['m_mla_q_g'], 'm_mla_k_g': out['m_mla_k_g'], 'm_swa_q_g': out['m_swa_q_g'], 'm_swa_k_g': out['m_swa_k_g'], 'm_swa_sinks': out['m_swa_sinks'], 'm_w_branch': out['m_w_branch'], 'm_w_o': out['m_w_o'], 'm_norm2_g': out['m_norm2_g'], 'm_ffn_w_up': out['m_ffn_w_up'], 'm_ffn_conv_w': out['m_ffn_conv_w'], 'm_ffn_conv_b': out['m_ffn_conv_b'], 'm_ffn_w_down': out['m_ffn_w_down'], 'v_meta_tokens': out['v_meta_tokens'], 'v_norm1_g': out['v_norm1_g'], 'v_w_in': out['v_w_in'], 'v_fox_forget_b': out['v_fox_forget_b'], 'v_fox_q_g': out['v_fox_q_g'], 'v_fox_k_g': out['v_fox_k_g'], 'v_mla_q_a_g': out['v_mla_q_a_g'], 'v_mla_w_q_up': out['v_mla_w_q_up'], 'v_mla_kv_a_g': out['v_mla_kv_a_g'], 'v_mla_w_kv_up': out['v_mla_w_kv_up'], 'v_mla_q_g': out['v_mla_q_g'], 'v_mla_k_g': out['v_mla_k_g'], 'v_swa_q_g': out['v_swa_q_g'], 'v_swa_k_g': out['v_swa_k_g'], 'v_swa_sinks': out['v_swa_sinks'], 'v_w_branch': out['v_w_branch'], 'v_w_o': out['v_w_o'], 'v_norm2_g': out['v_norm2_g'], 'v_ffn_w_up': out['v_ffn_w_up'], 'v_ffn_conv_w': out['v_ffn_conv_w'], 'v_ffn_conv_b': out['v_ffn_conv_b'], 'v_ffn_w_down': out['v_ffn_w_down']}


def _loss(weights, diff, rest, loss_target):
    with _jax.named_scope("forward"):
        args = {**rest, TWIN_DIFF_INPUT: diff, **{k: w.astype(_WEIGHT_DTYPES[k]) for k, w in weights.items()}}
        y = _forward(args)
    with _jax.named_scope("loss_head"):
        err = _jnp.square(y.astype(_jnp.float32) - loss_target)
        return 0.5 * _jnp.sum(_jnp.mean(err, axis=-1)) if err.ndim else 0.5 * err


def _adamw(w, g, m, v):
    m = ADAM_B1 * m + (1.0 - ADAM_B1) * g
    v = ADAM_B2 * v + (1.0 - ADAM_B2) * _jnp.square(g)
    m_hat = m / (1.0 - ADAM_B1 ** ADAM_STEP)
    v_hat = v / (1.0 - ADAM_B2 ** ADAM_STEP)
    delta = -ADAM_LR * (m_hat / (_jnp.sqrt(v_hat) + ADAM_EPS) + ADAM_WD * w)
    return delta, m, v


def reference(x, meta_tokens, norm1_g, w_in, fox_forget_b, fox_q_g, fox_k_g, mla_q_a_g, mla_w_q_up, mla_kv_a_g, mla_w_kv_up, mla_q_g, mla_k_g, swa_q_g, swa_k_g, swa_sinks, w_branch, w_o, norm2_g, ffn_w_up, ffn_conv_w, ffn_conv_b, ffn_w_down, loss_target, m_meta_tokens, m_norm1_g, m_w_in, m_fox_forget_b, m_fox_q_g, m_fox_k_g, m_mla_q_a_g, m_mla_w_q_up, m_mla_kv_a_g, m_mla_w_kv_up, m_mla_q_g, m_mla_k_g, m_swa_q_g, m_swa_k_g, m_swa_sinks, m_w_branch, m_w_o, m_norm2_g, m_ffn_w_up, m_ffn_conv_w, m_ffn_conv_b, m_ffn_w_down, v_meta_tokens, v_norm1_g, v_w_in, v_fox_forget_b, v_fox_q_g, v_fox_k_g, v_mla_q_a_g, v_mla_w_q_up, v_mla_kv_a_g, v_mla_w_kv_up, v_mla_q_g, v_mla_k_g, v_swa_q_g, v_swa_k_g, v_swa_sinks, v_w_branch, v_w_o, v_norm2_g, v_ffn_w_up, v_ffn_conv_w, v_ffn_conv_b, v_ffn_w_down):
    given = dict(x=x, meta_tokens=meta_tokens, norm1_g=norm1_g, w_in=w_in, fox_forget_b=fox_forget_b, fox_q_g=fox_q_g, fox_k_g=fox_k_g, mla_q_a_g=mla_q_a_g, mla_w_q_up=mla_w_q_up, mla_kv_a_g=mla_kv_a_g, mla_w_kv_up=mla_w_kv_up, mla_q_g=mla_q_g, mla_k_g=mla_k_g, swa_q_g=swa_q_g, swa_k_g=swa_k_g, swa_sinks=swa_sinks, w_branch=w_branch, w_o=w_o, norm2_g=norm2_g, ffn_w_up=ffn_w_up, ffn_conv_w=ffn_conv_w, ffn_conv_b=ffn_conv_b, ffn_w_down=ffn_w_down, loss_target=loss_target, m_meta_tokens=m_meta_tokens, m_norm1_g=m_norm1_g, m_w_in=m_w_in, m_fox_forget_b=m_fox_forget_b, m_fox_q_g=m_fox_q_g, m_fox_k_g=m_fox_k_g, m_mla_q_a_g=m_mla_q_a_g, m_mla_w_q_up=m_mla_w_q_up, m_mla_kv_a_g=m_mla_kv_a_g, m_mla_w_kv_up=m_mla_w_kv_up, m_mla_q_g=m_mla_q_g, m_mla_k_g=m_mla_k_g, m_swa_q_g=m_swa_q_g, m_swa_k_g=m_swa_k_g, m_swa_sinks=m_swa_sinks, m_w_branch=m_w_branch, m_w_o=m_w_o, m_norm2_g=m_norm2_g, m_ffn_w_up=m_ffn_w_up, m_ffn_conv_w=m_ffn_conv_w, m_ffn_conv_b=m_ffn_conv_b, m_ffn_w_down=m_ffn_w_down, v_meta_tokens=v_meta_tokens, v_norm1_g=v_norm1_g, v_w_in=v_w_in, v_fox_forget_b=v_fox_forget_b, v_fox_q_g=v_fox_q_g, v_fox_k_g=v_fox_k_g, v_mla_q_a_g=v_mla_q_a_g, v_mla_w_q_up=v_mla_w_q_up, v_mla_kv_a_g=v_mla_kv_a_g, v_mla_w_kv_up=v_mla_w_kv_up, v_mla_q_g=v_mla_q_g, v_mla_k_g=v_mla_k_g, v_swa_q_g=v_swa_q_g, v_swa_k_g=v_swa_k_g, v_swa_sinks=v_swa_sinks, v_w_branch=v_w_branch, v_w_o=v_w_o, v_norm2_g=v_norm2_g, v_ffn_w_up=v_ffn_w_up, v_ffn_conv_w=v_ffn_conv_w, v_ffn_conv_b=v_ffn_conv_b, v_ffn_w_down=v_ffn_w_down)
    weights = {n: given[n] for n in TWIN_WEIGHTS}
    shared = {n: given[n] for n in SHARED_INPUTS}
    per_example = {n: given[n] for n in ['x']}
    grad_fn = _jax.value_and_grad(_loss, argnums=(0, 1))

    def one_microbatch(ex, loss_target):
        ex = dict(ex)
        diff = ex.pop(TWIN_DIFF_INPUT)
        return grad_fn(weights, diff, {**shared, **ex}, loss_target)

    if N_MICROBATCH == 1:
        loss, (grad_w, grad_x) = one_microbatch(per_example, given["loss_target"])
    else:
        def body(carry, xs):
            loss_sum, grad_sum = carry
            l_k, (gw_k, gx_k) = one_microbatch(xs[0], xs[1])
            with _jax.named_scope("update"):
                return (loss_sum + l_k, _jax.tree.map(_jnp.add, grad_sum, gw_k)), gx_k

        init = (_jnp.zeros((), _jnp.float32), _jax.tree.map(_jnp.zeros_like, weights))
        (loss, grad_w), grad_x = _jax.lax.scan(body, init, (per_example, given["loss_target"]))
    with _jax.named_scope("update"):
        delta_w, new_m, new_v = {}, {}, {}
        for n in TWIN_WEIGHTS:
            delta_w[n], new_m[n], new_v[n] = _adamw(weights[n], grad_w[n], given["m_" + n], given["v_" + n])
    return (loss, grad_x, *[grad_w[n] for n in TWIN_WEIGHTS], *[delta_w[n] for n in TWIN_WEIGHTS],
            *[new_m[n] for n in TWIN_WEIGHTS], *[new_v[n] for n in TWIN_WEIGHTS])
```

```python
import functools
import math

import numpy as np
import jax
import jax.numpy as jnp
from jax import lax
from jax.experimental import pallas as pl
from jax.experimental.pallas import tpu as pltpu

F32, BF16 = jnp.float32, jnp.bfloat16
SDS = jax.ShapeDtypeStruct

D_MODEL = 1024
N_META = 16
EPS = 1e-6
WINDOW = 128
ROPE_THETA = 10000.0
HEADS = 8
D_FF = 2816
DEPTH = 2
N_DEV = 8
ADAM_LR, ADAM_B1, ADAM_B2, ADAM_EPS, ADAM_WD, ADAM_STEP = 0.001, 0.9, 0.999, 1e-08, 0.01, 10

ROW_ALIGN = 384
TILE_MM = 384
TILE_ROW = 128
TILE_ATT = 384
TILE_POST = 128
VMEM_LIMIT = 56 * 1024 * 1024

GATES_W = 3072
OTHER_W = 2816
IN_W = GATES_W + OTHER_W
O_FQ, O_FK, O_FV, O_SQ, O_SK, O_SV, O_CQ, O_CKV, O_MISC = 0, 512, 1024, 1536, 2048, 2176, 2304, 2560, 2688
FF_LANE = 32

NEG = -1e30


def _dot(a, b):
    return jnp.dot(a, b, preferred_element_type=F32)


def _dot_nt(a, b):
    return lax.dot_general(a, b, (((1,), (1,)), ((), ())), preferred_element_type=F32)


def _dot_tn(a, b):
    return lax.dot_general(a, b, (((0,), (0,)), ((), ())), preferred_element_type=F32)


def _params(sem):
    return pltpu.CompilerParams(dimension_semantics=sem, vmem_limit_bytes=VMEM_LIMIT)


def _rms(x, g):
    return x * lax.rsqrt(jnp.mean(x * x, axis=-1, keepdims=True) + EPS) * g


def _split_dot(x, m, pieces=2):
    acc, rest = None, x
    for _ in range(pieces):
        part = rest.astype(BF16)
        rest = rest - part.astype(F32)
        acc = _dot(part, m) if acc is None else acc + _dot(part, m)
    return acc


@jax.custom_vjp
def _sel(x, m, mt):
    return _split_dot(x, m)


_sel.defvjp(lambda x, m, mt: (_split_dot(x, m), (m, mt)), lambda res, dy: (_split_dot(dy, res[1]), None, None))


@jax.custom_vjp
def _mm(x, w):
    return _dot(x.astype(BF16), w.astype(BF16))


def _mm_bwd(res, dy):
    x, w = res
    dyb = dy.astype(BF16)
    return _dot_nt(dyb, w.astype(BF16)), _dot_tn(x.astype(BF16), dyb)


_mm.defvjp(lambda x, w: (_mm(x, w), (x, w)), _mm_bwd)


def _rot_impl(x):
    w = x.shape[1]
    lane = lax.broadcasted_iota(jnp.int32, x.shape, 1) % 128
    lo = (lane >= 64) & (lane < 80)
    hi = (lane >= 80) & (lane < 96)
    return jnp.where(hi, pltpu.roll(x, 16, 1), 0.0) - jnp.where(lo, pltpu.roll(x, w - 16, 1), 0.0)


@jax.custom_vjp
def _rot(x):
    return _rot_impl(x)


_rot.defvjp(lambda x: (_rot_impl(x), None), lambda _, dy: (-_rot_impl(dy),))


def _gnorm(x, g, e, et, dim):
    inv = lax.rsqrt(_sel(x * x, e, et) * (1.0 / dim) + EPS)
    return x * _sel(inv, et, e) * g


def _indicator(width, period):
    m = np.zeros((width, 128), np.float32)
    m[np.arange(width), np.arange(width) // period] = 1.0
    return m


def _consts():
    e64 = _indicator(512, 64)
    e128 = _indicator(1024, 128)
    sk = np.zeros((128, 1024), np.float32)
    for h in range(HEADS):
        sk[np.arange(32), 128 * h + 64 + np.arange(32)] = 1.0
    dup = np.zeros((128, 256), np.float32)
    for g in range(2):
        for r in range(2):
            dup[64 * g + np.arange(64), 128 * g + 64 * r + np.arange(64)] = 1.0
    mats = [e64, e64.T, e128, e128.T, sk, sk.T, dup, dup.T]
    return [jnp.asarray(m, BF16) for m in mats]


def _fold_matrix(width, period):
    m = np.zeros((width, 128), np.float32)
    m[np.arange(width), np.arange(width) % period] = 1.0
    return jnp.asarray(m, BF16)


def _rope_tables(lp):
    half = 16
    freqs = ROPE_THETA ** (-np.arange(half, dtype=np.float32) / half)
    ang = np.arange(lp, dtype=np.float32)[:, None] * freqs[None, :]
    cos = np.ones((lp, 128), np.float32)
    sin = np.zeros((lp, 128), np.float32)
    cos[:, 64:80] = np.cos(ang)
    cos[:, 80:96] = np.cos(ang)
    sin[:, 64:80] = np.sin(ang)
    sin[:, 80:96] = np.sin(ang)
    return jnp.asarray(cos), jnp.asarray(sin)


def _norm_matmul(h, g, w, tn, name):
    lp, d = h.shape
    n = w.shape[1]
    tb = TILE_MM

    def body(h_ref, g_ref, w_ref, xn_ref, y_ref):
        @pl.when(pl.program_id(1) == 0)
        def _():
            xn_ref[...] = _rms(h_ref[...], g_ref[...]).astype(BF16)

        y_ref[...] = _dot(xn_ref[...], w_ref[...])

    return pl.pallas_call(
        body, name=name, grid=(lp // tb, n // tn),
        in_specs=[pl.BlockSpec((tb, d), lambda i, j: (i, 0)), pl.BlockSpec((1, d), lambda i, j: (0, 0)),
                  pl.BlockSpec((d, tn), lambda i, j: (0, j))],
        out_specs=[pl.BlockSpec((tb, d), lambda i, j: (i, 0)), pl.BlockSpec((tb, tn), lambda i, j: (i, j))],
        out_shape=[SDS((lp, d), BF16), SDS((lp, n), F32)],
        compiler_params=_params(("parallel", "arbitrary")),
    )(h, g, w)


def _matmul_residual(a, w, res, name):
    m, k = a.shape
    n = w.shape[1]
    tb = TILE_MM

    def body(a_ref, w_ref, r_ref, o_ref):
        o_ref[...] = r_ref[...] + _dot(a_ref[...], w_ref[...])

    return pl.pallas_call(
        body, name=name, grid=(m // tb,),
        in_specs=[pl.BlockSpec((tb, k), lambda i: (i, 0)), pl.BlockSpec((k, n), lambda i: (0, 0)),
                  pl.BlockSpec((tb, n), lambda i: (i, 0))],
        out_specs=pl.BlockSpec((tb, n), lambda i: (i, 0)),
        out_shape=SDS((m, n), F32),
        compiler_params=_params(("parallel",)),
    )(a, w, res)


def _matmul_nt(dy, w, tn, name):
    m, k = dy.shape
    n = w.shape[0]
    tb = TILE_MM

    def body(dy_ref, w_ref, o_ref):
        o_ref[...] = _dot_nt(dy_ref[...].astype(BF16), w_ref[...])

    return pl.pallas_call(
        body, name=name, grid=(m // tb, n // tn),
        in_specs=[pl.BlockSpec((tb, k), lambda i, j: (i, 0)), pl.BlockSpec((tn, k), lambda i, j: (j, 0))],
        out_specs=pl.BlockSpec((tb, tn), lambda i, j: (i, j)),
        out_shape=SDS((m, n), F32),
        compiler_params=_params(("parallel", "arbitrary")),
    )(dy, w)


def _matmul_tn(x, dy, tn, name):
    m, k = x.shape
    n = dy.shape[1]
    tb = TILE_MM

    def body(x_ref, dy_ref, o_ref):
        @pl.when(pl.program_id(1) == 0)
        def _():
            o_ref[...] = jnp.zeros_like(o_ref)

        o_ref[...] += _dot_tn(x_ref[...].astype(BF16), dy_ref[...].astype(BF16))

    return pl.pallas_call(
        body, name=name, grid=(n // tn, m // tb),
        in_specs=[pl.BlockSpec((tb, k), lambda j, i: (i, 0)), pl.BlockSpec((tb, tn), lambda j, i: (i, j))],
        out_specs=pl.BlockSpec((k, tn), lambda j, i: (0, j)),
        out_shape=SDS((k, n), F32),
        compiler_params=_params(("parallel", "arbitrary")),
    )(x, dy)


def _norm_matmul_bwd(dy1, w1, dy2, w2, x, g, dres, specs, name):
    m, d = x.shape
    tb = TILE_MM
    (dy1_spec, w1_spec, dy2_spec, w2_spec) = specs

    def body(dy1_ref, w1_ref, dy2_ref, w2_ref, x_ref, g_ref, r_ref, o_ref, dg_ref):
        @pl.when(pl.program_id(0) == 0)
        def _():
            dg_ref[...] = jnp.zeros_like(dg_ref)

        dxn = _dot_nt(dy1_ref[...], w1_ref[...]) + _dot_nt(dy2_ref[...], w2_ref[...])
        _, vjp = jax.vjp(_rms, x_ref[...], g_ref[...])
        dx, dg = vjp(dxn)
        o_ref[...] = r_ref[...] + dx
        dg_ref[...] += dg

    row = pl.BlockSpec((tb, d), lambda i: (i, 0))
    vec = pl.BlockSpec((1, d), lambda i: (0, 0))
    return pl.pallas_call(
        body, name=name, grid=(m // tb,),
        in_specs=[dy1_spec, w1_spec, dy2_spec, w2_spec, row, vec, row],
        out_specs=[row, vec],
        out_shape=[SDS((m, d), F32), SDS((1, d), F32)],
        compiler_params=_params(("arbitrary",)),
    )(dy1, w1, dy2, w2, x, g, dres)


def _prep_math(pieces, prm, consts, cos, sin):
    fq, fk, sq, sk, sv, cq, ckv, misc = pieces
    gfq, gfk, gsq, gsk, fb, gqa, gkva, gmq, gmk, wq, wkk, wkv = prm
    e64, e64t, e128, e128t, skm, skt, dup, dupt = consts
    cos8 = jnp.concatenate([cos] * HEADS, axis=1)
    sin8 = jnp.concatenate([sin] * HEADS, axis=1)
    fq_n = _gnorm(fq, gfq, e64, e64t, 64)
    fk_n = _gnorm(fk, gfk, e64, e64t, 64)
    ls = jax.nn.log_sigmoid(misc + fb)
    q = _gnorm(_mm(_rms(cq, gqa), wq), gmq, e128, e128t, 96)
    mq = q * cos8 + _rot(q) * sin8
    kva = _rms(ckv, gkva)
    k = _gnorm(_mm(kva, wkk) + _sel(misc, skm, skt), gmk, e128, e128t, 96)
    mk = k * cos8 + _rot(k) * sin8
    mv = _mm(kva, wkv)
    sq_n = _gnorm(sq, gsq, e64, e64t, 64)
    sk_n = _gnorm(sk, gsk, e64[0:128], e64t[:, 0:128], 64)
    skd = _sel(sk_n, dup, dupt)
    svd = _sel(sv, dup, dupt)
    return fq_n, fk_n, ls, mq, mk, mv, sq_n, skd, svd


_PIECES = [(O_FQ, 512), (O_FK, 512), (O_SQ, 512), (O_SK, 128), (O_SV, 128), (O_CQ, 256), (O_CKV, 128), (O_MISC, 128)]
_PRM_SHAPES = [(1, 512), (1, 512), (1, 512), (1, 128), (1, 128), (1, 256), (1, 128), (1, 1024), (1, 1024),
               (256, 1024), (128, 1024), (128, 512)]
_CONST_SHAPES = [(512, 128), (128, 512), (1024, 128), (128, 1024), (128, 1024), (1024, 128), (128, 256), (256, 128)]


def _piece_specs(tb):
    def spec(off, width):
        blk = (GATES_W + off) // width
        return pl.BlockSpec((tb, width), lambda i, blk=blk: (i, blk))
    return [spec(o, w) for o, w in _PIECES] + [spec(O_FV, 512)]


def _full_specs(shapes):
    return [pl.BlockSpec(s, lambda i: (0, 0)) for s in shapes]


def _prep_fwd(proj, prm, consts, cos, sin, name):
    lp = proj.shape[0]
    tb = TILE_ROW
    row = lambda w: pl.BlockSpec((tb, w), lambda i: (i, 0))

    def body(*refs):
        pieces = [r[...] for r in refs[0:8]]
        fv = refs[8][...]
        prm_v = [r[...] for r in refs[9:21]]
        consts_v = [r[...] for r in refs[21:29]]
        cos_v, sin_v = refs[29][...], refs[30][...]
        outs = refs[31:]
        fq_n, fk_n, ls, mq, mk, mv, sq_n, skd, svd = _prep_math(pieces, prm_v, consts_v, cos_v, sin_v)
        for ref, val in zip(outs, (fq_n, fk_n, fv, mq, mk, mv, sq_n, skd, svd)):
            ref[...] = val.astype(BF16)
        outs[9][...] = ls

    widths = [512, 512, 512, 1024, 1024, 512, 512, 256, 256]
    return pl.pallas_call(
        body, name=name, grid=(lp // tb,),
        in_specs=_piece_specs(tb) + _full_specs(_PRM_SHAPES) + _full_specs(_CONST_SHAPES) + [row(128), row(128)],
        out_specs=[row(w) for w in widths] + [row(128)],
        out_shape=[SDS((lp, w), BF16) for w in widths] + [SDS((lp, 128), F32)],
        compiler_params=_params(("parallel",)),
    )(*([proj] * 9), *prm, *consts, cos, sin)


def _prep_bwd(proj, prm, consts, cos, sin, cots, folds, name):
    lp = proj.shape[0]
    tb = TILE_ROW
    row = lambda w: pl.BlockSpec((tb, w), lambda i: (i, 0))
    fold64, fold128 = folds

    def body(*refs):
        pieces = [r[...] for r in refs[0:8]]
        prm_v = [r[...] for r in refs[9:21]]
        consts_v = [r[...] for r in refs[21:29]]
        cos_v, sin_v = refs[29][...], refs[30][...]
        dfq, dfk, dfv, dmq, dmk, dmv, dsq, dskp, dsvp, dls = [r[...] for r in refs[31:41]]
        f64, f128 = refs[41][...], refs[42][...]
        d_ref = refs[43]
        g_refs = refs[44:]

        @pl.when(pl.program_id(0) == 0)
        def _():
            for r in g_refs:
                r[...] = jnp.zeros_like(r)

        def pair_sum(p):
            return jnp.concatenate([p[:, 0:128] + p[:, 128:256], p[:, 256:384] + p[:, 384:512]], axis=1)

        f = lambda pc, pr: _prep_math(pc, pr, consts_v, cos_v, sin_v)
        _, vjp = jax.vjp(f, pieces, prm_v)
        dpc, dprm = vjp((dfq, dfk, dls, dmq, dmk, dmv, dsq, pair_sum(dskp), pair_sum(dsvp)))
        d_fq, d_fk, d_sq, d_sk, d_sv, d_cq, d_ckv, d_misc = dpc
        for off, val in ((O_FQ, d_fq), (O_FK, d_fk), (O_FV, dfv), (O_SQ, d_sq), (O_SK, d_sk), (O_SV, d_sv),
                         (O_CQ, d_cq), (O_CKV, d_ckv), (O_MISC, d_misc)):
            d_ref[:, off:off + val.shape[1]] = val.astype(BF16)
        folded = {0: f64, 1: f64, 2: f64, 3: f64[0:128], 7: f128, 8: f128}
        for idx, (ref, val) in enumerate(zip(g_refs, dprm)):
            if idx in folded:
                ref[...] += _split_dot(jnp.broadcast_to(val, (8, val.shape[1])), folded[idx], 3)
            elif val.shape[0] == 1:
                ref[...] += jnp.broadcast_to(val, ref.shape)
            else:
                ref[...] += val

    g_shapes = [(8, 128), (8, 128), (8, 128), (8, 128), (8, 128), (8, 256), (8, 128), (8, 128), (8, 128),
                (256, 1024), (128, 1024), (128, 512)]
    cot_widths = [512, 512, 512, 1024, 1024, 512, 512, 512, 512, 128]
    return pl.pallas_call(
        body, name=name, grid=(lp // tb,),
        in_specs=(_piece_specs(tb) + _full_specs(_PRM_SHAPES) + _full_specs(_CONST_SHAPES) + [row(128), row(128)]
                  + [row(w) for w in cot_widths] + _full_specs([(512, 128), (1024, 128)])),
        out_specs=[row(OTHER_W)] + _full_specs(g_shapes),
        out_shape=[SDS((lp, OTHER_W), BF16)] + [SDS(s, F32) for s in g_shapes],
        compiler_params=_params(("arbitrary",)),
    )(*([proj] * 9), *prm, *consts, cos, sin, *cots, fold64, fold128)


def _cumsum(xs, reverse, name):
    lp = xs[0].shape[0]
    tb = TILE_MM
    nb = lp // tb
    n_in = len(xs)
    idx = (lambda i: (nb - 1 - i, 0)) if reverse else (lambda i: (i, 0))

    def body(*refs):
        o_ref, carry = refs[n_in], refs[n_in + 1]

        @pl.when(pl.program_id(0) == 0)
        def _():
            carry[...] = jnp.zeros_like(carry)

        x = refs[0][...]
        for r in refs[1:n_in]:
            x = x + r[...]
        r_i = lax.broadcasted_iota(jnp.int32, (tb, tb), 0)
        c_i = lax.broadcasted_iota(jnp.int32, (tb, tb), 1)
        tri = ((c_i >= r_i) if reverse else (c_i <= r_i)).astype(BF16)
        acc, rest = None, x
        for _ in range(3):
            part = rest.astype(BF16)
            rest = rest - part.astype(F32)
            acc = _dot(tri, part) if acc is None else acc + _dot(tri, part)
        o_ref[...] = acc + carry[...]
        carry[...] += jnp.sum(x, axis=0, keepdims=True)

    return pl.pallas_call(
        body, name=name, grid=(nb,),
        in_specs=[pl.BlockSpec((tb, 128), idx)] * n_in,
        out_specs=pl.BlockSpec((tb, 128), idx),
        out_shape=SDS((lp, 128), F32),
        scratch_shapes=[pltpu.VMEM((1, 128), F32)],
        compiler_params=_params(("arbitrary",)),
    )(*xs)


class _Att:
    def __init__(self, mode):
        self.mode = mode
        self.wide = mode == "mla"
        self.qw = 256 if self.wide else 128
        self.scale = (96 if mode == "mla" else 64) ** -0.5
        self.kv_of = (lambda p: p // 2) if mode == "swa" else (lambda p: p)


def _split_pair(att, x, lo):
    if att.wide:
        return x[:, 0:128], x[:, 128:256]
    zero = jnp.zeros_like(x)
    return jnp.where(lo, x, zero), jnp.where(lo, zero, x)


def _valid(att, qpos, kpos):
    ok = kpos <= qpos
    if att.mode == "swa":
        ok = ok & ((kpos < N_META) | (qpos - kpos < WINDOW))
    return ok


def _q_chunks(att, qi):
    if att.mode == "swa":
        far = qi >= 2
        return jnp.where(far, 3, qi + 1), lambda t: jnp.where(far, jnp.where(t == 0, 0, qi - 2 + t), t)
    return qi + 1, lambda t: t


def _att_fwd(att, q, k, v, extra, name):
    lp = q.shape[0]
    t = TILE_ATT
    nq = lp // t
    qw = att.qw
    mode = att.mode

    def body(*refs):
        q_ref, k_ref, v_ref = refs[0:3]
        rest = list(refs[3:])
        if mode == "fox":
            cc_ref, cr_ref = rest[0:2]
            rest = rest[2:]
        if mode == "swa":
            sink_ref, slope_ref = rest[0:2]
            rest = rest[2:]
        o_ref, lse_ref = rest
        p, qi = pl.program_id(0), pl.program_id(1)
        lo = lax.broadcasted_iota(jnp.int32, (1, 128), 1) < 64
        q_pair = _split_pair(att, q_ref[...], lo) if att.wide else (q_ref[...], q_ref[...])
        qpos = qi * t + lax.broadcasted_iota(jnp.int32, (t, 1), 0)
        n_steps, chunk_of = _q_chunks(att, qi)

        def step(s_idx, carry):
            kj = chunk_of(s_idx)
            ks = pl.multiple_of(kj * t, t)
            kc, vc = k_ref[pl.ds(ks, t), :], v_ref[pl.ds(ks, t), :]
            k_pair = _split_pair(att, kc, lo)
            v_pair = _split_pair(_Att("fox"), vc, lo)
            kpos = kj * t + lax.broadcasted_iota(jnp.int32, (1, t), 1)
            ok = _valid(att, qpos, kpos)
            out = []
            for hh in range(2):
                m, l, acc = carry[3 * hh:3 * hh + 3]
                s = _dot_nt(q_pair[hh], k_pair[hh]) * att.scale
                if mode == "fox":
                    s = s + cc_ref[hh] - cr_ref[hh, :, pl.ds(ks, t)]
                if mode == "swa":
                    s = s - slope_ref[2 * p + hh] * (qpos - kpos).astype(F32)
                s = jnp.where(ok, s, NEG)
                m_new = jnp.maximum(m, jnp.max(s, axis=-1, keepdims=True))
                alpha = jnp.exp(m - m_new)
                pe = jnp.exp(s - m_new)
                l = alpha * l + jnp.sum(pe, axis=-1, keepdims=True)
                acc = alpha * acc + _dot(pe.astype(BF16), v_pair[hh])
                out += [m_new, l, acc]
            return tuple(out)

        init = []
        for hh in range(2):
            if mode == "swa":
                init += [jnp.full((t, 1), sink_ref[2 * p + hh], F32), jnp.ones((t, 1), F32)]
            else:
                init += [jnp.full((t, 1), NEG, F32), jnp.zeros((t, 1), F32)]
            init.append(jnp.zeros((t, 128), F32))
        ma, la, acca, mb, lb, accb = lax.fori_loop(0, n_steps, step, tuple(init))
        o_ref[...] = jnp.where(lo, acca / la, accb / lb).astype(BF16)
        lse_ref[0] = ma + jnp.log(la)
        lse_ref[1] = mb + jnp.log(lb)

    kvi = att.kv_of
    in_specs = [pl.BlockSpec((t, qw), lambda p, i: (i, p)), pl.BlockSpec((lp, qw), lambda p, i: (0, kvi(p))),
                pl.BlockSpec((lp, 128), lambda p, i: (0, kvi(p)))]
    if mode == "fox":
        in_specs += [pl.BlockSpec((2, t, 1), lambda p, i: (p, i, 0)), pl.BlockSpec((2, 1, lp), lambda p, i: (p, 0, 0))]
    if mode == "swa":
        in_specs += [pl.BlockSpec(memory_space=pltpu.SMEM)] * 2
    return pl.pallas_call(
        body, name=name, grid=(4, nq), in_specs=in_specs,
        out_specs=[pl.BlockSpec((t, 128), lambda p, i: (i, p)), pl.BlockSpec((2, t, 1), lambda p, i: (p, i, 0))],
        out_shape=[SDS((lp, 512), BF16), SDS((HEADS, lp, 1), F32)],
        compiler_params=_params(("parallel", "arbitrary")),
    )(q, k, v, *extra)


def _att_dq(att, q, k, v, o, do, lse, extra, name):
    lp = q.shape[0]
    t = TILE_ATT
    nq = lp // t
    qw = att.qw
    mode = att.mode

    def body(*refs):
        q_ref, k_ref, v_ref, o_ref, do_ref, lse_ref = refs[0:6]
        rest = list(refs[6:])
        if mode == "fox":
            cc_ref, cr_ref = rest[0:2]
            rest = rest[2:]
        if mode == "swa":
            sink_ref, slope_ref = rest[0:2]
            rest = rest[2:]
        dq_ref, delta_ref = rest[0:2]
        p, qi = pl.program_id(0), pl.program_id(1)
        lo = lax.broadcasted_iota(jnp.int32, (1, 128), 1) < 64
        q_pair = _split_pair(att, q_ref[...], lo) if att.wide else (q_ref[...], q_ref[...])
        do_v = do_ref[...]
        prod = do_v.astype(F32) * o_ref[...].astype(F32)
        delta = [jnp.sum(jnp.where(lo, prod, 0.0), axis=-1, keepdims=True),
                 jnp.sum(jnp.where(lo, 0.0, prod), axis=-1, keepdims=True)]
        lse_v = [lse_ref[0], lse_ref[1]]
        qpos = qi * t + lax.broadcasted_iota(jnp.int32, (t, 1), 0)
        n_steps, chunk_of = _q_chunks(att, qi)

        def step(s_idx, carry):
            kj = chunk_of(s_idx)
            ks = pl.multiple_of(kj * t, t)
            kc, vc = k_ref[pl.ds(ks, t), :], v_ref[pl.ds(ks, t), :]
            k_pair = _split_pair(att, kc, lo)
            v_pair = _split_pair(_Att("fox"), vc, lo)
            kpos = kj * t + lax.broadcasted_iota(jnp.int32, (1, t), 1)
            ok = _valid(att, qpos, kpos)
            out = []
            for hh in range(2):
                dq_acc, dc_acc = carry[2 * hh:2 * hh + 2]
                s = _dot_nt(q_pair[hh], k_pair[hh]) * att.scale
                if mode == "fox":
                    s = s + cc_ref[hh] - cr_ref[hh, :, pl.ds(ks, t)]
                if mode == "swa":
                    s = s - slope_ref[2 * p + hh] * (qpos - kpos).astype(F32)
                pr = jnp.exp(jnp.where(ok, s, NEG) - lse_v[hh])
                dp = _dot_nt(do_v, v_pair[hh])
                ds = pr * (dp - delta[hh])
                dq_acc = dq_acc + _dot(ds.astype(BF16), k_pair[hh])
                dc_acc = dc_acc + jnp.sum(ds, axis=-1, keepdims=True)
                out += [dq_acc, dc_acc]
            return tuple(out)

        init = (jnp.zeros((t, 128), F32), jnp.zeros((t, 1), F32)) * 2
        dqa, dca, dqb, dcb = lax.fori_loop(0, n_steps, step, init)
        if att.wide:
            dq_ref[...] = jnp.concatenate([dqa, dqb], axis=1) * att.scale
        else:
            dq_ref[...] = (dqa + dqb) * att.scale
        delta_ref[0] = delta[0]
        delta_ref[1] = delta[1]
        if mode == "fox":
            rest[2][0] = dca
            rest[2][1] = dcb
        if mode == "swa":
            ds_ref = rest[2]

            @pl.when(qi == 0)
            def _():
                ds_ref[...] = jnp.zeros_like(ds_ref)

            lane = lax.broadcasted_iota(jnp.int32, (8, 128), 1)
            tot = [-jnp.sum(jnp.exp(sink_ref[2 * p + hh] - lse_v[hh]) * delta[hh]) for hh in range(2)]
            ds_ref[0] += jnp.where(lane == 0, tot[0], jnp.where(lane == 1, tot[1], 0.0))

    kvi = att.kv_of
    col = pl.BlockSpec((2, t, 1), lambda p, i: (p, i, 0))
    in_specs = [pl.BlockSpec((t, qw), lambda p, i: (i, p)), pl.BlockSpec((lp, qw), lambda p, i: (0, kvi(p))),
                pl.BlockSpec((lp, 128), lambda p, i: (0, kvi(p))), pl.BlockSpec((t, 128), lambda p, i: (i, p)),
                pl.BlockSpec((t, 128), lambda p, i: (i, p)), col]
    out_specs = [pl.BlockSpec((t, qw), lambda p, i: (i, p)), col]
    out_shape = [SDS((lp, 4 * qw), F32), SDS((HEADS, lp, 1), F32)]
    if mode == "fox":
        in_specs += [col, pl.BlockSpec((2, 1, lp), lambda p, i: (p, 0, 0))]
        out_specs.append(col)
        out_shape.append(SDS((HEADS, lp, 1), F32))
    if mode == "swa":
        in_specs += [pl.BlockSpec(memory_space=pltpu.SMEM)] * 2
        out_specs.append(pl.BlockSpec((1, 8, 128), lambda p, i: (p, 0, 0)))
        out_shape.append(SDS((4, 8, 128), F32))
    return pl.pallas_call(
        body, name=name, grid=(4, nq), in_specs=in_specs, out_specs=out_specs, out_shape=out_shape,
        compiler_params=_params(("parallel", "arbitrary")),
    )(q, k, v, o, do, lse, *extra)


def _att_dkv(att, q, k, v, do, lse_row, delta_row, extra, name):
    lp = q.shape[0]
    t = TILE_ATT
    nq = lp // t
    qw = att.qw
    mode = att.mode

    def body(*refs):
        q_ref, k_ref, v_ref, do_ref, lse_ref, delta_ref = refs[0:6]
        rest = list(refs[6:])
        if mode == "fox":
            cc_ref, cr_ref = rest[0:2]
            rest = rest[2:]
        if mode == "swa":
            slope_ref = rest[0]
            rest = rest[1:]
        dk_ref, dv_ref = rest[0:2]
        p, kj = pl.program_id(0), pl.program_id(1)
        lo = lax.broadcasted_iota(jnp.int32, (1, 128), 1) < 64
        k_pair = _split_pair(att, k_ref[...], lo)
        v_pair = _split_pair(_Att("fox"), v_ref[...], lo)
        kpos = kj * t + lax.broadcasted_iota(jnp.int32, (t, 1), 0)
        if mode == "swa":
            first = kj
            last = jnp.where(kj == 0, nq, jnp.minimum(kj + 2, nq))
        else:
            first, last = kj, nq

        def step(qi, carry):
            qs = pl.multiple_of(qi * t, t)
            qc, doc = q_ref[pl.ds(qs, t), :], do_ref[pl.ds(qs, t), :]
            q_pair = _split_pair(att, qc, lo) if att.wide else (qc, qc)
            qpos = qi * t + lax.broadcasted_iota(jnp.int32, (1, t), 1)
            ok = _valid(att, qpos, kpos)
            out = []
            for hh in range(2):
                dk_acc, dv_acc, dc_acc = carry[3 * hh:3 * hh + 3]
                st = _dot_nt(k_pair[hh], q_pair[hh]) * att.scale
                if mode == "fox":
                    st = st + cr_ref[hh, :, pl.ds(qs, t)] - cc_ref[hh]
                if mode == "swa":
                    st = st - slope_ref[2 * p + hh] * (qpos - kpos).astype(F32)
                pt = jnp.exp(jnp.where(ok, st, NEG) - lse_ref[hh, :, pl.ds(qs, t)])
                dpt = _dot_nt(v_pair[hh], doc)
                dst = pt * (dpt - delta_ref[hh, :, pl.ds(qs, t)])
                dv_acc = dv_acc + _dot(pt.astype(BF16), doc)
                dk_acc = dk_acc + _dot(dst.astype(BF16), q_pair[hh])
                dc_acc = dc_acc - jnp.sum(dst, axis=-1, keepdims=True)
                out += [dk_acc, dv_acc, dc_acc]
            return tuple(out)

        init = (jnp.zeros((t, 128), F32), jnp.zeros((t, 128), F32), jnp.zeros((t, 1), F32)) * 2
        dka, dva, dca, dkb, dvb, dcb = lax.fori_loop(first, last, step, init)
        if att.wide:
            dk_ref[...] = jnp.concatenate([dka, dkb], axis=1) * att.scale
        else:
            dk_ref[...] = jnp.where(lo, dka, dkb) * att.scale
        dv_ref[...] = jnp.where(lo, dva, dvb)
        if mode == "fox":
            rest[2][0] = dca
            rest[2][1] = dcb

    kvi = att.kv_of
    rowv = pl.BlockSpec((2, 1, lp), lambda p, j: (p, 0, 0))
    col = pl.BlockSpec((2, t, 1), lambda p, j: (p, j, 0))
    in_specs = [pl.BlockSpec((lp, qw), lambda p, j: (0, p)), pl.BlockSpec((t, qw), lambda p, j: (j, kvi(p))),
                pl.BlockSpec((t, 128), lambda p, j: (j, kvi(p))), pl.BlockSpec((lp, 128), lambda p, j: (0, p)), rowv, rowv]
    out_specs = [pl.BlockSpec((t, qw), lambda p, j: (j, p)), pl.BlockSpec((t, 128), lambda p, j: (j, p))]
    out_shape = [SDS((lp, 4 * qw), F32), SDS((lp, 512), F32)]
    if mode == "fox":
        in_specs += [col, rowv]
        out_specs.append(col)
        out_shape.append(SDS((HEADS, lp, 1), F32))
    if mode == "swa":
        in_specs += [pl.BlockSpec(memory_space=pltpu.SMEM)]
    return pl.pallas_call(
        body, name=name, grid=(4, nq), in_specs=in_specs, out_specs=out_specs, out_shape=out_shape,
        compiler_params=_params(("parallel", "arbitrary")),
    )(q, k, v, do, lse_row, delta_row, *extra)


def _post_fwd(h, proj, outs, wb, wo, name):
    lp, d = h.shape
    tb = TILE_POST
    row = lambda w: pl.BlockSpec((tb, w), lambda i: (i, 0))

    def body(h_ref, g0, g1, g2, oa, ob, oc, wb_ref, wo_ref, o_ref):
        merged = jnp.zeros((tb, d), F32)
        for n, (g_ref, br) in enumerate(((g0, oa), (g1, ob), (g2, oc))):
            merged = merged + jax.nn.sigmoid(g_ref[...]) * _dot(br[...], wb_ref[n])
        o_ref[...] = h_ref[...] + _dot(merged.astype(BF16), wo_ref[...])

    gate = lambda n: pl.BlockSpec((tb, d), lambda i, n=n: (i, n))
    return pl.pallas_call(
        body, name=name, grid=(lp // tb,),
        in_specs=[row(d), gate(0), gate(1), gate(2), row(512), row(512), row(512),
                  pl.BlockSpec((3, 512, d), lambda i: (0, 0, 0)), pl.BlockSpec((d, d), lambda i: (0, 0))],
        out_specs=row(d), out_shape=SDS((lp, d), F32),
        compiler_params=_params(("parallel",)),
    )(h, proj, proj, proj, *outs, wb, wo)


def _post_bwd(dh, proj, outs, wb, wo, name):
    lp, d = dh.shape
    tb = TILE_POST
    row = lambda w: pl.BlockSpec((tb, w), lambda i: (i, 0))

    def body(dh_ref, g0, g1, g2, oa, ob, oc, wb_ref, wo_ref, dg_ref, doa, dob, doc, dwb_ref, dwo_ref):
        @pl.when(pl.program_id(0) == 0)
        def _():
            dwb_ref[...] = jnp.zeros_like(dwb_ref)
            dwo_ref[...] = jnp.zeros_like(dwo_ref)

        dhb = dh_ref[...].astype(BF16)
        dm = _dot_nt(dhb, wo_ref[...])
        merged = jnp.zeros((tb, d), F32)
        for n, (g_ref, br, do_ref) in enumerate(((g0, oa, doa), (g1, ob, dob), (g2, oc, doc))):
            gate = jax.nn.sigmoid(g_ref[...])
            o_n = br[...]
            y = _dot(o_n, wb_ref[n])
            merged = merged + gate * y
            dy = (dm * gate).astype(BF16)
            dg_ref[:, n * d:(n + 1) * d] = (dm * y * gate * (1.0 - gate)).astype(BF16)
            do_ref[...] = _dot_nt(dy, wb_ref[n]).astype(BF16)
            dwb_ref[n] += _dot_tn(o_n, dy)
        dwo_ref[...] += _dot_tn(merged.astype(BF16), dhb)

    gate = lambda n: pl.BlockSpec((tb, d), lambda i, n=n: (i, n))
    wb_spec = pl.BlockSpec((3, 512, d), lambda i: (0, 0, 0))
    wo_spec = pl.BlockSpec((d, d), lambda i: (0, 0))
    return pl.pallas_call(
        body, name=name, grid=(lp // tb,),
        in_specs=[row(d), gate(0), gate(1), gate(2), row(512), row(512), row(512), wb_spec, wo_spec],
        out_specs=[row(GATES_W), row(512), row(512), row(512), wb_spec, wo_spec],
        out_shape=[SDS((lp, GATES_W), BF16)] + [SDS((lp, 512), BF16)] * 3 + [SDS((3, 512, d), F32), SDS((d, d), F32)],
        compiler_params=_params(("arbitrary",)),
    )(dh, proj, proj, proj, *outs, wb, wo)


def _shift_down(x, halo, n, first):
    rows = lax.broadcasted_iota(jnp.int32, x.shape, 0)
    edge = jnp.concatenate([pltpu.roll(halo, n, 0), jnp.zeros((x.shape[0] - 8, x.shape[1]), F32)], axis=0)
    edge = jnp.where(first, 0.0, edge)
    return jnp.where(rows < n, edge, pltpu.roll(x, n, 0))


def _shift_up(x, halo, n, last):
    tb = x.shape[0]
    rows = lax.broadcasted_iota(jnp.int32, x.shape, 0)
    edge = jnp.concatenate([jnp.zeros((tb - 8, x.shape[1]), F32), pltpu.roll(halo, 8 - n, 0)], axis=0)
    edge = jnp.where(last, 0.0, edge)
    return jnp.where(rows >= tb - n, edge, pltpu.roll(x, tb - n, 0))


def _conv(u, halo, w_ref, b_ref, first):
    taps = (_shift_down(u, halo, 2, first), _shift_down(u, halo, 1, first), u)
    c = b_ref[...] + w_ref[0:1, :] * taps[0] + w_ref[1:2, :] * taps[1] + w_ref[2:3, :] * taps[2]
    return c, taps


def _ffn_specs(tb, f):
    hb = tb // 8
    cur = lambda c: pl.BlockSpec((tb, f), lambda i, c=c: (i, c))
    prev = lambda c: pl.BlockSpec((8, f), lambda i, c=c: (jnp.maximum(i * hb - 1, 0), c))
    vec = lambda r, c: pl.BlockSpec((r, f), lambda i, c=c: (0, c))
    return cur, prev, vec


def _ffn_act_fwd(u, cw, cb, name):
    lp = u.shape[0]
    f = D_FF
    tb = TILE_ROW
    cur, prev, vec = _ffn_specs(tb, f)

    def body(ug, uv, hg, hv, wg, wv, bg, bv, o_ref):
        first = pl.program_id(0) == 0
        cg, _ = _conv(ug[...], hg[...], wg, bg, first)
        cv, _ = _conv(uv[...], hv[...], wv, bv, first)
        o_ref[...] = (cg * jax.nn.sigmoid(cg) * cv).astype(BF16)

    return pl.pallas_call(
        body, name=name, grid=(lp // tb,),
        in_specs=[cur(0), cur(1), prev(0), prev(1), vec(8, 0), vec(8, 1), vec(1, 0), vec(1, 1)],
        out_specs=pl.BlockSpec((tb, f), lambda i: (i, 0)), out_shape=SDS((lp, f), BF16),
        compiler_params=_params(("parallel",)),
    )(u, u, u, u, cw, cw, cb, cb)


def _ffn_act_bwd_conv(u, dact, cw, cb, name):
    lp = u.shape[0]
    f = D_FF
    tb = TILE_ROW
    cur, prev, vec = _ffn_specs(tb, f)

    def body(ug, uv, hg, hv, wg, wv, bg, bv, da_ref, dcg_ref, dcv_ref, dwg, dwv, dbg, dbv):
        first = pl.program_id(0) == 0

        @pl.when(first)
        def _():
            for r in (dwg, dwv, dbg, dbv):
                r[...] = jnp.zeros_like(r)

        cg, tg = _conv(ug[...], hg[...], wg, bg, first)
        cv, tv = _conv(uv[...], hv[...], wv, bv, first)
        da = da_ref[...]
        sg = jax.nn.sigmoid(cg)
        dcg = da * cv * sg * (1.0 + cg * (1.0 - sg))
        dcv = da * cg * sg
        dcg_ref[...] = dcg
        dcv_ref[...] = dcv
        for dc, taps, dw, db in ((dcg, tg, dwg, dbg), (dcv, tv, dwv, dbv)):
            for n in range(3):
                dw[n:n + 1, :] += jnp.sum(dc * taps[n], axis=0, keepdims=True)
            db[0:1, :] += jnp.sum(dc, axis=0, keepdims=True)

    row = pl.BlockSpec((tb, f), lambda i: (i, 0))
    acc = pl.BlockSpec((8, f), lambda i: (0, 0))
    return pl.pallas_call(
        body, name=name, grid=(lp // tb,),
        in_specs=[cur(0), cur(1), prev(0), prev(1), vec(8, 0), vec(8, 1), vec(1, 0), vec(1, 1), row],
        out_specs=[row, row, acc, acc, acc, acc],
        out_shape=[SDS((lp, f), F32)] * 2 + [SDS((8, f), F32)] * 4,
        compiler_params=_params(("arbitrary",)),
    )(u, u, u, u, cw, cw, cb, cb, dact)


def _ffn_act_bwd_in(dcg, dcv, cw, name):
    lp = dcg.shape[0]
    f = D_FF
    tb = TILE_ROW
    nb = lp // tb
    hb = tb // 8
    cur = pl.BlockSpec((tb, f), lambda i: (i, 0))
    nxt = pl.BlockSpec((8, f), lambda i: (jnp.minimum((i + 1) * hb, nb * hb - 1), 0))
    vec = lambda c: pl.BlockSpec((8, f), lambda i, c=c: (0, c))

    def body(dg, dv, ng, nv, wg, wv, og, ov):
        last = pl.program_id(0) == nb - 1
        for dc_ref, n_ref, w_ref, o_ref in ((dg, ng, wg, og), (dv, nv, wv, ov)):
            dc, halo = dc_ref[...], n_ref[...]
            du = (w_ref[2:3, :] * dc + w_ref[1:2, :] * _shift_up(dc, halo, 1, last)
                  + w_ref[0:1, :] * _shift_up(dc, halo, 2, last))
            o_ref[...] = du.astype(BF16)

    return pl.pallas_call(
        body, name=name, grid=(nb,),
        in_specs=[cur, cur, nxt, nxt, vec(0), vec(1)],
        out_specs=[cur, cur],
        out_shape=[SDS((lp, f), BF16)] * 2,
        compiler_params=_params(("parallel",)),
    )(dcg, dcv, dcg, dcv, cw, cw)


def _loss_head(y, target, n_real, name):
    lp, d = y.shape
    tb = TILE_MM

    def body(y_ref, t_ref, dy_ref, loss_ref):
        i = pl.program_id(0)

        @pl.when(i == 0)
        def _():
            loss_ref[...] = jnp.zeros_like(loss_ref)

        rows = i * tb + lax.broadcasted_iota(jnp.int32, (tb, 1), 0)
        real = (rows >= N_META) & (rows < N_META + n_real)
        diff = jnp.where(real, y_ref[...] - t_ref[...], 0.0)
        dy_ref[...] = diff * (1.0 / d)
        loss_ref[...] += (0.5 / d) * jnp.sum(diff * diff).reshape(1, 1)

    row = pl.BlockSpec((tb, d), lambda i: (i, 0))
    return pl.pallas_call(
        body, name=name, grid=(lp // tb,), in_specs=[row, row],
        out_specs=[row, pl.BlockSpec((1, 1), lambda i: (0, 0))],
        out_shape=[SDS((lp, d), F32), SDS((1, 1), F32)],
        compiler_params=_params(("arbitrary",)),
    )(y, target)


def _pad_lanes(v, width, at=0):
    return jnp.zeros((1, width), F32).at[0, at:at + v.shape[0]].set(v)


def _layer_params(w, l):
    b = lambda a: a.astype(BF16)
    win = w["w_in"][l]
    fq, fk, fv, ff, cq, ckv, kr, sq, sk, sv, gates = jnp.split(
        win, [512, 1024, 1536, 1544, 1800, 1928, 1960, 2472, 2600, 2728], axis=1)
    misc = jnp.concatenate([kr, ff, jnp.zeros((D_MODEL, 88), win.dtype)], axis=1)
    w_in = b(jnp.concatenate([gates, fq, fk, fv, sq, sk, sv, cq, ckv, misc], axis=1))
    wq = jnp.pad(w["mla_w_q_up"][l].reshape(256, HEADS, 96), ((0, 0), (0, 0), (0, 32))).reshape(256, 1024)
    wkv = w["mla_w_kv_up"][l].reshape(128, HEADS, 128)
    wkk = jnp.pad(wkv[:, :, :64], ((0, 0), (0, 0), (0, 64))).reshape(128, 1024)
    wkvv = wkv[:, :, 64:].reshape(128, 512)
    tile = lambda g, n: jnp.tile(g.astype(F32), n)[None, :]
    prm = [tile(w["fox_q_g"][l], 8), tile(w["fox_k_g"][l], 8), tile(w["swa_q_g"][l], 8), tile(w["swa_k_g"][l], 2),
           _pad_lanes(w["fox_forget_b"][l], 128, FF_LANE), w["mla_q_a_g"][l][None, :], w["mla_kv_a_g"][l][None, :],
           tile(jnp.pad(w["mla_q_g"][l], (0, 32)), 8), tile(jnp.pad(w["mla_k_g"][l], (0, 32)), 8),
           wq.astype(F32), wkk.astype(F32), wkvv.astype(F32)]
    cw = jnp.pad(w["ffn_conv_w"][l].astype(F32), ((0, 5), (0, 0)))
    return dict(
        g1=w["norm1_g"][l][None, :], w_in=w_in, prm=prm, sinks=w["swa_sinks"][l].astype(F32),
        wb=b(w["w_branch"][l]), wo=b(w["w_o"][l]), g2=w["norm2_g"][l][None, :], w_up=b(w["ffn_w_up"][l]),
        cw=cw, cb=w["ffn_conv_b"][l][None, :].astype(F32), w_down=b(w["ffn_w_down"][l]))


def _cols(c):
    ct = c[:, FF_LANE:FF_LANE + HEADS].T
    return ct[:, :, None], ct[:, None, :]


def _rows(col):
    return jnp.swapaxes(col, 1, 2)


def _from_cols(col):
    return jnp.pad(col[:, :, 0].T, ((0, 0), (FF_LANE, 128 - FF_LANE - HEADS)))


def _layer_fwd(h, lw, consts, cos, sin, slopes, l):
    tag = f"l{l}_"
    xn, proj = _norm_matmul(h, lw["g1"], lw["w_in"], IN_W // 2, tag + "in_proj")
    fq, fk, fv, mq, mk, mv, sq, skd, svd, ls = _prep_fwd(proj, lw["prm"], consts, cos, sin, tag + "prep")
    c = _cumsum([ls], False, tag + "decay_cumsum")
    c_col, c_row = _cols(c)
    oa, lse_a = _att_fwd(_Att("fox"), fq, fk, fv, (c_col, c_row), tag + "fox_fwd")
    ob, lse_b = _att_fwd(_Att("mla"), mq, mk, mv, (), tag + "mla_fwd")
    oc, lse_c = _att_fwd(_Att("swa"), sq, skd, svd, (lw["sinks"], slopes), tag + "swa_fwd")
    h2 = _post_fwd(h, proj, (oa, ob, oc), lw["wb"], lw["wo"], tag + "merge")
    xn2, u = _norm_matmul(h2, lw["g2"], lw["w_up"], D_FF, tag + "ffn_up")
    act = _ffn_act_fwd(u, lw["cw"], lw["cb"], tag + "ffn_act")
    h3 = _matmul_residual(act, lw["w_down"], h2, tag + "ffn_down")
    saved = dict(h=h, xn=xn, proj=proj, q=(fq, mq, sq), k=(fk, mk, skd), v=(fv, mv, svd), c=(c_col, c_row),
                 o=(oa, ob, oc), lse=(lse_a, lse_b, lse_c), h2=h2, xn2=xn2, u=u, act=act)
    return h3, saved


def _layer_bwd(dh3, lw, sv, consts, folds, cos, sin, slopes, l):
    tag = f"l{l}_"
    lp = dh3.shape[0]
    f = D_FF
    dact = _matmul_nt(dh3, lw["w_down"], f, tag + "ffn_down_dx")
    dw_down = _matmul_tn(sv["act"], dh3, D_MODEL, tag + "ffn_down_dw")
    dcg, dcv, dwg, dwv, dbg, dbv = _ffn_act_bwd_conv(sv["u"], dact, lw["cw"], lw["cb"], tag + "ffn_act_dc")
    dug, duv = _ffn_act_bwd_in(dcg, dcv, lw["cw"], tag + "ffn_act_du")
    du = jnp.concatenate([dug, duv], axis=1)
    dw_up = _matmul_tn(sv["xn2"], du, f, tag + "ffn_up_dw")
    tb = TILE_MM
    half = lambda c: pl.BlockSpec((tb, f), lambda i, c=c: (i, c))
    whalf = lambda c: pl.BlockSpec((D_MODEL, f), lambda i, c=c: (0, c))
    dh2, dg2 = _norm_matmul_bwd(du, lw["w_up"], du, lw["w_up"], sv["h2"], lw["g2"], dh3,
                                (half(0), whalf(0), half(1), whalf(1)), tag + "ffn_up_dx")
    dgates, doa, dob, doc, dwb, dwo = _post_bwd(dh2, sv["proj"], sv["o"], lw["wb"], lw["wo"], tag + "merge_bwd")
    c_col, c_row = sv["c"]
    extras = ((c_col, c_row), (), (lw["sinks"], slopes))
    extras_kv = ((c_col, c_row), (), (slopes,))
    grads = []
    for n, (mode, do) in enumerate((("fox", doa), ("mla", dob), ("swa", doc))):
        att = _Att(mode)
        q, k, v = sv["q"][n], sv["k"][n], sv["v"][n]
        res = _att_dq(att, q, k, v, sv["o"][n], do, sv["lse"][n], extras[n], tag + mode + "_dq")
        dq, delta = res[0], res[1]
        res_kv = _att_dkv(att, q, k, v, do, _rows(sv["lse"][n]), _rows(delta), extras_kv[n], tag + mode + "_dkv")
        grads.append((dq, res_kv[0], res_kv[1], res[2:], res_kv[2:]))
    (dfq, dfk, dfv, (dcq,), (dck,)), (dmq, dmk, dmv, _, _), (dsq, dskp, dsvp, (dsink,), _) = grads
    dls = _cumsum([_from_cols(dcq), _from_cols(dck)], True, tag + "decay_cumsum_bwd")
    res = _prep_bwd(sv["proj"], lw["prm"], consts, cos, sin,
                    (dfq, dfk, dfv, dmq, dmk, dmv, dsq, dskp, dsvp, dls), folds, tag + "prep_bwd")
    dother, pg = res[0], res[1:]
    dw_g = _matmul_tn(sv["xn"], dgates, GATES_W, tag + "in_proj_dw_gates")
    dw_o = _matmul_tn(sv["xn"], dother, OTHER_W, tag + "in_proj_dw_other")
    full = lambda w: pl.BlockSpec((tb, w), lambda i: (i, 0))
    wfull = lambda w: pl.BlockSpec((D_MODEL, w), lambda i: (0, 0))
    dh, dg1 = _norm_matmul_bwd(dgates, lw["w_in"][:, :GATES_W], dother, lw["w_in"][:, GATES_W:], sv["h"], lw["g1"], dh2,
                               (full(GATES_W), wfull(GATES_W), full(OTHER_W), wfull(OTHER_W)), tag + "in_proj_dx")
    d_in = jnp.concatenate([
        dw_o[:, O_FQ:O_FV + 512], dw_o[:, O_MISC + FF_LANE:O_MISC + FF_LANE + 8], dw_o[:, O_CQ:O_CQ + 256],
        dw_o[:, O_CKV:O_CKV + 128], dw_o[:, O_MISC:O_MISC + 32], dw_o[:, O_SQ:O_SQ + 512], dw_o[:, O_SK:O_SK + 128],
        dw_o[:, O_SV:O_SV + 128], dw_g], axis=1)
    d_wq = pg[9].reshape(256, HEADS, 128)[:, :, :96].reshape(256, 768)
    d_wkv = jnp.concatenate([pg[10].reshape(128, HEADS, 128)[:, :, :64], pg[11].reshape(128, HEADS, 64)],
                            axis=2).reshape(128, 1024)
    g = dict(
        norm1_g=dg1[0], w_in=d_in, fox_forget_b=pg[4][0, FF_LANE:FF_LANE + 8], fox_q_g=pg[0][0, :64],
        fox_k_g=pg[1][0, :64], mla_q_a_g=pg[5][0], mla_w_q_up=d_wq, mla_kv_a_g=pg[6][0], mla_w_kv_up=d_wkv,
        mla_q_g=pg[7][0, :96], mla_k_g=pg[8][0, :96], swa_q_g=pg[2][0, :64], swa_k_g=pg[3][0, :64],
        swa_sinks=dsink[:, 0, 0:2].reshape(HEADS), w_branch=dwb, w_o=dwo, norm2_g=dg2[0], ffn_w_up=dw_up,
        ffn_conv_w=jnp.concatenate([dwg[0:3], dwv[0:3]], axis=1),
        ffn_conv_b=jnp.concatenate([dbg[0], dbv[0]]), ffn_w_down=dw_down)
    return dh, g


def _local_step(x, target, w):
    seq = x.shape[0]
    length = N_META + seq
    lp = -(-length // ROW_ALIGN) * ROW_ALIGN
    pad = lp - length
    h = jnp.concatenate([w["meta_tokens"].astype(F32), x, jnp.zeros((pad, D_MODEL), F32)], axis=0)
    tgt = jnp.pad(target, ((N_META, pad), (0, 0)))
    consts = _consts()
    folds = (_fold_matrix(512, 64), _fold_matrix(1024, 128))
    cos, sin = _rope_tables(lp)
    slopes = jnp.asarray(2.0 ** (-8.0 * np.arange(1, HEADS + 1, dtype=np.float32) / HEADS), F32)
    lws = [_layer_params(w, l) for l in range(DEPTH)]
    saved = []
    for l in range(DEPTH):
        h, sv = _layer_fwd(h, lws[l], consts, cos, sin, slopes, l)
        saved.append(sv)
    dh, loss = _loss_head(h, tgt, seq, "loss_head")
    grads = [None] * DEPTH
    for l in reversed(range(DEPTH)):
        dh, grads[l] = _layer_bwd(dh, lws[l], saved[l], consts, folds, cos, sin, slopes, l)
    stacked = {k: jnp.stack([grads[l][k] for l in range(DEPTH)]) for k in grads[0]}
    return loss, dh[N_META:length], dh[:N_META], stacked


def _place():
    return lax.axis_index("x"), lax.axis_index("y"), lax.axis_index("c")


def _flip(pos, k):
    x, y, c = pos
    return (1 - x if k & 4 else x, 1 - y if k & 2 else y, 1 - c if k & 1 else c)


def _index(pos):
    return 4 * pos[0] + 2 * pos[1] + pos[2]


def _exchange(send, gather, name):
    shape = send.shape[-2:]

    def body(x_ref, out_ref, send_sems, recv_sems, local_sem):
        me = _place()
        mine = _index(me)
        src = (lambda d: x_ref) if gather else (lambda d: x_ref.at[d])
        own = pltpu.make_async_copy(src(mine), out_ref.at[mine], local_sem)
        own.start()
        copies = []
        for k in range(1, N_DEV):
            peer = _flip(me, k)
            copies.append(pltpu.make_async_remote_copy(
                src_ref=src(_index(peer)), dst_ref=out_ref.at[mine], send_sem=send_sems.at[k - 1],
                recv_sem=recv_sems.at[k - 1], device_id=peer, device_id_type=pl.DeviceIdType.MESH))
        for cp in copies:
            cp.start()
        for cp in copies:
            cp.wait_send()
        for k in range(1, N_DEV):
            peer = _flip(me, k)
            pltpu.make_async_remote_copy(
                src_ref=src(mine), dst_ref=out_ref.at[_index(peer)], send_sem=send_sems.at[k - 1],
                recv_sem=recv_sems.at[k - 1], device_id=peer, device_id_type=pl.DeviceIdType.MESH).wait_recv()
        own.wait()

    return pl.pallas_call(
        body, name=name,
        in_specs=[pl.BlockSpec(memory_space=pl.ANY)], out_specs=pl.BlockSpec(memory_space=pl.ANY),
        out_shape=SDS((N_DEV,) + shape, send.dtype),
        scratch_shapes=[pltpu.SemaphoreType.DMA((N_DEV - 1,)), pltpu.SemaphoreType.DMA((N_DEV - 1,)),
                        pltpu.SemaphoreType.DMA],
    )(send)


def _sum_slots(parts, name):
    n, rows, w = parts.shape
    tb = 8

    def body(p_ref, o_ref):
        acc = p_ref[0].astype(F32)
        for s in range(1, n):
            acc = acc + p_ref[s].astype(F32)
        o_ref[...] = acc

    return pl.pallas_call(
        body, name=name, grid=(rows // tb,),
        in_specs=[pl.BlockSpec((n, tb, w), lambda i: (0, i, 0))], out_specs=pl.BlockSpec((tb, w), lambda i: (i, 0)),
        out_shape=SDS((rows, w), F32), compiler_params=_params(("parallel",)),
    )(parts)


def _adamw(wt, m, v, parts, tb, name):
    n, rows, w = parts.shape
    c1 = 1.0 / (1.0 - ADAM_B1 ** ADAM_STEP)
    c2 = 1.0 / (1.0 - ADAM_B2 ** ADAM_STEP)

    def body(w_ref, m_ref, v_ref, p_ref, g_out, d_out, m_out, v_out):
        g = p_ref[0].astype(F32)
        for s in range(1, n):
            g = g + p_ref[s].astype(F32)
        m_new = ADAM_B1 * m_ref[...] + (1.0 - ADAM_B1) * g
        v_new = ADAM_B2 * v_ref[...] + (1.0 - ADAM_B2) * (g * g)
        g_out[...] = g
        m_out[...] = m_new
        v_out[...] = v_new
        d_out[...] = -ADAM_LR * ((m_new * c1) / (jnp.sqrt(v_new * c2) + ADAM_EPS) + ADAM_WD * w_ref[...])

    row = pl.BlockSpec((tb, w), lambda i: (i, 0))
    return pl.pallas_call(
        body, name=name, grid=(rows // tb,),
        in_specs=[row, row, row, pl.BlockSpec((n, tb, w), lambda i: (0, i, 0))], out_specs=[row] * 4,
        out_shape=[SDS((rows, w), F32)] * 4, compiler_params=_params(("parallel",)),
    )(wt, m, v, parts)


_BIG = [("w_in", 2), ("mla_w_q_up", 2), ("mla_w_kv_up", 2), ("w_branch", 3), ("w_o", 1), ("ffn_w_up", 2), ("ffn_w_down", 1)]
_SMALL_SHARDED = [("meta_tokens", 1), ("ffn_conv_w", 2)]
_REPLICATED = ["norm1_g", "fox_forget_b", "fox_q_g", "fox_k_g", "mla_q_a_g", "mla_kv_a_g", "mla_q_g", "mla_k_g",
               "swa_q_g", "swa_k_g", "swa_sinks", "norm2_g", "ffn_conv_b"]
_ORDER = ["meta_tokens", "norm1_g", "w_in", "fox_forget_b", "fox_q_g", "fox_k_g", "mla_q_a_g", "mla_w_q_up",
          "mla_kv_a_g", "mla_w_kv_up", "mla_q_g", "mla_k_g", "swa_q_g", "swa_k_g", "swa_sinks", "w_branch", "w_o",
          "norm2_g", "ffn_w_up", "ffn_conv_w", "ffn_conv_b", "ffn_w_down"]


def _flat_rows(vecs, dtype, row_mult):
    flat = jnp.concatenate([a.reshape(-1).astype(dtype) for a in vecs])
    rows = -(-flat.shape[0] // (1024 * row_mult)) * row_mult
    return jnp.pad(flat, (0, rows * 1024 - flat.shape[0])).reshape(rows, 1024)


def _unflatten(flat, shapes):
    out, off = [], 0
    for s in shapes:
        n = math.prod(s)
        out.append(flat[off:off + n].reshape(s))
        off += n
    return out


def _to_full(blocks, axis):
    moved = jnp.moveaxis(blocks, 0, axis)
    s = moved.shape
    return moved.reshape(s[:axis] + (s[axis] * s[axis + 1],) + s[axis + 2:])


def _to_blocks(full, axis):
    s = full.shape
    split = full.reshape(s[:axis] + (N_DEV, s[axis] // N_DEV) + s[axis + 1:])
    return jnp.moveaxis(split, axis, 0)


def kernel(x, meta_tokens, norm1_g, w_in, fox_forget_b, fox_q_g, fox_k_g, mla_q_a_g, mla_w_q_up, mla_kv_a_g, mla_w_kv_up, mla_q_g, mla_k_g, swa_q_g, swa_k_g, swa_sinks, w_branch, w_o, norm2_g, ffn_w_up, ffn_conv_w, ffn_conv_b, ffn_w_down, loss_target, m_meta_tokens, m_norm1_g, m_w_in, m_fox_forget_b, m_fox_q_g, m_fox_k_g, m_mla_q_a_g, m_mla_w_q_up, m_mla_kv_a_g, m_mla_w_kv_up, m_mla_q_g, m_mla_k_g, m_swa_q_g, m_swa_k_g, m_swa_sinks, m_w_branch, m_w_o, m_norm2_g, m_ffn_w_up, m_ffn_conv_w, m_ffn_conv_b, m_ffn_w_down, v_meta_tokens, v_norm1_g, v_w_in, v_fox_forget_b, v_fox_q_g, v_fox_k_g, v_mla_q_a_g, v_mla_w_q_up, v_mla_kv_a_g, v_mla_w_kv_up, v_mla_q_g, v_mla_k_g, v_swa_q_g, v_swa_k_g, v_swa_sinks, v_w_branch, v_w_o, v_norm2_g, v_ffn_w_up, v_ffn_conv_w, v_ffn_conv_b, v_ffn_w_down):
    wl = dict(zip(_ORDER, (meta_tokens, norm1_g, w_in, fox_forget_b, fox_q_g, fox_k_g, mla_q_a_g, mla_w_q_up,
                           mla_kv_a_g, mla_w_kv_up, mla_q_g, mla_k_g, swa_q_g, swa_k_g, swa_sinks, w_branch, w_o,
                           norm2_g, ffn_w_up, ffn_conv_w, ffn_conv_b, ffn_w_down)))
    ml = dict(zip(_ORDER, (m_meta_tokens, m_norm1_g, m_w_in, m_fox_forget_b, m_fox_q_g, m_fox_k_g, m_mla_q_a_g,
                           m_mla_w_q_up, m_mla_kv_a_g, m_mla_w_kv_up, m_mla_q_g, m_mla_k_g, m_swa_q_g, m_swa_k_g,
                           m_swa_sinks, m_w_branch, m_w_o, m_norm2_g, m_ffn_w_up, m_ffn_conv_w, m_ffn_conv_b,
                           m_ffn_w_down)))
    vl = dict(zip(_ORDER, (v_meta_tokens, v_norm1_g, v_w_in, v_fox_forget_b, v_fox_q_g, v_fox_k_g, v_mla_q_a_g,
                           v_mla_w_q_up, v_mla_kv_a_g, v_mla_w_kv_up, v_mla_q_g, v_mla_k_g, v_swa_q_g, v_swa_k_g,
                           v_swa_sinks, v_w_branch, v_w_o, v_norm2_g, v_ffn_w_up, v_ffn_conv_w, v_ffn_conv_b,
                           v_ffn_w_down)))
    big = [n for n, _ in _BIG]
    small_sh = [n for n, _ in _SMALL_SHARDED]
    big_shapes = [wl[n].shape for n in big]
    small_shapes = [wl[n].shape for n in small_sh]

    got_big = _exchange(_flat_rows([wl[n] for n in big], BF16, 16), True, "gather_weights")
    got_small = _exchange(_flat_rows([wl[n] for n in small_sh], F32, 8), True, "gather_small_weights")
    full = {n: wl[n] for n in _REPLICATED}
    flat_big = got_big.reshape(N_DEV, -1)
    off = 0
    for (n, axis), s in zip(_BIG, big_shapes):
        size = math.prod(s)
        full[n] = _to_full(flat_big[:, off:off + size].reshape((N_DEV,) + s), axis)
        off += size
    flat_small = got_small.reshape(N_DEV, -1)
    off = 0
    for (n, axis), s in zip(_SMALL_SHARDED, small_shapes):
        size = math.prod(s)
        full[n] = _to_full(flat_small[:, off:off + size].reshape((N_DEV,) + s), axis)
        off += size

    loss, grad_x, grad_meta, grads = _local_step(x[0], loss_target[0], full)
    grads["meta_tokens"] = grad_meta

    send = jnp.concatenate(
        [_to_blocks(grads[n], axis).reshape(N_DEV, -1).astype(BF16) for n, axis in _BIG], axis=1)
    rows_big = got_big.shape[1]
    send = jnp.pad(send, ((0, 0), (0, rows_big * 1024 - send.shape[1]))).reshape(N_DEV, rows_big, 1024)
    parts_big = _exchange(send, False, "exchange_grads")
    tb_big = rows_big // 8
    g_b, d_b, m_b, v_b = _adamw(_flat_rows([wl[n] for n in big], F32, 16), _flat_rows([ml[n] for n in big], F32, 16),
                                _flat_rows([vl[n] for n in big], F32, 16), parts_big, tb_big, "adamw_large")

    small_full = _REPLICATED + small_sh
    mine_small = _flat_rows([grads[n] for n in small_full] + [loss], F32, 8)
    total_small = _sum_slots(_exchange(mine_small, True, "gather_small_grads"), "sum_small_grads").reshape(-1)
    pieces = _unflatten(total_small, [grads[n].shape for n in small_full] + [()])
    loss_total = pieces[-1]
    g_small = dict(zip(small_full, pieces[:-1]))
    idx = _index(_place())
    for (n, axis), s in zip(_SMALL_SHARDED, small_shapes):
        g_small[n] = lax.dynamic_slice_in_dim(g_small[n], idx * s[axis], s[axis], axis)
    g_s, d_s, m_s, v_s = _adamw(
        _flat_rows([wl[n] for n in small_full], F32, 8), _flat_rows([ml[n] for n in small_full], F32, 8),
        _flat_rows([vl[n] for n in small_full], F32, 8),
        _flat_rows([g_small[n] for n in small_full], F32, 8)[None], 8, "adamw_small")

    result = {}
    for kind, fb, fs in (("grad", g_b, g_s), ("delta", d_b, d_s), ("new_m", m_b, m_s), ("new_v", v_b, v_s)):
        result[kind] = dict(zip(big, _unflatten(fb.reshape(-1), big_shapes)))
        result[kind].update(zip(small_full, _unflatten(fs.reshape(-1), [wl[n].shape for n in small_full])))
    outs = [loss_total, grad_x[None]]
    for kind in ("grad", "delta", "new_m", "new_v"):
        outs += [result[kind][n] for n in _ORDER]
    return tuple(outs)
```

```python
import functools
import math

import numpy as np
import jax
import jax.numpy as jnp
from jax import lax
from jax.experimental import pallas as pl
from jax.experimental.pallas import tpu as pltpu

F32, BF16 = jnp.float32, jnp.bfloat16
SDS = jax.ShapeDtypeStruct

D_MODEL = 1024
N_META = 16
EPS = 1e-6
WINDOW = 128
ROPE_THETA = 10000.0
HEADS = 8
D_FF = 2816
DEPTH = 2
N_DEV = 8
ADAM_LR, ADAM_B1, ADAM_B2, ADAM_EPS, ADAM_WD, ADAM_STEP = 0.001, 0.9, 0.999, 1e-08, 0.01, 10

ROW_ALIGN = 384
TILE_MM = 384
TILE_ROW = 128
TILE_ATT = 384
TILE_POST = 128
VMEM_LIMIT = 56 * 1024 * 1024

GATES_W = 3072
OTHER_W = 2816
IN_W = GATES_W + OTHER_W
O_FQ, O_FK, O_FV, O_SQ, O_SK, O_SV, O_CQ, O_CKV, O_MISC = 0, 512, 1024, 1536, 2048, 2176, 2304, 2560, 2688
FF_LANE = 32

NEG = -1e30


def _dot(a, b):
    return jnp.dot(a, b, preferred_element_type=F32)


def _dot_nt(a, b):
    return lax.dot_general(a, b, (((1,), (1,)), ((), ())), preferred_element_type=F32)


def _dot_tn(a, b):
    return lax.dot_general(a, b, (((0,), (0,)), ((), ())), preferred_element_type=F32)


def _params(sem):
    return pltpu.CompilerParams(dimension_semantics=sem, vmem_limit_bytes=VMEM_LIMIT)


def _rms(x, g):
    return x * lax.rsqrt(jnp.mean(x * x, axis=-1, keepdims=True) + EPS) * g


def _split_dot(x, m, pieces=2):
    acc, rest = None, x
    for _ in range(pieces):
        part = rest.astype(BF16)
        rest = rest - part.astype(F32)
        acc = _dot(part, m) if acc is None else acc + _dot(part, m)
    return acc


@jax.custom_vjp
def _sel(x, m, mt):
    return _split_dot(x, m)


_sel.defvjp(lambda x, m, mt: (_split_dot(x, m), (m, mt)), lambda res, dy: (_split_dot(dy, res[1]), None, None))


@jax.custom_vjp
def _mm(x, w):
    return _dot(x.astype(BF16), w.astype(BF16))


def _mm_bwd(res, dy):
    x, w = res
    dyb = dy.astype(BF16)
    return _dot_nt(dyb, w.astype(BF16)), _dot_tn(x.astype(BF16), dyb)


_mm.defvjp(lambda x, w: (_mm(x, w), (x, w)), _mm_bwd)


def _rot_impl(x):
    w = x.shape[1]
    lane = lax.broadcasted_iota(jnp.int32, x.shape, 1) % 128
    lo = (lane >= 64) & (lane < 80)
    hi = (lane >= 80) & (lane < 96)
    return jnp.where(hi, pltpu.roll(x, 16, 1), 0.0) - jnp.where(lo, pltpu.roll(x, w - 16, 1), 0.0)


@jax.custom_vjp
def _rot(x):
    return _rot_impl(x)


_rot.defvjp(lambda x: (_rot_impl(x), None), lambda _, dy: (-_rot_impl(dy),))


def _gnorm(x, g, e, et, dim):
    inv = lax.rsqrt(_sel(x * x, e, et) * (1.0 / dim) + EPS)
    return x * _sel(inv, et, e) * g


def _indicator(width, period):
    m = np.zeros((width, 128), np.float32)
    m[np.arange(width), np.arange(width) // period] = 1.0
    return m


def _consts():
    e64 = _indicator(512, 64)
    e128 = _indicator(1024, 128)
    sk = np.zeros((128, 1024), np.float32)
    for h in range(HEADS):
        sk[np.arange(32), 128 * h + 64 + np.arange(32)] = 1.0
    dup = np.zeros((128, 256), np.float32)
    for g in range(2):
        for r in range(2):
            dup[64 * g + np.arange(64), 128 * g + 64 * r + np.arange(64)] = 1.0
    mats = [e64, e64.T, e128, e128.T, sk, sk.T, dup, dup.T]
    return [jnp.asarray(m, BF16) for m in mats]


def _fold_matrix(width, period):
    m = np.zeros((width, 128), np.float32)
    m[np.arange(width), np.arange(width) % period] = 1.0
    return jnp.asarray(m, BF16)


def _rope_tables(lp):
    half = 16
    freqs = ROPE_THETA ** (-np.arange(half, dtype=np.float32) / half)
    ang = np.arange(lp, dtype=np.float32)[:, None] * freqs[None, :]
    cos = np.ones((lp, 128), np.float32)
    sin = np.zeros((lp, 128), np.float32)
    cos[:, 64:80] = np.cos(ang)
    cos[:, 80:96] = np.cos(ang)
    sin[:, 64:80] = np.sin(ang)
    sin[:, 80:96] = np.sin(ang)
    return jnp.asarray(cos), jnp.asarray(sin)


def _norm_matmul(h, g, w, tn, name):
    lp, d = h.shape
    n = w.shape[1]
    tb = TILE_MM

    def body(h_ref, g_ref, w_ref, xn_ref, y_ref):
        @pl.when(pl.program_id(1) == 0)
        def _():
            xn_ref[...] = _rms(h_ref[...], g_ref[...]).astype(BF16)

        y_ref[...] = _dot(xn_ref[...], w_ref[...])

    return pl.pallas_call(
        body, name=name, grid=(lp // tb, n // tn),
        in_specs=[pl.BlockSpec((tb, d), lambda i, j: (i, 0)), pl.BlockSpec((1, d), lambda i, j: (0, 0)),
                  pl.BlockSpec((d, tn), lambda i, j: (0, j))],
        out_specs=[pl.BlockSpec((tb, d), lambda i, j: (i, 0)), pl.BlockSpec((tb, tn), lambda i, j: (i, j))],
        out_shape=[SDS((lp, d), BF16), SDS((lp, n), F32)],
        compiler_params=_params(("parallel", "arbitrary")),
    )(h, g, w)


def _matmul_residual(a, w, res, name):
    m, k = a.shape
    n = w.shape[1]
    tb = TILE_MM

    def body(a_ref, w_ref, r_ref, o_ref):
        o_ref[...] = r_ref[...] + _dot(a_ref[...], w_ref[...])

    return pl.pallas_call(
        body, name=name, grid=(m // tb,),
        in_specs=[pl.BlockSpec((tb, k), lambda i: (i, 0)), pl.BlockSpec((k, n), lambda i: (0, 0)),
                  pl.BlockSpec((tb, n), lambda i: (i, 0))],
        out_specs=pl.BlockSpec((tb, n), lambda i: (i, 0)),
        out_shape=SDS((m, n), F32),
        compiler_params=_params(("parallel",)),
    )(a, w, res)


def _matmul_nt(dy, w, tn, name):
    m, k = dy.shape
    n = w.shape[0]
    tb = TILE_MM

    def body(dy_ref, w_ref, o_ref):
        o_ref[...] = _dot_nt(dy_ref[...].astype(BF16), w_ref[...])

    return pl.pallas_call(
        body, name=name, grid=(m // tb, n // tn),
        in_specs=[pl.BlockSpec((tb, k), lambda i, j: (i, 0)), pl.BlockSpec((tn, k), lambda i, j: (j, 0))],
        out_specs=pl.BlockSpec((tb, tn), lambda i, j: (i, j)),
        out_shape=SDS((m, n), F32),
        compiler_params=_params(("parallel", "arbitrary")),
    )(dy, w)


def _matmul_tn(x, dy, tn, name):
    m, k = x.shape
    n = dy.shape[1]
    tb = TILE_MM

    def body(x_ref, dy_ref, o_ref):
        @pl.when(pl.program_id(1) == 0)
        def _():
            o_ref[...] = jnp.zeros_like(o_ref)

        o_ref[...] += _dot_tn(x_ref[...].astype(BF16), dy_ref[...].astype(BF16))

    return pl.pallas_call(
        body, name=name, grid=(n // tn, m // tb),
        in_specs=[pl.BlockSpec((tb, k), lambda j, i: (i, 0)), pl.BlockSpec((tb, tn), lambda j, i: (i, j))],
        out_specs=pl.BlockSpec((k, tn), lambda j, i: (0, j)),
        out_shape=SDS((k, n), F32),
        compiler_params=_params(("parallel", "arbitrary")),
    )(x, dy)


def _norm_matmul_bwd(dy1, w1, dy2, w2, x, g, dres, specs, name):
    m, d = x.shape
    tb = TILE_MM
    (dy1_spec, w1_spec, dy2_spec, w2_spec) = specs

    def body(dy1_ref, w1_ref, dy2_ref, w2_ref, x_ref, g_ref, r_ref, o_ref, dg_ref):
        @pl.when(pl.program_id(0) == 0)
        def _():
            dg_ref[...] = jnp.zeros_like(dg_ref)

        dxn = _dot_nt(dy1_ref[...], w1_ref[...]) + _dot_nt(dy2_ref[...], w2_ref[...])
        _, vjp = jax.vjp(_rms, x_ref[...], g_ref[...])
        dx, dg = vjp(dxn)
        o_ref[...] = r_ref[...] + dx
        dg_ref[...] += dg

    row = pl.BlockSpec((tb, d), lambda i: (i, 0))
    vec = pl.BlockSpec((1, d), lambda i: (0, 0))
    return pl.pallas_call(
        body, name=name, grid=(m // tb,),
        in_specs=[dy1_spec, w1_spec, dy2_spec, w2_spec, row, vec, row],
        out_specs=[row, vec],
        out_shape=[SDS((m, d), F32), SDS((1, d), F32)],
        compiler_params=_params(("arbitrary",)),
    )(dy1, w1, dy2, w2, x, g, dres)


def _prep_math(pieces, prm, consts, cos, sin):
    fq, fk, sq, sk, sv, cq, ckv, misc = pieces
    gfq, gfk, gsq, gsk, fb, gqa, gkva, gmq, gmk, wq, wkk, wkv = prm
    e64, e64t, e128, e128t, skm, skt, dup, dupt = consts
    cos8 = jnp.concatenate([cos] * HEADS, axis=1)
    sin8 = jnp.concatenate([sin] * HEADS, axis=1)
    fq_n = _gnorm(fq, gfq, e64, e64t, 64)
    fk_n = _gnorm(fk, gfk, e64, e64t, 64)
    ls = jax.nn.log_sigmoid(misc + fb)
    q = _gnorm(_mm(_rms(cq, gqa), wq), gmq, e128, e128t, 96)
    mq = q * cos8 + _rot(q) * sin8
    kva = _rms(ckv, gkva)
    k = _gnorm(_mm(kva, wkk) + _sel(misc, skm, skt), gmk, e128, e128t, 96)
    mk = k * cos8 + _rot(k) * sin8
    mv = _mm(kva, wkv)
    sq_n = _gnorm(sq, gsq, e64, e64t, 64)
    sk_n = _gnorm(sk, gsk, e64[0:128], e64t[:, 0:128], 64)
    skd = _sel(sk_n, dup, dupt)
    svd = _sel(sv, dup, dupt)
    return fq_n, fk_n, ls, mq, mk, mv, sq_n, skd, svd


_PIECES = [(O_FQ, 512), (O_FK, 512), (O_SQ, 512), (O_SK, 128), (O_SV, 128), (O_CQ, 256), (O_CKV, 128), (O_MISC, 128)]
_PRM_SHAPES = [(1, 512), (1, 512), (1, 512), (1, 128), (1, 128), (1, 256), (1, 128), (1, 1024), (1, 1024),
               (256, 1024), (128, 1024), (128, 512)]
_CONST_SHAPES = [(512, 128), (128, 512), (1024, 128), (128, 1024), (128, 1024), (1024, 128), (128, 256), (256, 128)]


def _piece_specs(tb):
    def spec(off, width):
        blk = (GATES_W + off) // width
        return pl.BlockSpec((tb, width), lambda i, blk=blk: (i, blk))
    return [spec(o, w) for o, w in _PIECES] + [spec(O_FV, 512)]


def _full_specs(shapes):
    return [pl.BlockSpec(s, lambda i: (0, 0)) for s in shapes]


def _prep_fwd(proj, prm, consts, cos, sin, name):
    lp = proj.shape[0]
    tb = TILE_ROW
    row = lambda w: pl.BlockSpec((tb, w), lambda i: (i, 0))

    def body(*refs):
        pieces = [r[...] for r in refs[0:8]]
        fv = refs[8][...]
        prm_v = [r[...] for r in refs[9:21]]
        consts_v = [r[...] for r in refs[21:29]]
        cos_v, sin_v = refs[29][...], refs[30][...]
        outs = refs[31:]
        fq_n, fk_n, ls, mq, mk, mv, sq_n, skd, svd = _prep_math(pieces, prm_v, consts_v, cos_v, sin_v)
        for ref, val in zip(outs, (fq_n, fk_n, fv, mq, mk, mv, sq_n, skd, svd)):
            ref[...] = val.astype(BF16)
        outs[9][...] = ls

    widths = [512, 512, 512, 1024, 1024, 512, 512, 256, 256]
    return pl.pallas_call(
        body, name=name, grid=(lp // tb,),
        in_specs=_piece_specs(tb) + _full_specs(_PRM_SHAPES) + _full_specs(_CONST_SHAPES) + [row(128), row(128)],
        out_specs=[row(w) for w in widths] + [row(128)],
        out_shape=[SDS((lp, w), BF16) for w in widths] + [SDS((lp, 128), F32)],
        compiler_params=_params(("parallel",)),
    )(*([proj] * 9), *prm, *consts, cos, sin)


def _prep_bwd(proj, prm, consts, cos, sin, cots, folds, name):
    lp = proj.shape[0]
    tb = TILE_ROW
    row = lambda w: pl.BlockSpec((tb, w), lambda i: (i, 0))
    fold64, fold128 = folds

    def body(*refs):
        pieces = [r[...] for r in refs[0:8]]
        prm_v = [r[...] for r in refs[9:21]]
        consts_v = [r[...] for r in refs[21:29]]
        cos_v, sin_v = refs[29][...], refs[30][...]
        dfq, dfk, dfv, dmq, dmk, dmv, dsq, dskp, dsvp, dls = [r[...] for r in refs[31:41]]
        f64, f128 = refs[41][...], refs[42][...]
        d_ref = refs[43]
        g_refs = refs[44:]

        @pl.when(pl.program_id(0) == 0)
        def _():
            for r in g_refs:
                r[...] = jnp.zeros_like(r)

        def pair_sum(p):
            return jnp.concatenate([p[:, 0:128] + p[:, 128:256], p[:, 256:384] + p[:, 384:512]], axis=1)

        f = lambda pc, pr: _prep_math(pc, pr, consts_v, cos_v, sin_v)
        _, vjp = jax.vjp(f, pieces, prm_v)
        dpc, dprm = vjp((dfq, dfk, dls, dmq, dmk, dmv, dsq, pair_sum(dskp), pair_sum(dsvp)))
        d_fq, d_fk, d_sq, d_sk, d_sv, d_cq, d_ckv, d_misc = dpc
        for off, val in ((O_FQ, d_fq), (O_FK, d_fk), (O_FV, dfv), (O_SQ, d_sq), (O_SK, d_sk), (O_SV, d_sv),
                         (O_CQ, d_cq), (O_CKV, d_ckv), (O_MISC, d_misc)):
            d_ref[:, off:off + val.shape[1]] = val.astype(BF16)
        folded = {0: f64, 1: f64, 2: f64, 3: f64[0:128], 7: f128, 8: f128}
        for idx, (ref, val) in enumerate(zip(g_refs, dprm)):
            if idx in folded:
                ref[...] += _split_dot(jnp.broadcast_to(val, (8, val.shape[1])), folded[idx], 3)
            elif val.shape[0] == 1:
                ref[...] += jnp.broadcast_to(val, ref.shape)
            else:
                ref[...] += val

    g_shapes = [(8, 128), (8, 128), (8, 128), (8, 128), (8, 128), (8, 256), (8, 128), (8, 128), (8, 128),
                (256, 1024), (128, 1024), (128, 512)]
    cot_widths = [512, 512, 512, 1024, 1024, 512, 512, 512, 512, 128]
    return pl.pallas_call(
        body, name=name, grid=(lp // tb,),
        in_specs=(_piece_specs(tb) + _full_specs(_PRM_SHAPES) + _full_specs(_CONST_SHAPES) + [row(128), row(128)]
                  + [row(w) for w in cot_widths] + _full_specs([(512, 128), (1024, 128)])),
        out_specs=[row(OTHER_W)] + _full_specs(g_shapes),
        out_shape=[SDS((lp, OTHER_W), BF16)] + [SDS(s, F32) for s in g_shapes],
        compiler_params=_params(("arbitrary",)),
    )(*([proj] * 9), *prm, *consts, cos, sin, *cots, fold64, fold128)


def _cumsum(xs, reverse, name):
    lp = xs[0].shape[0]
    tb = TILE_MM
    nb = lp // tb
    n_in = len(xs)
    idx = (lambda i: (nb - 1 - i, 0)) if reverse else (lambda i: (i, 0))

    def body(*refs):
        o_ref, carry = refs[n_in], refs[n_in + 1]

        @pl.when(pl.program_id(0) == 0)
        def _():
            carry[...] = jnp.zeros_like(carry)

        x = refs[0][...]
        for r in refs[1:n_in]:
            x = x + r[...]
        r_i = lax.broadcasted_iota(jnp.int32, (tb, tb), 0)
        c_i = lax.broadcasted_iota(jnp.int32, (tb, tb), 1)
        tri = ((c_i >= r_i) if reverse else (c_i <= r_i)).astype(BF16)
        acc, rest = None, x
        for _ in range(3):
            part = rest.astype(BF16)
            rest = rest - part.astype(F32)
            acc = _dot(tri, part) if acc is None else acc + _dot(tri, part)
        o_ref[...] = acc + carry[...]
        carry[...] += jnp.sum(x, axis=0, keepdims=True)

    return pl.pallas_call(
        body, name=name, grid=(nb,),
        in_specs=[pl.BlockSpec((tb, 128), idx)] * n_in,
        out_specs=pl.BlockSpec((tb, 128), idx),
        out_shape=SDS((lp, 128), F32),
        scratch_shapes=[pltpu.VMEM((1, 128), F32)],
        compiler_params=_params(("arbitrary",)),
    )(*xs)


class _Att:
    def __init__(self, mode):
        self.mode = mode
        self.wide = mode == "mla"
        self.qw = 256 if self.wide else 128
        self.scale = (96 if mode == "mla" else 64) ** -0.5
        self.kv_of = (lambda p: p // 2) if mode == "swa" else (lambda p: p)

    def resident(self, x, lo, scaled):
        if self.wide:
            return x[:, 0:128], x[:, 128:256]
        if scaled:
            x = x * jnp.asarray(self.scale, x.dtype)
        zero = jnp.zeros_like(x)
        return jnp.where(lo, x, zero), jnp.where(lo, zero, x)

    def moving(self, x):
        return (x[:, 0:128], x[:, 128:256]) if self.wide else (x, x)

    def logits(self, a, b, qpos, kpos, key_decay, slope, masked):
        s = _dot_nt(a, b)
        if self.wide:
            s = s * self.scale
        if self.mode == "fox":
            s = s - key_decay
        if self.mode == "swa":
            s = s - slope * (qpos - kpos).astype(F32)
        if masked:
            ok = kpos <= qpos
            if self.mode == "swa":
                ok = ok & ((kpos < N_META) | (qpos - kpos < WINDOW))
            s = jnp.where(ok, s, NEG)
        return s


def _halves(x, lo):
    zero = jnp.zeros_like(x)
    return jnp.where(lo, x, zero), jnp.where(lo, zero, x)


def _q_chunks(att, qi):
    far = qi >= 2
    return jnp.where(far, 3, qi + 1), lambda t: jnp.where(far, jnp.where(t == 0, 0, qi - 2 + t), t)


def _att_fwd(att, q, k, v, extra, name):
    lp = q.shape[0]
    t = TILE_ATT
    nq = lp // t
    qw = att.qw
    mode = att.mode

    def body(*refs):
        q_ref, k_ref, v_ref = refs[0:3]
        o_ref, lse_ref = refs[-2:]
        p, qi = pl.program_id(0), pl.program_id(1)
        lo = lax.broadcasted_iota(jnp.int32, (1, 128), 1) < 64
        q_pair = att.resident(q_ref[...], lo, True)
        qpos = qi * t + lax.broadcasted_iota(jnp.int32, (t, 1), 0)

        def step(kj, carry, masked):
            ks = pl.multiple_of(kj * t, t)
            k_pair = att.moving(k_ref[pl.ds(ks, t), :])
            vc = v_ref[pl.ds(ks, t), :]
            kpos = kj * t + lax.broadcasted_iota(jnp.int32, (1, t), 1)
            out = []
            for hh in range(2):
                m, l, acc = carry[3 * hh:3 * hh + 3]
                decay = refs[3][hh, :, pl.ds(ks, t)] if mode == "fox" else None
                slope = refs[4][2 * p + hh] if mode == "swa" else None
                s = att.logits(q_pair[hh], k_pair[hh], qpos, kpos, decay, slope, masked)
                m_new = jnp.maximum(m, jnp.max(s, axis=-1, keepdims=True))
                alpha = jnp.exp(m - m_new)
                pe = jnp.exp(s - m_new)
                l = alpha * l + jnp.sum(pe, axis=-1, keepdims=True)
                acc = alpha * acc + _dot(pe.astype(BF16), vc)
                out += [m_new, l, acc]
            return tuple(out)

        init = []
        for hh in range(2):
            if mode == "swa":
                init += [jnp.full((t, 1), refs[3][2 * p + hh], F32), jnp.ones((t, 1), F32)]
            else:
                init += [jnp.full((t, 1), NEG, F32), jnp.zeros((t, 1), F32)]
            init.append(jnp.zeros((t, 128), F32))
        if mode == "swa":
            n_steps, chunk_of = _q_chunks(att, qi)
            carry = lax.fori_loop(0, n_steps, lambda i, c: step(chunk_of(i), c, True), tuple(init))
        else:
            carry = lax.fori_loop(0, qi, lambda kj, c: step(kj, c, False), tuple(init))
            carry = step(qi, carry, True)
        ma, la, acca, mb, lb, accb = carry
        o_ref[...] = jnp.where(lo, acca / la, accb / lb).astype(BF16)
        lse_ref[0] = ma + jnp.log(la)
        lse_ref[1] = mb + jnp.log(lb)

    kvi = att.kv_of
    in_specs = [pl.BlockSpec((t, qw), lambda p, i: (i, p)), pl.BlockSpec((lp, qw), lambda p, i: (0, kvi(p))),
                pl.BlockSpec((lp, 128), lambda p, i: (0, kvi(p)))]
    if mode == "fox":
        in_specs += [pl.BlockSpec((2, 1, lp), lambda p, i: (p, 0, 0))]
    if mode == "swa":
        in_specs += [pl.BlockSpec(memory_space=pltpu.SMEM)] * 2
    return pl.pallas_call(
        body, name=name, grid=(4, nq), in_specs=in_specs,
        out_specs=[pl.BlockSpec((t, 128), lambda p, i: (i, p)), pl.BlockSpec((2, t, 1), lambda p, i: (p, i, 0))],
        out_shape=[SDS((lp, 512), BF16), SDS((HEADS, lp, 1), F32)],
        compiler_params=_params(("parallel", "arbitrary")),
    )(q, k, v, *extra)


def _att_dq(att, q, k, v, o, do, lse, extra, name):
    lp = q.shape[0]
    t = TILE_ATT
    nq = lp // t
    qw = att.qw
    mode = att.mode

    def body(*refs):
        q_ref, k_ref, v_ref, o_ref, do_ref, lse_ref = refs[0:6]
        n_out = 2 if mode == "mla" else 3
        outs = refs[len(refs) - n_out:]
        dq_ref, delta_ref = outs[0:2]
        p, qi = pl.program_id(0), pl.program_id(1)
        lo = lax.broadcasted_iota(jnp.int32, (1, 128), 1) < 64
        q_pair = att.resident(q_ref[...], lo, True)
        do_pair = _halves(do_ref[...], lo)
        prod = do_ref[...].astype(F32) * o_ref[...].astype(F32)
        delta = [jnp.sum(jnp.where(lo, prod, 0.0), axis=-1, keepdims=True),
                 jnp.sum(jnp.where(lo, 0.0, prod), axis=-1, keepdims=True)]
        lse_v = [lse_ref[0], lse_ref[1]]
        qpos = qi * t + lax.broadcasted_iota(jnp.int32, (t, 1), 0)

        def step(kj, carry, masked):
            ks = pl.multiple_of(kj * t, t)
            k_pair = att.moving(k_ref[pl.ds(ks, t), :])
            vc = v_ref[pl.ds(ks, t), :]
            kpos = kj * t + lax.broadcasted_iota(jnp.int32, (1, t), 1)
            out = []
            for hh in range(2):
                decay = refs[6][hh, :, pl.ds(ks, t)] if mode == "fox" else None
                slope = refs[7][2 * p + hh] if mode == "swa" else None
                s = att.logits(q_pair[hh], k_pair[hh], qpos, kpos, decay, slope, masked)
                pr = jnp.exp(s - lse_v[hh])
                ds = pr * (_dot_nt(do_pair[hh], vc) - delta[hh])
                out.append(carry[2 * hh] + _dot(ds.astype(BF16), k_pair[hh]))
                out.append(carry[2 * hh + 1] + jnp.sum(ds, axis=-1, keepdims=True) if mode == "fox" else carry[2 * hh + 1])
            return tuple(out)

        init = (jnp.zeros((t, 128), F32), jnp.zeros((t, 1), F32)) * 2
        if mode == "swa":
            n_steps, chunk_of = _q_chunks(att, qi)
            carry = lax.fori_loop(0, n_steps, lambda i, c: step(chunk_of(i), c, True), init)
        else:
            carry = lax.fori_loop(0, qi, lambda kj, c: step(kj, c, False), init)
            carry = step(qi, carry, True)
        dqa, dca, dqb, dcb = carry
        if att.wide:
            dq_ref[...] = jnp.concatenate([dqa, dqb], axis=1) * att.scale
        else:
            dq_ref[...] = jnp.where(lo, dqa, dqb) * att.scale
        delta_ref[0] = delta[0]
        delta_ref[1] = delta[1]
        if mode == "fox":
            outs[2][0] = dca
            outs[2][1] = dcb
        if mode == "swa":
            ds_ref = outs[2]

            @pl.when(qi == 0)
            def _():
                ds_ref[...] = jnp.zeros_like(ds_ref)

            lane = lax.broadcasted_iota(jnp.int32, (8, 128), 1)
            tot = [-jnp.sum(jnp.exp(refs[6][2 * p + hh] - lse_v[hh]) * delta[hh]) for hh in range(2)]
            ds_ref[0] += jnp.where(lane == 0, tot[0], jnp.where(lane == 1, tot[1], 0.0))

    kvi = att.kv_of
    col = pl.BlockSpec((2, t, 1), lambda p, i: (p, i, 0))
    in_specs = [pl.BlockSpec((t, qw), lambda p, i: (i, p)), pl.BlockSpec((lp, qw), lambda p, i: (0, kvi(p))),
                pl.BlockSpec((lp, 128), lambda p, i: (0, kvi(p))), pl.BlockSpec((t, 128), lambda p, i: (i, p)),
                pl.BlockSpec((t, 128), lambda p, i: (i, p)), col]
    out_specs = [pl.BlockSpec((t, qw), lambda p, i: (i, p)), col]
    out_shape = [SDS((lp, 4 * qw), F32), SDS((HEADS, lp, 1), F32)]
    if mode == "fox":
        in_specs += [pl.BlockSpec((2, 1, lp), lambda p, i: (p, 0, 0))]
        out_specs.append(col)
        out_shape.append(SDS((HEADS, lp, 1), F32))
    if mode == "swa":
        in_specs += [pl.BlockSpec(memory_space=pltpu.SMEM)] * 2
        out_specs.append(pl.BlockSpec((1, 8, 128), lambda p, i: (p, 0, 0)))
        out_shape.append(SDS((4, 8, 128), F32))
    return pl.pallas_call(
        body, name=name, grid=(4, nq), in_specs=in_specs, out_specs=out_specs, out_shape=out_shape,
        compiler_params=_params(("parallel", "arbitrary")),
    )(q, k, v, o, do, lse, *extra)


def _att_dkv(att, q, k, v, do, lse_row, delta_row, extra, name):
    lp = q.shape[0]
    t = TILE_ATT
    nq = lp // t
    qw = att.qw
    mode = att.mode

    def body(*refs):
        q_ref, k_ref, v_ref, do_ref, lse_ref, delta_ref = refs[0:6]
        n_out = 3 if mode == "fox" else 2
        outs = refs[len(refs) - n_out:]
        dk_ref, dv_ref = outs[0:2]
        p, kj = pl.program_id(0), pl.program_id(1)
        lo = lax.broadcasted_iota(jnp.int32, (1, 128), 1) < 64
        k_pair = att.resident(k_ref[...], lo, True)
        v_pair = _halves(v_ref[...], lo)
        kpos = kj * t + lax.broadcasted_iota(jnp.int32, (t, 1), 0)

        def step(qi, carry, masked):
            qs = pl.multiple_of(qi * t, t)
            q_pair = att.moving(q_ref[pl.ds(qs, t), :])
            doc = do_ref[pl.ds(qs, t), :]
            qpos = qi * t + lax.broadcasted_iota(jnp.int32, (1, t), 1)
            out = []
            for hh in range(2):
                dk_acc, dv_acc, dc_acc = carry[3 * hh:3 * hh + 3]
                decay = refs[6][hh] if mode == "fox" else None
                slope = refs[6][2 * p + hh] if mode == "swa" else None
                st = att.logits(k_pair[hh], q_pair[hh], qpos, kpos, decay, slope, masked)
                pt = jnp.exp(st - lse_ref[hh, :, pl.ds(qs, t)])
                dst = pt * (_dot_nt(v_pair[hh], doc) - delta_ref[hh, :, pl.ds(qs, t)])
                dv_acc = dv_acc + _dot(pt.astype(BF16), doc)
                dk_acc = dk_acc + _dot(dst.astype(BF16), q_pair[hh])
                if mode == "fox":
                    dc_acc = dc_acc - jnp.sum(dst, axis=-1, keepdims=True)
                out += [dk_acc, dv_acc, dc_acc]
            return tuple(out)

        init = (jnp.zeros((t, 128), F32), jnp.zeros((t, 128), F32), jnp.zeros((t, 1), F32)) * 2
        if mode == "swa":
            last = jnp.where(kj == 0, nq, jnp.minimum(kj + 2, nq))
            carry = lax.fori_loop(kj, last, lambda qi, c: step(qi, c, True), init)
        else:
            carry = step(kj, init, True)
            carry = lax.fori_loop(kj + 1, nq, lambda qi, c: step(qi, c, False), carry)
        dka, dva, dca, dkb, dvb, dcb = carry
        if att.wide:
            dk_ref[...] = jnp.concatenate([dka, dkb], axis=1) * att.scale
        else:
            dk_ref[...] = jnp.where(lo, dka, dkb) * att.scale
        dv_ref[...] = jnp.where(lo, dva, dvb)
        if mode == "fox":
            outs[2][0] = dca
            outs[2][1] = dcb

    kvi = att.kv_of
    rowv = pl.BlockSpec((2, 1, lp), lambda p, j: (p, 0, 0))
    col = pl.BlockSpec((2, t, 1), lambda p, j: (p, j, 0))
    in_specs = [pl.BlockSpec((lp, qw), lambda p, j: (0, p)), pl.BlockSpec((t, qw), lambda p, j: (j, kvi(p))),
                pl.BlockSpec((t, 128), lambda p, j: (j, kvi(p))), pl.BlockSpec((lp, 128), lambda p, j: (0, p)), rowv, rowv]
    out_specs = [pl.BlockSpec((t, qw), lambda p, j: (j, p)), pl.BlockSpec((t, 128), lambda p, j: (j, p))]
    out_shape = [SDS((lp, 4 * qw), F32), SDS((lp, 512), F32)]
    if mode == "fox":
        in_specs += [col]
        out_specs.append(col)
        out_shape.append(SDS((HEADS, lp, 1), F32))
    if mode == "swa":
        in_specs += [pl.BlockSpec(memory_space=pltpu.SMEM)]
    return pl.pallas_call(
        body, name=name, grid=(4, nq), in_specs=in_specs, out_specs=out_specs, out_shape=out_shape,
        compiler_params=_params(("parallel", "arbitrary")),
    )(q, k, v, do, lse_row, delta_row, *extra)


def _post_fwd(h, proj, outs, wb, wo, name):
    lp, d = h.shape
    tb = TILE_POST
    row = lambda w: pl.BlockSpec((tb, w), lambda i: (i, 0))

    def body(h_ref, g0, g1, g2, oa, ob, oc, wb_ref, wo_ref, o_ref):
        merged = jnp.zeros((tb, d), F32)
        for n, (g_ref, br) in enumerate(((g0, oa), (g1, ob), (g2, oc))):
            merged = merged + jax.nn.sigmoid(g_ref[...]) * _dot(br[...], wb_ref[n])
        o_ref[...] = h_ref[...] + _dot(merged.astype(BF16), wo_ref[...])

    gate = lambda n: pl.BlockSpec((tb, d), lambda i, n=n: (i, n))
    return pl.pallas_call(
        body, name=name, grid=(lp // tb,),
        in_specs=[row(d), gate(0), gate(1), gate(2), row(512), row(512), row(512),
                  pl.BlockSpec((3, 512, d), lambda i: (0, 0, 0)), pl.BlockSpec((d, d), lambda i: (0, 0))],
        out_specs=row(d), out_shape=SDS((lp, d), F32),
        compiler_params=_params(("parallel",)),
    )(h, proj, proj, proj, *outs, wb, wo)


def _post_bwd(dh, proj, outs, wb, wo, name):
    lp, d = dh.shape
    tb = TILE_POST
    row = lambda w: pl.BlockSpec((tb, w), lambda i: (i, 0))

    def body(dh_ref, g0, g1, g2, oa, ob, oc, wb_ref, wo_ref, dg_ref, doa, dob, doc, dwb_ref, dwo_ref):
        @pl.when(pl.program_id(0) == 0)
        def _():
            dwb_ref[...] = jnp.zeros_like(dwb_ref)
            dwo_ref[...] = jnp.zeros_like(dwo_ref)

        dhb = dh_ref[...].astype(BF16)
        dm = _dot_nt(dhb, wo_ref[...])
        merged = jnp.zeros((tb, d), F32)
        for n, (g_ref, br, do_ref) in enumerate(((g0, oa, doa), (g1, ob, dob), (g2, oc, doc))):
            gate = jax.nn.sigmoid(g_ref[...])
            o_n = br[...]
            y = _dot(o_n, wb_ref[n])
            merged = merged + gate * y
            dy = (dm * gate).astype(BF16)
            dg_ref[:, n * d:(n + 1) * d] = (dm * y * gate * (1.0 - gate)).astype(BF16)
            do_ref[...] = _dot_nt(dy, wb_ref[n]).astype(BF16)
            dwb_ref[n] += _dot_tn(o_n, dy)
        dwo_ref[...] += _dot_tn(merged.astype(BF16), dhb)

    gate = lambda n: pl.BlockSpec((tb, d), lambda i, n=n: (i, n))
    wb_spec = pl.BlockSpec((3, 512, d), lambda i: (0, 0, 0))
    wo_spec = pl.BlockSpec((d, d), lambda i: (0, 0))
    return pl.pallas_call(
        body, name=name, grid=(lp // tb,),
        in_specs=[row(d), gate(0), gate(1), gate(2), row(512), row(512), row(512), wb_spec, wo_spec],
        out_specs=[row(GATES_W), row(512), row(512), row(512), wb_spec, wo_spec],
        out_shape=[SDS((lp, GATES_W), BF16)] + [SDS((lp, 512), BF16)] * 3 + [SDS((3, 512, d), F32), SDS((d, d), F32)],
        compiler_params=_params(("arbitrary",)),
    )(dh, proj, proj, proj, *outs, wb, wo)


def _shift_down(x, halo, n, first):
    rows = lax.broadcasted_iota(jnp.int32, x.shape, 0)
    edge = jnp.concatenate([pltpu.roll(halo, n, 0), jnp.zeros((x.shape[0] - 8, x.shape[1]), F32)], axis=0)
    edge = jnp.where(first, 0.0, edge)
    return jnp.where(rows < n, edge, pltpu.roll(x, n, 0))


def _shift_up(x, halo, n, last):
    tb = x.shape[0]
    rows = lax.broadcasted_iota(jnp.int32, x.shape, 0)
    edge = jnp.concatenate([jnp.zeros((tb - 8, x.shape[1]), F32), pltpu.roll(halo, 8 - n, 0)], axis=0)
    edge = jnp.where(last, 0.0, edge)
    return jnp.where(rows >= tb - n, edge, pltpu.roll(x, tb - n, 0))


def _conv(u, halo, w_ref, b_ref, first):
    taps = (_shift_down(u, halo, 2, first), _shift_down(u, halo, 1, first), u)
    c = b_ref[...] + w_ref[0:1, :] * taps[0] + w_ref[1:2, :] * taps[1] + w_ref[2:3, :] * taps[2]
    return c, taps


def _ffn_specs(tb, f):
    hb = tb // 8
    cur = lambda c: pl.BlockSpec((tb, f), lambda i, c=c: (i, c))
    prev = lambda c: pl.BlockSpec((8, f), lambda i, c=c: (jnp.maximum(i * hb - 1, 0), c))
    vec = lambda r, c: pl.BlockSpec((r, f), lambda i, c=c: (0, c))
    return cur, prev, vec


def _ffn_act_fwd(u, cw, cb, name):
    lp = u.shape[0]
    f = D_FF
    tb = TILE_ROW
    cur, prev, vec = _ffn_specs(tb, f)

    def body(ug, uv, hg, hv, wg, wv, bg, bv, o_ref):
        first = pl.program_id(0) == 0
        cg, _ = _conv(ug[...], hg[...], wg, bg, first)
        cv, _ = _conv(uv[...], hv[...], wv, bv, first)
        o_ref[...] = (cg * jax.nn.sigmoid(cg) * cv).astype(BF16)

    return pl.pallas_call(
        body, name=name, grid=(lp // tb,),
        in_specs=[cur(0), cur(1), prev(0), prev(1), vec(8, 0), vec(8, 1), vec(1, 0), vec(1, 1)],
        out_specs=pl.BlockSpec((tb, f), lambda i: (i, 0)), out_shape=SDS((lp, f), BF16),
        compiler_params=_params(("parallel",)),
    )(u, u, u, u, cw, cw, cb, cb)


def _ffn_act_bwd_conv(u, dact, cw, cb, name):
    lp = u.shape[0]
    f = D_FF
    tb = TILE_ROW
    cur, prev, vec = _ffn_specs(tb, f)

    def body(ug, uv, hg, hv, wg, wv, bg, bv, da_ref, dcg_ref, dcv_ref, dwg, dwv, dbg, dbv):
        first = pl.program_id(0) == 0

        @pl.when(first)
        def _():
            for r in (dwg, dwv, dbg, dbv):
                r[...] = jnp.zeros_like(r)

        cg, tg = _conv(ug[...], hg[...], wg, bg, first)
        cv, tv = _conv(uv[...], hv[...], wv, bv, first)
        da = da_ref[...]
        sg = jax.nn.sigmoid(cg)
        dcg = da * cv * sg * (1.0 + cg * (1.0 - sg))
        dcv = da * cg * sg
        dcg_ref[...] = dcg
        dcv_ref[...] = dcv
        for dc, taps, dw, db in ((dcg, tg, dwg, dbg), (dcv, tv, dwv, dbv)):
            for n in range(3):
                dw[n:n + 1, :] += jnp.sum(dc * taps[n], axis=0, keepdims=True)
            db[0:1, :] += jnp.sum(dc, axis=0, keepdims=True)

    row = pl.BlockSpec((tb, f), lambda i: (i, 0))
    acc = pl.BlockSpec((8, f), lambda i: (0, 0))
    return pl.pallas_call(
        body, name=name, grid=(lp // tb,),
        in_specs=[cur(0), cur(1), prev(0), prev(1), vec(8, 0), vec(8, 1), vec(1, 0), vec(1, 1), row],
        out_specs=[row, row, acc, acc, acc, acc],
        out_shape=[SDS((lp, f), F32)] * 2 + [SDS((8, f), F32)] * 4,
        compiler_params=_params(("arbitrary",)),
    )(u, u, u, u, cw, cw, cb, cb, dact)


def _ffn_act_bwd_in(dcg, dcv, cw, name):
    lp = dcg.shape[0]
    f = D_FF
    tb = TILE_ROW
    nb = lp // tb
    hb = tb // 8
    cur = pl.BlockSpec((tb, f), lambda i: (i, 0))
    nxt = pl.BlockSpec((8, f), lambda i: (jnp.minimum((i + 1) * hb, nb * hb - 1), 0))
    vec = lambda c: pl.BlockSpec((8, f), lambda i, c=c: (0, c))

    def body(dg, dv, ng, nv, wg, wv, og, ov):
        last = pl.program_id(0) == nb - 1
        for dc_ref, n_ref, w_ref, o_ref in ((dg, ng, wg, og), (dv, nv, wv, ov)):
            dc, halo = dc_ref[...], n_ref[...]
            du = (w_ref[2:3, :] * dc + w_ref[1:2, :] * _shift_up(dc, halo, 1, last)
                  + w_ref[0:1, :] * _shift_up(dc, halo, 2, last))
            o_ref[...] = du.astype(BF16)

    return pl.pallas_call(
        body, name=name, grid=(nb,),
        in_specs=[cur, cur, nxt, nxt, vec(0), vec(1)],
        out_specs=[cur, cur],
        out_shape=[SDS((lp, f), BF16)] * 2,
        compiler_params=_params(("parallel",)),
    )(dcg, dcv, dcg, dcv, cw, cw)


def _loss_head(y, target, n_real, name):
    lp, d = y.shape
    tb = TILE_MM

    def body(y_ref, t_ref, dy_ref, loss_ref):
        i = pl.program_id(0)

        @pl.when(i == 0)
        def _():
            loss_ref[...] = jnp.zeros_like(loss_ref)

        rows = i * tb + lax.broadcasted_iota(jnp.int32, (tb, 1), 0)
        real = (rows >= N_META) & (rows < N_META + n_real)
        diff = jnp.where(real, y_ref[...] - t_ref[...], 0.0)
        dy_ref[...] = diff * (1.0 / d)
        loss_ref[...] += (0.5 / d) * jnp.sum(diff * diff).reshape(1, 1)

    row = pl.BlockSpec((tb, d), lambda i: (i, 0))
    return pl.pallas_call(
        body, name=name, grid=(lp // tb,), in_specs=[row, row],
        out_specs=[row, pl.BlockSpec((1, 1), lambda i: (0, 0))],
        out_shape=[SDS((lp, d), F32), SDS((1, 1), F32)],
        compiler_params=_params(("arbitrary",)),
    )(y, target)


def _pad_lanes(v, width, at=0):
    return jnp.pad(v.astype(F32), (at, width - at - v.shape[0]))[None, :]


def _layer_params(w, l):
    b = lambda a: a.astype(BF16)
    win = w["w_in"][l]
    fq, fk, fv, ff, cq, ckv, kr, sq, sk, sv, gates = jnp.split(
        win, [512, 1024, 1536, 1544, 1800, 1928, 1960, 2472, 2600, 2728], axis=1)
    misc = jnp.concatenate([kr, ff, jnp.zeros((D_MODEL, 88), win.dtype)], axis=1)
    w_in = b(jnp.concatenate([gates, fq, fk, fv, sq, sk, sv, cq, ckv, misc], axis=1))
    wq = jnp.pad(w["mla_w_q_up"][l].reshape(256, HEADS, 96), ((0, 0), (0, 0), (0, 32))).reshape(256, 1024)
    wkv = w["mla_w_kv_up"][l].reshape(128, HEADS, 128)
    wkk = jnp.pad(wkv[:, :, :64], ((0, 0), (0, 0), (0, 64))).reshape(128, 1024)
    wkvv = wkv[:, :, 64:].reshape(128, 512)
    tile = lambda g, n: jnp.tile(g.astype(F32), n)[None, :]
    prm = [tile(w["fox_q_g"][l], 8), tile(w["fox_k_g"][l], 8), tile(w["swa_q_g"][l], 8), tile(w["swa_k_g"][l], 2),
           _pad_lanes(w["fox_forget_b"][l], 128, FF_LANE), w["mla_q_a_g"][l][None, :], w["mla_kv_a_g"][l][None, :],
           tile(jnp.pad(w["mla_q_g"][l], (0, 32)), 8), tile(jnp.pad(w["mla_k_g"][l], (0, 32)), 8),
           wq.astype(F32), wkk.astype(F32), wkvv.astype(F32)]
    cw = jnp.pad(w["ffn_conv_w"][l].astype(F32), ((0, 5), (0, 0)))
    return dict(
        g1=w["norm1_g"][l][None, :], w_in=w_in, prm=prm, sinks=w["swa_sinks"][l].astype(F32),
        wb=b(w["w_branch"][l]), wo=b(w["w_o"][l]), g2=w["norm2_g"][l][None, :], w_up=b(w["ffn_w_up"][l]),
        cw=cw, cb=w["ffn_conv_b"][l][None, :].astype(F32), w_down=b(w["ffn_w_down"][l]))


def _cols(c):
    ct = c[:, FF_LANE:FF_LANE + HEADS].T
    return ct[:, :, None], ct[:, None, :]


def _rows(col):
    return jnp.swapaxes(col, 1, 2)


def _from_cols(col):
    return jnp.pad(col[:, :, 0].T, ((0, 0), (FF_LANE, 128 - FF_LANE - HEADS)))


def _layer_fwd(h, lw, consts, cos, sin, slopes, l):
    tag = f"l{l}_"
    xn, proj = _norm_matmul(h, lw["g1"], lw["w_in"], IN_W // 2, tag + "in_proj")
    fq, fk, fv, mq, mk, mv, sq, skd, svd, ls = _prep_fwd(proj, lw["prm"], consts, cos, sin, tag + "prep")
    c = _cumsum([ls], False, tag + "decay_cumsum")
    c_col, c_row = _cols(c)
    oa, lse_a = _att_fwd(_Att("fox"), fq, fk, fv, (c_row,), tag + "fox_fwd")
    ob, lse_b = _att_fwd(_Att("mla"), mq, mk, mv, (), tag + "mla_fwd")
    oc, lse_c = _att_fwd(_Att("swa"), sq, skd, svd, (lw["sinks"], slopes), tag + "swa_fwd")
    h2 = _post_fwd(h, proj, (oa, ob, oc), lw["wb"], lw["wo"], tag + "merge")
    xn2, u = _norm_matmul(h2, lw["g2"], lw["w_up"], D_FF, tag + "ffn_up")
    act = _ffn_act_fwd(u, lw["cw"], lw["cb"], tag + "ffn_act")
    h3 = _matmul_residual(act, lw["w_down"], h2, tag + "ffn_down")
    saved = dict(h=h, xn=xn, proj=proj, q=(fq, mq, sq), k=(fk, mk, skd), v=(fv, mv, svd), c=(c_col, c_row),
                 o=(oa, ob, oc), lse=(lse_a, lse_b, lse_c), h2=h2, xn2=xn2, u=u, act=act)
    return h3, saved


def _layer_bwd(dh3, lw, sv, consts, folds, cos, sin, slopes, l):
    tag = f"l{l}_"
    lp = dh3.shape[0]
    f = D_FF
    dact = _matmul_nt(dh3, lw["w_down"], f, tag + "ffn_down_dx")
    dw_down = _matmul_tn(sv["act"], dh3, D_MODEL, tag + "ffn_down_dw")
    dcg, dcv, dwg, dwv, dbg, dbv = _ffn_act_bwd_conv(sv["u"], dact, lw["cw"], lw["cb"], tag + "ffn_act_dc")
    dug, duv = _ffn_act_bwd_in(dcg, dcv, lw["cw"], tag + "ffn_act_du")
    du = jnp.concatenate([dug, duv], axis=1)
    dw_up = _matmul_tn(sv["xn2"], du, f, tag + "ffn_up_dw")
    tb = TILE_MM
    half = lambda c: pl.BlockSpec((tb, f), lambda i, c=c: (i, c))
    whalf = lambda c: pl.BlockSpec((D_MODEL, f), lambda i, c=c: (0, c))
    dh2, dg2 = _norm_matmul_bwd(du, lw["w_up"], du, lw["w_up"], sv["h2"], lw["g2"], dh3,
                                (half(0), whalf(0), half(1), whalf(1)), tag + "ffn_up_dx")
    dgates, doa, dob, doc, dwb, dwo = _post_bwd(dh2, sv["proj"], sv["o"], lw["wb"], lw["wo"], tag + "merge_bwd")
    c_col, c_row = sv["c"]
    extras = ((c_row,), (), (lw["sinks"], slopes))
    extras_kv = ((c_col,), (), (slopes,))
    grads = []
    for n, (mode, do) in enumerate((("fox", doa), ("mla", dob), ("swa", doc))):
        att = _Att(mode)
        q, k, v = sv["q"][n], sv["k"][n], sv["v"][n]
        res = _att_dq(att, q, k, v, sv["o"][n], do, sv["lse"][n], extras[n], tag + mode + "_dq")
        dq, delta = res[0], res[1]
        res_kv = _att_dkv(att, q, k, v, do, _rows(sv["lse"][n]), _rows(delta), extras_kv[n], tag + mode + "_dkv")
        grads.append((dq, res_kv[0], res_kv[1], res[2:], res_kv[2:]))
    (dfq, dfk, dfv, (dcq,), (dck,)), (dmq, dmk, dmv, _, _), (dsq, dskp, dsvp, (dsink,), _) = grads
    dls = _cumsum([_from_cols(dcq), _from_cols(dck)], True, tag + "decay_cumsum_bwd")
    res = _prep_bwd(sv["proj"], lw["prm"], consts, cos, sin,
                    (dfq, dfk, dfv, dmq, dmk, dmv, dsq, dskp, dsvp, dls), folds, tag + "prep_bwd")
    dother, pg = res[0], res[1:]
    dw_g = _matmul_tn(sv["xn"], dgates, GATES_W, tag + "in_proj_dw_gates")
    dw_o = _matmul_tn(sv["xn"], dother, OTHER_W, tag + "in_proj_dw_other")
    full = lambda w: pl.BlockSpec((tb, w), lambda i: (i, 0))
    wfull = lambda w: pl.BlockSpec((D_MODEL, w), lambda i: (0, 0))
    dh, dg1 = _norm_matmul_bwd(dgates, lw["w_in"][:, :GATES_W], dother, lw["w_in"][:, GATES_W:], sv["h"], lw["g1"], dh2,
                               (full(GATES_W), wfull(GATES_W), full(OTHER_W), wfull(OTHER_W)), tag + "in_proj_dx")
    d_in = jnp.concatenate([
        dw_o[:, O_FQ:O_FV + 512], dw_o[:, O_MISC + FF_LANE:O_MISC + FF_LANE + 8], dw_o[:, O_CQ:O_CQ + 256],
        dw_o[:, O_CKV:O_CKV + 128], dw_o[:, O_MISC:O_MISC + 32], dw_o[:, O_SQ:O_SQ + 512], dw_o[:, O_SK:O_SK + 128],
        dw_o[:, O_SV:O_SV + 128], dw_g], axis=1)
    d_wq = pg[9].reshape(256, HEADS, 128)[:, :, :96].reshape(256, 768)
    d_wkv = jnp.concatenate([pg[10].reshape(128, HEADS, 128)[:, :, :64], pg[11].reshape(128, HEADS, 64)],
                            axis=2).reshape(128, 1024)
    g = dict(
        norm1_g=dg1[0], w_in=d_in, fox_forget_b=pg[4][0, FF_LANE:FF_LANE + 8], fox_q_g=pg[0][0, :64],
        fox_k_g=pg[1][0, :64], mla_q_a_g=pg[5][0], mla_w_q_up=d_wq, mla_kv_a_g=pg[6][0], mla_w_kv_up=d_wkv,
        mla_q_g=pg[7][0, :96], mla_k_g=pg[8][0, :96], swa_q_g=pg[2][0, :64], swa_k_g=pg[3][0, :64],
        swa_sinks=dsink[:, 0, 0:2].reshape(HEADS), w_branch=dwb, w_o=dwo, norm2_g=dg2[0], ffn_w_up=dw_up,
        ffn_conv_w=jnp.concatenate([dwg[0:3], dwv[0:3]], axis=1),
        ffn_conv_b=jnp.concatenate([dbg[0], dbv[0]]), ffn_w_down=dw_down)
    return dh, g


def _local_step(x, target, w):
    seq = x.shape[0]
    length = N_META + seq
    lp = -(-length // ROW_ALIGN) * ROW_ALIGN
    pad = lp - length
    h = jnp.concatenate([w["meta_tokens"].astype(F32), x, jnp.zeros((pad, D_MODEL), F32)], axis=0)
    tgt = jnp.pad(target, ((N_META, pad), (0, 0)))
    consts = _consts()
    folds = (_fold_matrix(512, 64), _fold_matrix(1024, 128))
    cos, sin = _rope_tables(lp)
    slopes = jnp.asarray(2.0 ** (-8.0 * np.arange(1, HEADS + 1, dtype=np.float32) / HEADS), F32)
    lws = [_layer_params(w, l) for l in range(DEPTH)]
    saved = []
    for l in range(DEPTH):
        h, sv = _layer_fwd(h, lws[l], consts, cos, sin, slopes, l)
        saved.append(sv)
    dh, loss = _loss_head(h, tgt, seq, "loss_head")
    grads = [None] * DEPTH
    for l in reversed(range(DEPTH)):
        dh, grads[l] = _layer_bwd(dh, lws[l], saved[l], consts, folds, cos, sin, slopes, l)
    stacked = {k: jnp.stack([grads[l][k] for l in range(DEPTH)]) for k in grads[0]}
    return loss, dh[N_META:length], dh[:N_META], stacked


def _place():
    return lax.axis_index("x"), lax.axis_index("y"), lax.axis_index("c")


def _flip(pos, k):
    x, y, c = pos
    return (1 - x if k & 4 else x, 1 - y if k & 2 else y, 1 - c if k & 1 else c)


def _index(pos):
    return 4 * pos[0] + 2 * pos[1] + pos[2]


def _gather(tensors, name):
    n_t = len(tensors)

    def body(*refs):
        ins, outs = refs[:n_t], refs[n_t:2 * n_t]
        send_sems, recv_sems, local_sems = refs[2 * n_t:]
        x, y, c = _place()
        me, sibling = (x, y, c), (x, y, 1 - c)
        chips = [(1 - x, y), (x, 1 - y), (1 - x, 1 - y)]

        def copy(t, k, block, to, src=None):
            dst = outs[t].at[_index(block)]
            return pltpu.make_async_remote_copy(
                src_ref=dst if src is None else src, dst_ref=dst, send_sem=send_sems.at[t, k],
                recv_sem=recv_sems.at[t, k], device_id=to, device_id_type=pl.DeviceIdType.MESH)

        local, sent = [], []
        for t in range(n_t):
            local.append(pltpu.make_async_copy(ins[t], outs[t].at[_index(me)], local_sems.at[t]))
            local[-1].start()
            sent.append(copy(t, 0, me, sibling, src=ins[t]))
            sent += [copy(t, 1 + j, me, (*chip, c), src=ins[t]) for j, chip in enumerate(chips)]
        for cp in sent:
            cp.start()
        for j, chip in enumerate(chips):
            for t in range(n_t):
                copy(t, 1 + j, (*chip, c), me).wait_recv()
                sent.append(copy(t, 4 + j, (*chip, c), sibling))
                sent[-1].start()
        for t in range(n_t):
            copy(t, 0, sibling, me).wait_recv()
            for j, chip in enumerate(chips):
                copy(t, 4 + j, (*chip, 1 - c), me).wait_recv()
        for cp in sent:
            cp.wait_send()
        for cp in local:
            cp.wait()

    any_spec = pl.BlockSpec(memory_space=pl.ANY)
    return pl.pallas_call(
        body, name=name, in_specs=[any_spec] * n_t, out_specs=[any_spec] * n_t,
        out_shape=[SDS((N_DEV,) + a.shape, a.dtype) for a in tensors],
        scratch_shapes=[pltpu.SemaphoreType.DMA((n_t, N_DEV - 1)), pltpu.SemaphoreType.DMA((n_t, N_DEV - 1)),
                        pltpu.SemaphoreType.DMA((n_t,))],
    )(*tensors)


def _all_to_all(tensors, name):
    n_t = len(tensors)

    def body(*refs):
        ins, outs = refs[:n_t], refs[n_t:2 * n_t]
        send_sems, recv_sems, local_sems = refs[2 * n_t:]
        me = _place()
        mine = _index(me)
        local, sent = [], []
        for t in range(n_t):
            local.append(pltpu.make_async_copy(ins[t].at[mine], outs[t].at[mine], local_sems.at[t]))
            local[-1].start()
            for k in range(1, N_DEV):
                peer = _flip(me, k)
                sent.append(pltpu.make_async_remote_copy(
                    src_ref=ins[t].at[_index(peer)], dst_ref=outs[t].at[mine], send_sem=send_sems.at[t, k - 1],
                    recv_sem=recv_sems.at[t, k - 1], device_id=peer, device_id_type=pl.DeviceIdType.MESH))
                sent[-1].start()
        for cp in sent:
            cp.wait_send()
        for t in range(n_t):
            for k in range(1, N_DEV):
                peer = _flip(me, k)
                pltpu.make_async_remote_copy(
                    src_ref=ins[t].at[mine], dst_ref=outs[t].at[_index(peer)], send_sem=send_sems.at[t, k - 1],
                    recv_sem=recv_sems.at[t, k - 1], device_id=peer, device_id_type=pl.DeviceIdType.MESH).wait_recv()
        for cp in local:
            cp.wait()

    any_spec = pl.BlockSpec(memory_space=pl.ANY)
    return pl.pallas_call(
        body, name=name, in_specs=[any_spec] * n_t, out_specs=[any_spec] * n_t,
        out_shape=[SDS(a.shape, a.dtype) for a in tensors],
        scratch_shapes=[pltpu.SemaphoreType.DMA((n_t, N_DEV - 1)), pltpu.SemaphoreType.DMA((n_t, N_DEV - 1)),
                        pltpu.SemaphoreType.DMA((n_t,))],
    )(*tensors)


def _sum_slots(parts, name):
    n, rows, w = parts.shape
    tb = 8

    def body(p_ref, o_ref):
        acc = p_ref[0].astype(F32)
        for s in range(1, n):
            acc = acc + p_ref[s].astype(F32)
        o_ref[...] = acc

    return pl.pallas_call(
        body, name=name, grid=(rows // tb,),
        in_specs=[pl.BlockSpec((n, tb, w), lambda i: (0, i, 0))], out_specs=pl.BlockSpec((tb, w), lambda i: (i, 0)),
        out_shape=SDS((rows, w), F32), compiler_params=_params(("parallel",)),
    )(parts)


def _adamw(wt, m, v, parts, name):
    shape = wt.shape
    n, w = parts.shape[0], shape[-1]
    rows = math.prod(shape[:-1])
    step = 16 if parts.dtype == BF16 else 8
    tb = max([t for t in range(step, 257, step) if rows % t == 0] or [rows])
    c1 = 1.0 / (1.0 - ADAM_B1 ** ADAM_STEP)
    c2 = 1.0 / (1.0 - ADAM_B2 ** ADAM_STEP)

    def body(w_ref, m_ref, v_ref, p_ref, g_out, d_out, m_out, v_out):
        g = p_ref[0].astype(F32)
        for s in range(1, n):
            g = g + p_ref[s].astype(F32)
        m_new = ADAM_B1 * m_ref[...] + (1.0 - ADAM_B1) * g
        v_new = ADAM_B2 * v_ref[...] + (1.0 - ADAM_B2) * (g * g)
        g_out[...] = g
        m_out[...] = m_new
        v_out[...] = v_new
        d_out[...] = -ADAM_LR * ((m_new * c1) / (jnp.sqrt(v_new * c2) + ADAM_EPS) + ADAM_WD * w_ref[...])

    row = pl.BlockSpec((tb, w), lambda i: (i, 0))
    outs = pl.pallas_call(
        body, name=name, grid=(rows // tb,),
        in_specs=[row, row, row, pl.BlockSpec((n, tb, w), lambda i: (0, i, 0))], out_specs=[row] * 4,
        out_shape=[SDS((rows, w), F32)] * 4, compiler_params=_params(("parallel",)),
    )(wt.reshape(rows, w), m.reshape(rows, w), v.reshape(rows, w), parts.reshape(n, rows, w))
    return [o.reshape(shape) for o in outs]


_BIG = [("w_in", 2), ("mla_w_q_up", 2), ("mla_w_kv_up", 2), ("w_branch", 3), ("w_o", 1), ("ffn_w_up", 2), ("ffn_w_down", 1)]
_SMALL_SHARDED = [("meta_tokens", 1), ("ffn_conv_w", 2)]
_REPLICATED = ["norm1_g", "fox_forget_b", "fox_q_g", "fox_k_g", "mla_q_a_g", "mla_kv_a_g", "mla_q_g", "mla_k_g",
               "swa_q_g", "swa_k_g", "swa_sinks", "norm2_g", "ffn_conv_b"]
_ORDER = ["meta_tokens", "norm1_g", "w_in", "fox_forget_b", "fox_q_g", "fox_k_g", "mla_q_a_g", "mla_w_q_up",
          "mla_kv_a_g", "mla_w_kv_up", "mla_q_g", "mla_k_g", "swa_q_g", "swa_k_g", "swa_sinks", "w_branch", "w_o",
          "norm2_g", "ffn_w_up", "ffn_conv_w", "ffn_conv_b", "ffn_w_down"]


def _flat_rows(vecs, dtype, row_mult):
    flat = jnp.concatenate([a.reshape(-1).astype(dtype) for a in vecs])
    rows = -(-flat.shape[0] // (1024 * row_mult)) * row_mult
    return jnp.pad(flat, (0, rows * 1024 - flat.shape[0])).reshape(rows, 1024)


def _unflatten(flat, shapes):
    out, off = [], 0
    for s in shapes:
        n = math.prod(s)
        out.append(flat[off:off + n].reshape(s))
        off += n
    return out


def _to_full(blocks, axis):
    moved = jnp.moveaxis(blocks, 0, axis)
    s = moved.shape
    return moved.reshape(s[:axis] + (s[axis] * s[axis + 1],) + s[axis + 2:])


def _to_blocks(full, axis):
    s = full.shape
    split = full.reshape(s[:axis] + (N_DEV, s[axis] // N_DEV) + s[axis + 1:])
    return jnp.moveaxis(split, axis, 0)


def kernel(x, meta_tokens, norm1_g, w_in, fox_forget_b, fox_q_g, fox_k_g, mla_q_a_g, mla_w_q_up, mla_kv_a_g, mla_w_kv_up, mla_q_g, mla_k_g, swa_q_g, swa_k_g, swa_sinks, w_branch, w_o, norm2_g, ffn_w_up, ffn_conv_w, ffn_conv_b, ffn_w_down, loss_target, m_meta_tokens, m_norm1_g, m_w_in, m_fox_forget_b, m_fox_q_g, m_fox_k_g, m_mla_q_a_g, m_mla_w_q_up, m_mla_kv_a_g, m_mla_w_kv_up, m_mla_q_g, m_mla_k_g, m_swa_q_g, m_swa_k_g, m_swa_sinks, m_w_branch, m_w_o, m_norm2_g, m_ffn_w_up, m_ffn_conv_w, m_ffn_conv_b, m_ffn_w_down, v_meta_tokens, v_norm1_g, v_w_in, v_fox_forget_b, v_fox_q_g, v_fox_k_g, v_mla_q_a_g, v_mla_w_q_up, v_mla_kv_a_g, v_mla_w_kv_up, v_mla_q_g, v_mla_k_g, v_swa_q_g, v_swa_k_g, v_swa_sinks, v_w_branch, v_w_o, v_norm2_g, v_ffn_w_up, v_ffn_conv_w, v_ffn_conv_b, v_ffn_w_down):
    wl = dict(zip(_ORDER, (meta_tokens, norm1_g, w_in, fox_forget_b, fox_q_g, fox_k_g, mla_q_a_g, mla_w_q_up,
                           mla_kv_a_g, mla_w_kv_up, mla_q_g, mla_k_g, swa_q_g, swa_k_g, swa_sinks, w_branch, w_o,
                           norm2_g, ffn_w_up, ffn_conv_w, ffn_conv_b, ffn_w_down)))
    ml = dict(zip(_ORDER, (m_meta_tokens, m_norm1_g, m_w_in, m_fox_forget_b, m_fox_q_g, m_fox_k_g, m_mla_q_a_g,
                           m_mla_w_q_up, m_mla_kv_a_g, m_mla_w_kv_up, m_mla_q_g, m_mla_k_g, m_swa_q_g, m_swa_k_g,
                           m_swa_sinks, m_w_branch, m_w_o, m_norm2_g, m_ffn_w_up, m_ffn_conv_w, m_ffn_conv_b,
                           m_ffn_w_down)))
    vl = dict(zip(_ORDER, (v_meta_tokens, v_norm1_g, v_w_in, v_fox_forget_b, v_fox_q_g, v_fox_k_g, v_mla_q_a_g,
                           v_mla_w_q_up, v_mla_kv_a_g, v_mla_w_kv_up, v_mla_q_g, v_mla_k_g, v_swa_q_g, v_swa_k_g,
                           v_swa_sinks, v_w_branch, v_w_o, v_norm2_g, v_ffn_w_up, v_ffn_conv_w, v_ffn_conv_b,
                           v_ffn_w_down)))
    big = [n for n, _ in _BIG]
    small_sh = [n for n, _ in _SMALL_SHARDED]

    got = _gather([wl[n].astype(BF16) for n in big] + [wl[n] for n in small_sh], "gather_weights")
    full = {n: wl[n] for n in _REPLICATED}
    for (n, axis), blocks in zip(_BIG + _SMALL_SHARDED, got):
        full[n] = _to_full(blocks, axis)

    loss, grad_x, grad_meta, grads = _local_step(x[0], loss_target[0], full)
    grads["meta_tokens"] = grad_meta

    parts = _all_to_all([_to_blocks(grads[n], axis).astype(BF16) for n, axis in _BIG], "exchange_grads")
    result = {kind: {} for kind in ("grad", "delta", "new_m", "new_v")}
    for n, p in zip(big, parts):
        for kind, val in zip(result, _adamw(wl[n], ml[n], vl[n], p, "adamw_" + n)):
            result[kind][n] = val

    small_full = _REPLICATED + small_sh
    mine_small = _flat_rows([grads[n] for n in small_full] + [loss], F32, 8)
    total_small = _sum_slots(_gather([mine_small], "gather_small_grads")[0], "sum_small_grads").reshape(-1)
    pieces = _unflatten(total_small, [grads[n].shape for n in small_full] + [()])
    loss_total = pieces[-1]
    g_small = dict(zip(small_full, pieces[:-1]))
    idx = _index(_place())
    for n, axis in _SMALL_SHARDED:
        size = wl[n].shape[axis]
        g_small[n] = lax.dynamic_slice_in_dim(g_small[n], idx * size, size, axis)
    flat = lambda d: _flat_rows([d[n] for n in small_full], F32, 8)
    small_out = _adamw(flat(wl), flat(ml), flat(vl), flat(g_small)[None], "adamw_small")
    for kind, fs in zip(result, small_out):
        result[kind].update(zip(small_full, _unflatten(fs.reshape(-1), [wl[n].shape for n in small_full])))
    outs = [loss_total, grad_x[None]]
    for kind in ("grad", "delta", "new_m", "new_v"):
        outs += [result[kind][n] for n in _ORDER]
    return tuple(outs)
```

```python
import functools
import math

import numpy as np
import jax
import jax.numpy as jnp
from jax import lax
from jax.experimental import pallas as pl
from jax.experimental.pallas import tpu as pltpu

F32, BF16 = jnp.float32, jnp.bfloat16
SDS = jax.ShapeDtypeStruct

D_MODEL = 1024
N_META = 16
EPS = 1e-6
WINDOW = 128
ROPE_THETA = 10000.0
HEADS = 8
D_FF = 2816
DEPTH = 2
N_DEV = 8
ADAM_LR, ADAM_B1, ADAM_B2, ADAM_EPS, ADAM_WD, ADAM_STEP = 0.001, 0.9, 0.999, 1e-08, 0.01, 10

ROW_ALIGN = 384
TILE_MM = 384
TILE_ROW = 128
TILE_ATT = 384
TILE_POST = 128
PAIRS = 2
VMEM_LIMIT = 56 * 1024 * 1024

GATES_W = 3072
OTHER_W = 2816
IN_W = GATES_W + OTHER_W
O_FQ, O_FK, O_FV, O_SQ, O_SK, O_SV, O_CQ, O_CKV, O_MISC = 0, 512, 1024, 1536, 2048, 2176, 2304, 2560, 2688
FF_LANE = 32

NEG = -1e30


def _dot(a, b):
    return jnp.dot(a, b, preferred_element_type=F32)


def _dot_nt(a, b):
    return lax.dot_general(a, b, (((1,), (1,)), ((), ())), preferred_element_type=F32)


def _dot_tn(a, b):
    return lax.dot_general(a, b, (((0,), (0,)), ((), ())), preferred_element_type=F32)


def _params(sem):
    return pltpu.CompilerParams(dimension_semantics=sem, vmem_limit_bytes=VMEM_LIMIT)


def _rms(x, g):
    return x * lax.rsqrt(jnp.mean(x * x, axis=-1, keepdims=True) + EPS) * g


def _split_dot(x, m, pieces=2):
    acc, rest = None, x
    for _ in range(pieces):
        part = rest.astype(BF16)
        rest = rest - part.astype(F32)
        acc = _dot(part, m) if acc is None else acc + _dot(part, m)
    return acc


@jax.custom_vjp
def _sel(x, m, mt):
    return _split_dot(x, m)


_sel.defvjp(lambda x, m, mt: (_split_dot(x, m), (m, mt)), lambda res, dy: (_split_dot(dy, res[1]), None, None))


@jax.custom_vjp
def _mm(x, w):
    return _dot(x.astype(BF16), w.astype(BF16))


def _mm_bwd(res, dy):
    x, w = res
    dyb = dy.astype(BF16)
    return _dot_nt(dyb, w.astype(BF16)), _dot_tn(x.astype(BF16), dyb)


_mm.defvjp(lambda x, w: (_mm(x, w), (x, w)), _mm_bwd)


def _rot_impl(x):
    w = x.shape[1]
    lane = lax.broadcasted_iota(jnp.int32, x.shape, 1) % 128
    lo = (lane >= 64) & (lane < 80)
    hi = (lane >= 80) & (lane < 96)
    return jnp.where(hi, pltpu.roll(x, 16, 1), 0.0) - jnp.where(lo, pltpu.roll(x, w - 16, 1), 0.0)


@jax.custom_vjp
def _rot(x):
    return _rot_impl(x)


_rot.defvjp(lambda x: (_rot_impl(x), None), lambda _, dy: (-_rot_impl(dy),))


def _gnorm(x, g, e, et, dim):
    inv = lax.rsqrt(_sel(x * x, e, et) * (1.0 / dim) + EPS)
    return x * _sel(inv, et, e) * g


def _indicator(width, period):
    m = np.zeros((width, 128), np.float32)
    m[np.arange(width), np.arange(width) // period] = 1.0
    return m


def _consts():
    e64 = _indicator(512, 64)
    e128 = _indicator(1024, 128)
    sk = np.zeros((128, 1024), np.float32)
    for h in range(HEADS):
        sk[np.arange(32), 128 * h + 64 + np.arange(32)] = 1.0
    dup = np.zeros((128, 256), np.float32)
    for g in range(2):
        for r in range(2):
            dup[64 * g + np.arange(64), 128 * g + 64 * r + np.arange(64)] = 1.0
    mats = [e64, e64.T, e128, e128.T, sk, sk.T, dup, dup.T]
    return [jnp.asarray(m, BF16) for m in mats]


def _fold_matrix(width, period):
    m = np.zeros((width, 128), np.float32)
    m[np.arange(width), np.arange(width) % period] = 1.0
    return jnp.asarray(m, BF16)


def _rope_tables(lp):
    half = 16
    freqs = ROPE_THETA ** (-np.arange(half, dtype=np.float32) / half)
    ang = np.arange(lp, dtype=np.float32)[:, None] * freqs[None, :]
    cos = np.ones((lp, 128), np.float32)
    sin = np.zeros((lp, 128), np.float32)
    cos[:, 64:80] = np.cos(ang)
    cos[:, 80:96] = np.cos(ang)
    sin[:, 64:80] = np.sin(ang)
    sin[:, 80:96] = np.sin(ang)
    return jnp.asarray(cos), jnp.asarray(sin)


def _norm_matmul(h, g, w, tn, name):
    lp, d = h.shape
    n = w.shape[1]
    tb = TILE_MM

    def body(h_ref, g_ref, w_ref, xn_ref, y_ref):
        @pl.when(pl.program_id(1) == 0)
        def _():
            xn_ref[...] = _rms(h_ref[...], g_ref[...]).astype(BF16)

        y_ref[...] = _dot(xn_ref[...], w_ref[...])

    return pl.pallas_call(
        body, name=name, grid=(lp // tb, n // tn),
        in_specs=[pl.BlockSpec((tb, d), lambda i, j: (i, 0)), pl.BlockSpec((1, d), lambda i, j: (0, 0)),
                  pl.BlockSpec((d, tn), lambda i, j: (0, j))],
        out_specs=[pl.BlockSpec((tb, d), lambda i, j: (i, 0)), pl.BlockSpec((tb, tn), lambda i, j: (i, j))],
        out_shape=[SDS((lp, d), BF16), SDS((lp, n), F32)],
        compiler_params=_params(("parallel", "arbitrary")),
    )(h, g, w)


def _matmul_residual(a, w, res, name):
    m, k = a.shape
    n = w.shape[1]
    tb = TILE_MM

    def body(a_ref, w_ref, r_ref, o_ref):
        o_ref[...] = r_ref[...] + _dot(a_ref[...], w_ref[...])

    return pl.pallas_call(
        body, name=name, grid=(m // tb,),
        in_specs=[pl.BlockSpec((tb, k), lambda i: (i, 0)), pl.BlockSpec((k, n), lambda i: (0, 0)),
                  pl.BlockSpec((tb, n), lambda i: (i, 0))],
        out_specs=pl.BlockSpec((tb, n), lambda i: (i, 0)),
        out_shape=SDS((m, n), F32),
        compiler_params=_params(("parallel",)),
    )(a, w, res)


def _matmul_nt(dy, w, tn, name):
    m, k = dy.shape
    n = w.shape[0]
    tb = TILE_MM

    def body(dy_ref, w_ref, o_ref):
        o_ref[...] = _dot_nt(dy_ref[...].astype(BF16), w_ref[...])

    return pl.pallas_call(
        body, name=name, grid=(m // tb, n // tn),
        in_specs=[pl.BlockSpec((tb, k), lambda i, j: (i, 0)), pl.BlockSpec((tn, k), lambda i, j: (j, 0))],
        out_specs=pl.BlockSpec((tb, tn), lambda i, j: (i, j)),
        out_shape=SDS((m, n), F32),
        compiler_params=_params(("parallel", "arbitrary")),
    )(dy, w)


def _matmul_tn(x, dy, tn, name):
    m, k = x.shape
    n = dy.shape[1]
    tb = TILE_MM

    def body(x_ref, dy_ref, o_ref):
        @pl.when(pl.program_id(1) == 0)
        def _():
            o_ref[...] = jnp.zeros_like(o_ref)

        o_ref[...] += _dot_tn(x_ref[...].astype(BF16), dy_ref[...].astype(BF16))

    return pl.pallas_call(
        body, name=name, grid=(n // tn, m // tb),
        in_specs=[pl.BlockSpec((tb, k), lambda j, i: (i, 0)), pl.BlockSpec((tb, tn), lambda j, i: (i, j))],
        out_specs=pl.BlockSpec((k, tn), lambda j, i: (0, j)),
        out_shape=SDS((k, n), F32),
        compiler_params=_params(("parallel", "arbitrary")),
    )(x, dy)


def _norm_matmul_bwd(dy1, w1, dy2, w2, x, g, dres, specs, name):
    m, d = x.shape
    tb = TILE_MM
    (dy1_spec, w1_spec, dy2_spec, w2_spec) = specs

    def body(dy1_ref, w1_ref, dy2_ref, w2_ref, x_ref, g_ref, r_ref, o_ref, dg_ref):
        @pl.when(pl.program_id(0) == 0)
        def _():
            dg_ref[...] = jnp.zeros_like(dg_ref)

        dxn = _dot_nt(dy1_ref[...], w1_ref[...]) + _dot_nt(dy2_ref[...], w2_ref[...])
        _, vjp = jax.vjp(_rms, x_ref[...], g_ref[...])
        dx, dg = vjp(dxn)
        o_ref[...] = r_ref[...] + dx
        dg_ref[...] += dg

    row = pl.BlockSpec((tb, d), lambda i: (i, 0))
    vec = pl.BlockSpec((1, d), lambda i: (0, 0))
    return pl.pallas_call(
        body, name=name, grid=(m // tb,),
        in_specs=[dy1_spec, w1_spec, dy2_spec, w2_spec, row, vec, row],
        out_specs=[row, vec],
        out_shape=[SDS((m, d), F32), SDS((1, d), F32)],
        compiler_params=_params(("arbitrary",)),
    )(dy1, w1, dy2, w2, x, g, dres)


def _prep_math(pieces, prm, consts, cos, sin):
    fq, fk, sq, sk, sv, cq, ckv, misc = pieces
    gfq, gfk, gsq, gsk, fb, gqa, gkva, gmq, gmk, wq, wkk, wkv = prm
    e64, e64t, e128, e128t, skm, skt, dup, dupt = consts
    cos8 = jnp.concatenate([cos] * HEADS, axis=1)
    sin8 = jnp.concatenate([sin] * HEADS, axis=1)
    fq_n = _gnorm(fq, gfq, e64, e64t, 64)
    fk_n = _gnorm(fk, gfk, e64, e64t, 64)
    ls = jax.nn.log_sigmoid(misc + fb)
    q = _gnorm(_mm(_rms(cq, gqa), wq), gmq, e128, e128t, 96)
    mq = q * cos8 + _rot(q) * sin8
    kva = _rms(ckv, gkva)
    k = _gnorm(_mm(kva, wkk) + _sel(misc, skm, skt), gmk, e128, e128t, 96)
    mk = k * cos8 + _rot(k) * sin8
    mv = _mm(kva, wkv)
    sq_n = _gnorm(sq, gsq, e64, e64t, 64)
    sk_n = _gnorm(sk, gsk, e64[0:128], e64t[:, 0:128], 64)
    skd = _sel(sk_n, dup, dupt)
    svd = _sel(sv, dup, dupt)
    return fq_n, fk_n, ls, mq, mk, mv, sq_n, skd, svd


_PIECES = [(O_FQ, 512), (O_FK, 512), (O_SQ, 512), (O_SK, 128), (O_SV, 128), (O_CQ, 256), (O_CKV, 128), (O_MISC, 128)]
_PRM_SHAPES = [(1, 512), (1, 512), (1, 512), (1, 128), (1, 128), (1, 256), (1, 128), (1, 1024), (1, 1024),
               (256, 1024), (128, 1024), (128, 512)]
_CONST_SHAPES = [(512, 128), (128, 512), (1024, 128), (128, 1024), (128, 1024), (1024, 128), (128, 256), (256, 128)]


def _piece_specs(tb):
    def spec(off, width):
        blk = (GATES_W + off) // width
        return pl.BlockSpec((tb, width), lambda i, blk=blk: (i, blk))
    return [spec(o, w) for o, w in _PIECES] + [spec(O_FV, 512)]


def _full_specs(shapes):
    return [pl.BlockSpec(s, lambda i: (0, 0)) for s in shapes]


def _prep_fwd(proj, prm, consts, cos, sin, name):
    lp = proj.shape[0]
    tb = TILE_ROW
    row = lambda w: pl.BlockSpec((tb, w), lambda i: (i, 0))

    def body(*refs):
        pieces = [r[...] for r in refs[0:8]]
        fv = refs[8][...]
        prm_v = [r[...] for r in refs[9:21]]
        consts_v = [r[...] for r in refs[21:29]]
        cos_v, sin_v = refs[29][...], refs[30][...]
        outs = refs[31:]
        fq_n, fk_n, ls, mq, mk, mv, sq_n, skd, svd = _prep_math(pieces, prm_v, consts_v, cos_v, sin_v)
        for ref, val in zip(outs, (fq_n, fk_n, fv, mq, mk, mv, sq_n, skd, svd)):
            ref[...] = val.astype(BF16)
        outs[9][...] = ls

    widths = [512, 512, 512, 1024, 1024, 512, 512, 256, 256]
    return pl.pallas_call(
        body, name=name, grid=(lp // tb,),
        in_specs=_piece_specs(tb) + _full_specs(_PRM_SHAPES) + _full_specs(_CONST_SHAPES) + [row(128), row(128)],
        out_specs=[row(w) for w in widths] + [row(128)],
        out_shape=[SDS((lp, w), BF16) for w in widths] + [SDS((lp, 128), F32)],
        compiler_params=_params(("parallel",)),
    )(*([proj] * 9), *prm, *consts, cos, sin)


def _prep_bwd(proj, prm, consts, cos, sin, cots, folds, name):
    lp = proj.shape[0]
    tb = TILE_ROW
    row = lambda w: pl.BlockSpec((tb, w), lambda i: (i, 0))
    fold64, fold128 = folds

    def body(*refs):
        pieces = [r[...] for r in refs[0:8]]
        prm_v = [r[...] for r in refs[9:21]]
        consts_v = [r[...] for r in refs[21:29]]
        cos_v, sin_v = refs[29][...], refs[30][...]
        dfq, dfk, dfv, dmq, dmk, dmv, dsq, dskp, dsvp, dls = [r[...] for r in refs[31:41]]
        f64, f128 = refs[41][...], refs[42][...]
        d_ref = refs[43]
        g_refs = refs[44:]

        @pl.when(pl.program_id(0) == 0)
        def _():
            for r in g_refs:
                r[...] = jnp.zeros_like(r)

        def pair_sum(p):
            return jnp.concatenate([p[:, 0:128] + p[:, 128:256], p[:, 256:384] + p[:, 384:512]], axis=1)

        f = lambda pc, pr: _prep_math(pc, pr, consts_v, cos_v, sin_v)
        _, vjp = jax.vjp(f, pieces, prm_v)
        dpc, dprm = vjp((dfq, dfk, dls, dmq, dmk, dmv, dsq, pair_sum(dskp), pair_sum(dsvp)))
        d_fq, d_fk, d_sq, d_sk, d_sv, d_cq, d_ckv, d_misc = dpc
        for off, val in ((O_FQ, d_fq), (O_FK, d_fk), (O_FV, dfv), (O_SQ, d_sq), (O_SK, d_sk), (O_SV, d_sv),
                         (O_CQ, d_cq), (O_CKV, d_ckv), (O_MISC, d_misc)):
            d_ref[:, off:off + val.shape[1]] = val.astype(BF16)
        folded = {0: f64, 1: f64, 2: f64, 3: f64[0:128], 7: f128, 8: f128}
        for idx, (ref, val) in enumerate(zip(g_refs, dprm)):
            if idx in folded:
                ref[...] += _split_dot(jnp.broadcast_to(val, (8, val.shape[1])), folded[idx], 3)
            elif val.shape[0] == 1:
                ref[...] += jnp.broadcast_to(val, ref.shape)
            else:
                ref[...] += val

    g_shapes = [(8, 128), (8, 128), (8, 128), (8, 128), (8, 128), (8, 256), (8, 128), (8, 128), (8, 128),
                (256, 1024), (128, 1024), (128, 512)]
    cot_widths = [512, 512, 512, 1024, 1024, 512, 512, 512, 512, 128]
    return pl.pallas_call(
        body, name=name, grid=(lp // tb,),
        in_specs=(_piece_specs(tb) + _full_specs(_PRM_SHAPES) + _full_specs(_CONST_SHAPES) + [row(128), row(128)]
                  + [row(w) for w in cot_widths] + _full_specs([(512, 128), (1024, 128)])),
        out_specs=[row(OTHER_W)] + _full_specs(g_shapes),
        out_shape=[SDS((lp, OTHER_W), BF16)] + [SDS(s, F32) for s in g_shapes],
        compiler_params=_params(("arbitrary",)),
    )(*([proj] * 9), *prm, *consts, cos, sin, *cots, fold64, fold128)


def _cumsum(xs, reverse, name):
    lp = xs[0].shape[0]
    tb = TILE_MM
    nb = lp // tb
    n_in = len(xs)
    idx = (lambda i: (nb - 1 - i, 0)) if reverse else (lambda i: (i, 0))

    def body(*refs):
        o_ref, carry = refs[n_in], refs[n_in + 1]

        @pl.when(pl.program_id(0) == 0)
        def _():
            carry[...] = jnp.zeros_like(carry)

        x = refs[0][...]
        for r in refs[1:n_in]:
            x = x + r[...]
        r_i = lax.broadcasted_iota(jnp.int32, (tb, tb), 0)
        c_i = lax.broadcasted_iota(jnp.int32, (tb, tb), 1)
        tri = ((c_i >= r_i) if reverse else (c_i <= r_i)).astype(BF16)
        acc, rest = None, x
        for _ in range(3):
            part = rest.astype(BF16)
            rest = rest - part.astype(F32)
            acc = _dot(tri, part) if acc is None else acc + _dot(tri, part)
        o_ref[...] = acc + carry[...]
        carry[...] += jnp.sum(x, axis=0, keepdims=True)

    return pl.pallas_call(
        body, name=name, grid=(nb,),
        in_specs=[pl.BlockSpec((tb, 128), idx)] * n_in,
        out_specs=pl.BlockSpec((tb, 128), idx),
        out_shape=SDS((lp, 128), F32),
        scratch_shapes=[pltpu.VMEM((1, 128), F32)],
        compiler_params=_params(("arbitrary",)),
    )(*xs)


class _Att:
    def __init__(self, mode):
        self.mode = mode
        self.wide = mode == "mla"
        self.qw = 256 if self.wide else 128
        self.scale = (96 if mode == "mla" else 64) ** -0.5

    def resident(self, x, lo, scaled):
        if self.wide:
            return x[:, 0:128], x[:, 128:256]
        if scaled:
            x = x * jnp.asarray(self.scale, x.dtype)
        zero = jnp.zeros_like(x)
        return jnp.where(lo, x, zero), jnp.where(lo, zero, x)

    def moving(self, x):
        return (x[:, 0:128], x[:, 128:256]) if self.wide else (x, x)

    def logits(self, a, b, qpos, kpos, key_decay, slope, masked):
        s = _dot_nt(a, b)
        if self.wide:
            s = s * self.scale
        if self.mode == "fox":
            s = s - key_decay
        if self.mode == "swa":
            s = s - slope * (qpos - kpos).astype(F32)
        if masked:
            ok = kpos <= qpos
            if self.mode == "swa":
                ok = ok & ((kpos < N_META) | (qpos - kpos < WINDOW))
            s = jnp.where(ok, s, NEG)
        return s


def _halves(x, lo):
    zero = jnp.zeros_like(x)
    return jnp.where(lo, x, zero), jnp.where(lo, zero, x)


def _q_chunks(qi):
    far = qi >= 2
    return jnp.where(far, 3, qi + 1), lambda t: jnp.where(far, jnp.where(t == 0, 0, qi - 2 + t), t)


def _kv_specs(att, lp, rows):
    if att.mode == "swa":
        return (pl.BlockSpec((rows, 128), lambda g, i: (i if rows != lp else 0, g)),) * 2
    return (pl.BlockSpec((rows, PAIRS * att.qw), lambda g, i: (i if rows != lp else 0, g)),
            pl.BlockSpec((rows, PAIRS * 128), lambda g, i: (i if rows != lp else 0, g)))


def _pair_cols(att, x, pp, width):
    return x if x.shape[1] == width else x[:, pp * width:(pp + 1) * width]


def _att_fwd(att, q, k, v, extra, name):
    lp = q.shape[0]
    t = TILE_ATT
    nq = lp // t
    qw = att.qw
    mode = att.mode
    nh = 2 * PAIRS

    def body(*refs):
        q_ref, k_ref, v_ref = refs[0:3]
        o_ref, lse_ref = refs[-2:]
        g, qi = pl.program_id(0), pl.program_id(1)
        lo = lax.broadcasted_iota(jnp.int32, (1, 128), 1) < 64
        q_all = q_ref[...]
        q_heads = [h for pp in range(PAIRS) for h in att.resident(_pair_cols(att, q_all, pp, qw), lo, True)]
        qpos = qi * t + lax.broadcasted_iota(jnp.int32, (t, 1), 0)

        def step(kj, carry, masked):
            ks = pl.multiple_of(kj * t, t)
            kc, vc = k_ref[pl.ds(ks, t), :], v_ref[pl.ds(ks, t), :]
            kpos = kj * t + lax.broadcasted_iota(jnp.int32, (1, t), 1)
            out = []
            for h in range(nh):
                pp = h // 2
                m, l, acc = carry[3 * h:3 * h + 3]
                k_h = att.moving(_pair_cols(att, kc, pp, qw))[h % 2]
                decay = refs[3][h, :, pl.ds(ks, t)] if mode == "fox" else None
                slope = refs[4][nh * g + h] if mode == "swa" else None
                s = att.logits(q_heads[h], k_h, qpos, kpos, decay, slope, masked)
                m_new = jnp.maximum(m, jnp.max(s, axis=-1, keepdims=True))
                alpha = jnp.exp(m - m_new)
                pe = jnp.exp(s - m_new)
                l = alpha * l + jnp.sum(pe, axis=-1, keepdims=True)
                acc = alpha * acc + _dot(pe.astype(BF16), _pair_cols(att, vc, pp, 128))
                out += [m_new, l, acc]
            return tuple(out)

        init = []
        for h in range(nh):
            if mode == "swa":
                init += [jnp.full((t, 1), refs[3][nh * g + h], F32), jnp.ones((t, 1), F32)]
            else:
                init += [jnp.full((t, 1), NEG, F32), jnp.zeros((t, 1), F32)]
            init.append(jnp.zeros((t, 128), F32))
        if mode == "swa":
            n_steps, chunk_of = _q_chunks(qi)
            carry = lax.fori_loop(0, n_steps, lambda i, c: step(chunk_of(i), c, True), tuple(init))
        else:
            carry = lax.fori_loop(0, qi, lambda kj, c: step(kj, c, False), tuple(init))
            carry = step(qi, carry, True)
        outs = []
        for pp in range(PAIRS):
            (ma, la, acca), (mb, lb, accb) = carry[6 * pp:6 * pp + 3], carry[6 * pp + 3:6 * pp + 6]
            outs.append(jnp.where(lo, acca / la, accb / lb).astype(BF16))
            lse_ref[2 * pp] = ma + jnp.log(la)
            lse_ref[2 * pp + 1] = mb + jnp.log(lb)
        o_ref[...] = jnp.concatenate(outs, axis=1)

    in_specs = [pl.BlockSpec((t, PAIRS * qw), lambda g, i: (i, g)), *_kv_specs(att, lp, lp)]
    if mode == "fox":
        in_specs += [pl.BlockSpec((nh, 1, lp), lambda g, i: (g, 0, 0))]
    if mode == "swa":
        in_specs += [pl.BlockSpec(memory_space=pltpu.SMEM)] * 2
    return pl.pallas_call(
        body, name=name, grid=(4 // PAIRS, nq), in_specs=in_specs,
        out_specs=[pl.BlockSpec((t, PAIRS * 128), lambda g, i: (i, g)), pl.BlockSpec((nh, t, 1), lambda g, i: (g, i, 0))],
        out_shape=[SDS((lp, 512), BF16), SDS((HEADS, lp, 1), F32)],
        compiler_params=_params(("parallel", "arbitrary")),
    )(q, k, v, *extra)


def _att_dq(att, q, k, v, o, do, lse, extra, name):
    lp = q.shape[0]
    t = TILE_ATT
    nq = lp // t
    qw = att.qw
    mode = att.mode
    nh = 2 * PAIRS

    def body(*refs):
        q_ref, k_ref, v_ref, o_ref, do_ref, lse_ref = refs[0:6]
        n_out = 2 if mode == "mla" else 3
        outs = refs[len(refs) - n_out:]
        dq_ref, delta_ref = outs[0:2]
        g, qi = pl.program_id(0), pl.program_id(1)
        lo = lax.broadcasted_iota(jnp.int32, (1, 128), 1) < 64
        q_all, do_all = q_ref[...], do_ref[...]
        prod = do_all.astype(F32) * o_ref[...].astype(F32)
        q_heads, do_heads, delta = [], [], []
        for pp in range(PAIRS):
            q_heads += att.resident(_pair_cols(att, q_all, pp, qw), lo, True)
            do_heads += _halves(_pair_cols(att, do_all, pp, 128), lo)
            pr_pp = _pair_cols(att, prod, pp, 128)
            delta += [jnp.sum(jnp.where(lo, pr_pp, 0.0), axis=-1, keepdims=True),
                      jnp.sum(jnp.where(lo, 0.0, pr_pp), axis=-1, keepdims=True)]
        lse_v = [lse_ref[h] for h in range(nh)]
        qpos = qi * t + lax.broadcasted_iota(jnp.int32, (t, 1), 0)

        def step(kj, carry, masked):
            ks = pl.multiple_of(kj * t, t)
            kc, vc = k_ref[pl.ds(ks, t), :], v_ref[pl.ds(ks, t), :]
            kpos = kj * t + lax.broadcasted_iota(jnp.int32, (1, t), 1)
            out = []
            for h in range(nh):
                pp = h // 2
                k_h = att.moving(_pair_cols(att, kc, pp, qw))[h % 2]
                decay = refs[6][h, :, pl.ds(ks, t)] if mode == "fox" else None
                slope = refs[7][nh * g + h] if mode == "swa" else None
                s = att.logits(q_heads[h], k_h, qpos, kpos, decay, slope, masked)
                pr = jnp.exp(s - lse_v[h])
                ds = pr * (_dot_nt(do_heads[h], _pair_cols(att, vc, pp, 128)) - delta[h])
                out.append(carry[2 * h] + _dot(ds.astype(BF16), k_h))
                out.append(carry[2 * h + 1] + jnp.sum(ds, axis=-1, keepdims=True) if mode == "fox" else carry[2 * h + 1])
            return tuple(out)

        init = (jnp.zeros((t, 128), F32), jnp.zeros((t, 1), F32)) * nh
        if mode == "swa":
            n_steps, chunk_of = _q_chunks(qi)
            carry = lax.fori_loop(0, n_steps, lambda i, c: step(chunk_of(i), c, True), init)
        else:
            carry = lax.fori_loop(0, qi, lambda kj, c: step(kj, c, False), init)
            carry = step(qi, carry, True)
        dq = []
        for pp in range(PAIRS):
            dqa, dca, dqb, dcb = carry[4 * pp:4 * pp + 4]
            dq += [dqa, dqb] if att.wide else [jnp.where(lo, dqa, dqb)]
            if mode == "fox":
                outs[2][2 * pp] = dca
                outs[2][2 * pp + 1] = dcb
        dq_ref[...] = jnp.concatenate(dq, axis=1) * att.scale
        for h in range(nh):
            delta_ref[h] = delta[h]
        if mode == "swa":
            ds_ref = outs[2]

            @pl.when(qi == 0)
            def _():
                ds_ref[...] = jnp.zeros_like(ds_ref)

            lane = lax.broadcasted_iota(jnp.int32, (8, 128), 1)
            acc = jnp.zeros((8, 128), F32)
            for h in range(nh):
                tot = -jnp.sum(jnp.exp(refs[6][nh * g + h] - lse_v[h]) * delta[h])
                acc = acc + jnp.where(lane == h, tot, 0.0)
            ds_ref[0] += acc

    col = pl.BlockSpec((nh, t, 1), lambda g, i: (g, i, 0))
    in_specs = [pl.BlockSpec((t, PAIRS * qw), lambda g, i: (i, g)), *_kv_specs(att, lp, lp),
                pl.BlockSpec((t, PAIRS * 128), lambda g, i: (i, g)), pl.BlockSpec((t, PAIRS * 128), lambda g, i: (i, g)), col]
    out_specs = [pl.BlockSpec((t, PAIRS * qw), lambda g, i: (i, g)), col]
    out_shape = [SDS((lp, 4 * qw), F32), SDS((HEADS, lp, 1), F32)]
    if mode == "fox":
        in_specs += [pl.BlockSpec((nh, 1, lp), lambda g, i: (g, 0, 0))]
        out_specs.append(col)
        out_shape.append(SDS((HEADS, lp, 1), F32))
    if mode == "swa":
        in_specs += [pl.BlockSpec(memory_space=pltpu.SMEM)] * 2
        out_specs.append(pl.BlockSpec((1, 8, 128), lambda g, i: (g, 0, 0)))
        out_shape.append(SDS((4 // PAIRS, 8, 128), F32))
    return pl.pallas_call(
        body, name=name, grid=(4 // PAIRS, nq), in_specs=in_specs, out_specs=out_specs, out_shape=out_shape,
        compiler_params=_params(("parallel", "arbitrary")),
    )(q, k, v, o, do, lse, *extra)


def _att_dkv(att, q, k, v, do, lse_row, delta_row, extra, name):
    lp = q.shape[0]
    t = TILE_ATT
    nq = lp // t
    qw = att.qw
    mode = att.mode
    nh = 2 * PAIRS

    def body(*refs):
        q_ref, k_ref, v_ref, do_ref, lse_ref, delta_ref = refs[0:6]
        n_out = 3 if mode == "fox" else 2
        outs = refs[len(refs) - n_out:]
        dk_ref, dv_ref = outs[0:2]
        g, kj = pl.program_id(0), pl.program_id(1)
        lo = lax.broadcasted_iota(jnp.int32, (1, 128), 1) < 64
        k_all, v_all = k_ref[...], v_ref[...]
        k_heads, v_heads = [], []
        for pp in range(PAIRS):
            k_heads += att.resident(_pair_cols(att, k_all, pp, qw), lo, True)
            v_heads += _halves(_pair_cols(att, v_all, pp, 128), lo)
        kpos = kj * t + lax.broadcasted_iota(jnp.int32, (t, 1), 0)

        def step(qi, carry, masked):
            qs = pl.multiple_of(qi * t, t)
            qc, doc = q_ref[pl.ds(qs, t), :], do_ref[pl.ds(qs, t), :]
            qpos = qi * t + lax.broadcasted_iota(jnp.int32, (1, t), 1)
            out = []
            for h in range(nh):
                pp = h // 2
                dk_acc, dv_acc, dc_acc = carry[3 * h:3 * h + 3]
                q_h = att.moving(_pair_cols(att, qc, pp, qw))[h % 2]
                do_h = _pair_cols(att, doc, pp, 128)
                decay = refs[6][h] if mode == "fox" else None
                slope = refs[6][nh * g + h] if mode == "swa" else None
                st = att.logits(k_heads[h], q_h, qpos, kpos, decay, slope, masked)
                pt = jnp.exp(st - lse_ref[h, :, pl.ds(qs, t)])
                dst = pt * (_dot_nt(v_heads[h], do_h) - delta_ref[h, :, pl.ds(qs, t)])
                dv_acc = dv_acc + _dot(pt.astype(BF16), do_h)
                dk_acc = dk_acc + _dot(dst.astype(BF16), q_h)
                if mode == "fox":
                    dc_acc = dc_acc - jnp.sum(dst, axis=-1, keepdims=True)
                out += [dk_acc, dv_acc, dc_acc]
            return tuple(out)

        init = (jnp.zeros((t, 128), F32), jnp.zeros((t, 128), F32), jnp.zeros((t, 1), F32)) * nh
        if mode == "swa":
            last = jnp.where(kj == 0, nq, jnp.minimum(kj + 2, nq))
            carry = lax.fori_loop(kj, last, lambda qi, c: step(qi, c, True), init)
        else:
            carry = step(kj, init, True)
            carry = lax.fori_loop(kj + 1, nq, lambda qi, c: step(qi, c, False), carry)
        dk, dv = [], []
        for pp in range(PAIRS):
            dka, dva, dca, dkb, dvb, dcb = carry[6 * pp:6 * pp + 6]
            dk += [dka, dkb] if att.wide else [jnp.where(lo, dka, dkb)]
            dv.append(jnp.where(lo, dva, dvb))
            if mode == "fox":
                outs[2][2 * pp] = dca
                outs[2][2 * pp + 1] = dcb
        dk_ref[...] = jnp.concatenate(dk, axis=1) * att.scale
        dv_ref[...] = jnp.concatenate(dv, axis=1)

    rowv = pl.BlockSpec((nh, 1, lp), lambda g, j: (g, 0, 0))
    col = pl.BlockSpec((nh, t, 1), lambda g, j: (g, j, 0))
    in_specs = [pl.BlockSpec((lp, PAIRS * qw), lambda g, j: (0, g)), *_kv_specs(att, lp, t),
                pl.BlockSpec((lp, PAIRS * 128), lambda g, j: (0, g)), rowv, rowv]
    out_specs = [pl.BlockSpec((t, PAIRS * qw), lambda g, j: (j, g)), pl.BlockSpec((t, PAIRS * 128), lambda g, j: (j, g))]
    out_shape = [SDS((lp, 4 * qw), F32), SDS((lp, 512), F32)]
    if mode == "fox":
        in_specs += [col]
        out_specs.append(col)
        out_shape.append(SDS((HEADS, lp, 1), F32))
    if mode == "swa":
        in_specs += [pl.BlockSpec(memory_space=pltpu.SMEM)]
    return pl.pallas_call(
        body, name=name, grid=(4 // PAIRS, nq), in_specs=in_specs, out_specs=out_specs, out_shape=out_shape,
        compiler_params=_params(("parallel", "arbitrary")),
    )(q, k, v, do, lse_row, delta_row, *extra)


def _post_fwd(h, proj, outs, wb, wo, name):
    lp, d = h.shape
    tb = TILE_POST
    row = lambda w: pl.BlockSpec((tb, w), lambda i: (i, 0))

    def body(h_ref, g0, g1, g2, oa, ob, oc, wb_ref, wo_ref, o_ref):
        merged = jnp.zeros((tb, d), F32)
        for n, (g_ref, br) in enumerate(((g0, oa), (g1, ob), (g2, oc))):
            merged = merged + jax.nn.sigmoid(g_ref[...]) * _dot(br[...], wb_ref[n])
        o_ref[...] = h_ref[...] + _dot(merged.astype(BF16), wo_ref[...])

    gate = lambda n: pl.BlockSpec((tb, d), lambda i, n=n: (i, n))
    return pl.pallas_call(
        body, name=name, grid=(lp // tb,),
        in_specs=[row(d), gate(0), gate(1), gate(2), row(512), row(512), row(512),
                  pl.BlockSpec((3, 512, d), lambda i: (0, 0, 0)), pl.BlockSpec((d, d), lambda i: (0, 0))],
        out_specs=row(d), out_shape=SDS((lp, d), F32),
        compiler_params=_params(("parallel",)),
    )(h, proj, proj, proj, *outs, wb, wo)


def _post_bwd(dh, proj, outs, wb, wo, name):
    lp, d = dh.shape
    tb = TILE_POST
    row = lambda w: pl.BlockSpec((tb, w), lambda i: (i, 0))

    def body(dh_ref, g0, g1, g2, oa, ob, oc, wb_ref, wo_ref, dg_ref, doa, dob, doc, dwb_ref, dwo_ref):
        @pl.when(pl.program_id(0) == 0)
        def _():
            dwb_ref[...] = jnp.zeros_like(dwb_ref)
            dwo_ref[...] = jnp.zeros_like(dwo_ref)

        dhb = dh_ref[...].astype(BF16)
        dm = _dot_nt(dhb, wo_ref[...])
        merged = jnp.zeros((tb, d), F32)
        for n, (g_ref, br, do_ref) in enumerate(((g0, oa, doa), (g1, ob, dob), (g2, oc, doc))):
            gate = jax.nn.sigmoid(g_ref[...])
            o_n = br[...]
            y = _dot(o_n, wb_ref[n])
            merged = merged + gate * y
            dy = (dm * gate).astype(BF16)
            dg_ref[:, n * d:(n + 1) * d] = (dm * y * gate * (1.0 - gate)).astype(BF16)
            do_ref[...] = _dot_nt(dy, wb_ref[n]).astype(BF16)
            dwb_ref[n] += _dot_tn(o_n, dy)
        dwo_ref[...] += _dot_tn(merged.astype(BF16), dhb)

    gate = lambda n: pl.BlockSpec((tb, d), lambda i, n=n: (i, n))
    wb_spec = pl.BlockSpec((3, 512, d), lambda i: (0, 0, 0))
    wo_spec = pl.BlockSpec((d, d), lambda i: (0, 0))
    return pl.pallas_call(
        body, name=name, grid=(lp // tb,),
        in_specs=[row(d), gate(0), gate(1), gate(2), row(512), row(512), row(512), wb_spec, wo_spec],
        out_specs=[row(GATES_W), row(512), row(512), row(512), wb_spec, wo_spec],
        out_shape=[SDS((lp, GATES_W), BF16)] + [SDS((lp, 512), BF16)] * 3 + [SDS((3, 512, d), F32), SDS((d, d), F32)],
        compiler_params=_params(("arbitrary",)),
    )(dh, proj, proj, proj, *outs, wb, wo)


def _shift_down(x, halo, n, first):
    rows = lax.broadcasted_iota(jnp.int32, x.shape, 0)
    edge = jnp.concatenate([pltpu.roll(halo, n, 0), jnp.zeros((x.shape[0] - 8, x.shape[1]), F32)], axis=0)
    edge = jnp.where(first, 0.0, edge)
    return jnp.where(rows < n, edge, pltpu.roll(x, n, 0))


def _shift_up(x, halo, n, last):
    tb = x.shape[0]
    rows = lax.broadcasted_iota(jnp.int32, x.shape, 0)
    edge = jnp.concatenate([jnp.zeros((tb - 8, x.shape[1]), F32), pltpu.roll(halo, 8 - n, 0)], axis=0)
    edge = jnp.where(last, 0.0, edge)
    return jnp.where(rows >= tb - n, edge, pltpu.roll(x, tb - n, 0))


def _conv(u, halo, w_ref, b_ref, first):
    taps = (_shift_down(u, halo, 2, first), _shift_down(u, halo, 1, first), u)
    c = b_ref[...] + w_ref[0:1, :] * taps[0] + w_ref[1:2, :] * taps[1] + w_ref[2:3, :] * taps[2]
    return c, taps


def _ffn_specs(tb, f):
    hb = tb // 8
    cur = lambda c: pl.BlockSpec((tb, f), lambda i, c=c: (i, c))
    prev = lambda c: pl.BlockSpec((8, f), lambda i, c=c: (jnp.maximum(i * hb - 1, 0), c))
    vec = lambda r, c: pl.BlockSpec((r, f), lambda i, c=c: (0, c))
    return cur, prev, vec


def _ffn_act_fwd(u, cw, cb, name):
    lp = u.shape[0]
    f = D_FF
    tb = TILE_ROW
    cur, prev, vec = _ffn_specs(tb, f)

    def body(ug, uv, hg, hv, wg, wv, bg, bv, o_ref):
        first = pl.program_id(0) == 0
        cg, _ = _conv(ug[...], hg[...], wg, bg, first)
        cv, _ = _conv(uv[...], hv[...], wv, bv, first)
        o_ref[...] = (cg * jax.nn.sigmoid(cg) * cv).astype(BF16)

    return pl.pallas_call(
        body, name=name, grid=(lp // tb,),
        in_specs=[cur(0), cur(1), prev(0), prev(1), vec(8, 0), vec(8, 1), vec(1, 0), vec(1, 1)],
        out_specs=pl.BlockSpec((tb, f), lambda i: (i, 0)), out_shape=SDS((lp, f), BF16),
        compiler_params=_params(("parallel",)),
    )(u, u, u, u, cw, cw, cb, cb)


def _ffn_act_bwd_conv(u, dact, cw, cb, name):
    lp = u.shape[0]
    f = D_FF
    tb = TILE_ROW
    cur, prev, vec = _ffn_specs(tb, f)

    def body(ug, uv, hg, hv, wg, wv, bg, bv, da_ref, dcg_ref, dcv_ref, dwg, dwv, dbg, dbv):
        first = pl.program_id(0) == 0

        @pl.when(first)
        def _():
            for r in (dwg, dwv, dbg, dbv):
                r[...] = jnp.zeros_like(r)

        cg, tg = _conv(ug[...], hg[...], wg, bg, first)
        cv, tv = _conv(uv[...], hv[...], wv, bv, first)
        da = da_ref[...]
        sg = jax.nn.sigmoid(cg)
        dcg = da * cv * sg * (1.0 + cg * (1.0 - sg))
        dcv = da * cg * sg
        dcg_ref[...] = dcg
        dcv_ref[...] = dcv
        for dc, taps, dw, db in ((dcg, tg, dwg, dbg), (dcv, tv, dwv, dbv)):
            for n in range(3):
                dw[n:n + 1, :] += jnp.sum(dc * taps[n], axis=0, keepdims=True)
            db[0:1, :] += jnp.sum(dc, axis=0, keepdims=True)

    row = pl.BlockSpec((tb, f), lambda i: (i, 0))
    acc = pl.BlockSpec((8, f), lambda i: (0, 0))
    return pl.pallas_call(
        body, name=name, grid=(lp // tb,),
        in_specs=[cur(0), cur(1), prev(0), prev(1), vec(8, 0), vec(8, 1), vec(1, 0), vec(1, 1), row],
        out_specs=[row, row, acc, acc, acc, acc],
        out_shape=[SDS((lp, f), F32)] * 2 + [SDS((8, f), F32)] * 4,
        compiler_params=_params(("arbitrary",)),
    )(u, u, u, u, cw, cw, cb, cb, dact)


def _ffn_act_bwd_in(dcg, dcv, cw, name):
    lp = dcg.shape[0]
    f = D_FF
    tb = TILE_ROW
    nb = lp // tb
    hb = tb // 8
    cur = pl.BlockSpec((tb, f), lambda i: (i, 0))
    nxt = pl.BlockSpec((8, f), lambda i: (jnp.minimum((i + 1) * hb, nb * hb - 1), 0))
    vec = lambda c: pl.BlockSpec((8, f), lambda i, c=c: (0, c))

    def body(dg, dv, ng, nv, wg, wv, og, ov):
        last = pl.program_id(0) == nb - 1
        for dc_ref, n_ref, w_ref, o_ref in ((dg, ng, wg, og), (dv, nv, wv, ov)):
            dc, halo = dc_ref[...], n_ref[...]
            du = (w_ref[2:3, :] * dc + w_ref[1:2, :] * _shift_up(dc, halo, 1, last)
                  + w_ref[0:1, :] * _shift_up(dc, halo, 2, last))
            o_ref[...] = du.astype(BF16)

    return pl.pallas_call(
        body, name=name, grid=(nb,),
        in_specs=[cur, cur, nxt, nxt, vec(0), vec(1)],
        out_specs=[cur, cur],
        out_shape=[SDS((lp, f), BF16)] * 2,
        compiler_params=_params(("parallel",)),
    )(dcg, dcv, dcg, dcv, cw, cw)


def _loss_head(y, target, n_real, name):
    lp, d = y.shape
    tb = TILE_MM

    def body(y_ref, t_ref, dy_ref, loss_ref):
        i = pl.program_id(0)

        @pl.when(i == 0)
        def _():
            loss_ref[...] = jnp.zeros_like(loss_ref)

        rows = i * tb + lax.broadcasted_iota(jnp.int32, (tb, 1), 0)
        real = (rows >= N_META) & (rows < N_META + n_real)
        diff = jnp.where(real, y_ref[...] - t_ref[...], 0.0)
        dy_ref[...] = diff * (1.0 / d)
        loss_ref[...] += (0.5 / d) * jnp.sum(diff * diff).reshape(1, 1)

    row = pl.BlockSpec((tb, d), lambda i: (i, 0))
    return pl.pallas_call(
        body, name=name, grid=(lp // tb,), in_specs=[row, row],
        out_specs=[row, pl.BlockSpec((1, 1), lambda i: (0, 0))],
        out_shape=[SDS((lp, d), F32), SDS((1, 1), F32)],
        compiler_params=_params(("arbitrary",)),
    )(y, target)


def _pad_lanes(v, width, at=0):
    return jnp.pad(v.astype(F32), (at, width - at - v.shape[0]))[None, :]


def _layer_params(w, l):
    b = lambda a: a.astype(BF16)
    win = w["w_in"][l]
    fq, fk, fv, ff, cq, ckv, kr, sq, sk, sv, gates = jnp.split(
        win, [512, 1024, 1536, 1544, 1800, 1928, 1960, 2472, 2600, 2728], axis=1)
    misc = jnp.concatenate([kr, ff, jnp.zeros((D_MODEL, 88), win.dtype)], axis=1)
    w_in = b(jnp.concatenate([gates, fq, fk, fv, sq, sk, sv, cq, ckv, misc], axis=1))
    wq = jnp.pad(w["mla_w_q_up"][l].reshape(256, HEADS, 96), ((0, 0), (0, 0), (0, 32))).reshape(256, 1024)
    wkv = w["mla_w_kv_up"][l].reshape(128, HEADS, 128)
    wkk = jnp.pad(wkv[:, :, :64], ((0, 0), (0, 0), (0, 64))).reshape(128, 1024)
    wkvv = wkv[:, :, 64:].reshape(128, 512)
    tile = lambda g, n: jnp.tile(g.astype(F32), n)[None, :]
    prm = [tile(w["fox_q_g"][l], 8), tile(w["fox_k_g"][l], 8), tile(w["swa_q_g"][l], 8), tile(w["swa_k_g"][l], 2),
           _pad_lanes(w["fox_forget_b"][l], 128, FF_LANE), w["mla_q_a_g"][l][None, :], w["mla_kv_a_g"][l][None, :],
           tile(jnp.pad(w["mla_q_g"][l], (0, 32)), 8), tile(jnp.pad(w["mla_k_g"][l], (0, 32)), 8),
           wq.astype(F32), wkk.astype(F32), wkvv.astype(F32)]
    cw = jnp.pad(w["ffn_conv_w"][l].astype(F32), ((0, 5), (0, 0)))
    return dict(
        g1=w["norm1_g"][l][None, :], w_in=w_in, prm=prm, sinks=w["swa_sinks"][l].astype(F32),
        wb=b(w["w_branch"][l]), wo=b(w["w_o"][l]), g2=w["norm2_g"][l][None, :], w_up=b(w["ffn_w_up"][l]),
        cw=cw, cb=w["ffn_conv_b"][l][None, :].astype(F32), w_down=b(w["ffn_w_down"][l]))


def _cols(c):
    ct = c[:, FF_LANE:FF_LANE + HEADS].T
    return ct[:, :, None], ct[:, None, :]


def _rows(col):
    return jnp.swapaxes(col, 1, 2)


def _from_cols(col):
    return jnp.pad(col[:, :, 0].T, ((0, 0), (FF_LANE, 128 - FF_LANE - HEADS)))


def _layer_fwd(h, lw, consts, cos, sin, slopes, l):
    tag = f"l{l}_"
    xn, proj = _norm_matmul(h, lw["g1"], lw["w_in"], IN_W // 2, tag + "in_proj")
    fq, fk, fv, mq, mk, mv, sq, skd, svd, ls = _prep_fwd(proj, lw["prm"], consts, cos, sin, tag + "prep")
    c = _cumsum([ls], False, tag + "decay_cumsum")
    c_col, c_row = _cols(c)
    oa, lse_a = _att_fwd(_Att("fox"), fq, fk, fv, (c_row,), tag + "fox_fwd")
    ob, lse_b = _att_fwd(_Att("mla"), mq, mk, mv, (), tag + "mla_fwd")
    oc, lse_c = _att_fwd(_Att("swa"), sq, skd, svd, (lw["sinks"], slopes), tag + "swa_fwd")
    h2 = _post_fwd(h, proj, (oa, ob, oc), lw["wb"], lw["wo"], tag + "merge")
    xn2, u = _norm_matmul(h2, lw["g2"], lw["w_up"], D_FF, tag + "ffn_up")
    act = _ffn_act_fwd(u, lw["cw"], lw["cb"], tag + "ffn_act")
    h3 = _matmul_residual(act, lw["w_down"], h2, tag + "ffn_down")
    saved = dict(h=h, xn=xn, proj=proj, q=(fq, mq, sq), k=(fk, mk, skd), v=(fv, mv, svd), c=(c_col, c_row),
                 o=(oa, ob, oc), lse=(lse_a, lse_b, lse_c), h2=h2, xn2=xn2, u=u, act=act)
    return h3, saved


def _layer_bwd_ffn(dh3, lw, sv, l):
    tag = f"l{l}_"
    f = D_FF
    dact = _matmul_nt(dh3, lw["w_down"], f, tag + "ffn_down_dx")
    dw_down = _matmul_tn(sv["act"], dh3, D_MODEL, tag + "ffn_down_dw")
    dcg, dcv, dwg, dwv, dbg, dbv = _ffn_act_bwd_conv(sv["u"], dact, lw["cw"], lw["cb"], tag + "ffn_act_dc")
    dug, duv = _ffn_act_bwd_in(dcg, dcv, lw["cw"], tag + "ffn_act_du")
    du = jnp.concatenate([dug, duv], axis=1)
    dw_up = _matmul_tn(sv["xn2"], du, f, tag + "ffn_up_dw")
    tb = TILE_MM
    half = lambda c: pl.BlockSpec((tb, f), lambda i, c=c: (i, c))
    whalf = lambda c: pl.BlockSpec((D_MODEL, f), lambda i, c=c: (0, c))
    dh2, dg2 = _norm_matmul_bwd(du, lw["w_up"], du, lw["w_up"], sv["h2"], lw["g2"], dh3,
                                (half(0), whalf(0), half(1), whalf(1)), tag + "ffn_up_dx")
    g = dict(norm2_g=dg2[0], ffn_w_up=dw_up, ffn_conv_w=jnp.concatenate([dwg[0:3], dwv[0:3]], axis=1),
             ffn_conv_b=jnp.concatenate([dbg[0], dbv[0]]), ffn_w_down=dw_down)
    return dh2, g


def _layer_bwd_mix(dh2, lw, sv, consts, folds, cos, sin, slopes, l):
    tag = f"l{l}_"
    tb = TILE_MM
    dgates, doa, dob, doc, dwb, dwo = _post_bwd(dh2, sv["proj"], sv["o"], lw["wb"], lw["wo"], tag + "merge_bwd")
    c_col, c_row = sv["c"]
    extras = ((c_row,), (), (lw["sinks"], slopes))
    extras_kv = ((c_col,), (), (slopes,))
    grads = []
    for n, (mode, do) in enumerate((("fox", doa), ("mla", dob), ("swa", doc))):
        att = _Att(mode)
        q, k, v = sv["q"][n], sv["k"][n], sv["v"][n]
        res = _att_dq(att, q, k, v, sv["o"][n], do, sv["lse"][n], extras[n], tag + mode + "_dq")
        dq, delta = res[0], res[1]
        res_kv = _att_dkv(att, q, k, v, do, _rows(sv["lse"][n]), _rows(delta), extras_kv[n], tag + mode + "_dkv")
        grads.append((dq, res_kv[0], res_kv[1], res[2:], res_kv[2:]))
    (dfq, dfk, dfv, (dcq,), (dck,)), (dmq, dmk, dmv, _, _), (dsq, dskp, dsvp, (dsink,), _) = grads
    dls = _cumsum([_from_cols(dcq), _from_cols(dck)], True, tag + "decay_cumsum_bwd")
    res = _prep_bwd(sv["proj"], lw["prm"], consts, cos, sin,
                    (dfq, dfk, dfv, dmq, dmk, dmv, dsq, dskp, dsvp, dls), folds, tag + "prep_bwd")
    dother, pg = res[0], res[1:]
    dw_g = _matmul_tn(sv["xn"], dgates, GATES_W, tag + "in_proj_dw_gates")
    dw_o = _matmul_tn(sv["xn"], dother, OTHER_W, tag + "in_proj_dw_other")
    full = lambda w: pl.BlockSpec((tb, w), lambda i: (i, 0))
    wfull = lambda w: pl.BlockSpec((D_MODEL, w), lambda i: (0, 0))
    dh, dg1 = _norm_matmul_bwd(dgates, lw["w_in"][:, :GATES_W], dother, lw["w_in"][:, GATES_W:], sv["h"], lw["g1"], dh2,
                               (full(GATES_W), wfull(GATES_W), full(OTHER_W), wfull(OTHER_W)), tag + "in_proj_dx")
    d_in = jnp.concatenate([
        dw_o[:, O_FQ:O_FV + 512], dw_o[:, O_MISC + FF_LANE:O_MISC + FF_LANE + 8], dw_o[:, O_CQ:O_CQ + 256],
        dw_o[:, O_CKV:O_CKV + 128], dw_o[:, O_MISC:O_MISC + 32], dw_o[:, O_SQ:O_SQ + 512], dw_o[:, O_SK:O_SK + 128],
        dw_o[:, O_SV:O_SV + 128], dw_g], axis=1)
    d_wq = pg[9].reshape(256, HEADS, 128)[:, :, :96].reshape(256, 768)
    d_wkv = jnp.concatenate([pg[10].reshape(128, HEADS, 128)[:, :, :64], pg[11].reshape(128, HEADS, 64)],
                            axis=2).reshape(128, 1024)
    g = dict(
        norm1_g=dg1[0], w_in=d_in, fox_forget_b=pg[4][0, FF_LANE:FF_LANE + 8], fox_q_g=pg[0][0, :64],
        fox_k_g=pg[1][0, :64], mla_q_a_g=pg[5][0], mla_w_q_up=d_wq, mla_kv_a_g=pg[6][0], mla_w_kv_up=d_wkv,
        mla_q_g=pg[7][0, :96], mla_k_g=pg[8][0, :96], swa_q_g=pg[2][0, :64], swa_k_g=pg[3][0, :64],
        swa_sinks=dsink[:, 0, 0:2 * PAIRS].reshape(HEADS), w_branch=dwb, w_o=dwo)
    return dh, g


def _local_step(x, target, w, hook=None):
    seq = x.shape[0]
    length = N_META + seq
    lp = -(-length // ROW_ALIGN) * ROW_ALIGN
    pad = lp - length
    h = jnp.concatenate([w["meta_tokens"].astype(F32), x, jnp.zeros((pad, D_MODEL), F32)], axis=0)
    tgt = jnp.pad(target, ((N_META, pad), (0, 0)))
    consts = _consts()
    folds = (_fold_matrix(512, 64), _fold_matrix(1024, 128))
    cos, sin = _rope_tables(lp)
    slopes = jnp.asarray(2.0 ** (-8.0 * np.arange(1, HEADS + 1, dtype=np.float32) / HEADS), F32)
    lws = [_layer_params(w, l) for l in range(DEPTH)]
    saved = []
    for l in range(DEPTH):
        h, sv = _layer_fwd(h, lws[l], consts, cos, sin, slopes, l)
        saved.append(sv)
    dh, loss = _loss_head(h, tgt, seq, "loss_head")
    grads = [None] * DEPTH
    for l in reversed(range(DEPTH)):
        dh, g_ffn = _layer_bwd_ffn(dh, lws[l], saved[l], l)
        tick = hook(l, "ffn", g_ffn) if hook else None
        if tick is not None:
            lws[l]["sinks"] = lws[l]["sinks"] + tick
        dh, g_mix = _layer_bwd_mix(dh, lws[l], saved[l], consts, folds, cos, sin, slopes, l)
        grads[l] = {**g_ffn, **g_mix}
        tick = hook(l, "mix", grads[l]) if hook else None
        if tick is not None and l > 0:
            lws[l - 1]["cw"] = lws[l - 1]["cw"] + tick
    return loss, dh[N_META:length], dh[:N_META], grads


def _place():
    return lax.axis_index("x"), lax.axis_index("y"), lax.axis_index("c")


def _flip(pos, k):
    x, y, c = pos
    return (1 - x if k & 4 else x, 1 - y if k & 2 else y, 1 - c if k & 1 else c)


def _index(pos):
    return 4 * pos[0] + 2 * pos[1] + pos[2]


def _gather(tensors, name):
    n_t = len(tensors)

    def body(*refs):
        ins, outs = refs[:n_t], refs[n_t:2 * n_t]
        send_sems, recv_sems, local_sems = refs[2 * n_t:]
        x, y, c = _place()
        me, sibling = (x, y, c), (x, y, 1 - c)
        chips = [(1 - x, y), (x, 1 - y), (1 - x, 1 - y)]

        def copy(t, k, block, to, src=None):
            dst = outs[t].at[_index(block)]
            return pltpu.make_async_remote_copy(
                src_ref=dst if src is None else src, dst_ref=dst, send_sem=send_sems.at[t, k],
                recv_sem=recv_sems.at[t, k], device_id=to, device_id_type=pl.DeviceIdType.MESH)

        local, sent = [], []
        for t in range(n_t):
            local.append(pltpu.make_async_copy(ins[t], outs[t].at[_index(me)], local_sems.at[t]))
            local[-1].start()
            sent.append(copy(t, 0, me, sibling, src=ins[t]))
            sent += [copy(t, 1 + j, me, (*chip, c), src=ins[t]) for j, chip in enumerate(chips)]
        for cp in sent:
            cp.start()
        for j, chip in enumerate(chips):
            for t in range(n_t):
                copy(t, 1 + j, (*chip, c), me).wait_recv()
                sent.append(copy(t, 4 + j, (*chip, c), sibling))
                sent[-1].start()
        for t in range(n_t):
            copy(t, 0, sibling, me).wait_recv()
            for j, chip in enumerate(chips):
                copy(t, 4 + j, (*chip, 1 - c), me).wait_recv()
        for cp in sent:
            cp.wait_send()
        for cp in local:
            cp.wait()

    any_spec = pl.BlockSpec(memory_space=pl.ANY)
    return pl.pallas_call(
        body, name=name, in_specs=[any_spec] * n_t, out_specs=[any_spec] * n_t,
        out_shape=[SDS((N_DEV,) + a.shape, a.dtype) for a in tensors],
        scratch_shapes=[pltpu.SemaphoreType.DMA((n_t, N_DEV - 1)), pltpu.SemaphoreType.DMA((n_t, N_DEV - 1)),
                        pltpu.SemaphoreType.DMA((n_t,))],
    )(*tensors)


def _all_to_all(tensors, name):
    n_t = len(tensors)

    def body(*refs):
        ins, outs = refs[:n_t], refs[n_t:2 * n_t]
        send_sems, recv_sems, local_sems = refs[2 * n_t:]
        me = _place()
        mine = _index(me)
        local, sent = [], []
        for t in range(n_t):
            local.append(pltpu.make_async_copy(ins[t].at[mine], outs[t].at[mine], local_sems.at[t]))
            local[-1].start()
            for k in range(1, N_DEV):
                peer = _flip(me, k)
                sent.append(pltpu.make_async_remote_copy(
                    src_ref=ins[t].at[_index(peer)], dst_ref=outs[t].at[mine], send_sem=send_sems.at[t, k - 1],
                    recv_sem=recv_sems.at[t, k - 1], device_id=peer, device_id_type=pl.DeviceIdType.MESH))
                sent[-1].start()
        for cp in sent:
            cp.wait_send()
        for t in range(n_t):
            for k in range(1, N_DEV):
                peer = _flip(me, k)
                pltpu.make_async_remote_copy(
                    src_ref=ins[t].at[mine], dst_ref=outs[t].at[_index(peer)], send_sem=send_sems.at[t, k - 1],
                    recv_sem=recv_sems.at[t, k - 1], device_id=peer, device_id_type=pl.DeviceIdType.MESH).wait_recv()
        for cp in local:
            cp.wait()

    any_spec = pl.BlockSpec(memory_space=pl.ANY)
    return pl.pallas_call(
        body, name=name, in_specs=[any_spec] * n_t, out_specs=[any_spec] * n_t,
        out_shape=[SDS(a.shape, a.dtype) for a in tensors],
        scratch_shapes=[pltpu.SemaphoreType.DMA((n_t, N_DEV - 1)), pltpu.SemaphoreType.DMA((n_t, N_DEV - 1)),
                        pltpu.SemaphoreType.DMA((n_t,))],
    )(*tensors)


def _exchange_start(tensors, name):
    n_t = len(tensors)

    def body(*refs):
        ins, lands = refs[:n_t], refs[n_t:2 * n_t]
        send_sems, recv_sems = refs[2 * n_t:2 * n_t + 2]
        token = refs[-1]
        me = _place()
        mine = _index(me)
        for t in range(n_t):
            for k in range(1, N_DEV):
                peer = _flip(me, k)
                pltpu.make_async_remote_copy(
                    src_ref=ins[t].at[_index(peer)], dst_ref=lands[t].at[mine], send_sem=send_sems,
                    recv_sem=recv_sems, device_id=peer, device_id_type=pl.DeviceIdType.MESH).start()
        token[...] = jnp.zeros_like(token)

    hbm = pl.BlockSpec(memory_space=pltpu.HBM)
    sem = pl.BlockSpec(memory_space=pltpu.SEMAPHORE)
    sems = pltpu.SemaphoreType.DMA(())
    bufs = [pltpu.HBM(a.shape, a.dtype) for a in tensors]
    outs = pl.pallas_call(
        body, name=name, in_specs=[hbm] * (2 * n_t),
        out_specs=[sem, sem] + [hbm] * (2 * n_t) + [pl.BlockSpec(memory_space=pltpu.VMEM)],
        out_shape=[sems, sems] + bufs + bufs + [SDS((8, 128), F32)],
        input_output_aliases={i: 2 + i for i in range(2 * n_t)},
        compiler_params=pltpu.CompilerParams(has_side_effects=pltpu.SideEffectType.DATAFLOW_SIDE_EFFECTING),
    )(*[pltpu.with_memory_space_constraint(a, pltpu.HBM) for a in tensors],
      *[pltpu.with_memory_space_constraint(lax.empty(a.shape, a.dtype), pltpu.HBM) for a in tensors])
    return outs[:-1], outs[-1][0, 0]


def _exchange_wait(state, after, name):
    n_t = (len(state) - 2) // 2

    def body(*refs):
        send_sems, recv_sems = refs[0:2]
        ins, lands = refs[2:2 + n_t], refs[2 + n_t:2 + 2 * n_t]
        me = _place()
        mine = _index(me)
        for t in range(n_t):
            for k in range(1, N_DEV):
                peer = _flip(me, k)
                copy = pltpu.make_async_remote_copy(
                    src_ref=ins[t].at[_index(peer)], dst_ref=lands[t].at[_index(peer)], send_sem=send_sems,
                    recv_sem=recv_sems, device_id=peer, device_id_type=pl.DeviceIdType.MESH)
                copy.wait_send()
                copy.wait_recv()

    hbm = pl.BlockSpec(memory_space=pltpu.HBM)
    sem = pl.BlockSpec(memory_space=pltpu.SEMAPHORE)
    bufs = [pltpu.HBM(a.shape, a.dtype) for a in state[2:]]
    outs = pl.pallas_call(
        body, name=name, in_specs=[sem, sem] + [hbm] * (2 * n_t) + [pl.BlockSpec(memory_space=pl.ANY)],
        out_specs=[hbm] * (2 * n_t), out_shape=bufs,
        input_output_aliases={2 + i: i for i in range(2 * n_t)},
        compiler_params=pltpu.CompilerParams(has_side_effects=pltpu.SideEffectType.DATAFLOW_SIDE_EFFECTING),
    )(*state, after)
    return outs[n_t:]


def _fill_own(sends, lands, name):
    n_t = len(sends)

    def body(*refs):
        ins, outs, sems = refs[:n_t], refs[2 * n_t:3 * n_t], refs[3 * n_t]
        mine = _index(_place())
        copies = [pltpu.make_async_copy(ins[t].at[mine], outs[t].at[mine], sems.at[t]) for t in range(n_t)]
        for cp in copies:
            cp.start()
        for cp in copies:
            cp.wait()

    any_spec = pl.BlockSpec(memory_space=pl.ANY)
    return pl.pallas_call(
        body, name=name, in_specs=[any_spec] * (2 * n_t), out_specs=[any_spec] * n_t,
        out_shape=[SDS(a.shape, a.dtype) for a in lands],
        input_output_aliases={n_t + t: t for t in range(n_t)},
        scratch_shapes=[pltpu.SemaphoreType.DMA((n_t,))],
    )(*sends, *lands)


def _sum_slots(parts, name):
    n, rows, w = parts.shape
    tb = 8

    def body(p_ref, o_ref):
        acc = p_ref[0].astype(F32)
        for s in range(1, n):
            acc = acc + p_ref[s].astype(F32)
        o_ref[...] = acc

    return pl.pallas_call(
        body, name=name, grid=(rows // tb,),
        in_specs=[pl.BlockSpec((n, tb, w), lambda i: (0, i, 0))], out_specs=pl.BlockSpec((tb, w), lambda i: (i, 0)),
        out_shape=SDS((rows, w), F32), compiler_params=_params(("parallel",)),
    )(parts)


def _adamw(wt, m, v, parts, name):
    shape = wt.shape
    parts = parts if isinstance(parts, (list, tuple)) else [parts]
    n, w = parts[0].shape[0], shape[-1]
    rows = math.prod(shape[:-1])
    per = rows // len(parts)
    step = 16 if parts[0].dtype == BF16 else 8
    tb = max([t for t in range(step, 257, step) if per % t == 0] or [per])
    nb = per // tb
    c1 = 1.0 / (1.0 - ADAM_B1 ** ADAM_STEP)
    c2 = 1.0 / (1.0 - ADAM_B2 ** ADAM_STEP)
    state = [a.reshape(rows, w) for a in (wt, m, v)]
    outs = None
    for l, p in enumerate(parts):
        def body(w_ref, m_ref, v_ref, p_ref, *rest):
            g_out, d_out, m_out, v_out = rest[-4:]
            g = p_ref[0].astype(F32)
            for s in range(1, n):
                g = g + p_ref[s].astype(F32)
            m_new = ADAM_B1 * m_ref[...] + (1.0 - ADAM_B1) * g
            v_new = ADAM_B2 * v_ref[...] + (1.0 - ADAM_B2) * (g * g)
            g_out[...] = g
            m_out[...] = m_new
            v_out[...] = v_new
            d_out[...] = -ADAM_LR * ((m_new * c1) / (jnp.sqrt(v_new * c2) + ADAM_EPS) + ADAM_WD * w_ref[...])

        row = pl.BlockSpec((tb, w), lambda i, l=l: (l * nb + i, 0))
        prev = [] if outs is None else list(outs)
        outs = pl.pallas_call(
            body, name=f"{name}_{l}", grid=(nb,),
            in_specs=[row, row, row, pl.BlockSpec((n, tb, w), lambda i: (0, i, 0))] + [pl.BlockSpec(memory_space=pl.ANY)] * len(prev),
            out_specs=[row] * 4, out_shape=[SDS((rows, w), F32)] * 4,
            input_output_aliases={4 + k: k for k in range(len(prev))},
            compiler_params=_params(("parallel",)),
        )(*state, p.reshape(n, per, w), *prev)
    return [o.reshape(shape) for o in outs]


_BIG = [("w_in", 2), ("mla_w_q_up", 2), ("mla_w_kv_up", 2), ("w_branch", 3), ("w_o", 1), ("ffn_w_up", 2), ("ffn_w_down", 1)]
_SMALL_SHARDED = [("meta_tokens", 1), ("ffn_conv_w", 2)]
_REPLICATED = ["norm1_g", "fox_forget_b", "fox_q_g", "fox_k_g", "mla_q_a_g", "mla_kv_a_g", "mla_q_g", "mla_k_g",
               "swa_q_g", "swa_k_g", "swa_sinks", "norm2_g", "ffn_conv_b"]
_ORDER = ["meta_tokens", "norm1_g", "w_in", "fox_forget_b", "fox_q_g", "fox_k_g", "mla_q_a_g", "mla_w_q_up",
          "mla_kv_a_g", "mla_w_kv_up", "mla_q_g", "mla_k_g", "swa_q_g", "swa_k_g", "swa_sinks", "w_branch", "w_o",
          "norm2_g", "ffn_w_up", "ffn_conv_w", "ffn_conv_b", "ffn_w_down"]


def _flat_rows(vecs, dtype, row_mult):
    flat = jnp.concatenate([a.reshape(-1).astype(dtype) for a in vecs])
    rows = -(-flat.shape[0] // (1024 * row_mult)) * row_mult
    return jnp.pad(flat, (0, rows * 1024 - flat.shape[0])).reshape(rows, 1024)


def _unflatten(flat, shapes):
    out, off = [], 0
    for s in shapes:
        n = math.prod(s)
        out.append(flat[off:off + n].reshape(s))
        off += n
    return out


def _to_full(blocks, axis):
    moved = jnp.moveaxis(blocks, 0, axis)
    s = moved.shape
    return moved.reshape(s[:axis] + (s[axis] * s[axis + 1],) + s[axis + 2:])


def _to_blocks(full, axis):
    s = full.shape
    split = full.reshape(s[:axis] + (N_DEV, s[axis] // N_DEV) + s[axis + 1:])
    return jnp.moveaxis(split, axis, 0)


def kernel(x, meta_tokens, norm1_g, w_in, fox_forget_b, fox_q_g, fox_k_g, mla_q_a_g, mla_w_q_up, mla_kv_a_g, mla_w_kv_up, mla_q_g, mla_k_g, swa_q_g, swa_k_g, swa_sinks, w_branch, w_o, norm2_g, ffn_w_up, ffn_conv_w, ffn_conv_b, ffn_w_down, loss_target, m_meta_tokens, m_norm1_g, m_w_in, m_fox_forget_b, m_fox_q_g, m_fox_k_g, m_mla_q_a_g, m_mla_w_q_up, m_mla_kv_a_g, m_mla_w_kv_up, m_mla_q_g, m_mla_k_g, m_swa_q_g, m_swa_k_g, m_swa_sinks, m_w_branch, m_w_o, m_norm2_g, m_ffn_w_up, m_ffn_conv_w, m_ffn_conv_b, m_ffn_w_down, v_meta_tokens, v_norm1_g, v_w_in, v_fox_forget_b, v_fox_q_g, v_fox_k_g, v_mla_q_a_g, v_mla_w_q_up, v_mla_kv_a_g, v_mla_w_kv_up, v_mla_q_g, v_mla_k_g, v_swa_q_g, v_swa_k_g, v_swa_sinks, v_w_branch, v_w_o, v_norm2_g, v_ffn_w_up, v_ffn_conv_w, v_ffn_conv_b, v_ffn_w_down):
    wl = dict(zip(_ORDER, (meta_tokens, norm1_g, w_in, fox_forget_b, fox_q_g, fox_k_g, mla_q_a_g, mla_w_q_up,
                           mla_kv_a_g, mla_w_kv_up, mla_q_g, mla_k_g, swa_q_g, swa_k_g, swa_sinks, w_branch, w_o,
                           norm2_g, ffn_w_up, ffn_conv_w, ffn_conv_b, ffn_w_down)))
    ml = dict(zip(_ORDER, (m_meta_tokens, m_norm1_g, m_w_in, m_fox_forget_b, m_fox_q_g, m_fox_k_g, m_mla_q_a_g,
                           m_mla_w_q_up, m_mla_kv_a_g, m_mla_w_kv_up, m_mla_q_g, m_mla_k_g, m_swa_q_g, m_swa_k_g,
                           m_swa_sinks, m_w_branch, m_w_o, m_norm2_g, m_ffn_w_up, m_ffn_conv_w, m_ffn_conv_b,
                           m_ffn_w_down)))
    vl = dict(zip(_ORDER, (v_meta_tokens, v_norm1_g, v_w_in, v_fox_forget_b, v_fox_q_g, v_fox_k_g, v_mla_q_a_g,
                           v_mla_w_q_up, v_mla_kv_a_g, v_mla_w_kv_up, v_mla_q_g, v_mla_k_g, v_swa_q_g, v_swa_k_g,
                           v_swa_sinks, v_w_branch, v_w_o, v_norm2_g, v_ffn_w_up, v_ffn_conv_w, v_ffn_conv_b,
                           v_ffn_w_down)))
    small_sh = [n for n, _ in _SMALL_SHARDED]

    got = _gather([wl[n].astype(BF16) for n, _ in _BIG] + [wl[n] for n in small_sh], "gather_weights")
    full = {n: wl[n] for n in _REPLICATED}
    for (n, axis), blocks in zip(_BIG + _SMALL_SHARDED, got):
        full[n] = _to_full(blocks, axis)

    ffn_big = [("ffn_w_up", 2), ("ffn_w_down", 1)]
    mix_big = [e for e in _BIG if e not in ffn_big]
    blocks = lambda g, group: [_to_blocks(g[n], axis - 1).astype(BF16) for n, axis in group]
    early = {}

    def hook(l, stage, g):
        if l == DEPTH - 1 and stage == "mix":
            early["l1"] = (blocks(g, _BIG), _BIG)
        elif l == 0 and stage == "ffn":
            early["l0_ffn"] = (blocks(g, ffn_big), ffn_big)
        else:
            return None
        key = "l1" if l else "l0_ffn"
        state, tick = _exchange_start(early[key][0], "exchange_grads_" + key + "_start")
        early[key] += (state,)
        return tick

    loss, grad_x, grad_meta, grads = _local_step(x[0], loss_target[0], full, hook)
    idx = _index(_place())
    sends0 = blocks(grads[0], mix_big)
    landed = {(n, 0): p for (n, _), p in zip(mix_big, _all_to_all(sends0, "exchange_grads_l0_mix"))}
    after = landed[("w_in", 0)]
    for key, l in (("l1", 1), ("l0_ffn", 0)):
        sends, group, state = early[key]
        got = _exchange_wait(state, after, "exchange_grads_" + key + "_wait")
        got = _fill_own(sends, got, "exchange_grads_" + key + "_own")
        landed.update({(n, l): p for (n, _), p in zip(group, got)})
    big = [n for n, _ in _BIG]
    result = {kind: {} for kind in ("grad", "delta", "new_m", "new_v")}
    for n in big:
        outs = _adamw(wl[n], ml[n], vl[n], [landed[(n, l)] for l in range(DEPTH)], "adamw_" + n)
        for kind, val in zip(result, outs):
            result[kind][n] = val
    grads = {k: jnp.stack([grads[l][k] for l in range(DEPTH)]) for k in grads[0] if k not in big}
    grads["meta_tokens"] = grad_meta

    small_full = _REPLICATED + small_sh
    mine_small = _flat_rows([grads[n] for n in small_full] + [loss], F32, 8)
    total_small = _sum_slots(_gather([mine_small], "gather_small_grads")[0], "sum_small_grads").reshape(-1)
    pieces = _unflatten(total_small, [grads[n].shape for n in small_full] + [()])
    loss_total = pieces[-1]
    g_small = dict(zip(small_full, pieces[:-1]))
    for n, axis in _SMALL_SHARDED:
        size = wl[n].shape[axis]
        g_small[n] = lax.dynamic_slice_in_dim(g_small[n], idx * size, size, axis)
    flat = lambda d: _flat_rows([d[n] for n in small_full], F32, 8)
    small_out = _adamw(flat(wl), flat(ml), flat(vl), flat(g_small)[None], "adamw_small")
    for kind, fs in zip(result, small_out):
        result[kind].update(zip(small_full, _unflatten(fs.reshape(-1), [wl[n].shape for n in small_full])))
    outs = [loss_total, grad_x[None]]
    for kind in ("grad", "delta", "new_m", "new_v"):
        outs += [result[kind][n] for n in _ORDER]
    return tuple(outs)
```

```python
import functools
import math

import numpy as np
import jax
import jax.numpy as jnp
from jax import lax
from jax.experimental import pallas as pl
from jax.experimental.pallas import tpu as pltpu

F32, BF16 = jnp.float32, jnp.bfloat16
SDS = jax.ShapeDtypeStruct

D_MODEL = 1024
N_META = 16
EPS = 1e-6
WINDOW = 128
ROPE_THETA = 10000.0
HEADS = 8
D_FF = 2816
DEPTH = 2
N_DEV = 8
ADAM_LR, ADAM_B1, ADAM_B2, ADAM_EPS, ADAM_WD, ADAM_STEP = 0.001, 0.9, 0.999, 1e-08, 0.01, 10

ROW_ALIGN = 384
TILE_MM = 384
TILE_ROW = 128
TILE_ATT = 384
TILE_POST = 128
PAIRS = 2
VMEM_LIMIT = 56 * 1024 * 1024

GATES_W = 3072
OTHER_W = 2816
IN_W = GATES_W + OTHER_W
O_FQ, O_FK, O_FV, O_SQ, O_SK, O_SV, O_CQ, O_CKV, O_MISC = 0, 512, 1024, 1536, 2048, 2176, 2304, 2560, 2688
FF_LANE = 32

NEG = -1e30


def _dot(a, b):
    return jnp.dot(a, b, preferred_element_type=F32)


def _dot_nt(a, b):
    return lax.dot_general(a, b, (((1,), (1,)), ((), ())), preferred_element_type=F32)


def _dot_tn(a, b):
    return lax.dot_general(a, b, (((0,), (0,)), ((), ())), preferred_element_type=F32)


def _params(sem):
    return pltpu.CompilerParams(dimension_semantics=sem, vmem_limit_bytes=VMEM_LIMIT)


def _rms(x, g):
    return x * lax.rsqrt(jnp.mean(x * x, axis=-1, keepdims=True) + EPS) * g


def _split_dot(x, m, pieces=2):
    acc, rest = None, x
    for _ in range(pieces):
        part = rest.astype(BF16)
        rest = rest - part.astype(F32)
        acc = _dot(part, m) if acc is None else acc + _dot(part, m)
    return acc


@jax.custom_vjp
def _sel(x, m, mt):
    return _split_dot(x, m)


_sel.defvjp(lambda x, m, mt: (_split_dot(x, m), (m, mt)), lambda res, dy: (_split_dot(dy, res[1]), None, None))


@jax.custom_vjp
def _mm(x, w):
    return _dot(x.astype(BF16), w.astype(BF16))


def _mm_bwd(res, dy):
    x, w = res
    dyb = dy.astype(BF16)
    return _dot_nt(dyb, w.astype(BF16)), _dot_tn(x.astype(BF16), dyb)


_mm.defvjp(lambda x, w: (_mm(x, w), (x, w)), _mm_bwd)


def _rot_impl(x):
    w = x.shape[1]
    lane = lax.broadcasted_iota(jnp.int32, x.shape, 1) % 128
    lo = (lane >= 64) & (lane < 80)
    hi = (lane >= 80) & (lane < 96)
    return jnp.where(hi, pltpu.roll(x, 16, 1), 0.0) - jnp.where(lo, pltpu.roll(x, w - 16, 1), 0.0)


@jax.custom_vjp
def _rot(x):
    return _rot_impl(x)


_rot.defvjp(lambda x: (_rot_impl(x), None), lambda _, dy: (-_rot_impl(dy),))


def _gnorm(x, g, e, et, dim):
    inv = lax.rsqrt(_sel(x * x, e, et) * (1.0 / dim) + EPS)
    return x * _sel(inv, et, e) * g


def _indicator(width, period):
    m = np.zeros((width, 128), np.float32)
    m[np.arange(width), np.arange(width) // period] = 1.0
    return m


def _consts():
    e64 = _indicator(512, 64)
    e128 = _indicator(1024, 128)
    sk = np.zeros((128, 1024), np.float32)
    for h in range(HEADS):
        sk[np.arange(32), 128 * h + 64 + np.arange(32)] = 1.0
    dup = np.zeros((128, 256), np.float32)
    for g in range(2):
        for r in range(2):
            dup[64 * g + np.arange(64), 128 * g + 64 * r + np.arange(64)] = 1.0
    mats = [e64, e64.T, e128, e128.T, sk, sk.T, dup, dup.T]
    return [jnp.asarray(m, BF16) for m in mats]


def _fold_matrix(width, period):
    m = np.zeros((width, 128), np.float32)
    m[np.arange(width), np.arange(width) % period] = 1.0
    return jnp.asarray(m, BF16)


def _rope_tables(lp):
    half = 16
    freqs = ROPE_THETA ** (-np.arange(half, dtype=np.float32) / half)
    ang = np.arange(lp, dtype=np.float32)[:, None] * freqs[None, :]
    cos = np.ones((lp, 128), np.float32)
    sin = np.zeros((lp, 128), np.float32)
    cos[:, 64:80] = np.cos(ang)
    cos[:, 80:96] = np.cos(ang)
    sin[:, 64:80] = np.sin(ang)
    sin[:, 80:96] = np.sin(ang)
    return jnp.asarray(cos), jnp.asarray(sin)


def _norm_matmul(h, g, w, tn, name):
    lp, d = h.shape
    n = w.shape[1]
    tb = TILE_MM

    def body(h_ref, g_ref, w_ref, xn_ref, y_ref):
        @pl.when(pl.program_id(1) == 0)
        def _():
            xn_ref[...] = _rms(h_ref[...], g_ref[...]).astype(BF16)

        y_ref[...] = _dot(xn_ref[...], w_ref[...])

    return pl.pallas_call(
        body, name=name, grid=(lp // tb, n // tn),
        in_specs=[pl.BlockSpec((tb, d), lambda i, j: (i, 0)), pl.BlockSpec((1, d), lambda i, j: (0, 0)),
                  pl.BlockSpec((d, tn), lambda i, j: (0, j))],
        out_specs=[pl.BlockSpec((tb, d), lambda i, j: (i, 0)), pl.BlockSpec((tb, tn), lambda i, j: (i, j))],
        out_shape=[SDS((lp, d), BF16), SDS((lp, n), F32)],
        compiler_params=_params(("parallel", "arbitrary")),
    )(h, g, w)


def _matmul_residual(a, w, res, name):
    m, k = a.shape
    n = w.shape[1]
    tb = TILE_MM

    def body(a_ref, w_ref, r_ref, o_ref):
        o_ref[...] = r_ref[...] + _dot(a_ref[...], w_ref[...])

    return pl.pallas_call(
        body, name=name, grid=(m // tb,),
        in_specs=[pl.BlockSpec((tb, k), lambda i: (i, 0)), pl.BlockSpec((k, n), lambda i: (0, 0)),
                  pl.BlockSpec((tb, n), lambda i: (i, 0))],
        out_specs=pl.BlockSpec((tb, n), lambda i: (i, 0)),
        out_shape=SDS((m, n), F32),
        compiler_params=_params(("parallel",)),
    )(a, w, res)


def _matmul_nt(dy, w, tn, name):
    m, k = dy.shape
    n = w.shape[0]
    tb = TILE_MM

    def body(dy_ref, w_ref, o_ref):
        o_ref[...] = _dot_nt(dy_ref[...].astype(BF16), w_ref[...])

    return pl.pallas_call(
        body, name=name, grid=(m // tb, n // tn),
        in_specs=[pl.BlockSpec((tb, k), lambda i, j: (i, 0)), pl.BlockSpec((tn, k), lambda i, j: (j, 0))],
        out_specs=pl.BlockSpec((tb, tn), lambda i, j: (i, j)),
        out_shape=SDS((m, n), F32),
        compiler_params=_params(("parallel", "arbitrary")),
    )(dy, w)


def _matmul_tn(x, dy, tn, name):
    m, k = x.shape
    n = dy.shape[1]
    tb = TILE_MM

    def body(x_ref, dy_ref, o_ref):
        @pl.when(pl.program_id(1) == 0)
        def _():
            o_ref[...] = jnp.zeros_like(o_ref)

        o_ref[...] += _dot_tn(x_ref[...].astype(BF16), dy_ref[...].astype(BF16))

    return pl.pallas_call(
        body, name=name, grid=(n // tn, m // tb),
        in_specs=[pl.BlockSpec((tb, k), lambda j, i: (i, 0)), pl.BlockSpec((tb, tn), lambda j, i: (i, j))],
        out_specs=pl.BlockSpec((k, tn), lambda j, i: (0, j)),
        out_shape=SDS((k, n), F32),
        compiler_params=_params(("parallel", "arbitrary")),
    )(x, dy)


def _norm_matmul_bwd(dy1, w1, dy2, w2, x, g, dres, specs, name):
    m, d = x.shape
    tb = TILE_MM
    (dy1_spec, w1_spec, dy2_spec, w2_spec) = specs

    def body(dy1_ref, w1_ref, dy2_ref, w2_ref, x_ref, g_ref, r_ref, o_ref, dg_ref):
        @pl.when(pl.program_id(0) == 0)
        def _():
            dg_ref[...] = jnp.zeros_like(dg_ref)

        dxn = _dot_nt(dy1_ref[...], w1_ref[...]) + _dot_nt(dy2_ref[...], w2_ref[...])
        _, vjp = jax.vjp(_rms, x_ref[...], g_ref[...])
        dx, dg = vjp(dxn)
        o_ref[...] = r_ref[...] + dx
        dg_ref[...] += dg

    row = pl.BlockSpec((tb, d), lambda i: (i, 0))
    vec = pl.BlockSpec((1, d), lambda i: (0, 0))
    return pl.pallas_call(
        body, name=name, grid=(m // tb,),
        in_specs=[dy1_spec, w1_spec, dy2_spec, w2_spec, row, vec, row],
        out_specs=[row, vec],
        out_shape=[SDS((m, d), F32), SDS((1, d), F32)],
        compiler_params=_params(("arbitrary",)),
    )(dy1, w1, dy2, w2, x, g, dres)


def _prep_math(pieces, prm, consts, cos, sin):
    fq, fk, sq, sk, sv, cq, ckv, misc = pieces
    gfq, gfk, gsq, gsk, fb, gqa, gkva, gmq, gmk, wq, wkk, wkv = prm
    e64, e64t, e128, e128t, skm, skt, dup, dupt = consts
    cos8 = jnp.concatenate([cos] * HEADS, axis=1)
    sin8 = jnp.concatenate([sin] * HEADS, axis=1)
    fq_n = _gnorm(fq, gfq, e64, e64t, 64)
    fk_n = _gnorm(fk, gfk, e64, e64t, 64)
    ls = jax.nn.log_sigmoid(misc + fb)
    q = _gnorm(_mm(_rms(cq, gqa), wq), gmq, e128, e128t, 96)
    mq = q * cos8 + _rot(q) * sin8
    kva = _rms(ckv, gkva)
    k = _gnorm(_mm(kva, wkk) + _sel(misc, skm, skt), gmk, e128, e128t, 96)
    mk = k * cos8 + _rot(k) * sin8
    mv = _mm(kva, wkv)
    sq_n = _gnorm(sq, gsq, e64, e64t, 64)
    sk_n = _gnorm(sk, gsk, e64[0:128], e64t[:, 0:128], 64)
    skd = _sel(sk_n, dup, dupt)
    svd = _sel(sv, dup, dupt)
    return fq_n, fk_n, ls, mq, mk, mv, sq_n, skd, svd


_PIECES = [(O_FQ, 512), (O_FK, 512), (O_SQ, 512), (O_SK, 128), (O_SV, 128), (O_CQ, 256), (O_CKV, 128), (O_MISC, 128)]
_PRM_SHAPES = [(1, 512), (1, 512), (1, 512), (1, 128), (1, 128), (1, 256), (1, 128), (1, 1024), (1, 1024),
               (256, 1024), (128, 1024), (128, 512)]
_CONST_SHAPES = [(512, 128), (128, 512), (1024, 128), (128, 1024), (128, 1024), (1024, 128), (128, 256), (256, 128)]


def _piece_specs(tb):
    def spec(off, width):
        blk = (GATES_W + off) // width
        return pl.BlockSpec((tb, width), lambda i, blk=blk: (i, blk))
    return [spec(o, w) for o, w in _PIECES] + [spec(O_FV, 512)]


def _full_specs(shapes):
    return [pl.BlockSpec(s, lambda i: (0, 0)) for s in shapes]


def _prep_fwd(proj, prm, consts, cos, sin, name):
    lp = proj.shape[0]
    tb = TILE_ROW
    row = lambda w: pl.BlockSpec((tb, w), lambda i: (i, 0))

    def body(*refs):
        pieces = [r[...] for r in refs[0:8]]
        fv = refs[8][...]
        prm_v = [r[...] for r in refs[9:21]]
        consts_v = [r[...] for r in refs[21:29]]
        cos_v, sin_v = refs[29][...], refs[30][...]
        outs = refs[31:]
        fq_n, fk_n, ls, mq, mk, mv, sq_n, skd, svd = _prep_math(pieces, prm_v, consts_v, cos_v, sin_v)
        for ref, val in zip(outs, (fq_n, fk_n, fv, mq, mk, mv, sq_n, skd, svd)):
            ref[...] = val.astype(BF16)
        outs[9][...] = ls

    widths = [512, 512, 512, 1024, 1024, 512, 512, 256, 256]
    return pl.pallas_call(
        body, name=name, grid=(lp // tb,),
        in_specs=_piece_specs(tb) + _full_specs(_PRM_SHAPES) + _full_specs(_CONST_SHAPES) + [row(128), row(128)],
        out_specs=[row(w) for w in widths] + [row(128)],
        out_shape=[SDS((lp, w), BF16) for w in widths] + [SDS((lp, 128), F32)],
        compiler_params=_params(("parallel",)),
    )(*([proj] * 9), *prm, *consts, cos, sin)


def _prep_bwd(proj, prm, consts, cos, sin, cots, folds, name):
    lp = proj.shape[0]
    tb = TILE_ROW
    row = lambda w: pl.BlockSpec((tb, w), lambda i: (i, 0))
    fold64, fold128 = folds

    def body(*refs):
        pieces = [r[...] for r in refs[0:8]]
        prm_v = [r[...] for r in refs[9:21]]
        consts_v = [r[...] for r in refs[21:29]]
        cos_v, sin_v = refs[29][...], refs[30][...]
        dfq, dfk, dfv, dmq, dmk, dmv, dsq, dskp, dsvp, dls = [r[...] for r in refs[31:41]]
        f64, f128 = refs[41][...], refs[42][...]
        d_ref = refs[43]
        g_refs = refs[44:]

        @pl.when(pl.program_id(0) == 0)
        def _():
            for r in g_refs:
                r[...] = jnp.zeros_like(r)

        def pair_sum(p):
            return jnp.concatenate([p[:, 0:128] + p[:, 128:256], p[:, 256:384] + p[:, 384:512]], axis=1)

        f = lambda pc, pr: _prep_math(pc, pr, consts_v, cos_v, sin_v)
        _, vjp = jax.vjp(f, pieces, prm_v)
        dpc, dprm = vjp((dfq, dfk, dls, dmq, dmk, dmv, dsq, pair_sum(dskp), pair_sum(dsvp)))
        d_fq, d_fk, d_sq, d_sk, d_sv, d_cq, d_ckv, d_misc = dpc
        for off, val in ((O_FQ, d_fq), (O_FK, d_fk), (O_FV, dfv), (O_SQ, d_sq), (O_SK, d_sk), (O_SV, d_sv),
                         (O_CQ, d_cq), (O_CKV, d_ckv), (O_MISC, d_misc)):
            d_ref[:, off:off + val.shape[1]] = val.astype(BF16)
        folded = {0: f64, 1: f64, 2: f64, 3: f64[0:128], 7: f128, 8: f128}
        for idx, (ref, val) in enumerate(zip(g_refs, dprm)):
            if idx in folded:
                ref[...] += _split_dot(jnp.broadcast_to(val, (8, val.shape[1])), folded[idx], 3)
            elif val.shape[0] == 1:
                ref[...] += jnp.broadcast_to(val, ref.shape)
            else:
                ref[...] += val

    g_shapes = [(8, 128), (8, 128), (8, 128), (8, 128), (8, 128), (8, 256), (8, 128), (8, 128), (8, 128),
                (256, 1024), (128, 1024), (128, 512)]
    cot_widths = [512, 512, 512, 1024, 1024, 512, 512, 512, 512, 128]
    return pl.pallas_call(
        body, name=name, grid=(lp // tb,),
        in_specs=(_piece_specs(tb) + _full_specs(_PRM_SHAPES) + _full_specs(_CONST_SHAPES) + [row(128), row(128)]
                  + [row(w) for w in cot_widths] + _full_specs([(512, 128), (1024, 128)])),
        out_specs=[row(OTHER_W)] + _full_specs(g_shapes),
        out_shape=[SDS((lp, OTHER_W), BF16)] + [SDS(s, F32) for s in g_shapes],
        compiler_params=_params(("arbitrary",)),
    )(*([proj] * 9), *prm, *consts, cos, sin, *cots, fold64, fold128)


def _cumsum(xs, reverse, name):
    lp = xs[0].shape[0]
    tb = TILE_MM
    nb = lp // tb
    n_in = len(xs)
    idx = (lambda i: (nb - 1 - i, 0)) if reverse else (lambda i: (i, 0))

    def body(*refs):
        o_ref, carry = refs[n_in], refs[n_in + 1]

        @pl.when(pl.program_id(0) == 0)
        def _():
            carry[...] = jnp.zeros_like(carry)

        x = refs[0][...]
        for r in refs[1:n_in]:
            x = x + r[...]
        r_i = lax.broadcasted_iota(jnp.int32, (tb, tb), 0)
        c_i = lax.broadcasted_iota(jnp.int32, (tb, tb), 1)
        tri = ((c_i >= r_i) if reverse else (c_i <= r_i)).astype(BF16)
        acc, rest = None, x
        for _ in range(3):
            part = rest.astype(BF16)
            rest = rest - part.astype(F32)
            acc = _dot(tri, part) if acc is None else acc + _dot(tri, part)
        o_ref[...] = acc + carry[...]
        carry[...] += jnp.sum(x, axis=0, keepdims=True)

    return pl.pallas_call(
        body, name=name, grid=(nb,),
        in_specs=[pl.BlockSpec((tb, 128), idx)] * n_in,
        out_specs=pl.BlockSpec((tb, 128), idx),
        out_shape=SDS((lp, 128), F32),
        scratch_shapes=[pltpu.VMEM((1, 128), F32)],
        compiler_params=_params(("arbitrary",)),
    )(*xs)


class _Att:
    def __init__(self, mode):
        self.mode = mode
        self.wide = mode == "mla"
        self.qw = 256 if self.wide else 128
        self.scale = (96 if mode == "mla" else 64) ** -0.5

    def resident(self, x, lo, scaled):
        if self.wide:
            return x[:, 0:128], x[:, 128:256]
        if scaled:
            x = x * jnp.asarray(self.scale, x.dtype)
        zero = jnp.zeros_like(x)
        return jnp.where(lo, x, zero), jnp.where(lo, zero, x)

    def moving(self, x):
        return (x[:, 0:128], x[:, 128:256]) if self.wide else (x, x)

    def logits(self, a, b, qpos, kpos, key_decay, slope, masked):
        s = _dot_nt(a, b)
        if self.wide:
            s = s * self.scale
        if self.mode == "fox":
            s = s - key_decay
        if self.mode == "swa":
            s = s - slope * (qpos - kpos).astype(F32)
        if masked:
            ok = kpos <= qpos
            if self.mode == "swa":
                ok = ok & ((kpos < N_META) | (qpos - kpos < WINDOW))
            s = jnp.where(ok, s, NEG)
        return s


def _halves(x, lo):
    zero = jnp.zeros_like(x)
    return jnp.where(lo, x, zero), jnp.where(lo, zero, x)


def _q_chunks(qi):
    far = qi >= 2
    return jnp.where(far, 3, qi + 1), lambda t: jnp.where(far, jnp.where(t == 0, 0, qi - 2 + t), t)


def _kv_specs(att, lp, rows):
    if att.mode == "swa":
        return (pl.BlockSpec((rows, 128), lambda g, i: (i if rows != lp else 0, g)),) * 2
    return (pl.BlockSpec((rows, PAIRS * att.qw), lambda g, i: (i if rows != lp else 0, g)),
            pl.BlockSpec((rows, PAIRS * 128), lambda g, i: (i if rows != lp else 0, g)))


def _pair_cols(att, x, pp, width):
    return x if x.shape[1] == width else x[:, pp * width:(pp + 1) * width]


def _att_fwd(att, q, k, v, extra, name):
    lp = q.shape[0]
    t = TILE_ATT
    nq = lp // t
    qw = att.qw
    mode = att.mode
    nh = 2 * PAIRS

    def body(*refs):
        q_ref, k_ref, v_ref = refs[0:3]
        o_ref, lse_ref = refs[-2:]
        g, qi = pl.program_id(0), pl.program_id(1)
        lo = lax.broadcasted_iota(jnp.int32, (1, 128), 1) < 64
        q_all = q_ref[...]
        q_heads = [h for pp in range(PAIRS) for h in att.resident(_pair_cols(att, q_all, pp, qw), lo, True)]
        qpos = qi * t + lax.broadcasted_iota(jnp.int32, (t, 1), 0)

        def step(kj, carry, masked):
            ks = pl.multiple_of(kj * t, t)
            kc, vc = k_ref[pl.ds(ks, t), :], v_ref[pl.ds(ks, t), :]
            kpos = kj * t + lax.broadcasted_iota(jnp.int32, (1, t), 1)
            out = []
            for h in range(nh):
                pp = h // 2
                m, l, acc = carry[3 * h:3 * h + 3]
                k_h = att.moving(_pair_cols(att, kc, pp, qw))[h % 2]
                decay = refs[3][h, :, pl.ds(ks, t)] if mode == "fox" else None
                slope = refs[4][nh * g + h] if mode == "swa" else None
                s = att.logits(q_heads[h], k_h, qpos, kpos, decay, slope, masked)
                m_new = jnp.maximum(m, jnp.max(s, axis=-1, keepdims=True))
                alpha = jnp.exp(m - m_new)
                pe = jnp.exp(s - m_new)
                l = alpha * l + jnp.sum(pe, axis=-1, keepdims=True)
                acc = alpha * acc + _dot(pe.astype(BF16), _pair_cols(att, vc, pp, 128))
                out += [m_new, l, acc]
            return tuple(out)

        init = []
        for h in range(nh):
            if mode == "swa":
                init += [jnp.full((t, 1), refs[3][nh * g + h], F32), jnp.ones((t, 1), F32)]
            else:
                init += [jnp.full((t, 1), NEG, F32), jnp.zeros((t, 1), F32)]
            init.append(jnp.zeros((t, 128), F32))
        if mode == "swa":
            n_steps, chunk_of = _q_chunks(qi)
            carry = lax.fori_loop(0, n_steps, lambda i, c: step(chunk_of(i), c, True), tuple(init))
        else:
            carry = lax.fori_loop(0, qi, lambda kj, c: step(kj, c, False), tuple(init))
            carry = step(qi, carry, True)
        outs = []
        for pp in range(PAIRS):
            (ma, la, acca), (mb, lb, accb) = carry[6 * pp:6 * pp + 3], carry[6 * pp + 3:6 * pp + 6]
            outs.append(jnp.where(lo, acca / la, accb / lb).astype(BF16))
            lse_ref[2 * pp] = ma + jnp.log(la)
            lse_ref[2 * pp + 1] = mb + jnp.log(lb)
        o_ref[...] = jnp.concatenate(outs, axis=1)

    in_specs = [pl.BlockSpec((t, PAIRS * qw), lambda g, i: (i, g)), *_kv_specs(att, lp, lp)]
    if mode == "fox":
        in_specs += [pl.BlockSpec((nh, 1, lp), lambda g, i: (g, 0, 0))]
    if mode == "swa":
        in_specs += [pl.BlockSpec(memory_space=pltpu.SMEM)] * 2
    return pl.pallas_call(
        body, name=name, grid=(4 // PAIRS, nq), in_specs=in_specs,
        out_specs=[pl.BlockSpec((t, PAIRS * 128), lambda g, i: (i, g)), pl.BlockSpec((nh, t, 1), lambda g, i: (g, i, 0))],
        out_shape=[SDS((lp, 512), BF16), SDS((HEADS, lp, 1), F32)],
        compiler_params=_params(("parallel", "arbitrary")),
    )(q, k, v, *extra)


def _att_dq(att, q, k, v, o, do, lse, extra, name):
    lp = q.shape[0]
    t = TILE_ATT
    nq = lp // t
    qw = att.qw
    mode = att.mode
    nh = 2 * PAIRS

    def body(*refs):
        q_ref, k_ref, v_ref, o_ref, do_ref, lse_ref = refs[0:6]
        n_out = 2 if mode == "mla" else 3
        outs = refs[len(refs) - n_out:]
        dq_ref, delta_ref = outs[0:2]
        g, qi = pl.program_id(0), pl.program_id(1)
        lo = lax.broadcasted_iota(jnp.int32, (1, 128), 1) < 64
        q_all, do_all = q_ref[...], do_ref[...]
        prod = do_all.astype(F32) * o_ref[...].astype(F32)
        q_heads, do_heads, delta = [], [], []
        for pp in range(PAIRS):
            q_heads += att.resident(_pair_cols(att, q_all, pp, qw), lo, True)
            do_heads += _halves(_pair_cols(att, do_all, pp, 128), lo)
            pr_pp = _pair_cols(att, prod, pp, 128)
            delta += [jnp.sum(jnp.where(lo, pr_pp, 0.0), axis=-1, keepdims=True),
                      jnp.sum(jnp.where(lo, 0.0, pr_pp), axis=-1, keepdims=True)]
        lse_v = [lse_ref[h] for h in range(nh)]
        qpos = qi * t + lax.broadcasted_iota(jnp.int32, (t, 1), 0)

        def step(kj, carry, masked):
            ks = pl.multiple_of(kj * t, t)
            kc, vc = k_ref[pl.ds(ks, t), :], v_ref[pl.ds(ks, t), :]
            kpos = kj * t + lax.broadcasted_iota(jnp.int32, (1, t), 1)
            out = []
            for h in range(nh):
                pp = h // 2
                k_h = att.moving(_pair_cols(att, kc, pp, qw))[h % 2]
                decay = refs[6][h, :, pl.ds(ks, t)] if mode == "fox" else None
                slope = refs[7][nh * g + h] if mode == "swa" else None
                s = att.logits(q_heads[h], k_h, qpos, kpos, decay, slope, masked)
                pr = jnp.exp(s - lse_v[h])
                ds = pr * (_dot_nt(do_heads[h], _pair_cols(att, vc, pp, 128)) - delta[h])
                out.append(carry[2 * h] + _dot(ds.astype(BF16), k_h))
                out.append(carry[2 * h + 1] + jnp.sum(ds, axis=-1, keepdims=True) if mode == "fox" else carry[2 * h + 1])
            return tuple(out)

        init = (jnp.zeros((t, 128), F32), jnp.zeros((t, 1), F32)) * nh
        if mode == "swa":
            n_steps, chunk_of = _q_chunks(qi)
            carry = lax.fori_loop(0, n_steps, lambda i, c: step(chunk_of(i), c, True), init)
        else:
            carry = lax.fori_loop(0, qi, lambda kj, c: step(kj, c, False), init)
            carry = step(qi, carry, True)
        dq = []
        for pp in range(PAIRS):
            dqa, dca, dqb, dcb = carry[4 * pp:4 * pp + 4]
            dq += [dqa, dqb] if att.wide else [jnp.where(lo, dqa, dqb)]
            if mode == "fox":
                outs[2][2 * pp] = dca
                outs[2][2 * pp + 1] = dcb
        dq_ref[...] = jnp.concatenate(dq, axis=1) * att.scale
        for h in range(nh):
            delta_ref[h] = delta[h]
        if mode == "swa":
            ds_ref = outs[2]

            @pl.when(qi == 0)
            def _():
                ds_ref[...] = jnp.zeros_like(ds_ref)

            lane = lax.broadcasted_iota(jnp.int32, (8, 128), 1)
            acc = jnp.zeros((8, 128), F32)
            for h in range(nh):
                tot = -jnp.sum(jnp.exp(refs[6][nh * g + h] - lse_v[h]) * delta[h])
                acc = acc + jnp.where(lane == h, tot, 0.0)
            ds_ref[0] += acc

    col = pl.BlockSpec((nh, t, 1), lambda g, i: (g, i, 0))
    in_specs = [pl.BlockSpec((t, PAIRS * qw), lambda g, i: (i, g)), *_kv_specs(att, lp, lp),
                pl.BlockSpec((t, PAIRS * 128), lambda g, i: (i, g)), pl.BlockSpec((t, PAIRS * 128), lambda g, i: (i, g)), col]
    out_specs = [pl.BlockSpec((t, PAIRS * qw), lambda g, i: (i, g)), col]
    out_shape = [SDS((lp, 4 * qw), F32), SDS((HEADS, lp, 1), F32)]
    if mode == "fox":
        in_specs += [pl.BlockSpec((nh, 1, lp), lambda g, i: (g, 0, 0))]
        out_specs.append(col)
        out_shape.append(SDS((HEADS, lp, 1), F32))
    if mode == "swa":
        in_specs += [pl.BlockSpec(memory_space=pltpu.SMEM)] * 2
        out_specs.append(pl.BlockSpec((1, 8, 128), lambda g, i: (g, 0, 0)))
        out_shape.append(SDS((4 // PAIRS, 8, 128), F32))
    return pl.pallas_call(
        body, name=name, grid=(4 // PAIRS, nq), in_specs=in_specs, out_specs=out_specs, out_shape=out_shape,
        compiler_params=_params(("parallel", "arbitrary")),
    )(q, k, v, o, do, lse, *extra)


def _att_dkv(att, q, k, v, do, lse_row, delta_row, extra, name):
    lp = q.shape[0]
    t = TILE_ATT
    nq = lp // t
    qw = att.qw
    mode = att.mode
    nh = 2 * PAIRS

    def body(*refs):
        q_ref, k_ref, v_ref, do_ref, lse_ref, delta_ref = refs[0:6]
        n_out = 3 if mode == "fox" else 2
        outs = refs[len(refs) - n_out:]
        dk_ref, dv_ref = outs[0:2]
        g, kj = pl.program_id(0), pl.program_id(1)
        lo = lax.broadcasted_iota(jnp.int32, (1, 128), 1) < 64
        k_all, v_all = k_ref[...], v_ref[...]
        k_heads, v_heads = [], []
        for pp in range(PAIRS):
            k_heads += att.resident(_pair_cols(att, k_all, pp, qw), lo, True)
            v_heads += _halves(_pair_cols(att, v_all, pp, 128), lo)
        kpos = kj * t + lax.broadcasted_iota(jnp.int32, (t, 1), 0)

        def step(qi, carry, masked):
            qs = pl.multiple_of(qi * t, t)
            qc, doc = q_ref[pl.ds(qs, t), :], do_ref[pl.ds(qs, t), :]
            qpos = qi * t + lax.broadcasted_iota(jnp.int32, (1, t), 1)
            out = []
            for h in range(nh):
                pp = h // 2
                dk_acc, dv_acc, dc_acc = carry[3 * h:3 * h + 3]
                q_h = att.moving(_pair_cols(att, qc, pp, qw))[h % 2]
                do_h = _pair_cols(att, doc, pp, 128)
                decay = refs[6][h] if mode == "fox" else None
                slope = refs[6][nh * g + h] if mode == "swa" else None
                st = att.logits(k_heads[h], q_h, qpos, kpos, decay, slope, masked)
                pt = jnp.exp(st - lse_ref[h, :, pl.ds(qs, t)])
                dst = pt * (_dot_nt(v_heads[h], do_h) - delta_ref[h, :, pl.ds(qs, t)])
                dv_acc = dv_acc + _dot(pt.astype(BF16), do_h)
                dk_acc = dk_acc + _dot(dst.astype(BF16), q_h)
                if mode == "fox":
                    dc_acc = dc_acc - jnp.sum(dst, axis=-1, keepdims=True)
                out += [dk_acc, dv_acc, dc_acc]
            return tuple(out)

        init = (jnp.zeros((t, 128), F32), jnp.zeros((t, 128), F32), jnp.zeros((t, 1), F32)) * nh
        if mode == "swa":
            last = jnp.where(kj == 0, nq, jnp.minimum(kj + 2, nq))
            carry = lax.fori_loop(kj, last, lambda qi, c: step(qi, c, True), init)
        else:
            carry = step(kj, init, True)
            carry = lax.fori_loop(kj + 1, nq, lambda qi, c: step(qi, c, False), carry)
        dk, dv = [], []
        for pp in range(PAIRS):
            dka, dva, dca, dkb, dvb, dcb = carry[6 * pp:6 * pp + 6]
            dk += [dka, dkb] if att.wide else [jnp.where(lo, dka, dkb)]
            dv.append(jnp.where(lo, dva, dvb))
            if mode == "fox":
                outs[2][2 * pp] = dca
                outs[2][2 * pp + 1] = dcb
        dk_ref[...] = jnp.concatenate(dk, axis=1) * att.scale
        dv_ref[...] = jnp.concatenate(dv, axis=1)

    rowv = pl.BlockSpec((nh, 1, lp), lambda g, j: (g, 0, 0))
    col = pl.BlockSpec((nh, t, 1), lambda g, j: (g, j, 0))
    in_specs = [pl.BlockSpec((lp, PAIRS * qw), lambda g, j: (0, g)), *_kv_specs(att, lp, t),
                pl.BlockSpec((lp, PAIRS * 128), lambda g, j: (0, g)), rowv, rowv]
    out_specs = [pl.BlockSpec((t, PAIRS * qw), lambda g, j: (j, g)), pl.BlockSpec((t, PAIRS * 128), lambda g, j: (j, g))]
    out_shape = [SDS((lp, 4 * qw), F32), SDS((lp, 512), F32)]
    if mode == "fox":
        in_specs += [col]
        out_specs.append(col)
        out_shape.append(SDS((HEADS, lp, 1), F32))
    if mode == "swa":
        in_specs += [pl.BlockSpec(memory_space=pltpu.SMEM)]
    return pl.pallas_call(
        body, name=name, grid=(4 // PAIRS, nq), in_specs=in_specs, out_specs=out_specs, out_shape=out_shape,
        compiler_params=_params(("parallel", "arbitrary")),
    )(q, k, v, do, lse_row, delta_row, *extra)


def _post_fwd(h, proj, outs, wb, wo, name):
    lp, d = h.shape
    tb = TILE_POST
    row = lambda w: pl.BlockSpec((tb, w), lambda i: (i, 0))

    def body(h_ref, g0, g1, g2, oa, ob, oc, wb_ref, wo_ref, o_ref):
        merged = jnp.zeros((tb, d), F32)
        for n, (g_ref, br) in enumerate(((g0, oa), (g1, ob), (g2, oc))):
            merged = merged + jax.nn.sigmoid(g_ref[...]) * _dot(br[...], wb_ref[n])
        o_ref[...] = h_ref[...] + _dot(merged.astype(BF16), wo_ref[...])

    gate = lambda n: pl.BlockSpec((tb, d), lambda i, n=n: (i, n))
    return pl.pallas_call(
        body, name=name, grid=(lp // tb,),
        in_specs=[row(d), gate(0), gate(1), gate(2), row(512), row(512), row(512),
                  pl.BlockSpec((3, 512, d), lambda i: (0, 0, 0)), pl.BlockSpec((d, d), lambda i: (0, 0))],
        out_specs=row(d), out_shape=SDS((lp, d), F32),
        compiler_params=_params(("parallel",)),
    )(h, proj, proj, proj, *outs, wb, wo)


def _post_bwd(dh, proj, outs, wb, wo, name):
    lp, d = dh.shape
    tb = TILE_POST
    row = lambda w: pl.BlockSpec((tb, w), lambda i: (i, 0))

    def body(dh_ref, g0, g1, g2, oa, ob, oc, wb_ref, wo_ref, dg_ref, doa, dob, doc, dwb_ref, dwo_ref):
        @pl.when(pl.program_id(0) == 0)
        def _():
            dwb_ref[...] = jnp.zeros_like(dwb_ref)
            dwo_ref[...] = jnp.zeros_like(dwo_ref)

        dhb = dh_ref[...].astype(BF16)
        dm = _dot_nt(dhb, wo_ref[...])
        merged = jnp.zeros((tb, d), F32)
        for n, (g_ref, br, do_ref) in enumerate(((g0, oa, doa), (g1, ob, dob), (g2, oc, doc))):
            gate = jax.nn.sigmoid(g_ref[...])
            o_n = br[...]
            y = _dot(o_n, wb_ref[n])
            merged = merged + gate * y
            dy = (dm * gate).astype(BF16)
            dg_ref[:, n * d:(n + 1) * d] = (dm * y * gate * (1.0 - gate)).astype(BF16)
            do_ref[...] = _dot_nt(dy, wb_ref[n]).astype(BF16)
            dwb_ref[n] += _dot_tn(o_n, dy)
        dwo_ref[...] += _dot_tn(merged.astype(BF16), dhb)

    gate = lambda n: pl.BlockSpec((tb, d), lambda i, n=n: (i, n))
    wb_spec = pl.BlockSpec((3, 512, d), lambda i: (0, 0, 0))
    wo_spec = pl.BlockSpec((d, d), lambda i: (0, 0))
    return pl.pallas_call(
        body, name=name, grid=(lp // tb,),
        in_specs=[row(d), gate(0), gate(1), gate(2), row(512), row(512), row(512), wb_spec, wo_spec],
        out_specs=[row(GATES_W), row(512), row(512), row(512), wb_spec, wo_spec],
        out_shape=[SDS((lp, GATES_W), BF16)] + [SDS((lp, 512), BF16)] * 3 + [SDS((3, 512, d), F32), SDS((d, d), F32)],
        compiler_params=_params(("arbitrary",)),
    )(dh, proj, proj, proj, *outs, wb, wo)


def _shift_down(x, halo, n, first):
    rows = lax.broadcasted_iota(jnp.int32, x.shape, 0)
    edge = jnp.concatenate([pltpu.roll(halo, n, 0), jnp.zeros((x.shape[0] - 8, x.shape[1]), F32)], axis=0)
    edge = jnp.where(first, 0.0, edge)
    return jnp.where(rows < n, edge, pltpu.roll(x, n, 0))


def _shift_up(x, halo, n, last):
    tb = x.shape[0]
    rows = lax.broadcasted_iota(jnp.int32, x.shape, 0)
    edge = jnp.concatenate([jnp.zeros((tb - 8, x.shape[1]), F32), pltpu.roll(halo, 8 - n, 0)], axis=0)
    edge = jnp.where(last, 0.0, edge)
    return jnp.where(rows >= tb - n, edge, pltpu.roll(x, tb - n, 0))


def _conv(u, halo, w_ref, b_ref, first):
    taps = (_shift_down(u, halo, 2, first), _shift_down(u, halo, 1, first), u)
    c = b_ref[...] + w_ref[0:1, :] * taps[0] + w_ref[1:2, :] * taps[1] + w_ref[2:3, :] * taps[2]
    return c, taps


def _ffn_specs(tb, f):
    hb = tb // 8
    cur = lambda c: pl.BlockSpec((tb, f), lambda i, c=c: (i, c))
    prev = lambda c: pl.BlockSpec((8, f), lambda i, c=c: (jnp.maximum(i * hb - 1, 0), c))
    vec = lambda r, c: pl.BlockSpec((r, f), lambda i, c=c: (0, c))
    return cur, prev, vec


def _ffn_act_fwd(u, cw, cb, name):
    lp = u.shape[0]
    f = D_FF
    tb = TILE_ROW
    cur, prev, vec = _ffn_specs(tb, f)

    def body(ug, uv, hg, hv, wg, wv, bg, bv, o_ref):
        first = pl.program_id(0) == 0
        cg, _ = _conv(ug[...], hg[...], wg, bg, first)
        cv, _ = _conv(uv[...], hv[...], wv, bv, first)
        o_ref[...] = (cg * jax.nn.sigmoid(cg) * cv).astype(BF16)

    return pl.pallas_call(
        body, name=name, grid=(lp // tb,),
        in_specs=[cur(0), cur(1), prev(0), prev(1), vec(8, 0), vec(8, 1), vec(1, 0), vec(1, 1)],
        out_specs=pl.BlockSpec((tb, f), lambda i: (i, 0)), out_shape=SDS((lp, f), BF16),
        compiler_params=_params(("parallel",)),
    )(u, u, u, u, cw, cw, cb, cb)


def _ffn_act_bwd_conv(u, dact, cw, cb, name):
    lp = u.shape[0]
    f = D_FF
    tb = TILE_ROW
    cur, prev, vec = _ffn_specs(tb, f)

    def body(ug, uv, hg, hv, wg, wv, bg, bv, da_ref, dcg_ref, dcv_ref, dwg, dwv, dbg, dbv):
        first = pl.program_id(0) == 0

        @pl.when(first)
        def _():
            for r in (dwg, dwv, dbg, dbv):
                r[...] = jnp.zeros_like(r)

        cg, tg = _conv(ug[...], hg[...], wg, bg, first)
        cv, tv = _conv(uv[...], hv[...], wv, bv, first)
        da = da_ref[...]
        sg = jax.nn.sigmoid(cg)
        dcg = da * cv * sg * (1.0 + cg * (1.0 - sg))
        dcv = da * cg * sg
        dcg_ref[...] = dcg
        dcv_ref[...] = dcv
        for dc, taps, dw, db in ((dcg, tg, dwg, dbg), (dcv, tv, dwv, dbv)):
            for n in range(3):
                dw[n:n + 1, :] += jnp.sum(dc * taps[n], axis=0, keepdims=True)
            db[0:1, :] += jnp.sum(dc, axis=0, keepdims=True)

    row = pl.BlockSpec((tb, f), lambda i: (i, 0))
    acc = pl.BlockSpec((8, f), lambda i: (0, 0))
    return pl.pallas_call(
        body, name=name, grid=(lp // tb,),
        in_specs=[cur(0), cur(1), prev(0), prev(1), vec(8, 0), vec(8, 1), vec(1, 0), vec(1, 1), row],
        out_specs=[row, row, acc, acc, acc, acc],
        out_shape=[SDS((lp, f), F32)] * 2 + [SDS((8, f), F32)] * 4,
        compiler_params=_params(("arbitrary",)),
    )(u, u, u, u, cw, cw, cb, cb, dact)


def _ffn_act_bwd_in(dcg, dcv, cw, name):
    lp = dcg.shape[0]
    f = D_FF
    tb = TILE_ROW
    nb = lp // tb
    hb = tb // 8
    cur = pl.BlockSpec((tb, f), lambda i: (i, 0))
    nxt = pl.BlockSpec((8, f), lambda i: (jnp.minimum((i + 1) * hb, nb * hb - 1), 0))
    vec = lambda c: pl.BlockSpec((8, f), lambda i, c=c: (0, c))

    def body(dg, dv, ng, nv, wg, wv, og, ov):
        last = pl.program_id(0) == nb - 1
        for dc_ref, n_ref, w_ref, o_ref in ((dg, ng, wg, og), (dv, nv, wv, ov)):
            dc, halo = dc_ref[...], n_ref[...]
            du = (w_ref[2:3, :] * dc + w_ref[1:2, :] * _shift_up(dc, halo, 1, last)
                  + w_ref[0:1, :] * _shift_up(dc, halo, 2, last))
            o_ref[...] = du.astype(BF16)

    return pl.pallas_call(
        body, name=name, grid=(nb,),
        in_specs=[cur, cur, nxt, nxt, vec(0), vec(1)],
        out_specs=[cur, cur],
        out_shape=[SDS((lp, f), BF16)] * 2,
        compiler_params=_params(("parallel",)),
    )(dcg, dcv, dcg, dcv, cw, cw)


def _loss_head(y, target, n_real, name):
    lp, d = y.shape
    tb = TILE_MM

    def body(y_ref, t_ref, dy_ref, loss_ref):
        i = pl.program_id(0)

        @pl.when(i == 0)
        def _():
            loss_ref[...] = jnp.zeros_like(loss_ref)

        rows = i * tb + lax.broadcasted_iota(jnp.int32, (tb, 1), 0)
        real = (rows >= N_META) & (rows < N_META + n_real)
        diff = jnp.where(real, y_ref[...] - t_ref[...], 0.0)
        dy_ref[...] = diff * (1.0 / d)
        loss_ref[...] += (0.5 / d) * jnp.sum(diff * diff).reshape(1, 1)

    row = pl.BlockSpec((tb, d), lambda i: (i, 0))
    return pl.pallas_call(
        body, name=name, grid=(lp // tb,), in_specs=[row, row],
        out_specs=[row, pl.BlockSpec((1, 1), lambda i: (0, 0))],
        out_shape=[SDS((lp, d), F32), SDS((1, 1), F32)],
        compiler_params=_params(("arbitrary",)),
    )(y, target)


def _pad_lanes(v, width, at=0):
    return jnp.pad(v.astype(F32), (at, width - at - v.shape[0]))[None, :]


def _mix_params(w, big, l):
    b = lambda a: a.astype(BF16)
    win = big["w_in"]
    fq, fk, fv, ff, cq, ckv, kr, sq, sk, sv, gates = jnp.split(
        win, [512, 1024, 1536, 1544, 1800, 1928, 1960, 2472, 2600, 2728], axis=1)
    misc = jnp.concatenate([kr, ff, jnp.zeros((D_MODEL, 88), win.dtype)], axis=1)
    w_in = b(jnp.concatenate([gates, fq, fk, fv, sq, sk, sv, cq, ckv, misc], axis=1))
    wq = jnp.pad(big["mla_w_q_up"].reshape(256, HEADS, 96), ((0, 0), (0, 0), (0, 32))).reshape(256, 1024)
    wkv = big["mla_w_kv_up"].reshape(128, HEADS, 128)
    wkk = jnp.pad(wkv[:, :, :64], ((0, 0), (0, 0), (0, 64))).reshape(128, 1024)
    wkvv = wkv[:, :, 64:].reshape(128, 512)
    tile = lambda g, n: jnp.tile(g.astype(F32), n)[None, :]
    prm = [tile(w["fox_q_g"][l], 8), tile(w["fox_k_g"][l], 8), tile(w["swa_q_g"][l], 8), tile(w["swa_k_g"][l], 2),
           _pad_lanes(w["fox_forget_b"][l], 128, FF_LANE), w["mla_q_a_g"][l][None, :], w["mla_kv_a_g"][l][None, :],
           tile(jnp.pad(w["mla_q_g"][l], (0, 32)), 8), tile(jnp.pad(w["mla_k_g"][l], (0, 32)), 8),
           wq.astype(F32), wkk.astype(F32), wkvv.astype(F32)]
    return dict(g1=w["norm1_g"][l][None, :], w_in=w_in, prm=prm, sinks=w["swa_sinks"][l].astype(F32),
                wb=b(big["w_branch"]), wo=b(big["w_o"]))


def _ffn_params(w, big, l):
    cw = jnp.pad(w["ffn_conv_w"][l].astype(F32), ((0, 5), (0, 0)))
    return dict(g2=w["norm2_g"][l][None, :], w_up=big["ffn_w_up"].astype(BF16), cw=cw,
                cb=w["ffn_conv_b"][l][None, :].astype(F32), w_down=big["ffn_w_down"].astype(BF16))


def _cols(c):
    ct = c[:, FF_LANE:FF_LANE + HEADS].T
    return ct[:, :, None], ct[:, None, :]


def _rows(col):
    return jnp.swapaxes(col, 1, 2)


def _from_cols(col):
    return jnp.pad(col[:, :, 0].T, ((0, 0), (FF_LANE, 128 - FF_LANE - HEADS)))


def _layer_fwd_mix(h, lw, consts, cos, sin, slopes, l):
    tag = f"l{l}_"
    xn, proj = _norm_matmul(h, lw["g1"], lw["w_in"], IN_W // 2, tag + "in_proj")
    fq, fk, fv, mq, mk, mv, sq, skd, svd, ls = _prep_fwd(proj, lw["prm"], consts, cos, sin, tag + "prep")
    c = _cumsum([ls], False, tag + "decay_cumsum")
    c_col, c_row = _cols(c)
    oa, lse_a = _att_fwd(_Att("fox"), fq, fk, fv, (c_row,), tag + "fox_fwd")
    ob, lse_b = _att_fwd(_Att("mla"), mq, mk, mv, (), tag + "mla_fwd")
    oc, lse_c = _att_fwd(_Att("swa"), sq, skd, svd, (lw["sinks"], slopes), tag + "swa_fwd")
    h2 = _post_fwd(h, proj, (oa, ob, oc), lw["wb"], lw["wo"], tag + "merge")
    saved = dict(h=h, xn=xn, proj=proj, q=(fq, mq, sq), k=(fk, mk, skd), v=(fv, mv, svd), c=(c_col, c_row),
                 o=(oa, ob, oc), lse=(lse_a, lse_b, lse_c), h2=h2)
    return h2, saved


def _layer_fwd_ffn(h2, lw, l):
    tag = f"l{l}_"
    xn2, u = _norm_matmul(h2, lw["g2"], lw["w_up"], D_FF, tag + "ffn_up")
    act = _ffn_act_fwd(u, lw["cw"], lw["cb"], tag + "ffn_act")
    h3 = _matmul_residual(act, lw["w_down"], h2, tag + "ffn_down")
    return h3, dict(xn2=xn2, u=u, act=act)


def _layer_bwd_ffn(dh3, lw, sv, l):
    tag = f"l{l}_"
    f = D_FF
    dact = _matmul_nt(dh3, lw["w_down"], f, tag + "ffn_down_dx")
    dw_down = _matmul_tn(sv["act"], dh3, D_MODEL, tag + "ffn_down_dw")
    dcg, dcv, dwg, dwv, dbg, dbv = _ffn_act_bwd_conv(sv["u"], dact, lw["cw"], lw["cb"], tag + "ffn_act_dc")
    dug, duv = _ffn_act_bwd_in(dcg, dcv, lw["cw"], tag + "ffn_act_du")
    du = jnp.concatenate([dug, duv], axis=1)
    dw_up = _matmul_tn(sv["xn2"], du, f, tag + "ffn_up_dw")
    tb = TILE_MM
    half = lambda c: pl.BlockSpec((tb, f), lambda i, c=c: (i, c))
    whalf = lambda c: pl.BlockSpec((D_MODEL, f), lambda i, c=c: (0, c))
    dh2, dg2 = _norm_matmul_bwd(du, lw["w_up"], du, lw["w_up"], sv["h2"], lw["g2"], dh3,
                                (half(0), whalf(0), half(1), whalf(1)), tag + "ffn_up_dx")
    g = dict(norm2_g=dg2[0], ffn_w_up=dw_up, ffn_conv_w=jnp.concatenate([dwg[0:3], dwv[0:3]], axis=1),
             ffn_conv_b=jnp.concatenate([dbg[0], dbv[0]]), ffn_w_down=dw_down)
    return dh2, g


def _layer_bwd_mix(dh2, lw, sv, consts, folds, cos, sin, slopes, l):
    tag = f"l{l}_"
    tb = TILE_MM
    dgates, doa, dob, doc, dwb, dwo = _post_bwd(dh2, sv["proj"], sv["o"], lw["wb"], lw["wo"], tag + "merge_bwd")
    c_col, c_row = sv["c"]
    extras = ((c_row,), (), (lw["sinks"], slopes))
    extras_kv = ((c_col,), (), (slopes,))
    grads = []
    for n, (mode, do) in enumerate((("fox", doa), ("mla", dob), ("swa", doc))):
        att = _Att(mode)
        q, k, v = sv["q"][n], sv["k"][n], sv["v"][n]
        res = _att_dq(att, q, k, v, sv["o"][n], do, sv["lse"][n], extras[n], tag + mode + "_dq")
        dq, delta = res[0], res[1]
        res_kv = _att_dkv(att, q, k, v, do, _rows(sv["lse"][n]), _rows(delta), extras_kv[n], tag + mode + "_dkv")
        grads.append((dq, res_kv[0], res_kv[1], res[2:], res_kv[2:]))
    (dfq, dfk, dfv, (dcq,), (dck,)), (dmq, dmk, dmv, _, _), (dsq, dskp, dsvp, (dsink,), _) = grads
    dls = _cumsum([_from_cols(dcq), _from_cols(dck)], True, tag + "decay_cumsum_bwd")
    res = _prep_bwd(sv["proj"], lw["prm"], consts, cos, sin,
                    (dfq, dfk, dfv, dmq, dmk, dmv, dsq, dskp, dsvp, dls), folds, tag + "prep_bwd")
    dother, pg = res[0], res[1:]
    dw_g = _matmul_tn(sv["xn"], dgates, GATES_W, tag + "in_proj_dw_gates")
    dw_o = _matmul_tn(sv["xn"], dother, OTHER_W, tag + "in_proj_dw_other")
    full = lambda w: pl.BlockSpec((tb, w), lambda i: (i, 0))
    wfull = lambda w: pl.BlockSpec((D_MODEL, w), lambda i: (0, 0))
    dh, dg1 = _norm_matmul_bwd(dgates, lw["w_in"][:, :GATES_W], dother, lw["w_in"][:, GATES_W:], sv["h"], lw["g1"], dh2,
                               (full(GATES_W), wfull(GATES_W), full(OTHER_W), wfull(OTHER_W)), tag + "in_proj_dx")
    d_in = jnp.concatenate([
        dw_o[:, O_FQ:O_FV + 512], dw_o[:, O_MISC + FF_LANE:O_MISC + FF_LANE + 8], dw_o[:, O_CQ:O_CQ + 256],
        dw_o[:, O_CKV:O_CKV + 128], dw_o[:, O_MISC:O_MISC + 32], dw_o[:, O_SQ:O_SQ + 512], dw_o[:, O_SK:O_SK + 128],
        dw_o[:, O_SV:O_SV + 128], dw_g], axis=1)
    d_wq = pg[9].reshape(256, HEADS, 128)[:, :, :96].reshape(256, 768)
    d_wkv = jnp.concatenate([pg[10].reshape(128, HEADS, 128)[:, :, :64], pg[11].reshape(128, HEADS, 64)],
                            axis=2).reshape(128, 1024)
    g = dict(
        norm1_g=dg1[0], w_in=d_in, fox_forget_b=pg[4][0, FF_LANE:FF_LANE + 8], fox_q_g=pg[0][0, :64],
        fox_k_g=pg[1][0, :64], mla_q_a_g=pg[5][0], mla_w_q_up=d_wq, mla_kv_a_g=pg[6][0], mla_w_kv_up=d_wkv,
        mla_q_g=pg[7][0, :96], mla_k_g=pg[8][0, :96], swa_q_g=pg[2][0, :64], swa_k_g=pg[3][0, :64],
        swa_sinks=dsink[:, 0, 0:2 * PAIRS].reshape(HEADS), w_branch=dwb, w_o=dwo)
    return dh, g


_MIX_BIG = ("w_in", "mla_w_q_up", "mla_w_kv_up", "w_branch", "w_o")
_FFN_BIG = ("ffn_w_up", "ffn_w_down")


def _local_step(x, target, w, hook=None, fetch=None):
    if fetch is None:
        fetch = lambda l, stage, after: {n: w[n][l] for n in (_MIX_BIG if stage == "mix" else _FFN_BIG)}
    seq = x.shape[0]
    length = N_META + seq
    lp = -(-length // ROW_ALIGN) * ROW_ALIGN
    pad = lp - length
    h = jnp.concatenate([w["meta_tokens"].astype(F32), x, jnp.zeros((pad, D_MODEL), F32)], axis=0)
    tgt = jnp.pad(target, ((N_META, pad), (0, 0)))
    consts = _consts()
    folds = (_fold_matrix(512, 64), _fold_matrix(1024, 128))
    cos, sin = _rope_tables(lp)
    slopes = jnp.asarray(2.0 ** (-8.0 * np.arange(1, HEADS + 1, dtype=np.float32) / HEADS), F32)
    lws, saved = [], []
    for l in range(DEPTH):
        lw = _mix_params(w, fetch(l, "mix", h), l)
        h, sv = _layer_fwd_mix(h, lw, consts, cos, sin, slopes, l)
        lw.update(_ffn_params(w, fetch(l, "ffn", h), l))
        h, sv_ffn = _layer_fwd_ffn(h, lw, l)
        lws.append(lw)
        saved.append({**sv, **sv_ffn})
    dh, loss = _loss_head(h, tgt, seq, "loss_head")
    grads = [None] * DEPTH
    for l in reversed(range(DEPTH)):
        dh, g_ffn = _layer_bwd_ffn(dh, lws[l], saved[l], l)
        tick = hook(l, "ffn", g_ffn) if hook else None
        if tick is not None:
            lws[l]["sinks"] = lws[l]["sinks"] + tick
        dh, g_mix = _layer_bwd_mix(dh, lws[l], saved[l], consts, folds, cos, sin, slopes, l)
        grads[l] = {**g_ffn, **g_mix}
        tick = hook(l, "mix", grads[l]) if hook else None
        if tick is not None and l > 0:
            lws[l - 1]["cw"] = lws[l - 1]["cw"] + tick
    return loss, dh[N_META:length], dh[:N_META], grads


def _place():
    return lax.axis_index("x"), lax.axis_index("y"), lax.axis_index("c")


def _flip(pos, k):
    x, y, c = pos
    return (1 - x if k & 4 else x, 1 - y if k & 2 else y, 1 - c if k & 1 else c)


def _index(pos):
    return 4 * pos[0] + 2 * pos[1] + pos[2]


def _gather(tensors, name):
    n_t = len(tensors)

    def body(*refs):
        ins, outs = refs[:n_t], refs[n_t:2 * n_t]
        send_sems, recv_sems, local_sems = refs[2 * n_t:]
        x, y, c = _place()
        me, sibling = (x, y, c), (x, y, 1 - c)
        chips = [(1 - x, y), (x, 1 - y), (1 - x, 1 - y)]

        def copy(t, k, block, to, src=None):
            dst = outs[t].at[_index(block)]
            return pltpu.make_async_remote_copy(
                src_ref=dst if src is None else src, dst_ref=dst, send_sem=send_sems.at[t, k],
                recv_sem=recv_sems.at[t, k], device_id=to, device_id_type=pl.DeviceIdType.MESH)

        local, sent = [], []
        for t in range(n_t):
            local.append(pltpu.make_async_copy(ins[t], outs[t].at[_index(me)], local_sems.at[t]))
            local[-1].start()
            sent.append(copy(t, 0, me, sibling, src=ins[t]))
            sent += [copy(t, 1 + j, me, (*chip, c), src=ins[t]) for j, chip in enumerate(chips)]
        for cp in sent:
            cp.start()
        for j, chip in enumerate(chips):
            for t in range(n_t):
                copy(t, 1 + j, (*chip, c), me).wait_recv()
                sent.append(copy(t, 4 + j, (*chip, c), sibling))
                sent[-1].start()
        for t in range(n_t):
            copy(t, 0, sibling, me).wait_recv()
            for j, chip in enumerate(chips):
                copy(t, 4 + j, (*chip, 1 - c), me).wait_recv()
        for cp in sent:
            cp.wait_send()
        for cp in local:
            cp.wait()

    any_spec = pl.BlockSpec(memory_space=pl.ANY)
    return pl.pallas_call(
        body, name=name, in_specs=[any_spec] * n_t, out_specs=[any_spec] * n_t,
        out_shape=[SDS((N_DEV,) + a.shape, a.dtype) for a in tensors],
        scratch_shapes=[pltpu.SemaphoreType.DMA((n_t, N_DEV - 1)), pltpu.SemaphoreType.DMA((n_t, N_DEV - 1)),
                        pltpu.SemaphoreType.DMA((n_t,))],
    )(*tensors)


def _exchange_start(tensors, name, gather=False, after=None):
    n_t = len(tensors)

    def body(*refs):
        ins, lands = refs[:n_t], refs[n_t:2 * n_t]
        send_sem, recv_sem = refs[2 * n_t + 1:2 * n_t + 3]
        token = refs[-1]
        me = _place()
        mine = _index(me)
        for t in range(n_t):
            for k in range(1, N_DEV):
                peer = _flip(me, k)
                pltpu.make_async_remote_copy(
                    src_ref=ins[t] if gather else ins[t].at[_index(peer)], dst_ref=lands[t].at[mine],
                    send_sem=send_sem, recv_sem=recv_sem, device_id=peer, device_id_type=pl.DeviceIdType.MESH).start()
        token[...] = jnp.zeros_like(token)

    hbm = pl.BlockSpec(memory_space=pltpu.HBM)
    sem = pl.BlockSpec(memory_space=pltpu.SEMAPHORE)
    one = pltpu.SemaphoreType.DMA(())
    land_shape = lambda a: ((N_DEV,) + a.shape) if gather else a.shape
    bufs = ([pltpu.HBM(a.shape, a.dtype) for a in tensors] + [pltpu.HBM(land_shape(a), a.dtype) for a in tensors])
    after = jnp.zeros((8, 128), F32) if after is None else after
    outs = pl.pallas_call(
        body, name=name, in_specs=[hbm] * (2 * n_t) + [pl.BlockSpec(memory_space=pl.ANY)],
        out_specs=[sem, sem] + [hbm] * (2 * n_t) + [pl.BlockSpec(memory_space=pltpu.VMEM)],
        out_shape=[one, one] + bufs + [SDS((8, 128), F32)],
        input_output_aliases={i: 2 + i for i in range(2 * n_t)},
        compiler_params=pltpu.CompilerParams(has_side_effects=pltpu.SideEffectType.DATAFLOW_SIDE_EFFECTING),
    )(*[pltpu.with_memory_space_constraint(a, pltpu.HBM) for a in tensors],
      *[pltpu.with_memory_space_constraint(lax.empty(land_shape(a), a.dtype), pltpu.HBM) for a in tensors], after)
    return outs[:-1], outs[-1][0, 0]


def _exchange_wait(state, after, name, gather=False):
    n_t = (len(state) - 2) // 2

    def body(*refs):
        send_sem, recv_sem = refs[0:2]
        ins, lands = refs[2:2 + n_t], refs[2 + n_t:2 + 2 * n_t]
        me = _place()
        for t in range(n_t):
            for k in range(1, N_DEV):
                peer = _flip(me, k)
                copy = pltpu.make_async_remote_copy(
                    src_ref=ins[t] if gather else ins[t].at[_index(peer)], dst_ref=lands[t].at[_index(peer)],
                    send_sem=send_sem, recv_sem=recv_sem, device_id=peer, device_id_type=pl.DeviceIdType.MESH)
                copy.wait_send()
                copy.wait_recv()

    hbm = pl.BlockSpec(memory_space=pltpu.HBM)
    sem = pl.BlockSpec(memory_space=pltpu.SEMAPHORE)
    bufs = [pltpu.HBM(a.shape, a.dtype) for a in state[2:]]
    outs = pl.pallas_call(
        body, name=name, in_specs=[sem, sem] + [hbm] * (2 * n_t) + [pl.BlockSpec(memory_space=pl.ANY)],
        out_specs=[hbm] * (2 * n_t), out_shape=bufs,
        input_output_aliases={2 + i: i for i in range(2 * n_t)},
        compiler_params=pltpu.CompilerParams(has_side_effects=pltpu.SideEffectType.DATAFLOW_SIDE_EFFECTING),
    )(*state, after)
    return outs[n_t:]


def _sum_slots(parts, name):
    n, rows, w = parts.shape
    tb = 8

    def body(p_ref, o_ref):
        acc = p_ref[0].astype(F32)
        for s in range(1, n):
            acc = acc + p_ref[s].astype(F32)
        o_ref[...] = acc

    return pl.pallas_call(
        body, name=name, grid=(rows // tb,),
        in_specs=[pl.BlockSpec((n, tb, w), lambda i: (0, i, 0))], out_specs=pl.BlockSpec((tb, w), lambda i: (i, 0)),
        out_shape=SDS((rows, w), F32), compiler_params=_params(("parallel",)),
    )(parts)


def _adamw(wt, m, v, parts, name, own=None):
    shape = wt.shape
    parts = parts if isinstance(parts, (list, tuple)) else [parts]
    n, w = parts[0].shape[0], shape[-1]
    rows = math.prod(shape[:-1])
    per = rows // len(parts)
    step = 16 if parts[0].dtype == BF16 else 8
    tb = max([t for t in range(step, 257, step) if per % t == 0] or [per])
    nb = per // tb
    c1 = 1.0 / (1.0 - ADAM_B1 ** ADAM_STEP)
    c2 = 1.0 / (1.0 - ADAM_B2 ** ADAM_STEP)
    state = [a.reshape(rows, w) for a in (wt, m, v)]
    n_in = 4 if own is None else 5
    outs = None
    for l in reversed(range(len(parts))):
        def body(*refs):
            idx_ref = None if own is None else refs[0]
            w_ref, m_ref, v_ref, p_ref = refs[n_in - 4:n_in] if own is None else refs[1:5]
            g_out, d_out, m_out, v_out = refs[-4:]
            g = None
            for s in range(n):
                term = p_ref[s] if own is None else jnp.where(idx_ref[0] == s, refs[5][0], p_ref[s])
                g = term.astype(F32) if g is None else g + term.astype(F32)
            m_new = ADAM_B1 * m_ref[...] + (1.0 - ADAM_B1) * g
            v_new = ADAM_B2 * v_ref[...] + (1.0 - ADAM_B2) * (g * g)
            g_out[...] = g
            m_out[...] = m_new
            v_out[...] = v_new
            d_out[...] = -ADAM_LR * ((m_new * c1) / (jnp.sqrt(v_new * c2) + ADAM_EPS) + ADAM_WD * w_ref[...])

        row = pl.BlockSpec((tb, w), lambda i, *_, l=l: (l * nb + i, 0))
        in_specs = [row, row, row, pl.BlockSpec((n, tb, w), lambda i, *_: (0, i, 0))]
        args = [*state, parts[l].reshape(n, per, w)]
        if own is not None:
            in_specs.append(pl.BlockSpec((1, tb, w), lambda i, idx: (idx[0], i, 0)))
            args.append(own[l].reshape(n, per, w))
        prev = [] if outs is None else list(outs)
        in_specs += [pl.BlockSpec(memory_space=pl.ANY)] * len(prev)
        n_pre = 0 if own is None else 1
        call = dict(name=f"{name}_{l}", out_shape=[SDS((rows, w), F32)] * 4,
                    input_output_aliases={n_pre + len(args) + k: k for k in range(len(prev))},
                    compiler_params=_params(("parallel",)))
        if own is None:
            outs = pl.pallas_call(body, grid=(nb,), in_specs=in_specs, out_specs=[row] * 4, **call)(*args, *prev)
        else:
            spec = pltpu.PrefetchScalarGridSpec(num_scalar_prefetch=1, grid=(nb,), in_specs=in_specs, out_specs=[row] * 4)
            idx = jnp.reshape(_index(_place()), (1,)).astype(jnp.int32)
            outs = pl.pallas_call(body, grid_spec=spec, **call)(idx, *args, *prev)
    return [o.reshape(shape) for o in outs]


_BIG = [("w_in", 2), ("mla_w_q_up", 2), ("mla_w_kv_up", 2), ("w_branch", 3), ("w_o", 1), ("ffn_w_up", 2), ("ffn_w_down", 1)]
_SMALL_SHARDED = [("meta_tokens", 1), ("ffn_conv_w", 2)]
_REPLICATED = ["norm1_g", "fox_forget_b", "fox_q_g", "fox_k_g", "mla_q_a_g", "mla_kv_a_g", "mla_q_g", "mla_k_g",
               "swa_q_g", "swa_k_g", "swa_sinks", "norm2_g", "ffn_conv_b"]
_ORDER = ["meta_tokens", "norm1_g", "w_in", "fox_forget_b", "fox_q_g", "fox_k_g", "mla_q_a_g", "mla_w_q_up",
          "mla_kv_a_g", "mla_w_kv_up", "mla_q_g", "mla_k_g", "swa_q_g", "swa_k_g", "swa_sinks", "w_branch", "w_o",
          "norm2_g", "ffn_w_up", "ffn_conv_w", "ffn_conv_b", "ffn_w_down"]


def _flat_rows(vecs, dtype, row_mult):
    flat = jnp.concatenate([a.reshape(-1).astype(dtype) for a in vecs])
    rows = -(-flat.shape[0] // (1024 * row_mult)) * row_mult
    return jnp.pad(flat, (0, rows * 1024 - flat.shape[0])).reshape(rows, 1024)


def _unflatten(flat, shapes):
    out, off = [], 0
    for s in shapes:
        n = math.prod(s)
        out.append(flat[off:off + n].reshape(s))
        off += n
    return out


def _to_full(blocks, axis):
    moved = jnp.moveaxis(blocks, 0, axis)
    s = moved.shape
    return moved.reshape(s[:axis] + (s[axis] * s[axis + 1],) + s[axis + 2:])


def _to_blocks(full, axis):
    s = full.shape
    split = full.reshape(s[:axis] + (N_DEV, s[axis] // N_DEV) + s[axis + 1:])
    return jnp.moveaxis(split, axis, 0)


def kernel(x, meta_tokens, norm1_g, w_in, fox_forget_b, fox_q_g, fox_k_g, mla_q_a_g, mla_w_q_up, mla_kv_a_g, mla_w_kv_up, mla_q_g, mla_k_g, swa_q_g, swa_k_g, swa_sinks, w_branch, w_o, norm2_g, ffn_w_up, ffn_conv_w, ffn_conv_b, ffn_w_down, loss_target, m_meta_tokens, m_norm1_g, m_w_in, m_fox_forget_b, m_fox_q_g, m_fox_k_g, m_mla_q_a_g, m_mla_w_q_up, m_mla_kv_a_g, m_mla_w_kv_up, m_mla_q_g, m_mla_k_g, m_swa_q_g, m_swa_k_g, m_swa_sinks, m_w_branch, m_w_o, m_norm2_g, m_ffn_w_up, m_ffn_conv_w, m_ffn_conv_b, m_ffn_w_down, v_meta_tokens, v_norm1_g, v_w_in, v_fox_forget_b, v_fox_q_g, v_fox_k_g, v_mla_q_a_g, v_mla_w_q_up, v_mla_kv_a_g, v_mla_w_kv_up, v_mla_q_g, v_mla_k_g, v_swa_q_g, v_swa_k_g, v_swa_sinks, v_w_branch, v_w_o, v_norm2_g, v_ffn_w_up, v_ffn_conv_w, v_ffn_conv_b, v_ffn_w_down):
    wl = dict(zip(_ORDER, (meta_tokens, norm1_g, w_in, fox_forget_b, fox_q_g, fox_k_g, mla_q_a_g, mla_w_q_up,
                           mla_kv_a_g, mla_w_kv_up, mla_q_g, mla_k_g, swa_q_g, swa_k_g, swa_sinks, w_branch, w_o,
                           norm2_g, ffn_w_up, ffn_conv_w, ffn_conv_b, ffn_w_down)))
    ml = dict(zip(_ORDER, (m_meta_tokens, m_norm1_g, m_w_in, m_fox_forget_b, m_fox_q_g, m_fox_k_g, m_mla_q_a_g,
                           m_mla_w_q_up, m_mla_kv_a_g, m_mla_w_kv_up, m_mla_q_g, m_mla_k_g, m_swa_q_g, m_swa_k_g,
                           m_swa_sinks, m_w_branch, m_w_o, m_norm2_g, m_ffn_w_up, m_ffn_conv_w, m_ffn_conv_b,
                           m_ffn_w_down)))
    vl = dict(zip(_ORDER, (v_meta_tokens, v_norm1_g, v_w_in, v_fox_forget_b, v_fox_q_g, v_fox_k_g, v_mla_q_a_g,
                           v_mla_w_q_up, v_mla_kv_a_g, v_mla_w_kv_up, v_mla_q_g, v_mla_k_g, v_swa_q_g, v_swa_k_g,
                           v_swa_sinks, v_w_branch, v_w_o, v_norm2_g, v_ffn_w_up, v_ffn_conv_w, v_ffn_conv_b,
                           v_ffn_w_down)))
    small_sh = [n for n, _ in _SMALL_SHARDED]
    big = [n for n, _ in _BIG]
    axis_of = dict(_BIG)
    idx = _index(_place())

    def to_full(n, blocks, own=None):
        if own is not None:
            sel = (jnp.arange(N_DEV) == idx).reshape((N_DEV,) + (1,) * own.ndim)
            blocks = jnp.where(sel, own[None], blocks)
        return _to_full(blocks, axis_of[n] - 1)

    local = {(n, l): wl[n][l].astype(BF16) for n in big for l in range(DEPTH)}
    got = _gather([local[(n, 0)] for n in _MIX_BIG] + [wl[n] for n in small_sh], "gather_weights_l0_mix")
    full = {n: wl[n] for n in _REPLICATED}
    for (n, axis), blocks in zip(_SMALL_SHARDED, got[len(_MIX_BIG):]):
        full[n] = _to_full(blocks, axis)
    ready = {(n, 0): to_full(n, blocks) for n, blocks in zip(_MIX_BIG, got)}
    later = {"l0_ffn": [(n, 0) for n in _FFN_BIG], "l1": [(n, 1) for n in big]}
    states = {}
    for key, names in later.items():
        states[key], _ = _exchange_start([local[e] for e in names], "gather_weights_" + key + "_start", True, got[0])

    def fetch(l, stage, after):
        key = "l0_ffn" if l == 0 else "l1"
        if (l, stage) != (0, "mix") and key in states:
            lands = _exchange_wait(states.pop(key), after, "gather_weights_" + key + "_wait", True)
            ready.update({e: to_full(e[0], blocks, local[e]) for e, blocks in zip(later[key], lands)})
        return {n: ready[(n, l)] for n in (_MIX_BIG if stage == "mix" else _FFN_BIG)}

    blocks_of = lambda g, names: [_to_blocks(g[n], axis_of[n] - 1).astype(BF16) for n in names]
    early = {}

    def hook(l, stage, g):
        if l == DEPTH - 1 and stage == "mix":
            key, names = "l1", big
        elif l == 0:
            key, names = "l0_" + stage, (_FFN_BIG if stage == "ffn" else _MIX_BIG)
        else:
            return None
        sends = blocks_of(g, names)
        state, tick = _exchange_start(sends, "exchange_grads_" + key + "_start")
        early[key] = (names, l, sends, state)
        return tick

    loss, grad_x, grad_meta, grads = _local_step(x[0], loss_target[0], full, hook, fetch)
    result = {kind: {} for kind in ("grad", "delta", "new_m", "new_v")}
    landed, sent = {}, {}
    after = early["l0_mix"][2][0]
    for key in ("l1", "l0_ffn"):
        names, l, sends, state = early[key]
        got = _exchange_wait(state, after, "exchange_grads_" + key + "_wait")
        landed.update({(n, l): p for n, p in zip(names, got)})
        sent.update({(n, l): p for n, p in zip(names, sends)})

    def update(names):
        for n in names:
            outs = _adamw(wl[n], ml[n], vl[n], [landed[(n, l)] for l in range(DEPTH)], "adamw_" + n,
                          [sent[(n, l)] for l in range(DEPTH)])
            for kind, val in zip(result, outs):
                result[kind][n] = val

    update(_FFN_BIG)
    grads = {k: jnp.stack([grads[l][k] for l in range(DEPTH)]) for k in grads[0] if k not in big}
    grads["meta_tokens"] = grad_meta

    small_full = _REPLICATED + small_sh
    mine_small = _flat_rows([grads[n] for n in small_full] + [loss], F32, 8)
    total_small = _sum_slots(_gather([mine_small], "gather_small_grads")[0], "sum_small_grads").reshape(-1)
    pieces = _unflatten(total_small, [grads[n].shape for n in small_full] + [()])
    loss_total = pieces[-1]
    g_small = dict(zip(small_full, pieces[:-1]))
    for n, axis in _SMALL_SHARDED:
        size = wl[n].shape[axis]
        g_small[n] = lax.dynamic_slice_in_dim(g_small[n], idx * size, size, axis)
    flat = lambda d: _flat_rows([d[n] for n in small_full], F32, 8)
    small_out = _adamw(flat(wl), flat(ml), flat(vl), flat(g_small)[None], "adamw_small")
    for kind, fs in zip(result, small_out):
        result[kind].update(zip(small_full, _unflatten(fs.reshape(-1), [wl[n].shape for n in small_full])))
    names, l, sends, state = early["l0_mix"]
    got = _exchange_wait(state, small_out[0], "exchange_grads_l0_mix_wait")
    landed.update({(n, l): p for n, p in zip(names, got)})
    sent.update({(n, l): p for n, p in zip(names, sends)})
    update(_MIX_BIG)
    outs = [loss_total, grad_x[None]]
    for kind in ("grad", "delta", "new_m", "new_v"):
        outs += [result[kind][n] for n in _ORDER]
    return tuple(outs)
```

```python
import functools
import math

import numpy as np
import jax
import jax.numpy as jnp
from jax import lax
from jax.experimental import pallas as pl
from jax.experimental.pallas import tpu as pltpu

F32, BF16 = jnp.float32, jnp.bfloat16
SDS = jax.ShapeDtypeStruct

D_MODEL = 1024
N_META = 16
EPS = 1e-6
WINDOW = 128
ROPE_THETA = 10000.0
HEADS = 8
D_FF = 2816
DEPTH = 2
N_DEV = 8
ADAM_LR, ADAM_B1, ADAM_B2, ADAM_EPS, ADAM_WD, ADAM_STEP = 0.001, 0.9, 0.999, 1e-08, 0.01, 10

ROW_ALIGN = 384
TILE_MM = 384
TILE_ROW = 128
TILE_ATT = 384
TILE_POST = 128
PAIRS = 2
VMEM_LIMIT = 56 * 1024 * 1024

GATES_W = 3072
OTHER_W = 2816
IN_W = GATES_W + OTHER_W
O_FQ, O_FK, O_FV, O_SQ, O_SK, O_SV, O_CQ, O_CKV, O_MISC = 0, 512, 1024, 1536, 2048, 2176, 2304, 2560, 2688
FF_LANE = 32

NEG = -1e30


def _dot(a, b):
    return jnp.dot(a, b, preferred_element_type=F32)


def _dot_nt(a, b):
    return lax.dot_general(a, b, (((1,), (1,)), ((), ())), preferred_element_type=F32)


def _dot_tn(a, b):
    return lax.dot_general(a, b, (((0,), (0,)), ((), ())), preferred_element_type=F32)


def _params(sem):
    return pltpu.CompilerParams(dimension_semantics=sem, vmem_limit_bytes=VMEM_LIMIT)


def _rms(x, g):
    return x * lax.rsqrt(jnp.mean(x * x, axis=-1, keepdims=True) + EPS) * g


def _split_dot(x, m, pieces=2):
    acc, rest = None, x
    for _ in range(pieces):
        part = rest.astype(BF16)
        rest = rest - part.astype(F32)
        acc = _dot(part, m) if acc is None else acc + _dot(part, m)
    return acc


@jax.custom_vjp
def _sel(x, m, mt):
    return _split_dot(x, m)


_sel.defvjp(lambda x, m, mt: (_split_dot(x, m), (m, mt)), lambda res, dy: (_split_dot(dy, res[1]), None, None))


@jax.custom_vjp
def _mm(x, w):
    return _dot(x.astype(BF16), w.astype(BF16))


def _mm_bwd(res, dy):
    x, w = res
    dyb = dy.astype(BF16)
    return _dot_nt(dyb, w.astype(BF16)), _dot_tn(x.astype(BF16), dyb)


_mm.defvjp(lambda x, w: (_mm(x, w), (x, w)), _mm_bwd)


def _rot_impl(x):
    w = x.shape[1]
    lane = lax.broadcasted_iota(jnp.int32, x.shape, 1) % 128
    lo = (lane >= 64) & (lane < 80)
    hi = (lane >= 80) & (lane < 96)
    return jnp.where(hi, pltpu.roll(x, 16, 1), 0.0) - jnp.where(lo, pltpu.roll(x, w - 16, 1), 0.0)


@jax.custom_vjp
def _rot(x):
    return _rot_impl(x)


_rot.defvjp(lambda x: (_rot_impl(x), None), lambda _, dy: (-_rot_impl(dy),))


def _gnorm(x, g, e, et, dim):
    inv = lax.rsqrt(_sel(x * x, e, et) * (1.0 / dim) + EPS)
    return x * _sel(inv, et, e) * g


def _indicator(width, period):
    m = np.zeros((width, 128), np.float32)
    m[np.arange(width), np.arange(width) // period] = 1.0
    return m


def _consts():
    e64 = _indicator(512, 64)
    e128 = _indicator(1024, 128)
    sk = np.zeros((128, 1024), np.float32)
    for h in range(HEADS):
        sk[np.arange(32), 128 * h + 64 + np.arange(32)] = 1.0
    dup = np.zeros((128, 256), np.float32)
    for g in range(2):
        for r in range(2):
            dup[64 * g + np.arange(64), 128 * g + 64 * r + np.arange(64)] = 1.0
    mats = [e64, e64.T, e128, e128.T, sk, sk.T, dup, dup.T]
    return [jnp.asarray(m, BF16) for m in mats]


def _fold_matrix(width, period):
    m = np.zeros((width, 128), np.float32)
    m[np.arange(width), np.arange(width) % period] = 1.0
    return jnp.asarray(m, BF16)


def _rope_tables(lp):
    half = 16
    freqs = ROPE_THETA ** (-np.arange(half, dtype=np.float32) / half)
    ang = np.arange(lp, dtype=np.float32)[:, None] * freqs[None, :]
    cos = np.ones((lp, 128), np.float32)
    sin = np.zeros((lp, 128), np.float32)
    cos[:, 64:80] = np.cos(ang)
    cos[:, 80:96] = np.cos(ang)
    sin[:, 64:80] = np.sin(ang)
    sin[:, 80:96] = np.sin(ang)
    return jnp.asarray(cos), jnp.asarray(sin)


def _norm_matmul(h, g, w, tn, name):
    lp, d = h.shape
    n = w.shape[1]
    tb = TILE_MM

    def body(h_ref, g_ref, w_ref, xn_ref, y_ref):
        @pl.when(pl.program_id(1) == 0)
        def _():
            xn_ref[...] = _rms(h_ref[...], g_ref[...]).astype(BF16)

        y_ref[...] = _dot(xn_ref[...], w_ref[...])

    return pl.pallas_call(
        body, name=name, grid=(lp // tb, n // tn),
        in_specs=[pl.BlockSpec((tb, d), lambda i, j: (i, 0)), pl.BlockSpec((1, d), lambda i, j: (0, 0)),
                  pl.BlockSpec((d, tn), lambda i, j: (0, j))],
        out_specs=[pl.BlockSpec((tb, d), lambda i, j: (i, 0)), pl.BlockSpec((tb, tn), lambda i, j: (i, j))],
        out_shape=[SDS((lp, d), BF16), SDS((lp, n), F32)],
        compiler_params=_params(("parallel", "arbitrary")),
    )(h, g, w)


def _matmul_residual(a, w, res, name):
    m, k = a.shape
    n = w.shape[1]
    tb = TILE_MM

    def body(a_ref, w_ref, r_ref, o_ref):
        o_ref[...] = r_ref[...] + _dot(a_ref[...], w_ref[...])

    return pl.pallas_call(
        body, name=name, grid=(m // tb,),
        in_specs=[pl.BlockSpec((tb, k), lambda i: (i, 0)), pl.BlockSpec((k, n), lambda i: (0, 0)),
                  pl.BlockSpec((tb, n), lambda i: (i, 0))],
        out_specs=pl.BlockSpec((tb, n), lambda i: (i, 0)),
        out_shape=SDS((m, n), F32),
        compiler_params=_params(("parallel",)),
    )(a, w, res)


def _matmul_nt(dy, w, tn, name):
    m, k = dy.shape
    n = w.shape[0]
    tb = TILE_MM

    def body(dy_ref, w_ref, o_ref):
        o_ref[...] = _dot_nt(dy_ref[...].astype(BF16), w_ref[...])

    return pl.pallas_call(
        body, name=name, grid=(m // tb, n // tn),
        in_specs=[pl.BlockSpec((tb, k), lambda i, j: (i, 0)), pl.BlockSpec((tn, k), lambda i, j: (j, 0))],
        out_specs=pl.BlockSpec((tb, tn), lambda i, j: (i, j)),
        out_shape=SDS((m, n), F32),
        compiler_params=_params(("parallel", "arbitrary")),
    )(dy, w)


def _matmul_tn(x, dy, tn, name):
    m, k = x.shape
    n = dy.shape[1]
    tb = TILE_MM

    def body(x_ref, dy_ref, o_ref):
        @pl.when(pl.program_id(1) == 0)
        def _():
            o_ref[...] = jnp.zeros_like(o_ref)

        o_ref[...] += _dot_tn(x_ref[...].astype(BF16), dy_ref[...].astype(BF16))

    return pl.pallas_call(
        body, name=name, grid=(n // tn, m // tb),
        in_specs=[pl.BlockSpec((tb, k), lambda j, i: (i, 0)), pl.BlockSpec((tb, tn), lambda j, i: (i, j))],
        out_specs=pl.BlockSpec((k, tn), lambda j, i: (0, j)),
        out_shape=SDS((k, n), F32),
        compiler_params=_params(("parallel", "arbitrary")),
    )(x, dy)


def _norm_matmul_bwd(dy1, w1, dy2, w2, x, g, dres, specs, name):
    m, d = x.shape
    tb = TILE_MM
    (dy1_spec, w1_spec, dy2_spec, w2_spec) = specs

    def body(dy1_ref, w1_ref, dy2_ref, w2_ref, x_ref, g_ref, r_ref, o_ref, dg_ref):
        @pl.when(pl.program_id(0) == 0)
        def _():
            dg_ref[...] = jnp.zeros_like(dg_ref)

        dxn = _dot_nt(dy1_ref[...], w1_ref[...]) + _dot_nt(dy2_ref[...], w2_ref[...])
        _, vjp = jax.vjp(_rms, x_ref[...], g_ref[...])
        dx, dg = vjp(dxn)
        o_ref[...] = r_ref[...] + dx
        dg_ref[...] += dg

    row = pl.BlockSpec((tb, d), lambda i: (i, 0))
    vec = pl.BlockSpec((1, d), lambda i: (0, 0))
    return pl.pallas_call(
        body, name=name, grid=(m // tb,),
        in_specs=[dy1_spec, w1_spec, dy2_spec, w2_spec, row, vec, row],
        out_specs=[row, vec],
        out_shape=[SDS((m, d), F32), SDS((1, d), F32)],
        compiler_params=_params(("arbitrary",)),
    )(dy1, w1, dy2, w2, x, g, dres)


def _prep_math(pieces, prm, consts, cos, sin):
    fq, fk, sq, sk, sv, cq, ckv, misc = pieces
    gfq, gfk, gsq, gsk, fb, gqa, gkva, gmq, gmk, wq, wkk, wkv = prm
    e64, e64t, e128, e128t, skm, skt, dup, dupt = consts
    cos8 = jnp.concatenate([cos] * HEADS, axis=1)
    sin8 = jnp.concatenate([sin] * HEADS, axis=1)
    fq_n = _gnorm(fq, gfq, e64, e64t, 64)
    fk_n = _gnorm(fk, gfk, e64, e64t, 64)
    ls = jax.nn.log_sigmoid(misc + fb)
    q = _gnorm(_mm(_rms(cq, gqa), wq), gmq, e128, e128t, 96)
    mq = q * cos8 + _rot(q) * sin8
    kva = _rms(ckv, gkva)
    k = _gnorm(_mm(kva, wkk) + _sel(misc, skm, skt), gmk, e128, e128t, 96)
    mk = k * cos8 + _rot(k) * sin8
    mv = _mm(kva, wkv)
    sq_n = _gnorm(sq, gsq, e64, e64t, 64)
    sk_n = _gnorm(sk, gsk, e64[0:128], e64t[:, 0:128], 64)
    skd = _sel(sk_n, dup, dupt)
    svd = _sel(sv, dup, dupt)
    return fq_n, fk_n, ls, mq, mk, mv, sq_n, skd, svd


_PIECES = [(O_FQ, 512), (O_FK, 512), (O_SQ, 512), (O_SK, 128), (O_SV, 128), (O_CQ, 256), (O_CKV, 128), (O_MISC, 128)]
_PRM_SHAPES = [(1, 512), (1, 512), (1, 512), (1, 128), (1, 128), (1, 256), (1, 128), (1, 1024), (1, 1024),
               (256, 1024), (128, 1024), (128, 512)]
_CONST_SHAPES = [(512, 128), (128, 512), (1024, 128), (128, 1024), (128, 1024), (1024, 128), (128, 256), (256, 128)]


def _piece_specs(tb):
    def spec(off, width):
        blk = (GATES_W + off) // width
        return pl.BlockSpec((tb, width), lambda i, blk=blk: (i, blk))
    return [spec(o, w) for o, w in _PIECES] + [spec(O_FV, 512)]


def _full_specs(shapes):
    return [pl.BlockSpec(s, lambda i: (0, 0)) for s in shapes]


def _prep_fwd(proj, prm, consts, cos, sin, name):
    lp = proj.shape[0]
    tb = TILE_ROW
    row = lambda w: pl.BlockSpec((tb, w), lambda i: (i, 0))

    def body(*refs):
        pieces = [r[...] for r in refs[0:8]]
        fv = refs[8][...]
        prm_v = [r[...] for r in refs[9:21]]
        consts_v = [r[...] for r in refs[21:29]]
        cos_v, sin_v = refs[29][...], refs[30][...]
        outs = refs[31:]
        fq_n, fk_n, ls, mq, mk, mv, sq_n, skd, svd = _prep_math(pieces, prm_v, consts_v, cos_v, sin_v)
        for ref, val in zip(outs, (fq_n, fk_n, fv, mq, mk, mv, sq_n, skd, svd)):
            ref[...] = val.astype(BF16)
        outs[9][...] = ls

    widths = [512, 512, 512, 1024, 1024, 512, 512, 256, 256]
    return pl.pallas_call(
        body, name=name, grid=(lp // tb,),
        in_specs=_piece_specs(tb) + _full_specs(_PRM_SHAPES) + _full_specs(_CONST_SHAPES) + [row(128), row(128)],
        out_specs=[row(w) for w in widths] + [row(128)],
        out_shape=[SDS((lp, w), BF16) for w in widths] + [SDS((lp, 128), F32)],
        compiler_params=_params(("parallel",)),
    )(*([proj] * 9), *prm, *consts, cos, sin)


def _prep_bwd(proj, prm, consts, cos, sin, cots, folds, name):
    lp = proj.shape[0]
    tb = TILE_ROW
    row = lambda w: pl.BlockSpec((tb, w), lambda i: (i, 0))
    fold64, fold128 = folds

    def body(*refs):
        pieces = [r[...] for r in refs[0:8]]
        prm_v = [r[...] for r in refs[9:21]]
        consts_v = [r[...] for r in refs[21:29]]
        cos_v, sin_v = refs[29][...], refs[30][...]
        dfq, dfk, dfv, dmq, dmk, dmv, dsq, dskp, dsvp, dls = [r[...] for r in refs[31:41]]
        f64, f128 = refs[41][...], refs[42][...]
        d_ref = refs[43]
        g_refs = refs[44:]

        @pl.when(pl.program_id(0) == 0)
        def _():
            for r in g_refs:
                r[...] = jnp.zeros_like(r)

        def pair_sum(p):
            return jnp.concatenate([p[:, 0:128] + p[:, 128:256], p[:, 256:384] + p[:, 384:512]], axis=1)

        f = lambda pc, pr: _prep_math(pc, pr, consts_v, cos_v, sin_v)
        _, vjp = jax.vjp(f, pieces, prm_v)
        dpc, dprm = vjp((dfq, dfk, dls, dmq, dmk, dmv, dsq, pair_sum(dskp), pair_sum(dsvp)))
        d_fq, d_fk, d_sq, d_sk, d_sv, d_cq, d_ckv, d_misc = dpc
        for off, val in ((O_FQ, d_fq), (O_FK, d_fk), (O_FV, dfv), (O_SQ, d_sq), (O_SK, d_sk), (O_SV, d_sv),
                         (O_CQ, d_cq), (O_CKV, d_ckv), (O_MISC, d_misc)):
            d_ref[:, off:off + val.shape[1]] = val.astype(BF16)
        folded = {0: f64, 1: f64, 2: f64, 3: f64[0:128], 7: f128, 8: f128}
        for idx, (ref, val) in enumerate(zip(g_refs, dprm)):
            if idx in folded:
                ref[...] += _split_dot(jnp.broadcast_to(val, (8, val.shape[1])), folded[idx], 3)
            elif val.shape[0] == 1:
                ref[...] += jnp.broadcast_to(val, ref.shape)
            else:
                ref[...] += val

    g_shapes = [(8, 128), (8, 128), (8, 128), (8, 128), (8, 128), (8, 256), (8, 128), (8, 128), (8, 128),
                (256, 1024), (128, 1024), (128, 512)]
    cot_widths = [512, 512, 512, 1024, 1024, 512, 512, 512, 512, 128]
    return pl.pallas_call(
        body, name=name, grid=(lp // tb,),
        in_specs=(_piece_specs(tb) + _full_specs(_PRM_SHAPES) + _full_specs(_CONST_SHAPES) + [row(128), row(128)]
                  + [row(w) for w in cot_widths] + _full_specs([(512, 128), (1024, 128)])),
        out_specs=[row(OTHER_W)] + _full_specs(g_shapes),
        out_shape=[SDS((lp, OTHER_W), BF16)] + [SDS(s, F32) for s in g_shapes],
        compiler_params=_params(("arbitrary",)),
    )(*([proj] * 9), *prm, *consts, cos, sin, *cots, fold64, fold128)


def _cumsum(xs, reverse, name):
    lp = xs[0].shape[0]
    tb = TILE_MM
    nb = lp // tb
    n_in = len(xs)
    idx = (lambda i: (nb - 1 - i, 0)) if reverse else (lambda i: (i, 0))

    def body(*refs):
        o_ref, carry = refs[n_in], refs[n_in + 1]

        @pl.when(pl.program_id(0) == 0)
        def _():
            carry[...] = jnp.zeros_like(carry)

        x = refs[0][...]
        for r in refs[1:n_in]:
            x = x + r[...]
        r_i = lax.broadcasted_iota(jnp.int32, (tb, tb), 0)
        c_i = lax.broadcasted_iota(jnp.int32, (tb, tb), 1)
        tri = ((c_i >= r_i) if reverse else (c_i <= r_i)).astype(BF16)
        acc, rest = None, x
        for _ in range(3):
            part = rest.astype(BF16)
            rest = rest - part.astype(F32)
            acc = _dot(tri, part) if acc is None else acc + _dot(tri, part)
        o_ref[...] = acc + carry[...]
        carry[...] += jnp.sum(x, axis=0, keepdims=True)

    return pl.pallas_call(
        body, name=name, grid=(nb,),
        in_specs=[pl.BlockSpec((tb, 128), idx)] * n_in,
        out_specs=pl.BlockSpec((tb, 128), idx),
        out_shape=SDS((lp, 128), F32),
        scratch_shapes=[pltpu.VMEM((1, 128), F32)],
        compiler_params=_params(("arbitrary",)),
    )(*xs)


class _Att:
    def __init__(self, mode):
        self.mode = mode
        self.wide = mode == "mla"
        self.qw = 256 if self.wide else 128
        self.scale = (96 if mode == "mla" else 64) ** -0.5

    def resident(self, x, lo, scaled):
        if self.wide:
            return x[:, 0:128], x[:, 128:256]
        if scaled:
            x = x * jnp.asarray(self.scale, x.dtype)
        zero = jnp.zeros_like(x)
        return jnp.where(lo, x, zero), jnp.where(lo, zero, x)

    def moving(self, x):
        return (x[:, 0:128], x[:, 128:256]) if self.wide else (x, x)

    def logits(self, a, b, qpos, kpos, key_decay, slope, masked):
        s = _dot_nt(a, b)
        if self.wide:
            s = s * self.scale
        if self.mode == "fox":
            s = s - key_decay
        if self.mode == "swa":
            s = s - slope * (qpos - kpos).astype(F32)
        if masked:
            ok = kpos <= qpos
            if self.mode == "swa":
                ok = ok & ((kpos < N_META) | (qpos - kpos < WINDOW))
            s = jnp.where(ok, s, NEG)
        return s


def _halves(x, lo):
    zero = jnp.zeros_like(x)
    return jnp.where(lo, x, zero), jnp.where(lo, zero, x)


def _q_chunks(qi):
    far = qi >= 2
    return jnp.where(far, 3, qi + 1), lambda t: jnp.where(far, jnp.where(t == 0, 0, qi - 2 + t), t)


def _kv_specs(att, lp, rows):
    if att.mode == "swa":
        return (pl.BlockSpec((rows, 128), lambda g, i: (i if rows != lp else 0, g)),) * 2
    return (pl.BlockSpec((rows, PAIRS * att.qw), lambda g, i: (i if rows != lp else 0, g)),
            pl.BlockSpec((rows, PAIRS * 128), lambda g, i: (i if rows != lp else 0, g)))


def _pair_cols(att, x, pp, width):
    return x if x.shape[1] == width else x[:, pp * width:(pp + 1) * width]


def _att_fwd(att, q, k, v, extra, name):
    lp = q.shape[0]
    t = TILE_ATT
    nq = lp // t
    qw = att.qw
    mode = att.mode
    nh = 2 * PAIRS

    def body(*refs):
        q_ref, k_ref, v_ref = refs[0:3]
        o_ref, lse_ref = refs[-2:]
        g, qi = pl.program_id(0), pl.program_id(1)
        lo = lax.broadcasted_iota(jnp.int32, (1, 128), 1) < 64
        q_all = q_ref[...]
        q_heads = [h for pp in range(PAIRS) for h in att.resident(_pair_cols(att, q_all, pp, qw), lo, True)]
        qpos = qi * t + lax.broadcasted_iota(jnp.int32, (t, 1), 0)

        def step(kj, carry, masked, width=1):
            ks = pl.multiple_of(kj * t, t)
            kc, vc = k_ref[pl.ds(ks, width * t), :], v_ref[pl.ds(ks, width * t), :]
            kpos = kj * t + lax.broadcasted_iota(jnp.int32, (1, width * t), 1)
            out = []
            for h in range(nh):
                pp = h // 2
                m, l, acc = carry[3 * h:3 * h + 3]
                k_h = att.moving(_pair_cols(att, kc, pp, qw))[h % 2]
                decay = refs[3][h, :, pl.ds(ks, width * t)] if mode == "fox" else None
                slope = refs[4][nh * g + h] if mode == "swa" else None
                s = att.logits(q_heads[h], k_h, qpos, kpos, decay, slope, masked)
                m_new = jnp.maximum(m, jnp.max(s, axis=-1, keepdims=True))
                alpha = jnp.exp(m - m_new)
                pe = jnp.exp(s - m_new)
                l = alpha * l + jnp.sum(pe, axis=-1, keepdims=True)
                acc = alpha * acc + _dot(pe.astype(BF16), _pair_cols(att, vc, pp, 128))
                out += [m_new, l, acc]
            return tuple(out)

        init = []
        for h in range(nh):
            if mode == "swa":
                init += [jnp.full((t, 1), refs[3][nh * g + h], F32), jnp.ones((t, 1), F32)]
            else:
                init += [jnp.full((t, 1), NEG, F32), jnp.zeros((t, 1), F32)]
            init.append(jnp.zeros((t, 128), F32))
        if mode == "swa":
            n_steps, chunk_of = _q_chunks(qi)
            carry = lax.fori_loop(0, n_steps, lambda i, c: step(chunk_of(i), c, True), tuple(init))
        else:
            carry = lax.fori_loop(0, qi // 2, lambda j, c: step(2 * j, c, False, 2), tuple(init))
            carry = lax.fori_loop(0, qi % 2, lambda j, c: step(qi - 1, c, False), carry)
            carry = step(qi, carry, True)
        outs = []
        for pp in range(PAIRS):
            (ma, la, acca), (mb, lb, accb) = carry[6 * pp:6 * pp + 3], carry[6 * pp + 3:6 * pp + 6]
            outs.append(jnp.where(lo, acca / la, accb / lb).astype(BF16))
            lse_ref[2 * pp] = ma + jnp.log(la)
            lse_ref[2 * pp + 1] = mb + jnp.log(lb)
        o_ref[...] = jnp.concatenate(outs, axis=1)

    in_specs = [pl.BlockSpec((t, PAIRS * qw), lambda g, i: (i, g)), *_kv_specs(att, lp, lp)]
    if mode == "fox":
        in_specs += [pl.BlockSpec((nh, 1, lp), lambda g, i: (g, 0, 0))]
    if mode == "swa":
        in_specs += [pl.BlockSpec(memory_space=pltpu.SMEM)] * 2
    return pl.pallas_call(
        body, name=name, grid=(4 // PAIRS, nq), in_specs=in_specs,
        out_specs=[pl.BlockSpec((t, PAIRS * 128), lambda g, i: (i, g)), pl.BlockSpec((nh, t, 1), lambda g, i: (g, i, 0))],
        out_shape=[SDS((lp, 512), BF16), SDS((HEADS, lp, 1), F32)],
        compiler_params=_params(("parallel", "arbitrary")),
    )(q, k, v, *extra)


def _att_dq(att, q, k, v, o, do, lse, extra, name):
    lp = q.shape[0]
    t = TILE_ATT
    nq = lp // t
    qw = att.qw
    mode = att.mode
    nh = 2 * PAIRS

    def body(*refs):
        q_ref, k_ref, v_ref, o_ref, do_ref, lse_ref = refs[0:6]
        n_out = 2 if mode == "mla" else 3
        outs = refs[len(refs) - n_out:]
        dq_ref, delta_ref = outs[0:2]
        g, qi = pl.program_id(0), pl.program_id(1)
        lo = lax.broadcasted_iota(jnp.int32, (1, 128), 1) < 64
        q_all, do_all = q_ref[...], do_ref[...]
        prod = do_all.astype(F32) * o_ref[...].astype(F32)
        q_heads, do_heads, delta = [], [], []
        for pp in range(PAIRS):
            q_heads += att.resident(_pair_cols(att, q_all, pp, qw), lo, True)
            do_heads += _halves(_pair_cols(att, do_all, pp, 128), lo)
            pr_pp = _pair_cols(att, prod, pp, 128)
            delta += [jnp.sum(jnp.where(lo, pr_pp, 0.0), axis=-1, keepdims=True),
                      jnp.sum(jnp.where(lo, 0.0, pr_pp), axis=-1, keepdims=True)]
        lse_v = [lse_ref[h] for h in range(nh)]
        qpos = qi * t + lax.broadcasted_iota(jnp.int32, (t, 1), 0)

        def step(kj, carry, masked, width=1):
            ks = pl.multiple_of(kj * t, t)
            kc, vc = k_ref[pl.ds(ks, width * t), :], v_ref[pl.ds(ks, width * t), :]
            kpos = kj * t + lax.broadcasted_iota(jnp.int32, (1, width * t), 1)
            out = []
            for h in range(nh):
                pp = h // 2
                k_h = att.moving(_pair_cols(att, kc, pp, qw))[h % 2]
                decay = refs[6][h, :, pl.ds(ks, width * t)] if mode == "fox" else None
                slope = refs[7][nh * g + h] if mode == "swa" else None
                s = att.logits(q_heads[h], k_h, qpos, kpos, decay, slope, masked)
                pr = jnp.exp(s - lse_v[h])
                ds = pr * (_dot_nt(do_heads[h], _pair_cols(att, vc, pp, 128)) - delta[h])
                out.append(carry[2 * h] + _dot(ds.astype(BF16), k_h))
                out.append(carry[2 * h + 1] + jnp.sum(ds, axis=-1, keepdims=True) if mode == "fox" else carry[2 * h + 1])
            return tuple(out)

        init = (jnp.zeros((t, 128), F32), jnp.zeros((t, 1), F32)) * nh
        if mode == "swa":
            n_steps, chunk_of = _q_chunks(qi)
            carry = lax.fori_loop(0, n_steps, lambda i, c: step(chunk_of(i), c, True), init)
        else:
            carry = lax.fori_loop(0, qi // 2, lambda j, c: step(2 * j, c, False, 2), init)
            carry = lax.fori_loop(0, qi % 2, lambda j, c: step(qi - 1, c, False), carry)
            carry = step(qi, carry, True)
        dq = []
        for pp in range(PAIRS):
            dqa, dca, dqb, dcb = carry[4 * pp:4 * pp + 4]
            dq += [dqa, dqb] if att.wide else [jnp.where(lo, dqa, dqb)]
            if mode == "fox":
                outs[2][2 * pp] = dca
                outs[2][2 * pp + 1] = dcb
        dq_ref[...] = jnp.concatenate(dq, axis=1) * att.scale
        for h in range(nh):
            delta_ref[h] = delta[h]
        if mode == "swa":
            ds_ref = outs[2]

            @pl.when(qi == 0)
            def _():
                ds_ref[...] = jnp.zeros_like(ds_ref)

            lane = lax.broadcasted_iota(jnp.int32, (8, 128), 1)
            acc = jnp.zeros((8, 128), F32)
            for h in range(nh):
                tot = -jnp.sum(jnp.exp(refs[6][nh * g + h] - lse_v[h]) * delta[h])
                acc = acc + jnp.where(lane == h, tot, 0.0)
            ds_ref[0] += acc

    col = pl.BlockSpec((nh, t, 1), lambda g, i: (g, i, 0))
    in_specs = [pl.BlockSpec((t, PAIRS * qw), lambda g, i: (i, g)), *_kv_specs(att, lp, lp),
                pl.BlockSpec((t, PAIRS * 128), lambda g, i: (i, g)), pl.BlockSpec((t, PAIRS * 128), lambda g, i: (i, g)), col]
    out_specs = [pl.BlockSpec((t, PAIRS * qw), lambda g, i: (i, g)), col]
    out_shape = [SDS((lp, 4 * qw), F32), SDS((HEADS, lp, 1), F32)]
    if mode == "fox":
        in_specs += [pl.BlockSpec((nh, 1, lp), lambda g, i: (g, 0, 0))]
        out_specs.append(col)
        out_shape.append(SDS((HEADS, lp, 1), F32))
    if mode == "swa":
        in_specs += [pl.BlockSpec(memory_space=pltpu.SMEM)] * 2
        out_specs.append(pl.BlockSpec((1, 8, 128), lambda g, i: (g, 0, 0)))
        out_shape.append(SDS((4 // PAIRS, 8, 128), F32))
    return pl.pallas_call(
        body, name=name, grid=(4 // PAIRS, nq), in_specs=in_specs, out_specs=out_specs, out_shape=out_shape,
        compiler_params=_params(("parallel", "arbitrary")),
    )(q, k, v, o, do, lse, *extra)


def _att_dkv(att, q, k, v, do, lse_row, delta_row, extra, name):
    lp = q.shape[0]
    t = TILE_ATT
    nq = lp // t
    qw = att.qw
    mode = att.mode
    nh = 2 * PAIRS

    def body(*refs):
        q_ref, k_ref, v_ref, do_ref, lse_ref, delta_ref = refs[0:6]
        n_out = 3 if mode == "fox" else 2
        outs = refs[len(refs) - n_out:]
        dk_ref, dv_ref = outs[0:2]
        g, kj = pl.program_id(0), pl.program_id(1)
        lo = lax.broadcasted_iota(jnp.int32, (1, 128), 1) < 64
        k_all, v_all = k_ref[...], v_ref[...]
        k_heads, v_heads = [], []
        for pp in range(PAIRS):
            k_heads += att.resident(_pair_cols(att, k_all, pp, qw), lo, True)
            v_heads += _halves(_pair_cols(att, v_all, pp, 128), lo)
        kpos = kj * t + lax.broadcasted_iota(jnp.int32, (t, 1), 0)

        def step(qi, carry, masked, width=1):
            qs = pl.multiple_of(qi * t, t)
            qc, doc = q_ref[pl.ds(qs, width * t), :], do_ref[pl.ds(qs, width * t), :]
            qpos = qi * t + lax.broadcasted_iota(jnp.int32, (1, width * t), 1)
            out = []
            for h in range(nh):
                pp = h // 2
                dk_acc, dv_acc, dc_acc = carry[3 * h:3 * h + 3]
                q_h = att.moving(_pair_cols(att, qc, pp, qw))[h % 2]
                do_h = _pair_cols(att, doc, pp, 128)
                decay = refs[6][h] if mode == "fox" else None
                slope = refs[6][nh * g + h] if mode == "swa" else None
                st = att.logits(k_heads[h], q_h, qpos, kpos, decay, slope, masked)
                pt = jnp.exp(st - lse_ref[h, :, pl.ds(qs, width * t)])
                dst = pt * (_dot_nt(v_heads[h], do_h) - delta_ref[h, :, pl.ds(qs, width * t)])
                dv_acc = dv_acc + _dot(pt.astype(BF16), do_h)
                dk_acc = dk_acc + _dot(dst.astype(BF16), q_h)
                if mode == "fox":
                    dc_acc = dc_acc - jnp.sum(dst, axis=-1, keepdims=True)
                out += [dk_acc, dv_acc, dc_acc]
            return tuple(out)

        init = (jnp.zeros((t, 128), F32), jnp.zeros((t, 128), F32), jnp.zeros((t, 1), F32)) * nh
        if mode == "swa":
            last = jnp.where(kj == 0, nq, jnp.minimum(kj + 2, nq))
            carry = lax.fori_loop(kj, last, lambda qi, c: step(qi, c, True), init)
        else:
            carry = step(kj, init, True)
            rest = nq - 1 - kj
            carry = lax.fori_loop(0, rest // 2, lambda j, c: step(kj + 1 + 2 * j, c, False, 2), carry)
            carry = lax.fori_loop(0, rest % 2, lambda j, c: step(nq - 1, c, False), carry)
        dk, dv = [], []
        for pp in range(PAIRS):
            dka, dva, dca, dkb, dvb, dcb = carry[6 * pp:6 * pp + 6]
            dk += [dka, dkb] if att.wide else [jnp.where(lo, dka, dkb)]
            dv.append(jnp.where(lo, dva, dvb))
            if mode == "fox":
                outs[2][2 * pp] = dca
                outs[2][2 * pp + 1] = dcb
        dk_ref[...] = jnp.concatenate(dk, axis=1) * att.scale
        dv_ref[...] = jnp.concatenate(dv, axis=1)

    rowv = pl.BlockSpec((nh, 1, lp), lambda g, j: (g, 0, 0))
    col = pl.BlockSpec((nh, t, 1), lambda g, j: (g, j, 0))
    in_specs = [pl.BlockSpec((lp, PAIRS * qw), lambda g, j: (0, g)), *_kv_specs(att, lp, t),
                pl.BlockSpec((lp, PAIRS * 128), lambda g, j: (0, g)), rowv, rowv]
    out_specs = [pl.BlockSpec((t, PAIRS * qw), lambda g, j: (j, g)), pl.BlockSpec((t, PAIRS * 128), lambda g, j: (j, g))]
    out_shape = [SDS((lp, 4 * qw), F32), SDS((lp, 512), F32)]
    if mode == "fox":
        in_specs += [col]
        out_specs.append(col)
        out_shape.append(SDS((HEADS, lp, 1), F32))
    if mode == "swa":
        in_specs += [pl.BlockSpec(memory_space=pltpu.SMEM)]
    return pl.pallas_call(
        body, name=name, grid=(4 // PAIRS, nq), in_specs=in_specs, out_specs=out_specs, out_shape=out_shape,
        compiler_params=_params(("parallel", "arbitrary")),
    )(q, k, v, do, lse_row, delta_row, *extra)


def _post_fwd(h, proj, outs, wb, wo, name):
    lp, d = h.shape
    tb = TILE_POST
    row = lambda w: pl.BlockSpec((tb, w), lambda i: (i, 0))

    def body(h_ref, g0, g1, g2, oa, ob, oc, wb_ref, wo_ref, o_ref):
        merged = jnp.zeros((tb, d), F32)
        for n, (g_ref, br) in enumerate(((g0, oa), (g1, ob), (g2, oc))):
            merged = merged + jax.nn.sigmoid(g_ref[...]) * _dot(br[...], wb_ref[n])
        o_ref[...] = h_ref[...] + _dot(merged.astype(BF16), wo_ref[...])

    gate = lambda n: pl.BlockSpec((tb, d), lambda i, n=n: (i, n))
    return pl.pallas_call(
        body, name=name, grid=(lp // tb,),
        in_specs=[row(d), gate(0), gate(1), gate(2), row(512), row(512), row(512),
                  pl.BlockSpec((3, 512, d), lambda i: (0, 0, 0)), pl.BlockSpec((d, d), lambda i: (0, 0))],
        out_specs=row(d), out_shape=SDS((lp, d), F32),
        compiler_params=_params(("parallel",)),
    )(h, proj, proj, proj, *outs, wb, wo)


def _post_bwd(dh, proj, outs, wb, wo, name):
    lp, d = dh.shape
    tb = TILE_POST
    row = lambda w: pl.BlockSpec((tb, w), lambda i: (i, 0))

    def body(dh_ref, g0, g1, g2, oa, ob, oc, wb_ref, wo_ref, dg_ref, doa, dob, doc, dwb_ref, dwo_ref):
        @pl.when(pl.program_id(0) == 0)
        def _():
            dwb_ref[...] = jnp.zeros_like(dwb_ref)
            dwo_ref[...] = jnp.zeros_like(dwo_ref)

        dhb = dh_ref[...].astype(BF16)
        dm = _dot_nt(dhb, wo_ref[...])
        merged = jnp.zeros((tb, d), F32)
        for n, (g_ref, br, do_ref) in enumerate(((g0, oa, doa), (g1, ob, dob), (g2, oc, doc))):
            gate = jax.nn.sigmoid(g_ref[...])
            o_n = br[...]
            y = _dot(o_n, wb_ref[n])
            merged = merged + gate * y
            dy = (dm * gate).astype(BF16)
            dg_ref[:, n * d:(n + 1) * d] = (dm * y * gate * (1.0 - gate)).astype(BF16)
            do_ref[...] = _dot_nt(dy, wb_ref[n]).astype(BF16)
            dwb_ref[n] += _dot_tn(o_n, dy)
        dwo_ref[...] += _dot_tn(merged.astype(BF16), dhb)

    gate = lambda n: pl.BlockSpec((tb, d), lambda i, n=n: (i, n))
    wb_spec = pl.BlockSpec((3, 512, d), lambda i: (0, 0, 0))
    wo_spec = pl.BlockSpec((d, d), lambda i: (0, 0))
    return pl.pallas_call(
        body, name=name, grid=(lp // tb,),
        in_specs=[row(d), gate(0), gate(1), gate(2), row(512), row(512), row(512), wb_spec, wo_spec],
        out_specs=[row(GATES_W), row(512), row(512), row(512), wb_spec, wo_spec],
        out_shape=[SDS((lp, GATES_W), BF16)] + [SDS((lp, 512), BF16)] * 3 + [SDS((3, 512, d), F32), SDS((d, d), F32)],
        compiler_params=_params(("arbitrary",)),
    )(dh, proj, proj, proj, *outs, wb, wo)


def _shift_down(x, halo, n, first):
    rows = lax.broadcasted_iota(jnp.int32, x.shape, 0)
    edge = jnp.concatenate([pltpu.roll(halo, n, 0), jnp.zeros((x.shape[0] - 8, x.shape[1]), F32)], axis=0)
    edge = jnp.where(first, 0.0, edge)
    return jnp.where(rows < n, edge, pltpu.roll(x, n, 0))


def _shift_up(x, halo, n, last):
    tb = x.shape[0]
    rows = lax.broadcasted_iota(jnp.int32, x.shape, 0)
    edge = jnp.concatenate([jnp.zeros((tb - 8, x.shape[1]), F32), pltpu.roll(halo, 8 - n, 0)], axis=0)
    edge = jnp.where(last, 0.0, edge)
    return jnp.where(rows >= tb - n, edge, pltpu.roll(x, tb - n, 0))


def _conv(u, halo, w_ref, b_ref, first):
    taps = (_shift_down(u, halo, 2, first), _shift_down(u, halo, 1, first), u)
    c = b_ref[...] + w_ref[0:1, :] * taps[0] + w_ref[1:2, :] * taps[1] + w_ref[2:3, :] * taps[2]
    return c, taps


def _ffn_specs(tb, f):
    hb = tb // 8
    cur = lambda c: pl.BlockSpec((tb, f), lambda i, c=c: (i, c))
    prev = lambda c: pl.BlockSpec((8, f), lambda i, c=c: (jnp.maximum(i * hb - 1, 0), c))
    vec = lambda r, c: pl.BlockSpec((r, f), lambda i, c=c: (0, c))
    return cur, prev, vec


def _ffn_act_fwd(u, cw, cb, name):
    lp = u.shape[0]
    f = D_FF
    tb = TILE_ROW
    cur, prev, vec = _ffn_specs(tb, f)

    def body(ug, uv, hg, hv, wg, wv, bg, bv, o_ref):
        first = pl.program_id(0) == 0
        cg, _ = _conv(ug[...], hg[...], wg, bg, first)
        cv, _ = _conv(uv[...], hv[...], wv, bv, first)
        o_ref[...] = (cg * jax.nn.sigmoid(cg) * cv).astype(BF16)

    return pl.pallas_call(
        body, name=name, grid=(lp // tb,),
        in_specs=[cur(0), cur(1), prev(0), prev(1), vec(8, 0), vec(8, 1), vec(1, 0), vec(1, 1)],
        out_specs=pl.BlockSpec((tb, f), lambda i: (i, 0)), out_shape=SDS((lp, f), BF16),
        compiler_params=_params(("parallel",)),
    )(u, u, u, u, cw, cw, cb, cb)


def _ffn_act_bwd_conv(u, dact, cw, cb, name):
    lp = u.shape[0]
    f = D_FF
    tb = TILE_ROW
    cur, prev, vec = _ffn_specs(tb, f)

    def body(ug, uv, hg, hv, wg, wv, bg, bv, da_ref, dcg_ref, dcv_ref, dwg, dwv, dbg, dbv):
        first = pl.program_id(0) == 0

        @pl.when(first)
        def _():
            for r in (dwg, dwv, dbg, dbv):
                r[...] = jnp.zeros_like(r)

        cg, tg = _conv(ug[...], hg[...], wg, bg, first)
        cv, tv = _conv(uv[...], hv[...], wv, bv, first)
        da = da_ref[...]
        sg = jax.nn.sigmoid(cg)
        dcg = da * cv * sg * (1.0 + cg * (1.0 - sg))
        dcv = da * cg * sg
        dcg_ref[...] = dcg
        dcv_ref[...] = dcv
        for dc, taps, dw, db in ((dcg, tg, dwg, dbg), (dcv, tv, dwv, dbv)):
            for n in range(3):
                dw[n:n + 1, :] += jnp.sum(dc * taps[n], axis=0, keepdims=True)
            db[0:1, :] += jnp.sum(dc, axis=0, keepdims=True)

    row = pl.BlockSpec((tb, f), lambda i: (i, 0))
    acc = pl.BlockSpec((8, f), lambda i: (0, 0))
    return pl.pallas_call(
        body, name=name, grid=(lp // tb,),
        in_specs=[cur(0), cur(1), prev(0), prev(1), vec(8, 0), vec(8, 1), vec(1, 0), vec(1, 1), row],
        out_specs=[row, row, acc, acc, acc, acc],
        out_shape=[SDS((lp, f), F32)] * 2 + [SDS((8, f), F32)] * 4,
        compiler_params=_params(("arbitrary",)),
    )(u, u, u, u, cw, cw, cb, cb, dact)


def _ffn_act_bwd_in(dcg, dcv, cw, name):
    lp = dcg.shape[0]
    f = D_FF
    tb = TILE_ROW
    nb = lp // tb
    hb = tb // 8
    cur = pl.BlockSpec((tb, f), lambda i: (i, 0))
    nxt = pl.BlockSpec((8, f), lambda i: (jnp.minimum((i + 1) * hb, nb * hb - 1), 0))
    vec = lambda c: pl.BlockSpec((8, f), lambda i, c=c: (0, c))

    def body(dg, dv, ng, nv, wg, wv, og, ov):
        last = pl.program_id(0) == nb - 1
        for dc_ref, n_ref, w_ref, o_ref in ((dg, ng, wg, og), (dv, nv, wv, ov)):
            dc, halo = dc_ref[...], n_ref[...]
            du = (w_ref[2:3, :] * dc + w_ref[1:2, :] * _shift_up(dc, halo, 1, last)
                  + w_ref[0:1, :] * _shift_up(dc, halo, 2, last))
            o_ref[...] = du.astype(BF16)

    return pl.pallas_call(
        body, name=name, grid=(nb,),
        in_specs=[cur, cur, nxt, nxt, vec(0), vec(1)],
        out_specs=[cur, cur],
        out_shape=[SDS((lp, f), BF16)] * 2,
        compiler_params=_params(("parallel",)),
    )(dcg, dcv, dcg, dcv, cw, cw)


def _loss_head(y, target, n_real, name):
    lp, d = y.shape
    tb = TILE_MM

    def body(y_ref, t_ref, dy_ref, loss_ref):
        i = pl.program_id(0)

        @pl.when(i == 0)
        def _():
            loss_ref[...] = jnp.zeros_like(loss_ref)

        rows = i * tb + lax.broadcasted_iota(jnp.int32, (tb, 1), 0)
        real = (rows >= N_META) & (rows < N_META + n_real)
        diff = jnp.where(real, y_ref[...] - t_ref[...], 0.0)
        dy_ref[...] = diff * (1.0 / d)
        loss_ref[...] += (0.5 / d) * jnp.sum(diff * diff).reshape(1, 1)

    row = pl.BlockSpec((tb, d), lambda i: (i, 0))
    return pl.pallas_call(
        body, name=name, grid=(lp // tb,), in_specs=[row, row],
        out_specs=[row, pl.BlockSpec((1, 1), lambda i: (0, 0))],
        out_shape=[SDS((lp, d), F32), SDS((1, 1), F32)],
        compiler_params=_params(("arbitrary",)),
    )(y, target)


def _pad_lanes(v, width, at=0):
    return jnp.pad(v.astype(F32), (at, width - at - v.shape[0]))[None, :]


def _mix_params(w, big, l):
    b = lambda a: a.astype(BF16)
    win = big["w_in"]
    fq, fk, fv, ff, cq, ckv, kr, sq, sk, sv, gates = jnp.split(
        win, [512, 1024, 1536, 1544, 1800, 1928, 1960, 2472, 2600, 2728], axis=1)
    misc = jnp.concatenate([kr, ff, jnp.zeros((D_MODEL, 88), win.dtype)], axis=1)
    w_in = b(jnp.concatenate([gates, fq, fk, fv, sq, sk, sv, cq, ckv, misc], axis=1))
    wq = jnp.pad(big["mla_w_q_up"].reshape(256, HEADS, 96), ((0, 0), (0, 0), (0, 32))).reshape(256, 1024)
    wkv = big["mla_w_kv_up"].reshape(128, HEADS, 128)
    wkk = jnp.pad(wkv[:, :, :64], ((0, 0), (0, 0), (0, 64))).reshape(128, 1024)
    wkvv = wkv[:, :, 64:].reshape(128, 512)
    tile = lambda g, n: jnp.tile(g.astype(F32), n)[None, :]
    prm = [tile(w["fox_q_g"][l], 8), tile(w["fox_k_g"][l], 8), tile(w["swa_q_g"][l], 8), tile(w["swa_k_g"][l], 2),
           _pad_lanes(w["fox_forget_b"][l], 128, FF_LANE), w["mla_q_a_g"][l][None, :], w["mla_kv_a_g"][l][None, :],
           tile(jnp.pad(w["mla_q_g"][l], (0, 32)), 8), tile(jnp.pad(w["mla_k_g"][l], (0, 32)), 8),
           wq.astype(F32), wkk.astype(F32), wkvv.astype(F32)]
    return dict(g1=w["norm1_g"][l][None, :], w_in=w_in, prm=prm, sinks=w["swa_sinks"][l].astype(F32),
                wb=b(big["w_branch"]), wo=b(big["w_o"]))


def _ffn_params(w, big, l):
    cw = jnp.pad(w["ffn_conv_w"][l].astype(F32), ((0, 5), (0, 0)))
    return dict(g2=w["norm2_g"][l][None, :], w_up=big["ffn_w_up"].astype(BF16), cw=cw,
                cb=w["ffn_conv_b"][l][None, :].astype(F32), w_down=big["ffn_w_down"].astype(BF16))


def _cols(c):
    ct = c[:, FF_LANE:FF_LANE + HEADS].T
    return ct[:, :, None], ct[:, None, :]


def _rows(col):
    return jnp.swapaxes(col, 1, 2)


def _from_cols(col):
    return jnp.pad(col[:, :, 0].T, ((0, 0), (FF_LANE, 128 - FF_LANE - HEADS)))


def _layer_fwd_mix(h, lw, consts, cos, sin, slopes, l):
    tag = f"l{l}_"
    xn, proj = _norm_matmul(h, lw["g1"], lw["w_in"], IN_W // 2, tag + "in_proj")
    fq, fk, fv, mq, mk, mv, sq, skd, svd, ls = _prep_fwd(proj, lw["prm"], consts, cos, sin, tag + "prep")
    c = _cumsum([ls], False, tag + "decay_cumsum")
    c_col, c_row = _cols(c)
    oa, lse_a = _att_fwd(_Att("fox"), fq, fk, fv, (c_row,), tag + "fox_fwd")
    ob, lse_b = _att_fwd(_Att("mla"), mq, mk, mv, (), tag + "mla_fwd")
    oc, lse_c = _att_fwd(_Att("swa"), sq, skd, svd, (lw["sinks"], slopes), tag + "swa_fwd")
    h2 = _post_fwd(h, proj, (oa, ob, oc), lw["wb"], lw["wo"], tag + "merge")
    saved = dict(h=h, xn=xn, proj=proj, q=(fq, mq, sq), k=(fk, mk, skd), v=(fv, mv, svd), c=(c_col, c_row),
                 o=(oa, ob, oc), lse=(lse_a, lse_b, lse_c), h2=h2)
    return h2, saved


def _layer_fwd_ffn(h2, lw, l):
    tag = f"l{l}_"
    xn2, u = _norm_matmul(h2, lw["g2"], lw["w_up"], D_FF, tag + "ffn_up")
    act = _ffn_act_fwd(u, lw["cw"], lw["cb"], tag + "ffn_act")
    h3 = _matmul_residual(act, lw["w_down"], h2, tag + "ffn_down")
    return h3, dict(xn2=xn2, u=u, act=act)


def _layer_bwd_ffn(dh3, lw, sv, l):
    tag = f"l{l}_"
    f = D_FF
    dact = _matmul_nt(dh3, lw["w_down"], f, tag + "ffn_down_dx")
    dw_down = _matmul_tn(sv["act"], dh3, D_MODEL, tag + "ffn_down_dw")
    dcg, dcv, dwg, dwv, dbg, dbv = _ffn_act_bwd_conv(sv["u"], dact, lw["cw"], lw["cb"], tag + "ffn_act_dc")
    dug, duv = _ffn_act_bwd_in(dcg, dcv, lw["cw"], tag + "ffn_act_du")
    du = jnp.concatenate([dug, duv], axis=1)
    dw_up = _matmul_tn(sv["xn2"], du, f, tag + "ffn_up_dw")
    tb = TILE_MM
    half = lambda c: pl.BlockSpec((tb, f), lambda i, c=c: (i, c))
    whalf = lambda c: pl.BlockSpec((D_MODEL, f), lambda i, c=c: (0, c))
    dh2, dg2 = _norm_matmul_bwd(du, lw["w_up"], du, lw["w_up"], sv["h2"], lw["g2"], dh3,
                                (half(0), whalf(0), half(1), whalf(1)), tag + "ffn_up_dx")
    g = dict(norm2_g=dg2[0], ffn_w_up=dw_up, ffn_conv_w=jnp.concatenate([dwg[0:3], dwv[0:3]], axis=1),
             ffn_conv_b=jnp.concatenate([dbg[0], dbv[0]]), ffn_w_down=dw_down)
    return dh2, g


def _layer_bwd_mix(dh2, lw, sv, consts, folds, cos, sin, slopes, l):
    tag = f"l{l}_"
    tb = TILE_MM
    dgates, doa, dob, doc, dwb, dwo = _post_bwd(dh2, sv["proj"], sv["o"], lw["wb"], lw["wo"], tag + "merge_bwd")
    c_col, c_row = sv["c"]
    extras = ((c_row,), (), (lw["sinks"], slopes))
    extras_kv = ((c_col,), (), (slopes,))
    grads = []
    for n, (mode, do) in enumerate((("fox", doa), ("mla", dob), ("swa", doc))):
        att = _Att(mode)
        q, k, v = sv["q"][n], sv["k"][n], sv["v"][n]
        res = _att_dq(att, q, k, v, sv["o"][n], do, sv["lse"][n], extras[n], tag + mode + "_dq")
        dq, delta = res[0], res[1]
        res_kv = _att_dkv(att, q, k, v, do, _rows(sv["lse"][n]), _rows(delta), extras_kv[n], tag + mode + "_dkv")
        grads.append((dq, res_kv[0], res_kv[1], res[2:], res_kv[2:]))
    (dfq, dfk, dfv, (dcq,), (dck,)), (dmq, dmk, dmv, _, _), (dsq, dskp, dsvp, (dsink,), _) = grads
    dls = _cumsum([_from_cols(dcq), _from_cols(dck)], True, tag + "decay_cumsum_bwd")
    res = _prep_bwd(sv["proj"], lw["prm"], consts, cos, sin,
                    (dfq, dfk, dfv, dmq, dmk, dmv, dsq, dskp, dsvp, dls), folds, tag + "prep_bwd")
    dother, pg = res[0], res[1:]
    dw_g = _matmul_tn(sv["xn"], dgates, GATES_W, tag + "in_proj_dw_gates")
    dw_o = _matmul_tn(sv["xn"], dother, OTHER_W, tag + "in_proj_dw_other")
    full = lambda w: pl.BlockSpec((tb, w), lambda i: (i, 0))
    wfull = lambda w: pl.BlockSpec((D_MODEL, w), lambda i: (0, 0))
    dh, dg1 = _norm_matmul_bwd(dgates, lw["w_in"][:, :GATES_W], dother, lw["w_in"][:, GATES_W:], sv["h"], lw["g1"], dh2,
                               (full(GATES_W), wfull(GATES_W), full(OTHER_W), wfull(OTHER_W)), tag + "in_proj_dx")
    d_in = jnp.concatenate([
        dw_o[:, O_FQ:O_FV + 512], dw_o[:, O_MISC + FF_LANE:O_MISC + FF_LANE + 8], dw_o[:, O_CQ:O_CQ + 256],
        dw_o[:, O_CKV:O_CKV + 128], dw_o[:, O_MISC:O_MISC + 32], dw_o[:, O_SQ:O_SQ + 512], dw_o[:, O_SK:O_SK + 128],
        dw_o[:, O_SV:O_SV + 128], dw_g], axis=1)
    d_wq = pg[9].reshape(256, HEADS, 128)[:, :, :96].reshape(256, 768)
    d_wkv = jnp.concatenate([pg[10].reshape(128, HEADS, 128)[:, :, :64], pg[11].reshape(128, HEADS, 64)],
                            axis=2).reshape(128, 1024)
    g = dict(
        norm1_g=dg1[0], w_in=d_in, fox_forget_b=pg[4][0, FF_LANE:FF_LANE + 8], fox_q_g=pg[0][0, :64],
        fox_k_g=pg[1][0, :64], mla_q_a_g=pg[5][0], mla_w_q_up=d_wq, mla_kv_a_g=pg[6][0], mla_w_kv_up=d_wkv,
        mla_q_g=pg[7][0, :96], mla_k_g=pg[8][0, :96], swa_q_g=pg[2][0, :64], swa_k_g=pg[3][0, :64],
        swa_sinks=dsink[:, 0, 0:2 * PAIRS].reshape(HEADS), w_branch=dwb, w_o=dwo)
    return dh, g


_MIX_BIG = ("w_in", "mla_w_q_up", "mla_w_kv_up", "w_branch", "w_o")
_FFN_BIG = ("ffn_w_up", "ffn_w_down")


def _local_step(x, target, w, hook=None, fetch=None):
    if fetch is None:
        fetch = lambda l, stage, after: {n: w[n][l] for n in (_MIX_BIG if stage == "mix" else _FFN_BIG)}
    seq = x.shape[0]
    length = N_META + seq
    lp = -(-length // ROW_ALIGN) * ROW_ALIGN
    pad = lp - length
    h = jnp.concatenate([w["meta_tokens"].astype(F32), x, jnp.zeros((pad, D_MODEL), F32)], axis=0)
    tgt = jnp.pad(target, ((N_META, pad), (0, 0)))
    consts = _consts()
    folds = (_fold_matrix(512, 64), _fold_matrix(1024, 128))
    cos, sin = _rope_tables(lp)
    slopes = jnp.asarray(2.0 ** (-8.0 * np.arange(1, HEADS + 1, dtype=np.float32) / HEADS), F32)
    lws, saved = [], []
    for l in range(DEPTH):
        lw = _mix_params(w, fetch(l, "mix", h), l)
        h, sv = _layer_fwd_mix(h, lw, consts, cos, sin, slopes, l)
        lw.update(_ffn_params(w, fetch(l, "ffn", h), l))
        h, sv_ffn = _layer_fwd_ffn(h, lw, l)
        lws.append(lw)
        saved.append({**sv, **sv_ffn})
    dh, loss = _loss_head(h, tgt, seq, "loss_head")
    grads = [None] * DEPTH
    for l in reversed(range(DEPTH)):
        dh, g_ffn = _layer_bwd_ffn(dh, lws[l], saved[l], l)
        tick = hook(l, "ffn", g_ffn) if hook else None
        if tick is not None:
            lws[l]["sinks"] = lws[l]["sinks"] + tick
        dh, g_mix = _layer_bwd_mix(dh, lws[l], saved[l], consts, folds, cos, sin, slopes, l)
        grads[l] = {**g_ffn, **g_mix}
        tick = hook(l, "mix", grads[l]) if hook else None
        if tick is not None and l > 0:
            lws[l - 1]["cw"] = lws[l - 1]["cw"] + tick
    return loss, dh[N_META:length], dh[:N_META], grads


def _place():
    return lax.axis_index("x"), lax.axis_index("y"), lax.axis_index("c")


def _flip(pos, k):
    x, y, c = pos
    return (1 - x if k & 4 else x, 1 - y if k & 2 else y, 1 - c if k & 1 else c)


def _index(pos):
    return 4 * pos[0] + 2 * pos[1] + pos[2]


def _gather(tensors, name):
    n_t = len(tensors)

    def body(*refs):
        ins, outs = refs[:n_t], refs[n_t:2 * n_t]
        send_sems, recv_sems, local_sems = refs[2 * n_t:]
        x, y, c = _place()
        me, sibling = (x, y, c), (x, y, 1 - c)
        chips = [(1 - x, y), (x, 1 - y), (1 - x, 1 - y)]

        def copy(t, k, block, to, src=None):
            dst = outs[t].at[_index(block)]
            return pltpu.make_async_remote_copy(
                src_ref=dst if src is None else src, dst_ref=dst, send_sem=send_sems.at[t, k],
                recv_sem=recv_sems.at[t, k], device_id=to, device_id_type=pl.DeviceIdType.MESH)

        local, sent = [], []
        for t in range(n_t):
            local.append(pltpu.make_async_copy(ins[t], outs[t].at[_index(me)], local_sems.at[t]))
            local[-1].start()
            sent.append(copy(t, 0, me, sibling, src=ins[t]))
            sent += [copy(t, 1 + j, me, (*chip, c), src=ins[t]) for j, chip in enumerate(chips)]
        for cp in sent:
            cp.start()
        for j, chip in enumerate(chips):
            for t in range(n_t):
                copy(t, 1 + j, (*chip, c), me).wait_recv()
                sent.append(copy(t, 4 + j, (*chip, c), sibling))
                sent[-1].start()
        for t in range(n_t):
            copy(t, 0, sibling, me).wait_recv()
            for j, chip in enumerate(chips):
                copy(t, 4 + j, (*chip, 1 - c), me).wait_recv()
        for cp in sent:
            cp.wait_send()
        for cp in local:
            cp.wait()

    any_spec = pl.BlockSpec(memory_space=pl.ANY)
    return pl.pallas_call(
        body, name=name, in_specs=[any_spec] * n_t, out_specs=[any_spec] * n_t,
        out_shape=[SDS((N_DEV,) + a.shape, a.dtype) for a in tensors],
        scratch_shapes=[pltpu.SemaphoreType.DMA((n_t, N_DEV - 1)), pltpu.SemaphoreType.DMA((n_t, N_DEV - 1)),
                        pltpu.SemaphoreType.DMA((n_t,))],
    )(*tensors)


def _exchange_start(tensors, name, gather=False, after=None):
    n_t = len(tensors)

    def body(*refs):
        ins, lands = refs[:n_t], refs[n_t:2 * n_t]
        send_sem, recv_sem = refs[2 * n_t + 1:2 * n_t + 3]
        token = refs[-1]
        me = _place()
        mine = _index(me)
        for t in range(n_t):
            for k in range(1, N_DEV):
                peer = _flip(me, k)
                pltpu.make_async_remote_copy(
                    src_ref=ins[t] if gather else ins[t].at[_index(peer)], dst_ref=lands[t].at[mine],
                    send_sem=send_sem, recv_sem=recv_sem, device_id=peer, device_id_type=pl.DeviceIdType.MESH).start()
        token[...] = jnp.zeros_like(token)

    hbm = pl.BlockSpec(memory_space=pltpu.HBM)
    sem = pl.BlockSpec(memory_space=pltpu.SEMAPHORE)
    one = pltpu.SemaphoreType.DMA(())
    land_shape = lambda a: ((N_DEV,) + a.shape) if gather else a.shape
    bufs = ([pltpu.HBM(a.shape, a.dtype) for a in tensors] + [pltpu.HBM(land_shape(a), a.dtype) for a in tensors])
    after = jnp.zeros((8, 128), F32) if after is None else after
    outs = pl.pallas_call(
        body, name=name, in_specs=[hbm] * (2 * n_t) + [pl.BlockSpec(memory_space=pl.ANY)],
        out_specs=[sem, sem] + [hbm] * (2 * n_t) + [pl.BlockSpec(memory_space=pltpu.VMEM)],
        out_shape=[one, one] + bufs + [SDS((8, 128), F32)],
        input_output_aliases={i: 2 + i for i in range(2 * n_t)},
        compiler_params=pltpu.CompilerParams(has_side_effects=pltpu.SideEffectType.DATAFLOW_SIDE_EFFECTING),
    )(*[pltpu.with_memory_space_constraint(a, pltpu.HBM) for a in tensors],
      *[pltpu.with_memory_space_constraint(lax.empty(land_shape(a), a.dtype), pltpu.HBM) for a in tensors], after)
    return outs[:-1], outs[-1][0, 0]


def _exchange_wait(state, after, name, gather=False):
    n_t = (len(state) - 2) // 2

    def body(*refs):
        send_sem, recv_sem = refs[0:2]
        ins, lands = refs[2:2 + n_t], refs[2 + n_t:2 + 2 * n_t]
        me = _place()
        for t in range(n_t):
            for k in range(1, N_DEV):
                peer = _flip(me, k)
                copy = pltpu.make_async_remote_copy(
                    src_ref=ins[t] if gather else ins[t].at[_index(peer)], dst_ref=lands[t].at[_index(peer)],
                    send_sem=send_sem, recv_sem=recv_sem, device_id=peer, device_id_type=pl.DeviceIdType.MESH)
                copy.wait_send()
                copy.wait_recv()

    hbm = pl.BlockSpec(memory_space=pltpu.HBM)
    sem = pl.BlockSpec(memory_space=pltpu.SEMAPHORE)
    bufs = [pltpu.HBM(a.shape, a.dtype) for a in state[2:]]
    outs = pl.pallas_call(
        body, name=name, in_specs=[sem, sem] + [hbm] * (2 * n_t) + [pl.BlockSpec(memory_space=pl.ANY)],
        out_specs=[hbm] * (2 * n_t), out_shape=bufs,
        input_output_aliases={2 + i: i for i in range(2 * n_t)},
        compiler_params=pltpu.CompilerParams(has_side_effects=pltpu.SideEffectType.DATAFLOW_SIDE_EFFECTING),
    )(*state, after)
    return outs[n_t:]


def _sum_slots(parts, name):
    n, rows, w = parts.shape
    tb = 8

    def body(p_ref, o_ref):
        acc = p_ref[0].astype(F32)
        for s in range(1, n):
            acc = acc + p_ref[s].astype(F32)
        o_ref[...] = acc

    return pl.pallas_call(
        body, name=name, grid=(rows // tb,),
        in_specs=[pl.BlockSpec((n, tb, w), lambda i: (0, i, 0))], out_specs=pl.BlockSpec((tb, w), lambda i: (i, 0)),
        out_shape=SDS((rows, w), F32), compiler_params=_params(("parallel",)),
    )(parts)


def _adamw(wt, m, v, parts, name, own=None):
    shape = wt.shape
    parts = parts if isinstance(parts, (list, tuple)) else [parts]
    n, w = parts[0].shape[0], shape[-1]
    rows = math.prod(shape[:-1])
    per = rows // len(parts)
    step = 16 if parts[0].dtype == BF16 else 8
    tb = max([t for t in range(step, 257, step) if per % t == 0] or [per])
    nb = per // tb
    c1 = 1.0 / (1.0 - ADAM_B1 ** ADAM_STEP)
    c2 = 1.0 / (1.0 - ADAM_B2 ** ADAM_STEP)
    state = [a.reshape(rows, w) for a in (wt, m, v)]
    n_in = 4 if own is None else 5
    outs = None
    for l in reversed(range(len(parts))):
        def body(*refs):
            idx_ref = None if own is None else refs[0]
            w_ref, m_ref, v_ref, p_ref = refs[n_in - 4:n_in] if own is None else refs[1:5]
            g_out, d_out, m_out, v_out = refs[-4:]
            g = None
            for s in range(n):
                term = p_ref[s] if own is None else jnp.where(idx_ref[0] == s, refs[5][0], p_ref[s])
                g = term.astype(F32) if g is None else g + term.astype(F32)
            m_new = ADAM_B1 * m_ref[...] + (1.0 - ADAM_B1) * g
            v_new = ADAM_B2 * v_ref[...] + (1.0 - ADAM_B2) * (g * g)
            g_out[...] = g
            m_out[...] = m_new
            v_out[...] = v_new
            d_out[...] = -ADAM_LR * ((m_new * c1) / (jnp.sqrt(v_new * c2) + ADAM_EPS) + ADAM_WD * w_ref[...])

        row = pl.BlockSpec((tb, w), lambda i, *_, l=l: (l * nb + i, 0))
        in_specs = [row, row, row, pl.BlockSpec((n, tb, w), lambda i, *_: (0, i, 0))]
        args = [*state, parts[l].reshape(n, per, w)]
        if own is not None:
            in_specs.append(pl.BlockSpec((1, tb, w), lambda i, idx: (idx[0], i, 0)))
            args.append(own[l].reshape(n, per, w))
        prev = [] if outs is None else list(outs)
        in_specs += [pl.BlockSpec(memory_space=pl.ANY)] * len(prev)
        n_pre = 0 if own is None else 1
        call = dict(name=f"{name}_{l}", out_shape=[SDS((rows, w), F32)] * 4,
                    input_output_aliases={n_pre + len(args) + k: k for k in range(len(prev))},
                    compiler_params=_params(("parallel",)))
        if own is None:
            outs = pl.pallas_call(body, grid=(nb,), in_specs=in_specs, out_specs=[row] * 4, **call)(*args, *prev)
        else:
            spec = pltpu.PrefetchScalarGridSpec(num_scalar_prefetch=1, grid=(nb,), in_specs=in_specs, out_specs=[row] * 4)
            idx = jnp.reshape(_index(_place()), (1,)).astype(jnp.int32)
            outs = pl.pallas_call(body, grid_spec=spec, **call)(idx, *args, *prev)
    return [o.reshape(shape) for o in outs]


_BIG = [("w_in", 2), ("mla_w_q_up", 2), ("mla_w_kv_up", 2), ("w_branch", 3), ("w_o", 1), ("ffn_w_up", 2), ("ffn_w_down", 1)]
_SMALL_SHARDED = [("meta_tokens", 1), ("ffn_conv_w", 2)]
_REPLICATED = ["norm1_g", "fox_forget_b", "fox_q_g", "fox_k_g", "mla_q_a_g", "mla_kv_a_g", "mla_q_g", "mla_k_g",
               "swa_q_g", "swa_k_g", "swa_sinks", "norm2_g", "ffn_conv_b"]
_ORDER = ["meta_tokens", "norm1_g", "w_in", "fox_forget_b", "fox_q_g", "fox_k_g", "mla_q_a_g", "mla_w_q_up",
          "mla_kv_a_g", "mla_w_kv_up", "mla_q_g", "mla_k_g", "swa_q_g", "swa_k_g", "swa_sinks", "w_branch", "w_o",
          "norm2_g", "ffn_w_up", "ffn_conv_w", "ffn_conv_b", "ffn_w_down"]


def _flat_rows(vecs, dtype, row_mult):
    flat = jnp.concatenate([a.reshape(-1).astype(dtype) for a in vecs])
    rows = -(-flat.shape[0] // (1024 * row_mult)) * row_mult
    return jnp.pad(flat, (0, rows * 1024 - flat.shape[0])).reshape(rows, 1024)


def _unflatten(flat, shapes):
    out, off = [], 0
    for s in shapes:
        n = math.prod(s)
        out.append(flat[off:off + n].reshape(s))
        off += n
    return out


def _to_full(blocks, axis):
    moved = jnp.moveaxis(blocks, 0, axis)
    s = moved.shape
    return moved.reshape(s[:axis] + (s[axis] * s[axis + 1],) + s[axis + 2:])


def _to_blocks(full, axis):
    s = full.shape
    split = full.reshape(s[:axis] + (N_DEV, s[axis] // N_DEV) + s[axis + 1:])
    return jnp.moveaxis(split, axis, 0)


def kernel(x, meta_tokens, norm1_g, w_in, fox_forget_b, fox_q_g, fox_k_g, mla_q_a_g, mla_w_q_up, mla_kv_a_g, mla_w_kv_up, mla_q_g, mla_k_g, swa_q_g, swa_k_g, swa_sinks, w_branch, w_o, norm2_g, ffn_w_up, ffn_conv_w, ffn_conv_b, ffn_w_down, loss_target, m_meta_tokens, m_norm1_g, m_w_in, m_fox_forget_b, m_fox_q_g, m_fox_k_g, m_mla_q_a_g, m_mla_w_q_up, m_mla_kv_a_g, m_mla_w_kv_up, m_mla_q_g, m_mla_k_g, m_swa_q_g, m_swa_k_g, m_swa_sinks, m_w_branch, m_w_o, m_norm2_g, m_ffn_w_up, m_ffn_conv_w, m_ffn_conv_b, m_ffn_w_down, v_meta_tokens, v_norm1_g, v_w_in, v_fox_forget_b, v_fox_q_g, v_fox_k_g, v_mla_q_a_g, v_mla_w_q_up, v_mla_kv_a_g, v_mla_w_kv_up, v_mla_q_g, v_mla_k_g, v_swa_q_g, v_swa_k_g, v_swa_sinks, v_w_branch, v_w_o, v_norm2_g, v_ffn_w_up, v_ffn_conv_w, v_ffn_conv_b, v_ffn_w_down):
    wl = dict(zip(_ORDER, (meta_tokens, norm1_g, w_in, fox_forget_b, fox_q_g, fox_k_g, mla_q_a_g, mla_w_q_up,
                           mla_kv_a_g, mla_w_kv_up, mla_q_g, mla_k_g, swa_q_g, swa_k_g, swa_sinks, w_branch, w_o,
                           norm2_g, ffn_w_up, ffn_conv_w, ffn_conv_b, ffn_w_down)))
    ml = dict(zip(_ORDER, (m_meta_tokens, m_norm1_g, m_w_in, m_fox_forget_b, m_fox_q_g, m_fox_k_g, m_mla_q_a_g,
                           m_mla_w_q_up, m_mla_kv_a_g, m_mla_w_kv_up, m_mla_q_g, m_mla_k_g, m_swa_q_g, m_swa_k_g,
                           m_swa_sinks, m_w_branch, m_w_o, m_norm2_g, m_ffn_w_up, m_ffn_conv_w, m_ffn_conv_b,
                           m_ffn_w_down)))
    vl = dict(zip(_ORDER, (v_meta_tokens, v_norm1_g, v_w_in, v_fox_forget_b, v_fox_q_g, v_fox_k_g, v_mla_q_a_g,
                           v_mla_w_q_up, v_mla_kv_a_g, v_mla_w_kv_up, v_mla_q_g, v_mla_k_g, v_swa_q_g, v_swa_k_g,
                           v_swa_sinks, v_w_branch, v_w_o, v_norm2_g, v_ffn_w_up, v_ffn_conv_w, v_ffn_conv_b,
                           v_ffn_w_down)))
    small_sh = [n for n, _ in _SMALL_SHARDED]
    big = [n for n, _ in _BIG]
    axis_of = dict(_BIG)
    idx = _index(_place())

    def to_full(n, blocks, own=None):
        if own is not None:
            sel = (jnp.arange(N_DEV) == idx).reshape((N_DEV,) + (1,) * own.ndim)
            blocks = jnp.where(sel, own[None], blocks)
        return _to_full(blocks, axis_of[n] - 1)

    local = {(n, l): wl[n][l].astype(BF16) for n in big for l in range(DEPTH)}
    got = _gather([local[(n, 0)] for n in _MIX_BIG] + [wl[n] for n in small_sh], "gather_weights_l0_mix")
    full = {n: wl[n] for n in _REPLICATED}
    for (n, axis), blocks in zip(_SMALL_SHARDED, got[len(_MIX_BIG):]):
        full[n] = _to_full(blocks, axis)
    ready = {(n, 0): to_full(n, blocks) for n, blocks in zip(_MIX_BIG, got)}
    later = {"l0_ffn": [(n, 0) for n in _FFN_BIG], "l1": [(n, 1) for n in big]}
    states = {}
    for key, names in later.items():
        states[key], tick = _exchange_start([local[e] for e in names], "gather_weights_" + key + "_start", True, got[0])
        full["norm1_g"] = full["norm1_g"] + tick

    def fetch(l, stage, after):
        key = "l0_ffn" if l == 0 else "l1"
        if (l, stage) != (0, "mix") and key in states:
            lands = _exchange_wait(states.pop(key), after, "gather_weights_" + key + "_wait", True)
            ready.update({e: to_full(e[0], blocks, local[e]) for e, blocks in zip(later[key], lands)})
        return {n: ready[(n, l)] for n in (_MIX_BIG if stage == "mix" else _FFN_BIG)}

    blocks_of = lambda g, names: [_to_blocks(g[n], axis_of[n] - 1).astype(BF16) for n in names]
    early = {}

    def hook(l, stage, g):
        if l == DEPTH - 1 and stage == "mix":
            key, names = "l1", big
        elif l == 0:
            key, names = "l0_" + stage, (_FFN_BIG if stage == "ffn" else _MIX_BIG)
        else:
            return None
        sends = blocks_of(g, names)
        state, tick = _exchange_start(sends, "exchange_grads_" + key + "_start")
        early[key] = (names, l, sends, state)
        return tick

    loss, grad_x, grad_meta, grads = _local_step(x[0], loss_target[0], full, hook, fetch)
    result = {kind: {} for kind in ("grad", "delta", "new_m", "new_v")}
    landed, sent = {}, {}
    after = early["l0_mix"][2][0]
    for key in ("l1", "l0_ffn"):
        names, l, sends, state = early[key]
        got = _exchange_wait(state, after, "exchange_grads_" + key + "_wait")
        landed.update({(n, l): p for n, p in zip(names, got)})
        sent.update({(n, l): p for n, p in zip(names, sends)})

    def update(names):
        for n in names:
            outs = _adamw(wl[n], ml[n], vl[n], [landed[(n, l)] for l in range(DEPTH)], "adamw_" + n,
                          [sent[(n, l)] for l in range(DEPTH)])
            for kind, val in zip(result, outs):
                result[kind][n] = val

    update(_FFN_BIG)
    grads = {k: jnp.stack([grads[l][k] for l in range(DEPTH)]) for k in grads[0] if k not in big}
    grads["meta_tokens"] = grad_meta

    small_full = _REPLICATED + small_sh
    mine_small = _flat_rows([grads[n] for n in small_full] + [loss], F32, 8)
    total_small = _sum_slots(_gather([mine_small], "gather_small_grads")[0], "sum_small_grads").reshape(-1)
    pieces = _unflatten(total_small, [grads[n].shape for n in small_full] + [()])
    loss_total = pieces[-1]
    g_small = dict(zip(small_full, pieces[:-1]))
    for n, axis in _SMALL_SHARDED:
        size = wl[n].shape[axis]
        g_small[n] = lax.dynamic_slice_in_dim(g_small[n], idx * size, size, axis)
    flat = lambda d: _flat_rows([d[n] for n in small_full], F32, 8)
    small_out = _adamw(flat(wl), flat(ml), flat(vl), flat(g_small)[None], "adamw_small")
    for kind, fs in zip(result, small_out):
        result[kind].update(zip(small_full, _unflatten(fs.reshape(-1), [wl[n].shape for n in small_full])))
    names, l, sends, state = early["l0_mix"]
    got = _exchange_wait(state, small_out[0], "exchange_grads_l0_mix_wait")
    landed.update({(n, l): p for n, p in zip(names, got)})
    sent.update({(n, l): p for n, p in zip(names, sends)})
    update(_MIX_BIG)
    outs = [loss_total, grad_x[None]]
    for kind in ("grad", "delta", "new_m", "new_v"):
        outs += [result[kind][n] for n in _ORDER]
    return tuple(outs)
```

```python
import functools
import math

import numpy as np
import jax
import jax.numpy as jnp
from jax import lax
from jax.experimental import pallas as pl
from jax.experimental.pallas import tpu as pltpu

F32, BF16 = jnp.float32, jnp.bfloat16
SDS = jax.ShapeDtypeStruct

D_MODEL = 1024
N_META = 16
EPS = 1e-6
WINDOW = 128
ROPE_THETA = 10000.0
HEADS = 8
D_FF = 2816
DEPTH = 2
N_DEV = 8
ADAM_LR, ADAM_B1, ADAM_B2, ADAM_EPS, ADAM_WD, ADAM_STEP = 0.001, 0.9, 0.999, 1e-08, 0.01, 10

ROW_ALIGN = 384
TILE_MM = 384
TILE_ROW = 128
TILE_ATT = 384
TILE_POST = 128
PAIRS = 2
VMEM_LIMIT = 56 * 1024 * 1024

GATES_W = 3072
OTHER_W = 2816
IN_W = GATES_W + OTHER_W
O_FQ, O_FK, O_FV, O_SQ, O_SK, O_SV, O_CQ, O_CKV, O_MISC = 0, 512, 1024, 1536, 2048, 2176, 2304, 2560, 2688
FF_LANE = 32

NEG = -1e30


def _dot(a, b):
    return jnp.dot(a, b, preferred_element_type=F32)


def _dot_nt(a, b):
    return lax.dot_general(a, b, (((1,), (1,)), ((), ())), preferred_element_type=F32)


def _dot_tn(a, b):
    return lax.dot_general(a, b, (((0,), (0,)), ((), ())), preferred_element_type=F32)


def _params(sem):
    return pltpu.CompilerParams(dimension_semantics=sem, vmem_limit_bytes=VMEM_LIMIT)


def _rms(x, g):
    return x * lax.rsqrt(jnp.mean(x * x, axis=-1, keepdims=True) + EPS) * g


def _split_dot(x, m, pieces=2):
    acc, rest = None, x
    for _ in range(pieces):
        part = rest.astype(BF16)
        rest = rest - part.astype(F32)
        acc = _dot(part, m) if acc is None else acc + _dot(part, m)
    return acc


@jax.custom_vjp
def _sel(x, m, mt):
    return _split_dot(x, m)


_sel.defvjp(lambda x, m, mt: (_split_dot(x, m), (m, mt)), lambda res, dy: (_split_dot(dy, res[1]), None, None))


@jax.custom_vjp
def _mm(x, w):
    return _dot(x.astype(BF16), w.astype(BF16))


def _mm_bwd(res, dy):
    x, w = res
    dyb = dy.astype(BF16)
    return _dot_nt(dyb, w.astype(BF16)), _dot_tn(x.astype(BF16), dyb)


_mm.defvjp(lambda x, w: (_mm(x, w), (x, w)), _mm_bwd)


def _rot_impl(x):
    w = x.shape[1]
    lane = lax.broadcasted_iota(jnp.int32, x.shape, 1) % 128
    lo = (lane >= 64) & (lane < 80)
    hi = (lane >= 80) & (lane < 96)
    return jnp.where(hi, pltpu.roll(x, 16, 1), 0.0) - jnp.where(lo, pltpu.roll(x, w - 16, 1), 0.0)


@jax.custom_vjp
def _rot(x):
    return _rot_impl(x)


_rot.defvjp(lambda x: (_rot_impl(x), None), lambda _, dy: (-_rot_impl(dy),))


def _gnorm(x, g, e, et, dim):
    inv = lax.rsqrt(_sel(x * x, e, et) * (1.0 / dim) + EPS)
    return x * _sel(inv, et, e) * g


def _indicator(width, period):
    m = np.zeros((width, 128), np.float32)
    m[np.arange(width), np.arange(width) // period] = 1.0
    return m


def _consts():
    e64 = _indicator(512, 64)
    e128 = _indicator(1024, 128)
    sk = np.zeros((128, 1024), np.float32)
    for h in range(HEADS):
        sk[np.arange(32), 128 * h + 64 + np.arange(32)] = 1.0
    dup = np.zeros((128, 256), np.float32)
    for g in range(2):
        for r in range(2):
            dup[64 * g + np.arange(64), 128 * g + 64 * r + np.arange(64)] = 1.0
    mats = [e64, e64.T, e128, e128.T, sk, sk.T, dup, dup.T]
    return [jnp.asarray(m, BF16) for m in mats]


def _fold_matrix(width, period):
    m = np.zeros((width, 128), np.float32)
    m[np.arange(width), np.arange(width) % period] = 1.0
    return jnp.asarray(m, BF16)


def _rope_tables(lp):
    half = 16
    freqs = ROPE_THETA ** (-np.arange(half, dtype=np.float32) / half)
    ang = np.arange(lp, dtype=np.float32)[:, None] * freqs[None, :]
    cos = np.ones((lp, 128), np.float32)
    sin = np.zeros((lp, 128), np.float32)
    cos[:, 64:80] = np.cos(ang)
    cos[:, 80:96] = np.cos(ang)
    sin[:, 64:80] = np.sin(ang)
    sin[:, 80:96] = np.sin(ang)
    return jnp.asarray(cos), jnp.asarray(sin)


def _norm_matmul(h, g, w, tn, name):
    lp, d = h.shape
    n = w.shape[1]
    tb = TILE_MM

    def body(h_ref, g_ref, w_ref, xn_ref, y_ref):
        @pl.when(pl.program_id(1) == 0)
        def _():
            xn_ref[...] = _rms(h_ref[...], g_ref[...]).astype(BF16)

        y_ref[...] = _dot(xn_ref[...], w_ref[...])

    return pl.pallas_call(
        body, name=name, grid=(lp // tb, n // tn),
        in_specs=[pl.BlockSpec((tb, d), lambda i, j: (i, 0)), pl.BlockSpec((1, d), lambda i, j: (0, 0)),
                  pl.BlockSpec((d, tn), lambda i, j: (0, j))],
        out_specs=[pl.BlockSpec((tb, d), lambda i, j: (i, 0)), pl.BlockSpec((tb, tn), lambda i, j: (i, j))],
        out_shape=[SDS((lp, d), BF16), SDS((lp, n), F32)],
        compiler_params=_params(("parallel", "arbitrary")),
    )(h, g, w)


def _matmul_residual(a, w, res, name):
    m, k = a.shape
    n = w.shape[1]
    tb = TILE_MM

    def body(a_ref, w_ref, r_ref, o_ref):
        o_ref[...] = r_ref[...] + _dot(a_ref[...], w_ref[...])

    return pl.pallas_call(
        body, name=name, grid=(m // tb,),
        in_specs=[pl.BlockSpec((tb, k), lambda i: (i, 0)), pl.BlockSpec((k, n), lambda i: (0, 0)),
                  pl.BlockSpec((tb, n), lambda i: (i, 0))],
        out_specs=pl.BlockSpec((tb, n), lambda i: (i, 0)),
        out_shape=SDS((m, n), F32),
        compiler_params=_params(("parallel",)),
    )(a, w, res)


def _matmul_nt(dy, w, tn, name):
    m, k = dy.shape
    n = w.shape[0]
    tb = TILE_MM

    def body(dy_ref, w_ref, o_ref):
        o_ref[...] = _dot_nt(dy_ref[...].astype(BF16), w_ref[...])

    return pl.pallas_call(
        body, name=name, grid=(m // tb, n // tn),
        in_specs=[pl.BlockSpec((tb, k), lambda i, j: (i, 0)), pl.BlockSpec((tn, k), lambda i, j: (j, 0))],
        out_specs=pl.BlockSpec((tb, tn), lambda i, j: (i, j)),
        out_shape=SDS((m, n), F32),
        compiler_params=_params(("parallel", "arbitrary")),
    )(dy, w)


def _matmul_tn(x, dy, tn, name):
    m, k = x.shape
    n = dy.shape[1]
    tb = TILE_MM

    def body(x_ref, dy_ref, o_ref):
        @pl.when(pl.program_id(1) == 0)
        def _():
            o_ref[...] = jnp.zeros_like(o_ref)

        o_ref[...] += _dot_tn(x_ref[...].astype(BF16), dy_ref[...].astype(BF16))

    return pl.pallas_call(
        body, name=name, grid=(n // tn, m // tb),
        in_specs=[pl.BlockSpec((tb, k), lambda j, i: (i, 0)), pl.BlockSpec((tb, tn), lambda j, i: (i, j))],
        out_specs=pl.BlockSpec((k, tn), lambda j, i: (0, j)),
        out_shape=SDS((k, n), F32),
        compiler_params=_params(("parallel", "arbitrary")),
    )(x, dy)


def _norm_matmul_bwd(dy1, w1, dy2, w2, x, g, dres, specs, name):
    m, d = x.shape
    tb = TILE_MM
    (dy1_spec, w1_spec, dy2_spec, w2_spec) = specs

    def body(dy1_ref, w1_ref, dy2_ref, w2_ref, x_ref, g_ref, r_ref, o_ref, dg_ref):
        @pl.when(pl.program_id(0) == 0)
        def _():
            dg_ref[...] = jnp.zeros_like(dg_ref)

        dxn = _dot_nt(dy1_ref[...], w1_ref[...]) + _dot_nt(dy2_ref[...], w2_ref[...])
        _, vjp = jax.vjp(_rms, x_ref[...], g_ref[...])
        dx, dg = vjp(dxn)
        o_ref[...] = r_ref[...] + dx
        dg_ref[...] += dg

    row = pl.BlockSpec((tb, d), lambda i: (i, 0))
    vec = pl.BlockSpec((1, d), lambda i: (0, 0))
    return pl.pallas_call(
        body, name=name, grid=(m // tb,),
        in_specs=[dy1_spec, w1_spec, dy2_spec, w2_spec, row, vec, row],
        out_specs=[row, vec],
        out_shape=[SDS((m, d), F32), SDS((1, d), F32)],
        compiler_params=_params(("arbitrary",)),
    )(dy1, w1, dy2, w2, x, g, dres)


def _prep_math(pieces, prm, consts, cos, sin):
    fq, fk, sq, sk, sv, cq, ckv, misc = pieces
    gfq, gfk, gsq, gsk, fb, gqa, gkva, gmq, gmk, wq, wkk, wkv = prm
    e64, e64t, e128, e128t, skm, skt, dup, dupt = consts
    cos8 = jnp.concatenate([cos] * HEADS, axis=1)
    sin8 = jnp.concatenate([sin] * HEADS, axis=1)
    fq_n = _gnorm(fq, gfq, e64, e64t, 64)
    fk_n = _gnorm(fk, gfk, e64, e64t, 64)
    ls = jax.nn.log_sigmoid(misc + fb)
    q = _gnorm(_mm(_rms(cq, gqa), wq), gmq, e128, e128t, 96)
    mq = q * cos8 + _rot(q) * sin8
    kva = _rms(ckv, gkva)
    k = _gnorm(_mm(kva, wkk) + _sel(misc, skm, skt), gmk, e128, e128t, 96)
    mk = k * cos8 + _rot(k) * sin8
    mv = _mm(kva, wkv)
    sq_n = _gnorm(sq, gsq, e64, e64t, 64)
    sk_n = _gnorm(sk, gsk, e64[0:128], e64t[:, 0:128], 64)
    skd = _sel(sk_n, dup, dupt)
    svd = _sel(sv, dup, dupt)
    return fq_n, fk_n, ls, mq, mk, mv, sq_n, skd, svd


_PIECES = [(O_FQ, 512), (O_FK, 512), (O_SQ, 512), (O_SK, 128), (O_SV, 128), (O_CQ, 256), (O_CKV, 128), (O_MISC, 128)]
_PRM_SHAPES = [(1, 512), (1, 512), (1, 512), (1, 128), (1, 128), (1, 256), (1, 128), (1, 1024), (1, 1024),
               (256, 1024), (128, 1024), (128, 512)]
_CONST_SHAPES = [(512, 128), (128, 512), (1024, 128), (128, 1024), (128, 1024), (1024, 128), (128, 256), (256, 128)]


def _piece_specs(tb):
    def spec(off, width):
        blk = (GATES_W + off) // width
        return pl.BlockSpec((tb, width), lambda i, blk=blk: (i, blk))
    return [spec(o, w) for o, w in _PIECES] + [spec(O_FV, 512)]


def _full_specs(shapes):
    return [pl.BlockSpec(s, lambda i: (0, 0)) for s in shapes]


def _prep_fwd(proj, prm, consts, cos, sin, name):
    lp = proj.shape[0]
    tb = TILE_ROW
    row = lambda w: pl.BlockSpec((tb, w), lambda i: (i, 0))

    def body(*refs):
        pieces = [r[...] for r in refs[0:8]]
        fv = refs[8][...]
        prm_v = [r[...] for r in refs[9:21]]
        consts_v = [r[...] for r in refs[21:29]]
        cos_v, sin_v = refs[29][...], refs[30][...]
        outs = refs[31:]
        fq_n, fk_n, ls, mq, mk, mv, sq_n, skd, svd = _prep_math(pieces, prm_v, consts_v, cos_v, sin_v)
        for ref, val in zip(outs, (fq_n, fk_n, fv, mq, mk, mv, sq_n, skd, svd)):
            ref[...] = val.astype(BF16)
        outs[9][...] = ls

    widths = [512, 512, 512, 1024, 1024, 512, 512, 256, 256]
    return pl.pallas_call(
        body, name=name, grid=(lp // tb,),
        in_specs=_piece_specs(tb) + _full_specs(_PRM_SHAPES) + _full_specs(_CONST_SHAPES) + [row(128), row(128)],
        out_specs=[row(w) for w in widths] + [row(128)],
        out_shape=[SDS((lp, w), BF16) for w in widths] + [SDS((lp, 128), F32)],
        compiler_params=_params(("parallel",)),
    )(*([proj] * 9), *prm, *consts, cos, sin)


def _prep_bwd(proj, prm, consts, cos, sin, cots, folds, name):
    lp = proj.shape[0]
    tb = TILE_ROW
    row = lambda w: pl.BlockSpec((tb, w), lambda i: (i, 0))
    fold64, fold128 = folds

    def body(*refs):
        pieces = [r[...] for r in refs[0:8]]
        prm_v = [r[...] for r in refs[9:21]]
        consts_v = [r[...] for r in refs[21:29]]
        cos_v, sin_v = refs[29][...], refs[30][...]
        dfq, dfk, dfv, dmq, dmk, dmv, dsq, dskp, dsvp, dls = [r[...] for r in refs[31:41]]
        f64, f128 = refs[41][...], refs[42][...]
        d_ref = refs[43]
        g_refs = refs[44:]

        @pl.when(pl.program_id(0) == 0)
        def _():
            for r in g_refs:
                r[...] = jnp.zeros_like(r)

        def pair_sum(p):
            return jnp.concatenate([p[:, 0:128] + p[:, 128:256], p[:, 256:384] + p[:, 384:512]], axis=1)

        f = lambda pc, pr: _prep_math(pc, pr, consts_v, cos_v, sin_v)
        _, vjp = jax.vjp(f, pieces, prm_v)
        dpc, dprm = vjp((dfq, dfk, dls, dmq, dmk, dmv, dsq, pair_sum(dskp), pair_sum(dsvp)))
        d_fq, d_fk, d_sq, d_sk, d_sv, d_cq, d_ckv, d_misc = dpc
        for off, val in ((O_FQ, d_fq), (O_FK, d_fk), (O_FV, dfv), (O_SQ, d_sq), (O_SK, d_sk), (O_SV, d_sv),
                         (O_CQ, d_cq), (O_CKV, d_ckv), (O_MISC, d_misc)):
            d_ref[:, off:off + val.shape[1]] = val.astype(BF16)
        folded = {0: f64, 1: f64, 2: f64, 3: f64[0:128], 7: f128, 8: f128}
        for idx, (ref, val) in enumerate(zip(g_refs, dprm)):
            if idx in folded:
                ref[...] += _split_dot(jnp.broadcast_to(val, (8, val.shape[1])), folded[idx], 3)
            elif val.shape[0] == 1:
                ref[...] += jnp.broadcast_to(val, ref.shape)
            else:
                ref[...] += val

    g_shapes = [(8, 128), (8, 128), (8, 128), (8, 128), (8, 128), (8, 256), (8, 128), (8, 128), (8, 128),
                (256, 1024), (128, 1024), (128, 512)]
    cot_widths = [512, 512, 512, 1024, 1024, 512, 512, 512, 512, 128]
    return pl.pallas_call(
        body, name=name, grid=(lp // tb,),
        in_specs=(_piece_specs(tb) + _full_specs(_PRM_SHAPES) + _full_specs(_CONST_SHAPES) + [row(128), row(128)]
                  + [row(w) for w in cot_widths] + _full_specs([(512, 128), (1024, 128)])),
        out_specs=[row(OTHER_W)] + _full_specs(g_shapes),
        out_shape=[SDS((lp, OTHER_W), BF16)] + [SDS(s, F32) for s in g_shapes],
        compiler_params=_params(("arbitrary",)),
    )(*([proj] * 9), *prm, *consts, cos, sin, *cots, fold64, fold128)


def _cumsum(xs, reverse, name):
    lp = xs[0].shape[0]
    tb = TILE_MM
    nb = lp // tb
    n_in = len(xs)
    idx = (lambda i: (nb - 1 - i, 0)) if reverse else (lambda i: (i, 0))

    def body(*refs):
        o_ref, carry = refs[n_in], refs[n_in + 1]

        @pl.when(pl.program_id(0) == 0)
        def _():
            carry[...] = jnp.zeros_like(carry)

        x = refs[0][...]
        for r in refs[1:n_in]:
            x = x + r[...]
        r_i = lax.broadcasted_iota(jnp.int32, (tb, tb), 0)
        c_i = lax.broadcasted_iota(jnp.int32, (tb, tb), 1)
        tri = ((c_i >= r_i) if reverse else (c_i <= r_i)).astype(BF16)
        acc, rest = None, x
        for _ in range(3):
            part = rest.astype(BF16)
            rest = rest - part.astype(F32)
            acc = _dot(tri, part) if acc is None else acc + _dot(tri, part)
        o_ref[...] = acc + carry[...]
        carry[...] += jnp.sum(x, axis=0, keepdims=True)

    return pl.pallas_call(
        body, name=name, grid=(nb,),
        in_specs=[pl.BlockSpec((tb, 128), idx)] * n_in,
        out_specs=pl.BlockSpec((tb, 128), idx),
        out_shape=SDS((lp, 128), F32),
        scratch_shapes=[pltpu.VMEM((1, 128), F32)],
        compiler_params=_params(("arbitrary",)),
    )(*xs)


class _Att:
    def __init__(self, mode):
        self.mode = mode
        self.wide = mode == "mla"
        self.qw = 256 if self.wide else 128
        self.scale = (96 if mode == "mla" else 64) ** -0.5

    def resident(self, x, lo, scaled):
        if self.wide:
            return x[:, 0:128], x[:, 128:256]
        if scaled:
            x = x * jnp.asarray(self.scale, x.dtype)
        zero = jnp.zeros_like(x)
        return jnp.where(lo, x, zero), jnp.where(lo, zero, x)

    def moving(self, x):
        return (x[:, 0:128], x[:, 128:256]) if self.wide else (x, x)

    def logits(self, a, b, qpos, kpos, key_decay, slope, masked):
        s = _dot_nt(a, b)
        if self.wide:
            s = s * self.scale
        if self.mode == "fox":
            s = s - key_decay
        if self.mode == "swa":
            s = s - slope * (qpos - kpos).astype(F32)
        if masked:
            ok = kpos <= qpos
            if self.mode == "swa":
                ok = ok & ((kpos < N_META) | (qpos - kpos < WINDOW))
            s = jnp.where(ok, s, NEG)
        return s


def _halves(x, lo):
    zero = jnp.zeros_like(x)
    return jnp.where(lo, x, zero), jnp.where(lo, zero, x)


def _kv_specs(att, lp, rows):
    if att.mode == "swa":
        return (pl.BlockSpec((rows, 128), lambda g, i: (i if rows != lp else 0, g)),) * 2
    return (pl.BlockSpec((rows, PAIRS * att.qw), lambda g, i: (i if rows != lp else 0, g)),
            pl.BlockSpec((rows, PAIRS * 128), lambda g, i: (i if rows != lp else 0, g)))


def _pair_cols(att, x, pp, width):
    return x if x.shape[1] == width else x[:, pp * width:(pp + 1) * width]


def _att_fwd(att, q, k, v, extra, name):
    lp = q.shape[0]
    t = TILE_ATT
    nq = lp // t
    qw = att.qw
    mode = att.mode
    nh = 2 * PAIRS

    def body(*refs):
        q_ref, k_ref, v_ref = refs[0:3]
        o_ref, lse_ref = refs[-2:]
        g, qi = pl.program_id(0), pl.program_id(1)
        lo = lax.broadcasted_iota(jnp.int32, (1, 128), 1) < 64
        q_all = q_ref[...]
        q_heads = [h for pp in range(PAIRS) for h in att.resident(_pair_cols(att, q_all, pp, qw), lo, True)]
        qpos = qi * t + lax.broadcasted_iota(jnp.int32, (t, 1), 0)

        def step(first, cols, carry, masked):
            ks = pl.multiple_of(first, 128)
            kc, vc = k_ref[pl.ds(ks, cols), :], v_ref[pl.ds(ks, cols), :]
            kpos = first + lax.broadcasted_iota(jnp.int32, (1, cols), 1)
            out = []
            for h in range(nh):
                pp = h // 2
                m, l, acc = carry[3 * h:3 * h + 3]
                k_h = att.moving(_pair_cols(att, kc, pp, qw))[h % 2]
                decay = refs[3][h, :, pl.ds(ks, cols)] if mode == "fox" else None
                slope = refs[4][nh * g + h] if mode == "swa" else None
                s = att.logits(q_heads[h], k_h, qpos, kpos, decay, slope, masked)
                m_new = jnp.maximum(m, jnp.max(s, axis=-1, keepdims=True))
                alpha = jnp.exp(m - m_new)
                pe = jnp.exp(s - m_new)
                l = alpha * l + jnp.sum(pe, axis=-1, keepdims=True)
                acc = alpha * acc + _dot(pe.astype(BF16), _pair_cols(att, vc, pp, 128))
                out += [m_new, l, acc]
            return tuple(out)

        init = []
        for h in range(nh):
            if mode == "swa":
                init += [jnp.full((t, 1), refs[3][nh * g + h], F32), jnp.ones((t, 1), F32)]
            else:
                init += [jnp.full((t, 1), NEG, F32), jnp.zeros((t, 1), F32)]
            init.append(jnp.zeros((t, 128), F32))
        if mode == "swa":
            band = jnp.maximum(qi * t - WINDOW, 0)
            carry = lax.fori_loop(0, (band >= 128).astype(jnp.int32), lambda j, c: step(0, 128, c, True), tuple(init))
            carry = step(band, t + WINDOW, carry, True)
        else:
            carry = lax.fori_loop(0, qi // 2, lambda j, c: step(2 * j * t, 2 * t, c, False), tuple(init))
            carry = lax.fori_loop(0, qi % 2, lambda j, c: step((qi - 1) * t, t, c, False), carry)
            carry = step(qi * t, t, carry, True)
        outs = []
        for pp in range(PAIRS):
            (ma, la, acca), (mb, lb, accb) = carry[6 * pp:6 * pp + 3], carry[6 * pp + 3:6 * pp + 6]
            outs.append(jnp.where(lo, acca / la, accb / lb).astype(BF16))
            lse_ref[2 * pp] = ma + jnp.log(la)
            lse_ref[2 * pp + 1] = mb + jnp.log(lb)
        o_ref[...] = jnp.concatenate(outs, axis=1)

    in_specs = [pl.BlockSpec((t, PAIRS * qw), lambda g, i: (i, g)), *_kv_specs(att, lp, lp)]
    if mode == "fox":
        in_specs += [pl.BlockSpec((nh, 1, lp), lambda g, i: (g, 0, 0))]
    if mode == "swa":
        in_specs += [pl.BlockSpec(memory_space=pltpu.SMEM)] * 2
    return pl.pallas_call(
        body, name=name, grid=(4 // PAIRS, nq), in_specs=in_specs,
        out_specs=[pl.BlockSpec((t, PAIRS * 128), lambda g, i: (i, g)), pl.BlockSpec((nh, t, 1), lambda g, i: (g, i, 0))],
        out_shape=[SDS((lp, 512), BF16), SDS((HEADS, lp, 1), F32)],
        compiler_params=_params(("parallel", "arbitrary")),
    )(q, k, v, *extra)


def _att_dq(att, q, k, v, o, do, lse, extra, name):
    lp = q.shape[0]
    t = TILE_ATT
    nq = lp // t
    qw = att.qw
    mode = att.mode
    nh = 2 * PAIRS

    def body(*refs):
        q_ref, k_ref, v_ref, o_ref, do_ref, lse_ref = refs[0:6]
        n_out = 2 if mode == "mla" else 3
        outs = refs[len(refs) - n_out:]
        dq_ref, delta_ref = outs[0:2]
        g, qi = pl.program_id(0), pl.program_id(1)
        lo = lax.broadcasted_iota(jnp.int32, (1, 128), 1) < 64
        q_all, do_all = q_ref[...], do_ref[...]
        prod = do_all.astype(F32) * o_ref[...].astype(F32)
        q_heads, do_heads, delta = [], [], []
        for pp in range(PAIRS):
            q_heads += att.resident(_pair_cols(att, q_all, pp, qw), lo, True)
            do_heads += _halves(_pair_cols(att, do_all, pp, 128), lo)
            pr_pp = _pair_cols(att, prod, pp, 128)
            delta += [jnp.sum(jnp.where(lo, pr_pp, 0.0), axis=-1, keepdims=True),
                      jnp.sum(jnp.where(lo, 0.0, pr_pp), axis=-1, keepdims=True)]
        lse_v = [lse_ref[h] for h in range(nh)]
        qpos = qi * t + lax.broadcasted_iota(jnp.int32, (t, 1), 0)

        def step(first, cols, carry, masked):
            ks = pl.multiple_of(first, 128)
            kc, vc = k_ref[pl.ds(ks, cols), :], v_ref[pl.ds(ks, cols), :]
            kpos = first + lax.broadcasted_iota(jnp.int32, (1, cols), 1)
            out = []
            for h in range(nh):
                pp = h // 2
                k_h = att.moving(_pair_cols(att, kc, pp, qw))[h % 2]
                decay = refs[6][h, :, pl.ds(ks, cols)] if mode == "fox" else None
                slope = refs[7][nh * g + h] if mode == "swa" else None
                s = att.logits(q_heads[h], k_h, qpos, kpos, decay, slope, masked)
                pr = jnp.exp(s - lse_v[h])
                ds = pr * (_dot_nt(do_heads[h], _pair_cols(att, vc, pp, 128)) - delta[h])
                out.append(carry[2 * h] + _dot(ds.astype(BF16), k_h))
                out.append(carry[2 * h + 1] + jnp.sum(ds, axis=-1, keepdims=True) if mode == "fox" else carry[2 * h + 1])
            return tuple(out)

        init = (jnp.zeros((t, 128), F32), jnp.zeros((t, 1), F32)) * nh
        if mode == "swa":
            band = jnp.maximum(qi * t - WINDOW, 0)
            carry = lax.fori_loop(0, (band >= 128).astype(jnp.int32), lambda j, c: step(0, 128, c, True), init)
            carry = step(band, t + WINDOW, carry, True)
        else:
            carry = lax.fori_loop(0, qi // 2, lambda j, c: step(2 * j * t, 2 * t, c, False), init)
            carry = lax.fori_loop(0, qi % 2, lambda j, c: step((qi - 1) * t, t, c, False), carry)
            carry = step(qi * t, t, carry, True)
        dq = []
        for pp in range(PAIRS):
            dqa, dca, dqb, dcb = carry[4 * pp:4 * pp + 4]
            dq += [dqa, dqb] if att.wide else [jnp.where(lo, dqa, dqb)]
            if mode == "fox":
                outs[2][2 * pp] = dca
                outs[2][2 * pp + 1] = dcb
        dq_ref[...] = jnp.concatenate(dq, axis=1) * att.scale
        for h in range(nh):
            delta_ref[h] = delta[h]
        if mode == "swa":
            ds_ref = outs[2]

            @pl.when(qi == 0)
            def _():
                ds_ref[...] = jnp.zeros_like(ds_ref)

            lane = lax.broadcasted_iota(jnp.int32, (8, 128), 1)
            acc = jnp.zeros((8, 128), F32)
            for h in range(nh):
                tot = -jnp.sum(jnp.exp(refs[6][nh * g + h] - lse_v[h]) * delta[h])
                acc = acc + jnp.where(lane == h, tot, 0.0)
            ds_ref[0] += acc

    col = pl.BlockSpec((nh, t, 1), lambda g, i: (g, i, 0))
    in_specs = [pl.BlockSpec((t, PAIRS * qw), lambda g, i: (i, g)), *_kv_specs(att, lp, lp),
                pl.BlockSpec((t, PAIRS * 128), lambda g, i: (i, g)), pl.BlockSpec((t, PAIRS * 128), lambda g, i: (i, g)), col]
    out_specs = [pl.BlockSpec((t, PAIRS * qw), lambda g, i: (i, g)), col]
    out_shape = [SDS((lp, 4 * qw), F32), SDS((HEADS, lp, 1), F32)]
    if mode == "fox":
        in_specs += [pl.BlockSpec((nh, 1, lp), lambda g, i: (g, 0, 0))]
        out_specs.append(col)
        out_shape.append(SDS((HEADS, lp, 1), F32))
    if mode == "swa":
        in_specs += [pl.BlockSpec(memory_space=pltpu.SMEM)] * 2
        out_specs.append(pl.BlockSpec((1, 8, 128), lambda g, i: (g, 0, 0)))
        out_shape.append(SDS((4 // PAIRS, 8, 128), F32))
    return pl.pallas_call(
        body, name=name, grid=(4 // PAIRS, nq), in_specs=in_specs, out_specs=out_specs, out_shape=out_shape,
        compiler_params=_params(("parallel", "arbitrary")),
    )(q, k, v, o, do, lse, *extra)


def _att_dkv(att, q, k, v, do, lse_row, delta_row, extra, name):
    lp = q.shape[0]
    t = TILE_ATT
    nq = lp // t
    qw = att.qw
    mode = att.mode
    nh = 2 * PAIRS

    def body(*refs):
        q_ref, k_ref, v_ref, do_ref, lse_ref, delta_ref = refs[0:6]
        n_out = 3 if mode == "fox" else 2
        outs = refs[len(refs) - n_out:]
        dk_ref, dv_ref = outs[0:2]
        g, kj = pl.program_id(0), pl.program_id(1)
        lo = lax.broadcasted_iota(jnp.int32, (1, 128), 1) < 64
        k_all, v_all = k_ref[...], v_ref[...]
        k_heads, v_heads = [], []
        for pp in range(PAIRS):
            k_heads += att.resident(_pair_cols(att, k_all, pp, qw), lo, True)
            v_heads += _halves(_pair_cols(att, v_all, pp, 128), lo)
        kpos = kj * t + lax.broadcasted_iota(jnp.int32, (t, 1), 0)

        def step(first, cols, carry, masked):
            qs = pl.multiple_of(first, 128)
            qc, doc = q_ref[pl.ds(qs, cols), :], do_ref[pl.ds(qs, cols), :]
            qpos = first + lax.broadcasted_iota(jnp.int32, (1, cols), 1)
            out = []
            for h in range(nh):
                pp = h // 2
                dk_acc, dv_acc, dc_acc = carry[3 * h:3 * h + 3]
                q_h = att.moving(_pair_cols(att, qc, pp, qw))[h % 2]
                do_h = _pair_cols(att, doc, pp, 128)
                decay = refs[6][h] if mode == "fox" else None
                slope = refs[6][nh * g + h] if mode == "swa" else None
                st = att.logits(k_heads[h], q_h, qpos, kpos, decay, slope, masked)
                pt = jnp.exp(st - lse_ref[h, :, pl.ds(qs, cols)])
                dst = pt * (_dot_nt(v_heads[h], do_h) - delta_ref[h, :, pl.ds(qs, cols)])
                dv_acc = dv_acc + _dot(pt.astype(BF16), do_h)
                dk_acc = dk_acc + _dot(dst.astype(BF16), q_h)
                if mode == "fox":
                    dc_acc = dc_acc - jnp.sum(dst, axis=-1, keepdims=True)
                out += [dk_acc, dv_acc, dc_acc]
            return tuple(out)

        init = (jnp.zeros((t, 128), F32), jnp.zeros((t, 128), F32), jnp.zeros((t, 1), F32)) * nh
        if mode == "swa":
            carry = lax.fori_loop(0, jnp.where(kj == 0, nq, 0), lambda qi, c: step(qi * t, t, c, True), init)
            near = jnp.minimum(kj * t, lp - (t + WINDOW))
            carry = lax.fori_loop(0, (kj > 0).astype(jnp.int32), lambda j, c: step(near, t + WINDOW, c, True), carry)
        else:
            carry = step(kj * t, t, init, True)
            rest = nq - 1 - kj
            carry = lax.fori_loop(0, rest // 2, lambda j, c: step((kj + 1 + 2 * j) * t, 2 * t, c, False), carry)
            carry = lax.fori_loop(0, rest % 2, lambda j, c: step((nq - 1) * t, t, c, False), carry)
        dk, dv = [], []
        for pp in range(PAIRS):
            dka, dva, dca, dkb, dvb, dcb = carry[6 * pp:6 * pp + 6]
            dk += [dka, dkb] if att.wide else [jnp.where(lo, dka, dkb)]
            dv.append(jnp.where(lo, dva, dvb))
            if mode == "fox":
                outs[2][2 * pp] = dca
                outs[2][2 * pp + 1] = dcb
        dk_ref[...] = jnp.concatenate(dk, axis=1) * att.scale
        dv_ref[...] = jnp.concatenate(dv, axis=1)

    rowv = pl.BlockSpec((nh, 1, lp), lambda g, j: (g, 0, 0))
    col = pl.BlockSpec((nh, t, 1), lambda g, j: (g, j, 0))
    in_specs = [pl.BlockSpec((lp, PAIRS * qw), lambda g, j: (0, g)), *_kv_specs(att, lp, t),
                pl.BlockSpec((lp, PAIRS * 128), lambda g, j: (0, g)), rowv, rowv]
    out_specs = [pl.BlockSpec((t, PAIRS * qw), lambda g, j: (j, g)), pl.BlockSpec((t, PAIRS * 128), lambda g, j: (j, g))]
    out_shape = [SDS((lp, 4 * qw), F32), SDS((lp, 512), F32)]
    if mode == "fox":
        in_specs += [col]
        out_specs.append(col)
        out_shape.append(SDS((HEADS, lp, 1), F32))
    if mode == "swa":
        in_specs += [pl.BlockSpec(memory_space=pltpu.SMEM)]
    return pl.pallas_call(
        body, name=name, grid=(4 // PAIRS, nq), in_specs=in_specs, out_specs=out_specs, out_shape=out_shape,
        compiler_params=_params(("parallel", "arbitrary")),
    )(q, k, v, do, lse_row, delta_row, *extra)


def _post_fwd(h, proj, outs, wb, wo, name):
    lp, d = h.shape
    tb = TILE_POST
    row = lambda w: pl.BlockSpec((tb, w), lambda i: (i, 0))

    def body(h_ref, g0, g1, g2, oa, ob, oc, wb_ref, wo_ref, o_ref):
        merged = jnp.zeros((tb, d), F32)
        for n, (g_ref, br) in enumerate(((g0, oa), (g1, ob), (g2, oc))):
            merged = merged + jax.nn.sigmoid(g_ref[...]) * _dot(br[...], wb_ref[n])
        o_ref[...] = h_ref[...] + _dot(merged.astype(BF16), wo_ref[...])

    gate = lambda n: pl.BlockSpec((tb, d), lambda i, n=n: (i, n))
    return pl.pallas_call(
        body, name=name, grid=(lp // tb,),
        in_specs=[row(d), gate(0), gate(1), gate(2), row(512), row(512), row(512),
                  pl.BlockSpec((3, 512, d), lambda i: (0, 0, 0)), pl.BlockSpec((d, d), lambda i: (0, 0))],
        out_specs=row(d), out_shape=SDS((lp, d), F32),
        compiler_params=_params(("parallel",)),
    )(h, proj, proj, proj, *outs, wb, wo)


def _post_bwd(dh, proj, outs, wb, wo, name):
    lp, d = dh.shape
    tb = TILE_POST
    row = lambda w: pl.BlockSpec((tb, w), lambda i: (i, 0))

    def body(dh_ref, g0, g1, g2, oa, ob, oc, wb_ref, wo_ref, dg_ref, doa, dob, doc, dwb_ref, dwo_ref):
        @pl.when(pl.program_id(0) == 0)
        def _():
            dwb_ref[...] = jnp.zeros_like(dwb_ref)
            dwo_ref[...] = jnp.zeros_like(dwo_ref)

        dhb = dh_ref[...].astype(BF16)
        dm = _dot_nt(dhb, wo_ref[...])
        merged = jnp.zeros((tb, d), F32)
        for n, (g_ref, br, do_ref) in enumerate(((g0, oa, doa), (g1, ob, dob), (g2, oc, doc))):
            gate = jax.nn.sigmoid(g_ref[...])
            o_n = br[...]
            y = _dot(o_n, wb_ref[n])
            merged = merged + gate * y
            dy = (dm * gate).astype(BF16)
            dg_ref[:, n * d:(n + 1) * d] = (dm * y * gate * (1.0 - gate)).astype(BF16)
            do_ref[...] = _dot_nt(dy, wb_ref[n]).astype(BF16)
            dwb_ref[n] += _dot_tn(o_n, dy)
        dwo_ref[...] += _dot_tn(merged.astype(BF16), dhb)

    gate = lambda n: pl.BlockSpec((tb, d), lambda i, n=n: (i, n))
    wb_spec = pl.BlockSpec((3, 512, d), lambda i: (0, 0, 0))
    wo_spec = pl.BlockSpec((d, d), lambda i: (0, 0))
    return pl.pallas_call(
        body, name=name, grid=(lp // tb,),
        in_specs=[row(d), gate(0), gate(1), gate(2), row(512), row(512), row(512), wb_spec, wo_spec],
        out_specs=[row(GATES_W), row(512), row(512), row(512), wb_spec, wo_spec],
        out_shape=[SDS((lp, GATES_W), BF16)] + [SDS((lp, 512), BF16)] * 3 + [SDS((3, 512, d), F32), SDS((d, d), F32)],
        compiler_params=_params(("arbitrary",)),
    )(dh, proj, proj, proj, *outs, wb, wo)


def _shift_down(x, halo, n, first):
    rows = lax.broadcasted_iota(jnp.int32, x.shape, 0)
    edge = jnp.concatenate([pltpu.roll(halo, n, 0), jnp.zeros((x.shape[0] - 8, x.shape[1]), F32)], axis=0)
    edge = jnp.where(first, 0.0, edge)
    return jnp.where(rows < n, edge, pltpu.roll(x, n, 0))


def _shift_up(x, halo, n, last):
    tb = x.shape[0]
    rows = lax.broadcasted_iota(jnp.int32, x.shape, 0)
    edge = jnp.concatenate([jnp.zeros((tb - 8, x.shape[1]), F32), pltpu.roll(halo, 8 - n, 0)], axis=0)
    edge = jnp.where(last, 0.0, edge)
    return jnp.where(rows >= tb - n, edge, pltpu.roll(x, tb - n, 0))


def _conv(u, halo, w_ref, b_ref, first):
    taps = (_shift_down(u, halo, 2, first), _shift_down(u, halo, 1, first), u)
    c = b_ref[...] + w_ref[0:1, :] * taps[0] + w_ref[1:2, :] * taps[1] + w_ref[2:3, :] * taps[2]
    return c, taps


def _ffn_specs(tb, f):
    hb = tb // 8
    cur = lambda c: pl.BlockSpec((tb, f), lambda i, c=c: (i, c))
    prev = lambda c: pl.BlockSpec((8, f), lambda i, c=c: (jnp.maximum(i * hb - 1, 0), c))
    vec = lambda r, c: pl.BlockSpec((r, f), lambda i, c=c: (0, c))
    return cur, prev, vec


def _ffn_act_fwd(u, cw, cb, name):
    lp = u.shape[0]
    f = D_FF
    tb = TILE_ROW
    cur, prev, vec = _ffn_specs(tb, f)

    def body(ug, uv, hg, hv, wg, wv, bg, bv, o_ref):
        first = pl.program_id(0) == 0
        cg, _ = _conv(ug[...], hg[...], wg, bg, first)
        cv, _ = _conv(uv[...], hv[...], wv, bv, first)
        o_ref[...] = (cg * jax.nn.sigmoid(cg) * cv).astype(BF16)

    return pl.pallas_call(
        body, name=name, grid=(lp // tb,),
        in_specs=[cur(0), cur(1), prev(0), prev(1), vec(8, 0), vec(8, 1), vec(1, 0), vec(1, 1)],
        out_specs=pl.BlockSpec((tb, f), lambda i: (i, 0)), out_shape=SDS((lp, f), BF16),
        compiler_params=_params(("parallel",)),
    )(u, u, u, u, cw, cw, cb, cb)


def _ffn_act_bwd_conv(u, dact, cw, cb, name):
    lp = u.shape[0]
    f = D_FF
    tb = TILE_ROW
    cur, prev, vec = _ffn_specs(tb, f)

    def body(ug, uv, hg, hv, wg, wv, bg, bv, da_ref, dcg_ref, dcv_ref, dwg, dwv, dbg, dbv):
        first = pl.program_id(0) == 0

        @pl.when(first)
        def _():
            for r in (dwg, dwv, dbg, dbv):
                r[...] = jnp.zeros_like(r)

        cg, tg = _conv(ug[...], hg[...], wg, bg, first)
        cv, tv = _conv(uv[...], hv[...], wv, bv, first)
        da = da_ref[...]
        sg = jax.nn.sigmoid(cg)
        dcg = da * cv * sg * (1.0 + cg * (1.0 - sg))
        dcv = da * cg * sg
        dcg_ref[...] = dcg
        dcv_ref[...] = dcv
        for dc, taps, dw, db in ((dcg, tg, dwg, dbg), (dcv, tv, dwv, dbv)):
            for n in range(3):
                dw[n:n + 1, :] += jnp.sum(dc * taps[n], axis=0, keepdims=True)
            db[0:1, :] += jnp.sum(dc, axis=0, keepdims=True)

    row = pl.BlockSpec((tb, f), lambda i: (i, 0))
    acc = pl.BlockSpec((8, f), lambda i: (0, 0))
    return pl.pallas_call(
        body, name=name, grid=(lp // tb,),
        in_specs=[cur(0), cur(1), prev(0), prev(1), vec(8, 0), vec(8, 1), vec(1, 0), vec(1, 1), row],
        out_specs=[row, row, acc, acc, acc, acc],
        out_shape=[SDS((lp, f), F32)] * 2 + [SDS((8, f), F32)] * 4,
        compiler_params=_params(("arbitrary",)),
    )(u, u, u, u, cw, cw, cb, cb, dact)


def _ffn_act_bwd_in(dcg, dcv, cw, name):
    lp = dcg.shape[0]
    f = D_FF
    tb = TILE_ROW
    nb = lp // tb
    hb = tb // 8
    cur = pl.BlockSpec((tb, f), lambda i: (i, 0))
    nxt = pl.BlockSpec((8, f), lambda i: (jnp.minimum((i + 1) * hb, nb * hb - 1), 0))
    vec = lambda c: pl.BlockSpec((8, f), lambda i, c=c: (0, c))

    def body(dg, dv, ng, nv, wg, wv, og, ov):
        last = pl.program_id(0) == nb - 1
        for dc_ref, n_ref, w_ref, o_ref in ((dg, ng, wg, og), (dv, nv, wv, ov)):
            dc, halo = dc_ref[...], n_ref[...]
            du = (w_ref[2:3, :] * dc + w_ref[1:2, :] * _shift_up(dc, halo, 1, last)
                  + w_ref[0:1, :] * _shift_up(dc, halo, 2, last))
            o_ref[...] = du.astype(BF16)

    return pl.pallas_call(
        body, name=name, grid=(nb,),
        in_specs=[cur, cur, nxt, nxt, vec(0), vec(1)],
        out_specs=[cur, cur],
        out_shape=[SDS((lp, f), BF16)] * 2,
        compiler_params=_params(("parallel",)),
    )(dcg, dcv, dcg, dcv, cw, cw)


def _loss_head(y, target, n_real, name):
    lp, d = y.shape
    tb = TILE_MM

    def body(y_ref, t_ref, dy_ref, loss_ref):
        i = pl.program_id(0)

        @pl.when(i == 0)
        def _():
            loss_ref[...] = jnp.zeros_like(loss_ref)

        rows = i * tb + lax.broadcasted_iota(jnp.int32, (tb, 1), 0)
        real = (rows >= N_META) & (rows < N_META + n_real)
        diff = jnp.where(real, y_ref[...] - t_ref[...], 0.0)
        dy_ref[...] = diff * (1.0 / d)
        loss_ref[...] += (0.5 / d) * jnp.sum(diff * diff).reshape(1, 1)

    row = pl.BlockSpec((tb, d), lambda i: (i, 0))
    return pl.pallas_call(
        body, name=name, grid=(lp // tb,), in_specs=[row, row],
        out_specs=[row, pl.BlockSpec((1, 1), lambda i: (0, 0))],
        out_shape=[SDS((lp, d), F32), SDS((1, 1), F32)],
        compiler_params=_params(("arbitrary",)),
    )(y, target)


def _pad_lanes(v, width, at=0):
    return jnp.pad(v.astype(F32), (at, width - at - v.shape[0]))[None, :]


def _mix_params(w, big, l):
    b = lambda a: a.astype(BF16)
    win = big["w_in"]
    fq, fk, fv, ff, cq, ckv, kr, sq, sk, sv, gates = jnp.split(
        win, [512, 1024, 1536, 1544, 1800, 1928, 1960, 2472, 2600, 2728], axis=1)
    misc = jnp.concatenate([kr, ff, jnp.zeros((D_MODEL, 88), win.dtype)], axis=1)
    w_in = b(jnp.concatenate([gates, fq, fk, fv, sq, sk, sv, cq, ckv, misc], axis=1))
    wq = jnp.pad(big["mla_w_q_up"].reshape(256, HEADS, 96), ((0, 0), (0, 0), (0, 32))).reshape(256, 1024)
    wkv = big["mla_w_kv_up"].reshape(128, HEADS, 128)
    wkk = jnp.pad(wkv[:, :, :64], ((0, 0), (0, 0), (0, 64))).reshape(128, 1024)
    wkvv = wkv[:, :, 64:].reshape(128, 512)
    tile = lambda g, n: jnp.tile(g.astype(F32), n)[None, :]
    prm = [tile(w["fox_q_g"][l], 8), tile(w["fox_k_g"][l], 8), tile(w["swa_q_g"][l], 8), tile(w["swa_k_g"][l], 2),
           _pad_lanes(w["fox_forget_b"][l], 128, FF_LANE), w["mla_q_a_g"][l][None, :], w["mla_kv_a_g"][l][None, :],
           tile(jnp.pad(w["mla_q_g"][l], (0, 32)), 8), tile(jnp.pad(w["mla_k_g"][l], (0, 32)), 8),
           wq.astype(F32), wkk.astype(F32), wkvv.astype(F32)]
    return dict(g1=w["norm1_g"][l][None, :], w_in=w_in, prm=prm, sinks=w["swa_sinks"][l].astype(F32),
                wb=b(big["w_branch"]), wo=b(big["w_o"]))


def _ffn_params(w, big, l):
    cw = jnp.pad(w["ffn_conv_w"][l].astype(F32), ((0, 5), (0, 0)))
    return dict(g2=w["norm2_g"][l][None, :], w_up=big["ffn_w_up"].astype(BF16), cw=cw,
                cb=w["ffn_conv_b"][l][None, :].astype(F32), w_down=big["ffn_w_down"].astype(BF16))


def _cols(c):
    ct = c[:, FF_LANE:FF_LANE + HEADS].T
    return ct[:, :, None], ct[:, None, :]


def _rows(col):
    return jnp.swapaxes(col, 1, 2)


def _from_cols(col):
    return jnp.pad(col[:, :, 0].T, ((0, 0), (FF_LANE, 128 - FF_LANE - HEADS)))


def _layer_fwd_mix(h, lw, consts, cos, sin, slopes, l):
    tag = f"l{l}_"
    xn, proj = _norm_matmul(h, lw["g1"], lw["w_in"], IN_W // 2, tag + "in_proj")
    fq, fk, fv, mq, mk, mv, sq, skd, svd, ls = _prep_fwd(proj, lw["prm"], consts, cos, sin, tag + "prep")
    c = _cumsum([ls], False, tag + "decay_cumsum")
    c_col, c_row = _cols(c)
    oa, lse_a = _att_fwd(_Att("fox"), fq, fk, fv, (c_row,), tag + "fox_fwd")
    ob, lse_b = _att_fwd(_Att("mla"), mq, mk, mv, (), tag + "mla_fwd")
    oc, lse_c = _att_fwd(_Att("swa"), sq, skd, svd, (lw["sinks"], slopes), tag + "swa_fwd")
    h2 = _post_fwd(h, proj, (oa, ob, oc), lw["wb"], lw["wo"], tag + "merge")
    saved = dict(h=h, xn=xn, proj=proj, q=(fq, mq, sq), k=(fk, mk, skd), v=(fv, mv, svd), c=(c_col, c_row),
                 o=(oa, ob, oc), lse=(lse_a, lse_b, lse_c), h2=h2)
    return h2, saved


def _layer_fwd_ffn(h2, lw, l):
    tag = f"l{l}_"
    xn2, u = _norm_matmul(h2, lw["g2"], lw["w_up"], D_FF, tag + "ffn_up")
    act = _ffn_act_fwd(u, lw["cw"], lw["cb"], tag + "ffn_act")
    h3 = _matmul_residual(act, lw["w_down"], h2, tag + "ffn_down")
    return h3, dict(xn2=xn2, u=u, act=act)


def _layer_bwd_ffn(dh3, lw, sv, l):
    tag = f"l{l}_"
    f = D_FF
    dact = _matmul_nt(dh3, lw["w_down"], f, tag + "ffn_down_dx")
    dw_down = _matmul_tn(sv["act"], dh3, D_MODEL, tag + "ffn_down_dw")
    dcg, dcv, dwg, dwv, dbg, dbv = _ffn_act_bwd_conv(sv["u"], dact, lw["cw"], lw["cb"], tag + "ffn_act_dc")
    dug, duv = _ffn_act_bwd_in(dcg, dcv, lw["cw"], tag + "ffn_act_du")
    du = jnp.concatenate([dug, duv], axis=1)
    dw_up = _matmul_tn(sv["xn2"], du, f, tag + "ffn_up_dw")
    tb = TILE_MM
    half = lambda c: pl.BlockSpec((tb, f), lambda i, c=c: (i, c))
    whalf = lambda c: pl.BlockSpec((D_MODEL, f), lambda i, c=c: (0, c))
    dh2, dg2 = _norm_matmul_bwd(du, lw["w_up"], du, lw["w_up"], sv["h2"], lw["g2"], dh3,
                                (half(0), whalf(0), half(1), whalf(1)), tag + "ffn_up_dx")
    g = dict(norm2_g=dg2[0], ffn_w_up=dw_up, ffn_conv_w=jnp.concatenate([dwg[0:3], dwv[0:3]], axis=1),
             ffn_conv_b=jnp.concatenate([dbg[0], dbv[0]]), ffn_w_down=dw_down)
    return dh2, g


def _layer_bwd_mix(dh2, lw, sv, consts, folds, cos, sin, slopes, l, hook=None):
    tag = f"l{l}_"
    tb = TILE_MM
    dgates, doa, dob, doc, dwb, dwo = _post_bwd(dh2, sv["proj"], sv["o"], lw["wb"], lw["wo"], tag + "merge_bwd")
    c_col, c_row = sv["c"]
    extras = ((c_row,), (), (lw["sinks"], slopes))
    extras_kv = ((c_col,), (), (slopes,))
    grads = []
    for n, (mode, do) in enumerate((("fox", doa), ("mla", dob), ("swa", doc))):
        att = _Att(mode)
        q, k, v = sv["q"][n], sv["k"][n], sv["v"][n]
        res = _att_dq(att, q, k, v, sv["o"][n], do, sv["lse"][n], extras[n], tag + mode + "_dq")
        dq, delta = res[0], res[1]
        res_kv = _att_dkv(att, q, k, v, do, _rows(sv["lse"][n]), _rows(delta), extras_kv[n], tag + mode + "_dkv")
        grads.append((dq, res_kv[0], res_kv[1], res[2:], res_kv[2:]))
    (dfq, dfk, dfv, (dcq,), (dck,)), (dmq, dmk, dmv, _, _), (dsq, dskp, dsvp, (dsink,), _) = grads
    dls = _cumsum([_from_cols(dcq), _from_cols(dck)], True, tag + "decay_cumsum_bwd")
    res = _prep_bwd(sv["proj"], lw["prm"], consts, cos, sin,
                    (dfq, dfk, dfv, dmq, dmk, dmv, dsq, dskp, dsvp, dls), folds, tag + "prep_bwd")
    dother, pg = res[0], res[1:]
    dw_g = _matmul_tn(sv["xn"], dgates, GATES_W, tag + "in_proj_dw_gates")
    dw_o = _matmul_tn(sv["xn"], dother, OTHER_W, tag + "in_proj_dw_other")
    d_in = jnp.concatenate([
        dw_o[:, O_FQ:O_FV + 512], dw_o[:, O_MISC + FF_LANE:O_MISC + FF_LANE + 8], dw_o[:, O_CQ:O_CQ + 256],
        dw_o[:, O_CKV:O_CKV + 128], dw_o[:, O_MISC:O_MISC + 32], dw_o[:, O_SQ:O_SQ + 512], dw_o[:, O_SK:O_SK + 128],
        dw_o[:, O_SV:O_SV + 128], dw_g], axis=1)
    d_wq = pg[9].reshape(256, HEADS, 128)[:, :, :96].reshape(256, 768)
    d_wkv = jnp.concatenate([pg[10].reshape(128, HEADS, 128)[:, :, :64], pg[11].reshape(128, HEADS, 64)],
                            axis=2).reshape(128, 1024)
    g = dict(
        w_in=d_in, fox_forget_b=pg[4][0, FF_LANE:FF_LANE + 8], fox_q_g=pg[0][0, :64],
        fox_k_g=pg[1][0, :64], mla_q_a_g=pg[5][0], mla_w_q_up=d_wq, mla_kv_a_g=pg[6][0], mla_w_kv_up=d_wkv,
        mla_q_g=pg[7][0, :96], mla_k_g=pg[8][0, :96], swa_q_g=pg[2][0, :64], swa_k_g=pg[3][0, :64],
        swa_sinks=dsink[:, 0, 0:2 * PAIRS].reshape(HEADS), w_branch=dwb, w_o=dwo)
    tick = hook(g) if hook else None
    g1 = lw["g1"] if tick is None else lw["g1"] + tick
    full = lambda w: pl.BlockSpec((tb, w), lambda i: (i, 0))
    wfull = lambda w: pl.BlockSpec((D_MODEL, w), lambda i: (0, 0))
    dh, dg1 = _norm_matmul_bwd(dgates, lw["w_in"][:, :GATES_W], dother, lw["w_in"][:, GATES_W:], sv["h"], g1, dh2,
                               (full(GATES_W), wfull(GATES_W), full(OTHER_W), wfull(OTHER_W)), tag + "in_proj_dx")
    g["norm1_g"] = dg1[0]
    return dh, g


_MIX_BIG = ("w_in", "mla_w_q_up", "mla_w_kv_up", "w_branch", "w_o")
_FFN_BIG = ("ffn_w_up", "ffn_w_down")


def _local_step(x, target, w, hook=None, fetch=None):
    if fetch is None:
        fetch = lambda l, stage, after: {n: w[n][l] for n in (_MIX_BIG if stage == "mix" else _FFN_BIG)}
    seq = x.shape[0]
    length = N_META + seq
    lp = -(-length // ROW_ALIGN) * ROW_ALIGN
    pad = lp - length
    h = jnp.concatenate([w["meta_tokens"].astype(F32), x, jnp.zeros((pad, D_MODEL), F32)], axis=0)
    tgt = jnp.pad(target, ((N_META, pad), (0, 0)))
    consts = _consts()
    folds = (_fold_matrix(512, 64), _fold_matrix(1024, 128))
    cos, sin = _rope_tables(lp)
    slopes = jnp.asarray(2.0 ** (-8.0 * np.arange(1, HEADS + 1, dtype=np.float32) / HEADS), F32)
    lws, saved = [], []
    for l in range(DEPTH):
        lw = _mix_params(w, fetch(l, "mix", h), l)
        h, sv = _layer_fwd_mix(h, lw, consts, cos, sin, slopes, l)
        lw.update(_ffn_params(w, fetch(l, "ffn", h), l))
        h, sv_ffn = _layer_fwd_ffn(h, lw, l)
        lws.append(lw)
        saved.append({**sv, **sv_ffn})
    dh, loss = _loss_head(h, tgt, seq, "loss_head")
    grads = [None] * DEPTH
    for l in reversed(range(DEPTH)):
        dh, g_ffn = _layer_bwd_ffn(dh, lws[l], saved[l], l)
        tick = hook(l, "ffn", g_ffn) if hook else None
        if tick is not None:
            lws[l]["sinks"] = lws[l]["sinks"] + tick
        mix_hook = (lambda g, l=l, g_ffn=g_ffn: hook(l, "mix", {**g_ffn, **g})) if hook else None
        dh, g_mix = _layer_bwd_mix(dh, lws[l], saved[l], consts, folds, cos, sin, slopes, l, mix_hook)
        grads[l] = {**g_ffn, **g_mix}
    return loss, dh[N_META:length], dh[:N_META], grads


def _place():
    return lax.axis_index("x"), lax.axis_index("y"), lax.axis_index("c")


def _flip(pos, k):
    x, y, c = pos
    return (1 - x if k & 4 else x, 1 - y if k & 2 else y, 1 - c if k & 1 else c)


def _index(pos):
    return 4 * pos[0] + 2 * pos[1] + pos[2]


def _gather(tensors, name):
    n_t = len(tensors)

    def body(*refs):
        ins, outs = refs[:n_t], refs[n_t:2 * n_t]
        send_sems, recv_sems, local_sems = refs[2 * n_t:]
        x, y, c = _place()
        me, sibling = (x, y, c), (x, y, 1 - c)
        chips = [(1 - x, y), (x, 1 - y), (1 - x, 1 - y)]

        def copy(t, k, block, to, src=None):
            dst = outs[t].at[_index(block)]
            return pltpu.make_async_remote_copy(
                src_ref=dst if src is None else src, dst_ref=dst, send_sem=send_sems.at[t, k],
                recv_sem=recv_sems.at[t, k], device_id=to, device_id_type=pl.DeviceIdType.MESH)

        local, sent = [], []
        for t in range(n_t):
            local.append(pltpu.make_async_copy(ins[t], outs[t].at[_index(me)], local_sems.at[t]))
            local[-1].start()
            sent.append(copy(t, 0, me, sibling, src=ins[t]))
            sent += [copy(t, 1 + j, me, (*chip, c), src=ins[t]) for j, chip in enumerate(chips)]
        for cp in sent:
            cp.start()
        for j, chip in enumerate(chips):
            for t in range(n_t):
                copy(t, 1 + j, (*chip, c), me).wait_recv()
                sent.append(copy(t, 4 + j, (*chip, c), sibling))
                sent[-1].start()
        for t in range(n_t):
            copy(t, 0, sibling, me).wait_recv()
            for j, chip in enumerate(chips):
                copy(t, 4 + j, (*chip, 1 - c), me).wait_recv()
        for cp in sent:
            cp.wait_send()
        for cp in local:
            cp.wait()

    any_spec = pl.BlockSpec(memory_space=pl.ANY)
    return pl.pallas_call(
        body, name=name, in_specs=[any_spec] * n_t, out_specs=[any_spec] * n_t,
        out_shape=[SDS((N_DEV,) + a.shape, a.dtype) for a in tensors],
        scratch_shapes=[pltpu.SemaphoreType.DMA((n_t, N_DEV - 1)), pltpu.SemaphoreType.DMA((n_t, N_DEV - 1)),
                        pltpu.SemaphoreType.DMA((n_t,))],
    )(*tensors)


def _exchange_start(tensors, name, gather=False, after=None):
    n_t = len(tensors)

    def body(*refs):
        ins, lands = refs[:n_t], refs[n_t:2 * n_t]
        send_sem, recv_sem = refs[2 * n_t + 1:2 * n_t + 3]
        token = refs[-1]
        me = _place()
        mine = _index(me)
        for t in range(n_t):
            for k in range(1, N_DEV):
                peer = _flip(me, k)
                pltpu.make_async_remote_copy(
                    src_ref=ins[t] if gather else ins[t].at[_index(peer)], dst_ref=lands[t].at[mine],
                    send_sem=send_sem, recv_sem=recv_sem, device_id=peer, device_id_type=pl.DeviceIdType.MESH).start()
        token[...] = jnp.zeros_like(token)

    hbm = pl.BlockSpec(memory_space=pltpu.HBM)
    sem = pl.BlockSpec(memory_space=pltpu.SEMAPHORE)
    one = pltpu.SemaphoreType.DMA(())
    land_shape = lambda a: ((N_DEV,) + a.shape) if gather else a.shape
    bufs = ([pltpu.HBM(a.shape, a.dtype) for a in tensors] + [pltpu.HBM(land_shape(a), a.dtype) for a in tensors])
    after = jnp.zeros((8, 128), F32) if after is None else after
    outs = pl.pallas_call(
        body, name=name, in_specs=[hbm] * (2 * n_t) + [pl.BlockSpec(memory_space=pl.ANY)],
        out_specs=[sem, sem] + [hbm] * (2 * n_t) + [pl.BlockSpec(memory_space=pltpu.VMEM)],
        out_shape=[one, one] + bufs + [SDS((8, 128), F32)],
        input_output_aliases={i: 2 + i for i in range(2 * n_t)},
        compiler_params=pltpu.CompilerParams(has_side_effects=pltpu.SideEffectType.DATAFLOW_SIDE_EFFECTING),
    )(*[pltpu.with_memory_space_constraint(a, pltpu.HBM) for a in tensors],
      *[pltpu.with_memory_space_constraint(lax.empty(land_shape(a), a.dtype), pltpu.HBM) for a in tensors], after)
    return outs[:-1], outs[-1][0, 0]


def _exchange_wait(state, after, name, gather=False):
    n_t = (len(state) - 2) // 2

    def body(*refs):
        send_sem, recv_sem = refs[0:2]
        ins, lands = refs[2:2 + n_t], refs[2 + n_t:2 + 2 * n_t]
        me = _place()
        for t in range(n_t):
            for k in range(1, N_DEV):
                peer = _flip(me, k)
                copy = pltpu.make_async_remote_copy(
                    src_ref=ins[t] if gather else ins[t].at[_index(peer)], dst_ref=lands[t].at[_index(peer)],
                    send_sem=send_sem, recv_sem=recv_sem, device_id=peer, device_id_type=pl.DeviceIdType.MESH)
                copy.wait_send()
                copy.wait_recv()

    hbm = pl.BlockSpec(memory_space=pltpu.HBM)
    sem = pl.BlockSpec(memory_space=pltpu.SEMAPHORE)
    bufs = [pltpu.HBM(a.shape, a.dtype) for a in state[2:]]
    outs = pl.pallas_call(
        body, name=name, in_specs=[sem, sem] + [hbm] * (2 * n_t) + [pl.BlockSpec(memory_space=pl.ANY)],
        out_specs=[hbm] * (2 * n_t), out_shape=bufs,
        input_output_aliases={2 + i: i for i in range(2 * n_t)},
        compiler_params=pltpu.CompilerParams(has_side_effects=pltpu.SideEffectType.DATAFLOW_SIDE_EFFECTING),
    )(*state, after)
    return outs[n_t:]


def _sum_slots(parts, name):
    n, rows, w = parts.shape
    tb = 8

    def body(p_ref, o_ref):
        acc = p_ref[0].astype(F32)
        for s in range(1, n):
            acc = acc + p_ref[s].astype(F32)
        o_ref[...] = acc

    return pl.pallas_call(
        body, name=name, grid=(rows // tb,),
        in_specs=[pl.BlockSpec((n, tb, w), lambda i: (0, i, 0))], out_specs=pl.BlockSpec((tb, w), lambda i: (i, 0)),
        out_shape=SDS((rows, w), F32), compiler_params=_params(("parallel",)),
    )(parts)


def _adamw(wt, m, v, parts, name, own=None):
    shape = wt.shape
    parts = parts if isinstance(parts, (list, tuple)) else [parts]
    n, w = parts[0].shape[0], shape[-1]
    rows = math.prod(shape[:-1])
    per = rows // len(parts)
    step = 16 if parts[0].dtype == BF16 else 8
    tb = max([t for t in range(step, 257, step) if per % t == 0] or [per])
    nb = per // tb
    c1 = 1.0 / (1.0 - ADAM_B1 ** ADAM_STEP)
    c2 = 1.0 / (1.0 - ADAM_B2 ** ADAM_STEP)
    state = [a.reshape(rows, w) for a in (wt, m, v)]
    n_in = 4 if own is None else 5
    outs = None
    for l in reversed(range(len(parts))):
        def body(*refs):
            idx_ref = None if own is None else refs[0]
            w_ref, m_ref, v_ref, p_ref = refs[n_in - 4:n_in] if own is None else refs[1:5]
            g_out, d_out, m_out, v_out = refs[-4:]
            g = None
            for s in range(n):
                term = p_ref[s] if own is None else jnp.where(idx_ref[0] == s, refs[5][0], p_ref[s])
                g = term.astype(F32) if g is None else g + term.astype(F32)
            m_new = ADAM_B1 * m_ref[...] + (1.0 - ADAM_B1) * g
            v_new = ADAM_B2 * v_ref[...] + (1.0 - ADAM_B2) * (g * g)
            g_out[...] = g
            m_out[...] = m_new
            v_out[...] = v_new
            d_out[...] = -ADAM_LR * ((m_new * c1) / (jnp.sqrt(v_new * c2) + ADAM_EPS) + ADAM_WD * w_ref[...])

        row = pl.BlockSpec((tb, w), lambda i, *_, l=l: (l * nb + i, 0))
        in_specs = [row, row, row, pl.BlockSpec((n, tb, w), lambda i, *_: (0, i, 0))]
        args = [*state, parts[l].reshape(n, per, w)]
        if own is not None:
            in_specs.append(pl.BlockSpec((1, tb, w), lambda i, idx: (idx[0], i, 0)))
            args.append(own[l].reshape(n, per, w))
        prev = [] if outs is None else list(outs)
        in_specs += [pl.BlockSpec(memory_space=pl.ANY)] * len(prev)
        n_pre = 0 if own is None else 1
        call = dict(name=f"{name}_{l}", out_shape=[SDS((rows, w), F32)] * 4,
                    input_output_aliases={n_pre + len(args) + k: k for k in range(len(prev))},
                    compiler_params=_params(("parallel",)))
        if own is None:
            outs = pl.pallas_call(body, grid=(nb,), in_specs=in_specs, out_specs=[row] * 4, **call)(*args, *prev)
        else:
            spec = pltpu.PrefetchScalarGridSpec(num_scalar_prefetch=1, grid=(nb,), in_specs=in_specs, out_specs=[row] * 4)
            idx = jnp.reshape(_index(_place()), (1,)).astype(jnp.int32)
            outs = pl.pallas_call(body, grid_spec=spec, **call)(idx, *args, *prev)
    return [o.reshape(shape) for o in outs]


_BIG = [("w_in", 2), ("mla_w_q_up", 2), ("mla_w_kv_up", 2), ("w_branch", 3), ("w_o", 1), ("ffn_w_up", 2), ("ffn_w_down", 1)]
_SMALL_SHARDED = [("meta_tokens", 1), ("ffn_conv_w", 2)]
_REPLICATED = ["norm1_g", "fox_forget_b", "fox_q_g", "fox_k_g", "mla_q_a_g", "mla_kv_a_g", "mla_q_g", "mla_k_g",
               "swa_q_g", "swa_k_g", "swa_sinks", "norm2_g", "ffn_conv_b"]
_ORDER = ["meta_tokens", "norm1_g", "w_in", "fox_forget_b", "fox_q_g", "fox_k_g", "mla_q_a_g", "mla_w_q_up",
          "mla_kv_a_g", "mla_w_kv_up", "mla_q_g", "mla_k_g", "swa_q_g", "swa_k_g", "swa_sinks", "w_branch", "w_o",
          "norm2_g", "ffn_w_up", "ffn_conv_w", "ffn_conv_b", "ffn_w_down"]


def _flat_rows(vecs, dtype, row_mult):
    flat = jnp.concatenate([a.reshape(-1).astype(dtype) for a in vecs])
    rows = -(-flat.shape[0] // (1024 * row_mult)) * row_mult
    return jnp.pad(flat, (0, rows * 1024 - flat.shape[0])).reshape(rows, 1024)


def _unflatten(flat, shapes):
    out, off = [], 0
    for s in shapes:
        n = math.prod(s)
        out.append(flat[off:off + n].reshape(s))
        off += n
    return out


def _to_full(blocks, axis):
    moved = jnp.moveaxis(blocks, 0, axis)
    s = moved.shape
    return moved.reshape(s[:axis] + (s[axis] * s[axis + 1],) + s[axis + 2:])


def _to_blocks(full, axis):
    s = full.shape
    split = full.reshape(s[:axis] + (N_DEV, s[axis] // N_DEV) + s[axis + 1:])
    return jnp.moveaxis(split, axis, 0)


def kernel(x, meta_tokens, norm1_g, w_in, fox_forget_b, fox_q_g, fox_k_g, mla_q_a_g, mla_w_q_up, mla_kv_a_g, mla_w_kv_up, mla_q_g, mla_k_g, swa_q_g, swa_k_g, swa_sinks, w_branch, w_o, norm2_g, ffn_w_up, ffn_conv_w, ffn_conv_b, ffn_w_down, loss_target, m_meta_tokens, m_norm1_g, m_w_in, m_fox_forget_b, m_fox_q_g, m_fox_k_g, m_mla_q_a_g, m_mla_w_q_up, m_mla_kv_a_g, m_mla_w_kv_up, m_mla_q_g, m_mla_k_g, m_swa_q_g, m_swa_k_g, m_swa_sinks, m_w_branch, m_w_o, m_norm2_g, m_ffn_w_up, m_ffn_conv_w, m_ffn_conv_b, m_ffn_w_down, v_meta_tokens, v_norm1_g, v_w_in, v_fox_forget_b, v_fox_q_g, v_fox_k_g, v_mla_q_a_g, v_mla_w_q_up, v_mla_kv_a_g, v_mla_w_kv_up, v_mla_q_g, v_mla_k_g, v_swa_q_g, v_swa_k_g, v_swa_sinks, v_w_branch, v_w_o, v_norm2_g, v_ffn_w_up, v_ffn_conv_w, v_ffn_conv_b, v_ffn_w_down):
    wl = dict(zip(_ORDER, (meta_tokens, norm1_g, w_in, fox_forget_b, fox_q_g, fox_k_g, mla_q_a_g, mla_w_q_up,
                           mla_kv_a_g, mla_w_kv_up, mla_q_g, mla_k_g, swa_q_g, swa_k_g, swa_sinks, w_branch, w_o,
                           norm2_g, ffn_w_up, ffn_conv_w, ffn_conv_b, ffn_w_down)))
    ml = dict(zip(_ORDER, (m_meta_tokens, m_norm1_g, m_w_in, m_fox_forget_b, m_fox_q_g, m_fox_k_g, m_mla_q_a_g,
                           m_mla_w_q_up, m_mla_kv_a_g, m_mla_w_kv_up, m_mla_q_g, m_mla_k_g, m_swa_q_g, m_swa_k_g,
                           m_swa_sinks, m_w_branch, m_w_o, m_norm2_g, m_ffn_w_up, m_ffn_conv_w, m_ffn_conv_b,
                           m_ffn_w_down)))
    vl = dict(zip(_ORDER, (v_meta_tokens, v_norm1_g, v_w_in, v_fox_forget_b, v_fox_q_g, v_fox_k_g, v_mla_q_a_g,
                           v_mla_w_q_up, v_mla_kv_a_g, v_mla_w_kv_up, v_mla_q_g, v_mla_k_g, v_swa_q_g, v_swa_k_g,
                           v_swa_sinks, v_w_branch, v_w_o, v_norm2_g, v_ffn_w_up, v_ffn_conv_w, v_ffn_conv_b,
                           v_ffn_w_down)))
    small_sh = [n for n, _ in _SMALL_SHARDED]
    big = [n for n, _ in _BIG]
    axis_of = dict(_BIG)
    idx = _index(_place())

    def to_full(n, blocks, own=None):
        if own is not None:
            sel = (jnp.arange(N_DEV) == idx).reshape((N_DEV,) + (1,) * own.ndim)
            blocks = jnp.where(sel, own[None], blocks)
        return _to_full(blocks, axis_of[n] - 1)

    local = {(n, l): wl[n][l].astype(BF16) for n in big for l in range(DEPTH)}
    got = _gather([local[(n, 0)] for n in _MIX_BIG] + [wl[n] for n in small_sh], "gather_weights_l0_mix")
    full = {n: wl[n] for n in _REPLICATED}
    for (n, axis), blocks in zip(_SMALL_SHARDED, got[len(_MIX_BIG):]):
        full[n] = _to_full(blocks, axis)
    ready = {(n, 0): to_full(n, blocks) for n, blocks in zip(_MIX_BIG, got)}
    later = {"l0_ffn": [(n, 0) for n in _FFN_BIG], "l1": [(n, 1) for n in big]}
    states = {}
    for key, names in later.items():
        states[key], tick = _exchange_start([local[e] for e in names], "gather_weights_" + key + "_start", True, got[0])
        full["norm1_g"] = full["norm1_g"] + tick

    def fetch(l, stage, after):
        key = "l0_ffn" if l == 0 else "l1"
        if (l, stage) != (0, "mix") and key in states:
            lands = _exchange_wait(states.pop(key), after, "gather_weights_" + key + "_wait", True)
            ready.update({e: to_full(e[0], blocks, local[e]) for e, blocks in zip(later[key], lands)})
        return {n: ready[(n, l)] for n in (_MIX_BIG if stage == "mix" else _FFN_BIG)}

    blocks_of = lambda g, names: [_to_blocks(g[n], axis_of[n] - 1).astype(BF16) for n in names]
    early = {}

    def hook(l, stage, g):
        if l == DEPTH - 1 and stage == "mix":
            key, names = "l1", big
        elif l == 0:
            key, names = "l0_" + stage, (_FFN_BIG if stage == "ffn" else _MIX_BIG)
        else:
            return None
        sends = blocks_of(g, names)
        state, tick = _exchange_start(sends, "exchange_grads_" + key + "_start")
        early[key] = (names, l, sends, state)
        return tick

    loss, grad_x, grad_meta, grads = _local_step(x[0], loss_target[0], full, hook, fetch)
    result = {kind: {} for kind in ("grad", "delta", "new_m", "new_v")}
    landed, sent = {}, {}
    after = early["l0_mix"][2][0]
    for key in ("l1", "l0_ffn"):
        names, l, sends, state = early[key]
        got = _exchange_wait(state, after, "exchange_grads_" + key + "_wait")
        landed.update({(n, l): p for n, p in zip(names, got)})
        sent.update({(n, l): p for n, p in zip(names, sends)})

    def update(names):
        for n in names:
            outs = _adamw(wl[n], ml[n], vl[n], [landed[(n, l)] for l in range(DEPTH)], "adamw_" + n,
                          [sent[(n, l)] for l in range(DEPTH)])
            for kind, val in zip(result, outs):
                result[kind][n] = val

    update(_FFN_BIG)
    grads = {k: jnp.stack([grads[l][k] for l in range(DEPTH)]) for k in grads[0] if k not in big}
    grads["meta_tokens"] = grad_meta

    small_full = _REPLICATED + small_sh
    mine_small = _flat_rows([grads[n] for n in small_full] + [loss], F32, 8)
    total_small = _sum_slots(_gather([mine_small], "gather_small_grads")[0], "sum_small_grads").reshape(-1)
    pieces = _unflatten(total_small, [grads[n].shape for n in small_full] + [()])
    loss_total = pieces[-1]
    g_small = dict(zip(small_full, pieces[:-1]))
    for n, axis in _SMALL_SHARDED:
        size = wl[n].shape[axis]
        g_small[n] = lax.dynamic_slice_in_dim(g_small[n], idx * size, size, axis)
    flat = lambda d: _flat_rows([d[n] for n in small_full], F32, 8)
    small_out = _adamw(flat(wl), flat(ml), flat(vl), flat(g_small)[None], "adamw_small")
    for kind, fs in zip(result, small_out):
        result[kind].update(zip(small_full, _unflatten(fs.reshape(-1), [wl[n].shape for n in small_full])))
    names, l, sends, state = early["l0_mix"]
    got = _exchange_wait(state, small_out[0], "exchange_grads_l0_mix_wait")
    landed.update({(n, l): p for n, p in zip(names, got)})
    sent.update({(n, l): p for n, p in zip(names, sends)})
    update(_MIX_BIG)
    outs = [loss_total, grad_x[None]]
    for kind in ("grad", "delta", "new_m", "new_v"):
        outs += [result[kind][n] for n in _ORDER]
    return tuple(outs)
```

```python
import functools
import math

import numpy as np
import jax
import jax.numpy as jnp
from jax import lax
from jax.experimental import pallas as pl
from jax.experimental.pallas import tpu as pltpu

F32, BF16 = jnp.float32, jnp.bfloat16
SDS = jax.ShapeDtypeStruct

D_MODEL = 1024
N_META = 16
EPS = 1e-6
WINDOW = 128
ROPE_THETA = 10000.0
HEADS = 8
D_FF = 2816
DEPTH = 2
N_DEV = 8
ADAM_LR, ADAM_B1, ADAM_B2, ADAM_EPS, ADAM_WD, ADAM_STEP = 0.001, 0.9, 0.999, 1e-08, 0.01, 10

ROW_ALIGN = 384
TILE_MM = 384
TILE_ROW = 128
TILE_ATT = 384
TILE_POST = 128
PAIRS = 2
VMEM_LIMIT = 56 * 1024 * 1024

GATES_W = 3072
OTHER_W = 2816
IN_W = GATES_W + OTHER_W
O_FQ, O_FK, O_FV, O_SQ, O_SK, O_SV, O_CQ, O_CKV, O_MISC = 0, 512, 1024, 1536, 2048, 2176, 2304, 2560, 2688
FF_LANE = 32

NEG = -1e30


def _dot(a, b):
    return jnp.dot(a, b, preferred_element_type=F32)


def _dot_nt(a, b):
    return lax.dot_general(a, b, (((1,), (1,)), ((), ())), preferred_element_type=F32)


def _dot_tn(a, b):
    return lax.dot_general(a, b, (((0,), (0,)), ((), ())), preferred_element_type=F32)


def _params(sem):
    return pltpu.CompilerParams(dimension_semantics=sem, vmem_limit_bytes=VMEM_LIMIT)


def _rms(x, g):
    return x * lax.rsqrt(jnp.mean(x * x, axis=-1, keepdims=True) + EPS) * g


def _split_dot(x, m, pieces=2):
    acc, rest = None, x
    for _ in range(pieces):
        part = rest.astype(BF16)
        rest = rest - part.astype(F32)
        acc = _dot(part, m) if acc is None else acc + _dot(part, m)
    return acc


@jax.custom_vjp
def _sel(x, m, mt):
    return _split_dot(x, m)


_sel.defvjp(lambda x, m, mt: (_split_dot(x, m), (m, mt)), lambda res, dy: (_split_dot(dy, res[1]), None, None))


@jax.custom_vjp
def _mm(x, w):
    return _dot(x.astype(BF16), w.astype(BF16))


def _mm_bwd(res, dy):
    x, w = res
    dyb = dy.astype(BF16)
    return _dot_nt(dyb, w.astype(BF16)), _dot_tn(x.astype(BF16), dyb)


_mm.defvjp(lambda x, w: (_mm(x, w), (x, w)), _mm_bwd)


def _rot_impl(x):
    w = x.shape[1]
    lane = lax.broadcasted_iota(jnp.int32, x.shape, 1) % 128
    lo = (lane >= 64) & (lane < 80)
    hi = (lane >= 80) & (lane < 96)
    return jnp.where(hi, pltpu.roll(x, 16, 1), 0.0) - jnp.where(lo, pltpu.roll(x, w - 16, 1), 0.0)


@jax.custom_vjp
def _rot(x):
    return _rot_impl(x)


_rot.defvjp(lambda x: (_rot_impl(x), None), lambda _, dy: (-_rot_impl(dy),))


def _gnorm(x, g, e, et, dim):
    inv = lax.rsqrt(_sel(x * x, e, et) * (1.0 / dim) + EPS)
    return x * _sel(inv, et, e) * g


def _indicator(width, period):
    m = np.zeros((width, 128), np.float32)
    m[np.arange(width), np.arange(width) // period] = 1.0
    return m


def _consts():
    e64 = _indicator(512, 64)
    e128 = _indicator(1024, 128)
    sk = np.zeros((128, 1024), np.float32)
    for h in range(HEADS):
        sk[np.arange(32), 128 * h + 64 + np.arange(32)] = 1.0
    dup = np.zeros((128, 256), np.float32)
    for g in range(2):
        for r in range(2):
            dup[64 * g + np.arange(64), 128 * g + 64 * r + np.arange(64)] = 1.0
    mats = [e64, e64.T, e128, e128.T, sk, sk.T, dup, dup.T]
    return [jnp.asarray(m, BF16) for m in mats]


def _fold_matrix(width, period):
    m = np.zeros((width, 128), np.float32)
    m[np.arange(width), np.arange(width) % period] = 1.0
    return jnp.asarray(m, BF16)


def _rope_tables(lp):
    half = 16
    freqs = ROPE_THETA ** (-np.arange(half, dtype=np.float32) / half)
    ang = np.arange(lp, dtype=np.float32)[:, None] * freqs[None, :]
    cos = np.ones((lp, 128), np.float32)
    sin = np.zeros((lp, 128), np.float32)
    cos[:, 64:80] = np.cos(ang)
    cos[:, 80:96] = np.cos(ang)
    sin[:, 64:80] = np.sin(ang)
    sin[:, 80:96] = np.sin(ang)
    return jnp.asarray(cos), jnp.asarray(sin)


def _norm_matmul(h, g, w, tn, name):
    lp, d = h.shape
    n = w.shape[1]
    tb = TILE_MM

    def body(h_ref, g_ref, w_ref, xn_ref, y_ref):
        @pl.when(pl.program_id(1) == 0)
        def _():
            xn_ref[...] = _rms(h_ref[...], g_ref[...]).astype(BF16)

        y_ref[...] = _dot(xn_ref[...], w_ref[...])

    return pl.pallas_call(
        body, name=name, grid=(lp // tb, n // tn),
        in_specs=[pl.BlockSpec((tb, d), lambda i, j: (i, 0)), pl.BlockSpec((1, d), lambda i, j: (0, 0)),
                  pl.BlockSpec((d, tn), lambda i, j: (0, j))],
        out_specs=[pl.BlockSpec((tb, d), lambda i, j: (i, 0)), pl.BlockSpec((tb, tn), lambda i, j: (i, j))],
        out_shape=[SDS((lp, d), BF16), SDS((lp, n), F32)],
        compiler_params=_params(("parallel", "arbitrary")),
    )(h, g, w)


def _matmul_residual(a, w, res, name):
    m, k = a.shape
    n = w.shape[1]
    tb = TILE_MM

    def body(a_ref, w_ref, r_ref, o_ref):
        o_ref[...] = r_ref[...] + _dot(a_ref[...], w_ref[...])

    return pl.pallas_call(
        body, name=name, grid=(m // tb,),
        in_specs=[pl.BlockSpec((tb, k), lambda i: (i, 0)), pl.BlockSpec((k, n), lambda i: (0, 0)),
                  pl.BlockSpec((tb, n), lambda i: (i, 0))],
        out_specs=pl.BlockSpec((tb, n), lambda i: (i, 0)),
        out_shape=SDS((m, n), F32),
        compiler_params=_params(("parallel",)),
    )(a, w, res)


def _matmul_nt(dy, w, tn, name):
    m, k = dy.shape
    n = w.shape[0]
    tb = TILE_MM

    def body(dy_ref, w_ref, o_ref):
        o_ref[...] = _dot_nt(dy_ref[...].astype(BF16), w_ref[...])

    return pl.pallas_call(
        body, name=name, grid=(m // tb, n // tn),
        in_specs=[pl.BlockSpec((tb, k), lambda i, j: (i, 0)), pl.BlockSpec((tn, k), lambda i, j: (j, 0))],
        out_specs=pl.BlockSpec((tb, tn), lambda i, j: (i, j)),
        out_shape=SDS((m, n), F32),
        compiler_params=_params(("parallel", "arbitrary")),
    )(dy, w)


def _matmul_tn(x, dy, tn, name):
    m, k = x.shape
    n = dy.shape[1]
    tb = TILE_MM
    nb = m // tb

    def body(x_ref, dy_ref, o_ref, acc):
        i = pl.program_id(1)

        @pl.when(i == 0)
        def _():
            acc[...] = jnp.zeros_like(acc)

        acc[...] += _dot_tn(x_ref[...].astype(BF16), dy_ref[...].astype(BF16))

        @pl.when(i == nb - 1)
        def _():
            o_ref[...] = acc[...].astype(BF16)

    return pl.pallas_call(
        body, name=name, grid=(n // tn, nb),
        in_specs=[pl.BlockSpec((tb, k), lambda j, i: (i, 0)), pl.BlockSpec((tb, tn), lambda j, i: (i, j))],
        out_specs=pl.BlockSpec((k, tn), lambda j, i: (0, j)),
        out_shape=SDS((k, n), BF16),
        scratch_shapes=[pltpu.VMEM((k, tn), F32)],
        compiler_params=_params(("parallel", "arbitrary")),
    )(x, dy)


def _norm_matmul_bwd(dys, w, x, g, dres, name):
    m, d = x.shape
    tb = TILE_MM
    widths = [a.shape[1] for a in dys]
    n_dy = len(dys)

    def body(*refs):
        w_ref, x_ref, g_ref, r_ref, o_ref, dg_ref = refs[n_dy:]

        @pl.when(pl.program_id(0) == 0)
        def _():
            dg_ref[...] = jnp.zeros_like(dg_ref)

        dxn, off = None, 0
        for dy_ref, width in zip(refs[:n_dy], widths):
            part = _dot_nt(dy_ref[...], w_ref[:, off:off + width])
            dxn = part if dxn is None else dxn + part
            off += width
        _, vjp = jax.vjp(_rms, x_ref[...], g_ref[...])
        dx, dg = vjp(dxn)
        o_ref[...] = r_ref[...] + dx
        dg_ref[...] += dg

    row = pl.BlockSpec((tb, d), lambda i: (i, 0))
    vec = pl.BlockSpec((1, d), lambda i: (0, 0))
    return pl.pallas_call(
        body, name=name, grid=(m // tb,),
        in_specs=[pl.BlockSpec((tb, wd), lambda i: (i, 0)) for wd in widths]
        + [pl.BlockSpec(w.shape, lambda i: (0, 0)), row, vec, row],
        out_specs=[row, vec],
        out_shape=[SDS((m, d), F32), SDS((1, d), F32)],
        compiler_params=_params(("arbitrary",)),
    )(*dys, w, x, g, dres)


def _prep_math(pieces, prm, consts, cos, sin):
    fq, fk, sq, sk, sv, cq, ckv, misc = pieces
    gfq, gfk, gsq, gsk, fb, gqa, gkva, gmq, gmk, wq, wkk, wkv = prm
    e64, e64t, e128, e128t, skm, skt, dup, dupt = consts
    cos8 = jnp.concatenate([cos] * HEADS, axis=1)
    sin8 = jnp.concatenate([sin] * HEADS, axis=1)
    fq_n = _gnorm(fq, gfq, e64, e64t, 64)
    fk_n = _gnorm(fk, gfk, e64, e64t, 64)
    ls = jax.nn.log_sigmoid(misc + fb)
    q = _gnorm(_mm(_rms(cq, gqa), wq), gmq, e128, e128t, 96)
    mq = q * cos8 + _rot(q) * sin8
    kva = _rms(ckv, gkva)
    k = _gnorm(_mm(kva, wkk) + _sel(misc, skm, skt), gmk, e128, e128t, 96)
    mk = k * cos8 + _rot(k) * sin8
    mv = _mm(kva, wkv)
    sq_n = _gnorm(sq, gsq, e64, e64t, 64)
    sk_n = _gnorm(sk, gsk, e64[0:128], e64t[:, 0:128], 64)
    skd = _sel(sk_n, dup, dupt)
    svd = _sel(sv, dup, dupt)
    return fq_n, fk_n, ls, mq, mk, mv, sq_n, skd, svd


_PIECES = [(O_FQ, 512), (O_FK, 512), (O_SQ, 512), (O_SK, 128), (O_SV, 128), (O_CQ, 256), (O_CKV, 128), (O_MISC, 128)]
_PRM_SHAPES = [(1, 512), (1, 512), (1, 512), (1, 128), (1, 128), (1, 256), (1, 128), (1, 1024), (1, 1024),
               (256, 1024), (128, 1024), (128, 512)]
_CONST_SHAPES = [(512, 128), (128, 512), (1024, 128), (128, 1024), (128, 1024), (1024, 128), (128, 256), (256, 128)]


def _piece_specs(tb):
    def spec(off, width):
        blk = (GATES_W + off) // width
        return pl.BlockSpec((tb, width), lambda i, blk=blk: (i, blk))
    return [spec(o, w) for o, w in _PIECES] + [spec(O_FV, 512)]


def _full_specs(shapes):
    return [pl.BlockSpec(s, lambda i: (0, 0)) for s in shapes]


def _prep_fwd(proj, prm, consts, cos, sin, name):
    lp = proj.shape[0]
    tb = TILE_ROW
    row = lambda w: pl.BlockSpec((tb, w), lambda i: (i, 0))

    def body(*refs):
        pieces = [r[...] for r in refs[0:8]]
        fv = refs[8][...]
        prm_v = [r[...] for r in refs[9:21]]
        consts_v = [r[...] for r in refs[21:29]]
        cos_v, sin_v = refs[29][...], refs[30][...]
        outs = refs[31:]
        fq_n, fk_n, ls, mq, mk, mv, sq_n, skd, svd = _prep_math(pieces, prm_v, consts_v, cos_v, sin_v)
        for ref, val in zip(outs, (fq_n, fk_n, fv, mq, mk, mv, sq_n, skd, svd)):
            ref[...] = val.astype(BF16)
        outs[9][...] = ls

    widths = [512, 512, 512, 1024, 1024, 512, 512, 256, 256]
    return pl.pallas_call(
        body, name=name, grid=(lp // tb,),
        in_specs=_piece_specs(tb) + _full_specs(_PRM_SHAPES) + _full_specs(_CONST_SHAPES) + [row(128), row(128)],
        out_specs=[row(w) for w in widths] + [row(128)],
        out_shape=[SDS((lp, w), BF16) for w in widths] + [SDS((lp, 128), F32)],
        compiler_params=_params(("parallel",)),
    )(*([proj] * 9), *prm, *consts, cos, sin)


def _prep_bwd(proj, prm, consts, cos, sin, cots, folds, name):
    lp = proj.shape[0]
    tb = TILE_ROW
    row = lambda w: pl.BlockSpec((tb, w), lambda i: (i, 0))
    fold64, fold128 = folds

    def body(*refs):
        pieces = [r[...] for r in refs[0:8]]
        prm_v = [r[...] for r in refs[9:21]]
        consts_v = [r[...] for r in refs[21:29]]
        cos_v, sin_v = refs[29][...], refs[30][...]
        dfq, dfk, dfv, dmq, dmk, dmv, dsq, dskp, dsvp, dls = [r[...] for r in refs[31:41]]
        f64, f128 = refs[41][...], refs[42][...]
        d_ref = refs[43]
        g_refs = refs[44:]

        @pl.when(pl.program_id(0) == 0)
        def _():
            for r in g_refs:
                r[...] = jnp.zeros_like(r)

        def pair_sum(p):
            return jnp.concatenate([p[:, 0:128] + p[:, 128:256], p[:, 256:384] + p[:, 384:512]], axis=1)

        f = lambda pc, pr: _prep_math(pc, pr, consts_v, cos_v, sin_v)
        _, vjp = jax.vjp(f, pieces, prm_v)
        dpc, dprm = vjp((dfq, dfk, dls, dmq, dmk, dmv, dsq, pair_sum(dskp), pair_sum(dsvp)))
        d_fq, d_fk, d_sq, d_sk, d_sv, d_cq, d_ckv, d_misc = dpc
        for off, val in ((O_FQ, d_fq), (O_FK, d_fk), (O_FV, dfv), (O_SQ, d_sq), (O_SK, d_sk), (O_SV, d_sv),
                         (O_CQ, d_cq), (O_CKV, d_ckv), (O_MISC, d_misc)):
            d_ref[:, off:off + val.shape[1]] = val.astype(BF16)
        folded = {0: f64, 1: f64, 2: f64, 3: f64[0:128], 7: f128, 8: f128}
        for idx, (ref, val) in enumerate(zip(g_refs, dprm)):
            if idx in folded:
                ref[...] += _split_dot(jnp.broadcast_to(val, (8, val.shape[1])), folded[idx], 3)
            elif val.shape[0] == 1:
                ref[...] += jnp.broadcast_to(val, ref.shape)
            else:
                ref[...] += val

    g_shapes = [(8, 128), (8, 128), (8, 128), (8, 128), (8, 128), (8, 256), (8, 128), (8, 128), (8, 128),
                (256, 1024), (128, 1024), (128, 512)]
    cot_widths = [512, 512, 512, 1024, 1024, 512, 512, 512, 512, 128]
    return pl.pallas_call(
        body, name=name, grid=(lp // tb,),
        in_specs=(_piece_specs(tb) + _full_specs(_PRM_SHAPES) + _full_specs(_CONST_SHAPES) + [row(128), row(128)]
                  + [row(w) for w in cot_widths] + _full_specs([(512, 128), (1024, 128)])),
        out_specs=[row(OTHER_W)] + _full_specs(g_shapes),
        out_shape=[SDS((lp, OTHER_W), BF16)] + [SDS(s, F32) for s in g_shapes],
        compiler_params=_params(("arbitrary",)),
    )(*([proj] * 9), *prm, *consts, cos, sin, *cots, fold64, fold128)


def _cumsum(xs, reverse, name):
    lp = xs[0].shape[0]
    tb = TILE_MM
    nb = lp // tb
    n_in = len(xs)
    idx = (lambda i: (nb - 1 - i, 0)) if reverse else (lambda i: (i, 0))

    def body(*refs):
        o_ref, carry = refs[n_in], refs[n_in + 1]

        @pl.when(pl.program_id(0) == 0)
        def _():
            carry[...] = jnp.zeros_like(carry)

        x = refs[0][...]
        for r in refs[1:n_in]:
            x = x + r[...]
        r_i = lax.broadcasted_iota(jnp.int32, (tb, tb), 0)
        c_i = lax.broadcasted_iota(jnp.int32, (tb, tb), 1)
        tri = ((c_i >= r_i) if reverse else (c_i <= r_i)).astype(BF16)
        acc, rest = None, x
        for _ in range(3):
            part = rest.astype(BF16)
            rest = rest - part.astype(F32)
            acc = _dot(tri, part) if acc is None else acc + _dot(tri, part)
        o_ref[...] = acc + carry[...]
        carry[...] += jnp.sum(x, axis=0, keepdims=True)

    return pl.pallas_call(
        body, name=name, grid=(nb,),
        in_specs=[pl.BlockSpec((tb, 128), idx)] * n_in,
        out_specs=pl.BlockSpec((tb, 128), idx),
        out_shape=SDS((lp, 128), F32),
        scratch_shapes=[pltpu.VMEM((1, 128), F32)],
        compiler_params=_params(("arbitrary",)),
    )(*xs)


class _Att:
    def __init__(self, mode):
        self.mode = mode
        self.wide = mode == "mla"
        self.qw = 256 if self.wide else 128
        self.scale = (96 if mode == "mla" else 64) ** -0.5

    def resident(self, x, lo, scaled):
        if self.wide:
            return x[:, 0:128], x[:, 128:256]
        if scaled:
            x = x * jnp.asarray(self.scale, x.dtype)
        zero = jnp.zeros_like(x)
        return jnp.where(lo, x, zero), jnp.where(lo, zero, x)

    def moving(self, x):
        return (x[:, 0:128], x[:, 128:256]) if self.wide else (x, x)

    def logits(self, a, b, qpos, kpos, key_decay, slope, masked):
        s = _dot_nt(a, b)
        if self.wide:
            s = s * self.scale
        if self.mode == "fox":
            s = s - key_decay
        if self.mode == "swa":
            s = s - slope * (qpos - kpos).astype(F32)
        if masked:
            ok = kpos <= qpos
            if self.mode == "swa":
                ok = ok & ((kpos < N_META) | (qpos - kpos < WINDOW))
            s = jnp.where(ok, s, NEG)
        return s


def _as_rows(col):
    return jnp.broadcast_to(col, (col.shape[0], 128)).T[0:8, :]


def _halves(x, lo):
    zero = jnp.zeros_like(x)
    return jnp.where(lo, x, zero), jnp.where(lo, zero, x)


def _kv_specs(att, lp, rows):
    if att.mode == "swa":
        return (pl.BlockSpec((rows, 128), lambda g, i: (i if rows != lp else 0, g)),) * 2
    return (pl.BlockSpec((rows, PAIRS * att.qw), lambda g, i: (i if rows != lp else 0, g)),
            pl.BlockSpec((rows, PAIRS * 128), lambda g, i: (i if rows != lp else 0, g)))


def _pair_cols(att, x, pp, width):
    return x if x.shape[1] == width else x[:, pp * width:(pp + 1) * width]


def _att_fwd(att, q, k, v, extra, name):
    lp = q.shape[0]
    t = TILE_ATT
    nq = lp // t
    qw = att.qw
    mode = att.mode
    nh = 2 * PAIRS

    def body(*refs):
        q_ref, k_ref, v_ref = refs[0:3]
        o_ref, lse_ref, row_ref = refs[-3:]
        g, qi = pl.program_id(0), pl.program_id(1)
        lo = lax.broadcasted_iota(jnp.int32, (1, 128), 1) < 64
        q_all = q_ref[...]
        q_heads = [h for pp in range(PAIRS) for h in att.resident(_pair_cols(att, q_all, pp, qw), lo, True)]
        qpos = qi * t + lax.broadcasted_iota(jnp.int32, (t, 1), 0)

        def step(first, cols, carry, masked):
            ks = pl.multiple_of(first, 128)
            kc, vc = k_ref[pl.ds(ks, cols), :], v_ref[pl.ds(ks, cols), :]
            kpos = first + lax.broadcasted_iota(jnp.int32, (1, cols), 1)
            out = []
            for h in range(nh):
                pp = h // 2
                m, l, acc = carry[3 * h:3 * h + 3]
                k_h = att.moving(_pair_cols(att, kc, pp, qw))[h % 2]
                decay = refs[3][h, :, pl.ds(ks, cols)] if mode == "fox" else None
                slope = refs[4][nh * g + h] if mode == "swa" else None
                s = att.logits(q_heads[h], k_h, qpos, kpos, decay, slope, masked)
                m_new = jnp.maximum(m, jnp.max(s, axis=-1, keepdims=True))
                alpha = jnp.exp(m - m_new)
                pe = jnp.exp(s - m_new)
                l = alpha * l + jnp.sum(pe, axis=-1, keepdims=True)
                acc = alpha * acc + _dot(pe.astype(BF16), _pair_cols(att, vc, pp, 128))
                out += [m_new, l, acc]
            return tuple(out)

        init = []
        for h in range(nh):
            if mode == "swa":
                init += [jnp.full((t, 1), refs[3][nh * g + h], F32), jnp.ones((t, 1), F32)]
            else:
                init += [jnp.full((t, 1), NEG, F32), jnp.zeros((t, 1), F32)]
            init.append(jnp.zeros((t, 128), F32))
        if mode == "swa":
            band = jnp.maximum(qi * t - WINDOW, 0)
            carry = lax.fori_loop(0, (band >= 128).astype(jnp.int32), lambda j, c: step(0, 128, c, True), tuple(init))
            carry = step(band, t + WINDOW, carry, True)
        else:
            carry = lax.fori_loop(0, qi // 2, lambda j, c: step(2 * j * t, 2 * t, c, False), tuple(init))
            carry = lax.fori_loop(0, qi % 2, lambda j, c: step((qi - 1) * t, t, c, False), carry)
            carry = step(qi * t, t, carry, True)
        outs = []
        for pp in range(PAIRS):
            (ma, la, acca), (mb, lb, accb) = carry[6 * pp:6 * pp + 3], carry[6 * pp + 3:6 * pp + 6]
            outs.append(jnp.where(lo, acca / la, accb / lb).astype(BF16))
            for h, lse in ((2 * pp, ma + jnp.log(la)), (2 * pp + 1, mb + jnp.log(lb))):
                lse_ref[h] = lse
                row_ref[h] = _as_rows(lse)
        o_ref[...] = jnp.concatenate(outs, axis=1)

    in_specs = [pl.BlockSpec((t, PAIRS * qw), lambda g, i: (i, g)), *_kv_specs(att, lp, lp)]
    if mode == "fox":
        in_specs += [pl.BlockSpec((nh, 1, lp), lambda g, i: (g, 0, 0))]
    if mode == "swa":
        in_specs += [pl.BlockSpec(memory_space=pltpu.SMEM)] * 2
    return pl.pallas_call(
        body, name=name, grid=(4 // PAIRS, nq), in_specs=in_specs,
        out_specs=[pl.BlockSpec((t, PAIRS * 128), lambda g, i: (i, g)), pl.BlockSpec((nh, t, 1), lambda g, i: (g, i, 0)),
                   pl.BlockSpec((nh, 8, t), lambda g, i: (g, 0, i))],
        out_shape=[SDS((lp, 512), BF16), SDS((HEADS, lp, 1), F32), SDS((HEADS, 8, lp), F32)],
        compiler_params=_params(("parallel", "arbitrary")),
    )(q, k, v, *extra)


def _att_dq(att, q, k, v, o, do, lse, extra, name):
    lp = q.shape[0]
    t = TILE_ATT
    nq = lp // t
    qw = att.qw
    mode = att.mode
    nh = 2 * PAIRS

    def body(*refs):
        q_ref, k_ref, v_ref, o_ref, do_ref, lse_ref = refs[0:6]
        n_out = 3 if mode == "mla" else 4
        outs = refs[len(refs) - n_out:]
        dq_ref, delta_ref, row_ref = outs[0:3]
        outs = outs[1:]
        g, qi = pl.program_id(0), pl.program_id(1)
        lo = lax.broadcasted_iota(jnp.int32, (1, 128), 1) < 64
        q_all, do_all = q_ref[...], do_ref[...]
        prod = do_all.astype(F32) * o_ref[...].astype(F32)
        q_heads, do_heads, delta = [], [], []
        for pp in range(PAIRS):
            q_heads += att.resident(_pair_cols(att, q_all, pp, qw), lo, True)
            do_heads += _halves(_pair_cols(att, do_all, pp, 128), lo)
            pr_pp = _pair_cols(att, prod, pp, 128)
            delta += [jnp.sum(jnp.where(lo, pr_pp, 0.0), axis=-1, keepdims=True),
                      jnp.sum(jnp.where(lo, 0.0, pr_pp), axis=-1, keepdims=True)]
        lse_v = [lse_ref[h] for h in range(nh)]
        qpos = qi * t + lax.broadcasted_iota(jnp.int32, (t, 1), 0)

        def step(first, cols, carry, masked):
            ks = pl.multiple_of(first, 128)
            kc, vc = k_ref[pl.ds(ks, cols), :], v_ref[pl.ds(ks, cols), :]
            kpos = first + lax.broadcasted_iota(jnp.int32, (1, cols), 1)
            out = []
            for h in range(nh):
                pp = h // 2
                k_h = att.moving(_pair_cols(att, kc, pp, qw))[h % 2]
                decay = refs[6][h, :, pl.ds(ks, cols)] if mode == "fox" else None
                slope = refs[7][nh * g + h] if mode == "swa" else None
                s = att.logits(q_heads[h], k_h, qpos, kpos, decay, slope, masked)
                pr = jnp.exp(s - lse_v[h])
                ds = pr * (_dot_nt(do_heads[h], _pair_cols(att, vc, pp, 128)) - delta[h])
                out.append(carry[2 * h] + _dot(ds.astype(BF16), k_h))
                out.append(carry[2 * h + 1] + jnp.sum(ds, axis=-1, keepdims=True) if mode == "fox" else carry[2 * h + 1])
            return tuple(out)

        init = (jnp.zeros((t, 128), F32), jnp.zeros((t, 1), F32)) * nh
        if mode == "swa":
            band = jnp.maximum(qi * t - WINDOW, 0)
            carry = lax.fori_loop(0, (band >= 128).astype(jnp.int32), lambda j, c: step(0, 128, c, True), init)
            carry = step(band, t + WINDOW, carry, True)
        else:
            carry = lax.fori_loop(0, qi // 2, lambda j, c: step(2 * j * t, 2 * t, c, False), init)
            carry = lax.fori_loop(0, qi % 2, lambda j, c: step((qi - 1) * t, t, c, False), carry)
            carry = step(qi * t, t, carry, True)
        dq = []
        for pp in range(PAIRS):
            dqa, dca, dqb, dcb = carry[4 * pp:4 * pp + 4]
            dq += [dqa, dqb] if att.wide else [jnp.where(lo, dqa, dqb)]
            if mode == "fox":
                outs[2][2 * pp] = dca
                outs[2][2 * pp + 1] = dcb
        dq_ref[...] = jnp.concatenate(dq, axis=1) * att.scale
        for h in range(nh):
            delta_ref[h] = delta[h]
            row_ref[h] = _as_rows(delta[h])
        if mode == "swa":
            ds_ref = outs[2]

            @pl.when(qi == 0)
            def _():
                ds_ref[...] = jnp.zeros_like(ds_ref)

            lane = lax.broadcasted_iota(jnp.int32, (8, 128), 1)
            acc = jnp.zeros((8, 128), F32)
            for h in range(nh):
                tot = -jnp.sum(jnp.exp(refs[6][nh * g + h] - lse_v[h]) * delta[h])
                acc = acc + jnp.where(lane == h, tot, 0.0)
            ds_ref[0] += acc

    col = pl.BlockSpec((nh, t, 1), lambda g, i: (g, i, 0))
    in_specs = [pl.BlockSpec((t, PAIRS * qw), lambda g, i: (i, g)), *_kv_specs(att, lp, lp),
                pl.BlockSpec((t, PAIRS * 128), lambda g, i: (i, g)), pl.BlockSpec((t, PAIRS * 128), lambda g, i: (i, g)), col]
    out_specs = [pl.BlockSpec((t, PAIRS * qw), lambda g, i: (i, g)), col, pl.BlockSpec((nh, 8, t), lambda g, i: (g, 0, i))]
    out_shape = [SDS((lp, 4 * qw), F32), SDS((HEADS, lp, 1), F32), SDS((HEADS, 8, lp), F32)]
    if mode == "fox":
        in_specs += [pl.BlockSpec((nh, 1, lp), lambda g, i: (g, 0, 0))]
        out_specs.append(col)
        out_shape.append(SDS((HEADS, lp, 1), F32))
    if mode == "swa":
        in_specs += [pl.BlockSpec(memory_space=pltpu.SMEM)] * 2
        out_specs.append(pl.BlockSpec((1, 8, 128), lambda g, i: (g, 0, 0)))
        out_shape.append(SDS((4 // PAIRS, 8, 128), F32))
    return pl.pallas_call(
        body, name=name, grid=(4 // PAIRS, nq), in_specs=in_specs, out_specs=out_specs, out_shape=out_shape,
        compiler_params=_params(("parallel", "arbitrary")),
    )(q, k, v, o, do, lse, *extra)


def _att_dkv(att, q, k, v, do, lse_row, delta_row, extra, name):
    lp = q.shape[0]
    t = TILE_ATT
    nq = lp // t
    qw = att.qw
    mode = att.mode
    nh = 2 * PAIRS

    def body(*refs):
        q_ref, k_ref, v_ref, do_ref, lse_ref, delta_ref = refs[0:6]
        n_out = 3 if mode == "fox" else 2
        outs = refs[len(refs) - n_out:]
        dk_ref, dv_ref = outs[0:2]
        g, kj = pl.program_id(0), pl.program_id(1)
        lo = lax.broadcasted_iota(jnp.int32, (1, 128), 1) < 64
        k_all, v_all = k_ref[...], v_ref[...]
        k_heads, v_heads = [], []
        for pp in range(PAIRS):
            k_heads += att.resident(_pair_cols(att, k_all, pp, qw), lo, True)
            v_heads += _halves(_pair_cols(att, v_all, pp, 128), lo)
        kpos = kj * t + lax.broadcasted_iota(jnp.int32, (t, 1), 0)

        def step(first, cols, carry, masked):
            qs = pl.multiple_of(first, 128)
            qc, doc = q_ref[pl.ds(qs, cols), :], do_ref[pl.ds(qs, cols), :]
            qpos = first + lax.broadcasted_iota(jnp.int32, (1, cols), 1)
            out = []
            for h in range(nh):
                pp = h // 2
                dk_acc, dv_acc, dc_acc = carry[3 * h:3 * h + 3]
                q_h = att.moving(_pair_cols(att, qc, pp, qw))[h % 2]
                do_h = _pair_cols(att, doc, pp, 128)
                decay = refs[6][h] if mode == "fox" else None
                slope = refs[6][nh * g + h] if mode == "swa" else None
                st = att.logits(k_heads[h], q_h, qpos, kpos, decay, slope, masked)
                pt = jnp.exp(st - lse_ref[h, 0:1, pl.ds(qs, cols)])
                dst = pt * (_dot_nt(v_heads[h], do_h) - delta_ref[h, 0:1, pl.ds(qs, cols)])
                dv_acc = dv_acc + _dot(pt.astype(BF16), do_h)
                dk_acc = dk_acc + _dot(dst.astype(BF16), q_h)
                if mode == "fox":
                    dc_acc = dc_acc - jnp.sum(dst, axis=-1, keepdims=True)
                out += [dk_acc, dv_acc, dc_acc]
            return tuple(out)

        init = (jnp.zeros((t, 128), F32), jnp.zeros((t, 128), F32), jnp.zeros((t, 1), F32)) * nh
        if mode == "swa":
            carry = lax.fori_loop(0, jnp.where(kj == 0, nq, 0), lambda qi, c: step(qi * t, t, c, True), init)
            near = jnp.minimum(kj * t, lp - (t + WINDOW))
            carry = lax.fori_loop(0, (kj > 0).astype(jnp.int32), lambda j, c: step(near, t + WINDOW, c, True), carry)
        else:
            carry = step(kj * t, t, init, True)
            rest = nq - 1 - kj
            carry = lax.fori_loop(0, rest // 2, lambda j, c: step((kj + 1 + 2 * j) * t, 2 * t, c, False), carry)
            carry = lax.fori_loop(0, rest % 2, lambda j, c: step((nq - 1) * t, t, c, False), carry)
        dk, dv = [], []
        for pp in range(PAIRS):
            dka, dva, dca, dkb, dvb, dcb = carry[6 * pp:6 * pp + 6]
            dk += [dka, dkb] if att.wide else [jnp.where(lo, dka, dkb)]
            dv.append(jnp.where(lo, dva, dvb))
            if mode == "fox":
                outs[2][2 * pp] = dca
                outs[2][2 * pp + 1] = dcb
        dk_ref[...] = jnp.concatenate(dk, axis=1) * att.scale
        dv_ref[...] = jnp.concatenate(dv, axis=1)

    rowv = pl.BlockSpec((nh, 8, lp), lambda g, j: (g, 0, 0))
    col = pl.BlockSpec((nh, t, 1), lambda g, j: (g, j, 0))
    in_specs = [pl.BlockSpec((lp, PAIRS * qw), lambda g, j: (0, g)), *_kv_specs(att, lp, t),
                pl.BlockSpec((lp, PAIRS * 128), lambda g, j: (0, g)), rowv, rowv]
    out_specs = [pl.BlockSpec((t, PAIRS * qw), lambda g, j: (j, g)), pl.BlockSpec((t, PAIRS * 128), lambda g, j: (j, g))]
    out_shape = [SDS((lp, 4 * qw), F32), SDS((lp, 512), F32)]
    if mode == "fox":
        in_specs += [col]
        out_specs.append(col)
        out_shape.append(SDS((HEADS, lp, 1), F32))
    if mode == "swa":
        in_specs += [pl.BlockSpec(memory_space=pltpu.SMEM)]
    return pl.pallas_call(
        body, name=name, grid=(4 // PAIRS, nq), in_specs=in_specs, out_specs=out_specs, out_shape=out_shape,
        compiler_params=_params(("parallel", "arbitrary")),
    )(q, k, v, do, lse_row, delta_row, *extra)


def _post_fwd(h, proj, outs, wb, wo, name):
    lp, d = h.shape
    tb = TILE_POST
    row = lambda w: pl.BlockSpec((tb, w), lambda i: (i, 0))

    def body(h_ref, g0, g1, g2, oa, ob, oc, wb_ref, wo_ref, o_ref):
        merged = jnp.zeros((tb, d), F32)
        for n, (g_ref, br) in enumerate(((g0, oa), (g1, ob), (g2, oc))):
            merged = merged + jax.nn.sigmoid(g_ref[...]) * _dot(br[...], wb_ref[n])
        o_ref[...] = h_ref[...] + _dot(merged.astype(BF16), wo_ref[...])

    gate = lambda n: pl.BlockSpec((tb, d), lambda i, n=n: (i, n))
    return pl.pallas_call(
        body, name=name, grid=(lp // tb,),
        in_specs=[row(d), gate(0), gate(1), gate(2), row(512), row(512), row(512),
                  pl.BlockSpec((3, 512, d), lambda i: (0, 0, 0)), pl.BlockSpec((d, d), lambda i: (0, 0))],
        out_specs=row(d), out_shape=SDS((lp, d), F32),
        compiler_params=_params(("parallel",)),
    )(h, proj, proj, proj, *outs, wb, wo)


def _post_bwd(dh, proj, outs, wb, wo, name):
    lp, d = dh.shape
    tb = TILE_POST
    row = lambda w: pl.BlockSpec((tb, w), lambda i: (i, 0))

    def body(dh_ref, g0, g1, g2, oa, ob, oc, wb_ref, wo_ref, dg_ref, doa, dob, doc, dwb_ref, dwo_ref):
        @pl.when(pl.program_id(0) == 0)
        def _():
            dwb_ref[...] = jnp.zeros_like(dwb_ref)
            dwo_ref[...] = jnp.zeros_like(dwo_ref)

        dhb = dh_ref[...].astype(BF16)
        dm = _dot_nt(dhb, wo_ref[...])
        merged = jnp.zeros((tb, d), F32)
        for n, (g_ref, br, do_ref) in enumerate(((g0, oa, doa), (g1, ob, dob), (g2, oc, doc))):
            gate = jax.nn.sigmoid(g_ref[...])
            o_n = br[...]
            y = _dot(o_n, wb_ref[n])
            merged = merged + gate * y
            dy = (dm * gate).astype(BF16)
            dg_ref[:, n * d:(n + 1) * d] = (dm * y * gate * (1.0 - gate)).astype(BF16)
            do_ref[...] = _dot_nt(dy, wb_ref[n]).astype(BF16)
            dwb_ref[n] += _dot_tn(o_n, dy)
        dwo_ref[...] += _dot_tn(merged.astype(BF16), dhb)

    gate = lambda n: pl.BlockSpec((tb, d), lambda i, n=n: (i, n))
    wb_spec = pl.BlockSpec((3, 512, d), lambda i: (0, 0, 0))
    wo_spec = pl.BlockSpec((d, d), lambda i: (0, 0))
    return pl.pallas_call(
        body, name=name, grid=(lp // tb,),
        in_specs=[row(d), gate(0), gate(1), gate(2), row(512), row(512), row(512), wb_spec, wo_spec],
        out_specs=[row(GATES_W), row(512), row(512), row(512), wb_spec, wo_spec],
        out_shape=[SDS((lp, GATES_W), BF16)] + [SDS((lp, 512), BF16)] * 3 + [SDS((3, 512, d), F32), SDS((d, d), F32)],
        compiler_params=_params(("arbitrary",)),
    )(dh, proj, proj, proj, *outs, wb, wo)


def _shift_down(x, halo, n, first):
    rows = lax.broadcasted_iota(jnp.int32, x.shape, 0)
    edge = jnp.concatenate([pltpu.roll(halo, n, 0), jnp.zeros((x.shape[0] - 8, x.shape[1]), F32)], axis=0)
    edge = jnp.where(first, 0.0, edge)
    return jnp.where(rows < n, edge, pltpu.roll(x, n, 0))


def _shift_up(x, halo, n, last):
    tb = x.shape[0]
    rows = lax.broadcasted_iota(jnp.int32, x.shape, 0)
    edge = jnp.concatenate([jnp.zeros((tb - 8, x.shape[1]), F32), pltpu.roll(halo, 8 - n, 0)], axis=0)
    edge = jnp.where(last, 0.0, edge)
    return jnp.where(rows >= tb - n, edge, pltpu.roll(x, tb - n, 0))


def _conv(u, halo, w_ref, b_ref, first):
    taps = (_shift_down(u, halo, 2, first), _shift_down(u, halo, 1, first), u)
    c = b_ref[...] + w_ref[0:1, :] * taps[0] + w_ref[1:2, :] * taps[1] + w_ref[2:3, :] * taps[2]
    return c, taps


def _ffn_specs(tb, f):
    hb = tb // 8
    cur = lambda c: pl.BlockSpec((tb, f), lambda i, c=c: (i, c))
    prev = lambda c: pl.BlockSpec((8, f), lambda i, c=c: (jnp.maximum(i * hb - 1, 0), c))
    vec = lambda r, c: pl.BlockSpec((r, f), lambda i, c=c: (0, c))
    return cur, prev, vec


def _ffn_act_fwd(u, cw, cb, name):
    lp = u.shape[0]
    f = D_FF
    tb = TILE_ROW
    cur, prev, vec = _ffn_specs(tb, f)

    def body(ug, uv, hg, hv, wg, wv, bg, bv, o_ref):
        first = pl.program_id(0) == 0
        cg, _ = _conv(ug[...], hg[...], wg, bg, first)
        cv, _ = _conv(uv[...], hv[...], wv, bv, first)
        o_ref[...] = (cg * jax.nn.sigmoid(cg) * cv).astype(BF16)

    return pl.pallas_call(
        body, name=name, grid=(lp // tb,),
        in_specs=[cur(0), cur(1), prev(0), prev(1), vec(8, 0), vec(8, 1), vec(1, 0), vec(1, 1)],
        out_specs=pl.BlockSpec((tb, f), lambda i: (i, 0)), out_shape=SDS((lp, f), BF16),
        compiler_params=_params(("parallel",)),
    )(u, u, u, u, cw, cw, cb, cb)


def _ffn_act_bwd_conv(u, dact, cw, cb, name):
    lp = u.shape[0]
    f = D_FF
    tb = TILE_ROW
    cur, prev, vec = _ffn_specs(tb, f)

    def body(ug, uv, hg, hv, wg, wv, bg, bv, da_ref, dc_ref, dw_ref, db_ref):
        first = pl.program_id(0) == 0

        @pl.when(first)
        def _():
            dw_ref[...] = jnp.zeros_like(dw_ref)
            db_ref[...] = jnp.zeros_like(db_ref)

        cg, tg = _conv(ug[...], hg[...], wg, bg, first)
        cv, tv = _conv(uv[...], hv[...], wv, bv, first)
        da = da_ref[...]
        sg = jax.nn.sigmoid(cg)
        dcg = da * cv * sg * (1.0 + cg * (1.0 - sg))
        dcv = da * cg * sg
        for c, (dc, taps) in enumerate(((dcg, tg), (dcv, tv))):
            dc_ref[:, c * f:(c + 1) * f] = dc
            for n in range(3):
                dw_ref[n:n + 1, c * f:(c + 1) * f] += jnp.sum(dc * taps[n], axis=0, keepdims=True)
            db_ref[0:1, c * f:(c + 1) * f] += jnp.sum(dc, axis=0, keepdims=True)

    acc = pl.BlockSpec((8, 2 * f), lambda i: (0, 0))
    return pl.pallas_call(
        body, name=name, grid=(lp // tb,),
        in_specs=[cur(0), cur(1), prev(0), prev(1), vec(8, 0), vec(8, 1), vec(1, 0), vec(1, 1),
                  pl.BlockSpec((tb, f), lambda i: (i, 0))],
        out_specs=[pl.BlockSpec((tb, 2 * f), lambda i: (i, 0)), acc, acc],
        out_shape=[SDS((lp, 2 * f), F32), SDS((8, 2 * f), F32), SDS((8, 2 * f), F32)],
        compiler_params=_params(("arbitrary",)),
    )(u, u, u, u, cw, cw, cb, cb, dact)


def _ffn_act_bwd_in(dc, cw, name):
    lp = dc.shape[0]
    f2 = 2 * D_FF
    tb = TILE_ROW
    nb = lp // tb
    hb = tb // 8

    def body(dc_ref, n_ref, w_ref, o_ref):
        last = pl.program_id(0) == nb - 1
        dcv, halo = dc_ref[...], n_ref[...]
        du = (w_ref[2:3, :] * dcv + w_ref[1:2, :] * _shift_up(dcv, halo, 1, last)
              + w_ref[0:1, :] * _shift_up(dcv, halo, 2, last))
        o_ref[...] = du.astype(BF16)

    cur = pl.BlockSpec((tb, f2), lambda i: (i, 0))
    return pl.pallas_call(
        body, name=name, grid=(nb,),
        in_specs=[cur, pl.BlockSpec((8, f2), lambda i: (jnp.minimum((i + 1) * hb, nb * hb - 1), 0)),
                  pl.BlockSpec((8, f2), lambda i: (0, 0))],
        out_specs=cur, out_shape=SDS((lp, f2), BF16),
        compiler_params=_params(("parallel",)),
    )(dc, dc, cw)


def _loss_head(y, target, n_real, name):
    lp, d = y.shape
    tb = TILE_MM

    def body(y_ref, t_ref, dy_ref, loss_ref):
        i = pl.program_id(0)

        @pl.when(i == 0)
        def _():
            loss_ref[...] = jnp.zeros_like(loss_ref)

        rows = i * tb + lax.broadcasted_iota(jnp.int32, (tb, 1), 0)
        real = (rows >= N_META) & (rows < N_META + n_real)
        diff = jnp.where(real, y_ref[...] - t_ref[...], 0.0)
        dy_ref[...] = diff * (1.0 / d)
        loss_ref[...] += (0.5 / d) * jnp.sum(diff * diff).reshape(1, 1)

    row = pl.BlockSpec((tb, d), lambda i: (i, 0))
    return pl.pallas_call(
        body, name=name, grid=(lp // tb,), in_specs=[row, row],
        out_specs=[row, pl.BlockSpec((1, 1), lambda i: (0, 0))],
        out_shape=[SDS((lp, d), F32), SDS((1, 1), F32)],
        compiler_params=_params(("arbitrary",)),
    )(y, target)


def _pad_lanes(v, width, at=0):
    return jnp.pad(v.astype(F32), (at, width - at - v.shape[0]))[None, :]


def _mix_params(w, big, l):
    b = lambda a: a.astype(BF16)
    win = big["w_in"]
    fq, fk, fv, ff, cq, ckv, kr, sq, sk, sv, gates = jnp.split(
        win, [512, 1024, 1536, 1544, 1800, 1928, 1960, 2472, 2600, 2728], axis=1)
    misc = jnp.concatenate([kr, ff, jnp.zeros((D_MODEL, 88), win.dtype)], axis=1)
    w_in = b(jnp.concatenate([gates, fq, fk, fv, sq, sk, sv, cq, ckv, misc], axis=1))
    wq = jnp.pad(big["mla_w_q_up"].reshape(256, HEADS, 96), ((0, 0), (0, 0), (0, 32))).reshape(256, 1024)
    wkv = big["mla_w_kv_up"].reshape(128, HEADS, 128)
    wkk = jnp.pad(wkv[:, :, :64], ((0, 0), (0, 0), (0, 64))).reshape(128, 1024)
    wkvv = wkv[:, :, 64:].reshape(128, 512)
    tile = lambda g, n: jnp.tile(g.astype(F32), n)[None, :]
    prm = [tile(w["fox_q_g"][l], 8), tile(w["fox_k_g"][l], 8), tile(w["swa_q_g"][l], 8), tile(w["swa_k_g"][l], 2),
           _pad_lanes(w["fox_forget_b"][l], 128, FF_LANE), w["mla_q_a_g"][l][None, :], w["mla_kv_a_g"][l][None, :],
           tile(jnp.pad(w["mla_q_g"][l], (0, 32)), 8), tile(jnp.pad(w["mla_k_g"][l], (0, 32)), 8),
           wq.astype(F32), wkk.astype(F32), wkvv.astype(F32)]
    return dict(g1=w["norm1_g"][l][None, :], w_in=w_in, prm=prm, sinks=w["swa_sinks"][l].astype(F32),
                wb=b(big["w_branch"]), wo=b(big["w_o"]))


def _ffn_params(w, big, l):
    cw = jnp.pad(w["ffn_conv_w"][l].astype(F32), ((0, 5), (0, 0)))
    return dict(g2=w["norm2_g"][l][None, :], w_up=big["ffn_w_up"].astype(BF16), cw=cw,
                cb=w["ffn_conv_b"][l][None, :].astype(F32), w_down=big["ffn_w_down"].astype(BF16))


def _cols(c):
    ct = c[:, FF_LANE:FF_LANE + HEADS].T
    return ct[:, :, None], ct[:, None, :]


def _from_cols(col):
    return jnp.pad(col[:, :, 0].T, ((0, 0), (FF_LANE, 128 - FF_LANE - HEADS)))


def _layer_fwd_mix(h, lw, consts, cos, sin, slopes, l):
    tag = f"l{l}_"
    xn, proj = _norm_matmul(h, lw["g1"], lw["w_in"], IN_W // 2, tag + "in_proj")
    fq, fk, fv, mq, mk, mv, sq, skd, svd, ls = _prep_fwd(proj, lw["prm"], consts, cos, sin, tag + "prep")
    c = _cumsum([ls], False, tag + "decay_cumsum")
    c_col, c_row = _cols(c)
    oa, *lse_a = _att_fwd(_Att("fox"), fq, fk, fv, (c_row,), tag + "fox_fwd")
    ob, *lse_b = _att_fwd(_Att("mla"), mq, mk, mv, (), tag + "mla_fwd")
    oc, *lse_c = _att_fwd(_Att("swa"), sq, skd, svd, (lw["sinks"], slopes), tag + "swa_fwd")
    h2 = _post_fwd(h, proj, (oa, ob, oc), lw["wb"], lw["wo"], tag + "merge")
    saved = dict(h=h, xn=xn, proj=proj, q=(fq, mq, sq), k=(fk, mk, skd), v=(fv, mv, svd), c=(c_col, c_row),
                 o=(oa, ob, oc), lse=(lse_a, lse_b, lse_c), h2=h2)
    return h2, saved


def _layer_fwd_ffn(h2, lw, l):
    tag = f"l{l}_"
    xn2, u = _norm_matmul(h2, lw["g2"], lw["w_up"], D_FF, tag + "ffn_up")
    act = _ffn_act_fwd(u, lw["cw"], lw["cb"], tag + "ffn_act")
    h3 = _matmul_residual(act, lw["w_down"], h2, tag + "ffn_down")
    return h3, dict(xn2=xn2, u=u, act=act)


def _layer_bwd_ffn(dh3, lw, sv, l):
    tag = f"l{l}_"
    f = D_FF
    dact = _matmul_nt(dh3, lw["w_down"], f, tag + "ffn_down_dx")
    dw_down = _matmul_tn(sv["act"], dh3, D_MODEL, tag + "ffn_down_dw")
    dc, dcw, dcb = _ffn_act_bwd_conv(sv["u"], dact, lw["cw"], lw["cb"], tag + "ffn_act_dc")
    du = _ffn_act_bwd_in(dc, lw["cw"], tag + "ffn_act_du")
    dw_up = _matmul_tn(sv["xn2"], du, f, tag + "ffn_up_dw")
    dh2, dg2 = _norm_matmul_bwd([du], lw["w_up"], sv["h2"], lw["g2"], dh3, tag + "ffn_up_dx")
    g = dict(norm2_g=dg2[0], ffn_w_up=dw_up, ffn_conv_w=dcw[0:3], ffn_conv_b=dcb[0], ffn_w_down=dw_down)
    return dh2, g


def _layer_bwd_mix(dh2, lw, sv, consts, folds, cos, sin, slopes, l, hook=None):
    tag = f"l{l}_"
    dgates, doa, dob, doc, dwb, dwo = _post_bwd(dh2, sv["proj"], sv["o"], lw["wb"], lw["wo"], tag + "merge_bwd")
    c_col, c_row = sv["c"]
    extras = ((c_row,), (), (lw["sinks"], slopes))
    extras_kv = ((c_col,), (), (slopes,))
    grads = []
    for n, (mode, do) in enumerate((("fox", doa), ("mla", dob), ("swa", doc))):
        att = _Att(mode)
        q, k, v = sv["q"][n], sv["k"][n], sv["v"][n]
        lse_col, lse_row = sv["lse"][n]
        res = _att_dq(att, q, k, v, sv["o"][n], do, lse_col, extras[n], tag + mode + "_dq")
        dq, delta_row = res[0], res[2]
        res_kv = _att_dkv(att, q, k, v, do, lse_row, delta_row, extras_kv[n], tag + mode + "_dkv")
        grads.append((dq, res_kv[0], res_kv[1], res[3:], res_kv[2:]))
    (dfq, dfk, dfv, (dcq,), (dck,)), (dmq, dmk, dmv, _, _), (dsq, dskp, dsvp, (dsink,), _) = grads
    dls = _cumsum([_from_cols(dcq), _from_cols(dck)], True, tag + "decay_cumsum_bwd")
    res = _prep_bwd(sv["proj"], lw["prm"], consts, cos, sin,
                    (dfq, dfk, dfv, dmq, dmk, dmv, dsq, dskp, dsvp, dls), folds, tag + "prep_bwd")
    dother, pg = res[0], res[1:]
    dw_g = _matmul_tn(sv["xn"], dgates, GATES_W, tag + "in_proj_dw_gates")
    dw_o = _matmul_tn(sv["xn"], dother, OTHER_W, tag + "in_proj_dw_other")
    d_in = jnp.concatenate([
        dw_o[:, O_FQ:O_FV + 512], dw_o[:, O_MISC + FF_LANE:O_MISC + FF_LANE + 8], dw_o[:, O_CQ:O_CQ + 256],
        dw_o[:, O_CKV:O_CKV + 128], dw_o[:, O_MISC:O_MISC + 32], dw_o[:, O_SQ:O_SQ + 512], dw_o[:, O_SK:O_SK + 128],
        dw_o[:, O_SV:O_SV + 128], dw_g], axis=1)
    d_wq = pg[9].reshape(256, HEADS, 128)[:, :, :96].reshape(256, 768)
    d_wkv = jnp.concatenate([pg[10].reshape(128, HEADS, 128)[:, :, :64], pg[11].reshape(128, HEADS, 64)],
                            axis=2).reshape(128, 1024)
    g = dict(
        w_in=d_in, fox_forget_b=pg[4][0, FF_LANE:FF_LANE + 8], fox_q_g=pg[0][0, :64],
        fox_k_g=pg[1][0, :64], mla_q_a_g=pg[5][0], mla_w_q_up=d_wq, mla_kv_a_g=pg[6][0], mla_w_kv_up=d_wkv,
        mla_q_g=pg[7][0, :96], mla_k_g=pg[8][0, :96], swa_q_g=pg[2][0, :64], swa_k_g=pg[3][0, :64],
        swa_sinks=dsink[:, 0, 0:2 * PAIRS].reshape(HEADS), w_branch=dwb, w_o=dwo)
    tick = hook(g) if hook else None
    g1 = lw["g1"] if tick is None else lw["g1"] + tick
    dh, dg1 = _norm_matmul_bwd([dgates, dother], lw["w_in"], sv["h"], g1, dh2, tag + "in_proj_dx")
    g["norm1_g"] = dg1[0]
    return dh, g


_MIX_BIG = ("w_in", "mla_w_q_up", "mla_w_kv_up", "w_branch", "w_o")
_FFN_BIG = ("ffn_w_up", "ffn_w_down")


def _local_step(x, target, w, hook=None, fetch=None):
    if fetch is None:
        fetch = lambda l, stage, after: {n: w[n][l] for n in (_MIX_BIG if stage == "mix" else _FFN_BIG)}
    seq = x.shape[0]
    length = N_META + seq
    lp = -(-length // ROW_ALIGN) * ROW_ALIGN
    pad = lp - length
    h = jnp.concatenate([w["meta_tokens"].astype(F32), x, jnp.zeros((pad, D_MODEL), F32)], axis=0)
    tgt = jnp.pad(target, ((N_META, pad), (0, 0)))
    consts = _consts()
    folds = (_fold_matrix(512, 64), _fold_matrix(1024, 128))
    cos, sin = _rope_tables(lp)
    slopes = jnp.asarray(2.0 ** (-8.0 * np.arange(1, HEADS + 1, dtype=np.float32) / HEADS), F32)
    lws, saved = [], []
    for l in range(DEPTH):
        lw = _mix_params(w, fetch(l, "mix", h), l)
        h, sv = _layer_fwd_mix(h, lw, consts, cos, sin, slopes, l)
        lw.update(_ffn_params(w, fetch(l, "ffn", h), l))
        h, sv_ffn = _layer_fwd_ffn(h, lw, l)
        lws.append(lw)
        saved.append({**sv, **sv_ffn})
    dh, loss = _loss_head(h, tgt, seq, "loss_head")
    grads = [None] * DEPTH
    for l in reversed(range(DEPTH)):
        dh, g_ffn = _layer_bwd_ffn(dh, lws[l], saved[l], l)
        tick = hook(l, "ffn", g_ffn) if hook else None
        if tick is not None:
            lws[l]["sinks"] = lws[l]["sinks"] + tick
        mix_hook = (lambda g, l=l, g_ffn=g_ffn: hook(l, "mix", {**g_ffn, **g})) if hook else None
        dh, g_mix = _layer_bwd_mix(dh, lws[l], saved[l], consts, folds, cos, sin, slopes, l, mix_hook)
        grads[l] = {**g_ffn, **g_mix}
    return loss, dh[N_META:length], dh[:N_META], grads


def _place():
    return lax.axis_index("x"), lax.axis_index("y"), lax.axis_index("c")


def _flip(pos, k):
    x, y, c = pos
    return (1 - x if k & 4 else x, 1 - y if k & 2 else y, 1 - c if k & 1 else c)


def _index(pos):
    return 4 * pos[0] + 2 * pos[1] + pos[2]


def _gather(tensors, name):
    n_t = len(tensors)

    def body(*refs):
        ins, outs = refs[:n_t], refs[n_t:2 * n_t]
        send_sems, recv_sems, local_sems = refs[2 * n_t:]
        x, y, c = _place()
        me, sibling = (x, y, c), (x, y, 1 - c)
        chips = [(1 - x, y), (x, 1 - y), (1 - x, 1 - y)]

        def copy(t, k, block, to, src=None):
            dst = outs[t].at[_index(block)]
            return pltpu.make_async_remote_copy(
                src_ref=dst if src is None else src, dst_ref=dst, send_sem=send_sems.at[t, k],
                recv_sem=recv_sems.at[t, k], device_id=to, device_id_type=pl.DeviceIdType.MESH)

        local, sent = [], []
        for t in range(n_t):
            local.append(pltpu.make_async_copy(ins[t], outs[t].at[_index(me)], local_sems.at[t]))
            local[-1].start()
            sent.append(copy(t, 0, me, sibling, src=ins[t]))
            sent += [copy(t, 1 + j, me, (*chip, c), src=ins[t]) for j, chip in enumerate(chips)]
        for cp in sent:
            cp.start()
        for j, chip in enumerate(chips):
            for t in range(n_t):
                copy(t, 1 + j, (*chip, c), me).wait_recv()
                sent.append(copy(t, 4 + j, (*chip, c), sibling))
                sent[-1].start()
        for t in range(n_t):
            copy(t, 0, sibling, me).wait_recv()
            for j, chip in enumerate(chips):
                copy(t, 4 + j, (*chip, 1 - c), me).wait_recv()
        for cp in sent:
            cp.wait_send()
        for cp in local:
            cp.wait()

    any_spec = pl.BlockSpec(memory_space=pl.ANY)
    return pl.pallas_call(
        body, name=name, in_specs=[any_spec] * n_t, out_specs=[any_spec] * n_t,
        out_shape=[SDS((N_DEV,) + a.shape, a.dtype) for a in tensors],
        scratch_shapes=[pltpu.SemaphoreType.DMA((n_t, N_DEV - 1)), pltpu.SemaphoreType.DMA((n_t, N_DEV - 1)),
                        pltpu.SemaphoreType.DMA((n_t,))],
    )(*tensors)


def _exchange_start(tensors, name, gather=False, after=None):
    n_t = len(tensors)

    def body(*refs):
        ins, lands = refs[:n_t], refs[n_t:2 * n_t]
        send_sem, recv_sem = refs[2 * n_t + 1:2 * n_t + 3]
        token = refs[-1]
        me = _place()
        mine = _index(me)
        for t in range(n_t):
            for k in range(1, N_DEV):
                peer = _flip(me, k)
                pltpu.make_async_remote_copy(
                    src_ref=ins[t] if gather else ins[t].at[_index(peer)], dst_ref=lands[t].at[mine],
                    send_sem=send_sem, recv_sem=recv_sem, device_id=peer, device_id_type=pl.DeviceIdType.MESH).start()
        token[...] = jnp.zeros_like(token)

    hbm = pl.BlockSpec(memory_space=pltpu.HBM)
    sem = pl.BlockSpec(memory_space=pltpu.SEMAPHORE)
    one = pltpu.SemaphoreType.DMA(())
    land_shape = lambda a: ((N_DEV,) + a.shape) if gather else a.shape
    bufs = ([pltpu.HBM(a.shape, a.dtype) for a in tensors] + [pltpu.HBM(land_shape(a), a.dtype) for a in tensors])
    after = jnp.zeros((8, 128), F32) if after is None else after
    outs = pl.pallas_call(
        body, name=name, in_specs=[hbm] * (2 * n_t) + [pl.BlockSpec(memory_space=pl.ANY)],
        out_specs=[sem, sem] + [hbm] * (2 * n_t) + [pl.BlockSpec(memory_space=pltpu.VMEM)],
        out_shape=[one, one] + bufs + [SDS((8, 128), F32)],
        input_output_aliases={i: 2 + i for i in range(2 * n_t)},
        compiler_params=pltpu.CompilerParams(has_side_effects=pltpu.SideEffectType.DATAFLOW_SIDE_EFFECTING),
    )(*[pltpu.with_memory_space_constraint(a, pltpu.HBM) for a in tensors],
      *[pltpu.with_memory_space_constraint(lax.empty(land_shape(a), a.dtype), pltpu.HBM) for a in tensors], after)
    return outs[:-1], outs[-1][0, 0]


def _exchange_wait(state, after, name, gather=False):
    n_t = (len(state) - 2) // 2

    def body(*refs):
        send_sem, recv_sem = refs[0:2]
        ins, lands = refs[2:2 + n_t], refs[2 + n_t:2 + 2 * n_t]
        me = _place()
        for t in range(n_t):
            for k in range(1, N_DEV):
                peer = _flip(me, k)
                copy = pltpu.make_async_remote_copy(
                    src_ref=ins[t] if gather else ins[t].at[_index(peer)], dst_ref=lands[t].at[_index(peer)],
                    send_sem=send_sem, recv_sem=recv_sem, device_id=peer, device_id_type=pl.DeviceIdType.MESH)
                copy.wait_send()
                copy.wait_recv()

    hbm = pl.BlockSpec(memory_space=pltpu.HBM)
    sem = pl.BlockSpec(memory_space=pltpu.SEMAPHORE)
    bufs = [pltpu.HBM(a.shape, a.dtype) for a in state[2:]]
    outs = pl.pallas_call(
        body, name=name, in_specs=[sem, sem] + [hbm] * (2 * n_t) + [pl.BlockSpec(memory_space=pl.ANY)],
        out_specs=[hbm] * (2 * n_t), out_shape=bufs,
        input_output_aliases={2 + i: i for i in range(2 * n_t)},
        compiler_params=pltpu.CompilerParams(has_side_effects=pltpu.SideEffectType.DATAFLOW_SIDE_EFFECTING),
    )(*state, after)
    return outs[n_t:]


def _sum_slots(parts, name):
    n, rows, w = parts.shape
    tb = 8

    def body(p_ref, o_ref):
        acc = p_ref[0].astype(F32)
        for s in range(1, n):
            acc = acc + p_ref[s].astype(F32)
        o_ref[...] = acc

    return pl.pallas_call(
        body, name=name, grid=(rows // tb,),
        in_specs=[pl.BlockSpec((n, tb, w), lambda i: (0, i, 0))], out_specs=pl.BlockSpec((tb, w), lambda i: (i, 0)),
        out_shape=SDS((rows, w), F32), compiler_params=_params(("parallel",)),
    )(parts)


def _adamw(wt, m, v, parts, name, own=None):
    shape = wt.shape
    parts = parts if isinstance(parts, (list, tuple)) else [parts]
    n, w = parts[0].shape[0], shape[-1]
    rows = math.prod(shape[:-1])
    per = rows // len(parts)
    step = 16 if parts[0].dtype == BF16 else 8
    tb = max([t for t in range(step, 257, step) if per % t == 0] or [per])
    nb = per // tb
    c1 = 1.0 / (1.0 - ADAM_B1 ** ADAM_STEP)
    c2 = 1.0 / (1.0 - ADAM_B2 ** ADAM_STEP)
    state = [a.reshape(rows, w) for a in (wt, m, v)]
    n_in = 4 if own is None else 5
    outs = None
    for l in reversed(range(len(parts))):
        def body(*refs):
            idx_ref = None if own is None else refs[0]
            w_ref, m_ref, v_ref, p_ref = refs[n_in - 4:n_in] if own is None else refs[1:5]
            g_out, d_out, m_out, v_out = refs[-4:]
            g = None
            for s in range(n):
                term = p_ref[s] if own is None else jnp.where(idx_ref[0] == s, refs[5][0], p_ref[s])
                g = term.astype(F32) if g is None else g + term.astype(F32)
            m_new = ADAM_B1 * m_ref[...] + (1.0 - ADAM_B1) * g
            v_new = ADAM_B2 * v_ref[...] + (1.0 - ADAM_B2) * (g * g)
            g_out[...] = g
            m_out[...] = m_new
            v_out[...] = v_new
            d_out[...] = -ADAM_LR * ((m_new * c1) / (jnp.sqrt(v_new * c2) + ADAM_EPS) + ADAM_WD * w_ref[...])

        row = pl.BlockSpec((tb, w), lambda i, *_, l=l: (l * nb + i, 0))
        in_specs = [row, row, row, pl.BlockSpec((n, tb, w), lambda i, *_: (0, i, 0))]
        args = [*state, parts[l].reshape(n, per, w)]
        if own is not None:
            in_specs.append(pl.BlockSpec((1, tb, w), lambda i, idx: (idx[0], i, 0)))
            args.append(own[l].reshape(n, per, w))
        prev = [] if outs is None else list(outs)
        in_specs += [pl.BlockSpec(memory_space=pl.ANY)] * len(prev)
        n_pre = 0 if own is None else 1
        call = dict(name=f"{name}_{l}", out_shape=[SDS((rows, w), F32)] * 4,
                    input_output_aliases={n_pre + len(args) + k: k for k in range(len(prev))},
                    compiler_params=_params(("parallel",)))
        if own is None:
            outs = pl.pallas_call(body, grid=(nb,), in_specs=in_specs, out_specs=[row] * 4, **call)(*args, *prev)
        else:
            spec = pltpu.PrefetchScalarGridSpec(num_scalar_prefetch=1, grid=(nb,), in_specs=in_specs, out_specs=[row] * 4)
            idx = jnp.reshape(_index(_place()), (1,)).astype(jnp.int32)
            outs = pl.pallas_call(body, grid_spec=spec, **call)(idx, *args, *prev)
    return [o.reshape(shape) for o in outs]


_BIG = [("w_in", 2), ("mla_w_q_up", 2), ("mla_w_kv_up", 2), ("w_branch", 3), ("w_o", 1), ("ffn_w_up", 2), ("ffn_w_down", 1)]
_SMALL_SHARDED = [("meta_tokens", 1), ("ffn_conv_w", 2)]
_REPLICATED = ["norm1_g", "fox_forget_b", "fox_q_g", "fox_k_g", "mla_q_a_g", "mla_kv_a_g", "mla_q_g", "mla_k_g",
               "swa_q_g", "swa_k_g", "swa_sinks", "norm2_g", "ffn_conv_b"]
_ORDER = ["meta_tokens", "norm1_g", "w_in", "fox_forget_b", "fox_q_g", "fox_k_g", "mla_q_a_g", "mla_w_q_up",
          "mla_kv_a_g", "mla_w_kv_up", "mla_q_g", "mla_k_g", "swa_q_g", "swa_k_g", "swa_sinks", "w_branch", "w_o",
          "norm2_g", "ffn_w_up", "ffn_conv_w", "ffn_conv_b", "ffn_w_down"]


def _flat_rows(vecs, dtype, row_mult):
    flat = jnp.concatenate([a.reshape(-1).astype(dtype) for a in vecs])
    rows = -(-flat.shape[0] // (1024 * row_mult)) * row_mult
    return jnp.pad(flat, (0, rows * 1024 - flat.shape[0])).reshape(rows, 1024)


def _unflatten(flat, shapes):
    out, off = [], 0
    for s in shapes:
        n = math.prod(s)
        out.append(flat[off:off + n].reshape(s))
        off += n
    return out


def _to_full(blocks, axis):
    moved = jnp.moveaxis(blocks, 0, axis)
    s = moved.shape
    return moved.reshape(s[:axis] + (s[axis] * s[axis + 1],) + s[axis + 2:])


def _to_blocks(full, axis):
    s = full.shape
    split = full.reshape(s[:axis] + (N_DEV, s[axis] // N_DEV) + s[axis + 1:])
    return jnp.moveaxis(split, axis, 0)


def kernel(x, meta_tokens, norm1_g, w_in, fox_forget_b, fox_q_g, fox_k_g, mla_q_a_g, mla_w_q_up, mla_kv_a_g, mla_w_kv_up, mla_q_g, mla_k_g, swa_q_g, swa_k_g, swa_sinks, w_branch, w_o, norm2_g, ffn_w_up, ffn_conv_w, ffn_conv_b, ffn_w_down, loss_target, m_meta_tokens, m_norm1_g, m_w_in, m_fox_forget_b, m_fox_q_g, m_fox_k_g, m_mla_q_a_g, m_mla_w_q_up, m_mla_kv_a_g, m_mla_w_kv_up, m_mla_q_g, m_mla_k_g, m_swa_q_g, m_swa_k_g, m_swa_sinks, m_w_branch, m_w_o, m_norm2_g, m_ffn_w_up, m_ffn_conv_w, m_ffn_conv_b, m_ffn_w_down, v_meta_tokens, v_norm1_g, v_w_in, v_fox_forget_b, v_fox_q_g, v_fox_k_g, v_mla_q_a_g, v_mla_w_q_up, v_mla_kv_a_g, v_mla_w_kv_up, v_mla_q_g, v_mla_k_g, v_swa_q_g, v_swa_k_g, v_swa_sinks, v_w_branch, v_w_o, v_norm2_g, v_ffn_w_up, v_ffn_conv_w, v_ffn_conv_b, v_ffn_w_down):
    wl = dict(zip(_ORDER, (meta_tokens, norm1_g, w_in, fox_forget_b, fox_q_g, fox_k_g, mla_q_a_g, mla_w_q_up,
                           mla_kv_a_g, mla_w_kv_up, mla_q_g, mla_k_g, swa_q_g, swa_k_g, swa_sinks, w_branch, w_o,
                           norm2_g, ffn_w_up, ffn_conv_w, ffn_conv_b, ffn_w_down)))
    ml = dict(zip(_ORDER, (m_meta_tokens, m_norm1_g, m_w_in, m_fox_forget_b, m_fox_q_g, m_fox_k_g, m_mla_q_a_g,
                           m_mla_w_q_up, m_mla_kv_a_g, m_mla_w_kv_up, m_mla_q_g, m_mla_k_g, m_swa_q_g, m_swa_k_g,
                           m_swa_sinks, m_w_branch, m_w_o, m_norm2_g, m_ffn_w_up, m_ffn_conv_w, m_ffn_conv_b,
                           m_ffn_w_down)))
    vl = dict(zip(_ORDER, (v_meta_tokens, v_norm1_g, v_w_in, v_fox_forget_b, v_fox_q_g, v_fox_k_g, v_mla_q_a_g,
                           v_mla_w_q_up, v_mla_kv_a_g, v_mla_w_kv_up, v_mla_q_g, v_mla_k_g, v_swa_q_g, v_swa_k_g,
                           v_swa_sinks, v_w_branch, v_w_o, v_norm2_g, v_ffn_w_up, v_ffn_conv_w, v_ffn_conv_b,
                           v_ffn_w_down)))
    small_sh = [n for n, _ in _SMALL_SHARDED]
    big = [n for n, _ in _BIG]
    axis_of = dict(_BIG)
    idx = _index(_place())

    def to_full(n, blocks, own=None):
        if own is not None:
            sel = (jnp.arange(N_DEV) == idx).reshape((N_DEV,) + (1,) * own.ndim)
            blocks = jnp.where(sel, own[None], blocks)
        return _to_full(blocks, axis_of[n] - 1)

    local = {(n, l): wl[n][l].astype(BF16) for n in big for l in range(DEPTH)}
    got = _gather([local[(n, 0)] for n in _MIX_BIG] + [wl[n] for n in small_sh], "gather_weights_l0_mix")
    full = {n: wl[n] for n in _REPLICATED}
    for (n, axis), blocks in zip(_SMALL_SHARDED, got[len(_MIX_BIG):]):
        full[n] = _to_full(blocks, axis)
    ready = {(n, 0): to_full(n, blocks) for n, blocks in zip(_MIX_BIG, got)}
    later = {"l0_ffn": [(n, 0) for n in _FFN_BIG], "l1": [(n, 1) for n in big]}
    states = {}
    for key, names in later.items():
        states[key], tick = _exchange_start([local[e] for e in names], "gather_weights_" + key + "_start", True, got[0])
        full["norm1_g"] = full["norm1_g"] + tick

    def fetch(l, stage, after):
        key = "l0_ffn" if l == 0 else "l1"
        if (l, stage) != (0, "mix") and key in states:
            lands = _exchange_wait(states.pop(key), after, "gather_weights_" + key + "_wait", True)
            ready.update({e: to_full(e[0], blocks, local[e]) for e, blocks in zip(later[key], lands)})
        return {n: ready[(n, l)] for n in (_MIX_BIG if stage == "mix" else _FFN_BIG)}

    blocks_of = lambda g, names: [_to_blocks(g[n], axis_of[n] - 1).astype(BF16) for n in names]
    early = {}

    def hook(l, stage, g):
        if l == DEPTH - 1 and stage == "mix":
            key, names = "l1", big
        elif l == 0:
            key, names = "l0_" + stage, (_FFN_BIG if stage == "ffn" else _MIX_BIG)
        else:
            return None
        sends = blocks_of(g, names)
        state, tick = _exchange_start(sends, "exchange_grads_" + key + "_start")
        early[key] = (names, l, sends, state)
        return tick

    loss, grad_x, grad_meta, grads = _local_step(x[0], loss_target[0], full, hook, fetch)
    result = {kind: {} for kind in ("grad", "delta", "new_m", "new_v")}
    landed, sent = {}, {}
    after = early["l0_mix"][2][0]
    for key in ("l1", "l0_ffn"):
        names, l, sends, state = early[key]
        got = _exchange_wait(state, after, "exchange_grads_" + key + "_wait")
        landed.update({(n, l): p for n, p in zip(names, got)})
        sent.update({(n, l): p for n, p in zip(names, sends)})

    def update(names):
        for n in names:
            outs = _adamw(wl[n], ml[n], vl[n], [landed[(n, l)] for l in range(DEPTH)], "adamw_" + n,
                          [sent[(n, l)] for l in range(DEPTH)])
            for kind, val in zip(result, outs):
                result[kind][n] = val

    update(_FFN_BIG)
    grads = {k: jnp.stack([grads[l][k] for l in range(DEPTH)]) for k in grads[0] if k not in big}
    grads["meta_tokens"] = grad_meta

    small_full = _REPLICATED + small_sh
    mine_small = _flat_rows([grads[n] for n in small_full] + [loss], F32, 8)
    total_small = _sum_slots(_gather([mine_small], "gather_small_grads")[0], "sum_small_grads").reshape(-1)
    pieces = _unflatten(total_small, [grads[n].shape for n in small_full] + [()])
    loss_total = pieces[-1]
    g_small = dict(zip(small_full, pieces[:-1]))
    for n, axis in _SMALL_SHARDED:
        size = wl[n].shape[axis]
        g_small[n] = lax.dynamic_slice_in_dim(g_small[n], idx * size, size, axis)
    flat = lambda d: _flat_rows([d[n] for n in small_full], F32, 8)
    small_out = _adamw(flat(wl), flat(ml), flat(vl), flat(g_small)[None], "adamw_small")
    for kind, fs in zip(result, small_out):
        result[kind].update(zip(small_full, _unflatten(fs.reshape(-1), [wl[n].shape for n in small_full])))
    names, l, sends, state = early["l0_mix"]
    got = _exchange_wait(state, small_out[0], "exchange_grads_l0_mix_wait")
    landed.update({(n, l): p for n, p in zip(names, got)})
    sent.update({(n, l): p for n, p in zip(names, sends)})
    update(_MIX_BIG)
    outs = [loss_total, grad_x[None]]
    for kind in ("grad", "delta", "new_m", "new_v"):
        outs += [result[kind][n] for n in _ORDER]
    return tuple(outs)
```

```python
import functools
import math

import numpy as np
import jax
import jax.numpy as jnp
from jax import lax
from jax.experimental import pallas as pl
from jax.experimental.pallas import tpu as pltpu

F32, BF16 = jnp.float32, jnp.bfloat16
SDS = jax.ShapeDtypeStruct

D_MODEL = 1024
N_META = 16
EPS = 1e-6
WINDOW = 128
ROPE_THETA = 10000.0
HEADS = 8
D_FF = 2816
DEPTH = 2
N_DEV = 8
ADAM_LR, ADAM_B1, ADAM_B2, ADAM_EPS, ADAM_WD, ADAM_STEP = 0.001, 0.9, 0.999, 1e-08, 0.01, 10

ROW_ALIGN = 384
TILE_MM = 384
TILE_ROW = 192
TILE_ATT = 384
TILE_POST = 384
PAIRS = 2
VMEM_LIMIT = 56 * 1024 * 1024

GATES_W = 3072
OTHER_W = 2816
IN_W = GATES_W + OTHER_W
O_FQ, O_FK, O_FV, O_SQ, O_SK, O_SV, O_CQ, O_CKV, O_MISC = 0, 512, 1024, 1536, 2048, 2176, 2304, 2560, 2688
FF_LANE = 32

NEG = -1e30


def _dot(a, b):
    return jnp.dot(a, b, preferred_element_type=F32)


def _dot_nt(a, b):
    return lax.dot_general(a, b, (((1,), (1,)), ((), ())), preferred_element_type=F32)


def _dot_tn(a, b):
    return lax.dot_general(a, b, (((0,), (0,)), ((), ())), preferred_element_type=F32)


def _params(sem):
    return pltpu.CompilerParams(dimension_semantics=sem, vmem_limit_bytes=VMEM_LIMIT)


def _rms(x, g):
    return x * lax.rsqrt(jnp.mean(x * x, axis=-1, keepdims=True) + EPS) * g


def _split_dot(x, m, pieces=2):
    acc, rest = None, x
    for _ in range(pieces):
        part = rest.astype(BF16)
        rest = rest - part.astype(F32)
        acc = _dot(part, m) if acc is None else acc + _dot(part, m)
    return acc


@jax.custom_vjp
def _sel(x, m, mt):
    return _split_dot(x, m)


_sel.defvjp(lambda x, m, mt: (_split_dot(x, m), (m, mt)), lambda res, dy: (_split_dot(dy, res[1]), None, None))


@jax.custom_vjp
def _mm(x, w):
    return _dot(x.astype(BF16), w.astype(BF16))


def _mm_bwd(res, dy):
    x, w = res
    dyb = dy.astype(BF16)
    return _dot_nt(dyb, w.astype(BF16)), _dot_tn(x.astype(BF16), dyb)


_mm.defvjp(lambda x, w: (_mm(x, w), (x, w)), _mm_bwd)


def _rot_impl(x):
    w = x.shape[1]
    lane = lax.broadcasted_iota(jnp.int32, x.shape, 1) % 128
    lo = (lane >= 64) & (lane < 80)
    hi = (lane >= 80) & (lane < 96)
    return jnp.where(hi, pltpu.roll(x, 16, 1), 0.0) - jnp.where(lo, pltpu.roll(x, w - 16, 1), 0.0)


@jax.custom_vjp
def _rot(x):
    return _rot_impl(x)


_rot.defvjp(lambda x: (_rot_impl(x), None), lambda _, dy: (-_rot_impl(dy),))


def _gnorm(x, g, e, et, dim):
    inv = lax.rsqrt(_sel(x * x, e, et) * (1.0 / dim) + EPS)
    return x * _sel(inv, et, e) * g


def _indicator(width, period):
    m = np.zeros((width, 128), np.float32)
    m[np.arange(width), np.arange(width) // period] = 1.0
    return m


def _consts():
    e64 = _indicator(512, 64)
    e128 = _indicator(1024, 128)
    sk = np.zeros((128, 1024), np.float32)
    for h in range(HEADS):
        sk[np.arange(32), 128 * h + 64 + np.arange(32)] = 1.0
    dup = np.zeros((128, 256), np.float32)
    for g in range(2):
        for r in range(2):
            dup[64 * g + np.arange(64), 128 * g + 64 * r + np.arange(64)] = 1.0
    mats = [e64, e64.T, e128, e128.T, sk, sk.T, dup, dup.T]
    return [jnp.asarray(m, BF16) for m in mats]


def _fold_matrix(width, period):
    m = np.zeros((width, 128), np.float32)
    m[np.arange(width), np.arange(width) % period] = 1.0
    return jnp.asarray(m, BF16)


def _rope_tables(lp):
    half = 16
    freqs = ROPE_THETA ** (-np.arange(half, dtype=np.float32) / half)
    ang = np.arange(lp, dtype=np.float32)[:, None] * freqs[None, :]
    cos = np.ones((lp, 128), np.float32)
    sin = np.zeros((lp, 128), np.float32)
    cos[:, 64:80] = np.cos(ang)
    cos[:, 80:96] = np.cos(ang)
    sin[:, 64:80] = np.sin(ang)
    sin[:, 80:96] = np.sin(ang)
    return jnp.asarray(cos), jnp.asarray(sin)


def _norm_matmul(h, g, w, tn, name):
    lp, d = h.shape
    n = w.shape[1]
    tb = TILE_MM

    def body(h_ref, g_ref, w_ref, xn_ref, y_ref):
        @pl.when(pl.program_id(1) == 0)
        def _():
            xn_ref[...] = _rms(h_ref[...], g_ref[...]).astype(BF16)

        y_ref[...] = _dot(xn_ref[...], w_ref[...])

    return pl.pallas_call(
        body, name=name, grid=(lp // tb, n // tn),
        in_specs=[pl.BlockSpec((tb, d), lambda i, j: (i, 0)), pl.BlockSpec((1, d), lambda i, j: (0, 0)),
                  pl.BlockSpec((d, tn), lambda i, j: (0, j))],
        out_specs=[pl.BlockSpec((tb, d), lambda i, j: (i, 0)), pl.BlockSpec((tb, tn), lambda i, j: (i, j))],
        out_shape=[SDS((lp, d), BF16), SDS((lp, n), F32)],
        compiler_params=_params(("parallel", "arbitrary")),
    )(h, g, w)


def _matmul_residual(a, w, res, name):
    m, k = a.shape
    n = w.shape[1]
    tb = TILE_MM

    def body(a_ref, w_ref, r_ref, o_ref):
        o_ref[...] = r_ref[...] + _dot(a_ref[...], w_ref[...])

    return pl.pallas_call(
        body, name=name, grid=(m // tb,),
        in_specs=[pl.BlockSpec((tb, k), lambda i: (i, 0)), pl.BlockSpec((k, n), lambda i: (0, 0)),
                  pl.BlockSpec((tb, n), lambda i: (i, 0))],
        out_specs=pl.BlockSpec((tb, n), lambda i: (i, 0)),
        out_shape=SDS((m, n), F32),
        compiler_params=_params(("parallel",)),
    )(a, w, res)


def _matmul_nt(dy, w, tn, name):
    m, k = dy.shape
    n = w.shape[0]
    tb = TILE_MM

    def body(dy_ref, w_ref, o_ref):
        o_ref[...] = _dot_nt(dy_ref[...].astype(BF16), w_ref[...])

    return pl.pallas_call(
        body, name=name, grid=(m // tb, n // tn),
        in_specs=[pl.BlockSpec((tb, k), lambda i, j: (i, 0)), pl.BlockSpec((tn, k), lambda i, j: (j, 0))],
        out_specs=pl.BlockSpec((tb, tn), lambda i, j: (i, j)),
        out_shape=SDS((m, n), F32),
        compiler_params=_params(("parallel", "arbitrary")),
    )(dy, w)


def _matmul_tn(x, dy, tn, name):
    m, k = x.shape
    n = dy.shape[1]
    tb = TILE_MM
    nb = m // tb

    def body(x_ref, dy_ref, o_ref, acc):
        i = pl.program_id(1)

        @pl.when(i == 0)
        def _():
            acc[...] = jnp.zeros_like(acc)

        acc[...] += _dot_tn(x_ref[...].astype(BF16), dy_ref[...].astype(BF16))

        @pl.when(i == nb - 1)
        def _():
            o_ref[...] = acc[...].astype(BF16)

    return pl.pallas_call(
        body, name=name, grid=(n // tn, nb),
        in_specs=[pl.BlockSpec((tb, k), lambda j, i: (i, 0)), pl.BlockSpec((tb, tn), lambda j, i: (i, j))],
        out_specs=pl.BlockSpec((k, tn), lambda j, i: (0, j)),
        out_shape=SDS((k, n), BF16),
        scratch_shapes=[pltpu.VMEM((k, tn), F32)],
        compiler_params=_params(("parallel", "arbitrary")),
    )(x, dy)


def _norm_matmul_bwd(dys, w, x, g, dres, name):
    m, d = x.shape
    tb = TILE_MM
    widths = [a.shape[1] for a in dys]
    n_dy = len(dys)

    def body(*refs):
        w_ref, x_ref, g_ref, r_ref, o_ref, dg_ref = refs[n_dy:]

        @pl.when(pl.program_id(0) == 0)
        def _():
            dg_ref[...] = jnp.zeros_like(dg_ref)

        dxn, off = None, 0
        for dy_ref, width in zip(refs[:n_dy], widths):
            part = _dot_nt(dy_ref[...], w_ref[:, off:off + width])
            dxn = part if dxn is None else dxn + part
            off += width
        _, vjp = jax.vjp(_rms, x_ref[...], g_ref[...])
        dx, dg = vjp(dxn)
        o_ref[...] = r_ref[...] + dx
        dg_ref[...] += dg

    row = pl.BlockSpec((tb, d), lambda i: (i, 0))
    vec = pl.BlockSpec((1, d), lambda i: (0, 0))
    return pl.pallas_call(
        body, name=name, grid=(m // tb,),
        in_specs=[pl.BlockSpec((tb, wd), lambda i: (i, 0)) for wd in widths]
        + [pl.BlockSpec(w.shape, lambda i: (0, 0)), row, vec, row],
        out_specs=[row, vec],
        out_shape=[SDS((m, d), F32), SDS((1, d), F32)],
        compiler_params=_params(("arbitrary",)),
    )(*dys, w, x, g, dres)


def _prep_math(pieces, prm, consts, cos, sin):
    fq, fk, sq, sk, sv, cq, ckv, misc = pieces
    gfq, gfk, gsq, gsk, fb, gqa, gkva, gmq, gmk, wq, wkk, wkv = prm
    e64, e64t, e128, e128t, skm, skt, dup, dupt = consts
    cos8 = jnp.concatenate([cos] * HEADS, axis=1)
    sin8 = jnp.concatenate([sin] * HEADS, axis=1)
    fq_n = _gnorm(fq, gfq, e64, e64t, 64)
    fk_n = _gnorm(fk, gfk, e64, e64t, 64)
    ls = jax.nn.log_sigmoid(misc + fb)
    q = _gnorm(_mm(_rms(cq, gqa), wq), gmq, e128, e128t, 96)
    mq = q * cos8 + _rot(q) * sin8
    kva = _rms(ckv, gkva)
    k = _gnorm(_mm(kva, wkk) + _sel(misc, skm, skt), gmk, e128, e128t, 96)
    mk = k * cos8 + _rot(k) * sin8
    mv = _mm(kva, wkv)
    sq_n = _gnorm(sq, gsq, e64, e64t, 64)
    sk_n = _gnorm(sk, gsk, e64[0:128], e64t[:, 0:128], 64)
    skd = _sel(sk_n, dup, dupt)
    svd = _sel(sv, dup, dupt)
    return fq_n, fk_n, ls, mq, mk, mv, sq_n, skd, svd


_PIECES = [(O_FQ, 512), (O_FK, 512), (O_SQ, 512), (O_SK, 128), (O_SV, 128), (O_CQ, 256), (O_CKV, 128), (O_MISC, 128)]
_PRM_SHAPES = [(1, 512), (1, 512), (1, 512), (1, 128), (1, 128), (1, 256), (1, 128), (1, 1024), (1, 1024),
               (256, 1024), (128, 1024), (128, 512)]
_CONST_SHAPES = [(512, 128), (128, 512), (1024, 128), (128, 1024), (128, 1024), (1024, 128), (128, 256), (256, 128)]


def _piece_specs(tb):
    def spec(off, width):
        blk = (GATES_W + off) // width
        return pl.BlockSpec((tb, width), lambda i, blk=blk: (i, blk))
    return [spec(o, w) for o, w in _PIECES] + [spec(O_FV, 512)]


def _full_specs(shapes):
    return [pl.BlockSpec(s, lambda i: (0, 0)) for s in shapes]


def _prep_fwd(proj, prm, consts, cos, sin, name):
    lp = proj.shape[0]
    tb = TILE_ROW
    row = lambda w: pl.BlockSpec((tb, w), lambda i: (i, 0))

    def body(*refs):
        pieces = [r[...] for r in refs[0:8]]
        fv = refs[8][...]
        prm_v = [r[...] for r in refs[9:21]]
        consts_v = [r[...] for r in refs[21:29]]
        cos_v, sin_v = refs[29][...], refs[30][...]
        outs = refs[31:]
        fq_n, fk_n, ls, mq, mk, mv, sq_n, skd, svd = _prep_math(pieces, prm_v, consts_v, cos_v, sin_v)
        for ref, val in zip(outs, (fq_n, fk_n, fv, mq, mk, mv, sq_n, skd, svd)):
            ref[...] = val.astype(BF16)
        outs[9][...] = ls

    widths = [512, 512, 512, 1024, 1024, 512, 512, 256, 256]
    return pl.pallas_call(
        body, name=name, grid=(lp // tb,),
        in_specs=_piece_specs(tb) + _full_specs(_PRM_SHAPES) + _full_specs(_CONST_SHAPES) + [row(128), row(128)],
        out_specs=[row(w) for w in widths] + [row(128)],
        out_shape=[SDS((lp, w), BF16) for w in widths] + [SDS((lp, 128), F32)],
        compiler_params=_params(("parallel",)),
    )(*([proj] * 9), *prm, *consts, cos, sin)


def _prep_bwd(proj, prm, consts, cos, sin, cots, folds, name):
    lp = proj.shape[0]
    tb = TILE_ROW
    row = lambda w: pl.BlockSpec((tb, w), lambda i: (i, 0))
    fold64, fold128 = folds

    def body(*refs):
        pieces = [r[...] for r in refs[0:8]]
        prm_v = [r[...] for r in refs[9:21]]
        consts_v = [r[...] for r in refs[21:29]]
        cos_v, sin_v = refs[29][...], refs[30][...]
        dfq, dfk, dfv, dmq, dmk, dmv, dsq, dskp, dsvp, dls = [r[...] for r in refs[31:41]]
        f64, f128 = refs[41][...], refs[42][...]
        d_ref = refs[43]
        g_refs = refs[44:]

        @pl.when(pl.program_id(0) == 0)
        def _():
            for r in g_refs:
                r[...] = jnp.zeros_like(r)

        def pair_sum(p):
            return jnp.concatenate([p[:, 0:128] + p[:, 128:256], p[:, 256:384] + p[:, 384:512]], axis=1)

        f = lambda pc, pr: _prep_math(pc, pr, consts_v, cos_v, sin_v)
        _, vjp = jax.vjp(f, pieces, prm_v)
        dpc, dprm = vjp((dfq, dfk, dls, dmq, dmk, dmv, dsq, pair_sum(dskp), pair_sum(dsvp)))
        d_fq, d_fk, d_sq, d_sk, d_sv, d_cq, d_ckv, d_misc = dpc
        for off, val in ((O_FQ, d_fq), (O_FK, d_fk), (O_FV, dfv), (O_SQ, d_sq), (O_SK, d_sk), (O_SV, d_sv),
                         (O_CQ, d_cq), (O_CKV, d_ckv), (O_MISC, d_misc)):
            d_ref[:, off:off + val.shape[1]] = val.astype(BF16)
        folded = {0: f64, 1: f64, 2: f64, 3: f64[0:128], 7: f128, 8: f128}
        for idx, (ref, val) in enumerate(zip(g_refs, dprm)):
            if idx in folded:
                ref[...] += _split_dot(jnp.broadcast_to(val, (8, val.shape[1])), folded[idx], 3)
            elif val.shape[0] == 1:
                ref[...] += jnp.broadcast_to(val, ref.shape)
            else:
                ref[...] += val

    g_shapes = [(8, 128), (8, 128), (8, 128), (8, 128), (8, 128), (8, 256), (8, 128), (8, 128), (8, 128),
                (256, 1024), (128, 1024), (128, 512)]
    cot_widths = [512, 512, 512, 1024, 1024, 512, 512, 512, 512, 128]
    return pl.pallas_call(
        body, name=name, grid=(lp // tb,),
        in_specs=(_piece_specs(tb) + _full_specs(_PRM_SHAPES) + _full_specs(_CONST_SHAPES) + [row(128), row(128)]
                  + [row(w) for w in cot_widths] + _full_specs([(512, 128), (1024, 128)])),
        out_specs=[row(OTHER_W)] + _full_specs(g_shapes),
        out_shape=[SDS((lp, OTHER_W), BF16)] + [SDS(s, F32) for s in g_shapes],
        compiler_params=_params(("arbitrary",)),
    )(*([proj] * 9), *prm, *consts, cos, sin, *cots, fold64, fold128)


def _cumsum(xs, reverse, name):
    lp = xs[0].shape[0]
    tb = TILE_MM
    nb = lp // tb
    n_in = len(xs)
    idx = (lambda i: (nb - 1 - i, 0)) if reverse else (lambda i: (i, 0))

    def body(*refs):
        o_ref, carry = refs[n_in], refs[n_in + 1]

        @pl.when(pl.program_id(0) == 0)
        def _():
            carry[...] = jnp.zeros_like(carry)

        x = refs[0][...]
        for r in refs[1:n_in]:
            x = x + r[...]
        r_i = lax.broadcasted_iota(jnp.int32, (tb, tb), 0)
        c_i = lax.broadcasted_iota(jnp.int32, (tb, tb), 1)
        tri = ((c_i >= r_i) if reverse else (c_i <= r_i)).astype(BF16)
        acc, rest = None, x
        for _ in range(3):
            part = rest.astype(BF16)
            rest = rest - part.astype(F32)
            acc = _dot(tri, part) if acc is None else acc + _dot(tri, part)
        o_ref[...] = acc + carry[...]
        carry[...] += jnp.sum(x, axis=0, keepdims=True)

    return pl.pallas_call(
        body, name=name, grid=(nb,),
        in_specs=[pl.BlockSpec((tb, 128), idx)] * n_in,
        out_specs=pl.BlockSpec((tb, 128), idx),
        out_shape=SDS((lp, 128), F32),
        scratch_shapes=[pltpu.VMEM((1, 128), F32)],
        compiler_params=_params(("arbitrary",)),
    )(*xs)


class _Att:
    def __init__(self, mode):
        self.mode = mode
        self.wide = mode == "mla"
        self.qw = 256 if self.wide else 128
        self.scale = (96 if mode == "mla" else 64) ** -0.5

    def resident(self, x, lo, scaled):
        if self.wide:
            return x[:, 0:128], x[:, 128:256]
        if scaled:
            x = x * jnp.asarray(self.scale, x.dtype)
        zero = jnp.zeros_like(x)
        return jnp.where(lo, x, zero), jnp.where(lo, zero, x)

    def moving(self, x):
        return (x[:, 0:128], x[:, 128:256]) if self.wide else (x, x)

    def logits(self, a, b, qpos, kpos, key_decay, slope, masked):
        s = _dot_nt(a, b)
        if self.wide:
            s = s * self.scale
        if self.mode == "fox":
            s = s - key_decay
        if self.mode == "swa":
            s = s - slope * (qpos - kpos).astype(F32)
        if masked:
            ok = kpos <= qpos
            if self.mode == "swa":
                ok = ok & ((kpos < N_META) | (qpos - kpos < WINDOW))
            s = jnp.where(ok, s, NEG)
        return s


def _as_rows(col):
    return jnp.broadcast_to(col, (col.shape[0], 128)).T[0:8, :]


def _halves(x, lo):
    zero = jnp.zeros_like(x)
    return jnp.where(lo, x, zero), jnp.where(lo, zero, x)


def _kv_specs(att, lp, rows):
    if att.mode == "swa":
        return (pl.BlockSpec((rows, 128), lambda g, i: (i if rows != lp else 0, g)),) * 2
    return (pl.BlockSpec((rows, PAIRS * att.qw), lambda g, i: (i if rows != lp else 0, g)),
            pl.BlockSpec((rows, PAIRS * 128), lambda g, i: (i if rows != lp else 0, g)))


def _pair_cols(att, x, pp, width):
    return x if x.shape[1] == width else x[:, pp * width:(pp + 1) * width]


def _att_fwd(att, q, k, v, extra, name):
    lp = q.shape[0]
    t = TILE_ATT
    nq = lp // t
    qw = att.qw
    mode = att.mode
    nh = 2 * PAIRS

    def body(*refs):
        q_ref, k_ref, v_ref = refs[0:3]
        o_ref, lse_ref, row_ref = refs[-3:]
        g, qi = pl.program_id(0), pl.program_id(1)
        lo = lax.broadcasted_iota(jnp.int32, (1, 128), 1) < 64
        q_all = q_ref[...]
        q_heads = [h for pp in range(PAIRS) for h in att.resident(_pair_cols(att, q_all, pp, qw), lo, True)]
        qpos = qi * t + lax.broadcasted_iota(jnp.int32, (t, 1), 0)

        def step(first, cols, carry, masked):
            ks = pl.multiple_of(first, 128)
            kc, vc = k_ref[pl.ds(ks, cols), :], v_ref[pl.ds(ks, cols), :]
            kpos = first + lax.broadcasted_iota(jnp.int32, (1, cols), 1)
            out = []
            for h in range(nh):
                pp = h // 2
                m, l, acc = carry[3 * h:3 * h + 3]
                k_h = att.moving(_pair_cols(att, kc, pp, qw))[h % 2]
                decay = refs[3][h, :, pl.ds(ks, cols)] if mode == "fox" else None
                slope = refs[4][nh * g + h] if mode == "swa" else None
                s = att.logits(q_heads[h], k_h, qpos, kpos, decay, slope, masked)
                m_new = jnp.maximum(m, jnp.max(s, axis=-1, keepdims=True))
                alpha = jnp.exp(m - m_new)
                pe = jnp.exp(s - m_new)
                l = alpha * l + jnp.sum(pe, axis=-1, keepdims=True)
                acc = alpha * acc + _dot(pe.astype(BF16), _pair_cols(att, vc, pp, 128))
                out += [m_new, l, acc]
            return tuple(out)

        init = []
        for h in range(nh):
            if mode == "swa":
                init += [jnp.full((t, 1), refs[3][nh * g + h], F32), jnp.ones((t, 1), F32)]
            else:
                init += [jnp.full((t, 1), NEG, F32), jnp.zeros((t, 1), F32)]
            init.append(jnp.zeros((t, 128), F32))
        if mode == "swa":
            band = jnp.maximum(qi * t - WINDOW, 0)
            carry = lax.fori_loop(0, (band >= 128).astype(jnp.int32), lambda j, c: step(0, 128, c, True), tuple(init))
            carry = step(band, t + WINDOW, carry, True)
        else:
            carry = lax.fori_loop(0, qi // 2, lambda j, c: step(2 * j * t, 2 * t, c, False), tuple(init))
            carry = lax.fori_loop(0, qi % 2, lambda j, c: step((qi - 1) * t, t, c, False), carry)
            carry = step(qi * t, t, carry, True)
        outs = []
        for pp in range(PAIRS):
            (ma, la, acca), (mb, lb, accb) = carry[6 * pp:6 * pp + 3], carry[6 * pp + 3:6 * pp + 6]
            outs.append(jnp.where(lo, acca / la, accb / lb).astype(BF16))
            for h, lse in ((2 * pp, ma + jnp.log(la)), (2 * pp + 1, mb + jnp.log(lb))):
                lse_ref[h] = lse
                row_ref[h] = _as_rows(lse)
        o_ref[...] = jnp.concatenate(outs, axis=1)

    in_specs = [pl.BlockSpec((t, PAIRS * qw), lambda g, i: (i, g)), *_kv_specs(att, lp, lp)]
    if mode == "fox":
        in_specs += [pl.BlockSpec((nh, 1, lp), lambda g, i: (g, 0, 0))]
    if mode == "swa":
        in_specs += [pl.BlockSpec(memory_space=pltpu.SMEM)] * 2
    return pl.pallas_call(
        body, name=name, grid=(4 // PAIRS, nq), in_specs=in_specs,
        out_specs=[pl.BlockSpec((t, PAIRS * 128), lambda g, i: (i, g)), pl.BlockSpec((nh, t, 1), lambda g, i: (g, i, 0)),
                   pl.BlockSpec((nh, 8, t), lambda g, i: (g, 0, i))],
        out_shape=[SDS((lp, 512), BF16), SDS((HEADS, lp, 1), F32), SDS((HEADS, 8, lp), F32)],
        compiler_params=_params(("parallel", "arbitrary")),
    )(q, k, v, *extra)


def _att_dq(att, q, k, v, o, do, lse, extra, name):
    lp = q.shape[0]
    t = TILE_ATT
    nq = lp // t
    qw = att.qw
    mode = att.mode
    nh = 2 * PAIRS

    def body(*refs):
        q_ref, k_ref, v_ref, o_ref, do_ref, lse_ref = refs[0:6]
        n_out = 3 if mode == "mla" else 4
        outs = refs[len(refs) - n_out:]
        dq_ref, delta_ref, row_ref = outs[0:3]
        outs = outs[1:]
        g, qi = pl.program_id(0), pl.program_id(1)
        lo = lax.broadcasted_iota(jnp.int32, (1, 128), 1) < 64
        q_all, do_all = q_ref[...], do_ref[...]
        prod = do_all.astype(F32) * o_ref[...].astype(F32)
        q_heads, do_heads, delta = [], [], []
        for pp in range(PAIRS):
            q_heads += att.resident(_pair_cols(att, q_all, pp, qw), lo, True)
            do_heads += _halves(_pair_cols(att, do_all, pp, 128), lo)
            pr_pp = _pair_cols(att, prod, pp, 128)
            delta += [jnp.sum(jnp.where(lo, pr_pp, 0.0), axis=-1, keepdims=True),
                      jnp.sum(jnp.where(lo, 0.0, pr_pp), axis=-1, keepdims=True)]
        lse_v = [lse_ref[h] for h in range(nh)]
        qpos = qi * t + lax.broadcasted_iota(jnp.int32, (t, 1), 0)

        def step(first, cols, carry, masked):
            ks = pl.multiple_of(first, 128)
            kc, vc = k_ref[pl.ds(ks, cols), :], v_ref[pl.ds(ks, cols), :]
            kpos = first + lax.broadcasted_iota(jnp.int32, (1, cols), 1)
            out = []
            for h in range(nh):
                pp = h // 2
                k_h = att.moving(_pair_cols(att, kc, pp, qw))[h % 2]
                decay = refs[6][h, :, pl.ds(ks, cols)] if mode == "fox" else None
                slope = refs[7][nh * g + h] if mode == "swa" else None
                s = att.logits(q_heads[h], k_h, qpos, kpos, decay, slope, masked)
                pr = jnp.exp(s - lse_v[h])
                ds = pr * (_dot_nt(do_heads[h], _pair_cols(att, vc, pp, 128)) - delta[h])
                out.append(carry[2 * h] + _dot(ds.astype(BF16), k_h))
                out.append(carry[2 * h + 1] + jnp.sum(ds, axis=-1, keepdims=True) if mode == "fox" else carry[2 * h + 1])
            return tuple(out)

        init = (jnp.zeros((t, 128), F32), jnp.zeros((t, 1), F32)) * nh
        if mode == "swa":
            band = jnp.maximum(qi * t - WINDOW, 0)
            carry = lax.fori_loop(0, (band >= 128).astype(jnp.int32), lambda j, c: step(0, 128, c, True), init)
            carry = step(band, t + WINDOW, carry, True)
        else:
            carry = lax.fori_loop(0, qi // 2, lambda j, c: step(2 * j * t, 2 * t, c, False), init)
            carry = lax.fori_loop(0, qi % 2, lambda j, c: step((qi - 1) * t, t, c, False), carry)
            carry = step(qi * t, t, carry, True)
        dq = []
        for pp in range(PAIRS):
            dqa, dca, dqb, dcb = carry[4 * pp:4 * pp + 4]
            dq += [dqa, dqb] if att.wide else [jnp.where(lo, dqa, dqb)]
            if mode == "fox":
                outs[2][2 * pp] = dca
                outs[2][2 * pp + 1] = dcb
        dq_ref[...] = jnp.concatenate(dq, axis=1) * att.scale
        for h in range(nh):
            delta_ref[h] = delta[h]
            row_ref[h] = _as_rows(delta[h])
        if mode == "swa":
            ds_ref = outs[2]

            @pl.when(qi == 0)
            def _():
                ds_ref[...] = jnp.zeros_like(ds_ref)

            lane = lax.broadcasted_iota(jnp.int32, (8, 128), 1)
            acc = jnp.zeros((8, 128), F32)
            for h in range(nh):
                tot = -jnp.sum(jnp.exp(refs[6][nh * g + h] - lse_v[h]) * delta[h])
                acc = acc + jnp.where(lane == h, tot, 0.0)
            ds_ref[0] += acc

    col = pl.BlockSpec((nh, t, 1), lambda g, i: (g, i, 0))
    in_specs = [pl.BlockSpec((t, PAIRS * qw), lambda g, i: (i, g)), *_kv_specs(att, lp, lp),
                pl.BlockSpec((t, PAIRS * 128), lambda g, i: (i, g)), pl.BlockSpec((t, PAIRS * 128), lambda g, i: (i, g)), col]
    out_specs = [pl.BlockSpec((t, PAIRS * qw), lambda g, i: (i, g)), col, pl.BlockSpec((nh, 8, t), lambda g, i: (g, 0, i))]
    out_shape = [SDS((lp, 4 * qw), F32), SDS((HEADS, lp, 1), F32), SDS((HEADS, 8, lp), F32)]
    if mode == "fox":
        in_specs += [pl.BlockSpec((nh, 1, lp), lambda g, i: (g, 0, 0))]
        out_specs.append(col)
        out_shape.append(SDS((HEADS, lp, 1), F32))
    if mode == "swa":
        in_specs += [pl.BlockSpec(memory_space=pltpu.SMEM)] * 2
        out_specs.append(pl.BlockSpec((1, 8, 128), lambda g, i: (g, 0, 0)))
        out_shape.append(SDS((4 // PAIRS, 8, 128), F32))
    return pl.pallas_call(
        body, name=name, grid=(4 // PAIRS, nq), in_specs=in_specs, out_specs=out_specs, out_shape=out_shape,
        compiler_params=_params(("parallel", "arbitrary")),
    )(q, k, v, o, do, lse, *extra)


def _att_dkv(att, q, k, v, do, lse_row, delta_row, extra, name):
    lp = q.shape[0]
    t = TILE_ATT
    nq = lp // t
    qw = att.qw
    mode = att.mode
    nh = 2 * PAIRS

    def body(*refs):
        q_ref, k_ref, v_ref, do_ref, lse_ref, delta_ref = refs[0:6]
        n_out = 3 if mode == "fox" else 2
        outs = refs[len(refs) - n_out:]
        dk_ref, dv_ref = outs[0:2]
        g, kj = pl.program_id(0), pl.program_id(1)
        lo = lax.broadcasted_iota(jnp.int32, (1, 128), 1) < 64
        k_all, v_all = k_ref[...], v_ref[...]
        k_heads, v_heads = [], []
        for pp in range(PAIRS):
            k_heads += att.resident(_pair_cols(att, k_all, pp, qw), lo, True)
            v_heads += _halves(_pair_cols(att, v_all, pp, 128), lo)
        kpos = kj * t + lax.broadcasted_iota(jnp.int32, (t, 1), 0)

        def step(first, cols, carry, masked):
            qs = pl.multiple_of(first, 128)
            qc, doc = q_ref[pl.ds(qs, cols), :], do_ref[pl.ds(qs, cols), :]
            qpos = first + lax.broadcasted_iota(jnp.int32, (1, cols), 1)
            out = []
            for h in range(nh):
                pp = h // 2
                dk_acc, dv_acc, dc_acc = carry[3 * h:3 * h + 3]
                q_h = att.moving(_pair_cols(att, qc, pp, qw))[h % 2]
                do_h = _pair_cols(att, doc, pp, 128)
                decay = refs[6][h] if mode == "fox" else None
                slope = refs[6][nh * g + h] if mode == "swa" else None
                st = att.logits(k_heads[h], q_h, qpos, kpos, decay, slope, masked)
                pt = jnp.exp(st - lse_ref[h, 0:1, pl.ds(qs, cols)])
                dst = pt * (_dot_nt(v_heads[h], do_h) - delta_ref[h, 0:1, pl.ds(qs, cols)])
                dv_acc = dv_acc + _dot(pt.astype(BF16), do_h)
                dk_acc = dk_acc + _dot(dst.astype(BF16), q_h)
                if mode == "fox":
                    dc_acc = dc_acc - jnp.sum(dst, axis=-1, keepdims=True)
                out += [dk_acc, dv_acc, dc_acc]
            return tuple(out)

        init = (jnp.zeros((t, 128), F32), jnp.zeros((t, 128), F32), jnp.zeros((t, 1), F32)) * nh
        if mode == "swa":
            carry = lax.fori_loop(0, jnp.where(kj == 0, nq, 0), lambda qi, c: step(qi * t, t, c, True), init)
            near = jnp.minimum(kj * t, lp - (t + WINDOW))
            carry = lax.fori_loop(0, (kj > 0).astype(jnp.int32), lambda j, c: step(near, t + WINDOW, c, True), carry)
        else:
            carry = step(kj * t, t, init, True)
            rest = nq - 1 - kj
            carry = lax.fori_loop(0, rest // 2, lambda j, c: step((kj + 1 + 2 * j) * t, 2 * t, c, False), carry)
            carry = lax.fori_loop(0, rest % 2, lambda j, c: step((nq - 1) * t, t, c, False), carry)
        dk, dv = [], []
        for pp in range(PAIRS):
            dka, dva, dca, dkb, dvb, dcb = carry[6 * pp:6 * pp + 6]
            dk += [dka, dkb] if att.wide else [jnp.where(lo, dka, dkb)]
            dv.append(jnp.where(lo, dva, dvb))
            if mode == "fox":
                outs[2][2 * pp] = dca
                outs[2][2 * pp + 1] = dcb
        dk_ref[...] = jnp.concatenate(dk, axis=1) * att.scale
        dv_ref[...] = jnp.concatenate(dv, axis=1)

    rowv = pl.BlockSpec((nh, 8, lp), lambda g, j: (g, 0, 0))
    col = pl.BlockSpec((nh, t, 1), lambda g, j: (g, j, 0))
    in_specs = [pl.BlockSpec((lp, PAIRS * qw), lambda g, j: (0, g)), *_kv_specs(att, lp, t),
                pl.BlockSpec((lp, PAIRS * 128), lambda g, j: (0, g)), rowv, rowv]
    out_specs = [pl.BlockSpec((t, PAIRS * qw), lambda g, j: (j, g)), pl.BlockSpec((t, PAIRS * 128), lambda g, j: (j, g))]
    out_shape = [SDS((lp, 4 * qw), F32), SDS((lp, 512), F32)]
    if mode == "fox":
        in_specs += [col]
        out_specs.append(col)
        out_shape.append(SDS((HEADS, lp, 1), F32))
    if mode == "swa":
        in_specs += [pl.BlockSpec(memory_space=pltpu.SMEM)]
    return pl.pallas_call(
        body, name=name, grid=(4 // PAIRS, nq), in_specs=in_specs, out_specs=out_specs, out_shape=out_shape,
        compiler_params=_params(("parallel", "arbitrary")),
    )(q, k, v, do, lse_row, delta_row, *extra)


def _post_fwd(h, proj, outs, wb, wo, name):
    lp, d = h.shape
    tb = TILE_POST
    row = lambda w: pl.BlockSpec((tb, w), lambda i: (i, 0))

    def body(h_ref, g0, g1, g2, oa, ob, oc, wb_ref, wo_ref, o_ref):
        merged = jnp.zeros((tb, d), F32)
        for n, (g_ref, br) in enumerate(((g0, oa), (g1, ob), (g2, oc))):
            merged = merged + jax.nn.sigmoid(g_ref[...]) * _dot(br[...], wb_ref[n])
        o_ref[...] = h_ref[...] + _dot(merged.astype(BF16), wo_ref[...])

    gate = lambda n: pl.BlockSpec((tb, d), lambda i, n=n: (i, n))
    return pl.pallas_call(
        body, name=name, grid=(lp // tb,),
        in_specs=[row(d), gate(0), gate(1), gate(2), row(512), row(512), row(512),
                  pl.BlockSpec((3, 512, d), lambda i: (0, 0, 0)), pl.BlockSpec((d, d), lambda i: (0, 0))],
        out_specs=row(d), out_shape=SDS((lp, d), F32),
        compiler_params=_params(("parallel",)),
    )(h, proj, proj, proj, *outs, wb, wo)


def _post_bwd(dh, proj, outs, wb, wo, name):
    lp, d = dh.shape
    tb = TILE_POST
    row = lambda w: pl.BlockSpec((tb, w), lambda i: (i, 0))

    def body(dh_ref, g0, g1, g2, oa, ob, oc, wb_ref, wo_ref, dg_ref, doa, dob, doc, dwb_ref, dwo_ref):
        @pl.when(pl.program_id(0) == 0)
        def _():
            dwb_ref[...] = jnp.zeros_like(dwb_ref)
            dwo_ref[...] = jnp.zeros_like(dwo_ref)

        dhb = dh_ref[...].astype(BF16)
        dm = _dot_nt(dhb, wo_ref[...])
        merged = jnp.zeros((tb, d), F32)
        for n, (g_ref, br, do_ref) in enumerate(((g0, oa, doa), (g1, ob, dob), (g2, oc, doc))):
            gate = jax.nn.sigmoid(g_ref[...])
            o_n = br[...]
            y = _dot(o_n, wb_ref[n])
            merged = merged + gate * y
            dy = (dm * gate).astype(BF16)
            dg_ref[:, n * d:(n + 1) * d] = (dm * y * gate * (1.0 - gate)).astype(BF16)
            do_ref[...] = _dot_nt(dy, wb_ref[n]).astype(BF16)
            dwb_ref[n] += _dot_tn(o_n, dy)
        dwo_ref[...] += _dot_tn(merged.astype(BF16), dhb)

    gate = lambda n: pl.BlockSpec((tb, d), lambda i, n=n: (i, n))
    wb_spec = pl.BlockSpec((3, 512, d), lambda i: (0, 0, 0))
    wo_spec = pl.BlockSpec((d, d), lambda i: (0, 0))
    return pl.pallas_call(
        body, name=name, grid=(lp // tb,),
        in_specs=[row(d), gate(0), gate(1), gate(2), row(512), row(512), row(512), wb_spec, wo_spec],
        out_specs=[row(GATES_W), row(512), row(512), row(512), wb_spec, wo_spec],
        out_shape=[SDS((lp, GATES_W), BF16)] + [SDS((lp, 512), BF16)] * 3 + [SDS((3, 512, d), F32), SDS((d, d), F32)],
        compiler_params=_params(("arbitrary",)),
    )(dh, proj, proj, proj, *outs, wb, wo)


def _shift_down(x, halo, n, first):
    rows = lax.broadcasted_iota(jnp.int32, x.shape, 0)
    edge = jnp.concatenate([pltpu.roll(halo, n, 0), jnp.zeros((x.shape[0] - 8, x.shape[1]), F32)], axis=0)
    edge = jnp.where(first, 0.0, edge)
    return jnp.where(rows < n, edge, pltpu.roll(x, n, 0))


def _shift_up(x, halo, n, last):
    tb = x.shape[0]
    rows = lax.broadcasted_iota(jnp.int32, x.shape, 0)
    edge = jnp.concatenate([jnp.zeros((tb - 8, x.shape[1]), F32), pltpu.roll(halo, 8 - n, 0)], axis=0)
    edge = jnp.where(last, 0.0, edge)
    return jnp.where(rows >= tb - n, edge, pltpu.roll(x, tb - n, 0))


def _conv(u, halo, w_ref, b_ref, first):
    taps = (_shift_down(u, halo, 2, first), _shift_down(u, halo, 1, first), u)
    c = b_ref[...] + w_ref[0:1, :] * taps[0] + w_ref[1:2, :] * taps[1] + w_ref[2:3, :] * taps[2]
    return c, taps


def _ffn_specs(tb, f):
    hb = tb // 8
    cur = lambda c: pl.BlockSpec((tb, f), lambda i, c=c: (i, c))
    prev = lambda c: pl.BlockSpec((8, f), lambda i, c=c: (jnp.maximum(i * hb - 1, 0), c))
    vec = lambda r, c: pl.BlockSpec((r, f), lambda i, c=c: (0, c))
    return cur, prev, vec


def _ffn_act_fwd(u, cw, cb, name):
    lp = u.shape[0]
    f = D_FF
    tb = TILE_ROW
    cur, prev, vec = _ffn_specs(tb, f)

    def body(ug, uv, hg, hv, wg, wv, bg, bv, o_ref):
        first = pl.program_id(0) == 0
        cg, _ = _conv(ug[...], hg[...], wg, bg, first)
        cv, _ = _conv(uv[...], hv[...], wv, bv, first)
        o_ref[...] = (cg * jax.nn.sigmoid(cg) * cv).astype(BF16)

    return pl.pallas_call(
        body, name=name, grid=(lp // tb,),
        in_specs=[cur(0), cur(1), prev(0), prev(1), vec(8, 0), vec(8, 1), vec(1, 0), vec(1, 1)],
        out_specs=pl.BlockSpec((tb, f), lambda i: (i, 0)), out_shape=SDS((lp, f), BF16),
        compiler_params=_params(("parallel",)),
    )(u, u, u, u, cw, cw, cb, cb)


def _ffn_act_bwd_conv(u, dact, cw, cb, name):
    lp = u.shape[0]
    f = D_FF
    tb = TILE_ROW
    cur, prev, vec = _ffn_specs(tb, f)

    def body(ug, uv, hg, hv, wg, wv, bg, bv, da_ref, dc_ref, dw_ref, db_ref):
        first = pl.program_id(0) == 0

        @pl.when(first)
        def _():
            dw_ref[...] = jnp.zeros_like(dw_ref)
            db_ref[...] = jnp.zeros_like(db_ref)

        cg, tg = _conv(ug[...], hg[...], wg, bg, first)
        cv, tv = _conv(uv[...], hv[...], wv, bv, first)
        da = da_ref[...]
        sg = jax.nn.sigmoid(cg)
        dcg = da * cv * sg * (1.0 + cg * (1.0 - sg))
        dcv = da * cg * sg
        for c, (dc, taps) in enumerate(((dcg, tg), (dcv, tv))):
            dc_ref[:, c * f:(c + 1) * f] = dc
            for n in range(3):
                dw_ref[n:n + 1, c * f:(c + 1) * f] += jnp.sum(dc * taps[n], axis=0, keepdims=True)
            db_ref[0:1, c * f:(c + 1) * f] += jnp.sum(dc, axis=0, keepdims=True)

    acc = pl.BlockSpec((8, 2 * f), lambda i: (0, 0))
    return pl.pallas_call(
        body, name=name, grid=(lp // tb,),
        in_specs=[cur(0), cur(1), prev(0), prev(1), vec(8, 0), vec(8, 1), vec(1, 0), vec(1, 1),
                  pl.BlockSpec((tb, f), lambda i: (i, 0))],
        out_specs=[pl.BlockSpec((tb, 2 * f), lambda i: (i, 0)), acc, acc],
        out_shape=[SDS((lp, 2 * f), F32), SDS((8, 2 * f), F32), SDS((8, 2 * f), F32)],
        compiler_params=_params(("arbitrary",)),
    )(u, u, u, u, cw, cw, cb, cb, dact)


def _ffn_act_bwd_in(dc, cw, name):
    lp = dc.shape[0]
    f2 = 2 * D_FF
    tb = TILE_ROW
    nb = lp // tb
    hb = tb // 8

    def body(dc_ref, n_ref, w_ref, o_ref):
        last = pl.program_id(0) == nb - 1
        dcv, halo = dc_ref[...], n_ref[...]
        du = (w_ref[2:3, :] * dcv + w_ref[1:2, :] * _shift_up(dcv, halo, 1, last)
              + w_ref[0:1, :] * _shift_up(dcv, halo, 2, last))
        o_ref[...] = du.astype(BF16)

    cur = pl.BlockSpec((tb, f2), lambda i: (i, 0))
    return pl.pallas_call(
        body, name=name, grid=(nb,),
        in_specs=[cur, pl.BlockSpec((8, f2), lambda i: (jnp.minimum((i + 1) * hb, nb * hb - 1), 0)),
                  pl.BlockSpec((8, f2), lambda i: (0, 0))],
        out_specs=cur, out_shape=SDS((lp, f2), BF16),
        compiler_params=_params(("parallel",)),
    )(dc, dc, cw)


def _loss_head(y, target, n_real, name):
    lp, d = y.shape
    tb = TILE_MM

    def body(y_ref, t_ref, dy_ref, loss_ref):
        i = pl.program_id(0)

        @pl.when(i == 0)
        def _():
            loss_ref[...] = jnp.zeros_like(loss_ref)

        rows = i * tb + lax.broadcasted_iota(jnp.int32, (tb, 1), 0)
        real = (rows >= N_META) & (rows < N_META + n_real)
        diff = jnp.where(real, y_ref[...] - t_ref[...], 0.0)
        dy_ref[...] = diff * (1.0 / d)
        loss_ref[...] += (0.5 / d) * jnp.sum(diff * diff).reshape(1, 1)

    row = pl.BlockSpec((tb, d), lambda i: (i, 0))
    return pl.pallas_call(
        body, name=name, grid=(lp // tb,), in_specs=[row, row],
        out_specs=[row, pl.BlockSpec((1, 1), lambda i: (0, 0))],
        out_shape=[SDS((lp, d), F32), SDS((1, 1), F32)],
        compiler_params=_params(("arbitrary",)),
    )(y, target)


def _pad_lanes(v, width, at=0):
    return jnp.pad(v.astype(F32), (at, width - at - v.shape[0]))[None, :]


def _mix_params(w, big, l):
    b = lambda a: a.astype(BF16)
    win = big["w_in"]
    fq, fk, fv, ff, cq, ckv, kr, sq, sk, sv, gates = jnp.split(
        win, [512, 1024, 1536, 1544, 1800, 1928, 1960, 2472, 2600, 2728], axis=1)
    misc = jnp.concatenate([kr, ff, jnp.zeros((D_MODEL, 88), win.dtype)], axis=1)
    w_in = b(jnp.concatenate([gates, fq, fk, fv, sq, sk, sv, cq, ckv, misc], axis=1))
    wq = jnp.pad(big["mla_w_q_up"].reshape(256, HEADS, 96), ((0, 0), (0, 0), (0, 32))).reshape(256, 1024)
    wkv = big["mla_w_kv_up"].reshape(128, HEADS, 128)
    wkk = jnp.pad(wkv[:, :, :64], ((0, 0), (0, 0), (0, 64))).reshape(128, 1024)
    wkvv = wkv[:, :, 64:].reshape(128, 512)
    tile = lambda g, n: jnp.tile(g.astype(F32), n)[None, :]
    prm = [tile(w["fox_q_g"][l], 8), tile(w["fox_k_g"][l], 8), tile(w["swa_q_g"][l], 8), tile(w["swa_k_g"][l], 2),
           _pad_lanes(w["fox_forget_b"][l], 128, FF_LANE), w["mla_q_a_g"][l][None, :], w["mla_kv_a_g"][l][None, :],
           tile(jnp.pad(w["mla_q_g"][l], (0, 32)), 8), tile(jnp.pad(w["mla_k_g"][l], (0, 32)), 8),
           wq.astype(F32), wkk.astype(F32), wkvv.astype(F32)]
    return dict(g1=w["norm1_g"][l][None, :], w_in=w_in, prm=prm, sinks=w["swa_sinks"][l].astype(F32),
                wb=b(big["w_branch"]), wo=b(big["w_o"]))


def _ffn_params(w, big, l):
    cw = jnp.pad(w["ffn_conv_w"][l].astype(F32), ((0, 5), (0, 0)))
    return dict(g2=w["norm2_g"][l][None, :], w_up=big["ffn_w_up"].astype(BF16), cw=cw,
                cb=w["ffn_conv_b"][l][None, :].astype(F32), w_down=big["ffn_w_down"].astype(BF16))


def _cols(c):
    ct = c[:, FF_LANE:FF_LANE + HEADS].T
    return ct[:, :, None], ct[:, None, :]


def _from_cols(col):
    return jnp.pad(col[:, :, 0].T, ((0, 0), (FF_LANE, 128 - FF_LANE - HEADS)))


def _layer_fwd_mix(h, lw, consts, cos, sin, slopes, l):
    tag = f"l{l}_"
    xn, proj = _norm_matmul(h, lw["g1"], lw["w_in"], IN_W // 2, tag + "in_proj")
    fq, fk, fv, mq, mk, mv, sq, skd, svd, ls = _prep_fwd(proj, lw["prm"], consts, cos, sin, tag + "prep")
    c = _cumsum([ls], False, tag + "decay_cumsum")
    c_col, c_row = _cols(c)
    oa, *lse_a = _att_fwd(_Att("fox"), fq, fk, fv, (c_row,), tag + "fox_fwd")
    ob, *lse_b = _att_fwd(_Att("mla"), mq, mk, mv, (), tag + "mla_fwd")
    oc, *lse_c = _att_fwd(_Att("swa"), sq, skd, svd, (lw["sinks"], slopes), tag + "swa_fwd")
    h2 = _post_fwd(h, proj, (oa, ob, oc), lw["wb"], lw["wo"], tag + "merge")
    saved = dict(h=h, xn=xn, proj=proj, q=(fq, mq, sq), k=(fk, mk, skd), v=(fv, mv, svd), c=(c_col, c_row),
                 o=(oa, ob, oc), lse=(lse_a, lse_b, lse_c), h2=h2)
    return h2, saved


def _layer_fwd_ffn(h2, lw, l):
    tag = f"l{l}_"
    xn2, u = _norm_matmul(h2, lw["g2"], lw["w_up"], D_FF, tag + "ffn_up")
    act = _ffn_act_fwd(u, lw["cw"], lw["cb"], tag + "ffn_act")
    h3 = _matmul_residual(act, lw["w_down"], h2, tag + "ffn_down")
    return h3, dict(xn2=xn2, u=u, act=act)


def _layer_bwd_ffn(dh3, lw, sv, l):
    tag = f"l{l}_"
    f = D_FF
    dact = _matmul_nt(dh3, lw["w_down"], f, tag + "ffn_down_dx")
    dw_down = _matmul_tn(sv["act"], dh3, D_MODEL, tag + "ffn_down_dw")
    dc, dcw, dcb = _ffn_act_bwd_conv(sv["u"], dact, lw["cw"], lw["cb"], tag + "ffn_act_dc")
    du = _ffn_act_bwd_in(dc, lw["cw"], tag + "ffn_act_du")
    dw_up = _matmul_tn(sv["xn2"], du, f, tag + "ffn_up_dw")
    dh2, dg2 = _norm_matmul_bwd([du], lw["w_up"], sv["h2"], lw["g2"], dh3, tag + "ffn_up_dx")
    g = dict(norm2_g=dg2[0], ffn_w_up=dw_up, ffn_conv_w=dcw[0:3], ffn_conv_b=dcb[0], ffn_w_down=dw_down)
    return dh2, g


def _layer_bwd_mix(dh2, lw, sv, consts, folds, cos, sin, slopes, l, hook=None):
    tag = f"l{l}_"
    dgates, doa, dob, doc, dwb, dwo = _post_bwd(dh2, sv["proj"], sv["o"], lw["wb"], lw["wo"], tag + "merge_bwd")
    c_col, c_row = sv["c"]
    extras = ((c_row,), (), (lw["sinks"], slopes))
    extras_kv = ((c_col,), (), (slopes,))
    grads = []
    for n, (mode, do) in enumerate((("fox", doa), ("mla", dob), ("swa", doc))):
        att = _Att(mode)
        q, k, v = sv["q"][n], sv["k"][n], sv["v"][n]
        lse_col, lse_row = sv["lse"][n]
        res = _att_dq(att, q, k, v, sv["o"][n], do, lse_col, extras[n], tag + mode + "_dq")
        dq, delta_row = res[0], res[2]
        res_kv = _att_dkv(att, q, k, v, do, lse_row, delta_row, extras_kv[n], tag + mode + "_dkv")
        grads.append((dq, res_kv[0], res_kv[1], res[3:], res_kv[2:]))
    (dfq, dfk, dfv, (dcq,), (dck,)), (dmq, dmk, dmv, _, _), (dsq, dskp, dsvp, (dsink,), _) = grads
    dls = _cumsum([_from_cols(dcq), _from_cols(dck)], True, tag + "decay_cumsum_bwd")
    res = _prep_bwd(sv["proj"], lw["prm"], consts, cos, sin,
                    (dfq, dfk, dfv, dmq, dmk, dmv, dsq, dskp, dsvp, dls), folds, tag + "prep_bwd")
    dother, pg = res[0], res[1:]
    dw_g = _matmul_tn(sv["xn"], dgates, GATES_W, tag + "in_proj_dw_gates")
    dw_o = _matmul_tn(sv["xn"], dother, OTHER_W, tag + "in_proj_dw_other")
    d_in = jnp.concatenate([
        dw_o[:, O_FQ:O_FV + 512], dw_o[:, O_MISC + FF_LANE:O_MISC + FF_LANE + 8], dw_o[:, O_CQ:O_CQ + 256],
        dw_o[:, O_CKV:O_CKV + 128], dw_o[:, O_MISC:O_MISC + 32], dw_o[:, O_SQ:O_SQ + 512], dw_o[:, O_SK:O_SK + 128],
        dw_o[:, O_SV:O_SV + 128], dw_g], axis=1)
    d_wq = pg[9].reshape(256, HEADS, 128)[:, :, :96].reshape(256, 768)
    d_wkv = jnp.concatenate([pg[10].reshape(128, HEADS, 128)[:, :, :64], pg[11].reshape(128, HEADS, 64)],
                            axis=2).reshape(128, 1024)
    g = dict(
        w_in=d_in, fox_forget_b=pg[4][0, FF_LANE:FF_LANE + 8], fox_q_g=pg[0][0, :64],
        fox_k_g=pg[1][0, :64], mla_q_a_g=pg[5][0], mla_w_q_up=d_wq, mla_kv_a_g=pg[6][0], mla_w_kv_up=d_wkv,
        mla_q_g=pg[7][0, :96], mla_k_g=pg[8][0, :96], swa_q_g=pg[2][0, :64], swa_k_g=pg[3][0, :64],
        swa_sinks=dsink[:, 0, 0:2 * PAIRS].reshape(HEADS), w_branch=dwb, w_o=dwo)
    tick = hook(g) if hook else None
    g1 = lw["g1"] if tick is None else lw["g1"] + tick
    dh, dg1 = _norm_matmul_bwd([dgates, dother], lw["w_in"], sv["h"], g1, dh2, tag + "in_proj_dx")
    g["norm1_g"] = dg1[0]
    return dh, g


_MIX_BIG = ("w_in", "mla_w_q_up", "mla_w_kv_up", "w_branch", "w_o")
_FFN_BIG = ("ffn_w_up", "ffn_w_down")


def _local_step(x, target, w, hook=None, fetch=None):
    if fetch is None:
        fetch = lambda l, stage, after: {n: w[n][l] for n in (_MIX_BIG if stage == "mix" else _FFN_BIG)}
    seq = x.shape[0]
    length = N_META + seq
    lp = -(-length // ROW_ALIGN) * ROW_ALIGN
    pad = lp - length
    h = jnp.concatenate([w["meta_tokens"].astype(F32), x, jnp.zeros((pad, D_MODEL), F32)], axis=0)
    tgt = jnp.pad(target, ((N_META, pad), (0, 0)))
    consts = _consts()
    folds = (_fold_matrix(512, 64), _fold_matrix(1024, 128))
    cos, sin = _rope_tables(lp)
    slopes = jnp.asarray(2.0 ** (-8.0 * np.arange(1, HEADS + 1, dtype=np.float32) / HEADS), F32)
    lws, saved = [], []
    for l in range(DEPTH):
        lw = _mix_params(w, fetch(l, "mix", h), l)
        h, sv = _layer_fwd_mix(h, lw, consts, cos, sin, slopes, l)
        lw.update(_ffn_params(w, fetch(l, "ffn", h), l))
        h, sv_ffn = _layer_fwd_ffn(h, lw, l)
        lws.append(lw)
        saved.append({**sv, **sv_ffn})
    dh, loss = _loss_head(h, tgt, seq, "loss_head")
    grads = [None] * DEPTH
    for l in reversed(range(DEPTH)):
        dh, g_ffn = _layer_bwd_ffn(dh, lws[l], saved[l], l)
        tick = hook(l, "ffn", g_ffn) if hook else None
        if tick is not None:
            lws[l]["sinks"] = lws[l]["sinks"] + tick
        mix_hook = (lambda g, l=l, g_ffn=g_ffn: hook(l, "mix", {**g_ffn, **g})) if hook else None
        dh, g_mix = _layer_bwd_mix(dh, lws[l], saved[l], consts, folds, cos, sin, slopes, l, mix_hook)
        grads[l] = {**g_ffn, **g_mix}
    return loss, dh[N_META:length], dh[:N_META], grads


def _place():
    return lax.axis_index("x"), lax.axis_index("y"), lax.axis_index("c")


def _flip(pos, k):
    x, y, c = pos
    return (1 - x if k & 4 else x, 1 - y if k & 2 else y, 1 - c if k & 1 else c)


def _index(pos):
    return 4 * pos[0] + 2 * pos[1] + pos[2]


def _gather(tensors, name):
    n_t = len(tensors)

    def body(*refs):
        ins, outs = refs[:n_t], refs[n_t:2 * n_t]
        send_sems, recv_sems, local_sems = refs[2 * n_t:]
        x, y, c = _place()
        me, sibling = (x, y, c), (x, y, 1 - c)
        chips = [(1 - x, y), (x, 1 - y), (1 - x, 1 - y)]

        def copy(t, k, block, to, src=None):
            dst = outs[t].at[_index(block)]
            return pltpu.make_async_remote_copy(
                src_ref=dst if src is None else src, dst_ref=dst, send_sem=send_sems.at[t, k],
                recv_sem=recv_sems.at[t, k], device_id=to, device_id_type=pl.DeviceIdType.MESH)

        local, sent = [], []
        for t in range(n_t):
            local.append(pltpu.make_async_copy(ins[t], outs[t].at[_index(me)], local_sems.at[t]))
            local[-1].start()
            sent.append(copy(t, 0, me, sibling, src=ins[t]))
            sent += [copy(t, 1 + j, me, (*chip, c), src=ins[t]) for j, chip in enumerate(chips)]
        for cp in sent:
            cp.start()
        for j, chip in enumerate(chips):
            for t in range(n_t):
                copy(t, 1 + j, (*chip, c), me).wait_recv()
                sent.append(copy(t, 4 + j, (*chip, c), sibling))
                sent[-1].start()
        for t in range(n_t):
            copy(t, 0, sibling, me).wait_recv()
            for j, chip in enumerate(chips):
                copy(t, 4 + j, (*chip, 1 - c), me).wait_recv()
        for cp in sent:
            cp.wait_send()
        for cp in local:
            cp.wait()

    any_spec = pl.BlockSpec(memory_space=pl.ANY)
    return pl.pallas_call(
        body, name=name, in_specs=[any_spec] * n_t, out_specs=[any_spec] * n_t,
        out_shape=[SDS((N_DEV,) + a.shape, a.dtype) for a in tensors],
        scratch_shapes=[pltpu.SemaphoreType.DMA((n_t, N_DEV - 1)), pltpu.SemaphoreType.DMA((n_t, N_DEV - 1)),
                        pltpu.SemaphoreType.DMA((n_t,))],
    )(*tensors)


def _exchange_start(tensors, name, gather=False, after=None):
    n_t = len(tensors)

    def body(*refs):
        ins, lands = refs[:n_t], refs[n_t:2 * n_t]
        send_sem, recv_sem = refs[2 * n_t + 1:2 * n_t + 3]
        token = refs[-1]
        me = _place()
        mine = _index(me)
        for t in range(n_t):
            for k in range(1, N_DEV):
                peer = _flip(me, k)
                pltpu.make_async_remote_copy(
                    src_ref=ins[t] if gather else ins[t].at[_index(peer)], dst_ref=lands[t].at[mine],
                    send_sem=send_sem, recv_sem=recv_sem, device_id=peer, device_id_type=pl.DeviceIdType.MESH).start()
        token[...] = jnp.zeros_like(token)

    hbm = pl.BlockSpec(memory_space=pltpu.HBM)
    sem = pl.BlockSpec(memory_space=pltpu.SEMAPHORE)
    one = pltpu.SemaphoreType.DMA(())
    land_shape = lambda a: ((N_DEV,) + a.shape) if gather else a.shape
    bufs = ([pltpu.HBM(a.shape, a.dtype) for a in tensors] + [pltpu.HBM(land_shape(a), a.dtype) for a in tensors])
    after = jnp.zeros((8, 128), F32) if after is None else after
    outs = pl.pallas_call(
        body, name=name, in_specs=[hbm] * (2 * n_t) + [pl.BlockSpec(memory_space=pl.ANY)],
        out_specs=[sem, sem] + [hbm] * (2 * n_t) + [pl.BlockSpec(memory_space=pltpu.VMEM)],
        out_shape=[one, one] + bufs + [SDS((8, 128), F32)],
        input_output_aliases={i: 2 + i for i in range(2 * n_t)},
        compiler_params=pltpu.CompilerParams(has_side_effects=pltpu.SideEffectType.DATAFLOW_SIDE_EFFECTING),
    )(*[pltpu.with_memory_space_constraint(a, pltpu.HBM) for a in tensors],
      *[pltpu.with_memory_space_constraint(lax.empty(land_shape(a), a.dtype), pltpu.HBM) for a in tensors], after)
    return outs[:-1], outs[-1][0, 0]


def _exchange_wait(state, after, name, gather=False):
    n_t = (len(state) - 2) // 2

    def body(*refs):
        send_sem, recv_sem = refs[0:2]
        ins, lands = refs[2:2 + n_t], refs[2 + n_t:2 + 2 * n_t]
        me = _place()
        for t in range(n_t):
            for k in range(1, N_DEV):
                peer = _flip(me, k)
                copy = pltpu.make_async_remote_copy(
                    src_ref=ins[t] if gather else ins[t].at[_index(peer)], dst_ref=lands[t].at[_index(peer)],
                    send_sem=send_sem, recv_sem=recv_sem, device_id=peer, device_id_type=pl.DeviceIdType.MESH)
                copy.wait_send()
                copy.wait_recv()

    hbm = pl.BlockSpec(memory_space=pltpu.HBM)
    sem = pl.BlockSpec(memory_space=pltpu.SEMAPHORE)
    bufs = [pltpu.HBM(a.shape, a.dtype) for a in state[2:]]
    outs = pl.pallas_call(
        body, name=name, in_specs=[sem, sem] + [hbm] * (2 * n_t) + [pl.BlockSpec(memory_space=pl.ANY)],
        out_specs=[hbm] * (2 * n_t), out_shape=bufs,
        input_output_aliases={2 + i: i for i in range(2 * n_t)},
        compiler_params=pltpu.CompilerParams(has_side_effects=pltpu.SideEffectType.DATAFLOW_SIDE_EFFECTING),
    )(*state, after)
    return outs[n_t:]


def _sum_slots(parts, name):
    n, rows, w = parts.shape
    tb = 8

    def body(p_ref, o_ref):
        acc = p_ref[0].astype(F32)
        for s in range(1, n):
            acc = acc + p_ref[s].astype(F32)
        o_ref[...] = acc

    return pl.pallas_call(
        body, name=name, grid=(rows // tb,),
        in_specs=[pl.BlockSpec((n, tb, w), lambda i: (0, i, 0))], out_specs=pl.BlockSpec((tb, w), lambda i: (i, 0)),
        out_shape=SDS((rows, w), F32), compiler_params=_params(("parallel",)),
    )(parts)


def _adamw(wt, m, v, parts, name, own=None):
    shape = wt.shape
    parts = parts if isinstance(parts, (list, tuple)) else [parts]
    n, w = parts[0].shape[0], shape[-1]
    rows = math.prod(shape[:-1])
    per = rows // len(parts)
    step = 16 if parts[0].dtype == BF16 else 8
    tb = max([t for t in range(step, 257, step) if per % t == 0] or [per])
    nb = per // tb
    c1 = 1.0 / (1.0 - ADAM_B1 ** ADAM_STEP)
    c2 = 1.0 / (1.0 - ADAM_B2 ** ADAM_STEP)
    state = [a.reshape(rows, w) for a in (wt, m, v)]
    n_in = 4 if own is None else 5
    outs = None
    for l in reversed(range(len(parts))):
        def body(*refs):
            idx_ref = None if own is None else refs[0]
            w_ref, m_ref, v_ref, p_ref = refs[n_in - 4:n_in] if own is None else refs[1:5]
            g_out, d_out, m_out, v_out = refs[-4:]
            g = None
            for s in range(n):
                term = p_ref[s] if own is None else jnp.where(idx_ref[0] == s, refs[5][0], p_ref[s])
                g = term.astype(F32) if g is None else g + term.astype(F32)
            m_new = ADAM_B1 * m_ref[...] + (1.0 - ADAM_B1) * g
            v_new = ADAM_B2 * v_ref[...] + (1.0 - ADAM_B2) * (g * g)
            g_out[...] = g
            m_out[...] = m_new
            v_out[...] = v_new
            d_out[...] = -ADAM_LR * ((m_new * c1) / (jnp.sqrt(v_new * c2) + ADAM_EPS) + ADAM_WD * w_ref[...])

        row = pl.BlockSpec((tb, w), lambda i, *_, l=l: (l * nb + i, 0))
        in_specs = [row, row, row, pl.BlockSpec((n, tb, w), lambda i, *_: (0, i, 0))]
        args = [*state, parts[l].reshape(n, per, w)]
        if own is not None:
            in_specs.append(pl.BlockSpec((1, tb, w), lambda i, idx: (idx[0], i, 0)))
            args.append(own[l].reshape(n, per, w))
        prev = [] if outs is None else list(outs)
        in_specs += [pl.BlockSpec(memory_space=pl.ANY)] * len(prev)
        n_pre = 0 if own is None else 1
        call = dict(name=f"{name}_{l}", out_shape=[SDS((rows, w), F32)] * 4,
                    input_output_aliases={n_pre + len(args) + k: k for k in range(len(prev))},
                    compiler_params=_params(("parallel",)))
        if own is None:
            outs = pl.pallas_call(body, grid=(nb,), in_specs=in_specs, out_specs=[row] * 4, **call)(*args, *prev)
        else:
            spec = pltpu.PrefetchScalarGridSpec(num_scalar_prefetch=1, grid=(nb,), in_specs=in_specs, out_specs=[row] * 4)
            idx = jnp.reshape(_index(_place()), (1,)).astype(jnp.int32)
            outs = pl.pallas_call(body, grid_spec=spec, **call)(idx, *args, *prev)
    return [o.reshape(shape) for o in outs]


_BIG = [("w_in", 2), ("mla_w_q_up", 2), ("mla_w_kv_up", 2), ("w_branch", 3), ("w_o", 1), ("ffn_w_up", 2), ("ffn_w_down", 1)]
_SMALL_SHARDED = [("meta_tokens", 1), ("ffn_conv_w", 2)]
_REPLICATED = ["norm1_g", "fox_forget_b", "fox_q_g", "fox_k_g", "mla_q_a_g", "mla_kv_a_g", "mla_q_g", "mla_k_g",
               "swa_q_g", "swa_k_g", "swa_sinks", "norm2_g", "ffn_conv_b"]
_ORDER = ["meta_tokens", "norm1_g", "w_in", "fox_forget_b", "fox_q_g", "fox_k_g", "mla_q_a_g", "mla_w_q_up",
          "mla_kv_a_g", "mla_w_kv_up", "mla_q_g", "mla_k_g", "swa_q_g", "swa_k_g", "swa_sinks", "w_branch", "w_o",
          "norm2_g", "ffn_w_up", "ffn_conv_w", "ffn_conv_b", "ffn_w_down"]


def _flat_rows(vecs, dtype, row_mult):
    flat = jnp.concatenate([a.reshape(-1).astype(dtype) for a in vecs])
    rows = -(-flat.shape[0] // (1024 * row_mult)) * row_mult
    return jnp.pad(flat, (0, rows * 1024 - flat.shape[0])).reshape(rows, 1024)


def _unflatten(flat, shapes):
    out, off = [], 0
    for s in shapes:
        n = math.prod(s)
        out.append(flat[off:off + n].reshape(s))
        off += n
    return out


def _to_full(blocks, axis):
    moved = jnp.moveaxis(blocks, 0, axis)
    s = moved.shape
    return moved.reshape(s[:axis] + (s[axis] * s[axis + 1],) + s[axis + 2:])


def _to_blocks(full, axis):
    s = full.shape
    split = full.reshape(s[:axis] + (N_DEV, s[axis] // N_DEV) + s[axis + 1:])
    return jnp.moveaxis(split, axis, 0)


def kernel(x, meta_tokens, norm1_g, w_in, fox_forget_b, fox_q_g, fox_k_g, mla_q_a_g, mla_w_q_up, mla_kv_a_g, mla_w_kv_up, mla_q_g, mla_k_g, swa_q_g, swa_k_g, swa_sinks, w_branch, w_o, norm2_g, ffn_w_up, ffn_conv_w, ffn_conv_b, ffn_w_down, loss_target, m_meta_tokens, m_norm1_g, m_w_in, m_fox_forget_b, m_fox_q_g, m_fox_k_g, m_mla_q_a_g, m_mla_w_q_up, m_mla_kv_a_g, m_mla_w_kv_up, m_mla_q_g, m_mla_k_g, m_swa_q_g, m_swa_k_g, m_swa_sinks, m_w_branch, m_w_o, m_norm2_g, m_ffn_w_up, m_ffn_conv_w, m_ffn_conv_b, m_ffn_w_down, v_meta_tokens, v_norm1_g, v_w_in, v_fox_forget_b, v_fox_q_g, v_fox_k_g, v_mla_q_a_g, v_mla_w_q_up, v_mla_kv_a_g, v_mla_w_kv_up, v_mla_q_g, v_mla_k_g, v_swa_q_g, v_swa_k_g, v_swa_sinks, v_w_branch, v_w_o, v_norm2_g, v_ffn_w_up, v_ffn_conv_w, v_ffn_conv_b, v_ffn_w_down):
    wl = dict(zip(_ORDER, (meta_tokens, norm1_g, w_in, fox_forget_b, fox_q_g, fox_k_g, mla_q_a_g, mla_w_q_up,
                           mla_kv_a_g, mla_w_kv_up, mla_q_g, mla_k_g, swa_q_g, swa_k_g, swa_sinks, w_branch, w_o,
                           norm2_g, ffn_w_up, ffn_conv_w, ffn_conv_b, ffn_w_down)))
    ml = dict(zip(_ORDER, (m_meta_tokens, m_norm1_g, m_w_in, m_fox_forget_b, m_fox_q_g, m_fox_k_g, m_mla_q_a_g,
                           m_mla_w_q_up, m_mla_kv_a_g, m_mla_w_kv_up, m_mla_q_g, m_mla_k_g, m_swa_q_g, m_swa_k_g,
                           m_swa_sinks, m_w_branch, m_w_o, m_norm2_g, m_ffn_w_up, m_ffn_conv_w, m_ffn_conv_b,
                           m_ffn_w_down)))
    vl = dict(zip(_ORDER, (v_meta_tokens, v_norm1_g, v_w_in, v_fox_forget_b, v_fox_q_g, v_fox_k_g, v_mla_q_a_g,
                           v_mla_w_q_up, v_mla_kv_a_g, v_mla_w_kv_up, v_mla_q_g, v_mla_k_g, v_swa_q_g, v_swa_k_g,
                           v_swa_sinks, v_w_branch, v_w_o, v_norm2_g, v_ffn_w_up, v_ffn_conv_w, v_ffn_conv_b,
                           v_ffn_w_down)))
    small_sh = [n for n, _ in _SMALL_SHARDED]
    big = [n for n, _ in _BIG]
    axis_of = dict(_BIG)
    idx = _index(_place())

    def to_full(n, blocks, own=None):
        if own is not None:
            sel = (jnp.arange(N_DEV) == idx).reshape((N_DEV,) + (1,) * own.ndim)
            blocks = jnp.where(sel, own[None], blocks)
        return _to_full(blocks, axis_of[n] - 1)

    local = {(n, l): wl[n][l].astype(BF16) for n in big for l in range(DEPTH)}
    got = _gather([local[(n, 0)] for n in _MIX_BIG] + [wl[n] for n in small_sh], "gather_weights_l0_mix")
    full = {n: wl[n] for n in _REPLICATED}
    for (n, axis), blocks in zip(_SMALL_SHARDED, got[len(_MIX_BIG):]):
        full[n] = _to_full(blocks, axis)
    ready = {(n, 0): to_full(n, blocks) for n, blocks in zip(_MIX_BIG, got)}
    later = {"l0_ffn": [(n, 0) for n in _FFN_BIG], "l1": [(n, 1) for n in big]}
    states = {}
    for key, names in later.items():
        states[key], tick = _exchange_start([local[e] for e in names], "gather_weights_" + key + "_start", True, got[0])
        full["norm1_g"] = full["norm1_g"] + tick

    def fetch(l, stage, after):
        key = "l0_ffn" if l == 0 else "l1"
        if (l, stage) != (0, "mix") and key in states:
            lands = _exchange_wait(states.pop(key), after, "gather_weights_" + key + "_wait", True)
            ready.update({e: to_full(e[0], blocks, local[e]) for e, blocks in zip(later[key], lands)})
        return {n: ready[(n, l)] for n in (_MIX_BIG if stage == "mix" else _FFN_BIG)}

    blocks_of = lambda g, names: [_to_blocks(g[n], axis_of[n] - 1).astype(BF16) for n in names]
    early = {}

    def hook(l, stage, g):
        if l == DEPTH - 1 and stage == "mix":
            key, names = "l1", big
        elif l == 0:
            key, names = "l0_" + stage, (_FFN_BIG if stage == "ffn" else _MIX_BIG)
        else:
            return None
        sends = blocks_of(g, names)
        state, tick = _exchange_start(sends, "exchange_grads_" + key + "_start")
        early[key] = (names, l, sends, state)
        return tick

    loss, grad_x, grad_meta, grads = _local_step(x[0], loss_target[0], full, hook, fetch)
    result = {kind: {} for kind in ("grad", "delta", "new_m", "new_v")}
    small_grads = {k: jnp.stack([grads[l][k] for l in range(DEPTH)]) for k in grads[0] if k not in big}
    small_grads["meta_tokens"] = grad_meta
    small_full = _REPLICATED + small_sh
    mine_small = _flat_rows([small_grads[n] for n in small_full] + [loss], F32, 8)
    small_state, _ = _exchange_start([mine_small], "gather_small_grads_start", True)
    landed, sent = {}, {}
    after = early["l0_mix"][2][0]
    for key in ("l1", "l0_ffn"):
        names, l, sends, state = early[key]
        got = _exchange_wait(state, after, "exchange_grads_" + key + "_wait")
        landed.update({(n, l): p for n, p in zip(names, got)})
        sent.update({(n, l): p for n, p in zip(names, sends)})

    def update(names):
        for n in names:
            outs = _adamw(wl[n], ml[n], vl[n], [landed[(n, l)] for l in range(DEPTH)], "adamw_" + n,
                          [sent[(n, l)] for l in range(DEPTH)])
            for kind, val in zip(result, outs):
                result[kind][n] = val

    update(_FFN_BIG)
    got_small = _exchange_wait(small_state, result["grad"]["ffn_w_down"], "gather_small_grads_wait", True)[0]
    sel = (jnp.arange(N_DEV) == idx).reshape(N_DEV, 1, 1)
    total_small = _sum_slots(jnp.where(sel, mine_small[None], got_small), "sum_small_grads").reshape(-1)
    pieces = _unflatten(total_small, [small_grads[n].shape for n in small_full] + [()])
    loss_total = pieces[-1]
    g_small = dict(zip(small_full, pieces[:-1]))
    for n, axis in _SMALL_SHARDED:
        size = wl[n].shape[axis]
        g_small[n] = lax.dynamic_slice_in_dim(g_small[n], idx * size, size, axis)
    flat = lambda d: _flat_rows([d[n] for n in small_full], F32, 8)
    small_out = _adamw(flat(wl), flat(ml), flat(vl), flat(g_small)[None], "adamw_small")
    for kind, fs in zip(result, small_out):
        result[kind].update(zip(small_full, _unflatten(fs.reshape(-1), [wl[n].shape for n in small_full])))
    names, l, sends, state = early["l0_mix"]
    got = _exchange_wait(state, small_out[0], "exchange_grads_l0_mix_wait")
    landed.update({(n, l): p for n, p in zip(names, got)})
    sent.update({(n, l): p for n, p in zip(names, sends)})
    update(_MIX_BIG)
    outs = [loss_total, grad_x[None]]
    for kind in ("grad", "delta", "new_m", "new_v"):
        outs += [result[kind][n] for n in _ORDER]
    return tuple(outs)
```

```python
import functools
import math

import numpy as np
import jax
import jax.numpy as jnp
from jax import lax
from jax.experimental import pallas as pl
from jax.experimental.pallas import tpu as pltpu

F32, BF16 = jnp.float32, jnp.bfloat16
SDS = jax.ShapeDtypeStruct

D_MODEL = 1024
N_META = 16
EPS = 1e-6
WINDOW = 128
ROPE_THETA = 10000.0
HEADS = 8
D_FF = 2816
DEPTH = 2
N_DEV = 8
ADAM_LR, ADAM_B1, ADAM_B2, ADAM_EPS, ADAM_WD, ADAM_STEP = 0.001, 0.9, 0.999, 1e-08, 0.01, 10

ROW_ALIGN = 384
TILE_MM = 384
TILE_ROW = 192
TILE_CONV_BWD = 128
TILE_ATT = 384
TILE_POST = 384
PAIRS = 2
VMEM_LIMIT = 56 * 1024 * 1024

GATES_W = 3072
OTHER_W = 2816
IN_W = GATES_W + OTHER_W
O_FQ, O_FK, O_FV, O_SQ, O_SK, O_SV, O_CQ, O_CKV, O_MISC = 0, 512, 1024, 1536, 2048, 2176, 2304, 2560, 2688
FF_LANE = 32

NEG = -1e30


def _dot(a, b):
    return jnp.dot(a, b, preferred_element_type=F32)


def _dot_nt(a, b):
    return lax.dot_general(a, b, (((1,), (1,)), ((), ())), preferred_element_type=F32)


def _dot_tn(a, b):
    return lax.dot_general(a, b, (((0,), (0,)), ((), ())), preferred_element_type=F32)


def _params(sem):
    return pltpu.CompilerParams(dimension_semantics=sem, vmem_limit_bytes=VMEM_LIMIT)


def _rms(x, g):
    return x * lax.rsqrt(jnp.mean(x * x, axis=-1, keepdims=True) + EPS) * g


def _split_dot(x, m, pieces=2):
    acc, rest = None, x
    for _ in range(pieces):
        part = rest.astype(BF16)
        rest = rest - part.astype(F32)
        acc = _dot(part, m) if acc is None else acc + _dot(part, m)
    return acc


@jax.custom_vjp
def _sel(x, m, mt):
    return _split_dot(x, m)


_sel.defvjp(lambda x, m, mt: (_split_dot(x, m), (m, mt)), lambda res, dy: (_split_dot(dy, res[1]), None, None))


@jax.custom_vjp
def _mm(x, w):
    return _dot(x.astype(BF16), w.astype(BF16))


def _mm_bwd(res, dy):
    x, w = res
    dyb = dy.astype(BF16)
    return _dot_nt(dyb, w.astype(BF16)), _dot_tn(x.astype(BF16), dyb)


_mm.defvjp(lambda x, w: (_mm(x, w), (x, w)), _mm_bwd)


def _rot_impl(x):
    w = x.shape[1]
    lane = lax.broadcasted_iota(jnp.int32, x.shape, 1) % 128
    lo = (lane >= 64) & (lane < 80)
    hi = (lane >= 80) & (lane < 96)
    return jnp.where(hi, pltpu.roll(x, 16, 1), 0.0) - jnp.where(lo, pltpu.roll(x, w - 16, 1), 0.0)


@jax.custom_vjp
def _rot(x):
    return _rot_impl(x)


_rot.defvjp(lambda x: (_rot_impl(x), None), lambda _, dy: (-_rot_impl(dy),))


def _gnorm(x, g, e, et, dim):
    inv = lax.rsqrt(_sel(x * x, e, et) * (1.0 / dim) + EPS)
    return x * _sel(inv, et, e) * g


def _indicator(width, period):
    m = np.zeros((width, 128), np.float32)
    m[np.arange(width), np.arange(width) // period] = 1.0
    return m


def _consts():
    e64 = _indicator(512, 64)
    e128 = _indicator(1024, 128)
    sk = np.zeros((128, 1024), np.float32)
    for h in range(HEADS):
        sk[np.arange(32), 128 * h + 64 + np.arange(32)] = 1.0
    dup = np.zeros((128, 256), np.float32)
    for g in range(2):
        for r in range(2):
            dup[64 * g + np.arange(64), 128 * g + 64 * r + np.arange(64)] = 1.0
    mats = [e64, e64.T, e128, e128.T, sk, sk.T, dup, dup.T]
    return [jnp.asarray(m, BF16) for m in mats]


def _fold_matrix(width, period):
    m = np.zeros((width, 128), np.float32)
    m[np.arange(width), np.arange(width) % period] = 1.0
    return jnp.asarray(m, BF16)


def _rope_tables(lp):
    half = 16
    freqs = ROPE_THETA ** (-np.arange(half, dtype=np.float32) / half)
    ang = np.arange(lp, dtype=np.float32)[:, None] * freqs[None, :]
    cos = np.ones((lp, 128), np.float32)
    sin = np.zeros((lp, 128), np.float32)
    cos[:, 64:80] = np.cos(ang)
    cos[:, 80:96] = np.cos(ang)
    sin[:, 64:80] = np.sin(ang)
    sin[:, 80:96] = np.sin(ang)
    return jnp.asarray(cos), jnp.asarray(sin)


def _norm_matmul(h, g, w, tn, name):
    lp, d = h.shape
    n = w.shape[1]
    tb = TILE_MM

    def body(h_ref, g_ref, w_ref, xn_ref, y_ref):
        @pl.when(pl.program_id(1) == 0)
        def _():
            xn_ref[...] = _rms(h_ref[...], g_ref[...]).astype(BF16)

        y_ref[...] = _dot(xn_ref[...], w_ref[...])

    return pl.pallas_call(
        body, name=name, grid=(lp // tb, n // tn),
        in_specs=[pl.BlockSpec((tb, d), lambda i, j: (i, 0)), pl.BlockSpec((1, d), lambda i, j: (0, 0)),
                  pl.BlockSpec((d, tn), lambda i, j: (0, j))],
        out_specs=[pl.BlockSpec((tb, d), lambda i, j: (i, 0)), pl.BlockSpec((tb, tn), lambda i, j: (i, j))],
        out_shape=[SDS((lp, d), BF16), SDS((lp, n), F32)],
        compiler_params=_params(("parallel", "arbitrary")),
    )(h, g, w)


def _matmul_residual(a, w, res, name):
    m, k = a.shape
    n = w.shape[1]
    tb = TILE_MM

    def body(a_ref, w_ref, r_ref, o_ref):
        o_ref[...] = r_ref[...] + _dot(a_ref[...], w_ref[...])

    return pl.pallas_call(
        body, name=name, grid=(m // tb,),
        in_specs=[pl.BlockSpec((tb, k), lambda i: (i, 0)), pl.BlockSpec((k, n), lambda i: (0, 0)),
                  pl.BlockSpec((tb, n), lambda i: (i, 0))],
        out_specs=pl.BlockSpec((tb, n), lambda i: (i, 0)),
        out_shape=SDS((m, n), F32),
        compiler_params=_params(("parallel",)),
    )(a, w, res)


def _matmul_nt(dy, w, tn, name):
    m, k = dy.shape
    n = w.shape[0]
    tb = TILE_MM

    def body(dy_ref, w_ref, o_ref):
        o_ref[...] = _dot_nt(dy_ref[...].astype(BF16), w_ref[...])

    return pl.pallas_call(
        body, name=name, grid=(m // tb, n // tn),
        in_specs=[pl.BlockSpec((tb, k), lambda i, j: (i, 0)), pl.BlockSpec((tn, k), lambda i, j: (j, 0))],
        out_specs=pl.BlockSpec((tb, tn), lambda i, j: (i, j)),
        out_shape=SDS((m, n), F32),
        compiler_params=_params(("parallel", "arbitrary")),
    )(dy, w)


def _matmul_tn(x, dy, tn, name):
    m, k = x.shape
    n = dy.shape[1]
    tb = TILE_MM
    nb = m // tb

    def body(x_ref, dy_ref, o_ref, acc):
        i = pl.program_id(1)

        @pl.when(i == 0)
        def _():
            acc[...] = jnp.zeros_like(acc)

        acc[...] += _dot_tn(x_ref[...].astype(BF16), dy_ref[...].astype(BF16))

        @pl.when(i == nb - 1)
        def _():
            o_ref[...] = acc[...].astype(BF16)

    return pl.pallas_call(
        body, name=name, grid=(n // tn, nb),
        in_specs=[pl.BlockSpec((tb, k), lambda j, i: (i, 0)), pl.BlockSpec((tb, tn), lambda j, i: (i, j))],
        out_specs=pl.BlockSpec((k, tn), lambda j, i: (0, j)),
        out_shape=SDS((k, n), BF16),
        scratch_shapes=[pltpu.VMEM((k, tn), F32)],
        compiler_params=_params(("parallel", "arbitrary")),
    )(x, dy)


def _norm_matmul_bwd(dys, w, x, g, dres, name):
    m, d = x.shape
    tb = TILE_MM
    widths = [a.shape[1] for a in dys]
    n_dy = len(dys)

    def body(*refs):
        w_ref, x_ref, g_ref, r_ref, o_ref, dg_ref = refs[n_dy:]

        @pl.when(pl.program_id(0) == 0)
        def _():
            dg_ref[...] = jnp.zeros_like(dg_ref)

        dxn, off = None, 0
        for dy_ref, width in zip(refs[:n_dy], widths):
            part = _dot_nt(dy_ref[...], w_ref[:, off:off + width])
            dxn = part if dxn is None else dxn + part
            off += width
        _, vjp = jax.vjp(_rms, x_ref[...], g_ref[...])
        dx, dg = vjp(dxn)
        o_ref[...] = r_ref[...] + dx
        dg_ref[...] += dg

    row = pl.BlockSpec((tb, d), lambda i: (i, 0))
    vec = pl.BlockSpec((1, d), lambda i: (0, 0))
    return pl.pallas_call(
        body, name=name, grid=(m // tb,),
        in_specs=[pl.BlockSpec((tb, wd), lambda i: (i, 0)) for wd in widths]
        + [pl.BlockSpec(w.shape, lambda i: (0, 0)), row, vec, row],
        out_specs=[row, vec],
        out_shape=[SDS((m, d), F32), SDS((1, d), F32)],
        compiler_params=_params(("arbitrary",)),
    )(*dys, w, x, g, dres)


def _prep_math(pieces, prm, consts, cos, sin):
    fq, fk, sq, sk, sv, cq, ckv, misc = pieces
    gfq, gfk, gsq, gsk, fb, gqa, gkva, gmq, gmk, wq, wkk, wkv = prm
    e64, e64t, e128, e128t, skm, skt, dup, dupt = consts
    cos8 = jnp.concatenate([cos] * HEADS, axis=1)
    sin8 = jnp.concatenate([sin] * HEADS, axis=1)
    fq_n = _gnorm(fq, gfq, e64, e64t, 64)
    fk_n = _gnorm(fk, gfk, e64, e64t, 64)
    ls = jax.nn.log_sigmoid(misc + fb)
    q = _gnorm(_mm(_rms(cq, gqa), wq), gmq, e128, e128t, 96)
    mq = q * cos8 + _rot(q) * sin8
    kva = _rms(ckv, gkva)
    k = _gnorm(_mm(kva, wkk) + _sel(misc, skm, skt), gmk, e128, e128t, 96)
    mk = k * cos8 + _rot(k) * sin8
    mv = _mm(kva, wkv)
    sq_n = _gnorm(sq, gsq, e64, e64t, 64)
    sk_n = _gnorm(sk, gsk, e64[0:128], e64t[:, 0:128], 64)
    skd = _sel(sk_n, dup, dupt)
    svd = _sel(sv, dup, dupt)
    return fq_n, fk_n, ls, mq, mk, mv, sq_n, skd, svd


_PIECES = [(O_FQ, 512), (O_FK, 512), (O_SQ, 512), (O_SK, 128), (O_SV, 128), (O_CQ, 256), (O_CKV, 128), (O_MISC, 128)]
_PRM_SHAPES = [(1, 512), (1, 512), (1, 512), (1, 128), (1, 128), (1, 256), (1, 128), (1, 1024), (1, 1024),
               (256, 1024), (128, 1024), (128, 512)]
_CONST_SHAPES = [(512, 128), (128, 512), (1024, 128), (128, 1024), (128, 1024), (1024, 128), (128, 256), (256, 128)]


def _piece_specs(tb):
    def spec(off, width):
        blk = (GATES_W + off) // width
        return pl.BlockSpec((tb, width), lambda i, blk=blk: (i, blk))
    return [spec(o, w) for o, w in _PIECES] + [spec(O_FV, 512)]


def _full_specs(shapes):
    return [pl.BlockSpec(s, lambda i: (0, 0)) for s in shapes]


def _prep_fwd(proj, prm, consts, cos, sin, name):
    lp = proj.shape[0]
    tb = TILE_ROW
    row = lambda w: pl.BlockSpec((tb, w), lambda i: (i, 0))

    def body(*refs):
        pieces = [r[...] for r in refs[0:8]]
        fv = refs[8][...]
        prm_v = [r[...] for r in refs[9:21]]
        consts_v = [r[...] for r in refs[21:29]]
        cos_v, sin_v = refs[29][...], refs[30][...]
        outs = refs[31:]
        fq_n, fk_n, ls, mq, mk, mv, sq_n, skd, svd = _prep_math(pieces, prm_v, consts_v, cos_v, sin_v)
        for ref, val in zip(outs, (fq_n, fk_n, fv, mq, mk, mv, sq_n, skd, svd)):
            ref[...] = val.astype(BF16)
        outs[9][...] = ls

    widths = [512, 512, 512, 1024, 1024, 512, 512, 256, 256]
    return pl.pallas_call(
        body, name=name, grid=(lp // tb,),
        in_specs=_piece_specs(tb) + _full_specs(_PRM_SHAPES) + _full_specs(_CONST_SHAPES) + [row(128), row(128)],
        out_specs=[row(w) for w in widths] + [row(128)],
        out_shape=[SDS((lp, w), BF16) for w in widths] + [SDS((lp, 128), F32)],
        compiler_params=_params(("parallel",)),
    )(*([proj] * 9), *prm, *consts, cos, sin)


def _prep_bwd(proj, prm, consts, cos, sin, cots, folds, name):
    lp = proj.shape[0]
    tb = TILE_ROW
    row = lambda w: pl.BlockSpec((tb, w), lambda i: (i, 0))
    fold64, fold128 = folds

    def body(*refs):
        pieces = [r[...] for r in refs[0:8]]
        prm_v = [r[...] for r in refs[9:21]]
        consts_v = [r[...] for r in refs[21:29]]
        cos_v, sin_v = refs[29][...], refs[30][...]
        dfq, dfk, dfv, dmq, dmk, dmv, dsq, dskp, dsvp, dls = [r[...] for r in refs[31:41]]
        f64, f128 = refs[41][...], refs[42][...]
        d_ref = refs[43]
        g_refs = refs[44:]

        @pl.when(pl.program_id(0) == 0)
        def _():
            for r in g_refs:
                r[...] = jnp.zeros_like(r)

        def pair_sum(p):
            return jnp.concatenate([p[:, 0:128] + p[:, 128:256], p[:, 256:384] + p[:, 384:512]], axis=1)

        f = lambda pc, pr: _prep_math(pc, pr, consts_v, cos_v, sin_v)
        _, vjp = jax.vjp(f, pieces, prm_v)
        dpc, dprm = vjp((dfq, dfk, dls, dmq, dmk, dmv, dsq, pair_sum(dskp), pair_sum(dsvp)))
        d_fq, d_fk, d_sq, d_sk, d_sv, d_cq, d_ckv, d_misc = dpc
        for off, val in ((O_FQ, d_fq), (O_FK, d_fk), (O_FV, dfv), (O_SQ, d_sq), (O_SK, d_sk), (O_SV, d_sv),
                         (O_CQ, d_cq), (O_CKV, d_ckv), (O_MISC, d_misc)):
            d_ref[:, off:off + val.shape[1]] = val.astype(BF16)
        folded = {0: f64, 1: f64, 2: f64, 3: f64[0:128], 7: f128, 8: f128}
        for idx, (ref, val) in enumerate(zip(g_refs, dprm)):
            if idx in folded:
                ref[...] += _split_dot(jnp.broadcast_to(val, (8, val.shape[1])), folded[idx], 3)
            elif val.shape[0] == 1:
                ref[...] += jnp.broadcast_to(val, ref.shape)
            else:
                ref[...] += val

    g_shapes = [(8, 128), (8, 128), (8, 128), (8, 128), (8, 128), (8, 256), (8, 128), (8, 128), (8, 128),
                (256, 1024), (128, 1024), (128, 512)]
    cot_widths = [512, 512, 512, 1024, 1024, 512, 512, 512, 512, 128]
    return pl.pallas_call(
        body, name=name, grid=(lp // tb,),
        in_specs=(_piece_specs(tb) + _full_specs(_PRM_SHAPES) + _full_specs(_CONST_SHAPES) + [row(128), row(128)]
                  + [row(w) for w in cot_widths] + _full_specs([(512, 128), (1024, 128)])),
        out_specs=[row(OTHER_W)] + _full_specs(g_shapes),
        out_shape=[SDS((lp, OTHER_W), BF16)] + [SDS(s, F32) for s in g_shapes],
        compiler_params=_params(("arbitrary",)),
    )(*([proj] * 9), *prm, *consts, cos, sin, *cots, fold64, fold128)


def _cumsum(xs, reverse, name):
    lp = xs[0].shape[0]
    tb = TILE_MM
    nb = lp // tb
    n_in = len(xs)
    idx = (lambda i: (nb - 1 - i, 0)) if reverse else (lambda i: (i, 0))

    def body(*refs):
        o_ref, carry = refs[n_in], refs[n_in + 1]

        @pl.when(pl.program_id(0) == 0)
        def _():
            carry[...] = jnp.zeros_like(carry)

        x = refs[0][...]
        for r in refs[1:n_in]:
            x = x + r[...]
        r_i = lax.broadcasted_iota(jnp.int32, (tb, tb), 0)
        c_i = lax.broadcasted_iota(jnp.int32, (tb, tb), 1)
        tri = ((c_i >= r_i) if reverse else (c_i <= r_i)).astype(BF16)
        acc, rest = None, x
        for _ in range(3):
            part = rest.astype(BF16)
            rest = rest - part.astype(F32)
            acc = _dot(tri, part) if acc is None else acc + _dot(tri, part)
        o_ref[...] = acc + carry[...]
        carry[...] += jnp.sum(x, axis=0, keepdims=True)

    return pl.pallas_call(
        body, name=name, grid=(nb,),
        in_specs=[pl.BlockSpec((tb, 128), idx)] * n_in,
        out_specs=pl.BlockSpec((tb, 128), idx),
        out_shape=SDS((lp, 128), F32),
        scratch_shapes=[pltpu.VMEM((1, 128), F32)],
        compiler_params=_params(("arbitrary",)),
    )(*xs)


class _Att:
    def __init__(self, mode):
        self.mode = mode
        self.wide = mode == "mla"
        self.qw = 256 if self.wide else 128
        self.scale = (96 if mode == "mla" else 64) ** -0.5

    def resident(self, x, lo, scaled):
        if self.wide:
            return x[:, 0:128], x[:, 128:256]
        if scaled:
            x = x * jnp.asarray(self.scale, x.dtype)
        zero = jnp.zeros_like(x)
        return jnp.where(lo, x, zero), jnp.where(lo, zero, x)

    def moving(self, x):
        return (x[:, 0:128], x[:, 128:256]) if self.wide else (x, x)

    def logits(self, a, b, qpos, kpos, key_decay, slope, masked):
        s = _dot_nt(a, b)
        if self.wide:
            s = s * self.scale
        if self.mode == "fox":
            s = s - key_decay
        if self.mode == "swa":
            s = s - slope * (qpos - kpos).astype(F32)
        if masked:
            ok = kpos <= qpos
            if self.mode == "swa":
                ok = ok & ((kpos < N_META) | (qpos - kpos < WINDOW))
            s = jnp.where(ok, s, NEG)
        return s


def _as_rows(col):
    return jnp.broadcast_to(col, (col.shape[0], 128)).T[0:8, :]


def _halves(x, lo):
    zero = jnp.zeros_like(x)
    return jnp.where(lo, x, zero), jnp.where(lo, zero, x)


def _kv_specs(att, lp, rows):
    if att.mode == "swa":
        return (pl.BlockSpec((rows, 128), lambda g, i: (i if rows != lp else 0, g)),) * 2
    return (pl.BlockSpec((rows, PAIRS * att.qw), lambda g, i: (i if rows != lp else 0, g)),
            pl.BlockSpec((rows, PAIRS * 128), lambda g, i: (i if rows != lp else 0, g)))


def _pair_cols(att, x, pp, width):
    return x if x.shape[1] == width else x[:, pp * width:(pp + 1) * width]


def _att_fwd(att, q, k, v, extra, name):
    lp = q.shape[0]
    t = TILE_ATT
    nq = lp // t
    qw = att.qw
    mode = att.mode
    nh = 2 * PAIRS

    def body(*refs):
        q_ref, k_ref, v_ref = refs[0:3]
        o_ref, lse_ref, row_ref = refs[-3:]
        g, qi = pl.program_id(0), pl.program_id(1)
        lo = lax.broadcasted_iota(jnp.int32, (1, 128), 1) < 64
        q_all = q_ref[...]
        q_heads = [h for pp in range(PAIRS) for h in att.resident(_pair_cols(att, q_all, pp, qw), lo, True)]
        qpos = qi * t + lax.broadcasted_iota(jnp.int32, (t, 1), 0)

        def step(first, cols, carry, masked):
            ks = pl.multiple_of(first, 128)
            kc, vc = k_ref[pl.ds(ks, cols), :], v_ref[pl.ds(ks, cols), :]
            kpos = first + lax.broadcasted_iota(jnp.int32, (1, cols), 1)
            out = []
            for h in range(nh):
                pp = h // 2
                m, l, acc = carry[3 * h:3 * h + 3]
                k_h = att.moving(_pair_cols(att, kc, pp, qw))[h % 2]
                decay = refs[3][h, :, pl.ds(ks, cols)] if mode == "fox" else None
                slope = refs[4][nh * g + h] if mode == "swa" else None
                s = att.logits(q_heads[h], k_h, qpos, kpos, decay, slope, masked)
                m_new = jnp.maximum(m, jnp.max(s, axis=-1, keepdims=True))
                alpha = jnp.exp(m - m_new)
                pe = jnp.exp(s - m_new)
                l = alpha * l + jnp.sum(pe, axis=-1, keepdims=True)
                acc = alpha * acc + _dot(pe.astype(BF16), _pair_cols(att, vc, pp, 128))
                out += [m_new, l, acc]
            return tuple(out)

        init = []
        for h in range(nh):
            if mode == "swa":
                init += [jnp.full((t, 1), refs[3][nh * g + h], F32), jnp.ones((t, 1), F32)]
            else:
                init += [jnp.full((t, 1), NEG, F32), jnp.zeros((t, 1), F32)]
            init.append(jnp.zeros((t, 128), F32))
        if mode == "swa":
            band = jnp.maximum(qi * t - WINDOW, 0)
            carry = lax.fori_loop(0, (band >= 128).astype(jnp.int32), lambda j, c: step(0, 128, c, True), tuple(init))
            carry = step(band, t + WINDOW, carry, True)
        else:
            carry = lax.fori_loop(0, qi // 2, lambda j, c: step(2 * j * t, 2 * t, c, False), tuple(init))
            carry = lax.fori_loop(0, qi % 2, lambda j, c: step((qi - 1) * t, t, c, False), carry)
            carry = step(qi * t, t, carry, True)
        outs = []
        for pp in range(PAIRS):
            (ma, la, acca), (mb, lb, accb) = carry[6 * pp:6 * pp + 3], carry[6 * pp + 3:6 * pp + 6]
            outs.append(jnp.where(lo, acca / la, accb / lb).astype(BF16))
            for h, lse in ((2 * pp, ma + jnp.log(la)), (2 * pp + 1, mb + jnp.log(lb))):
                lse_ref[h] = lse
                row_ref[h] = _as_rows(lse)
        o_ref[...] = jnp.concatenate(outs, axis=1)

    in_specs = [pl.BlockSpec((t, PAIRS * qw), lambda g, i: (i, g)), *_kv_specs(att, lp, lp)]
    if mode == "fox":
        in_specs += [pl.BlockSpec((nh, 1, lp), lambda g, i: (g, 0, 0))]
    if mode == "swa":
        in_specs += [pl.BlockSpec(memory_space=pltpu.SMEM)] * 2
    return pl.pallas_call(
        body, name=name, grid=(4 // PAIRS, nq), in_specs=in_specs,
        out_specs=[pl.BlockSpec((t, PAIRS * 128), lambda g, i: (i, g)), pl.BlockSpec((nh, t, 1), lambda g, i: (g, i, 0)),
                   pl.BlockSpec((nh, 8, t), lambda g, i: (g, 0, i))],
        out_shape=[SDS((lp, 512), BF16), SDS((HEADS, lp, 1), F32), SDS((HEADS, 8, lp), F32)],
        compiler_params=_params(("parallel", "arbitrary")),
    )(q, k, v, *extra)


def _att_dq(att, q, k, v, o, do, lse, extra, name):
    lp = q.shape[0]
    t = TILE_ATT
    nq = lp // t
    qw = att.qw
    mode = att.mode
    nh = 2 * PAIRS

    def body(*refs):
        q_ref, k_ref, v_ref, o_ref, do_ref, lse_ref = refs[0:6]
        n_out = 3 if mode == "mla" else 4
        outs = refs[len(refs) - n_out:]
        dq_ref, delta_ref, row_ref = outs[0:3]
        outs = outs[1:]
        g, qi = pl.program_id(0), pl.program_id(1)
        lo = lax.broadcasted_iota(jnp.int32, (1, 128), 1) < 64
        q_all, do_all = q_ref[...], do_ref[...]
        prod = do_all.astype(F32) * o_ref[...].astype(F32)
        q_heads, do_heads, delta = [], [], []
        for pp in range(PAIRS):
            q_heads += att.resident(_pair_cols(att, q_all, pp, qw), lo, True)
            do_heads += _halves(_pair_cols(att, do_all, pp, 128), lo)
            pr_pp = _pair_cols(att, prod, pp, 128)
            delta += [jnp.sum(jnp.where(lo, pr_pp, 0.0), axis=-1, keepdims=True),
                      jnp.sum(jnp.where(lo, 0.0, pr_pp), axis=-1, keepdims=True)]
        lse_v = [lse_ref[h] for h in range(nh)]
        qpos = qi * t + lax.broadcasted_iota(jnp.int32, (t, 1), 0)

        def step(first, cols, carry, masked):
            ks = pl.multiple_of(first, 128)
            kc, vc = k_ref[pl.ds(ks, cols), :], v_ref[pl.ds(ks, cols), :]
            kpos = first + lax.broadcasted_iota(jnp.int32, (1, cols), 1)
            out = []
            for h in range(nh):
                pp = h // 2
                k_h = att.moving(_pair_cols(att, kc, pp, qw))[h % 2]
                decay = refs[6][h, :, pl.ds(ks, cols)] if mode == "fox" else None
                slope = refs[7][nh * g + h] if mode == "swa" else None
                s = att.logits(q_heads[h], k_h, qpos, kpos, decay, slope, masked)
                pr = jnp.exp(s - lse_v[h])
                ds = pr * (_dot_nt(do_heads[h], _pair_cols(att, vc, pp, 128)) - delta[h])
                out.append(carry[2 * h] + _dot(ds.astype(BF16), k_h))
                out.append(carry[2 * h + 1] + jnp.sum(ds, axis=-1, keepdims=True) if mode == "fox" else carry[2 * h + 1])
            return tuple(out)

        init = (jnp.zeros((t, 128), F32), jnp.zeros((t, 1), F32)) * nh
        if mode == "swa":
            band = jnp.maximum(qi * t - WINDOW, 0)
            carry = lax.fori_loop(0, (band >= 128).astype(jnp.int32), lambda j, c: step(0, 128, c, True), init)
            carry = step(band, t + WINDOW, carry, True)
        else:
            carry = lax.fori_loop(0, qi // 2, lambda j, c: step(2 * j * t, 2 * t, c, False), init)
            carry = lax.fori_loop(0, qi % 2, lambda j, c: step((qi - 1) * t, t, c, False), carry)
            carry = step(qi * t, t, carry, True)
        dq = []
        for pp in range(PAIRS):
            dqa, dca, dqb, dcb = carry[4 * pp:4 * pp + 4]
            dq += [dqa, dqb] if att.wide else [jnp.where(lo, dqa, dqb)]
            if mode == "fox":
                outs[2][2 * pp] = dca
                outs[2][2 * pp + 1] = dcb
        dq_ref[...] = jnp.concatenate(dq, axis=1) * att.scale
        for h in range(nh):
            delta_ref[h] = delta[h]
            row_ref[h] = _as_rows(delta[h])
        if mode == "swa":
            ds_ref = outs[2]

            @pl.when(qi == 0)
            def _():
                ds_ref[...] = jnp.zeros_like(ds_ref)

            lane = lax.broadcasted_iota(jnp.int32, (8, 128), 1)
            acc = jnp.zeros((8, 128), F32)
            for h in range(nh):
                tot = -jnp.sum(jnp.exp(refs[6][nh * g + h] - lse_v[h]) * delta[h])
                acc = acc + jnp.where(lane == h, tot, 0.0)
            ds_ref[0] += acc

    col = pl.BlockSpec((nh, t, 1), lambda g, i: (g, i, 0))
    in_specs = [pl.BlockSpec((t, PAIRS * qw), lambda g, i: (i, g)), *_kv_specs(att, lp, lp),
                pl.BlockSpec((t, PAIRS * 128), lambda g, i: (i, g)), pl.BlockSpec((t, PAIRS * 128), lambda g, i: (i, g)), col]
    out_specs = [pl.BlockSpec((t, PAIRS * qw), lambda g, i: (i, g)), col, pl.BlockSpec((nh, 8, t), lambda g, i: (g, 0, i))]
    out_shape = [SDS((lp, 4 * qw), F32), SDS((HEADS, lp, 1), F32), SDS((HEADS, 8, lp), F32)]
    if mode == "fox":
        in_specs += [pl.BlockSpec((nh, 1, lp), lambda g, i: (g, 0, 0))]
        out_specs.append(col)
        out_shape.append(SDS((HEADS, lp, 1), F32))
    if mode == "swa":
        in_specs += [pl.BlockSpec(memory_space=pltpu.SMEM)] * 2
        out_specs.append(pl.BlockSpec((1, 8, 128), lambda g, i: (g, 0, 0)))
        out_shape.append(SDS((4 // PAIRS, 8, 128), F32))
    return pl.pallas_call(
        body, name=name, grid=(4 // PAIRS, nq), in_specs=in_specs, out_specs=out_specs, out_shape=out_shape,
        compiler_params=_params(("parallel", "arbitrary")),
    )(q, k, v, o, do, lse, *extra)


def _att_dkv(att, q, k, v, do, lse_row, delta_row, extra, name):
    lp = q.shape[0]
    t = TILE_ATT
    nq = lp // t
    qw = att.qw
    mode = att.mode
    nh = 2 * PAIRS

    def body(*refs):
        q_ref, k_ref, v_ref, do_ref, lse_ref, delta_ref = refs[0:6]
        n_out = 3 if mode == "fox" else 2
        outs = refs[len(refs) - n_out:]
        dk_ref, dv_ref = outs[0:2]
        g, kj = pl.program_id(0), pl.program_id(1)
        lo = lax.broadcasted_iota(jnp.int32, (1, 128), 1) < 64
        k_all, v_all = k_ref[...], v_ref[...]
        k_heads, v_heads = [], []
        for pp in range(PAIRS):
            k_heads += att.resident(_pair_cols(att, k_all, pp, qw), lo, True)
            v_heads += _halves(_pair_cols(att, v_all, pp, 128), lo)
        kpos = kj * t + lax.broadcasted_iota(jnp.int32, (t, 1), 0)

        def step(first, cols, carry, masked):
            qs = pl.multiple_of(first, 128)
            qc, doc = q_ref[pl.ds(qs, cols), :], do_ref[pl.ds(qs, cols), :]
            qpos = first + lax.broadcasted_iota(jnp.int32, (1, cols), 1)
            out = []
            for h in range(nh):
                pp = h // 2
                dk_acc, dv_acc, dc_acc = carry[3 * h:3 * h + 3]
                q_h = att.moving(_pair_cols(att, qc, pp, qw))[h % 2]
                do_h = _pair_cols(att, doc, pp, 128)
                decay = refs[6][h] if mode == "fox" else None
                slope = refs[6][nh * g + h] if mode == "swa" else None
                st = att.logits(k_heads[h], q_h, qpos, kpos, decay, slope, masked)
                pt = jnp.exp(st - lse_ref[h, 0:1, pl.ds(qs, cols)])
                dst = pt * (_dot_nt(v_heads[h], do_h) - delta_ref[h, 0:1, pl.ds(qs, cols)])
                dv_acc = dv_acc + _dot(pt.astype(BF16), do_h)
                dk_acc = dk_acc + _dot(dst.astype(BF16), q_h)
                if mode == "fox":
                    dc_acc = dc_acc - jnp.sum(dst, axis=-1, keepdims=True)
                out += [dk_acc, dv_acc, dc_acc]
            return tuple(out)

        init = (jnp.zeros((t, 128), F32), jnp.zeros((t, 128), F32), jnp.zeros((t, 1), F32)) * nh
        if mode == "swa":
            carry = lax.fori_loop(0, jnp.where(kj == 0, nq, 0), lambda qi, c: step(qi * t, t, c, True), init)
            near = jnp.minimum(kj * t, lp - (t + WINDOW))
            carry = lax.fori_loop(0, (kj > 0).astype(jnp.int32), lambda j, c: step(near, t + WINDOW, c, True), carry)
        else:
            carry = step(kj * t, t, init, True)
            rest = nq - 1 - kj
            carry = lax.fori_loop(0, rest // 2, lambda j, c: step((kj + 1 + 2 * j) * t, 2 * t, c, False), carry)
            carry = lax.fori_loop(0, rest % 2, lambda j, c: step((nq - 1) * t, t, c, False), carry)
        dk, dv = [], []
        for pp in range(PAIRS):
            dka, dva, dca, dkb, dvb, dcb = carry[6 * pp:6 * pp + 6]
            dk += [dka, dkb] if att.wide else [jnp.where(lo, dka, dkb)]
            dv.append(jnp.where(lo, dva, dvb))
            if mode == "fox":
                outs[2][2 * pp] = dca
                outs[2][2 * pp + 1] = dcb
        dk_ref[...] = jnp.concatenate(dk, axis=1) * att.scale
        dv_ref[...] = jnp.concatenate(dv, axis=1)

    rowv = pl.BlockSpec((nh, 8, lp), lambda g, j: (g, 0, 0))
    col = pl.BlockSpec((nh, t, 1), lambda g, j: (g, j, 0))
    in_specs = [pl.BlockSpec((lp, PAIRS * qw), lambda g, j: (0, g)), *_kv_specs(att, lp, t),
                pl.BlockSpec((lp, PAIRS * 128), lambda g, j: (0, g)), rowv, rowv]
    out_specs = [pl.BlockSpec((t, PAIRS * qw), lambda g, j: (j, g)), pl.BlockSpec((t, PAIRS * 128), lambda g, j: (j, g))]
    out_shape = [SDS((lp, 4 * qw), F32), SDS((lp, 512), F32)]
    if mode == "fox":
        in_specs += [col]
        out_specs.append(col)
        out_shape.append(SDS((HEADS, lp, 1), F32))
    if mode == "swa":
        in_specs += [pl.BlockSpec(memory_space=pltpu.SMEM)]
    return pl.pallas_call(
        body, name=name, grid=(4 // PAIRS, nq), in_specs=in_specs, out_specs=out_specs, out_shape=out_shape,
        compiler_params=_params(("parallel", "arbitrary")),
    )(q, k, v, do, lse_row, delta_row, *extra)


def _post_fwd(h, proj, outs, wb, wo, name):
    lp, d = h.shape
    tb = TILE_POST
    row = lambda w: pl.BlockSpec((tb, w), lambda i: (i, 0))

    def body(h_ref, g0, g1, g2, oa, ob, oc, wb_ref, wo_ref, o_ref):
        merged = jnp.zeros((tb, d), F32)
        for n, (g_ref, br) in enumerate(((g0, oa), (g1, ob), (g2, oc))):
            merged = merged + jax.nn.sigmoid(g_ref[...]) * _dot(br[...], wb_ref[n])
        o_ref[...] = h_ref[...] + _dot(merged.astype(BF16), wo_ref[...])

    gate = lambda n: pl.BlockSpec((tb, d), lambda i, n=n: (i, n))
    return pl.pallas_call(
        body, name=name, grid=(lp // tb,),
        in_specs=[row(d), gate(0), gate(1), gate(2), row(512), row(512), row(512),
                  pl.BlockSpec((3, 512, d), lambda i: (0, 0, 0)), pl.BlockSpec((d, d), lambda i: (0, 0))],
        out_specs=row(d), out_shape=SDS((lp, d), F32),
        compiler_params=_params(("parallel",)),
    )(h, proj, proj, proj, *outs, wb, wo)


def _post_bwd(dh, proj, outs, wb, wo, name):
    lp, d = dh.shape
    tb = TILE_POST
    row = lambda w: pl.BlockSpec((tb, w), lambda i: (i, 0))

    def body(dh_ref, g0, g1, g2, oa, ob, oc, wb_ref, wo_ref, dg_ref, doa, dob, doc, dwb_ref, dwo_ref):
        @pl.when(pl.program_id(0) == 0)
        def _():
            dwb_ref[...] = jnp.zeros_like(dwb_ref)
            dwo_ref[...] = jnp.zeros_like(dwo_ref)

        dhb = dh_ref[...].astype(BF16)
        dm = _dot_nt(dhb, wo_ref[...])
        merged = jnp.zeros((tb, d), F32)
        for n, (g_ref, br, do_ref) in enumerate(((g0, oa, doa), (g1, ob, dob), (g2, oc, doc))):
            gate = jax.nn.sigmoid(g_ref[...])
            o_n = br[...]
            y = _dot(o_n, wb_ref[n])
            merged = merged + gate * y
            dy = (dm * gate).astype(BF16)
            dg_ref[:, n * d:(n + 1) * d] = (dm * y * gate * (1.0 - gate)).astype(BF16)
            do_ref[...] = _dot_nt(dy, wb_ref[n]).astype(BF16)
            dwb_ref[n] += _dot_tn(o_n, dy)
        dwo_ref[...] += _dot_tn(merged.astype(BF16), dhb)

    gate = lambda n: pl.BlockSpec((tb, d), lambda i, n=n: (i, n))
    wb_spec = pl.BlockSpec((3, 512, d), lambda i: (0, 0, 0))
    wo_spec = pl.BlockSpec((d, d), lambda i: (0, 0))
    return pl.pallas_call(
        body, name=name, grid=(lp // tb,),
        in_specs=[row(d), gate(0), gate(1), gate(2), row(512), row(512), row(512), wb_spec, wo_spec],
        out_specs=[row(GATES_W), row(512), row(512), row(512), wb_spec, wo_spec],
        out_shape=[SDS((lp, GATES_W), BF16)] + [SDS((lp, 512), BF16)] * 3 + [SDS((3, 512, d), F32), SDS((d, d), F32)],
        compiler_params=_params(("arbitrary",)),
    )(dh, proj, proj, proj, *outs, wb, wo)


def _shift_down(x, halo, n, first):
    rows = lax.broadcasted_iota(jnp.int32, x.shape, 0)
    edge = jnp.concatenate([pltpu.roll(halo, n, 0), jnp.zeros((x.shape[0] - 8, x.shape[1]), F32)], axis=0)
    edge = jnp.where(first, 0.0, edge)
    return jnp.where(rows < n, edge, pltpu.roll(x, n, 0))


def _shift_up(x, halo, n, last):
    tb = x.shape[0]
    rows = lax.broadcasted_iota(jnp.int32, x.shape, 0)
    edge = jnp.concatenate([jnp.zeros((tb - 8, x.shape[1]), F32), pltpu.roll(halo, 8 - n, 0)], axis=0)
    edge = jnp.where(last, 0.0, edge)
    return jnp.where(rows >= tb - n, edge, pltpu.roll(x, tb - n, 0))


def _conv(u, halo, w_ref, b_ref, first):
    taps = (_shift_down(u, halo, 2, first), _shift_down(u, halo, 1, first), u)
    c = b_ref[...] + w_ref[0:1, :] * taps[0] + w_ref[1:2, :] * taps[1] + w_ref[2:3, :] * taps[2]
    return c, taps


def _ffn_specs(tb, f):
    hb = tb // 8
    cur = lambda c: pl.BlockSpec((tb, f), lambda i, c=c: (i, c))
    prev = lambda c: pl.BlockSpec((8, f), lambda i, c=c: (jnp.maximum(i * hb - 1, 0), c))
    vec = lambda r, c: pl.BlockSpec((r, f), lambda i, c=c: (0, c))
    return cur, prev, vec


def _ffn_act_fwd(u, cw, cb, name):
    lp = u.shape[0]
    f = D_FF
    tb = TILE_ROW
    cur, prev, vec = _ffn_specs(tb, f)

    def body(ug, uv, hg, hv, wg, wv, bg, bv, o_ref):
        first = pl.program_id(0) == 0
        cg, _ = _conv(ug[...], hg[...], wg, bg, first)
        cv, _ = _conv(uv[...], hv[...], wv, bv, first)
        o_ref[...] = (cg * jax.nn.sigmoid(cg) * cv).astype(BF16)

    return pl.pallas_call(
        body, name=name, grid=(lp // tb,),
        in_specs=[cur(0), cur(1), prev(0), prev(1), vec(8, 0), vec(8, 1), vec(1, 0), vec(1, 1)],
        out_specs=pl.BlockSpec((tb, f), lambda i: (i, 0)), out_shape=SDS((lp, f), BF16),
        compiler_params=_params(("parallel",)),
    )(u, u, u, u, cw, cw, cb, cb)


def _ffn_act_bwd_conv(u, dact, cw, cb, name):
    lp = u.shape[0]
    f = D_FF
    tb = TILE_CONV_BWD
    cur, prev, vec = _ffn_specs(tb, f)

    def body(ug, uv, hg, hv, wg, wv, bg, bv, da_ref, dc_ref, dw_ref, db_ref):
        first = pl.program_id(0) == 0

        @pl.when(first)
        def _():
            dw_ref[...] = jnp.zeros_like(dw_ref)
            db_ref[...] = jnp.zeros_like(db_ref)

        cg, tg = _conv(ug[...], hg[...], wg, bg, first)
        cv, tv = _conv(uv[...], hv[...], wv, bv, first)
        da = da_ref[...]
        sg = jax.nn.sigmoid(cg)
        dcg = da * cv * sg * (1.0 + cg * (1.0 - sg))
        dcv = da * cg * sg
        for c, (dc, taps) in enumerate(((dcg, tg), (dcv, tv))):
            dc_ref[:, c * f:(c + 1) * f] = dc
            for n in range(3):
                dw_ref[n:n + 1, c * f:(c + 1) * f] += jnp.sum(dc * taps[n], axis=0, keepdims=True)
            db_ref[0:1, c * f:(c + 1) * f] += jnp.sum(dc, axis=0, keepdims=True)

    acc = pl.BlockSpec((8, 2 * f), lambda i: (0, 0))
    return pl.pallas_call(
        body, name=name, grid=(lp // tb,),
        in_specs=[cur(0), cur(1), prev(0), prev(1), vec(8, 0), vec(8, 1), vec(1, 0), vec(1, 1),
                  pl.BlockSpec((tb, f), lambda i: (i, 0))],
        out_specs=[pl.BlockSpec((tb, 2 * f), lambda i: (i, 0)), acc, acc],
        out_shape=[SDS((lp, 2 * f), F32), SDS((8, 2 * f), F32), SDS((8, 2 * f), F32)],
        compiler_params=_params(("arbitrary",)),
    )(u, u, u, u, cw, cw, cb, cb, dact)


def _ffn_act_bwd_in(dc, cw, name):
    lp = dc.shape[0]
    f2 = 2 * D_FF
    tb = TILE_ROW
    nb = lp // tb
    hb = tb // 8

    def body(dc_ref, n_ref, w_ref, o_ref):
        last = pl.program_id(0) == nb - 1
        dcv, halo = dc_ref[...], n_ref[...]
        du = (w_ref[2:3, :] * dcv + w_ref[1:2, :] * _shift_up(dcv, halo, 1, last)
              + w_ref[0:1, :] * _shift_up(dcv, halo, 2, last))
        o_ref[...] = du.astype(BF16)

    cur = pl.BlockSpec((tb, f2), lambda i: (i, 0))
    return pl.pallas_call(
        body, name=name, grid=(nb,),
        in_specs=[cur, pl.BlockSpec((8, f2), lambda i: (jnp.minimum((i + 1) * hb, nb * hb - 1), 0)),
                  pl.BlockSpec((8, f2), lambda i: (0, 0))],
        out_specs=cur, out_shape=SDS((lp, f2), BF16),
        compiler_params=_params(("parallel",)),
    )(dc, dc, cw)


def _loss_head(y, target, n_real, name):
    lp, d = y.shape
    tb = TILE_MM

    def body(y_ref, t_ref, dy_ref, loss_ref):
        i = pl.program_id(0)

        @pl.when(i == 0)
        def _():
            loss_ref[...] = jnp.zeros_like(loss_ref)

        rows = i * tb + lax.broadcasted_iota(jnp.int32, (tb, 1), 0)
        real = (rows >= N_META) & (rows < N_META + n_real)
        diff = jnp.where(real, y_ref[...] - t_ref[...], 0.0)
        dy_ref[...] = diff * (1.0 / d)
        loss_ref[...] += (0.5 / d) * jnp.sum(diff * diff).reshape(1, 1)

    row = pl.BlockSpec((tb, d), lambda i: (i, 0))
    return pl.pallas_call(
        body, name=name, grid=(lp // tb,), in_specs=[row, row],
        out_specs=[row, pl.BlockSpec((1, 1), lambda i: (0, 0))],
        out_shape=[SDS((lp, d), F32), SDS((1, 1), F32)],
        compiler_params=_params(("arbitrary",)),
    )(y, target)


def _pad_lanes(v, width, at=0):
    return jnp.pad(v.astype(F32), (at, width - at - v.shape[0]))[None, :]


def _mix_params(w, big, l):
    b = lambda a: a.astype(BF16)
    win = big["w_in"]
    fq, fk, fv, ff, cq, ckv, kr, sq, sk, sv, gates = jnp.split(
        win, [512, 1024, 1536, 1544, 1800, 1928, 1960, 2472, 2600, 2728], axis=1)
    misc = jnp.concatenate([kr, ff, jnp.zeros((D_MODEL, 88), win.dtype)], axis=1)
    w_in = b(jnp.concatenate([gates, fq, fk, fv, sq, sk, sv, cq, ckv, misc], axis=1))
    wq = jnp.pad(big["mla_w_q_up"].reshape(256, HEADS, 96), ((0, 0), (0, 0), (0, 32))).reshape(256, 1024)
    wkv = big["mla_w_kv_up"].reshape(128, HEADS, 128)
    wkk = jnp.pad(wkv[:, :, :64], ((0, 0), (0, 0), (0, 64))).reshape(128, 1024)
    wkvv = wkv[:, :, 64:].reshape(128, 512)
    tile = lambda g, n: jnp.tile(g.astype(F32), n)[None, :]
    prm = [tile(w["fox_q_g"][l], 8), tile(w["fox_k_g"][l], 8), tile(w["swa_q_g"][l], 8), tile(w["swa_k_g"][l], 2),
           _pad_lanes(w["fox_forget_b"][l], 128, FF_LANE), w["mla_q_a_g"][l][None, :], w["mla_kv_a_g"][l][None, :],
           tile(jnp.pad(w["mla_q_g"][l], (0, 32)), 8), tile(jnp.pad(w["mla_k_g"][l], (0, 32)), 8),
           wq.astype(F32), wkk.astype(F32), wkvv.astype(F32)]
    return dict(g1=w["norm1_g"][l][None, :], w_in=w_in, prm=prm, sinks=w["swa_sinks"][l].astype(F32),
                wb=b(big["w_branch"]), wo=b(big["w_o"]))


def _ffn_params(w, big, l):
    cw = jnp.pad(w["ffn_conv_w"][l].astype(F32), ((0, 5), (0, 0)))
    return dict(g2=w["norm2_g"][l][None, :], w_up=big["ffn_w_up"].astype(BF16), cw=cw,
                cb=w["ffn_conv_b"][l][None, :].astype(F32), w_down=big["ffn_w_down"].astype(BF16))


def _cols(c):
    ct = c[:, FF_LANE:FF_LANE + HEADS].T
    return ct[:, :, None], ct[:, None, :]


def _from_cols(col):
    return jnp.pad(col[:, :, 0].T, ((0, 0), (FF_LANE, 128 - FF_LANE - HEADS)))


def _layer_fwd_mix(h, lw, consts, cos, sin, slopes, l):
    tag = f"l{l}_"
    xn, proj = _norm_matmul(h, lw["g1"], lw["w_in"], IN_W // 2, tag + "in_proj")
    fq, fk, fv, mq, mk, mv, sq, skd, svd, ls = _prep_fwd(proj, lw["prm"], consts, cos, sin, tag + "prep")
    c = _cumsum([ls], False, tag + "decay_cumsum")
    c_col, c_row = _cols(c)
    oa, *lse_a = _att_fwd(_Att("fox"), fq, fk, fv, (c_row,), tag + "fox_fwd")
    ob, *lse_b = _att_fwd(_Att("mla"), mq, mk, mv, (), tag + "mla_fwd")
    oc, *lse_c = _att_fwd(_Att("swa"), sq, skd, svd, (lw["sinks"], slopes), tag + "swa_fwd")
    h2 = _post_fwd(h, proj, (oa, ob, oc), lw["wb"], lw["wo"], tag + "merge")
    saved = dict(h=h, xn=xn, proj=proj, q=(fq, mq, sq), k=(fk, mk, skd), v=(fv, mv, svd), c=(c_col, c_row),
                 o=(oa, ob, oc), lse=(lse_a, lse_b, lse_c), h2=h2)
    return h2, saved


def _layer_fwd_ffn(h2, lw, l):
    tag = f"l{l}_"
    xn2, u = _norm_matmul(h2, lw["g2"], lw["w_up"], D_FF, tag + "ffn_up")
    act = _ffn_act_fwd(u, lw["cw"], lw["cb"], tag + "ffn_act")
    h3 = _matmul_residual(act, lw["w_down"], h2, tag + "ffn_down")
    return h3, dict(xn2=xn2, u=u, act=act)


def _layer_bwd_ffn(dh3, lw, sv, l):
    tag = f"l{l}_"
    f = D_FF
    dact = _matmul_nt(dh3, lw["w_down"], f, tag + "ffn_down_dx")
    dw_down = _matmul_tn(sv["act"], dh3, D_MODEL, tag + "ffn_down_dw")
    dc, dcw, dcb = _ffn_act_bwd_conv(sv["u"], dact, lw["cw"], lw["cb"], tag + "ffn_act_dc")
    du = _ffn_act_bwd_in(dc, lw["cw"], tag + "ffn_act_du")
    dw_up = _matmul_tn(sv["xn2"], du, f, tag + "ffn_up_dw")
    dh2, dg2 = _norm_matmul_bwd([du], lw["w_up"], sv["h2"], lw["g2"], dh3, tag + "ffn_up_dx")
    g = dict(norm2_g=dg2[0], ffn_w_up=dw_up, ffn_conv_w=dcw[0:3], ffn_conv_b=dcb[0], ffn_w_down=dw_down)
    return dh2, g


def _layer_bwd_mix(dh2, lw, sv, consts, folds, cos, sin, slopes, l, hook=None, merge_hook=None):
    tag = f"l{l}_"
    dgates, doa, dob, doc, dwb, dwo = _post_bwd(dh2, sv["proj"], sv["o"], lw["wb"], lw["wo"], tag + "merge_bwd")
    c_col, c_row = sv["c"]
    tick = merge_hook({"w_branch": dwb, "w_o": dwo}) if merge_hook else None
    if tick is not None:
        c_row = c_row + tick
    extras = ((c_row,), (), (lw["sinks"], slopes))
    extras_kv = ((c_col,), (), (slopes,))
    grads = []
    for n, (mode, do) in enumerate((("fox", doa), ("mla", dob), ("swa", doc))):
        att = _Att(mode)
        q, k, v = sv["q"][n], sv["k"][n], sv["v"][n]
        lse_col, lse_row = sv["lse"][n]
        res = _att_dq(att, q, k, v, sv["o"][n], do, lse_col, extras[n], tag + mode + "_dq")
        dq, delta_row = res[0], res[2]
        res_kv = _att_dkv(att, q, k, v, do, lse_row, delta_row, extras_kv[n], tag + mode + "_dkv")
        grads.append((dq, res_kv[0], res_kv[1], res[3:], res_kv[2:]))
    (dfq, dfk, dfv, (dcq,), (dck,)), (dmq, dmk, dmv, _, _), (dsq, dskp, dsvp, (dsink,), _) = grads
    dls = _cumsum([_from_cols(dcq), _from_cols(dck)], True, tag + "decay_cumsum_bwd")
    res = _prep_bwd(sv["proj"], lw["prm"], consts, cos, sin,
                    (dfq, dfk, dfv, dmq, dmk, dmv, dsq, dskp, dsvp, dls), folds, tag + "prep_bwd")
    dother, pg = res[0], res[1:]
    dw_g = _matmul_tn(sv["xn"], dgates, GATES_W, tag + "in_proj_dw_gates")
    dw_o = _matmul_tn(sv["xn"], dother, OTHER_W, tag + "in_proj_dw_other")
    d_in = jnp.concatenate([
        dw_o[:, O_FQ:O_FV + 512], dw_o[:, O_MISC + FF_LANE:O_MISC + FF_LANE + 8], dw_o[:, O_CQ:O_CQ + 256],
        dw_o[:, O_CKV:O_CKV + 128], dw_o[:, O_MISC:O_MISC + 32], dw_o[:, O_SQ:O_SQ + 512], dw_o[:, O_SK:O_SK + 128],
        dw_o[:, O_SV:O_SV + 128], dw_g], axis=1)
    d_wq = pg[9].reshape(256, HEADS, 128)[:, :, :96].reshape(256, 768)
    d_wkv = jnp.concatenate([pg[10].reshape(128, HEADS, 128)[:, :, :64], pg[11].reshape(128, HEADS, 64)],
                            axis=2).reshape(128, 1024)
    g = dict(
        w_in=d_in, fox_forget_b=pg[4][0, FF_LANE:FF_LANE + 8], fox_q_g=pg[0][0, :64],
        fox_k_g=pg[1][0, :64], mla_q_a_g=pg[5][0], mla_w_q_up=d_wq, mla_kv_a_g=pg[6][0], mla_w_kv_up=d_wkv,
        mla_q_g=pg[7][0, :96], mla_k_g=pg[8][0, :96], swa_q_g=pg[2][0, :64], swa_k_g=pg[3][0, :64],
        swa_sinks=dsink[:, 0, 0:2 * PAIRS].reshape(HEADS), w_branch=dwb, w_o=dwo)
    tick = hook(g) if hook else None
    g1 = lw["g1"] if tick is None else lw["g1"] + tick
    dh, dg1 = _norm_matmul_bwd([dgates, dother], lw["w_in"], sv["h"], g1, dh2, tag + "in_proj_dx")
    g["norm1_g"] = dg1[0]
    return dh, g


_MIX_BIG = ("w_in", "mla_w_q_up", "mla_w_kv_up", "w_branch", "w_o")
_FFN_BIG = ("ffn_w_up", "ffn_w_down")


def _local_step(x, target, w, hook=None, fetch=None):
    if fetch is None:
        fetch = lambda l, stage, after: {n: w[n][l] for n in (_MIX_BIG if stage == "mix" else _FFN_BIG)}
    seq = x.shape[0]
    length = N_META + seq
    lp = -(-length // ROW_ALIGN) * ROW_ALIGN
    pad = lp - length
    h = jnp.concatenate([w["meta_tokens"].astype(F32), x, jnp.zeros((pad, D_MODEL), F32)], axis=0)
    tgt = jnp.pad(target, ((N_META, pad), (0, 0)))
    consts = _consts()
    folds = (_fold_matrix(512, 64), _fold_matrix(1024, 128))
    cos, sin = _rope_tables(lp)
    slopes = jnp.asarray(2.0 ** (-8.0 * np.arange(1, HEADS + 1, dtype=np.float32) / HEADS), F32)
    lws, saved = [], []
    for l in range(DEPTH):
        lw = _mix_params(w, fetch(l, "mix", h), l)
        h, sv = _layer_fwd_mix(h, lw, consts, cos, sin, slopes, l)
        lw.update(_ffn_params(w, fetch(l, "ffn", h), l))
        h, sv_ffn = _layer_fwd_ffn(h, lw, l)
        lws.append(lw)
        saved.append({**sv, **sv_ffn})
    dh, loss = _loss_head(h, tgt, seq, "loss_head")
    grads = [None] * DEPTH
    for l in reversed(range(DEPTH)):
        dh, g_ffn = _layer_bwd_ffn(dh, lws[l], saved[l], l)
        tick = hook(l, "ffn", g_ffn) if hook else None
        if tick is not None:
            lws[l]["sinks"] = lws[l]["sinks"] + tick
        mix_hook = (lambda g, l=l, g_ffn=g_ffn: hook(l, "mix", {**g_ffn, **g})) if hook else None
        merge_hook = (lambda g, l=l: hook(l, "merge", g)) if hook else None
        dh, g_mix = _layer_bwd_mix(dh, lws[l], saved[l], consts, folds, cos, sin, slopes, l, mix_hook, merge_hook)
        grads[l] = {**g_ffn, **g_mix}
    return loss, dh[N_META:length], dh[:N_META], grads


def _place():
    return lax.axis_index("x"), lax.axis_index("y"), lax.axis_index("c")


def _flip(pos, k):
    x, y, c = pos
    return (1 - x if k & 4 else x, 1 - y if k & 2 else y, 1 - c if k & 1 else c)


def _index(pos):
    return 4 * pos[0] + 2 * pos[1] + pos[2]


def _gather(tensors, name):
    n_t = len(tensors)

    def body(*refs):
        ins, outs = refs[:n_t], refs[n_t:2 * n_t]
        send_sems, recv_sems, local_sems = refs[2 * n_t:]
        x, y, c = _place()
        me, sibling = (x, y, c), (x, y, 1 - c)
        chips = [(1 - x, y), (x, 1 - y), (1 - x, 1 - y)]

        def copy(t, k, block, to, src=None):
            dst = outs[t].at[_index(block)]
            return pltpu.make_async_remote_copy(
                src_ref=dst if src is None else src, dst_ref=dst, send_sem=send_sems.at[t, k],
                recv_sem=recv_sems.at[t, k], device_id=to, device_id_type=pl.DeviceIdType.MESH)

        local, sent = [], []
        for t in range(n_t):
            local.append(pltpu.make_async_copy(ins[t], outs[t].at[_index(me)], local_sems.at[t]))
            local[-1].start()
            sent.append(copy(t, 0, me, sibling, src=ins[t]))
            sent += [copy(t, 1 + j, me, (*chip, c), src=ins[t]) for j, chip in enumerate(chips)]
        for cp in sent:
            cp.start()
        for j, chip in enumerate(chips):
            for t in range(n_t):
                copy(t, 1 + j, (*chip, c), me).wait_recv()
                sent.append(copy(t, 4 + j, (*chip, c), sibling))
                sent[-1].start()
        for t in range(n_t):
            copy(t, 0, sibling, me).wait_recv()
            for j, chip in enumerate(chips):
                copy(t, 4 + j, (*chip, 1 - c), me).wait_recv()
        for cp in sent:
            cp.wait_send()
        for cp in local:
            cp.wait()

    any_spec = pl.BlockSpec(memory_space=pl.ANY)
    return pl.pallas_call(
        body, name=name, in_specs=[any_spec] * n_t, out_specs=[any_spec] * n_t,
        out_shape=[SDS((N_DEV,) + a.shape, a.dtype) for a in tensors],
        scratch_shapes=[pltpu.SemaphoreType.DMA((n_t, N_DEV - 1)), pltpu.SemaphoreType.DMA((n_t, N_DEV - 1)),
                        pltpu.SemaphoreType.DMA((n_t,))],
    )(*tensors)


def _exchange_start(tensors, name, gather=False, after=None):
    n_t = len(tensors)

    def body(*refs):
        ins, lands = refs[:n_t], refs[n_t:2 * n_t]
        send_sem, recv_sem = refs[2 * n_t + 1:2 * n_t + 3]
        token = refs[-1]
        me = _place()
        mine = _index(me)
        for t in range(n_t):
            for k in range(1, N_DEV):
                peer = _flip(me, k)
                pltpu.make_async_remote_copy(
                    src_ref=ins[t] if gather else ins[t].at[_index(peer)], dst_ref=lands[t].at[mine],
                    send_sem=send_sem, recv_sem=recv_sem, device_id=peer, device_id_type=pl.DeviceIdType.MESH).start()
        token[...] = jnp.zeros_like(token)

    hbm = pl.BlockSpec(memory_space=pltpu.HBM)
    sem = pl.BlockSpec(memory_space=pltpu.SEMAPHORE)
    one = pltpu.SemaphoreType.DMA(())
    land_shape = lambda a: ((N_DEV,) + a.shape) if gather else a.shape
    bufs = ([pltpu.HBM(a.shape, a.dtype) for a in tensors] + [pltpu.HBM(land_shape(a), a.dtype) for a in tensors])
    after = jnp.zeros((8, 128), F32) if after is None else after
    outs = pl.pallas_call(
        body, name=name, in_specs=[hbm] * (2 * n_t) + [pl.BlockSpec(memory_space=pl.ANY)],
        out_specs=[sem, sem] + [hbm] * (2 * n_t) + [pl.BlockSpec(memory_space=pltpu.VMEM)],
        out_shape=[one, one] + bufs + [SDS((8, 128), F32)],
        input_output_aliases={i: 2 + i for i in range(2 * n_t)},
        compiler_params=pltpu.CompilerParams(has_side_effects=pltpu.SideEffectType.DATAFLOW_SIDE_EFFECTING),
    )(*[pltpu.with_memory_space_constraint(a, pltpu.HBM) for a in tensors],
      *[pltpu.with_memory_space_constraint(lax.empty(land_shape(a), a.dtype), pltpu.HBM) for a in tensors], after)
    return outs[:-1], outs[-1][0, 0]


def _exchange_wait(state, after, name, gather=False):
    n_t = (len(state) - 2) // 2

    def body(*refs):
        send_sem, recv_sem = refs[0:2]
        ins, lands = refs[2:2 + n_t], refs[2 + n_t:2 + 2 * n_t]
        me = _place()
        for t in range(n_t):
            for k in range(1, N_DEV):
                peer = _flip(me, k)
                copy = pltpu.make_async_remote_copy(
                    src_ref=ins[t] if gather else ins[t].at[_index(peer)], dst_ref=lands[t].at[_index(peer)],
                    send_sem=send_sem, recv_sem=recv_sem, device_id=peer, device_id_type=pl.DeviceIdType.MESH)
                copy.wait_send()
                copy.wait_recv()

    hbm = pl.BlockSpec(memory_space=pltpu.HBM)
    sem = pl.BlockSpec(memory_space=pltpu.SEMAPHORE)
    bufs = [pltpu.HBM(a.shape, a.dtype) for a in state[2:]]
    outs = pl.pallas_call(
        body, name=name, in_specs=[sem, sem] + [hbm] * (2 * n_t) + [pl.BlockSpec(memory_space=pl.ANY)],
        out_specs=[hbm] * (2 * n_t), out_shape=bufs,
        input_output_aliases={2 + i: i for i in range(2 * n_t)},
        compiler_params=pltpu.CompilerParams(has_side_effects=pltpu.SideEffectType.DATAFLOW_SIDE_EFFECTING),
    )(*state, after)
    return outs[n_t:]


def _sum_slots(parts, name):
    n, rows, w = parts.shape
    tb = 8

    def body(p_ref, o_ref):
        acc = p_ref[0].astype(F32)
        for s in range(1, n):
            acc = acc + p_ref[s].astype(F32)
        o_ref[...] = acc

    return pl.pallas_call(
        body, name=name, grid=(rows // tb,),
        in_specs=[pl.BlockSpec((n, tb, w), lambda i: (0, i, 0))], out_specs=pl.BlockSpec((tb, w), lambda i: (i, 0)),
        out_shape=SDS((rows, w), F32), compiler_params=_params(("parallel",)),
    )(parts)


def _adamw(wt, m, v, parts, name, own=None):
    shape = wt.shape
    parts = parts if isinstance(parts, (list, tuple)) else [parts]
    n, w = parts[0].shape[0], shape[-1]
    rows = math.prod(shape[:-1])
    per = rows // len(parts)
    step = 16 if parts[0].dtype == BF16 else 8
    tb = max([t for t in range(step, 257, step) if per % t == 0] or [per])
    nb = per // tb
    c1 = 1.0 / (1.0 - ADAM_B1 ** ADAM_STEP)
    c2 = 1.0 / (1.0 - ADAM_B2 ** ADAM_STEP)
    state = [a.reshape(rows, w) for a in (wt, m, v)]
    n_in = 4 if own is None else 5
    outs = None
    for l in reversed(range(len(parts))):
        def body(*refs):
            idx_ref = None if own is None else refs[0]
            w_ref, m_ref, v_ref, p_ref = refs[n_in - 4:n_in] if own is None else refs[1:5]
            g_out, d_out, m_out, v_out = refs[-4:]
            g = None
            for s in range(n):
                term = p_ref[s] if own is None else jnp.where(idx_ref[0] == s, refs[5][0], p_ref[s])
                g = term.astype(F32) if g is None else g + term.astype(F32)
            m_new = ADAM_B1 * m_ref[...] + (1.0 - ADAM_B1) * g
            v_new = ADAM_B2 * v_ref[...] + (1.0 - ADAM_B2) * (g * g)
            g_out[...] = g
            m_out[...] = m_new
            v_out[...] = v_new
            d_out[...] = -ADAM_LR * ((m_new * c1) / (jnp.sqrt(v_new * c2) + ADAM_EPS) + ADAM_WD * w_ref[...])

        row = pl.BlockSpec((tb, w), lambda i, *_, l=l: (l * nb + i, 0))
        in_specs = [row, row, row, pl.BlockSpec((n, tb, w), lambda i, *_: (0, i, 0))]
        args = [*state, parts[l].reshape(n, per, w)]
        if own is not None:
            in_specs.append(pl.BlockSpec((1, tb, w), lambda i, idx: (idx[0], i, 0)))
            args.append(own[l].reshape(n, per, w))
        prev = [] if outs is None else list(outs)
        in_specs += [pl.BlockSpec(memory_space=pl.ANY)] * len(prev)
        n_pre = 0 if own is None else 1
        call = dict(name=f"{name}_{l}", out_shape=[SDS((rows, w), F32)] * 4,
                    input_output_aliases={n_pre + len(args) + k: k for k in range(len(prev))},
                    compiler_params=_params(("parallel",)))
        if own is None:
            outs = pl.pallas_call(body, grid=(nb,), in_specs=in_specs, out_specs=[row] * 4, **call)(*args, *prev)
        else:
            spec = pltpu.PrefetchScalarGridSpec(num_scalar_prefetch=1, grid=(nb,), in_specs=in_specs, out_specs=[row] * 4)
            idx = jnp.reshape(_index(_place()), (1,)).astype(jnp.int32)
            outs = pl.pallas_call(body, grid_spec=spec, **call)(idx, *args, *prev)
    return [o.reshape(shape) for o in outs]


_BIG = [("w_in", 2), ("mla_w_q_up", 2), ("mla_w_kv_up", 2), ("w_branch", 3), ("w_o", 1), ("ffn_w_up", 2), ("ffn_w_down", 1)]
_SMALL_SHARDED = [("meta_tokens", 1), ("ffn_conv_w", 2)]
_REPLICATED = ["norm1_g", "fox_forget_b", "fox_q_g", "fox_k_g", "mla_q_a_g", "mla_kv_a_g", "mla_q_g", "mla_k_g",
               "swa_q_g", "swa_k_g", "swa_sinks", "norm2_g", "ffn_conv_b"]
_ORDER = ["meta_tokens", "norm1_g", "w_in", "fox_forget_b", "fox_q_g", "fox_k_g", "mla_q_a_g", "mla_w_q_up",
          "mla_kv_a_g", "mla_w_kv_up", "mla_q_g", "mla_k_g", "swa_q_g", "swa_k_g", "swa_sinks", "w_branch", "w_o",
          "norm2_g", "ffn_w_up", "ffn_conv_w", "ffn_conv_b", "ffn_w_down"]


def _flat_rows(vecs, dtype, row_mult):
    flat = jnp.concatenate([a.reshape(-1).astype(dtype) for a in vecs])
    rows = -(-flat.shape[0] // (1024 * row_mult)) * row_mult
    return jnp.pad(flat, (0, rows * 1024 - flat.shape[0])).reshape(rows, 1024)


def _unflatten(flat, shapes):
    out, off = [], 0
    for s in shapes:
        n = math.prod(s)
        out.append(flat[off:off + n].reshape(s))
        off += n
    return out


def _to_full(blocks, axis):
    moved = jnp.moveaxis(blocks, 0, axis)
    s = moved.shape
    return moved.reshape(s[:axis] + (s[axis] * s[axis + 1],) + s[axis + 2:])


def _to_blocks(full, axis):
    s = full.shape
    split = full.reshape(s[:axis] + (N_DEV, s[axis] // N_DEV) + s[axis + 1:])
    return jnp.moveaxis(split, axis, 0)


def kernel(x, meta_tokens, norm1_g, w_in, fox_forget_b, fox_q_g, fox_k_g, mla_q_a_g, mla_w_q_up, mla_kv_a_g, mla_w_kv_up, mla_q_g, mla_k_g, swa_q_g, swa_k_g, swa_sinks, w_branch, w_o, norm2_g, ffn_w_up, ffn_conv_w, ffn_conv_b, ffn_w_down, loss_target, m_meta_tokens, m_norm1_g, m_w_in, m_fox_forget_b, m_fox_q_g, m_fox_k_g, m_mla_q_a_g, m_mla_w_q_up, m_mla_kv_a_g, m_mla_w_kv_up, m_mla_q_g, m_mla_k_g, m_swa_q_g, m_swa_k_g, m_swa_sinks, m_w_branch, m_w_o, m_norm2_g, m_ffn_w_up, m_ffn_conv_w, m_ffn_conv_b, m_ffn_w_down, v_meta_tokens, v_norm1_g, v_w_in, v_fox_forget_b, v_fox_q_g, v_fox_k_g, v_mla_q_a_g, v_mla_w_q_up, v_mla_kv_a_g, v_mla_w_kv_up, v_mla_q_g, v_mla_k_g, v_swa_q_g, v_swa_k_g, v_swa_sinks, v_w_branch, v_w_o, v_norm2_g, v_ffn_w_up, v_ffn_conv_w, v_ffn_conv_b, v_ffn_w_down):
    wl = dict(zip(_ORDER, (meta_tokens, norm1_g, w_in, fox_forget_b, fox_q_g, fox_k_g, mla_q_a_g, mla_w_q_up,
                           mla_kv_a_g, mla_w_kv_up, mla_q_g, mla_k_g, swa_q_g, swa_k_g, swa_sinks, w_branch, w_o,
                           norm2_g, ffn_w_up, ffn_conv_w, ffn_conv_b, ffn_w_down)))
    ml = dict(zip(_ORDER, (m_meta_tokens, m_norm1_g, m_w_in, m_fox_forget_b, m_fox_q_g, m_fox_k_g, m_mla_q_a_g,
                           m_mla_w_q_up, m_mla_kv_a_g, m_mla_w_kv_up, m_mla_q_g, m_mla_k_g, m_swa_q_g, m_swa_k_g,
                           m_swa_sinks, m_w_branch, m_w_o, m_norm2_g, m_ffn_w_up, m_ffn_conv_w, m_ffn_conv_b,
                           m_ffn_w_down)))
    vl = dict(zip(_ORDER, (v_meta_tokens, v_norm1_g, v_w_in, v_fox_forget_b, v_fox_q_g, v_fox_k_g, v_mla_q_a_g,
                           v_mla_w_q_up, v_mla_kv_a_g, v_mla_w_kv_up, v_mla_q_g, v_mla_k_g, v_swa_q_g, v_swa_k_g,
                           v_swa_sinks, v_w_branch, v_w_o, v_norm2_g, v_ffn_w_up, v_ffn_conv_w, v_ffn_conv_b,
                           v_ffn_w_down)))
    small_sh = [n for n, _ in _SMALL_SHARDED]
    big = [n for n, _ in _BIG]
    axis_of = dict(_BIG)
    idx = _index(_place())

    def to_full(n, blocks, own=None):
        if own is not None:
            sel = (jnp.arange(N_DEV) == idx).reshape((N_DEV,) + (1,) * own.ndim)
            blocks = jnp.where(sel, own[None], blocks)
        return _to_full(blocks, axis_of[n] - 1)

    local = {(n, l): wl[n][l].astype(BF16) for n in big for l in range(DEPTH)}
    got = _gather([local[(n, 0)] for n in _MIX_BIG] + [wl[n] for n in small_sh], "gather_weights_l0_mix")
    full = {n: wl[n] for n in _REPLICATED}
    for (n, axis), blocks in zip(_SMALL_SHARDED, got[len(_MIX_BIG):]):
        full[n] = _to_full(blocks, axis)
    ready = {(n, 0): to_full(n, blocks) for n, blocks in zip(_MIX_BIG, got)}
    later = {"l0_ffn": [(n, 0) for n in _FFN_BIG], "l1": [(n, 1) for n in big]}
    states = {}
    for key, names in later.items():
        states[key], tick = _exchange_start([local[e] for e in names], "gather_weights_" + key + "_start", True, got[0])
        full["norm1_g"] = full["norm1_g"] + tick

    def fetch(l, stage, after):
        key = "l0_ffn" if l == 0 else "l1"
        if (l, stage) != (0, "mix") and key in states:
            lands = _exchange_wait(states.pop(key), after, "gather_weights_" + key + "_wait", True)
            ready.update({e: to_full(e[0], blocks, local[e]) for e, blocks in zip(later[key], lands)})
        return {n: ready[(n, l)] for n in (_MIX_BIG if stage == "mix" else _FFN_BIG)}

    blocks_of = lambda g, names: [_to_blocks(g[n], axis_of[n] - 1).astype(BF16) for n in names]
    early = {}

    def hook(l, stage, g):
        if l == DEPTH - 1 and stage == "mix":
            key, names = "l1", big
        elif l == 0:
            merge = ("w_branch", "w_o")
            groups = {"ffn": _FFN_BIG, "merge": merge, "mix": tuple(n for n in _MIX_BIG if n not in merge)}
            key, names = "l0_" + stage, groups[stage]
        else:
            return None
        sends = blocks_of(g, names)
        state, tick = _exchange_start(sends, "exchange_grads_" + key + "_start")
        early[key] = (names, l, sends, state)
        return tick

    loss, grad_x, grad_meta, grads = _local_step(x[0], loss_target[0], full, hook, fetch)
    result = {kind: {} for kind in ("grad", "delta", "new_m", "new_v")}
    small_grads = {k: jnp.stack([grads[l][k] for l in range(DEPTH)]) for k in grads[0] if k not in big}
    small_grads["meta_tokens"] = grad_meta
    small_full = _REPLICATED + small_sh
    mine_small = _flat_rows([small_grads[n] for n in small_full] + [loss], F32, 8)
    small_state, _ = _exchange_start([mine_small], "gather_small_grads_start", True)
    landed, sent = {}, {}
    after = early["l0_mix"][2][0]
    for key in ("l1", "l0_ffn", "l0_merge"):
        names, l, sends, state = early[key]
        got = _exchange_wait(state, after, "exchange_grads_" + key + "_wait")
        landed.update({(n, l): p for n, p in zip(names, got)})
        sent.update({(n, l): p for n, p in zip(names, sends)})

    def update(names):
        for n in names:
            outs = _adamw(wl[n], ml[n], vl[n], [landed[(n, l)] for l in range(DEPTH)], "adamw_" + n,
                          [sent[(n, l)] for l in range(DEPTH)])
            for kind, val in zip(result, outs):
                result[kind][n] = val

    update(_FFN_BIG)
    got_small = _exchange_wait(small_state, result["grad"]["ffn_w_down"], "gather_small_grads_wait", True)[0]
    sel = (jnp.arange(N_DEV) == idx).reshape(N_DEV, 1, 1)
    total_small = _sum_slots(jnp.where(sel, mine_small[None], got_small), "sum_small_grads").reshape(-1)
    pieces = _unflatten(total_small, [small_grads[n].shape for n in small_full] + [()])
    loss_total = pieces[-1]
    g_small = dict(zip(small_full, pieces[:-1]))
    for n, axis in _SMALL_SHARDED:
        size = wl[n].shape[axis]
        g_small[n] = lax.dynamic_slice_in_dim(g_small[n], idx * size, size, axis)
    flat = lambda d: _flat_rows([d[n] for n in small_full], F32, 8)
    small_out = _adamw(flat(wl), flat(ml), flat(vl), flat(g_small)[None], "adamw_small")
    for kind, fs in zip(result, small_out):
        result[kind].update(zip(small_full, _unflatten(fs.reshape(-1), [wl[n].shape for n in small_full])))
    names, l, sends, state = early["l0_mix"]
    got = _exchange_wait(state, small_out[0], "exchange_grads_l0_mix_wait")
    landed.update({(n, l): p for n, p in zip(names, got)})
    sent.update({(n, l): p for n, p in zip(names, sends)})
    update(_MIX_BIG)
    outs = [loss_total, grad_x[None]]
    for kind in ("grad", "delta", "new_m", "new_v"):
        outs += [result[kind][n] for n in _ORDER]
    return tuple(outs)
```

```python
import functools
import math

import numpy as np
import jax
import jax.numpy as jnp
from jax import lax
from jax.experimental import pallas as pl
from jax.experimental.pallas import tpu as pltpu

F32, BF16 = jnp.float32, jnp.bfloat16
SDS = jax.ShapeDtypeStruct

D_MODEL = 1024
N_META = 16
EPS = 1e-6
WINDOW = 128
ROPE_THETA = 10000.0
HEADS = 8
D_FF = 2816
DEPTH = 2
N_DEV = 8
ADAM_LR, ADAM_B1, ADAM_B2, ADAM_EPS, ADAM_WD, ADAM_STEP = 0.001, 0.9, 0.999, 1e-08, 0.01, 10

ROW_ALIGN = 384
TILE_MM = 384
TILE_ROW = 192
TILE_CONV_BWD = 128
TILE_ATT = 384
TILE_POST = 384
PAIRS = 2
VMEM_LIMIT = 56 * 1024 * 1024

GATES_W = 3072
OTHER_W = 2816
IN_W = GATES_W + OTHER_W
O_FQ, O_FK, O_FV, O_SQ, O_SK, O_SV, O_CQ, O_CKV, O_MISC = 0, 512, 1024, 1536, 2048, 2176, 2304, 2560, 2688
FF_LANE = 32

NEG = -1e30


def _dot(a, b):
    return jnp.dot(a, b, preferred_element_type=F32)


def _dot_nt(a, b):
    return lax.dot_general(a, b, (((1,), (1,)), ((), ())), preferred_element_type=F32)


def _dot_tn(a, b):
    return lax.dot_general(a, b, (((0,), (0,)), ((), ())), preferred_element_type=F32)


def _params(sem):
    return pltpu.CompilerParams(dimension_semantics=sem, vmem_limit_bytes=VMEM_LIMIT)


def _rms(x, g):
    return x * lax.rsqrt(jnp.mean(x * x, axis=-1, keepdims=True) + EPS) * g


def _split_dot(x, m, pieces=2):
    acc, rest = None, x
    for _ in range(pieces):
        part = rest.astype(BF16)
        rest = rest - part.astype(F32)
        acc = _dot(part, m) if acc is None else acc + _dot(part, m)
    return acc


@jax.custom_vjp
def _sel(x, m, mt):
    return _split_dot(x, m)


_sel.defvjp(lambda x, m, mt: (_split_dot(x, m), (m, mt)), lambda res, dy: (_split_dot(dy, res[1]), None, None))


@jax.custom_vjp
def _mm(x, w):
    return _dot(x.astype(BF16), w.astype(BF16))


def _mm_bwd(res, dy):
    x, w = res
    dyb = dy.astype(BF16)
    return _dot_nt(dyb, w.astype(BF16)), _dot_tn(x.astype(BF16), dyb)


_mm.defvjp(lambda x, w: (_mm(x, w), (x, w)), _mm_bwd)


def _rot_impl(x):
    w = x.shape[1]
    lane = lax.broadcasted_iota(jnp.int32, x.shape, 1) % 128
    lo = (lane >= 64) & (lane < 80)
    hi = (lane >= 80) & (lane < 96)
    return jnp.where(hi, pltpu.roll(x, 16, 1), 0.0) - jnp.where(lo, pltpu.roll(x, w - 16, 1), 0.0)


@jax.custom_vjp
def _rot(x):
    return _rot_impl(x)


_rot.defvjp(lambda x: (_rot_impl(x), None), lambda _, dy: (-_rot_impl(dy),))


def _gnorm(x, g, e, et, dim):
    inv = lax.rsqrt(_sel(x * x, e, et) * (1.0 / dim) + EPS)
    return x * _sel(inv, et, e) * g


def _indicator(width, period):
    m = np.zeros((width, 128), np.float32)
    m[np.arange(width), np.arange(width) // period] = 1.0
    return m


def _consts():
    e64 = _indicator(512, 64)
    e128 = _indicator(1024, 128)
    sk = np.zeros((128, 1024), np.float32)
    for h in range(HEADS):
        sk[np.arange(32), 128 * h + 64 + np.arange(32)] = 1.0
    dup = np.zeros((128, 256), np.float32)
    for g in range(2):
        for r in range(2):
            dup[64 * g + np.arange(64), 128 * g + 64 * r + np.arange(64)] = 1.0
    mats = [e64, e64.T, e128, e128.T, sk, sk.T, dup, dup.T]
    return [jnp.asarray(m, BF16) for m in mats]


def _fold_matrix(width, period):
    m = np.zeros((width, 128), np.float32)
    m[np.arange(width), np.arange(width) % period] = 1.0
    return jnp.asarray(m, BF16)


def _rope_tables(lp):
    half = 16
    freqs = ROPE_THETA ** (-np.arange(half, dtype=np.float32) / half)
    ang = np.arange(lp, dtype=np.float32)[:, None] * freqs[None, :]
    cos = np.ones((lp, 128), np.float32)
    sin = np.zeros((lp, 128), np.float32)
    cos[:, 64:80] = np.cos(ang)
    cos[:, 80:96] = np.cos(ang)
    sin[:, 64:80] = np.sin(ang)
    sin[:, 80:96] = np.sin(ang)
    return jnp.asarray(cos), jnp.asarray(sin)


def _norm_matmul(h, g, w, tn, name):
    lp, d = h.shape
    n = w.shape[1]
    tb = TILE_MM

    def body(h_ref, g_ref, w_ref, xn_ref, y_ref):
        @pl.when(pl.program_id(1) == 0)
        def _():
            xn_ref[...] = _rms(h_ref[...], g_ref[...]).astype(BF16)

        y_ref[...] = _dot(xn_ref[...], w_ref[...])

    return pl.pallas_call(
        body, name=name, grid=(lp // tb, n // tn),
        in_specs=[pl.BlockSpec((tb, d), lambda i, j: (i, 0)), pl.BlockSpec((1, d), lambda i, j: (0, 0)),
                  pl.BlockSpec((d, tn), lambda i, j: (0, j))],
        out_specs=[pl.BlockSpec((tb, d), lambda i, j: (i, 0)), pl.BlockSpec((tb, tn), lambda i, j: (i, j))],
        out_shape=[SDS((lp, d), BF16), SDS((lp, n), F32)],
        compiler_params=_params(("parallel", "arbitrary")),
    )(h, g, w)


def _matmul_residual(a, w, res, name):
    m, k = a.shape
    n = w.shape[1]
    tb = TILE_MM

    def body(a_ref, w_ref, r_ref, o_ref):
        o_ref[...] = r_ref[...] + _dot(a_ref[...], w_ref[...])

    return pl.pallas_call(
        body, name=name, grid=(m // tb,),
        in_specs=[pl.BlockSpec((tb, k), lambda i: (i, 0)), pl.BlockSpec((k, n), lambda i: (0, 0)),
                  pl.BlockSpec((tb, n), lambda i: (i, 0))],
        out_specs=pl.BlockSpec((tb, n), lambda i: (i, 0)),
        out_shape=SDS((m, n), F32),
        compiler_params=_params(("parallel",)),
    )(a, w, res)


def _matmul_nt(dy, w, tn, name):
    m, k = dy.shape
    n = w.shape[0]
    tb = TILE_MM

    def body(dy_ref, w_ref, o_ref):
        o_ref[...] = _dot_nt(dy_ref[...].astype(BF16), w_ref[...])

    return pl.pallas_call(
        body, name=name, grid=(m // tb, n // tn),
        in_specs=[pl.BlockSpec((tb, k), lambda i, j: (i, 0)), pl.BlockSpec((tn, k), lambda i, j: (j, 0))],
        out_specs=pl.BlockSpec((tb, tn), lambda i, j: (i, j)),
        out_shape=SDS((m, n), F32),
        compiler_params=_params(("parallel", "arbitrary")),
    )(dy, w)


def _matmul_tn(x, dy, tn, name):
    m, k = x.shape
    n = dy.shape[1]
    tb = TILE_MM
    nb = m // tb

    def body(x_ref, dy_ref, o_ref, acc):
        i = pl.program_id(1)

        @pl.when(i == 0)
        def _():
            acc[...] = jnp.zeros_like(acc)

        acc[...] += _dot_tn(x_ref[...].astype(BF16), dy_ref[...].astype(BF16))

        @pl.when(i == nb - 1)
        def _():
            o_ref[...] = acc[...].astype(BF16)

    return pl.pallas_call(
        body, name=name, grid=(n // tn, nb),
        in_specs=[pl.BlockSpec((tb, k), lambda j, i: (i, 0)), pl.BlockSpec((tb, tn), lambda j, i: (i, j))],
        out_specs=pl.BlockSpec((k, tn), lambda j, i: (0, j)),
        out_shape=SDS((k, n), BF16),
        scratch_shapes=[pltpu.VMEM((k, tn), F32)],
        compiler_params=_params(("parallel", "arbitrary")),
    )(x, dy)


def _norm_matmul_bwd(dys, w, x, g, dres, name):
    m, d = x.shape
    tb = TILE_MM
    widths = [a.shape[1] for a in dys]
    n_dy = len(dys)

    def body(*refs):
        w_ref, x_ref, g_ref, r_ref, o_ref, dg_ref = refs[n_dy:]

        @pl.when(pl.program_id(0) == 0)
        def _():
            dg_ref[...] = jnp.zeros_like(dg_ref)

        dxn, off = None, 0
        for dy_ref, width in zip(refs[:n_dy], widths):
            part = _dot_nt(dy_ref[...], w_ref[:, off:off + width])
            dxn = part if dxn is None else dxn + part
            off += width
        _, vjp = jax.vjp(_rms, x_ref[...], g_ref[...])
        dx, dg = vjp(dxn)
        o_ref[...] = r_ref[...] + dx
        dg_ref[...] += dg

    row = pl.BlockSpec((tb, d), lambda i: (i, 0))
    vec = pl.BlockSpec((1, d), lambda i: (0, 0))
    return pl.pallas_call(
        body, name=name, grid=(m // tb,),
        in_specs=[pl.BlockSpec((tb, wd), lambda i: (i, 0)) for wd in widths]
        + [pl.BlockSpec(w.shape, lambda i: (0, 0)), row, vec, row],
        out_specs=[row, vec],
        out_shape=[SDS((m, d), F32), SDS((1, d), F32)],
        compiler_params=_params(("arbitrary",)),
    )(*dys, w, x, g, dres)


def _prep_math(pieces, prm, consts, cos, sin):
    fq, fk, sq, sk, sv, cq, ckv, misc = pieces
    gfq, gfk, gsq, gsk, fb, gqa, gkva, gmq, gmk, wq, wkk, wkv = prm
    e64, e64t, e128, e128t, skm, skt, dup, dupt = consts
    cos8 = jnp.concatenate([cos] * HEADS, axis=1)
    sin8 = jnp.concatenate([sin] * HEADS, axis=1)
    fq_n = _gnorm(fq, gfq, e64, e64t, 64)
    fk_n = _gnorm(fk, gfk, e64, e64t, 64)
    ls = jax.nn.log_sigmoid(misc + fb)
    q = _gnorm(_mm(_rms(cq, gqa), wq), gmq, e128, e128t, 96)
    mq = q * cos8 + _rot(q) * sin8
    kva = _rms(ckv, gkva)
    k = _gnorm(_mm(kva, wkk) + _sel(misc, skm, skt), gmk, e128, e128t, 96)
    mk = k * cos8 + _rot(k) * sin8
    mv = _mm(kva, wkv)
    sq_n = _gnorm(sq, gsq, e64, e64t, 64)
    sk_n = _gnorm(sk, gsk, e64[0:128], e64t[:, 0:128], 64)
    skd = _sel(sk_n, dup, dupt)
    svd = _sel(sv, dup, dupt)
    return fq_n, fk_n, ls, mq, mk, mv, sq_n, skd, svd


_PIECES = [(O_FQ, 512), (O_FK, 512), (O_SQ, 512), (O_SK, 128), (O_SV, 128), (O_CQ, 256), (O_CKV, 128), (O_MISC, 128)]
_PRM_SHAPES = [(1, 512), (1, 512), (1, 512), (1, 128), (1, 128), (1, 256), (1, 128), (1, 1024), (1, 1024),
               (256, 1024), (128, 1024), (128, 512)]
_CONST_SHAPES = [(512, 128), (128, 512), (1024, 128), (128, 1024), (128, 1024), (1024, 128), (128, 256), (256, 128)]


def _piece_specs(tb):
    def spec(off, width):
        blk = (GATES_W + off) // width
        return pl.BlockSpec((tb, width), lambda i, blk=blk: (i, blk))
    return [spec(o, w) for o, w in _PIECES] + [spec(O_FV, 512)]


def _full_specs(shapes):
    return [pl.BlockSpec(s, lambda i: (0, 0)) for s in shapes]


def _prep_fwd(proj, prm, consts, cos, sin, name):
    lp = proj.shape[0]
    tb = TILE_ROW
    row = lambda w: pl.BlockSpec((tb, w), lambda i: (i, 0))

    def body(*refs):
        pieces = [r[...] for r in refs[0:8]]
        fv = refs[8][...]
        prm_v = [r[...] for r in refs[9:21]]
        consts_v = [r[...] for r in refs[21:29]]
        cos_v, sin_v = refs[29][...], refs[30][...]
        outs = refs[31:]
        fq_n, fk_n, ls, mq, mk, mv, sq_n, skd, svd = _prep_math(pieces, prm_v, consts_v, cos_v, sin_v)
        for ref, val in zip(outs, (fq_n, fk_n, fv, mq, mk, mv, sq_n, skd, svd)):
            ref[...] = val.astype(BF16)
        outs[9][...] = ls

    widths = [512, 512, 512, 1024, 1024, 512, 512, 256, 256]
    return pl.pallas_call(
        body, name=name, grid=(lp // tb,),
        in_specs=_piece_specs(tb) + _full_specs(_PRM_SHAPES) + _full_specs(_CONST_SHAPES) + [row(128), row(128)],
        out_specs=[row(w) for w in widths] + [row(128)],
        out_shape=[SDS((lp, w), BF16) for w in widths] + [SDS((lp, 128), F32)],
        compiler_params=_params(("parallel",)),
    )(*([proj] * 9), *prm, *consts, cos, sin)


def _prep_bwd(proj, prm, consts, cos, sin, cots, folds, name):
    lp = proj.shape[0]
    tb = TILE_ROW
    row = lambda w: pl.BlockSpec((tb, w), lambda i: (i, 0))
    fold64, fold128 = folds

    def body(*refs):
        pieces = [r[...] for r in refs[0:8]]
        prm_v = [r[...] for r in refs[9:21]]
        consts_v = [r[...] for r in refs[21:29]]
        cos_v, sin_v = refs[29][...], refs[30][...]
        dfq, dfk, dfv, dmq, dmk, dmv, dsq, dskp, dsvp, dls = [r[...] for r in refs[31:41]]
        f64, f128 = refs[41][...], refs[42][...]
        d_ref = refs[43]
        g_refs = refs[44:]

        @pl.when(pl.program_id(0) == 0)
        def _():
            for r in g_refs:
                r[...] = jnp.zeros_like(r)

        def pair_sum(p):
            return jnp.concatenate([p[:, 0:128] + p[:, 128:256], p[:, 256:384] + p[:, 384:512]], axis=1)

        f = lambda pc, pr: _prep_math(pc, pr, consts_v, cos_v, sin_v)
        _, vjp = jax.vjp(f, pieces, prm_v)
        dpc, dprm = vjp((dfq, dfk, dls, dmq, dmk, dmv, dsq, pair_sum(dskp), pair_sum(dsvp)))
        d_fq, d_fk, d_sq, d_sk, d_sv, d_cq, d_ckv, d_misc = dpc
        for off, val in ((O_FQ, d_fq), (O_FK, d_fk), (O_FV, dfv), (O_SQ, d_sq), (O_SK, d_sk), (O_SV, d_sv),
                         (O_CQ, d_cq), (O_CKV, d_ckv), (O_MISC, d_misc)):
            d_ref[:, off:off + val.shape[1]] = val.astype(BF16)
        folded = {0: f64, 1: f64, 2: f64, 3: f64[0:128], 7: f128, 8: f128}
        for idx, (ref, val) in enumerate(zip(g_refs, dprm)):
            if idx in folded:
                ref[...] += _split_dot(jnp.broadcast_to(val, (8, val.shape[1])), folded[idx], 3)
            elif val.shape[0] == 1:
                ref[...] += jnp.broadcast_to(val, ref.shape)
            else:
                ref[...] += val

    g_shapes = [(8, 128), (8, 128), (8, 128), (8, 128), (8, 128), (8, 256), (8, 128), (8, 128), (8, 128),
                (256, 1024), (128, 1024), (128, 512)]
    cot_widths = [512, 512, 512, 1024, 1024, 512, 512, 512, 512, 128]
    return pl.pallas_call(
        body, name=name, grid=(lp // tb,),
        in_specs=(_piece_specs(tb) + _full_specs(_PRM_SHAPES) + _full_specs(_CONST_SHAPES) + [row(128), row(128)]
                  + [row(w) for w in cot_widths] + _full_specs([(512, 128), (1024, 128)])),
        out_specs=[row(OTHER_W)] + _full_specs(g_shapes),
        out_shape=[SDS((lp, OTHER_W), BF16)] + [SDS(s, F32) for s in g_shapes],
        compiler_params=_params(("arbitrary",)),
    )(*([proj] * 9), *prm, *consts, cos, sin, *cots, fold64, fold128)


def _cumsum(xs, reverse, name):
    lp = xs[0].shape[0]
    tb = TILE_MM
    nb = lp // tb
    n_in = len(xs)
    idx = (lambda i: (nb - 1 - i, 0)) if reverse else (lambda i: (i, 0))

    def body(*refs):
        o_ref, carry = refs[n_in], refs[n_in + 1]

        @pl.when(pl.program_id(0) == 0)
        def _():
            carry[...] = jnp.zeros_like(carry)

        x = refs[0][...]
        for r in refs[1:n_in]:
            x = x + r[...]
        r_i = lax.broadcasted_iota(jnp.int32, (tb, tb), 0)
        c_i = lax.broadcasted_iota(jnp.int32, (tb, tb), 1)
        tri = ((c_i >= r_i) if reverse else (c_i <= r_i)).astype(BF16)
        acc, rest = None, x
        for _ in range(3):
            part = rest.astype(BF16)
            rest = rest - part.astype(F32)
            acc = _dot(tri, part) if acc is None else acc + _dot(tri, part)
        o_ref[...] = acc + carry[...]
        carry[...] += jnp.sum(x, axis=0, keepdims=True)

    return pl.pallas_call(
        body, name=name, grid=(nb,),
        in_specs=[pl.BlockSpec((tb, 128), idx)] * n_in,
        out_specs=pl.BlockSpec((tb, 128), idx),
        out_shape=SDS((lp, 128), F32),
        scratch_shapes=[pltpu.VMEM((1, 128), F32)],
        compiler_params=_params(("arbitrary",)),
    )(*xs)


class _Att:
    def __init__(self, mode):
        self.mode = mode
        self.wide = mode == "mla"
        self.qw = 256 if self.wide else 128
        self.scale = (96 if mode == "mla" else 64) ** -0.5

    def resident(self, x, lo, scaled):
        if self.wide:
            return x[:, 0:128], x[:, 128:256]
        if scaled:
            x = x * jnp.asarray(self.scale, x.dtype)
        zero = jnp.zeros_like(x)
        return jnp.where(lo, x, zero), jnp.where(lo, zero, x)

    def moving(self, x):
        return (x[:, 0:128], x[:, 128:256]) if self.wide else (x, x)

    def logits(self, a, b, qpos, kpos, key_decay, slope, masked):
        s = _dot_nt(a, b)
        if self.wide:
            s = s * self.scale
        if self.mode == "fox":
            s = s - key_decay
        if self.mode == "swa":
            s = s - slope * (qpos - kpos).astype(F32)
        if masked:
            ok = kpos <= qpos
            if self.mode == "swa":
                ok = ok & ((kpos < N_META) | (qpos - kpos < WINDOW))
            s = jnp.where(ok, s, NEG)
        return s


def _as_rows(col):
    return jnp.broadcast_to(col, (col.shape[0], 128)).T[0:8, :]


def _halves(x, lo):
    zero = jnp.zeros_like(x)
    return jnp.where(lo, x, zero), jnp.where(lo, zero, x)


def _kv_specs(att, lp, rows):
    if att.mode == "swa":
        return (pl.BlockSpec((rows, 128), lambda g, i: (i if rows != lp else 0, g)),) * 2
    return (pl.BlockSpec((rows, PAIRS * att.qw), lambda g, i: (i if rows != lp else 0, g)),
            pl.BlockSpec((rows, PAIRS * 128), lambda g, i: (i if rows != lp else 0, g)))


def _pair_cols(att, x, pp, width):
    return x if x.shape[1] == width else x[:, pp * width:(pp + 1) * width]


def _att_fwd(att, q, k, v, extra, name):
    lp = q.shape[0]
    t = TILE_ATT
    nq = lp // t
    qw = att.qw
    mode = att.mode
    nh = 2 * PAIRS

    def body(*refs):
        q_ref, k_ref, v_ref = refs[0:3]
        o_ref, lse_ref, row_ref = refs[-3:]
        g, qi = pl.program_id(0), pl.program_id(1)
        lo = lax.broadcasted_iota(jnp.int32, (1, 128), 1) < 64
        q_all = q_ref[...]
        q_heads = [h for pp in range(PAIRS) for h in att.resident(_pair_cols(att, q_all, pp, qw), lo, True)]
        qpos = qi * t + lax.broadcasted_iota(jnp.int32, (t, 1), 0)

        def step(first, cols, carry, masked):
            ks = pl.multiple_of(first, 128)
            kc, vc = k_ref[pl.ds(ks, cols), :], v_ref[pl.ds(ks, cols), :]
            kpos = first + lax.broadcasted_iota(jnp.int32, (1, cols), 1)
            out = []
            for h in range(nh):
                pp = h // 2
                m, l, acc = carry[3 * h:3 * h + 3]
                k_h = att.moving(_pair_cols(att, kc, pp, qw))[h % 2]
                decay = refs[3][h, :, pl.ds(ks, cols)] if mode == "fox" else None
                slope = refs[4][nh * g + h] if mode == "swa" else None
                s = att.logits(q_heads[h], k_h, qpos, kpos, decay, slope, masked)
                m_new = jnp.maximum(m, jnp.max(s, axis=-1, keepdims=True))
                alpha = jnp.exp(m - m_new)
                pe = jnp.exp(s - m_new)
                l = alpha * l + jnp.sum(pe, axis=-1, keepdims=True)
                acc = alpha * acc + _dot(pe.astype(BF16), _pair_cols(att, vc, pp, 128))
                out += [m_new, l, acc]
            return tuple(out)

        init = []
        for h in range(nh):
            if mode == "swa":
                init += [jnp.full((t, 1), refs[3][nh * g + h], F32), jnp.ones((t, 1), F32)]
            else:
                init += [jnp.full((t, 1), NEG, F32), jnp.zeros((t, 1), F32)]
            init.append(jnp.zeros((t, 128), F32))
        if mode == "swa":
            band = jnp.maximum(qi * t - WINDOW, 0)
            carry = lax.fori_loop(0, (band >= 128).astype(jnp.int32), lambda j, c: step(0, 128, c, True), tuple(init))
            carry = step(band, t + WINDOW, carry, True)
        else:
            carry = lax.fori_loop(0, qi // 2, lambda j, c: step(2 * j * t, 2 * t, c, False), tuple(init))
            carry = lax.fori_loop(0, qi % 2, lambda j, c: step((qi - 1) * t, t, c, False), carry)
            carry = step(qi * t, t, carry, True)
        outs = []
        for pp in range(PAIRS):
            (ma, la, acca), (mb, lb, accb) = carry[6 * pp:6 * pp + 3], carry[6 * pp + 3:6 * pp + 6]
            outs.append(jnp.where(lo, acca / la, accb / lb).astype(BF16))
            for h, lse in ((2 * pp, ma + jnp.log(la)), (2 * pp + 1, mb + jnp.log(lb))):
                lse_ref[h] = lse
                row_ref[h] = _as_rows(lse)
        o_ref[...] = jnp.concatenate(outs, axis=1)

    in_specs = [pl.BlockSpec((t, PAIRS * qw), lambda g, i: (i, g)), *_kv_specs(att, lp, lp)]
    if mode == "fox":
        in_specs += [pl.BlockSpec((nh, 1, lp), lambda g, i: (g, 0, 0))]
    if mode == "swa":
        in_specs += [pl.BlockSpec(memory_space=pltpu.SMEM)] * 2
    return pl.pallas_call(
        body, name=name, grid=(4 // PAIRS, nq), in_specs=in_specs,
        out_specs=[pl.BlockSpec((t, PAIRS * 128), lambda g, i: (i, g)), pl.BlockSpec((nh, t, 1), lambda g, i: (g, i, 0)),
                   pl.BlockSpec((nh, 8, t), lambda g, i: (g, 0, i))],
        out_shape=[SDS((lp, 512), BF16), SDS((HEADS, lp, 1), F32), SDS((HEADS, 8, lp), F32)],
        compiler_params=_params(("parallel", "arbitrary")),
    )(q, k, v, *extra)


def _att_dq(att, q, k, v, o, do, lse, extra, name):
    lp = q.shape[0]
    t = TILE_ATT
    nq = lp // t
    qw = att.qw
    mode = att.mode
    nh = 2 * PAIRS

    def body(*refs):
        q_ref, k_ref, v_ref, o_ref, do_ref, lse_ref = refs[0:6]
        n_out = 3 if mode == "mla" else 4
        outs = refs[len(refs) - n_out:]
        dq_ref, delta_ref, row_ref = outs[0:3]
        outs = outs[1:]
        g, qi = pl.program_id(0), pl.program_id(1)
        lo = lax.broadcasted_iota(jnp.int32, (1, 128), 1) < 64
        q_all, do_all = q_ref[...], do_ref[...]
        prod = do_all.astype(F32) * o_ref[...].astype(F32)
        q_heads, do_heads, delta = [], [], []
        for pp in range(PAIRS):
            q_heads += att.resident(_pair_cols(att, q_all, pp, qw), lo, True)
            do_heads += _halves(_pair_cols(att, do_all, pp, 128), lo)
            pr_pp = _pair_cols(att, prod, pp, 128)
            delta += [jnp.sum(jnp.where(lo, pr_pp, 0.0), axis=-1, keepdims=True),
                      jnp.sum(jnp.where(lo, 0.0, pr_pp), axis=-1, keepdims=True)]
        lse_v = [lse_ref[h] for h in range(nh)]
        qpos = qi * t + lax.broadcasted_iota(jnp.int32, (t, 1), 0)

        def step(first, cols, carry, masked):
            ks = pl.multiple_of(first, 128)
            kc, vc = k_ref[pl.ds(ks, cols), :], v_ref[pl.ds(ks, cols), :]
            kpos = first + lax.broadcasted_iota(jnp.int32, (1, cols), 1)
            out = []
            for h in range(nh):
                pp = h // 2
                k_h = att.moving(_pair_cols(att, kc, pp, qw))[h % 2]
                decay = refs[6][h, :, pl.ds(ks, cols)] if mode == "fox" else None
                slope = refs[7][nh * g + h] if mode == "swa" else None
                s = att.logits(q_heads[h], k_h, qpos, kpos, decay, slope, masked)
                pr = jnp.exp(s - lse_v[h])
                ds = pr * (_dot_nt(do_heads[h], _pair_cols(att, vc, pp, 128)) - delta[h])
                out.append(carry[2 * h] + _dot(ds.astype(BF16), k_h))
                out.append(carry[2 * h + 1] + jnp.sum(ds, axis=-1, keepdims=True) if mode == "fox" else carry[2 * h + 1])
            return tuple(out)

        init = (jnp.zeros((t, 128), F32), jnp.zeros((t, 1), F32)) * nh
        if mode == "swa":
            band = jnp.maximum(qi * t - WINDOW, 0)
            carry = lax.fori_loop(0, (band >= 128).astype(jnp.int32), lambda j, c: step(0, 128, c, True), init)
            carry = step(band, t + WINDOW, carry, True)
        else:
            carry = lax.fori_loop(0, qi // 2, lambda j, c: step(2 * j * t, 2 * t, c, False), init)
            carry = lax.fori_loop(0, qi % 2, lambda j, c: step((qi - 1) * t, t, c, False), carry)
            carry = step(qi * t, t, carry, True)
        dq = []
        for pp in range(PAIRS):
            dqa, dca, dqb, dcb = carry[4 * pp:4 * pp + 4]
            dq += [dqa, dqb] if att.wide else [jnp.where(lo, dqa, dqb)]
            if mode == "fox":
                outs[2][2 * pp] = _as_rows(dca)
                outs[2][2 * pp + 1] = _as_rows(dcb)
        dq_ref[...] = jnp.concatenate(dq, axis=1) * att.scale
        for h in range(nh):
            delta_ref[h] = delta[h]
            row_ref[h] = _as_rows(delta[h])
        if mode == "swa":
            ds_ref = outs[2]

            @pl.when(qi == 0)
            def _():
                ds_ref[...] = jnp.zeros_like(ds_ref)

            lane = lax.broadcasted_iota(jnp.int32, (8, 128), 1)
            acc = jnp.zeros((8, 128), F32)
            for h in range(nh):
                tot = -jnp.sum(jnp.exp(refs[6][nh * g + h] - lse_v[h]) * delta[h])
                acc = acc + jnp.where(lane == h, tot, 0.0)
            ds_ref[0] += acc

    col = pl.BlockSpec((nh, t, 1), lambda g, i: (g, i, 0))
    in_specs = [pl.BlockSpec((t, PAIRS * qw), lambda g, i: (i, g)), *_kv_specs(att, lp, lp),
                pl.BlockSpec((t, PAIRS * 128), lambda g, i: (i, g)), pl.BlockSpec((t, PAIRS * 128), lambda g, i: (i, g)), col]
    out_specs = [pl.BlockSpec((t, PAIRS * qw), lambda g, i: (i, g)), col, pl.BlockSpec((nh, 8, t), lambda g, i: (g, 0, i))]
    out_shape = [SDS((lp, 4 * qw), F32), SDS((HEADS, lp, 1), F32), SDS((HEADS, 8, lp), F32)]
    if mode == "fox":
        in_specs += [pl.BlockSpec((nh, 1, lp), lambda g, i: (g, 0, 0))]
        out_specs.append(pl.BlockSpec((nh, 8, t), lambda g, i: (g, 0, i)))
        out_shape.append(SDS((HEADS, 8, lp), F32))
    if mode == "swa":
        in_specs += [pl.BlockSpec(memory_space=pltpu.SMEM)] * 2
        out_specs.append(pl.BlockSpec((1, 8, 128), lambda g, i: (g, 0, 0)))
        out_shape.append(SDS((4 // PAIRS, 8, 128), F32))
    return pl.pallas_call(
        body, name=name, grid=(4 // PAIRS, nq), in_specs=in_specs, out_specs=out_specs, out_shape=out_shape,
        compiler_params=_params(("parallel", "arbitrary")),
    )(q, k, v, o, do, lse, *extra)


def _att_dkv(att, q, k, v, do, lse_row, delta_row, extra, name):
    lp = q.shape[0]
    t = TILE_ATT
    nq = lp // t
    qw = att.qw
    mode = att.mode
    nh = 2 * PAIRS

    def body(*refs):
        q_ref, k_ref, v_ref, do_ref, lse_ref, delta_ref = refs[0:6]
        n_out = 3 if mode == "fox" else 2
        outs = refs[len(refs) - n_out:]
        dk_ref, dv_ref = outs[0:2]
        g, kj = pl.program_id(0), pl.program_id(1)
        lo = lax.broadcasted_iota(jnp.int32, (1, 128), 1) < 64
        k_all, v_all = k_ref[...], v_ref[...]
        k_heads, v_heads = [], []
        for pp in range(PAIRS):
            k_heads += att.resident(_pair_cols(att, k_all, pp, qw), lo, True)
            v_heads += _halves(_pair_cols(att, v_all, pp, 128), lo)
        kpos = kj * t + lax.broadcasted_iota(jnp.int32, (t, 1), 0)

        def step(first, cols, carry, masked):
            qs = pl.multiple_of(first, 128)
            qc, doc = q_ref[pl.ds(qs, cols), :], do_ref[pl.ds(qs, cols), :]
            qpos = first + lax.broadcasted_iota(jnp.int32, (1, cols), 1)
            out = []
            for h in range(nh):
                pp = h // 2
                dk_acc, dv_acc, dc_acc = carry[3 * h:3 * h + 3]
                q_h = att.moving(_pair_cols(att, qc, pp, qw))[h % 2]
                do_h = _pair_cols(att, doc, pp, 128)
                decay = refs[6][h] if mode == "fox" else None
                slope = refs[6][nh * g + h] if mode == "swa" else None
                st = att.logits(k_heads[h], q_h, qpos, kpos, decay, slope, masked)
                pt = jnp.exp(st - lse_ref[h, 0:1, pl.ds(qs, cols)])
                dst = pt * (_dot_nt(v_heads[h], do_h) - delta_ref[h, 0:1, pl.ds(qs, cols)])
                dv_acc = dv_acc + _dot(pt.astype(BF16), do_h)
                dk_acc = dk_acc + _dot(dst.astype(BF16), q_h)
                if mode == "fox":
                    dc_acc = dc_acc - jnp.sum(dst, axis=-1, keepdims=True)
                out += [dk_acc, dv_acc, dc_acc]
            return tuple(out)

        init = (jnp.zeros((t, 128), F32), jnp.zeros((t, 128), F32), jnp.zeros((t, 1), F32)) * nh
        if mode == "swa":
            carry = lax.fori_loop(0, jnp.where(kj == 0, nq, 0), lambda qi, c: step(qi * t, t, c, True), init)
            near = jnp.minimum(kj * t, lp - (t + WINDOW))
            carry = lax.fori_loop(0, (kj > 0).astype(jnp.int32), lambda j, c: step(near, t + WINDOW, c, True), carry)
        else:
            carry = step(kj * t, t, init, True)
            rest = nq - 1 - kj
            carry = lax.fori_loop(0, rest // 2, lambda j, c: step((kj + 1 + 2 * j) * t, 2 * t, c, False), carry)
            carry = lax.fori_loop(0, rest % 2, lambda j, c: step((nq - 1) * t, t, c, False), carry)
        dk, dv = [], []
        for pp in range(PAIRS):
            dka, dva, dca, dkb, dvb, dcb = carry[6 * pp:6 * pp + 6]
            dk += [dka, dkb] if att.wide else [jnp.where(lo, dka, dkb)]
            dv.append(jnp.where(lo, dva, dvb))
            if mode == "fox":
                outs[2][2 * pp] = _as_rows(dca)
                outs[2][2 * pp + 1] = _as_rows(dcb)
        dk_ref[...] = jnp.concatenate(dk, axis=1) * att.scale
        dv_ref[...] = jnp.concatenate(dv, axis=1)

    rowv = pl.BlockSpec((nh, 8, lp), lambda g, j: (g, 0, 0))
    col = pl.BlockSpec((nh, t, 1), lambda g, j: (g, j, 0))
    in_specs = [pl.BlockSpec((lp, PAIRS * qw), lambda g, j: (0, g)), *_kv_specs(att, lp, t),
                pl.BlockSpec((lp, PAIRS * 128), lambda g, j: (0, g)), rowv, rowv]
    out_specs = [pl.BlockSpec((t, PAIRS * qw), lambda g, j: (j, g)), pl.BlockSpec((t, PAIRS * 128), lambda g, j: (j, g))]
    out_shape = [SDS((lp, 4 * qw), F32), SDS((lp, 512), F32)]
    if mode == "fox":
        in_specs += [col]
        out_specs.append(pl.BlockSpec((nh, 8, t), lambda g, j: (g, 0, j)))
        out_shape.append(SDS((HEADS, 8, lp), F32))
    if mode == "swa":
        in_specs += [pl.BlockSpec(memory_space=pltpu.SMEM)]
    return pl.pallas_call(
        body, name=name, grid=(4 // PAIRS, nq), in_specs=in_specs, out_specs=out_specs, out_shape=out_shape,
        compiler_params=_params(("parallel", "arbitrary")),
    )(q, k, v, do, lse_row, delta_row, *extra)


def _post_fwd(h, proj, outs, wb, wo, name):
    lp, d = h.shape
    tb = TILE_POST
    row = lambda w: pl.BlockSpec((tb, w), lambda i: (i, 0))

    def body(h_ref, g0, g1, g2, oa, ob, oc, wb_ref, wo_ref, o_ref):
        merged = jnp.zeros((tb, d), F32)
        for n, (g_ref, br) in enumerate(((g0, oa), (g1, ob), (g2, oc))):
            merged = merged + jax.nn.sigmoid(g_ref[...]) * _dot(br[...], wb_ref[n])
        o_ref[...] = h_ref[...] + _dot(merged.astype(BF16), wo_ref[...])

    gate = lambda n: pl.BlockSpec((tb, d), lambda i, n=n: (i, n))
    return pl.pallas_call(
        body, name=name, grid=(lp // tb,),
        in_specs=[row(d), gate(0), gate(1), gate(2), row(512), row(512), row(512),
                  pl.BlockSpec((3, 512, d), lambda i: (0, 0, 0)), pl.BlockSpec((d, d), lambda i: (0, 0))],
        out_specs=row(d), out_shape=SDS((lp, d), F32),
        compiler_params=_params(("parallel",)),
    )(h, proj, proj, proj, *outs, wb, wo)


def _post_bwd(dh, proj, outs, wb, wo, name):
    lp, d = dh.shape
    tb = TILE_POST
    row = lambda w: pl.BlockSpec((tb, w), lambda i: (i, 0))

    def body(dh_ref, g0, g1, g2, oa, ob, oc, wb_ref, wo_ref, dg_ref, doa, dob, doc, dwb_ref, dwo_ref):
        @pl.when(pl.program_id(0) == 0)
        def _():
            dwb_ref[...] = jnp.zeros_like(dwb_ref)
            dwo_ref[...] = jnp.zeros_like(dwo_ref)

        dhb = dh_ref[...].astype(BF16)
        dm = _dot_nt(dhb, wo_ref[...])
        merged = jnp.zeros((tb, d), F32)
        for n, (g_ref, br, do_ref) in enumerate(((g0, oa, doa), (g1, ob, dob), (g2, oc, doc))):
            gate = jax.nn.sigmoid(g_ref[...])
            o_n = br[...]
            y = _dot(o_n, wb_ref[n])
            merged = merged + gate * y
            dy = (dm * gate).astype(BF16)
            dg_ref[:, n * d:(n + 1) * d] = (dm * y * gate * (1.0 - gate)).astype(BF16)
            do_ref[...] = _dot_nt(dy, wb_ref[n]).astype(BF16)
            dwb_ref[n] += _dot_tn(o_n, dy)
        dwo_ref[...] += _dot_tn(merged.astype(BF16), dhb)

    gate = lambda n: pl.BlockSpec((tb, d), lambda i, n=n: (i, n))
    wb_spec = pl.BlockSpec((3, 512, d), lambda i: (0, 0, 0))
    wo_spec = pl.BlockSpec((d, d), lambda i: (0, 0))
    return pl.pallas_call(
        body, name=name, grid=(lp // tb,),
        in_specs=[row(d), gate(0), gate(1), gate(2), row(512), row(512), row(512), wb_spec, wo_spec],
        out_specs=[row(GATES_W), row(512), row(512), row(512), wb_spec, wo_spec],
        out_shape=[SDS((lp, GATES_W), BF16)] + [SDS((lp, 512), BF16)] * 3 + [SDS((3, 512, d), F32), SDS((d, d), F32)],
        compiler_params=_params(("arbitrary",)),
    )(dh, proj, proj, proj, *outs, wb, wo)


def _shift_down(x, halo, n, first):
    rows = lax.broadcasted_iota(jnp.int32, x.shape, 0)
    edge = jnp.concatenate([pltpu.roll(halo, n, 0), jnp.zeros((x.shape[0] - 8, x.shape[1]), F32)], axis=0)
    edge = jnp.where(first, 0.0, edge)
    return jnp.where(rows < n, edge, pltpu.roll(x, n, 0))


def _shift_up(x, halo, n, last):
    tb = x.shape[0]
    rows = lax.broadcasted_iota(jnp.int32, x.shape, 0)
    edge = jnp.concatenate([jnp.zeros((tb - 8, x.shape[1]), F32), pltpu.roll(halo, 8 - n, 0)], axis=0)
    edge = jnp.where(last, 0.0, edge)
    return jnp.where(rows >= tb - n, edge, pltpu.roll(x, tb - n, 0))


def _conv(u, halo, w_ref, b_ref, first):
    taps = (_shift_down(u, halo, 2, first), _shift_down(u, halo, 1, first), u)
    c = b_ref[...] + w_ref[0:1, :] * taps[0] + w_ref[1:2, :] * taps[1] + w_ref[2:3, :] * taps[2]
    return c, taps


def _ffn_specs(tb, f):
    hb = tb // 8
    cur = lambda c: pl.BlockSpec((tb, f), lambda i, c=c: (i, c))
    prev = lambda c: pl.BlockSpec((8, f), lambda i, c=c: (jnp.maximum(i * hb - 1, 0), c))
    vec = lambda r, c: pl.BlockSpec((r, f), lambda i, c=c: (0, c))
    return cur, prev, vec


def _ffn_act_fwd(u, cw, cb, name):
    lp = u.shape[0]
    f = D_FF
    tb = TILE_ROW
    cur, prev, vec = _ffn_specs(tb, f)

    def body(ug, uv, hg, hv, wg, wv, bg, bv, o_ref):
        first = pl.program_id(0) == 0
        cg, _ = _conv(ug[...], hg[...], wg, bg, first)
        cv, _ = _conv(uv[...], hv[...], wv, bv, first)
        o_ref[...] = (cg * jax.nn.sigmoid(cg) * cv).astype(BF16)

    return pl.pallas_call(
        body, name=name, grid=(lp // tb,),
        in_specs=[cur(0), cur(1), prev(0), prev(1), vec(8, 0), vec(8, 1), vec(1, 0), vec(1, 1)],
        out_specs=pl.BlockSpec((tb, f), lambda i: (i, 0)), out_shape=SDS((lp, f), BF16),
        compiler_params=_params(("parallel",)),
    )(u, u, u, u, cw, cw, cb, cb)


def _ffn_act_bwd_conv(u, dact, cw, cb, name):
    lp = u.shape[0]
    f = D_FF
    tb = TILE_CONV_BWD
    cur, prev, vec = _ffn_specs(tb, f)

    def body(ug, uv, hg, hv, wg, wv, bg, bv, da_ref, dc_ref, dw_ref, db_ref):
        first = pl.program_id(0) == 0

        @pl.when(first)
        def _():
            dw_ref[...] = jnp.zeros_like(dw_ref)
            db_ref[...] = jnp.zeros_like(db_ref)

        cg, tg = _conv(ug[...], hg[...], wg, bg, first)
        cv, tv = _conv(uv[...], hv[...], wv, bv, first)
        da = da_ref[...]
        sg = jax.nn.sigmoid(cg)
        dcg = da * cv * sg * (1.0 + cg * (1.0 - sg))
        dcv = da * cg * sg
        for c, (dc, taps) in enumerate(((dcg, tg), (dcv, tv))):
            dc_ref[:, c * f:(c + 1) * f] = dc
            for n in range(3):
                dw_ref[n:n + 1, c * f:(c + 1) * f] += jnp.sum(dc * taps[n], axis=0, keepdims=True)
            db_ref[0:1, c * f:(c + 1) * f] += jnp.sum(dc, axis=0, keepdims=True)

    acc = pl.BlockSpec((8, 2 * f), lambda i: (0, 0))
    return pl.pallas_call(
        body, name=name, grid=(lp // tb,),
        in_specs=[cur(0), cur(1), prev(0), prev(1), vec(8, 0), vec(8, 1), vec(1, 0), vec(1, 1),
                  pl.BlockSpec((tb, f), lambda i: (i, 0))],
        out_specs=[pl.BlockSpec((tb, 2 * f), lambda i: (i, 0)), acc, acc],
        out_shape=[SDS((lp, 2 * f), F32), SDS((8, 2 * f), F32), SDS((8, 2 * f), F32)],
        compiler_params=_params(("arbitrary",)),
    )(u, u, u, u, cw, cw, cb, cb, dact)


def _ffn_act_bwd_in(dc, cw, name):
    lp = dc.shape[0]
    f2 = 2 * D_FF
    tb = TILE_ROW
    nb = lp // tb
    hb = tb // 8

    def body(dc_ref, n_ref, w_ref, o_ref):
        last = pl.program_id(0) == nb - 1
        dcv, halo = dc_ref[...], n_ref[...]
        du = (w_ref[2:3, :] * dcv + w_ref[1:2, :] * _shift_up(dcv, halo, 1, last)
              + w_ref[0:1, :] * _shift_up(dcv, halo, 2, last))
        o_ref[...] = du.astype(BF16)

    cur = pl.BlockSpec((tb, f2), lambda i: (i, 0))
    return pl.pallas_call(
        body, name=name, grid=(nb,),
        in_specs=[cur, pl.BlockSpec((8, f2), lambda i: (jnp.minimum((i + 1) * hb, nb * hb - 1), 0)),
                  pl.BlockSpec((8, f2), lambda i: (0, 0))],
        out_specs=cur, out_shape=SDS((lp, f2), BF16),
        compiler_params=_params(("parallel",)),
    )(dc, dc, cw)


def _loss_head(y, target, n_real, name):
    lp, d = y.shape
    tb = TILE_MM

    def body(y_ref, t_ref, dy_ref, loss_ref):
        i = pl.program_id(0)

        @pl.when(i == 0)
        def _():
            loss_ref[...] = jnp.zeros_like(loss_ref)

        rows = i * tb + lax.broadcasted_iota(jnp.int32, (tb, 1), 0)
        real = (rows >= N_META) & (rows < N_META + n_real)
        diff = jnp.where(real, y_ref[...] - t_ref[...], 0.0)
        dy_ref[...] = diff * (1.0 / d)
        loss_ref[...] += (0.5 / d) * jnp.sum(diff * diff).reshape(1, 1)

    row = pl.BlockSpec((tb, d), lambda i: (i, 0))
    return pl.pallas_call(
        body, name=name, grid=(lp // tb,), in_specs=[row, row],
        out_specs=[row, pl.BlockSpec((1, 1), lambda i: (0, 0))],
        out_shape=[SDS((lp, d), F32), SDS((1, 1), F32)],
        compiler_params=_params(("arbitrary",)),
    )(y, target)


def _pad_lanes(v, width, at=0):
    return jnp.pad(v.astype(F32), (at, width - at - v.shape[0]))[None, :]


def _mix_params(w, big, l):
    b = lambda a: a.astype(BF16)
    win = big["w_in"]
    fq, fk, fv, ff, cq, ckv, kr, sq, sk, sv, gates = jnp.split(
        win, [512, 1024, 1536, 1544, 1800, 1928, 1960, 2472, 2600, 2728], axis=1)
    misc = jnp.concatenate([kr, ff, jnp.zeros((D_MODEL, 88), win.dtype)], axis=1)
    w_in = b(jnp.concatenate([gates, fq, fk, fv, sq, sk, sv, cq, ckv, misc], axis=1))
    wq = jnp.pad(big["mla_w_q_up"].reshape(256, HEADS, 96), ((0, 0), (0, 0), (0, 32))).reshape(256, 1024)
    wkv = big["mla_w_kv_up"].reshape(128, HEADS, 128)
    wkk = jnp.pad(wkv[:, :, :64], ((0, 0), (0, 0), (0, 64))).reshape(128, 1024)
    wkvv = wkv[:, :, 64:].reshape(128, 512)
    tile = lambda g, n: jnp.tile(g.astype(F32), n)[None, :]
    prm = [tile(w["fox_q_g"][l], 8), tile(w["fox_k_g"][l], 8), tile(w["swa_q_g"][l], 8), tile(w["swa_k_g"][l], 2),
           _pad_lanes(w["fox_forget_b"][l], 128, FF_LANE), w["mla_q_a_g"][l][None, :], w["mla_kv_a_g"][l][None, :],
           tile(jnp.pad(w["mla_q_g"][l], (0, 32)), 8), tile(jnp.pad(w["mla_k_g"][l], (0, 32)), 8),
           wq.astype(F32), wkk.astype(F32), wkvv.astype(F32)]
    return dict(g1=w["norm1_g"][l][None, :], w_in=w_in, prm=prm, sinks=w["swa_sinks"][l].astype(F32),
                wb=b(big["w_branch"]), wo=b(big["w_o"]))


def _ffn_params(w, big, l):
    cw = jnp.pad(w["ffn_conv_w"][l].astype(F32), ((0, 5), (0, 0)))
    return dict(g2=w["norm2_g"][l][None, :], w_up=big["ffn_w_up"].astype(BF16), cw=cw,
                cb=w["ffn_conv_b"][l][None, :].astype(F32), w_down=big["ffn_w_down"].astype(BF16))


def _cols(c):
    ct = c[:, FF_LANE:FF_LANE + HEADS].T
    return ct[:, :, None], ct[:, None, :]


def _from_rows(row):
    return jnp.pad(row[:, 0, :].T, ((0, 0), (FF_LANE, 128 - FF_LANE - HEADS)))


def _layer_fwd_mix(h, lw, consts, cos, sin, slopes, l):
    tag = f"l{l}_"
    xn, proj = _norm_matmul(h, lw["g1"], lw["w_in"], IN_W, tag + "in_proj")
    fq, fk, fv, mq, mk, mv, sq, skd, svd, ls = _prep_fwd(proj, lw["prm"], consts, cos, sin, tag + "prep")
    c = _cumsum([ls], False, tag + "decay_cumsum")
    c_col, c_row = _cols(c)
    oa, *lse_a = _att_fwd(_Att("fox"), fq, fk, fv, (c_row,), tag + "fox_fwd")
    ob, *lse_b = _att_fwd(_Att("mla"), mq, mk, mv, (), tag + "mla_fwd")
    oc, *lse_c = _att_fwd(_Att("swa"), sq, skd, svd, (lw["sinks"], slopes), tag + "swa_fwd")
    h2 = _post_fwd(h, proj, (oa, ob, oc), lw["wb"], lw["wo"], tag + "merge")
    saved = dict(h=h, xn=xn, proj=proj, q=(fq, mq, sq), k=(fk, mk, skd), v=(fv, mv, svd), c=(c_col, c_row),
                 o=(oa, ob, oc), lse=(lse_a, lse_b, lse_c), h2=h2)
    return h2, saved


def _layer_fwd_ffn(h2, lw, l):
    tag = f"l{l}_"
    xn2, u = _norm_matmul(h2, lw["g2"], lw["w_up"], 2 * D_FF, tag + "ffn_up")
    act = _ffn_act_fwd(u, lw["cw"], lw["cb"], tag + "ffn_act")
    h3 = _matmul_residual(act, lw["w_down"], h2, tag + "ffn_down")
    return h3, dict(xn2=xn2, u=u, act=act)


def _layer_bwd_ffn(dh3, lw, sv, l):
    tag = f"l{l}_"
    f = D_FF
    dact = _matmul_nt(dh3, lw["w_down"], f, tag + "ffn_down_dx")
    dw_down = _matmul_tn(sv["act"], dh3, D_MODEL, tag + "ffn_down_dw")
    dc, dcw, dcb = _ffn_act_bwd_conv(sv["u"], dact, lw["cw"], lw["cb"], tag + "ffn_act_dc")
    du = _ffn_act_bwd_in(dc, lw["cw"], tag + "ffn_act_du")
    dw_up = _matmul_tn(sv["xn2"], du, f, tag + "ffn_up_dw")
    dh2, dg2 = _norm_matmul_bwd([du], lw["w_up"], sv["h2"], lw["g2"], dh3, tag + "ffn_up_dx")
    g = dict(norm2_g=dg2[0], ffn_w_up=dw_up, ffn_conv_w=dcw[0:3], ffn_conv_b=dcb[0], ffn_w_down=dw_down)
    return dh2, g


def _layer_bwd_mix(dh2, lw, sv, consts, folds, cos, sin, slopes, l, hook=None, merge_hook=None):
    tag = f"l{l}_"
    dgates, doa, dob, doc, dwb, dwo = _post_bwd(dh2, sv["proj"], sv["o"], lw["wb"], lw["wo"], tag + "merge_bwd")
    c_col, c_row = sv["c"]
    tick = merge_hook({"w_branch": dwb, "w_o": dwo}) if merge_hook else None
    if tick is not None:
        c_row = c_row + tick
    extras = ((c_row,), (), (lw["sinks"], slopes))
    extras_kv = ((c_col,), (), (slopes,))
    grads = []
    for n, (mode, do) in enumerate((("fox", doa), ("mla", dob), ("swa", doc))):
        att = _Att(mode)
        q, k, v = sv["q"][n], sv["k"][n], sv["v"][n]
        lse_col, lse_row = sv["lse"][n]
        res = _att_dq(att, q, k, v, sv["o"][n], do, lse_col, extras[n], tag + mode + "_dq")
        dq, delta_row = res[0], res[2]
        res_kv = _att_dkv(att, q, k, v, do, lse_row, delta_row, extras_kv[n], tag + mode + "_dkv")
        grads.append((dq, res_kv[0], res_kv[1], res[3:], res_kv[2:]))
    (dfq, dfk, dfv, (dcq,), (dck,)), (dmq, dmk, dmv, _, _), (dsq, dskp, dsvp, (dsink,), _) = grads
    dls = _cumsum([_from_rows(dcq), _from_rows(dck)], True, tag + "decay_cumsum_bwd")
    res = _prep_bwd(sv["proj"], lw["prm"], consts, cos, sin,
                    (dfq, dfk, dfv, dmq, dmk, dmv, dsq, dskp, dsvp, dls), folds, tag + "prep_bwd")
    dother, pg = res[0], res[1:]
    dw_g = _matmul_tn(sv["xn"], dgates, GATES_W, tag + "in_proj_dw_gates")
    dw_o = _matmul_tn(sv["xn"], dother, OTHER_W, tag + "in_proj_dw_other")
    d_in = jnp.concatenate([
        dw_o[:, O_FQ:O_FV + 512], dw_o[:, O_MISC + FF_LANE:O_MISC + FF_LANE + 8], dw_o[:, O_CQ:O_CQ + 256],
        dw_o[:, O_CKV:O_CKV + 128], dw_o[:, O_MISC:O_MISC + 32], dw_o[:, O_SQ:O_SQ + 512], dw_o[:, O_SK:O_SK + 128],
        dw_o[:, O_SV:O_SV + 128], dw_g], axis=1)
    d_wq = pg[9].reshape(256, HEADS, 128)[:, :, :96].reshape(256, 768)
    d_wkv = jnp.concatenate([pg[10].reshape(128, HEADS, 128)[:, :, :64], pg[11].reshape(128, HEADS, 64)],
                            axis=2).reshape(128, 1024)
    g = dict(
        w_in=d_in, fox_forget_b=pg[4][0, FF_LANE:FF_LANE + 8], fox_q_g=pg[0][0, :64],
        fox_k_g=pg[1][0, :64], mla_q_a_g=pg[5][0], mla_w_q_up=d_wq, mla_kv_a_g=pg[6][0], mla_w_kv_up=d_wkv,
        mla_q_g=pg[7][0, :96], mla_k_g=pg[8][0, :96], swa_q_g=pg[2][0, :64], swa_k_g=pg[3][0, :64],
        swa_sinks=dsink[:, 0, 0:2 * PAIRS].reshape(HEADS), w_branch=dwb, w_o=dwo)
    tick = hook(g) if hook else None
    g1 = lw["g1"] if tick is None else lw["g1"] + tick
    dh, dg1 = _norm_matmul_bwd([dgates, dother], lw["w_in"], sv["h"], g1, dh2, tag + "in_proj_dx")
    g["norm1_g"] = dg1[0]
    return dh, g


_MIX_BIG = ("w_in", "mla_w_q_up", "mla_w_kv_up", "w_branch", "w_o")
_FFN_BIG = ("ffn_w_up", "ffn_w_down")


def _local_step(x, target, w, hook=None, fetch=None):
    if fetch is None:
        fetch = lambda l, stage, after: {n: w[n][l] for n in (_MIX_BIG if stage == "mix" else _FFN_BIG)}
    seq = x.shape[0]
    length = N_META + seq
    lp = -(-length // ROW_ALIGN) * ROW_ALIGN
    pad = lp - length
    h = jnp.concatenate([w["meta_tokens"].astype(F32), x, jnp.zeros((pad, D_MODEL), F32)], axis=0)
    tgt = jnp.pad(target, ((N_META, pad), (0, 0)))
    consts = _consts()
    folds = (_fold_matrix(512, 64), _fold_matrix(1024, 128))
    cos, sin = _rope_tables(lp)
    slopes = jnp.asarray(2.0 ** (-8.0 * np.arange(1, HEADS + 1, dtype=np.float32) / HEADS), F32)
    lws, saved = [], []
    for l in range(DEPTH):
        lw = _mix_params(w, fetch(l, "mix", h), l)
        h, sv = _layer_fwd_mix(h, lw, consts, cos, sin, slopes, l)
        lw.update(_ffn_params(w, fetch(l, "ffn", h), l))
        h, sv_ffn = _layer_fwd_ffn(h, lw, l)
        lws.append(lw)
        saved.append({**sv, **sv_ffn})
    dh, loss = _loss_head(h, tgt, seq, "loss_head")
    grads = [None] * DEPTH
    for l in reversed(range(DEPTH)):
        dh, g_ffn = _layer_bwd_ffn(dh, lws[l], saved[l], l)
        tick = hook(l, "ffn", g_ffn) if hook else None
        if tick is not None:
            lws[l]["sinks"] = lws[l]["sinks"] + tick
        mix_hook = (lambda g, l=l, g_ffn=g_ffn: hook(l, "mix", {**g_ffn, **g})) if hook else None
        merge_hook = (lambda g, l=l: hook(l, "merge", g)) if hook else None
        dh, g_mix = _layer_bwd_mix(dh, lws[l], saved[l], consts, folds, cos, sin, slopes, l, mix_hook, merge_hook)
        grads[l] = {**g_ffn, **g_mix}
    return loss, dh[N_META:length], dh[:N_META], grads


def _place():
    return lax.axis_index("x"), lax.axis_index("y"), lax.axis_index("c")


def _flip(pos, k):
    x, y, c = pos
    return (1 - x if k & 4 else x, 1 - y if k & 2 else y, 1 - c if k & 1 else c)


def _index(pos):
    return 4 * pos[0] + 2 * pos[1] + pos[2]


def _gather(tensors, name):
    n_t = len(tensors)

    def body(*refs):
        ins, outs = refs[:n_t], refs[n_t:2 * n_t]
        send_sems, recv_sems, local_sems = refs[2 * n_t:]
        x, y, c = _place()
        me, sibling = (x, y, c), (x, y, 1 - c)
        chips = [(1 - x, y), (x, 1 - y), (1 - x, 1 - y)]

        def copy(t, k, block, to, src=None):
            dst = outs[t].at[_index(block)]
            return pltpu.make_async_remote_copy(
                src_ref=dst if src is None else src, dst_ref=dst, send_sem=send_sems.at[t, k],
                recv_sem=recv_sems.at[t, k], device_id=to, device_id_type=pl.DeviceIdType.MESH)

        local, sent = [], []
        for t in range(n_t):
            local.append(pltpu.make_async_copy(ins[t], outs[t].at[_index(me)], local_sems.at[t]))
            local[-1].start()
            sent.append(copy(t, 0, me, sibling, src=ins[t]))
            sent += [copy(t, 1 + j, me, (*chip, c), src=ins[t]) for j, chip in enumerate(chips)]
        for cp in sent:
            cp.start()
        for j, chip in enumerate(chips):
            for t in range(n_t):
                copy(t, 1 + j, (*chip, c), me).wait_recv()
                sent.append(copy(t, 4 + j, (*chip, c), sibling))
                sent[-1].start()
        for t in range(n_t):
            copy(t, 0, sibling, me).wait_recv()
            for j, chip in enumerate(chips):
                copy(t, 4 + j, (*chip, 1 - c), me).wait_recv()
        for cp in sent:
            cp.wait_send()
        for cp in local:
            cp.wait()

    any_spec = pl.BlockSpec(memory_space=pl.ANY)
    return pl.pallas_call(
        body, name=name, in_specs=[any_spec] * n_t, out_specs=[any_spec] * n_t,
        out_shape=[SDS((N_DEV,) + a.shape, a.dtype) for a in tensors],
        scratch_shapes=[pltpu.SemaphoreType.DMA((n_t, N_DEV - 1)), pltpu.SemaphoreType.DMA((n_t, N_DEV - 1)),
                        pltpu.SemaphoreType.DMA((n_t,))],
    )(*tensors)


def _exchange_start(tensors, name, gather=False, after=None):
    n_t = len(tensors)

    def body(*refs):
        ins, lands = refs[:n_t], refs[n_t:2 * n_t]
        send_sem, recv_sem = refs[2 * n_t + 1:2 * n_t + 3]
        token = refs[-1]
        me = _place()
        mine = _index(me)
        for t in range(n_t):
            for k in range(1, N_DEV):
                peer = _flip(me, k)
                pltpu.make_async_remote_copy(
                    src_ref=ins[t] if gather else ins[t].at[_index(peer)], dst_ref=lands[t].at[mine],
                    send_sem=send_sem, recv_sem=recv_sem, device_id=peer, device_id_type=pl.DeviceIdType.MESH).start()
        token[...] = jnp.zeros_like(token)

    hbm = pl.BlockSpec(memory_space=pltpu.HBM)
    sem = pl.BlockSpec(memory_space=pltpu.SEMAPHORE)
    one = pltpu.SemaphoreType.DMA(())
    land_shape = lambda a: ((N_DEV,) + a.shape) if gather else a.shape
    bufs = ([pltpu.HBM(a.shape, a.dtype) for a in tensors] + [pltpu.HBM(land_shape(a), a.dtype) for a in tensors])
    after = jnp.zeros((8, 128), F32) if after is None else after
    outs = pl.pallas_call(
        body, name=name, in_specs=[hbm] * (2 * n_t) + [pl.BlockSpec(memory_space=pl.ANY)],
        out_specs=[sem, sem] + [hbm] * (2 * n_t) + [pl.BlockSpec(memory_space=pltpu.VMEM)],
        out_shape=[one, one] + bufs + [SDS((8, 128), F32)],
        input_output_aliases={i: 2 + i for i in range(2 * n_t)},
        compiler_params=pltpu.CompilerParams(has_side_effects=pltpu.SideEffectType.DATAFLOW_SIDE_EFFECTING),
    )(*[pltpu.with_memory_space_constraint(a, pltpu.HBM) for a in tensors],
      *[pltpu.with_memory_space_constraint(lax.empty(land_shape(a), a.dtype), pltpu.HBM) for a in tensors], after)
    return outs[:-1], outs[-1][0, 0]


def _exchange_wait(state, after, name, gather=False):
    n_t = (len(state) - 2) // 2

    def body(*refs):
        send_sem, recv_sem = refs[0:2]
        ins, lands = refs[2:2 + n_t], refs[2 + n_t:2 + 2 * n_t]
        me = _place()
        for t in range(n_t):
            for k in range(1, N_DEV):
                peer = _flip(me, k)
                copy = pltpu.make_async_remote_copy(
                    src_ref=ins[t] if gather else ins[t].at[_index(peer)], dst_ref=lands[t].at[_index(peer)],
                    send_sem=send_sem, recv_sem=recv_sem, device_id=peer, device_id_type=pl.DeviceIdType.MESH)
                copy.wait_send()
                copy.wait_recv()

    hbm = pl.BlockSpec(memory_space=pltpu.HBM)
    sem = pl.BlockSpec(memory_space=pltpu.SEMAPHORE)
    bufs = [pltpu.HBM(a.shape, a.dtype) for a in state[2:]]
    outs = pl.pallas_call(
        body, name=name, in_specs=[sem, sem] + [hbm] * (2 * n_t) + [pl.BlockSpec(memory_space=pl.ANY)],
        out_specs=[hbm] * (2 * n_t), out_shape=bufs,
        input_output_aliases={2 + i: i for i in range(2 * n_t)},
        compiler_params=pltpu.CompilerParams(has_side_effects=pltpu.SideEffectType.DATAFLOW_SIDE_EFFECTING),
    )(*state, after)
    return outs[n_t:]


def _sum_slots(parts, name):
    n, rows, w = parts.shape
    tb = 8

    def body(p_ref, o_ref):
        acc = p_ref[0].astype(F32)
        for s in range(1, n):
            acc = acc + p_ref[s].astype(F32)
        o_ref[...] = acc

    return pl.pallas_call(
        body, name=name, grid=(rows // tb,),
        in_specs=[pl.BlockSpec((n, tb, w), lambda i: (0, i, 0))], out_specs=pl.BlockSpec((tb, w), lambda i: (i, 0)),
        out_shape=SDS((rows, w), F32), compiler_params=_params(("parallel",)),
    )(parts)


def _adamw(wt, m, v, parts, name, own=None):
    shape = wt.shape
    parts = parts if isinstance(parts, (list, tuple)) else [parts]
    n, w = parts[0].shape[0], shape[-1]
    rows = math.prod(shape[:-1])
    per = rows // len(parts)
    step = 16 if parts[0].dtype == BF16 else 8
    tb = max([t for t in range(step, 257, step) if per % t == 0] or [per])
    nb = per // tb
    c1 = 1.0 / (1.0 - ADAM_B1 ** ADAM_STEP)
    c2 = 1.0 / (1.0 - ADAM_B2 ** ADAM_STEP)
    state = [a.reshape(rows, w) for a in (wt, m, v)]
    n_in = 4 if own is None else 5
    outs = None
    for l in reversed(range(len(parts))):
        def body(*refs):
            idx_ref = None if own is None else refs[0]
            w_ref, m_ref, v_ref, p_ref = refs[n_in - 4:n_in] if own is None else refs[1:5]
            g_out, d_out, m_out, v_out = refs[-4:]
            g = None
            for s in range(n):
                term = p_ref[s] if own is None else jnp.where(idx_ref[0] == s, refs[5][0], p_ref[s])
                g = term.astype(F32) if g is None else g + term.astype(F32)
            m_new = ADAM_B1 * m_ref[...] + (1.0 - ADAM_B1) * g
            v_new = ADAM_B2 * v_ref[...] + (1.0 - ADAM_B2) * (g * g)
            g_out[...] = g
            m_out[...] = m_new
            v_out[...] = v_new
            d_out[...] = -ADAM_LR * ((m_new * c1) / (jnp.sqrt(v_new * c2) + ADAM_EPS) + ADAM_WD * w_ref[...])

        row = pl.BlockSpec((tb, w), lambda i, *_, l=l: (l * nb + i, 0))
        in_specs = [row, row, row, pl.BlockSpec((n, tb, w), lambda i, *_: (0, i, 0))]
        args = [*state, parts[l].reshape(n, per, w)]
        if own is not None:
            in_specs.append(pl.BlockSpec((1, tb, w), lambda i, idx: (idx[0], i, 0)))
            args.append(own[l].reshape(n, per, w))
        prev = [] if outs is None else list(outs)
        in_specs += [pl.BlockSpec(memory_space=pl.ANY)] * len(prev)
        n_pre = 0 if own is None else 1
        call = dict(name=f"{name}_{l}", out_shape=[SDS((rows, w), F32)] * 4,
                    input_output_aliases={n_pre + len(args) + k: k for k in range(len(prev))},
                    compiler_params=_params(("parallel",)))
        if own is None:
            outs = pl.pallas_call(body, grid=(nb,), in_specs=in_specs, out_specs=[row] * 4, **call)(*args, *prev)
        else:
            spec = pltpu.PrefetchScalarGridSpec(num_scalar_prefetch=1, grid=(nb,), in_specs=in_specs, out_specs=[row] * 4)
            idx = jnp.reshape(_index(_place()), (1,)).astype(jnp.int32)
            outs = pl.pallas_call(body, grid_spec=spec, **call)(idx, *args, *prev)
    return [o.reshape(shape) for o in outs]


_BIG = [("w_in", 2), ("mla_w_q_up", 2), ("mla_w_kv_up", 2), ("w_branch", 3), ("w_o", 1), ("ffn_w_up", 2), ("ffn_w_down", 1)]
_SMALL_SHARDED = [("meta_tokens", 1), ("ffn_conv_w", 2)]
_REPLICATED = ["norm1_g", "fox_forget_b", "fox_q_g", "fox_k_g", "mla_q_a_g", "mla_kv_a_g", "mla_q_g", "mla_k_g",
               "swa_q_g", "swa_k_g", "swa_sinks", "norm2_g", "ffn_conv_b"]
_ORDER = ["meta_tokens", "norm1_g", "w_in", "fox_forget_b", "fox_q_g", "fox_k_g", "mla_q_a_g", "mla_w_q_up",
          "mla_kv_a_g", "mla_w_kv_up", "mla_q_g", "mla_k_g", "swa_q_g", "swa_k_g", "swa_sinks", "w_branch", "w_o",
          "norm2_g", "ffn_w_up", "ffn_conv_w", "ffn_conv_b", "ffn_w_down"]


def _flat_rows(vecs, dtype, row_mult):
    flat = jnp.concatenate([a.reshape(-1).astype(dtype) for a in vecs])
    rows = -(-flat.shape[0] // (1024 * row_mult)) * row_mult
    return jnp.pad(flat, (0, rows * 1024 - flat.shape[0])).reshape(rows, 1024)


def _unflatten(flat, shapes):
    out, off = [], 0
    for s in shapes:
        n = math.prod(s)
        out.append(flat[off:off + n].reshape(s))
        off += n
    return out


def _to_full(blocks, axis):
    moved = jnp.moveaxis(blocks, 0, axis)
    s = moved.shape
    return moved.reshape(s[:axis] + (s[axis] * s[axis + 1],) + s[axis + 2:])


def _to_blocks(full, axis):
    s = full.shape
    split = full.reshape(s[:axis] + (N_DEV, s[axis] // N_DEV) + s[axis + 1:])
    return jnp.moveaxis(split, axis, 0)


def kernel(x, meta_tokens, norm1_g, w_in, fox_forget_b, fox_q_g, fox_k_g, mla_q_a_g, mla_w_q_up, mla_kv_a_g, mla_w_kv_up, mla_q_g, mla_k_g, swa_q_g, swa_k_g, swa_sinks, w_branch, w_o, norm2_g, ffn_w_up, ffn_conv_w, ffn_conv_b, ffn_w_down, loss_target, m_meta_tokens, m_norm1_g, m_w_in, m_fox_forget_b, m_fox_q_g, m_fox_k_g, m_mla_q_a_g, m_mla_w_q_up, m_mla_kv_a_g, m_mla_w_kv_up, m_mla_q_g, m_mla_k_g, m_swa_q_g, m_swa_k_g, m_swa_sinks, m_w_branch, m_w_o, m_norm2_g, m_ffn_w_up, m_ffn_conv_w, m_ffn_conv_b, m_ffn_w_down, v_meta_tokens, v_norm1_g, v_w_in, v_fox_forget_b, v_fox_q_g, v_fox_k_g, v_mla_q_a_g, v_mla_w_q_up, v_mla_kv_a_g, v_mla_w_kv_up, v_mla_q_g, v_mla_k_g, v_swa_q_g, v_swa_k_g, v_swa_sinks, v_w_branch, v_w_o, v_norm2_g, v_ffn_w_up, v_ffn_conv_w, v_ffn_conv_b, v_ffn_w_down):
    wl = dict(zip(_ORDER, (meta_tokens, norm1_g, w_in, fox_forget_b, fox_q_g, fox_k_g, mla_q_a_g, mla_w_q_up,
                           mla_kv_a_g, mla_w_kv_up, mla_q_g, mla_k_g, swa_q_g, swa_k_g, swa_sinks, w_branch, w_o,
                           norm2_g, ffn_w_up, ffn_conv_w, ffn_conv_b, ffn_w_down)))
    ml = dict(zip(_ORDER, (m_meta_tokens, m_norm1_g, m_w_in, m_fox_forget_b, m_fox_q_g, m_fox_k_g, m_mla_q_a_g,
                           m_mla_w_q_up, m_mla_kv_a_g, m_mla_w_kv_up, m_mla_q_g, m_mla_k_g, m_swa_q_g, m_swa_k_g,
                           m_swa_sinks, m_w_branch, m_w_o, m_norm2_g, m_ffn_w_up, m_ffn_conv_w, m_ffn_conv_b,
                           m_ffn_w_down)))
    vl = dict(zip(_ORDER, (v_meta_tokens, v_norm1_g, v_w_in, v_fox_forget_b, v_fox_q_g, v_fox_k_g, v_mla_q_a_g,
                           v_mla_w_q_up, v_mla_kv_a_g, v_mla_w_kv_up, v_mla_q_g, v_mla_k_g, v_swa_q_g, v_swa_k_g,
                           v_swa_sinks, v_w_branch, v_w_o, v_norm2_g, v_ffn_w_up, v_ffn_conv_w, v_ffn_conv_b,
                           v_ffn_w_down)))
    small_sh = [n for n, _ in _SMALL_SHARDED]
    big = [n for n, _ in _BIG]
    axis_of = dict(_BIG)
    idx = _index(_place())

    def to_full(n, blocks, own=None):
        if own is not None:
            sel = (jnp.arange(N_DEV) == idx).reshape((N_DEV,) + (1,) * own.ndim)
            blocks = jnp.where(sel, own[None], blocks)
        return _to_full(blocks, axis_of[n] - 1)

    local = {(n, l): wl[n][l].astype(BF16) for n in big for l in range(DEPTH)}
    got = _gather([local[(n, 0)] for n in _MIX_BIG] + [wl[n] for n in small_sh], "gather_weights_l0_mix")
    full = {n: wl[n] for n in _REPLICATED}
    for (n, axis), blocks in zip(_SMALL_SHARDED, got[len(_MIX_BIG):]):
        full[n] = _to_full(blocks, axis)
    ready = {(n, 0): to_full(n, blocks) for n, blocks in zip(_MIX_BIG, got)}
    later = {"l0_ffn": [(n, 0) for n in _FFN_BIG], "l1": [(n, 1) for n in big]}
    states = {}
    for key, names in later.items():
        states[key], tick = _exchange_start([local[e] for e in names], "gather_weights_" + key + "_start", True, got[0])
        full["norm1_g"] = full["norm1_g"] + tick

    def fetch(l, stage, after):
        key = "l0_ffn" if l == 0 else "l1"
        if (l, stage) != (0, "mix") and key in states:
            lands = _exchange_wait(states.pop(key), after, "gather_weights_" + key + "_wait", True)
            ready.update({e: to_full(e[0], blocks, local[e]) for e, blocks in zip(later[key], lands)})
        return {n: ready[(n, l)] for n in (_MIX_BIG if stage == "mix" else _FFN_BIG)}

    blocks_of = lambda g, names: [_to_blocks(g[n], axis_of[n] - 1).astype(BF16) for n in names]
    early = {}

    def hook(l, stage, g):
        if l == DEPTH - 1 and stage == "mix":
            key, names = "l1", big
        elif l == 0:
            merge = ("w_branch", "w_o")
            groups = {"ffn": _FFN_BIG, "merge": merge, "mix": tuple(n for n in _MIX_BIG if n not in merge)}
            key, names = "l0_" + stage, groups[stage]
        else:
            return None
        sends = blocks_of(g, names)
        state, tick = _exchange_start(sends, "exchange_grads_" + key + "_start")
        early[key] = (names, l, sends, state)
        return tick

    loss, grad_x, grad_meta, grads = _local_step(x[0], loss_target[0], full, hook, fetch)
    result = {kind: {} for kind in ("grad", "delta", "new_m", "new_v")}
    small_grads = {k: jnp.stack([grads[l][k] for l in range(DEPTH)]) for k in grads[0] if k not in big}
    small_grads["meta_tokens"] = grad_meta
    small_full = _REPLICATED + small_sh
    mine_small = _flat_rows([small_grads[n] for n in small_full] + [loss], F32, 8)
    small_state, _ = _exchange_start([mine_small], "gather_small_grads_start", True)
    landed, sent = {}, {}
    after = early["l0_mix"][2][0]
    for key in ("l1", "l0_ffn", "l0_merge"):
        names, l, sends, state = early[key]
        got = _exchange_wait(state, after, "exchange_grads_" + key + "_wait")
        landed.update({(n, l): p for n, p in zip(names, got)})
        sent.update({(n, l): p for n, p in zip(names, sends)})

    def update(names):
        for n in names:
            outs = _adamw(wl[n], ml[n], vl[n], [landed[(n, l)] for l in range(DEPTH)], "adamw_" + n,
                          [sent[(n, l)] for l in range(DEPTH)])
            for kind, val in zip(result, outs):
                result[kind][n] = val

    update(_FFN_BIG)
    got_small = _exchange_wait(small_state, result["grad"]["ffn_w_down"], "gather_small_grads_wait", True)[0]
    sel = (jnp.arange(N_DEV) == idx).reshape(N_DEV, 1, 1)
    total_small = _sum_slots(jnp.where(sel, mine_small[None], got_small), "sum_small_grads").reshape(-1)
    pieces = _unflatten(total_small, [small_grads[n].shape for n in small_full] + [()])
    loss_total = pieces[-1]
    g_small = dict(zip(small_full, pieces[:-1]))
    for n, axis in _SMALL_SHARDED:
        size = wl[n].shape[axis]
        g_small[n] = lax.dynamic_slice_in_dim(g_small[n], idx * size, size, axis)
    flat = lambda d: _flat_rows([d[n] for n in small_full], F32, 8)
    small_out = _adamw(flat(wl), flat(ml), flat(vl), flat(g_small)[None], "adamw_small")
    for kind, fs in zip(result, small_out):
        result[kind].update(zip(small_full, _unflatten(fs.reshape(-1), [wl[n].shape for n in small_full])))
    names, l, sends, state = early["l0_mix"]
    got = _exchange_wait(state, small_out[0], "exchange_grads_l0_mix_wait")
    landed.update({(n, l): p for n, p in zip(names, got)})
    sent.update({(n, l): p for n, p in zip(names, sends)})
    update(_MIX_BIG)
    outs = [loss_total, grad_x[None]]
    for kind in ("grad", "delta", "new_m", "new_v"):
        outs += [result[kind][n] for n in _ORDER]
    return tuple(outs)
```

```python
import functools
import math

import numpy as np
import jax
import jax.numpy as jnp
from jax import lax
from jax.experimental import pallas as pl
from jax.experimental.pallas import tpu as pltpu

F32, BF16 = jnp.float32, jnp.bfloat16
SDS = jax.ShapeDtypeStruct

D_MODEL = 1024
N_META = 16
EPS = 1e-6
WINDOW = 128
ROPE_THETA = 10000.0
HEADS = 8
D_FF = 2816
DEPTH = 2
N_DEV = 8
ADAM_LR, ADAM_B1, ADAM_B2, ADAM_EPS, ADAM_WD, ADAM_STEP = 0.001, 0.9, 0.999, 1e-08, 0.01, 10

ROW_ALIGN = 384
TILE_MM = 384
TILE_ROW = 192
TILE_CONV_BWD = 128
TILE_ATT = 384
TILE_POST = 384
PAIRS = 2
VMEM_LIMIT = 56 * 1024 * 1024

GATES_W = 3072
OTHER_W = 2816
IN_W = GATES_W + OTHER_W
O_FQ, O_FK, O_FV, O_SQ, O_SK, O_SV, O_CQ, O_CKV, O_MISC = 0, 512, 1024, 1536, 2048, 2176, 2304, 2560, 2688
FF_LANE = 32

NEG = -1e30


def _dot(a, b):
    return jnp.dot(a, b, preferred_element_type=F32)


def _dot_nt(a, b):
    return lax.dot_general(a, b, (((1,), (1,)), ((), ())), preferred_element_type=F32)


def _dot_tn(a, b):
    return lax.dot_general(a, b, (((0,), (0,)), ((), ())), preferred_element_type=F32)


def _params(sem):
    return pltpu.CompilerParams(dimension_semantics=sem, vmem_limit_bytes=VMEM_LIMIT)


def _rms(x, g):
    return x * lax.rsqrt(jnp.mean(x * x, axis=-1, keepdims=True) + EPS) * g


def _split_dot(x, m, pieces=2):
    acc, rest = None, x
    for _ in range(pieces):
        part = rest.astype(BF16)
        rest = rest - part.astype(F32)
        acc = _dot(part, m) if acc is None else acc + _dot(part, m)
    return acc


@jax.custom_vjp
def _sel(x, m, mt):
    return _split_dot(x, m)


_sel.defvjp(lambda x, m, mt: (_split_dot(x, m), (m, mt)), lambda res, dy: (_split_dot(dy, res[1]), None, None))


@jax.custom_vjp
def _mm(x, w):
    return _dot(x.astype(BF16), w.astype(BF16))


def _mm_bwd(res, dy):
    x, w = res
    dyb = dy.astype(BF16)
    return _dot_nt(dyb, w.astype(BF16)), _dot_tn(x.astype(BF16), dyb)


_mm.defvjp(lambda x, w: (_mm(x, w), (x, w)), _mm_bwd)


def _rot_impl(x):
    w = x.shape[1]
    lane = lax.broadcasted_iota(jnp.int32, x.shape, 1) % 128
    lo = (lane >= 64) & (lane < 80)
    hi = (lane >= 80) & (lane < 96)
    return jnp.where(hi, pltpu.roll(x, 16, 1), 0.0) - jnp.where(lo, pltpu.roll(x, w - 16, 1), 0.0)


@jax.custom_vjp
def _rot(x):
    return _rot_impl(x)


_rot.defvjp(lambda x: (_rot_impl(x), None), lambda _, dy: (-_rot_impl(dy),))


def _gnorm(x, g, e, et, dim):
    inv = lax.rsqrt(_sel(x * x, e, et) * (1.0 / dim) + EPS)
    return x * _sel(inv, et, e) * g


def _indicator(width, period):
    m = np.zeros((width, 128), np.float32)
    m[np.arange(width), np.arange(width) // period] = 1.0
    return m


def _consts():
    e64 = _indicator(512, 64)
    e128 = _indicator(1024, 128)
    sk = np.zeros((128, 1024), np.float32)
    for h in range(HEADS):
        sk[np.arange(32), 128 * h + 64 + np.arange(32)] = 1.0
    dup = np.zeros((128, 256), np.float32)
    for g in range(2):
        for r in range(2):
            dup[64 * g + np.arange(64), 128 * g + 64 * r + np.arange(64)] = 1.0
    mats = [e64, e64.T, e128, e128.T, sk, sk.T, dup, dup.T]
    return [jnp.asarray(m, BF16) for m in mats]


def _fold_matrix(width, period):
    m = np.zeros((width, 128), np.float32)
    m[np.arange(width), np.arange(width) % period] = 1.0
    return jnp.asarray(m, BF16)


def _rope_tables(lp):
    half = 16
    freqs = ROPE_THETA ** (-np.arange(half, dtype=np.float32) / half)
    ang = np.arange(lp, dtype=np.float32)[:, None] * freqs[None, :]
    cos = np.ones((lp, 128), np.float32)
    sin = np.zeros((lp, 128), np.float32)
    cos[:, 64:80] = np.cos(ang)
    cos[:, 80:96] = np.cos(ang)
    sin[:, 64:80] = np.sin(ang)
    sin[:, 80:96] = np.sin(ang)
    return jnp.asarray(cos), jnp.asarray(sin)


def _norm_matmul(h, g, w, tn, name):
    lp, d = h.shape
    n = w.shape[1]
    tb = TILE_MM

    def body(h_ref, g_ref, w_ref, xn_ref, y_ref):
        @pl.when(pl.program_id(1) == 0)
        def _():
            xn_ref[...] = _rms(h_ref[...], g_ref[...]).astype(BF16)

        y_ref[...] = _dot(xn_ref[...], w_ref[...])

    return pl.pallas_call(
        body, name=name, grid=(lp // tb, n // tn),
        in_specs=[pl.BlockSpec((tb, d), lambda i, j: (i, 0)), pl.BlockSpec((1, d), lambda i, j: (0, 0)),
                  pl.BlockSpec((d, tn), lambda i, j: (0, j))],
        out_specs=[pl.BlockSpec((tb, d), lambda i, j: (i, 0)), pl.BlockSpec((tb, tn), lambda i, j: (i, j))],
        out_shape=[SDS((lp, d), BF16), SDS((lp, n), F32)],
        compiler_params=_params(("parallel", "arbitrary")),
    )(h, g, w)


def _matmul_residual(a, w, res, name):
    m, k = a.shape
    n = w.shape[1]
    tb = TILE_MM

    def body(a_ref, w_ref, r_ref, o_ref):
        o_ref[...] = r_ref[...] + _dot(a_ref[...], w_ref[...])

    return pl.pallas_call(
        body, name=name, grid=(m // tb,),
        in_specs=[pl.BlockSpec((tb, k), lambda i: (i, 0)), pl.BlockSpec((k, n), lambda i: (0, 0)),
                  pl.BlockSpec((tb, n), lambda i: (i, 0))],
        out_specs=pl.BlockSpec((tb, n), lambda i: (i, 0)),
        out_shape=SDS((m, n), F32),
        compiler_params=_params(("parallel",)),
    )(a, w, res)


def _matmul_nt(dy, w, tn, name):
    m, k = dy.shape
    n = w.shape[0]
    tb = TILE_MM

    def body(dy_ref, w_ref, o_ref):
        o_ref[...] = _dot_nt(dy_ref[...].astype(BF16), w_ref[...])

    return pl.pallas_call(
        body, name=name, grid=(m // tb, n // tn),
        in_specs=[pl.BlockSpec((tb, k), lambda i, j: (i, 0)), pl.BlockSpec((tn, k), lambda i, j: (j, 0))],
        out_specs=pl.BlockSpec((tb, tn), lambda i, j: (i, j)),
        out_shape=SDS((m, n), F32),
        compiler_params=_params(("parallel", "arbitrary")),
    )(dy, w)


def _matmul_tn(x, dy, tn, name):
    m, k = x.shape
    n = dy.shape[1]
    tb = TILE_MM
    nb = m // tb

    def body(x_ref, dy_ref, o_ref, acc):
        i = pl.program_id(1)

        @pl.when(i == 0)
        def _():
            acc[...] = jnp.zeros_like(acc)

        acc[...] += _dot_tn(x_ref[...].astype(BF16), dy_ref[...].astype(BF16))

        @pl.when(i == nb - 1)
        def _():
            o_ref[...] = acc[...].astype(BF16)

    return pl.pallas_call(
        body, name=name, grid=(n // tn, nb),
        in_specs=[pl.BlockSpec((tb, k), lambda j, i: (i, 0)), pl.BlockSpec((tb, tn), lambda j, i: (i, j))],
        out_specs=pl.BlockSpec((k, tn), lambda j, i: (0, j)),
        out_shape=SDS((k, n), BF16),
        scratch_shapes=[pltpu.VMEM((k, tn), F32)],
        compiler_params=_params(("parallel", "arbitrary")),
    )(x, dy)


def _norm_matmul_bwd(dys, w, x, g, dres, name):
    m, d = x.shape
    tb = TILE_MM
    widths = [a.shape[1] for a in dys]
    n_dy = len(dys)

    def body(*refs):
        w_ref, x_ref, g_ref, r_ref, o_ref, dg_ref = refs[n_dy:]

        @pl.when(pl.program_id(0) == 0)
        def _():
            dg_ref[...] = jnp.zeros_like(dg_ref)

        dxn, off = None, 0
        for dy_ref, width in zip(refs[:n_dy], widths):
            part = _dot_nt(dy_ref[...], w_ref[:, off:off + width])
            dxn = part if dxn is None else dxn + part
            off += width
        _, vjp = jax.vjp(_rms, x_ref[...], g_ref[...])
        dx, dg = vjp(dxn)
        o_ref[...] = r_ref[...] + dx
        dg_ref[...] += dg

    row = pl.BlockSpec((tb, d), lambda i: (i, 0))
    vec = pl.BlockSpec((1, d), lambda i: (0, 0))
    return pl.pallas_call(
        body, name=name, grid=(m // tb,),
        in_specs=[pl.BlockSpec((tb, wd), lambda i: (i, 0)) for wd in widths]
        + [pl.BlockSpec(w.shape, lambda i: (0, 0)), row, vec, row],
        out_specs=[row, vec],
        out_shape=[SDS((m, d), F32), SDS((1, d), F32)],
        compiler_params=_params(("arbitrary",)),
    )(*dys, w, x, g, dres)


def _prep_math(pieces, prm, consts, cos, sin):
    fq, fk, sq, sk, sv, cq, ckv, misc = pieces
    gfq, gfk, gsq, gsk, fb, gqa, gkva, gmq, gmk, wq, wkk, wkv = prm
    e64, e64t, e128, e128t, skm, skt, dup, dupt = consts
    cos8 = jnp.concatenate([cos] * HEADS, axis=1)
    sin8 = jnp.concatenate([sin] * HEADS, axis=1)
    fq_n = _gnorm(fq, gfq, e64, e64t, 64)
    fk_n = _gnorm(fk, gfk, e64, e64t, 64)
    ls = jax.nn.log_sigmoid(misc + fb)
    q = _gnorm(_mm(_rms(cq, gqa), wq), gmq, e128, e128t, 96)
    mq = q * cos8 + _rot(q) * sin8
    kva = _rms(ckv, gkva)
    k = _gnorm(_mm(kva, wkk) + _sel(misc, skm, skt), gmk, e128, e128t, 96)
    mk = k * cos8 + _rot(k) * sin8
    mv = _mm(kva, wkv)
    sq_n = _gnorm(sq, gsq, e64, e64t, 64)
    sk_n = _gnorm(sk, gsk, e64[0:128], e64t[:, 0:128], 64)
    skd = _sel(sk_n, dup, dupt)
    svd = _sel(sv, dup, dupt)
    return fq_n, fk_n, ls, mq, mk, mv, sq_n, skd, svd


_PIECES = [(O_FQ, 512), (O_FK, 512), (O_SQ, 512), (O_SK, 128), (O_SV, 128), (O_CQ, 256), (O_CKV, 128), (O_MISC, 128)]
_PRM_SHAPES = [(1, 512), (1, 512), (1, 512), (1, 128), (1, 128), (1, 256), (1, 128), (1, 1024), (1, 1024),
               (256, 1024), (128, 1024), (128, 512)]
_CONST_SHAPES = [(512, 128), (128, 512), (1024, 128), (128, 1024), (128, 1024), (1024, 128), (128, 256), (256, 128)]


def _piece_specs(tb):
    def spec(off, width):
        blk = (GATES_W + off) // width
        return pl.BlockSpec((tb, width), lambda i, blk=blk: (i, blk))
    return [spec(o, w) for o, w in _PIECES] + [spec(O_FV, 512)]


def _full_specs(shapes):
    return [pl.BlockSpec(s, lambda i: (0, 0)) for s in shapes]


def _prep_fwd(proj, prm, consts, cos, sin, name):
    lp = proj.shape[0]
    tb = TILE_ROW
    row = lambda w: pl.BlockSpec((tb, w), lambda i: (i, 0))

    def body(*refs):
        pieces = [r[...] for r in refs[0:8]]
        fv = refs[8][...]
        prm_v = [r[...] for r in refs[9:21]]
        consts_v = [r[...] for r in refs[21:29]]
        cos_v, sin_v = refs[29][...], refs[30][...]
        outs = refs[31:]
        fq_n, fk_n, ls, mq, mk, mv, sq_n, skd, svd = _prep_math(pieces, prm_v, consts_v, cos_v, sin_v)
        for ref, val in zip(outs, (fq_n, fk_n, fv, mq, mk, mv, sq_n, skd, svd)):
            ref[...] = val.astype(BF16)
        outs[9][...] = ls

    widths = [512, 512, 512, 1024, 1024, 512, 512, 256, 256]
    return pl.pallas_call(
        body, name=name, grid=(lp // tb,),
        in_specs=_piece_specs(tb) + _full_specs(_PRM_SHAPES) + _full_specs(_CONST_SHAPES) + [row(128), row(128)],
        out_specs=[row(w) for w in widths] + [row(128)],
        out_shape=[SDS((lp, w), BF16) for w in widths] + [SDS((lp, 128), F32)],
        compiler_params=_params(("parallel",)),
    )(*([proj] * 9), *prm, *consts, cos, sin)


def _prep_bwd(proj, prm, consts, cos, sin, cots, folds, name):
    lp = proj.shape[0]
    tb = TILE_ROW
    row = lambda w: pl.BlockSpec((tb, w), lambda i: (i, 0))
    fold64, fold128 = folds

    def body(*refs):
        pieces = [r[...] for r in refs[0:8]]
        prm_v = [r[...] for r in refs[9:21]]
        consts_v = [r[...] for r in refs[21:29]]
        cos_v, sin_v = refs[29][...], refs[30][...]
        dfq, dfk, dfv, dmq, dmk, dmv, dsq, dskd, dsvd, dls = [r[...] for r in refs[31:41]]
        f64, f128 = refs[41][...], refs[42][...]
        d_ref = refs[43]
        g_refs = refs[44:]

        @pl.when(pl.program_id(0) == 0)
        def _():
            for r in g_refs:
                r[...] = jnp.zeros_like(r)

        f = lambda pc, pr: _prep_math(pc, pr, consts_v, cos_v, sin_v)
        _, vjp = jax.vjp(f, pieces, prm_v)
        dpc, dprm = vjp((dfq, dfk, dls, dmq, dmk, dmv, dsq, dskd, dsvd))
        d_fq, d_fk, d_sq, d_sk, d_sv, d_cq, d_ckv, d_misc = dpc
        for off, val in ((O_FQ, d_fq), (O_FK, d_fk), (O_FV, dfv), (O_SQ, d_sq), (O_SK, d_sk), (O_SV, d_sv),
                         (O_CQ, d_cq), (O_CKV, d_ckv), (O_MISC, d_misc)):
            d_ref[:, off:off + val.shape[1]] = val.astype(BF16)
        folded = {0: f64, 1: f64, 2: f64, 3: f64[0:128], 7: f128, 8: f128}
        for idx, (ref, val) in enumerate(zip(g_refs, dprm)):
            if idx in folded:
                ref[...] += _split_dot(jnp.broadcast_to(val, (8, val.shape[1])), folded[idx], 3)
            elif val.shape[0] == 1:
                ref[...] += jnp.broadcast_to(val, ref.shape)
            else:
                ref[...] += val

    g_shapes = [(8, 128), (8, 128), (8, 128), (8, 128), (8, 128), (8, 256), (8, 128), (8, 128), (8, 128),
                (256, 1024), (128, 1024), (128, 512)]
    cot_widths = [512, 512, 512, 1024, 1024, 512, 512, 256, 256, 128]
    return pl.pallas_call(
        body, name=name, grid=(lp // tb,),
        in_specs=(_piece_specs(tb) + _full_specs(_PRM_SHAPES) + _full_specs(_CONST_SHAPES) + [row(128), row(128)]
                  + [row(w) for w in cot_widths] + _full_specs([(512, 128), (1024, 128)])),
        out_specs=[row(OTHER_W)] + _full_specs(g_shapes),
        out_shape=[SDS((lp, OTHER_W), BF16)] + [SDS(s, F32) for s in g_shapes],
        compiler_params=_params(("arbitrary",)),
    )(*([proj] * 9), *prm, *consts, cos, sin, *cots, fold64, fold128)


def _cumsum(xs, reverse, name):
    lp = xs[0].shape[0]
    tb = TILE_MM
    nb = lp // tb
    n_in = len(xs)
    idx = (lambda i: (nb - 1 - i, 0)) if reverse else (lambda i: (i, 0))

    def body(*refs):
        o_ref, carry = refs[n_in], refs[n_in + 1]

        @pl.when(pl.program_id(0) == 0)
        def _():
            carry[...] = jnp.zeros_like(carry)

        x = refs[0][...]
        for r in refs[1:n_in]:
            x = x + r[...]
        r_i = lax.broadcasted_iota(jnp.int32, (tb, tb), 0)
        c_i = lax.broadcasted_iota(jnp.int32, (tb, tb), 1)
        tri = ((c_i >= r_i) if reverse else (c_i <= r_i)).astype(BF16)
        acc, rest = None, x
        for _ in range(3):
            part = rest.astype(BF16)
            rest = rest - part.astype(F32)
            acc = _dot(tri, part) if acc is None else acc + _dot(tri, part)
        o_ref[...] = acc + carry[...]
        carry[...] += jnp.sum(x, axis=0, keepdims=True)

    return pl.pallas_call(
        body, name=name, grid=(nb,),
        in_specs=[pl.BlockSpec((tb, 128), idx)] * n_in,
        out_specs=pl.BlockSpec((tb, 128), idx),
        out_shape=SDS((lp, 128), F32),
        scratch_shapes=[pltpu.VMEM((1, 128), F32)],
        compiler_params=_params(("arbitrary",)),
    )(*xs)


class _Att:
    def __init__(self, mode):
        self.mode = mode
        self.wide = mode == "mla"
        self.qw = 256 if self.wide else 128
        self.scale = (96 if mode == "mla" else 64) ** -0.5

    def resident(self, x, lo, scaled):
        if self.wide:
            return x[:, 0:128], x[:, 128:256]
        if scaled:
            x = x * jnp.asarray(self.scale, x.dtype)
        zero = jnp.zeros_like(x)
        return jnp.where(lo, x, zero), jnp.where(lo, zero, x)

    def moving(self, x):
        return (x[:, 0:128], x[:, 128:256]) if self.wide else (x, x)

    def logits(self, a, b, qpos, kpos, key_decay, slope, masked):
        s = _dot_nt(a, b)
        if self.wide:
            s = s * self.scale
        if self.mode == "fox":
            s = s - key_decay
        if self.mode == "swa":
            s = s - slope * (qpos - kpos).astype(F32)
        if masked:
            ok = kpos <= qpos
            if self.mode == "swa":
                ok = ok & ((kpos < N_META) | (qpos - kpos < WINDOW))
            s = jnp.where(ok, s, NEG)
        return s


def _as_rows(col):
    return jnp.broadcast_to(col, (col.shape[0], 128)).T[0:8, :]


def _halves(x, lo):
    zero = jnp.zeros_like(x)
    return jnp.where(lo, x, zero), jnp.where(lo, zero, x)


def _kv_specs(att, lp, rows):
    if att.mode == "swa":
        return (pl.BlockSpec((rows, 128), lambda g, i: (i if rows != lp else 0, g)),) * 2
    return (pl.BlockSpec((rows, PAIRS * att.qw), lambda g, i: (i if rows != lp else 0, g)),
            pl.BlockSpec((rows, PAIRS * 128), lambda g, i: (i if rows != lp else 0, g)))


def _pair_cols(att, x, pp, width):
    return x if x.shape[1] == width else x[:, pp * width:(pp + 1) * width]


def _att_fwd(att, q, k, v, extra, name):
    lp = q.shape[0]
    t = TILE_ATT
    nq = lp // t
    qw = att.qw
    mode = att.mode
    nh = 2 * PAIRS

    def body(*refs):
        q_ref, k_ref, v_ref = refs[0:3]
        o_ref, lse_ref = refs[-2:]
        g, qi = pl.program_id(0), pl.program_id(1)
        lo = lax.broadcasted_iota(jnp.int32, (1, 128), 1) < 64
        q_all = q_ref[...]
        q_heads = [h for pp in range(PAIRS) for h in att.resident(_pair_cols(att, q_all, pp, qw), lo, True)]
        qpos = qi * t + lax.broadcasted_iota(jnp.int32, (t, 1), 0)

        def step(first, cols, carry, masked):
            ks = pl.multiple_of(first, 128)
            kc, vc = k_ref[pl.ds(ks, cols), :], v_ref[pl.ds(ks, cols), :]
            kpos = first + lax.broadcasted_iota(jnp.int32, (1, cols), 1)
            out = []
            for h in range(nh):
                pp = h // 2
                m, l, acc = carry[3 * h:3 * h + 3]
                k_h = att.moving(_pair_cols(att, kc, pp, qw))[h % 2]
                decay = refs[3][h, :, pl.ds(ks, cols)] if mode == "fox" else None
                slope = refs[4][nh * g + h] if mode == "swa" else None
                s = att.logits(q_heads[h], k_h, qpos, kpos, decay, slope, masked)
                m_new = jnp.maximum(m, jnp.max(s, axis=-1, keepdims=True))
                alpha = jnp.exp(m - m_new)
                pe = jnp.exp(s - m_new)
                l = alpha * l + jnp.sum(pe, axis=-1, keepdims=True)
                acc = alpha * acc + _dot(pe.astype(BF16), _pair_cols(att, vc, pp, 128))
                out += [m_new, l, acc]
            return tuple(out)

        init = []
        for h in range(nh):
            if mode == "swa":
                init += [jnp.full((t, 1), refs[3][nh * g + h], F32), jnp.ones((t, 1), F32)]
            else:
                init += [jnp.full((t, 1), NEG, F32), jnp.zeros((t, 1), F32)]
            init.append(jnp.zeros((t, 128), F32))
        if mode == "swa":
            band = jnp.maximum(qi * t - WINDOW, 0)
            carry = lax.fori_loop(0, (band >= 128).astype(jnp.int32), lambda j, c: step(0, 128, c, True), tuple(init))
            carry = step(band, t + WINDOW, carry, True)
        else:
            carry = lax.fori_loop(0, qi // 2, lambda j, c: step(2 * j * t, 2 * t, c, False), tuple(init))
            carry = lax.fori_loop(0, qi % 2, lambda j, c: step((qi - 1) * t, t, c, False), carry)
            carry = step(qi * t, t, carry, True)
        outs = []
        for pp in range(PAIRS):
            (ma, la, acca), (mb, lb, accb) = carry[6 * pp:6 * pp + 3], carry[6 * pp + 3:6 * pp + 6]
            outs.append(jnp.where(lo, acca / la, accb / lb).astype(BF16))
            lse_ref[2 * pp] = ma + jnp.log(la)
            lse_ref[2 * pp + 1] = mb + jnp.log(lb)
        o_ref[...] = jnp.concatenate(outs, axis=1)

    in_specs = [pl.BlockSpec((t, PAIRS * qw), lambda g, i: (i, g)), *_kv_specs(att, lp, lp)]
    if mode == "fox":
        in_specs += [pl.BlockSpec((nh, 1, lp), lambda g, i: (g, 0, 0))]
    if mode == "swa":
        in_specs += [pl.BlockSpec(memory_space=pltpu.SMEM)] * 2
    return pl.pallas_call(
        body, name=name, grid=(4 // PAIRS, nq), in_specs=in_specs,
        out_specs=[pl.BlockSpec((t, PAIRS * 128), lambda g, i: (i, g)), pl.BlockSpec((nh, t, 1), lambda g, i: (g, i, 0))],
        out_shape=[SDS((lp, 512), BF16), SDS((HEADS, lp, 1), F32)],
        compiler_params=_params(("parallel", "arbitrary")),
    )(q, k, v, *extra)


def _att_bwd(att, q, k, v, o, do, lse, extra, name):
    lp = q.shape[0]
    t = TILE_ATT
    nq = lp // t
    qw = att.qw
    mode = att.mode
    nh = 2 * PAIRS
    kw = 128 if mode == "swa" else PAIRS * qw
    vw = 128 if mode == "swa" else PAIRS * 128

    def body(*refs):
        q_ref, k_ref, v_ref, o_ref, do_ref, lse_ref = refs[0:6]
        n_out = {"fox": 5, "mla": 3, "swa": 4}[mode]
        outs = refs[len(refs) - n_out:]
        dq_ref, dk_ref, dv_ref = outs[0:3]
        g, qi = pl.program_id(0), pl.program_id(1)

        @pl.when(qi == 0)
        def _():
            dk_ref[...] = jnp.zeros_like(dk_ref)
            dv_ref[...] = jnp.zeros_like(dv_ref)
            if mode == "fox":
                outs[4][...] = jnp.zeros_like(outs[4])

        lo = lax.broadcasted_iota(jnp.int32, (1, 128), 1) < 64
        q_all, do_all = q_ref[...], do_ref[...]
        prod = do_all.astype(F32) * o_ref[...].astype(F32)
        q_heads, q_plain, do_heads, do_pairs, delta = [], [], [], [], []
        for pp in range(PAIRS):
            q_pp = _pair_cols(att, q_all, pp, qw)
            q_heads += att.resident(q_pp, lo, True)
            q_plain += att.moving(q_pp)
            do_pp = _pair_cols(att, do_all, pp, 128)
            do_pairs.append(do_pp)
            do_heads += _halves(do_pp, lo)
            pr_pp = _pair_cols(att, prod, pp, 128)
            delta += [jnp.sum(jnp.where(lo, pr_pp, 0.0), axis=-1, keepdims=True),
                      jnp.sum(jnp.where(lo, 0.0, pr_pp), axis=-1, keepdims=True)]
        lse_v = [lse_ref[h] for h in range(nh)]
        qpos = qi * t + lax.broadcasted_iota(jnp.int32, (t, 1), 0)

        def step(first, cols, carry, masked):
            ks = pl.multiple_of(first, 128)
            kc, vc = k_ref[pl.ds(ks, cols), :], v_ref[pl.ds(ks, cols), :]
            kpos = first + lax.broadcasted_iota(jnp.int32, (1, cols), 1)
            out, dk_parts, dv_parts = [], [], []
            for h in range(nh):
                pp = h // 2
                k_h = att.moving(_pair_cols(att, kc, pp, qw))[h % 2]
                decay = refs[6][h, :, pl.ds(ks, cols)] if mode == "fox" else None
                slope = refs[7][nh * g + h] if mode == "swa" else None
                s = att.logits(q_heads[h], k_h, qpos, kpos, decay, slope, masked)
                pr = jnp.exp(s - lse_v[h])
                ds = pr * (_dot_nt(do_heads[h], _pair_cols(att, vc, pp, 128)) - delta[h])
                dsb = ds.astype(BF16)
                out.append(carry[2 * h] + _dot(dsb, k_h))
                out.append(carry[2 * h + 1] + jnp.sum(ds, axis=-1, keepdims=True) if mode == "fox" else carry[2 * h + 1])
                dk_parts.append(_dot_tn(dsb, q_plain[h]))
                dv_parts.append(_dot_tn(pr.astype(BF16), do_pairs[pp]))
                if mode == "fox":
                    outs[4][h, 0:1, pl.ds(ks, cols)] -= jnp.sum(ds, axis=0, keepdims=True)
            rows = pl.ds(ks, cols)
            for pp in range(PAIRS):
                dv_pp = jnp.where(lo, dv_parts[2 * pp], dv_parts[2 * pp + 1])
                if att.wide:
                    dk_pp = jnp.concatenate(dk_parts[2 * pp:2 * pp + 2], axis=1) * att.scale
                else:
                    dk_pp = jnp.where(lo, dk_parts[2 * pp], dk_parts[2 * pp + 1]) * att.scale
                if mode == "swa":
                    dk_ref[rows, :] += dk_pp
                    dv_ref[rows, :] += dv_pp
                else:
                    dk_ref[rows, pp * qw:(pp + 1) * qw] += dk_pp
                    dv_ref[rows, pp * 128:(pp + 1) * 128] += dv_pp
            return tuple(out)

        init = (jnp.zeros((t, 128), F32), jnp.zeros((t, 1), F32)) * nh
        if mode == "swa":
            band = jnp.maximum(qi * t - WINDOW, 0)
            carry = lax.fori_loop(0, (band >= 128).astype(jnp.int32), lambda j, c: step(0, 128, c, True), init)
            carry = step(band, t + WINDOW, carry, True)
        else:
            carry = lax.fori_loop(0, qi // 2, lambda j, c: step(2 * j * t, 2 * t, c, False), init)
            carry = lax.fori_loop(0, qi % 2, lambda j, c: step((qi - 1) * t, t, c, False), carry)
            carry = step(qi * t, t, carry, True)
        dq = []
        for pp in range(PAIRS):
            dqa, dca, dqb, dcb = carry[4 * pp:4 * pp + 4]
            dq += [dqa, dqb] if att.wide else [jnp.where(lo, dqa, dqb)]
            if mode == "fox":
                outs[3][2 * pp] = _as_rows(dca)
                outs[3][2 * pp + 1] = _as_rows(dcb)
        dq_ref[...] = jnp.concatenate(dq, axis=1) * att.scale
        if mode == "swa":
            ds_ref = outs[3]

            @pl.when(qi == 0)
            def _():
                ds_ref[...] = jnp.zeros_like(ds_ref)

            lane = lax.broadcasted_iota(jnp.int32, (8, 128), 1)
            acc = jnp.zeros((8, 128), F32)
            for h in range(nh):
                tot = -jnp.sum(jnp.exp(refs[6][nh * g + h] - lse_v[h]) * delta[h])
                acc = acc + jnp.where(lane == h, tot, 0.0)
            ds_ref[0] += acc

    col = pl.BlockSpec((nh, t, 1), lambda g, i: (g, i, 0))
    in_specs = [pl.BlockSpec((t, PAIRS * qw), lambda g, i: (i, g)), *_kv_specs(att, lp, lp),
                pl.BlockSpec((t, PAIRS * 128), lambda g, i: (i, g)), pl.BlockSpec((t, PAIRS * 128), lambda g, i: (i, g)), col]
    out_specs = [pl.BlockSpec((t, PAIRS * qw), lambda g, i: (i, g)), pl.BlockSpec((lp, kw), lambda g, i: (0, g)),
                 pl.BlockSpec((lp, vw), lambda g, i: (0, g))]
    n_groups = 4 // PAIRS
    out_shape = [SDS((lp, 4 * qw), F32), SDS((lp, n_groups * kw), F32), SDS((lp, n_groups * vw), F32)]
    if mode == "fox":
        in_specs += [pl.BlockSpec((nh, 1, lp), lambda g, i: (g, 0, 0))]
        out_specs += [pl.BlockSpec((nh, 8, t), lambda g, i: (g, 0, i)), pl.BlockSpec((nh, 8, lp), lambda g, i: (g, 0, 0))]
        out_shape += [SDS((HEADS, 8, lp), F32)] * 2
    if mode == "swa":
        in_specs += [pl.BlockSpec(memory_space=pltpu.SMEM)] * 2
        out_specs.append(pl.BlockSpec((1, 8, 128), lambda g, i: (g, 0, 0)))
        out_shape.append(SDS((n_groups, 8, 128), F32))
    return pl.pallas_call(
        body, name=name, grid=(n_groups, nq), in_specs=in_specs, out_specs=out_specs, out_shape=out_shape,
        compiler_params=_params(("parallel", "arbitrary")),
    )(q, k, v, o, do, lse, *extra)


def _post_fwd(h, proj, outs, wb, wo, name):
    lp, d = h.shape
    tb = TILE_POST
    row = lambda w: pl.BlockSpec((tb, w), lambda i: (i, 0))

    def body(h_ref, g0, g1, g2, oa, ob, oc, wb_ref, wo_ref, o_ref):
        merged = jnp.zeros((tb, d), F32)
        for n, (g_ref, br) in enumerate(((g0, oa), (g1, ob), (g2, oc))):
            merged = merged + jax.nn.sigmoid(g_ref[...]) * _dot(br[...], wb_ref[n])
        o_ref[...] = h_ref[...] + _dot(merged.astype(BF16), wo_ref[...])

    gate = lambda n: pl.BlockSpec((tb, d), lambda i, n=n: (i, n))
    return pl.pallas_call(
        body, name=name, grid=(lp // tb,),
        in_specs=[row(d), gate(0), gate(1), gate(2), row(512), row(512), row(512),
                  pl.BlockSpec((3, 512, d), lambda i: (0, 0, 0)), pl.BlockSpec((d, d), lambda i: (0, 0))],
        out_specs=row(d), out_shape=SDS((lp, d), F32),
        compiler_params=_params(("parallel",)),
    )(h, proj, proj, proj, *outs, wb, wo)


def _post_bwd(dh, proj, outs, wb, wo, name):
    lp, d = dh.shape
    tb = TILE_POST
    row = lambda w: pl.BlockSpec((tb, w), lambda i: (i, 0))

    def body(dh_ref, g0, g1, g2, oa, ob, oc, wb_ref, wo_ref, dg_ref, doa, dob, doc, dwb_ref, dwo_ref):
        @pl.when(pl.program_id(0) == 0)
        def _():
            dwb_ref[...] = jnp.zeros_like(dwb_ref)
            dwo_ref[...] = jnp.zeros_like(dwo_ref)

        dhb = dh_ref[...].astype(BF16)
        dm = _dot_nt(dhb, wo_ref[...])
        merged = jnp.zeros((tb, d), F32)
        for n, (g_ref, br, do_ref) in enumerate(((g0, oa, doa), (g1, ob, dob), (g2, oc, doc))):
            gate = jax.nn.sigmoid(g_ref[...])
            o_n = br[...]
            y = _dot(o_n, wb_ref[n])
            merged = merged + gate * y
            dy = (dm * gate).astype(BF16)
            dg_ref[:, n * d:(n + 1) * d] = (dm * y * gate * (1.0 - gate)).astype(BF16)
            do_ref[...] = _dot_nt(dy, wb_ref[n]).astype(BF16)
            dwb_ref[n] += _dot_tn(o_n, dy)
        dwo_ref[...] += _dot_tn(merged.astype(BF16), dhb)

    gate = lambda n: pl.BlockSpec((tb, d), lambda i, n=n: (i, n))
    wb_spec = pl.BlockSpec((3, 512, d), lambda i: (0, 0, 0))
    wo_spec = pl.BlockSpec((d, d), lambda i: (0, 0))
    return pl.pallas_call(
        body, name=name, grid=(lp // tb,),
        in_specs=[row(d), gate(0), gate(1), gate(2), row(512), row(512), row(512), wb_spec, wo_spec],
        out_specs=[row(GATES_W), row(512), row(512), row(512), wb_spec, wo_spec],
        out_shape=[SDS((lp, GATES_W), BF16)] + [SDS((lp, 512), BF16)] * 3 + [SDS((3, 512, d), F32), SDS((d, d), F32)],
        compiler_params=_params(("arbitrary",)),
    )(dh, proj, proj, proj, *outs, wb, wo)


def _shift_down(x, halo, n, first):
    rows = lax.broadcasted_iota(jnp.int32, x.shape, 0)
    edge = jnp.concatenate([pltpu.roll(halo, n, 0), jnp.zeros((x.shape[0] - 8, x.shape[1]), F32)], axis=0)
    edge = jnp.where(first, 0.0, edge)
    return jnp.where(rows < n, edge, pltpu.roll(x, n, 0))


def _shift_up(x, halo, n, last):
    tb = x.shape[0]
    rows = lax.broadcasted_iota(jnp.int32, x.shape, 0)
    edge = jnp.concatenate([jnp.zeros((tb - 8, x.shape[1]), F32), pltpu.roll(halo, 8 - n, 0)], axis=0)
    edge = jnp.where(last, 0.0, edge)
    return jnp.where(rows >= tb - n, edge, pltpu.roll(x, tb - n, 0))


def _conv(u, halo, w_ref, b_ref, first):
    taps = (_shift_down(u, halo, 2, first), _shift_down(u, halo, 1, first), u)
    c = b_ref[...] + w_ref[0:1, :] * taps[0] + w_ref[1:2, :] * taps[1] + w_ref[2:3, :] * taps[2]
    return c, taps


def _ffn_specs(tb, f):
    hb = tb // 8
    cur = lambda c: pl.BlockSpec((tb, f), lambda i, c=c: (i, c))
    prev = lambda c: pl.BlockSpec((8, f), lambda i, c=c: (jnp.maximum(i * hb - 1, 0), c))
    vec = lambda r, c: pl.BlockSpec((r, f), lambda i, c=c: (0, c))
    return cur, prev, vec


def _ffn_act_fwd(u, cw, cb, name):
    lp = u.shape[0]
    f = D_FF
    tb = TILE_ROW
    cur, prev, vec = _ffn_specs(tb, f)

    def body(ug, uv, hg, hv, wg, wv, bg, bv, o_ref):
        first = pl.program_id(0) == 0
        cg, _ = _conv(ug[...], hg[...], wg, bg, first)
        cv, _ = _conv(uv[...], hv[...], wv, bv, first)
        o_ref[...] = (cg * jax.nn.sigmoid(cg) * cv).astype(BF16)

    return pl.pallas_call(
        body, name=name, grid=(lp // tb,),
        in_specs=[cur(0), cur(1), prev(0), prev(1), vec(8, 0), vec(8, 1), vec(1, 0), vec(1, 1)],
        out_specs=pl.BlockSpec((tb, f), lambda i: (i, 0)), out_shape=SDS((lp, f), BF16),
        compiler_params=_params(("parallel",)),
    )(u, u, u, u, cw, cw, cb, cb)


def _ffn_act_bwd_conv(u, dact, cw, cb, name):
    lp = u.shape[0]
    f = D_FF
    tb = TILE_CONV_BWD
    cur, prev, vec = _ffn_specs(tb, f)

    def body(ug, uv, hg, hv, wg, wv, bg, bv, da_ref, dc_ref, dw_ref, db_ref):
        first = pl.program_id(0) == 0

        @pl.when(first)
        def _():
            dw_ref[...] = jnp.zeros_like(dw_ref)
            db_ref[...] = jnp.zeros_like(db_ref)

        cg, tg = _conv(ug[...], hg[...], wg, bg, first)
        cv, tv = _conv(uv[...], hv[...], wv, bv, first)
        da = da_ref[...]
        sg = jax.nn.sigmoid(cg)
        dcg = da * cv * sg * (1.0 + cg * (1.0 - sg))
        dcv = da * cg * sg
        for c, (dc, taps) in enumerate(((dcg, tg), (dcv, tv))):
            dc_ref[:, c * f:(c + 1) * f] = dc
            for n in range(3):
                dw_ref[n:n + 1, c * f:(c + 1) * f] += jnp.sum(dc * taps[n], axis=0, keepdims=True)
            db_ref[0:1, c * f:(c + 1) * f] += jnp.sum(dc, axis=0, keepdims=True)

    acc = pl.BlockSpec((8, 2 * f), lambda i: (0, 0))
    return pl.pallas_call(
        body, name=name, grid=(lp // tb,),
        in_specs=[cur(0), cur(1), prev(0), prev(1), vec(8, 0), vec(8, 1), vec(1, 0), vec(1, 1),
                  pl.BlockSpec((tb, f), lambda i: (i, 0))],
        out_specs=[pl.BlockSpec((tb, 2 * f), lambda i: (i, 0)), acc, acc],
        out_shape=[SDS((lp, 2 * f), F32), SDS((8, 2 * f), F32), SDS((8, 2 * f), F32)],
        compiler_params=_params(("arbitrary",)),
    )(u, u, u, u, cw, cw, cb, cb, dact)


def _ffn_act_bwd_in(dc, cw, name):
    lp = dc.shape[0]
    f2 = 2 * D_FF
    tb = TILE_ROW
    nb = lp // tb
    hb = tb // 8

    def body(dc_ref, n_ref, w_ref, o_ref):
        last = pl.program_id(0) == nb - 1
        dcv, halo = dc_ref[...], n_ref[...]
        du = (w_ref[2:3, :] * dcv + w_ref[1:2, :] * _shift_up(dcv, halo, 1, last)
              + w_ref[0:1, :] * _shift_up(dcv, halo, 2, last))
        o_ref[...] = du.astype(BF16)

    cur = pl.BlockSpec((tb, f2), lambda i: (i, 0))
    return pl.pallas_call(
        body, name=name, grid=(nb,),
        in_specs=[cur, pl.BlockSpec((8, f2), lambda i: (jnp.minimum((i + 1) * hb, nb * hb - 1), 0)),
                  pl.BlockSpec((8, f2), lambda i: (0, 0))],
        out_specs=cur, out_shape=SDS((lp, f2), BF16),
        compiler_params=_params(("parallel",)),
    )(dc, dc, cw)


def _loss_head(y, target, n_real, name):
    lp, d = y.shape
    tb = TILE_MM

    def body(y_ref, t_ref, dy_ref, loss_ref):
        i = pl.program_id(0)

        @pl.when(i == 0)
        def _():
            loss_ref[...] = jnp.zeros_like(loss_ref)

        rows = i * tb + lax.broadcasted_iota(jnp.int32, (tb, 1), 0)
        real = (rows >= N_META) & (rows < N_META + n_real)
        diff = jnp.where(real, y_ref[...] - t_ref[...], 0.0)
        dy_ref[...] = diff * (1.0 / d)
        loss_ref[...] += (0.5 / d) * jnp.sum(diff * diff).reshape(1, 1)

    row = pl.BlockSpec((tb, d), lambda i: (i, 0))
    return pl.pallas_call(
        body, name=name, grid=(lp // tb,), in_specs=[row, row],
        out_specs=[row, pl.BlockSpec((1, 1), lambda i: (0, 0))],
        out_shape=[SDS((lp, d), F32), SDS((1, 1), F32)],
        compiler_params=_params(("arbitrary",)),
    )(y, target)


def _pad_lanes(v, width, at=0):
    return jnp.pad(v.astype(F32), (at, width - at - v.shape[0]))[None, :]


def _mix_params(w, big, l):
    b = lambda a: a.astype(BF16)
    win = big["w_in"]
    fq, fk, fv, ff, cq, ckv, kr, sq, sk, sv, gates = jnp.split(
        win, [512, 1024, 1536, 1544, 1800, 1928, 1960, 2472, 2600, 2728], axis=1)
    misc = jnp.concatenate([kr, ff, jnp.zeros((D_MODEL, 88), win.dtype)], axis=1)
    w_in = b(jnp.concatenate([gates, fq, fk, fv, sq, sk, sv, cq, ckv, misc], axis=1))
    wq = jnp.pad(big["mla_w_q_up"].reshape(256, HEADS, 96), ((0, 0), (0, 0), (0, 32))).reshape(256, 1024)
    wkv = big["mla_w_kv_up"].reshape(128, HEADS, 128)
    wkk = jnp.pad(wkv[:, :, :64], ((0, 0), (0, 0), (0, 64))).reshape(128, 1024)
    wkvv = wkv[:, :, 64:].reshape(128, 512)
    tile = lambda g, n: jnp.tile(g.astype(F32), n)[None, :]
    prm = [tile(w["fox_q_g"][l], 8), tile(w["fox_k_g"][l], 8), tile(w["swa_q_g"][l], 8), tile(w["swa_k_g"][l], 2),
           _pad_lanes(w["fox_forget_b"][l], 128, FF_LANE), w["mla_q_a_g"][l][None, :], w["mla_kv_a_g"][l][None, :],
           tile(jnp.pad(w["mla_q_g"][l], (0, 32)), 8), tile(jnp.pad(w["mla_k_g"][l], (0, 32)), 8),
           wq.astype(F32), wkk.astype(F32), wkvv.astype(F32)]
    return dict(g1=w["norm1_g"][l][None, :], w_in=w_in, prm=prm, sinks=w["swa_sinks"][l].astype(F32),
                wb=b(big["w_branch"]), wo=b(big["w_o"]))


def _ffn_params(w, big, l):
    cw = jnp.pad(w["ffn_conv_w"][l].astype(F32), ((0, 5), (0, 0)))
    return dict(g2=w["norm2_g"][l][None, :], w_up=big["ffn_w_up"].astype(BF16), cw=cw,
                cb=w["ffn_conv_b"][l][None, :].astype(F32), w_down=big["ffn_w_down"].astype(BF16))


def _decay_rows(c):
    return c[:, FF_LANE:FF_LANE + HEADS].T[:, None, :]


def _from_rows(row):
    return jnp.pad(row[:, 0, :].T, ((0, 0), (FF_LANE, 128 - FF_LANE - HEADS)))


def _layer_fwd_mix(h, lw, consts, cos, sin, slopes, l):
    tag = f"l{l}_"
    xn, proj = _norm_matmul(h, lw["g1"], lw["w_in"], IN_W, tag + "in_proj")
    fq, fk, fv, mq, mk, mv, sq, skd, svd, ls = _prep_fwd(proj, lw["prm"], consts, cos, sin, tag + "prep")
    c = _cumsum([ls], False, tag + "decay_cumsum")
    c_row = _decay_rows(c)
    oa, lse_a = _att_fwd(_Att("fox"), fq, fk, fv, (c_row,), tag + "fox_fwd")
    ob, lse_b = _att_fwd(_Att("mla"), mq, mk, mv, (), tag + "mla_fwd")
    oc, lse_c = _att_fwd(_Att("swa"), sq, skd, svd, (lw["sinks"], slopes), tag + "swa_fwd")
    h2 = _post_fwd(h, proj, (oa, ob, oc), lw["wb"], lw["wo"], tag + "merge")
    saved = dict(h=h, xn=xn, proj=proj, q=(fq, mq, sq), k=(fk, mk, skd), v=(fv, mv, svd), c=c_row,
                 o=(oa, ob, oc), lse=(lse_a, lse_b, lse_c), h2=h2)
    return h2, saved


def _layer_fwd_ffn(h2, lw, l):
    tag = f"l{l}_"
    xn2, u = _norm_matmul(h2, lw["g2"], lw["w_up"], 2 * D_FF, tag + "ffn_up")
    act = _ffn_act_fwd(u, lw["cw"], lw["cb"], tag + "ffn_act")
    h3 = _matmul_residual(act, lw["w_down"], h2, tag + "ffn_down")
    return h3, dict(xn2=xn2, u=u, act=act)


def _layer_bwd_ffn(dh3, lw, sv, l):
    tag = f"l{l}_"
    f = D_FF
    dact = _matmul_nt(dh3, lw["w_down"], f, tag + "ffn_down_dx")
    dw_down = _matmul_tn(sv["act"], dh3, D_MODEL, tag + "ffn_down_dw")
    dc, dcw, dcb = _ffn_act_bwd_conv(sv["u"], dact, lw["cw"], lw["cb"], tag + "ffn_act_dc")
    du = _ffn_act_bwd_in(dc, lw["cw"], tag + "ffn_act_du")
    dw_up = _matmul_tn(sv["xn2"], du, f, tag + "ffn_up_dw")
    dh2, dg2 = _norm_matmul_bwd([du], lw["w_up"], sv["h2"], lw["g2"], dh3, tag + "ffn_up_dx")
    g = dict(norm2_g=dg2[0], ffn_w_up=dw_up, ffn_conv_w=dcw[0:3], ffn_conv_b=dcb[0], ffn_w_down=dw_down)
    return dh2, g


def _layer_bwd_mix(dh2, lw, sv, consts, folds, cos, sin, slopes, l, hook=None, merge_hook=None):
    tag = f"l{l}_"
    dgates, doa, dob, doc, dwb, dwo = _post_bwd(dh2, sv["proj"], sv["o"], lw["wb"], lw["wo"], tag + "merge_bwd")
    c_row = sv["c"]
    tick = merge_hook({"w_branch": dwb, "w_o": dwo}) if merge_hook else None
    if tick is not None:
        c_row = c_row + tick
    extras = ((c_row,), (), (lw["sinks"], slopes))
    grads = []
    for n, (mode, do) in enumerate((("fox", doa), ("mla", dob), ("swa", doc))):
        res = _att_bwd(_Att(mode), sv["q"][n], sv["k"][n], sv["v"][n], sv["o"][n], do, sv["lse"][n], extras[n],
                       tag + mode + "_bwd")
        grads.append((res[0], res[1], res[2], res[3:]))
    (dfq, dfk, dfv, (dcq, dck)), (dmq, dmk, dmv, _), (dsq, dskd, dsvd, (dsink,)) = grads
    dls = _cumsum([_from_rows(dcq), _from_rows(dck)], True, tag + "decay_cumsum_bwd")
    res = _prep_bwd(sv["proj"], lw["prm"], consts, cos, sin,
                    (dfq, dfk, dfv, dmq, dmk, dmv, dsq, dskd, dsvd, dls), folds, tag + "prep_bwd")
    dother, pg = res[0], res[1:]
    dw_g = _matmul_tn(sv["xn"], dgates, GATES_W, tag + "in_proj_dw_gates")
    dw_o = _matmul_tn(sv["xn"], dother, OTHER_W, tag + "in_proj_dw_other")
    d_in = jnp.concatenate([
        dw_o[:, O_FQ:O_FV + 512], dw_o[:, O_MISC + FF_LANE:O_MISC + FF_LANE + 8], dw_o[:, O_CQ:O_CQ + 256],
        dw_o[:, O_CKV:O_CKV + 128], dw_o[:, O_MISC:O_MISC + 32], dw_o[:, O_SQ:O_SQ + 512], dw_o[:, O_SK:O_SK + 128],
        dw_o[:, O_SV:O_SV + 128], dw_g], axis=1)
    d_wq = pg[9].reshape(256, HEADS, 128)[:, :, :96].reshape(256, 768)
    d_wkv = jnp.concatenate([pg[10].reshape(128, HEADS, 128)[:, :, :64], pg[11].reshape(128, HEADS, 64)],
                            axis=2).reshape(128, 1024)
    g = dict(
        w_in=d_in, fox_forget_b=pg[4][0, FF_LANE:FF_LANE + 8], fox_q_g=pg[0][0, :64],
        fox_k_g=pg[1][0, :64], mla_q_a_g=pg[5][0], mla_w_q_up=d_wq, mla_kv_a_g=pg[6][0], mla_w_kv_up=d_wkv,
        mla_q_g=pg[7][0, :96], mla_k_g=pg[8][0, :96], swa_q_g=pg[2][0, :64], swa_k_g=pg[3][0, :64],
        swa_sinks=dsink[:, 0, 0:2 * PAIRS].reshape(HEADS), w_branch=dwb, w_o=dwo)
    tick = hook(g) if hook else None
    g1 = lw["g1"] if tick is None else lw["g1"] + tick
    dh, dg1 = _norm_matmul_bwd([dgates, dother], lw["w_in"], sv["h"], g1, dh2, tag + "in_proj_dx")
    g["norm1_g"] = dg1[0]
    return dh, g


_MIX_BIG = ("w_in", "mla_w_q_up", "mla_w_kv_up", "w_branch", "w_o")
_FFN_BIG = ("ffn_w_up", "ffn_w_down")


def _local_step(x, target, w, hook=None, fetch=None):
    if fetch is None:
        fetch = lambda l, stage, after: {n: w[n][l] for n in (_MIX_BIG if stage == "mix" else _FFN_BIG)}
    seq = x.shape[0]
    length = N_META + seq
    lp = -(-length // ROW_ALIGN) * ROW_ALIGN
    pad = lp - length
    h = jnp.concatenate([w["meta_tokens"].astype(F32), x, jnp.zeros((pad, D_MODEL), F32)], axis=0)
    tgt = jnp.pad(target, ((N_META, pad), (0, 0)))
    consts = _consts()
    folds = (_fold_matrix(512, 64), _fold_matrix(1024, 128))
    cos, sin = _rope_tables(lp)
    slopes = jnp.asarray(2.0 ** (-8.0 * np.arange(1, HEADS + 1, dtype=np.float32) / HEADS), F32)
    lws, saved = [], []
    for l in range(DEPTH):
        lw = _mix_params(w, fetch(l, "mix", h), l)
        h, sv = _layer_fwd_mix(h, lw, consts, cos, sin, slopes, l)
        lw.update(_ffn_params(w, fetch(l, "ffn", h), l))
        h, sv_ffn = _layer_fwd_ffn(h, lw, l)
        lws.append(lw)
        saved.append({**sv, **sv_ffn})
    dh, loss = _loss_head(h, tgt, seq, "loss_head")
    grads = [None] * DEPTH
    for l in reversed(range(DEPTH)):
        dh, g_ffn = _layer_bwd_ffn(dh, lws[l], saved[l], l)
        tick = hook(l, "ffn", g_ffn) if hook else None
        if tick is not None:
            lws[l]["sinks"] = lws[l]["sinks"] + tick
        mix_hook = (lambda g, l=l, g_ffn=g_ffn: hook(l, "mix", {**g_ffn, **g})) if hook else None
        merge_hook = (lambda g, l=l: hook(l, "merge", g)) if hook else None
        dh, g_mix = _layer_bwd_mix(dh, lws[l], saved[l], consts, folds, cos, sin, slopes, l, mix_hook, merge_hook)
        grads[l] = {**g_ffn, **g_mix}
    return loss, dh[N_META:length], dh[:N_META], grads


def _place():
    return lax.axis_index("x"), lax.axis_index("y"), lax.axis_index("c")


def _flip(pos, k):
    x, y, c = pos
    return (1 - x if k & 4 else x, 1 - y if k & 2 else y, 1 - c if k & 1 else c)


def _index(pos):
    return 4 * pos[0] + 2 * pos[1] + pos[2]


def _gather(tensors, name):
    n_t = len(tensors)

    def body(*refs):
        ins, outs = refs[:n_t], refs[n_t:2 * n_t]
        send_sems, recv_sems, local_sems = refs[2 * n_t:]
        x, y, c = _place()
        me, sibling = (x, y, c), (x, y, 1 - c)
        chips = [(1 - x, y), (x, 1 - y), (1 - x, 1 - y)]

        def copy(t, k, block, to, src=None):
            dst = outs[t].at[_index(block)]
            return pltpu.make_async_remote_copy(
                src_ref=dst if src is None else src, dst_ref=dst, send_sem=send_sems.at[t, k],
                recv_sem=recv_sems.at[t, k], device_id=to, device_id_type=pl.DeviceIdType.MESH)

        local, sent = [], []
        for t in range(n_t):
            local.append(pltpu.make_async_copy(ins[t], outs[t].at[_index(me)], local_sems.at[t]))
            local[-1].start()
            sent.append(copy(t, 0, me, sibling, src=ins[t]))
            sent += [copy(t, 1 + j, me, (*chip, c), src=ins[t]) for j, chip in enumerate(chips)]
        for cp in sent:
            cp.start()
        for j, chip in enumerate(chips):
            for t in range(n_t):
                copy(t, 1 + j, (*chip, c), me).wait_recv()
                sent.append(copy(t, 4 + j, (*chip, c), sibling))
                sent[-1].start()
        for t in range(n_t):
            copy(t, 0, sibling, me).wait_recv()
            for j, chip in enumerate(chips):
                copy(t, 4 + j, (*chip, 1 - c), me).wait_recv()
        for cp in sent:
            cp.wait_send()
        for cp in local:
            cp.wait()

    any_spec = pl.BlockSpec(memory_space=pl.ANY)
    return pl.pallas_call(
        body, name=name, in_specs=[any_spec] * n_t, out_specs=[any_spec] * n_t,
        out_shape=[SDS((N_DEV,) + a.shape, a.dtype) for a in tensors],
        scratch_shapes=[pltpu.SemaphoreType.DMA((n_t, N_DEV - 1)), pltpu.SemaphoreType.DMA((n_t, N_DEV - 1)),
                        pltpu.SemaphoreType.DMA((n_t,))],
    )(*tensors)


def _exchange_start(tensors, name, gather=False, after=None):
    n_t = len(tensors)

    def body(*refs):
        ins, lands = refs[:n_t], refs[n_t:2 * n_t]
        send_sem, recv_sem = refs[2 * n_t + 1:2 * n_t + 3]
        token = refs[-1]
        me = _place()
        mine = _index(me)
        for t in range(n_t):
            for k in range(1, N_DEV):
                peer = _flip(me, k)
                pltpu.make_async_remote_copy(
                    src_ref=ins[t] if gather else ins[t].at[_index(peer)], dst_ref=lands[t].at[mine],
                    send_sem=send_sem, recv_sem=recv_sem, device_id=peer, device_id_type=pl.DeviceIdType.MESH).start()
        token[...] = jnp.zeros_like(token)

    hbm = pl.BlockSpec(memory_space=pltpu.HBM)
    sem = pl.BlockSpec(memory_space=pltpu.SEMAPHORE)
    one = pltpu.SemaphoreType.DMA(())
    land_shape = lambda a: ((N_DEV,) + a.shape) if gather else a.shape
    bufs = ([pltpu.HBM(a.shape, a.dtype) for a in tensors] + [pltpu.HBM(land_shape(a), a.dtype) for a in tensors])
    after = jnp.zeros((8, 128), F32) if after is None else after
    outs = pl.pallas_call(
        body, name=name, in_specs=[hbm] * (2 * n_t) + [pl.BlockSpec(memory_space=pl.ANY)],
        out_specs=[sem, sem] + [hbm] * (2 * n_t) + [pl.BlockSpec(memory_space=pltpu.VMEM)],
        out_shape=[one, one] + bufs + [SDS((8, 128), F32)],
        input_output_aliases={i: 2 + i for i in range(2 * n_t)},
        compiler_params=pltpu.CompilerParams(has_side_effects=pltpu.SideEffectType.DATAFLOW_SIDE_EFFECTING),
    )(*[pltpu.with_memory_space_constraint(a, pltpu.HBM) for a in tensors],
      *[pltpu.with_memory_space_constraint(lax.empty(land_shape(a), a.dtype), pltpu.HBM) for a in tensors], after)
    return outs[:-1], outs[-1][0, 0]


def _exchange_wait(state, after, name, gather=False):
    n_t = (len(state) - 2) // 2

    def body(*refs):
        send_sem, recv_sem = refs[0:2]
        ins, lands = refs[2:2 + n_t], refs[2 + n_t:2 + 2 * n_t]
        me = _place()
        for t in range(n_t):
            for k in range(1, N_DEV):
                peer = _flip(me, k)
                copy = pltpu.make_async_remote_copy(
                    src_ref=ins[t] if gather else ins[t].at[_index(peer)], dst_ref=lands[t].at[_index(peer)],
                    send_sem=send_sem, recv_sem=recv_sem, device_id=peer, device_id_type=pl.DeviceIdType.MESH)
                copy.wait_send()
                copy.wait_recv()

    hbm = pl.BlockSpec(memory_space=pltpu.HBM)
    sem = pl.BlockSpec(memory_space=pltpu.SEMAPHORE)
    bufs = [pltpu.HBM(a.shape, a.dtype) for a in state[2:]]
    outs = pl.pallas_call(
        body, name=name, in_specs=[sem, sem] + [hbm] * (2 * n_t) + [pl.BlockSpec(memory_space=pl.ANY)],
        out_specs=[hbm] * (2 * n_t), out_shape=bufs,
        input_output_aliases={2 + i: i for i in range(2 * n_t)},
        compiler_params=pltpu.CompilerParams(has_side_effects=pltpu.SideEffectType.DATAFLOW_SIDE_EFFECTING),
    )(*state, after)
    return outs[n_t:]


def _sum_slots(parts, name):
    n, rows, w = parts.shape
    tb = 8

    def body(p_ref, o_ref):
        acc = p_ref[0].astype(F32)
        for s in range(1, n):
            acc = acc + p_ref[s].astype(F32)
        o_ref[...] = acc

    return pl.pallas_call(
        body, name=name, grid=(rows // tb,),
        in_specs=[pl.BlockSpec((n, tb, w), lambda i: (0, i, 0))], out_specs=pl.BlockSpec((tb, w), lambda i: (i, 0)),
        out_shape=SDS((rows, w), F32), compiler_params=_params(("parallel",)),
    )(parts)


def _adamw(wt, m, v, parts, name, own=None):
    shape = wt.shape
    parts = parts if isinstance(parts, (list, tuple)) else [parts]
    n, w = parts[0].shape[0], shape[-1]
    rows = math.prod(shape[:-1])
    per = rows // len(parts)
    step = 16 if parts[0].dtype == BF16 else 8
    tb = max([t for t in range(step, 257, step) if per % t == 0] or [per])
    nb = per // tb
    c1 = 1.0 / (1.0 - ADAM_B1 ** ADAM_STEP)
    c2 = 1.0 / (1.0 - ADAM_B2 ** ADAM_STEP)
    state = [a.reshape(rows, w) for a in (wt, m, v)]
    n_in = 4 if own is None else 5
    outs = None
    for l in reversed(range(len(parts))):
        def body(*refs):
            idx_ref = None if own is None else refs[0]
            w_ref, m_ref, v_ref, p_ref = refs[n_in - 4:n_in] if own is None else refs[1:5]
            g_out, d_out, m_out, v_out = refs[-4:]
            g = None
            for s in range(n):
                term = p_ref[s] if own is None else jnp.where(idx_ref[0] == s, refs[5][0], p_ref[s])
                g = term.astype(F32) if g is None else g + term.astype(F32)
            m_new = ADAM_B1 * m_ref[...] + (1.0 - ADAM_B1) * g
            v_new = ADAM_B2 * v_ref[...] + (1.0 - ADAM_B2) * (g * g)
            g_out[...] = g
            m_out[...] = m_new
            v_out[...] = v_new
            d_out[...] = -ADAM_LR * ((m_new * c1) / (jnp.sqrt(v_new * c2) + ADAM_EPS) + ADAM_WD * w_ref[...])

        row = pl.BlockSpec((tb, w), lambda i, *_, l=l: (l * nb + i, 0))
        in_specs = [row, row, row, pl.BlockSpec((n, tb, w), lambda i, *_: (0, i, 0))]
        args = [*state, parts[l].reshape(n, per, w)]
        if own is not None:
            in_specs.append(pl.BlockSpec((1, tb, w), lambda i, idx: (idx[0], i, 0)))
            args.append(own[l].reshape(n, per, w))
        prev = [] if outs is None else list(outs)
        in_specs += [pl.BlockSpec(memory_space=pl.ANY)] * len(prev)
        n_pre = 0 if own is None else 1
        call = dict(name=f"{name}_{l}", out_shape=[SDS((rows, w), F32)] * 4,
                    input_output_aliases={n_pre + len(args) + k: k for k in range(len(prev))},
                    compiler_params=_params(("parallel",)))
        if own is None:
            outs = pl.pallas_call(body, grid=(nb,), in_specs=in_specs, out_specs=[row] * 4, **call)(*args, *prev)
        else:
            spec = pltpu.PrefetchScalarGridSpec(num_scalar_prefetch=1, grid=(nb,), in_specs=in_specs, out_specs=[row] * 4)
            idx = jnp.reshape(_index(_place()), (1,)).astype(jnp.int32)
            outs = pl.pallas_call(body, grid_spec=spec, **call)(idx, *args, *prev)
    return [o.reshape(shape) for o in outs]


_BIG = [("w_in", 2), ("mla_w_q_up", 2), ("mla_w_kv_up", 2), ("w_branch", 3), ("w_o", 1), ("ffn_w_up", 2), ("ffn_w_down", 1)]
_SMALL_SHARDED = [("meta_tokens", 1), ("ffn_conv_w", 2)]
_REPLICATED = ["norm1_g", "fox_forget_b", "fox_q_g", "fox_k_g", "mla_q_a_g", "mla_kv_a_g", "mla_q_g", "mla_k_g",
               "swa_q_g", "swa_k_g", "swa_sinks", "norm2_g", "ffn_conv_b"]
_ORDER = ["meta_tokens", "norm1_g", "w_in", "fox_forget_b", "fox_q_g", "fox_k_g", "mla_q_a_g", "mla_w_q_up",
          "mla_kv_a_g", "mla_w_kv_up", "mla_q_g", "mla_k_g", "swa_q_g", "swa_k_g", "swa_sinks", "w_branch", "w_o",
          "norm2_g", "ffn_w_up", "ffn_conv_w", "ffn_conv_b", "ffn_w_down"]


def _flat_rows(vecs, dtype, row_mult):
    flat = jnp.concatenate([a.reshape(-1).astype(dtype) for a in vecs])
    rows = -(-flat.shape[0] // (1024 * row_mult)) * row_mult
    return jnp.pad(flat, (0, rows * 1024 - flat.shape[0])).reshape(rows, 1024)


def _unflatten(flat, shapes):
    out, off = [], 0
    for s in shapes:
        n = math.prod(s)
        out.append(flat[off:off + n].reshape(s))
        off += n
    return out


def _to_full(blocks, axis):
    moved = jnp.moveaxis(blocks, 0, axis)
    s = moved.shape
    return moved.reshape(s[:axis] + (s[axis] * s[axis + 1],) + s[axis + 2:])


def _to_blocks(full, axis):
    s = full.shape
    split = full.reshape(s[:axis] + (N_DEV, s[axis] // N_DEV) + s[axis + 1:])
    return jnp.moveaxis(split, axis, 0)


def kernel(x, meta_tokens, norm1_g, w_in, fox_forget_b, fox_q_g, fox_k_g, mla_q_a_g, mla_w_q_up, mla_kv_a_g, mla_w_kv_up, mla_q_g, mla_k_g, swa_q_g, swa_k_g, swa_sinks, w_branch, w_o, norm2_g, ffn_w_up, ffn_conv_w, ffn_conv_b, ffn_w_down, loss_target, m_meta_tokens, m_norm1_g, m_w_in, m_fox_forget_b, m_fox_q_g, m_fox_k_g, m_mla_q_a_g, m_mla_w_q_up, m_mla_kv_a_g, m_mla_w_kv_up, m_mla_q_g, m_mla_k_g, m_swa_q_g, m_swa_k_g, m_swa_sinks, m_w_branch, m_w_o, m_norm2_g, m_ffn_w_up, m_ffn_conv_w, m_ffn_conv_b, m_ffn_w_down, v_meta_tokens, v_norm1_g, v_w_in, v_fox_forget_b, v_fox_q_g, v_fox_k_g, v_mla_q_a_g, v_mla_w_q_up, v_mla_kv_a_g, v_mla_w_kv_up, v_mla_q_g, v_mla_k_g, v_swa_q_g, v_swa_k_g, v_swa_sinks, v_w_branch, v_w_o, v_norm2_g, v_ffn_w_up, v_ffn_conv_w, v_ffn_conv_b, v_ffn_w_down):
    wl = dict(zip(_ORDER, (meta_tokens, norm1_g, w_in, fox_forget_b, fox_q_g, fox_k_g, mla_q_a_g, mla_w_q_up,
                           mla_kv_a_g, mla_w_kv_up, mla_q_g, mla_k_g, swa_q_g, swa_k_g, swa_sinks, w_branch, w_o,
                           norm2_g, ffn_w_up, ffn_conv_w, ffn_conv_b, ffn_w_down)))
    ml = dict(zip(_ORDER, (m_meta_tokens, m_norm1_g, m_w_in, m_fox_forget_b, m_fox_q_g, m_fox_k_g, m_mla_q_a_g,
                           m_mla_w_q_up, m_mla_kv_a_g, m_mla_w_kv_up, m_mla_q_g, m_mla_k_g, m_swa_q_g, m_swa_k_g,
                           m_swa_sinks, m_w_branch, m_w_o, m_norm2_g, m_ffn_w_up, m_ffn_conv_w, m_ffn_conv_b,
                           m_ffn_w_down)))
    vl = dict(zip(_ORDER, (v_meta_tokens, v_norm1_g, v_w_in, v_fox_forget_b, v_fox_q_g, v_fox_k_g, v_mla_q_a_g,
                           v_mla_w_q_up, v_mla_kv_a_g, v_mla_w_kv_up, v_mla_q_g, v_mla_k_g, v_swa_q_g, v_swa_k_g,
                           v_swa_sinks, v_w_branch, v_w_o, v_norm2_g, v_ffn_w_up, v_ffn_conv_w, v_ffn_conv_b,
                           v_ffn_w_down)))
    small_sh = [n for n, _ in _SMALL_SHARDED]
    big = [n for n, _ in _BIG]
    axis_of = dict(_BIG)
    idx = _index(_place())

    def to_full(n, blocks, own=None):
        if own is not None:
            sel = (jnp.arange(N_DEV) == idx).reshape((N_DEV,) + (1,) * own.ndim)
            blocks = jnp.where(sel, own[None], blocks)
        return _to_full(blocks, axis_of[n] - 1)

    local = {(n, l): wl[n][l].astype(BF16) for n in big for l in range(DEPTH)}
    got = _gather([local[(n, 0)] for n in _MIX_BIG] + [wl[n] for n in small_sh], "gather_weights_l0_mix")
    full = {n: wl[n] for n in _REPLICATED}
    for (n, axis), blocks in zip(_SMALL_SHARDED, got[len(_MIX_BIG):]):
        full[n] = _to_full(blocks, axis)
    ready = {(n, 0): to_full(n, blocks) for n, blocks in zip(_MIX_BIG, got)}
    later = {"l0_ffn": [(n, 0) for n in _FFN_BIG], "l1": [(n, 1) for n in big]}
    states = {}
    for key, names in later.items():
        states[key], tick = _exchange_start([local[e] for e in names], "gather_weights_" + key + "_start", True, got[0])
        full["norm1_g"] = full["norm1_g"] + tick

    def fetch(l, stage, after):
        key = "l0_ffn" if l == 0 else "l1"
        if (l, stage) != (0, "mix") and key in states:
            lands = _exchange_wait(states.pop(key), after, "gather_weights_" + key + "_wait", True)
            ready.update({e: to_full(e[0], blocks, local[e]) for e, blocks in zip(later[key], lands)})
        return {n: ready[(n, l)] for n in (_MIX_BIG if stage == "mix" else _FFN_BIG)}

    blocks_of = lambda g, names: [_to_blocks(g[n], axis_of[n] - 1).astype(BF16) for n in names]
    early = {}

    def hook(l, stage, g):
        if l == DEPTH - 1 and stage == "mix":
            key, names = "l1", big
        elif l == 0:
            merge = ("w_branch", "w_o")
            groups = {"ffn": _FFN_BIG, "merge": merge, "mix": tuple(n for n in _MIX_BIG if n not in merge)}
            key, names = "l0_" + stage, groups[stage]
        else:
            return None
        sends = blocks_of(g, names)
        state, tick = _exchange_start(sends, "exchange_grads_" + key + "_start")
        early[key] = (names, l, sends, state)
        return tick

    loss, grad_x, grad_meta, grads = _local_step(x[0], loss_target[0], full, hook, fetch)
    result = {kind: {} for kind in ("grad", "delta", "new_m", "new_v")}
    small_grads = {k: jnp.stack([grads[l][k] for l in range(DEPTH)]) for k in grads[0] if k not in big}
    small_grads["meta_tokens"] = grad_meta
    small_full = _REPLICATED + small_sh
    mine_small = _flat_rows([small_grads[n] for n in small_full] + [loss], F32, 8)
    small_state, _ = _exchange_start([mine_small], "gather_small_grads_start", True)
    landed, sent = {}, {}
    after = early["l0_mix"][2][0]
    for key in ("l1", "l0_ffn", "l0_merge"):
        names, l, sends, state = early[key]
        got = _exchange_wait(state, after, "exchange_grads_" + key + "_wait")
        landed.update({(n, l): p for n, p in zip(names, got)})
        sent.update({(n, l): p for n, p in zip(names, sends)})

    def update(names):
        for n in names:
            outs = _adamw(wl[n], ml[n], vl[n], [landed[(n, l)] for l in range(DEPTH)], "adamw_" + n,
                          [sent[(n, l)] for l in range(DEPTH)])
            for kind, val in zip(result, outs):
                result[kind][n] = val

    update(_FFN_BIG)
    got_small = _exchange_wait(small_state, result["grad"]["ffn_w_down"], "gather_small_grads_wait", True)[0]
    sel = (jnp.arange(N_DEV) == idx).reshape(N_DEV, 1, 1)
    total_small = _sum_slots(jnp.where(sel, mine_small[None], got_small), "sum_small_grads").reshape(-1)
    pieces = _unflatten(total_small, [small_grads[n].shape for n in small_full] + [()])
    loss_total = pieces[-1]
    g_small = dict(zip(small_full, pieces[:-1]))
    for n, axis in _SMALL_SHARDED:
        size = wl[n].shape[axis]
        g_small[n] = lax.dynamic_slice_in_dim(g_small[n], idx * size, size, axis)
    flat = lambda d: _flat_rows([d[n] for n in small_full], F32, 8)
    small_out = _adamw(flat(wl), flat(ml), flat(vl), flat(g_small)[None], "adamw_small")
    for kind, fs in zip(result, small_out):
        result[kind].update(zip(small_full, _unflatten(fs.reshape(-1), [wl[n].shape for n in small_full])))
    names, l, sends, state = early["l0_mix"]
    got = _exchange_wait(state, small_out[0], "exchange_grads_l0_mix_wait")
    landed.update({(n, l): p for n, p in zip(names, got)})
    sent.update({(n, l): p for n, p in zip(names, sends)})
    update(_MIX_BIG)
    outs = [loss_total, grad_x[None]]
    for kind in ("grad", "delta", "new_m", "new_v"):
        outs += [result[kind][n] for n in _ORDER]
    return tuple(outs)
```

```python
import functools
import math

import numpy as np
import jax
import jax.numpy as jnp
from jax import lax
from jax.experimental import pallas as pl
from jax.experimental.pallas import tpu as pltpu

F32, BF16 = jnp.float32, jnp.bfloat16
SDS = jax.ShapeDtypeStruct

D_MODEL = 1024
N_META = 16
EPS = 1e-6
WINDOW = 128
ROPE_THETA = 10000.0
HEADS = 8
D_FF = 2816
DEPTH = 2
N_DEV = 8
ADAM_LR, ADAM_B1, ADAM_B2, ADAM_EPS, ADAM_WD, ADAM_STEP = 0.001, 0.9, 0.999, 1e-08, 0.01, 10

ROW_ALIGN = 384
TILE_MM = 384
TILE_ROW = 192
TILE_CONV_BWD = 128
TILE_ATT = 384
TILE_POST = 384
PAIRS = 2
VMEM_LIMIT = 56 * 1024 * 1024

GATES_W = 3072
OTHER_W = 2816
IN_W = GATES_W + OTHER_W
O_FQ, O_FK, O_FV, O_SQ, O_SK, O_SV, O_CQ, O_CKV, O_MISC = 0, 512, 1024, 1536, 2048, 2176, 2304, 2560, 2688
FF_LANE = 32

NEG = -1e30


def _dot(a, b):
    return jnp.dot(a, b, preferred_element_type=F32)


def _dot_nt(a, b):
    return lax.dot_general(a, b, (((1,), (1,)), ((), ())), preferred_element_type=F32)


def _dot_tn(a, b):
    return lax.dot_general(a, b, (((0,), (0,)), ((), ())), preferred_element_type=F32)


def _params(sem):
    return pltpu.CompilerParams(dimension_semantics=sem, vmem_limit_bytes=VMEM_LIMIT)


def _rms(x, g):
    return x * lax.rsqrt(jnp.mean(x * x, axis=-1, keepdims=True) + EPS) * g


def _split_dot(x, m, pieces=2):
    acc, rest = None, x
    for _ in range(pieces):
        part = rest.astype(BF16)
        rest = rest - part.astype(F32)
        acc = _dot(part, m) if acc is None else acc + _dot(part, m)
    return acc


@jax.custom_vjp
def _sel(x, m, mt):
    return _split_dot(x, m)


_sel.defvjp(lambda x, m, mt: (_split_dot(x, m), (m, mt)), lambda res, dy: (_split_dot(dy, res[1]), None, None))


@jax.custom_vjp
def _mm(x, w):
    return _dot(x.astype(BF16), w.astype(BF16))


def _mm_bwd(res, dy):
    x, w = res
    dyb = dy.astype(BF16)
    return _dot_nt(dyb, w.astype(BF16)), _dot_tn(x.astype(BF16), dyb)


_mm.defvjp(lambda x, w: (_mm(x, w), (x, w)), _mm_bwd)


def _rot_impl(x):
    w = x.shape[1]
    lane = lax.broadcasted_iota(jnp.int32, x.shape, 1) % 128
    lo = (lane >= 64) & (lane < 80)
    hi = (lane >= 80) & (lane < 96)
    return jnp.where(hi, pltpu.roll(x, 16, 1), 0.0) - jnp.where(lo, pltpu.roll(x, w - 16, 1), 0.0)


@jax.custom_vjp
def _rot(x):
    return _rot_impl(x)


_rot.defvjp(lambda x: (_rot_impl(x), None), lambda _, dy: (-_rot_impl(dy),))


def _gnorm(x, g, e, et, dim):
    inv = lax.rsqrt(_sel(x * x, e, et) * (1.0 / dim) + EPS)
    return x * _sel(inv, et, e) * g


def _indicator(width, period):
    m = np.zeros((width, 128), np.float32)
    m[np.arange(width), np.arange(width) // period] = 1.0
    return m


def _consts():
    e64 = _indicator(512, 64)
    e128 = _indicator(1024, 128)
    sk = np.zeros((128, 1024), np.float32)
    for h in range(HEADS):
        sk[np.arange(32), 128 * h + 64 + np.arange(32)] = 1.0
    dup = np.zeros((128, 256), np.float32)
    for g in range(2):
        for r in range(2):
            dup[64 * g + np.arange(64), 128 * g + 64 * r + np.arange(64)] = 1.0
    mats = [e64, e64.T, e128, e128.T, sk, sk.T, dup, dup.T]
    return [jnp.asarray(m, BF16) for m in mats]


def _fold_matrix(width, period):
    m = np.zeros((width, 128), np.float32)
    m[np.arange(width), np.arange(width) % period] = 1.0
    return jnp.asarray(m, BF16)


def _rope_tables(lp):
    half = 16
    freqs = ROPE_THETA ** (-np.arange(half, dtype=np.float32) / half)
    ang = np.arange(lp, dtype=np.float32)[:, None] * freqs[None, :]
    cos = np.ones((lp, 128), np.float32)
    sin = np.zeros((lp, 128), np.float32)
    cos[:, 64:80] = np.cos(ang)
    cos[:, 80:96] = np.cos(ang)
    sin[:, 64:80] = np.sin(ang)
    sin[:, 80:96] = np.sin(ang)
    return jnp.asarray(cos), jnp.asarray(sin)


def _norm_matmul(h, g, w, tn, name):
    lp, d = h.shape
    n = w.shape[1]
    tb = TILE_MM

    def body(h_ref, g_ref, w_ref, xn_ref, y_ref):
        @pl.when(pl.program_id(1) == 0)
        def _():
            xn_ref[...] = _rms(h_ref[...], g_ref[...]).astype(BF16)

        y_ref[...] = _dot(xn_ref[...], w_ref[...])

    return pl.pallas_call(
        body, name=name, grid=(lp // tb, n // tn),
        in_specs=[pl.BlockSpec((tb, d), lambda i, j: (i, 0)), pl.BlockSpec((1, d), lambda i, j: (0, 0)),
                  pl.BlockSpec((d, tn), lambda i, j: (0, j))],
        out_specs=[pl.BlockSpec((tb, d), lambda i, j: (i, 0)), pl.BlockSpec((tb, tn), lambda i, j: (i, j))],
        out_shape=[SDS((lp, d), BF16), SDS((lp, n), F32)],
        compiler_params=_params(("parallel", "arbitrary")),
    )(h, g, w)


def _matmul_residual(a, w, res, name):
    m, k = a.shape
    n = w.shape[1]
    tb = TILE_MM

    def body(a_ref, w_ref, r_ref, o_ref):
        o_ref[...] = r_ref[...] + _dot(a_ref[...], w_ref[...])

    return pl.pallas_call(
        body, name=name, grid=(m // tb,),
        in_specs=[pl.BlockSpec((tb, k), lambda i: (i, 0)), pl.BlockSpec((k, n), lambda i: (0, 0)),
                  pl.BlockSpec((tb, n), lambda i: (i, 0))],
        out_specs=pl.BlockSpec((tb, n), lambda i: (i, 0)),
        out_shape=SDS((m, n), F32),
        compiler_params=_params(("parallel",)),
    )(a, w, res)


def _matmul_nt(dy, w, tn, name):
    m, k = dy.shape
    n = w.shape[0]
    tb = TILE_MM

    def body(dy_ref, w_ref, o_ref):
        o_ref[...] = _dot_nt(dy_ref[...].astype(BF16), w_ref[...])

    return pl.pallas_call(
        body, name=name, grid=(m // tb, n // tn),
        in_specs=[pl.BlockSpec((tb, k), lambda i, j: (i, 0)), pl.BlockSpec((tn, k), lambda i, j: (j, 0))],
        out_specs=pl.BlockSpec((tb, tn), lambda i, j: (i, j)),
        out_shape=SDS((m, n), F32),
        compiler_params=_params(("parallel", "arbitrary")),
    )(dy, w)


def _matmul_tn(x, dy, tn, name):
    m, k = x.shape
    n = dy.shape[1]
    tb = TILE_MM
    nb = m // tb

    def body(x_ref, dy_ref, o_ref, acc):
        i = pl.program_id(1)

        @pl.when(i == 0)
        def _():
            acc[...] = jnp.zeros_like(acc)

        acc[...] += _dot_tn(x_ref[...].astype(BF16), dy_ref[...].astype(BF16))

        @pl.when(i == nb - 1)
        def _():
            o_ref[...] = acc[...].astype(BF16)

    return pl.pallas_call(
        body, name=name, grid=(n // tn, nb),
        in_specs=[pl.BlockSpec((tb, k), lambda j, i: (i, 0)), pl.BlockSpec((tb, tn), lambda j, i: (i, j))],
        out_specs=pl.BlockSpec((k, tn), lambda j, i: (0, j)),
        out_shape=SDS((k, n), BF16),
        scratch_shapes=[pltpu.VMEM((k, tn), F32)],
        compiler_params=_params(("parallel", "arbitrary")),
    )(x, dy)


def _norm_matmul_bwd(dys, w, x, g, dres, name):
    m, d = x.shape
    tb = TILE_MM
    widths = [a.shape[1] for a in dys]
    n_dy = len(dys)

    def body(*refs):
        w_ref, x_ref, g_ref, r_ref, o_ref, dg_ref = refs[n_dy:]

        @pl.when(pl.program_id(0) == 0)
        def _():
            dg_ref[...] = jnp.zeros_like(dg_ref)

        dxn, off = None, 0
        for dy_ref, width in zip(refs[:n_dy], widths):
            part = _dot_nt(dy_ref[...], w_ref[:, off:off + width])
            dxn = part if dxn is None else dxn + part
            off += width
        _, vjp = jax.vjp(_rms, x_ref[...], g_ref[...])
        dx, dg = vjp(dxn)
        o_ref[...] = r_ref[...] + dx
        dg_ref[...] += dg

    row = pl.BlockSpec((tb, d), lambda i: (i, 0))
    vec = pl.BlockSpec((1, d), lambda i: (0, 0))
    return pl.pallas_call(
        body, name=name, grid=(m // tb,),
        in_specs=[pl.BlockSpec((tb, wd), lambda i: (i, 0)) for wd in widths]
        + [pl.BlockSpec(w.shape, lambda i: (0, 0)), row, vec, row],
        out_specs=[row, vec],
        out_shape=[SDS((m, d), F32), SDS((1, d), F32)],
        compiler_params=_params(("arbitrary",)),
    )(*dys, w, x, g, dres)


def _prep_math(pieces, prm, consts, cos, sin):
    fq, fk, sq, sk, sv, cq, ckv, misc = pieces
    gfq, gfk, gsq, gsk, fb, gqa, gkva, gmq, gmk, wq, wkk, wkv = prm
    e64, e64t, e128, e128t, skm, skt, dup, dupt = consts
    cos8 = jnp.concatenate([cos] * HEADS, axis=1)
    sin8 = jnp.concatenate([sin] * HEADS, axis=1)
    fq_n = _gnorm(fq, gfq, e64, e64t, 64)
    fk_n = _gnorm(fk, gfk, e64, e64t, 64)
    ls = jax.nn.log_sigmoid(misc + fb)
    q = _gnorm(_mm(_rms(cq, gqa), wq), gmq, e128, e128t, 96)
    mq = q * cos8 + _rot(q) * sin8
    kva = _rms(ckv, gkva)
    k = _gnorm(_mm(kva, wkk) + _sel(misc, skm, skt), gmk, e128, e128t, 96)
    mk = k * cos8 + _rot(k) * sin8
    mv = _mm(kva, wkv)
    sq_n = _gnorm(sq, gsq, e64, e64t, 64)
    sk_n = _gnorm(sk, gsk, e64[0:128], e64t[:, 0:128], 64)
    skd = _sel(sk_n, dup, dupt)
    svd = _sel(sv, dup, dupt)
    return fq_n, fk_n, ls, mq, mk, mv, sq_n, skd, svd


_PIECES = [(O_FQ, 512), (O_FK, 512), (O_SQ, 512), (O_SK, 128), (O_SV, 128), (O_CQ, 256), (O_CKV, 128), (O_MISC, 128)]
_PRM_SHAPES = [(1, 512), (1, 512), (1, 512), (1, 128), (1, 128), (1, 256), (1, 128), (1, 1024), (1, 1024),
               (256, 1024), (128, 1024), (128, 512)]
_CONST_SHAPES = [(512, 128), (128, 512), (1024, 128), (128, 1024), (128, 1024), (1024, 128), (128, 256), (256, 128)]


def _piece_specs(tb):
    def spec(off, width):
        blk = (GATES_W + off) // width
        return pl.BlockSpec((tb, width), lambda i, blk=blk: (i, blk))
    return [spec(o, w) for o, w in _PIECES] + [spec(O_FV, 512)]


def _full_specs(shapes):
    return [pl.BlockSpec(s, lambda i: (0, 0)) for s in shapes]


def _prep_fwd(proj, prm, consts, cos, sin, name):
    lp = proj.shape[0]
    tb = TILE_ROW
    row = lambda w: pl.BlockSpec((tb, w), lambda i: (i, 0))

    def body(*refs):
        pieces = [r[...] for r in refs[0:8]]
        fv = refs[8][...]
        prm_v = [r[...] for r in refs[9:21]]
        consts_v = [r[...] for r in refs[21:29]]
        cos_v, sin_v = refs[29][...], refs[30][...]
        outs = refs[31:]
        fq_n, fk_n, ls, mq, mk, mv, sq_n, skd, svd = _prep_math(pieces, prm_v, consts_v, cos_v, sin_v)
        for ref, val in zip(outs, (fq_n, fk_n, fv, mq, mk, mv, sq_n, skd, svd)):
            ref[...] = val.astype(BF16)
        outs[9][...] = ls

    widths = [512, 512, 512, 1024, 1024, 512, 512, 256, 256]
    return pl.pallas_call(
        body, name=name, grid=(lp // tb,),
        in_specs=_piece_specs(tb) + _full_specs(_PRM_SHAPES) + _full_specs(_CONST_SHAPES) + [row(128), row(128)],
        out_specs=[row(w) for w in widths] + [row(128)],
        out_shape=[SDS((lp, w), BF16) for w in widths] + [SDS((lp, 128), F32)],
        compiler_params=_params(("parallel",)),
    )(*([proj] * 9), *prm, *consts, cos, sin)


def _prep_bwd(proj, prm, consts, cos, sin, cots, folds, name):
    lp = proj.shape[0]
    tb = TILE_ROW
    row = lambda w: pl.BlockSpec((tb, w), lambda i: (i, 0))
    fold64, fold128 = folds

    def body(*refs):
        pieces = [r[...] for r in refs[0:8]]
        prm_v = [r[...] for r in refs[9:21]]
        consts_v = [r[...] for r in refs[21:29]]
        cos_v, sin_v = refs[29][...], refs[30][...]
        dfq, dfk, dfv, dmq, dmk, dmv, dsq, dskd, dsvd, dls = [r[...] for r in refs[31:41]]
        f64, f128 = refs[41][...], refs[42][...]
        d_ref = refs[43]
        g_refs = refs[44:]

        @pl.when(pl.program_id(0) == 0)
        def _():
            for r in g_refs:
                r[...] = jnp.zeros_like(r)

        f = lambda pc, pr: _prep_math(pc, pr, consts_v, cos_v, sin_v)
        _, vjp = jax.vjp(f, pieces, prm_v)
        dpc, dprm = vjp((dfq, dfk, dls, dmq, dmk, dmv, dsq, dskd, dsvd))
        d_fq, d_fk, d_sq, d_sk, d_sv, d_cq, d_ckv, d_misc = dpc
        for off, val in ((O_FQ, d_fq), (O_FK, d_fk), (O_FV, dfv), (O_SQ, d_sq), (O_SK, d_sk), (O_SV, d_sv),
                         (O_CQ, d_cq), (O_CKV, d_ckv), (O_MISC, d_misc)):
            d_ref[:, off:off + val.shape[1]] = val.astype(BF16)
        folded = {0: f64, 1: f64, 2: f64, 3: f64[0:128], 7: f128, 8: f128}
        for idx, (ref, val) in enumerate(zip(g_refs, dprm)):
            if idx in folded:
                ref[...] += _split_dot(jnp.broadcast_to(val, (8, val.shape[1])), folded[idx], 3)
            elif val.shape[0] == 1:
                ref[...] += jnp.broadcast_to(val, ref.shape)
            else:
                ref[...] += val

    g_shapes = [(8, 128), (8, 128), (8, 128), (8, 128), (8, 128), (8, 256), (8, 128), (8, 128), (8, 128),
                (256, 1024), (128, 1024), (128, 512)]
    cot_widths = [512, 512, 512, 1024, 1024, 512, 512, 256, 256, 128]
    return pl.pallas_call(
        body, name=name, grid=(lp // tb,),
        in_specs=(_piece_specs(tb) + _full_specs(_PRM_SHAPES) + _full_specs(_CONST_SHAPES) + [row(128), row(128)]
                  + [row(w) for w in cot_widths] + _full_specs([(512, 128), (1024, 128)])),
        out_specs=[row(OTHER_W)] + _full_specs(g_shapes),
        out_shape=[SDS((lp, OTHER_W), BF16)] + [SDS(s, F32) for s in g_shapes],
        compiler_params=_params(("arbitrary",)),
    )(*([proj] * 9), *prm, *consts, cos, sin, *cots, fold64, fold128)


def _cumsum(xs, reverse, name):
    lp = xs[0].shape[0]
    tb = TILE_MM
    nb = lp // tb
    n_in = len(xs)
    idx = (lambda i: (nb - 1 - i, 0)) if reverse else (lambda i: (i, 0))

    def body(*refs):
        o_ref, carry = refs[n_in], refs[n_in + 1]

        @pl.when(pl.program_id(0) == 0)
        def _():
            carry[...] = jnp.zeros_like(carry)

        x = refs[0][...]
        for r in refs[1:n_in]:
            x = x + r[...]
        r_i = lax.broadcasted_iota(jnp.int32, (tb, tb), 0)
        c_i = lax.broadcasted_iota(jnp.int32, (tb, tb), 1)
        tri = ((c_i >= r_i) if reverse else (c_i <= r_i)).astype(BF16)
        acc, rest = None, x
        for _ in range(3):
            part = rest.astype(BF16)
            rest = rest - part.astype(F32)
            acc = _dot(tri, part) if acc is None else acc + _dot(tri, part)
        o_ref[...] = acc + carry[...]
        carry[...] += jnp.sum(x, axis=0, keepdims=True)

    return pl.pallas_call(
        body, name=name, grid=(nb,),
        in_specs=[pl.BlockSpec((tb, 128), idx)] * n_in,
        out_specs=pl.BlockSpec((tb, 128), idx),
        out_shape=SDS((lp, 128), F32),
        scratch_shapes=[pltpu.VMEM((1, 128), F32)],
        compiler_params=_params(("arbitrary",)),
    )(*xs)


class _Att:
    def __init__(self, mode):
        self.mode = mode
        self.wide = mode == "mla"
        self.qw = 256 if self.wide else 128
        self.scale = (96 if mode == "mla" else 64) ** -0.5

    def resident(self, x, lo, scaled):
        if self.wide:
            return x[:, 0:128], x[:, 128:256]
        if scaled:
            x = x * jnp.asarray(self.scale, x.dtype)
        zero = jnp.zeros_like(x)
        return jnp.where(lo, x, zero), jnp.where(lo, zero, x)

    def moving(self, x):
        return (x[:, 0:128], x[:, 128:256]) if self.wide else (x, x)

    def logits(self, a, b, qpos, kpos, key_decay, slope, masked):
        s = _dot_nt(a, b)
        if self.wide:
            s = s * self.scale
        if self.mode == "fox":
            s = s - key_decay
        if self.mode == "swa":
            s = s - slope * (qpos - kpos).astype(F32)
        if masked:
            ok = kpos <= qpos
            if self.mode == "swa":
                ok = ok & ((kpos < N_META) | (qpos - kpos < WINDOW))
            s = jnp.where(ok, s, NEG)
        return s


def _as_rows(col):
    return jnp.broadcast_to(col, (col.shape[0], 128)).T[0:8, :]


def _halves(x, lo):
    zero = jnp.zeros_like(x)
    return jnp.where(lo, x, zero), jnp.where(lo, zero, x)


def _kv_specs(att, lp, rows):
    if att.mode == "swa":
        return (pl.BlockSpec((rows, 128), lambda g, i: (i if rows != lp else 0, g)),) * 2
    return (pl.BlockSpec((rows, PAIRS * att.qw), lambda g, i: (i if rows != lp else 0, g)),
            pl.BlockSpec((rows, PAIRS * 128), lambda g, i: (i if rows != lp else 0, g)))


def _pair_cols(att, x, pp, width):
    return x if x.shape[1] == width else x[:, pp * width:(pp + 1) * width]


def _att_fwd(att, q, k, v, extra, name):
    lp = q.shape[0]
    t = TILE_ATT
    nq = lp // t
    qw = att.qw
    mode = att.mode
    nh = 2 * PAIRS

    def body(*refs):
        q_ref, k_ref, v_ref = refs[0:3]
        o_ref, lse_ref = refs[-2:]
        g, qi = pl.program_id(0), pl.program_id(1)
        lo = lax.broadcasted_iota(jnp.int32, (1, 128), 1) < 64
        q_all = q_ref[...]
        q_heads = [h for pp in range(PAIRS) for h in att.resident(_pair_cols(att, q_all, pp, qw), lo, True)]
        qpos = qi * t + lax.broadcasted_iota(jnp.int32, (t, 1), 0)

        def step(first, cols, carry, masked):
            ks = pl.multiple_of(first, 128)
            kc, vc = k_ref[pl.ds(ks, cols), :], v_ref[pl.ds(ks, cols), :]
            kpos = first + lax.broadcasted_iota(jnp.int32, (1, cols), 1)
            out = []
            for h in range(nh):
                pp = h // 2
                m, l, acc = carry[3 * h:3 * h + 3]
                k_h = att.moving(_pair_cols(att, kc, pp, qw))[h % 2]
                decay = refs[3][h, :, pl.ds(ks, cols)] if mode == "fox" else None
                slope = refs[4][nh * g + h] if mode == "swa" else None
                s = att.logits(q_heads[h], k_h, qpos, kpos, decay, slope, masked)
                m_new = jnp.maximum(m, jnp.max(s, axis=-1, keepdims=True))
                alpha = jnp.exp(m - m_new)
                pe = jnp.exp(s - m_new)
                l = alpha * l + jnp.sum(pe, axis=-1, keepdims=True)
                acc = alpha * acc + _dot(pe.astype(BF16), _pair_cols(att, vc, pp, 128))
                out += [m_new, l, acc]
            return tuple(out)

        init = []
        for h in range(nh):
            if mode == "swa":
                init += [jnp.full((t, 1), refs[3][nh * g + h], F32), jnp.ones((t, 1), F32)]
            else:
                init += [jnp.full((t, 1), NEG, F32), jnp.zeros((t, 1), F32)]
            init.append(jnp.zeros((t, 128), F32))
        if mode == "swa":
            band = jnp.maximum(qi * t - WINDOW, 0)
            carry = lax.fori_loop(0, (band >= 128).astype(jnp.int32), lambda j, c: step(0, 128, c, True), tuple(init))
            carry = step(band, t + WINDOW, carry, True)
        else:
            carry = lax.fori_loop(0, qi // 2, lambda j, c: step(2 * j * t, 2 * t, c, False), tuple(init))
            carry = lax.fori_loop(0, qi % 2, lambda j, c: step((qi - 1) * t, t, c, False), carry)
            carry = step(qi * t, t, carry, True)
        outs = []
        for pp in range(PAIRS):
            (ma, la, acca), (mb, lb, accb) = carry[6 * pp:6 * pp + 3], carry[6 * pp + 3:6 * pp + 6]
            outs.append(jnp.where(lo, acca / la, accb / lb).astype(BF16))
            lse_ref[2 * pp] = ma + jnp.log(la)
            lse_ref[2 * pp + 1] = mb + jnp.log(lb)
        o_ref[...] = jnp.concatenate(outs, axis=1)

    in_specs = [pl.BlockSpec((t, PAIRS * qw), lambda g, i: (i, g)), *_kv_specs(att, lp, lp)]
    if mode == "fox":
        in_specs += [pl.BlockSpec((nh, 1, lp), lambda g, i: (g, 0, 0))]
    if mode == "swa":
        in_specs += [pl.BlockSpec(memory_space=pltpu.SMEM)] * 2
    return pl.pallas_call(
        body, name=name, grid=(4 // PAIRS, nq), in_specs=in_specs,
        out_specs=[pl.BlockSpec((t, PAIRS * 128), lambda g, i: (i, g)), pl.BlockSpec((nh, t, 1), lambda g, i: (g, i, 0))],
        out_shape=[SDS((lp, 512), BF16), SDS((HEADS, lp, 1), F32)],
        compiler_params=_params(("parallel", "arbitrary")),
    )(q, k, v, *extra)


def _att_bwd(att, q, k, v, o, do, lse, extra, name):
    lp = q.shape[0]
    t = TILE_ATT
    nq = lp // t
    qw = att.qw
    mode = att.mode
    nh = 2 * PAIRS
    kw = 128 if mode == "swa" else PAIRS * qw
    vw = 128 if mode == "swa" else PAIRS * 128

    def body(*refs):
        q_ref, k_ref, v_ref, o_ref, do_ref, lse_ref = refs[0:6]
        n_out = {"fox": 5, "mla": 3, "swa": 4}[mode]
        outs = refs[len(refs) - n_out:]
        dq_ref, dk_ref, dv_ref = outs[0:3]
        g, qi = pl.program_id(0), pl.program_id(1)

        @pl.when(qi == 0)
        def _():
            dk_ref[...] = jnp.zeros_like(dk_ref)
            dv_ref[...] = jnp.zeros_like(dv_ref)
            if mode == "fox":
                outs[4][...] = jnp.zeros_like(outs[4])

        lo = lax.broadcasted_iota(jnp.int32, (1, 128), 1) < 64
        q_all, do_all = q_ref[...], do_ref[...]
        prod = do_all.astype(F32) * o_ref[...].astype(F32)
        q_heads, q_plain, do_heads, do_pairs, delta = [], [], [], [], []
        for pp in range(PAIRS):
            q_pp = _pair_cols(att, q_all, pp, qw)
            q_heads += att.resident(q_pp, lo, True)
            q_plain += att.moving(q_pp)
            do_pp = _pair_cols(att, do_all, pp, 128)
            do_pairs.append(do_pp)
            do_heads += _halves(do_pp, lo)
            pr_pp = _pair_cols(att, prod, pp, 128)
            delta += [jnp.sum(jnp.where(lo, pr_pp, 0.0), axis=-1, keepdims=True),
                      jnp.sum(jnp.where(lo, 0.0, pr_pp), axis=-1, keepdims=True)]
        lse_v = [lse_ref[h] for h in range(nh)]
        qpos = qi * t + lax.broadcasted_iota(jnp.int32, (t, 1), 0)

        def step(first, cols, carry, masked):
            ks = pl.multiple_of(first, 128)
            kc, vc = k_ref[pl.ds(ks, cols), :], v_ref[pl.ds(ks, cols), :]
            kpos = first + lax.broadcasted_iota(jnp.int32, (1, cols), 1)
            out, dk_parts, dv_parts = [], [], []
            for h in range(nh):
                pp = h // 2
                k_h = att.moving(_pair_cols(att, kc, pp, qw))[h % 2]
                decay = refs[6][h, :, pl.ds(ks, cols)] if mode == "fox" else None
                slope = refs[7][nh * g + h] if mode == "swa" else None
                s = att.logits(q_heads[h], k_h, qpos, kpos, decay, slope, masked)
                pr = jnp.exp(s - lse_v[h])
                ds = pr * (_dot_nt(do_heads[h], _pair_cols(att, vc, pp, 128)) - delta[h])
                dsb = ds.astype(BF16)
                out.append(carry[2 * h] + _dot(dsb, k_h))
                out.append(carry[2 * h + 1] + jnp.sum(ds, axis=-1, keepdims=True) if mode == "fox" else carry[2 * h + 1])
                dk_parts.append(_dot_tn(dsb, q_plain[h]))
                dv_parts.append(_dot_tn(pr.astype(BF16), do_pairs[pp]))
                if mode == "fox":
                    outs[4][h, 0:1, pl.ds(ks, cols)] -= jnp.sum(ds, axis=0, keepdims=True)
            rows = pl.ds(ks, cols)
            for pp in range(PAIRS):
                dv_pp = jnp.where(lo, dv_parts[2 * pp], dv_parts[2 * pp + 1])
                if att.wide:
                    dk_pp = jnp.concatenate(dk_parts[2 * pp:2 * pp + 2], axis=1) * att.scale
                else:
                    dk_pp = jnp.where(lo, dk_parts[2 * pp], dk_parts[2 * pp + 1]) * att.scale
                if mode == "swa":
                    dk_ref[rows, :] += dk_pp
                    dv_ref[rows, :] += dv_pp
                else:
                    dk_ref[rows, pp * qw:(pp + 1) * qw] += dk_pp
                    dv_ref[rows, pp * 128:(pp + 1) * 128] += dv_pp
            return tuple(out)

        init = (jnp.zeros((t, 128), F32), jnp.zeros((t, 1), F32)) * nh
        if mode == "swa":
            band = jnp.maximum(qi * t - WINDOW, 0)
            carry = lax.fori_loop(0, (band >= 128).astype(jnp.int32), lambda j, c: step(0, 128, c, True), init)
            carry = step(band, t + WINDOW, carry, True)
        else:
            carry = lax.fori_loop(0, qi // 2, lambda j, c: step(2 * j * t, 2 * t, c, False), init)
            carry = lax.fori_loop(0, qi % 2, lambda j, c: step((qi - 1) * t, t, c, False), carry)
            carry = step(qi * t, t, carry, True)
        dq = []
        for pp in range(PAIRS):
            dqa, dca, dqb, dcb = carry[4 * pp:4 * pp + 4]
            dq += [dqa, dqb] if att.wide else [jnp.where(lo, dqa, dqb)]
            if mode == "fox":
                outs[3][2 * pp] = _as_rows(dca)
                outs[3][2 * pp + 1] = _as_rows(dcb)
        dq_ref[...] = jnp.concatenate(dq, axis=1) * att.scale
        if mode == "swa":
            ds_ref = outs[3]

            @pl.when(qi == 0)
            def _():
                ds_ref[...] = jnp.zeros_like(ds_ref)

            lane = lax.broadcasted_iota(jnp.int32, (8, 128), 1)
            acc = jnp.zeros((8, 128), F32)
            for h in range(nh):
                tot = -jnp.sum(jnp.exp(refs[6][nh * g + h] - lse_v[h]) * delta[h])
                acc = acc + jnp.where(lane == h, tot, 0.0)
            ds_ref[0] += acc

    col = pl.BlockSpec((nh, t, 1), lambda g, i: (g, i, 0))
    in_specs = [pl.BlockSpec((t, PAIRS * qw), lambda g, i: (i, g)), *_kv_specs(att, lp, lp),
                pl.BlockSpec((t, PAIRS * 128), lambda g, i: (i, g)), pl.BlockSpec((t, PAIRS * 128), lambda g, i: (i, g)), col]
    out_specs = [pl.BlockSpec((t, PAIRS * qw), lambda g, i: (i, g)), pl.BlockSpec((lp, kw), lambda g, i: (0, g)),
                 pl.BlockSpec((lp, vw), lambda g, i: (0, g))]
    n_groups = 4 // PAIRS
    out_shape = [SDS((lp, 4 * qw), F32), SDS((lp, n_groups * kw), F32), SDS((lp, n_groups * vw), F32)]
    if mode == "fox":
        in_specs += [pl.BlockSpec((nh, 1, lp), lambda g, i: (g, 0, 0))]
        out_specs += [pl.BlockSpec((nh, 8, t), lambda g, i: (g, 0, i)), pl.BlockSpec((nh, 8, lp), lambda g, i: (g, 0, 0))]
        out_shape += [SDS((HEADS, 8, lp), F32)] * 2
    if mode == "swa":
        in_specs += [pl.BlockSpec(memory_space=pltpu.SMEM)] * 2
        out_specs.append(pl.BlockSpec((1, 8, 128), lambda g, i: (g, 0, 0)))
        out_shape.append(SDS((n_groups, 8, 128), F32))
    return pl.pallas_call(
        body, name=name, grid=(n_groups, nq), in_specs=in_specs, out_specs=out_specs, out_shape=out_shape,
        compiler_params=_params(("parallel", "arbitrary")),
    )(q, k, v, o, do, lse, *extra)


def _post_fwd(h, proj, outs, wb, wo, name):
    lp, d = h.shape
    tb = TILE_POST
    row = lambda w: pl.BlockSpec((tb, w), lambda i: (i, 0))

    def body(h_ref, g0, g1, g2, oa, ob, oc, wb_ref, wo_ref, o_ref):
        merged = jnp.zeros((tb, d), F32)
        for n, (g_ref, br) in enumerate(((g0, oa), (g1, ob), (g2, oc))):
            merged = merged + jax.nn.sigmoid(g_ref[...]) * _dot(br[...], wb_ref[n])
        o_ref[...] = h_ref[...] + _dot(merged.astype(BF16), wo_ref[...])

    gate = lambda n: pl.BlockSpec((tb, d), lambda i, n=n: (i, n))
    return pl.pallas_call(
        body, name=name, grid=(lp // tb,),
        in_specs=[row(d), gate(0), gate(1), gate(2), row(512), row(512), row(512),
                  pl.BlockSpec((3, 512, d), lambda i: (0, 0, 0)), pl.BlockSpec((d, d), lambda i: (0, 0))],
        out_specs=row(d), out_shape=SDS((lp, d), F32),
        compiler_params=_params(("parallel",)),
    )(h, proj, proj, proj, *outs, wb, wo)


def _post_bwd(dh, proj, outs, wb, wo, name):
    lp, d = dh.shape
    tb = TILE_POST
    row = lambda w: pl.BlockSpec((tb, w), lambda i: (i, 0))

    def body(dh_ref, g0, g1, g2, oa, ob, oc, wb_ref, wo_ref, dg_ref, doa, dob, doc, dwb_ref, dwo_ref):
        @pl.when(pl.program_id(0) == 0)
        def _():
            dwb_ref[...] = jnp.zeros_like(dwb_ref)
            dwo_ref[...] = jnp.zeros_like(dwo_ref)

        dhb = dh_ref[...].astype(BF16)
        dm = _dot_nt(dhb, wo_ref[...])
        merged = jnp.zeros((tb, d), F32)
        for n, (g_ref, br, do_ref) in enumerate(((g0, oa, doa), (g1, ob, dob), (g2, oc, doc))):
            gate = jax.nn.sigmoid(g_ref[...])
            o_n = br[...]
            y = _dot(o_n, wb_ref[n])
            merged = merged + gate * y
            dy = (dm * gate).astype(BF16)
            dg_ref[:, n * d:(n + 1) * d] = (dm * y * gate * (1.0 - gate)).astype(BF16)
            do_ref[...] = _dot_nt(dy, wb_ref[n]).astype(BF16)
            dwb_ref[n] += _dot_tn(o_n, dy)
        dwo_ref[...] += _dot_tn(merged.astype(BF16), dhb)

    gate = lambda n: pl.BlockSpec((tb, d), lambda i, n=n: (i, n))
    wb_spec = pl.BlockSpec((3, 512, d), lambda i: (0, 0, 0))
    wo_spec = pl.BlockSpec((d, d), lambda i: (0, 0))
    return pl.pallas_call(
        body, name=name, grid=(lp // tb,),
        in_specs=[row(d), gate(0), gate(1), gate(2), row(512), row(512), row(512), wb_spec, wo_spec],
        out_specs=[row(GATES_W), row(512), row(512), row(512), wb_spec, wo_spec],
        out_shape=[SDS((lp, GATES_W), BF16)] + [SDS((lp, 512), BF16)] * 3 + [SDS((3, 512, d), F32), SDS((d, d), F32)],
        compiler_params=_params(("arbitrary",)),
    )(dh, proj, proj, proj, *outs, wb, wo)


def _shift_down(x, halo, n, first):
    rows = lax.broadcasted_iota(jnp.int32, x.shape, 0)
    edge = jnp.concatenate([pltpu.roll(halo, n, 0), jnp.zeros((x.shape[0] - 8, x.shape[1]), F32)], axis=0)
    edge = jnp.where(first, 0.0, edge)
    return jnp.where(rows < n, edge, pltpu.roll(x, n, 0))


def _shift_up(x, halo, n, last):
    tb = x.shape[0]
    rows = lax.broadcasted_iota(jnp.int32, x.shape, 0)
    edge = jnp.concatenate([jnp.zeros((tb - 8, x.shape[1]), F32), pltpu.roll(halo, 8 - n, 0)], axis=0)
    edge = jnp.where(last, 0.0, edge)
    return jnp.where(rows >= tb - n, edge, pltpu.roll(x, tb - n, 0))


def _conv(u, halo, w_ref, b_ref, first):
    taps = (_shift_down(u, halo, 2, first), _shift_down(u, halo, 1, first), u)
    c = b_ref[...] + w_ref[0:1, :] * taps[0] + w_ref[1:2, :] * taps[1] + w_ref[2:3, :] * taps[2]
    return c, taps


def _ffn_specs(tb, f):
    hb = tb // 8
    cur = lambda c: pl.BlockSpec((tb, f), lambda i, c=c: (i, c))
    prev = lambda c: pl.BlockSpec((8, f), lambda i, c=c: (jnp.maximum(i * hb - 1, 0), c))
    vec = lambda r, c: pl.BlockSpec((r, f), lambda i, c=c: (0, c))
    return cur, prev, vec


def _ffn_act_fwd(u, cw, cb, name):
    lp = u.shape[0]
    f = D_FF
    tb = TILE_ROW
    cur, prev, vec = _ffn_specs(tb, f)

    def body(ug, uv, hg, hv, wg, wv, bg, bv, o_ref):
        first = pl.program_id(0) == 0
        cg, _ = _conv(ug[...], hg[...], wg, bg, first)
        cv, _ = _conv(uv[...], hv[...], wv, bv, first)
        o_ref[...] = (cg * jax.nn.sigmoid(cg) * cv).astype(BF16)

    return pl.pallas_call(
        body, name=name, grid=(lp // tb,),
        in_specs=[cur(0), cur(1), prev(0), prev(1), vec(8, 0), vec(8, 1), vec(1, 0), vec(1, 1)],
        out_specs=pl.BlockSpec((tb, f), lambda i: (i, 0)), out_shape=SDS((lp, f), BF16),
        compiler_params=_params(("parallel",)),
    )(u, u, u, u, cw, cw, cb, cb)


def _ffn_act_bwd_conv(u, dact, cw, cb, name):
    lp = u.shape[0]
    f = D_FF
    tb = TILE_CONV_BWD
    cur, prev, vec = _ffn_specs(tb, f)

    def body(ug, uv, hg, hv, wg, wv, bg, bv, da_ref, dc_ref, dw_ref, db_ref):
        first = pl.program_id(0) == 0

        @pl.when(first)
        def _():
            dw_ref[...] = jnp.zeros_like(dw_ref)
            db_ref[...] = jnp.zeros_like(db_ref)

        cg, tg = _conv(ug[...], hg[...], wg, bg, first)
        cv, tv = _conv(uv[...], hv[...], wv, bv, first)
        da = da_ref[...]
        sg = jax.nn.sigmoid(cg)
        dcg = da * cv * sg * (1.0 + cg * (1.0 - sg))
        dcv = da * cg * sg
        for c, (dc, taps) in enumerate(((dcg, tg), (dcv, tv))):
            dc_ref[:, c * f:(c + 1) * f] = dc
            for n in range(3):
                dw_ref[n:n + 1, c * f:(c + 1) * f] += jnp.sum(dc * taps[n], axis=0, keepdims=True)
            db_ref[0:1, c * f:(c + 1) * f] += jnp.sum(dc, axis=0, keepdims=True)

    acc = pl.BlockSpec((8, 2 * f), lambda i: (0, 0))
    return pl.pallas_call(
        body, name=name, grid=(lp // tb,),
        in_specs=[cur(0), cur(1), prev(0), prev(1), vec(8, 0), vec(8, 1), vec(1, 0), vec(1, 1),
                  pl.BlockSpec((tb, f), lambda i: (i, 0))],
        out_specs=[pl.BlockSpec((tb, 2 * f), lambda i: (i, 0)), acc, acc],
        out_shape=[SDS((lp, 2 * f), F32), SDS((8, 2 * f), F32), SDS((8, 2 * f), F32)],
        compiler_params=_params(("arbitrary",)),
    )(u, u, u, u, cw, cw, cb, cb, dact)


def _ffn_act_bwd_in(dc, cw, name):
    lp = dc.shape[0]
    f2 = 2 * D_FF
    tb = TILE_ROW
    nb = lp // tb
    hb = tb // 8

    def body(dc_ref, n_ref, w_ref, o_ref):
        last = pl.program_id(0) == nb - 1
        dcv, halo = dc_ref[...], n_ref[...]
        du = (w_ref[2:3, :] * dcv + w_ref[1:2, :] * _shift_up(dcv, halo, 1, last)
              + w_ref[0:1, :] * _shift_up(dcv, halo, 2, last))
        o_ref[...] = du.astype(BF16)

    cur = pl.BlockSpec((tb, f2), lambda i: (i, 0))
    return pl.pallas_call(
        body, name=name, grid=(nb,),
        in_specs=[cur, pl.BlockSpec((8, f2), lambda i: (jnp.minimum((i + 1) * hb, nb * hb - 1), 0)),
                  pl.BlockSpec((8, f2), lambda i: (0, 0))],
        out_specs=cur, out_shape=SDS((lp, f2), BF16),
        compiler_params=_params(("parallel",)),
    )(dc, dc, cw)


def _loss_head(y, target, n_real, name):
    lp, d = y.shape
    tb = TILE_MM

    def body(y_ref, t_ref, dy_ref, loss_ref):
        i = pl.program_id(0)

        @pl.when(i == 0)
        def _():
            loss_ref[...] = jnp.zeros_like(loss_ref)

        rows = i * tb + lax.broadcasted_iota(jnp.int32, (tb, 1), 0)
        real = (rows >= N_META) & (rows < N_META + n_real)
        diff = jnp.where(real, y_ref[...] - t_ref[...], 0.0)
        dy_ref[...] = diff * (1.0 / d)
        loss_ref[...] += (0.5 / d) * jnp.sum(diff * diff).reshape(1, 1)

    row = pl.BlockSpec((tb, d), lambda i: (i, 0))
    return pl.pallas_call(
        body, name=name, grid=(lp // tb,), in_specs=[row, row],
        out_specs=[row, pl.BlockSpec((1, 1), lambda i: (0, 0))],
        out_shape=[SDS((lp, d), F32), SDS((1, 1), F32)],
        compiler_params=_params(("arbitrary",)),
    )(y, target)


def _pad_lanes(v, width, at=0):
    return jnp.pad(v.astype(F32), (at, width - at - v.shape[0]))[None, :]


def _mix_params(w, big, l):
    b = lambda a: a.astype(BF16)
    win = big["w_in"]
    fq, fk, fv, ff, cq, ckv, kr, sq, sk, sv, gates = jnp.split(
        win, [512, 1024, 1536, 1544, 1800, 1928, 1960, 2472, 2600, 2728], axis=1)
    misc = jnp.concatenate([kr, ff, jnp.zeros((D_MODEL, 88), win.dtype)], axis=1)
    w_in = b(jnp.concatenate([gates, fq, fk, fv, sq, sk, sv, cq, ckv, misc], axis=1))
    wq = jnp.pad(big["mla_w_q_up"].reshape(256, HEADS, 96), ((0, 0), (0, 0), (0, 32))).reshape(256, 1024)
    wkv = big["mla_w_kv_up"].reshape(128, HEADS, 128)
    wkk = jnp.pad(wkv[:, :, :64], ((0, 0), (0, 0), (0, 64))).reshape(128, 1024)
    wkvv = wkv[:, :, 64:].reshape(128, 512)
    tile = lambda g, n: jnp.tile(g.astype(F32), n)[None, :]
    prm = [tile(w["fox_q_g"][l], 8), tile(w["fox_k_g"][l], 8), tile(w["swa_q_g"][l], 8), tile(w["swa_k_g"][l], 2),
           _pad_lanes(w["fox_forget_b"][l], 128, FF_LANE), w["mla_q_a_g"][l][None, :], w["mla_kv_a_g"][l][None, :],
           tile(jnp.pad(w["mla_q_g"][l], (0, 32)), 8), tile(jnp.pad(w["mla_k_g"][l], (0, 32)), 8),
           wq.astype(F32), wkk.astype(F32), wkvv.astype(F32)]
    return dict(g1=w["norm1_g"][l][None, :], w_in=w_in, prm=prm, sinks=w["swa_sinks"][l].astype(F32),
                wb=b(big["w_branch"]), wo=b(big["w_o"]))


def _ffn_params(w, big, l):
    cw = jnp.pad(w["ffn_conv_w"][l].astype(F32), ((0, 5), (0, 0)))
    return dict(g2=w["norm2_g"][l][None, :], w_up=big["ffn_w_up"].astype(BF16), cw=cw,
                cb=w["ffn_conv_b"][l][None, :].astype(F32), w_down=big["ffn_w_down"].astype(BF16))


def _decay_rows(c):
    return c[:, FF_LANE:FF_LANE + HEADS].T[:, None, :]


def _from_rows(row):
    return jnp.pad(row[:, 0, :].T, ((0, 0), (FF_LANE, 128 - FF_LANE - HEADS)))


def _layer_fwd_mix(h, lw, consts, cos, sin, slopes, l):
    tag = f"l{l}_"
    xn, proj = _norm_matmul(h, lw["g1"], lw["w_in"], IN_W, tag + "in_proj")
    fq, fk, fv, mq, mk, mv, sq, skd, svd, ls = _prep_fwd(proj, lw["prm"], consts, cos, sin, tag + "prep")
    c = _cumsum([ls], False, tag + "decay_cumsum")
    c_row = _decay_rows(c)
    oa, lse_a = _att_fwd(_Att("fox"), fq, fk, fv, (c_row,), tag + "fox_fwd")
    ob, lse_b = _att_fwd(_Att("mla"), mq, mk, mv, (), tag + "mla_fwd")
    oc, lse_c = _att_fwd(_Att("swa"), sq, skd, svd, (lw["sinks"], slopes), tag + "swa_fwd")
    h2 = _post_fwd(h, proj, (oa, ob, oc), lw["wb"], lw["wo"], tag + "merge")
    saved = dict(h=h, xn=xn, proj=proj, q=(fq, mq, sq), k=(fk, mk, skd), v=(fv, mv, svd), c=c_row,
                 o=(oa, ob, oc), lse=(lse_a, lse_b, lse_c), h2=h2)
    return h2, saved


def _layer_fwd_ffn(h2, lw, l):
    tag = f"l{l}_"
    xn2, u = _norm_matmul(h2, lw["g2"], lw["w_up"], 2 * D_FF, tag + "ffn_up")
    act = _ffn_act_fwd(u, lw["cw"], lw["cb"], tag + "ffn_act")
    h3 = _matmul_residual(act, lw["w_down"], h2, tag + "ffn_down")
    return h3, dict(xn2=xn2, u=u, act=act)


def _layer_bwd_ffn(dh3, lw, sv, l):
    tag = f"l{l}_"
    f = D_FF
    dact = _matmul_nt(dh3, lw["w_down"], f, tag + "ffn_down_dx")
    dw_down = _matmul_tn(sv["act"], dh3, D_MODEL, tag + "ffn_down_dw")
    dc, dcw, dcb = _ffn_act_bwd_conv(sv["u"], dact, lw["cw"], lw["cb"], tag + "ffn_act_dc")
    du = _ffn_act_bwd_in(dc, lw["cw"], tag + "ffn_act_du")
    dw_up = _matmul_tn(sv["xn2"], du, f, tag + "ffn_up_dw")
    dh2, dg2 = _norm_matmul_bwd([du], lw["w_up"], sv["h2"], lw["g2"], dh3, tag + "ffn_up_dx")
    g = dict(norm2_g=dg2[0], ffn_w_up=dw_up, ffn_conv_w=dcw[0:3], ffn_conv_b=dcb[0], ffn_w_down=dw_down)
    return dh2, g


def _layer_bwd_mix(dh2, lw, sv, consts, folds, cos, sin, slopes, l, hook=None, merge_hook=None):
    tag = f"l{l}_"
    dgates, doa, dob, doc, dwb, dwo = _post_bwd(dh2, sv["proj"], sv["o"], lw["wb"], lw["wo"], tag + "merge_bwd")
    c_row = sv["c"]
    tick = merge_hook({"w_branch": dwb, "w_o": dwo}) if merge_hook else None
    if tick is not None:
        c_row = c_row + tick
    extras = ((c_row,), (), (lw["sinks"], slopes))
    grads = []
    for n, (mode, do) in enumerate((("fox", doa), ("mla", dob), ("swa", doc))):
        res = _att_bwd(_Att(mode), sv["q"][n], sv["k"][n], sv["v"][n], sv["o"][n], do, sv["lse"][n], extras[n],
                       tag + mode + "_bwd")
        grads.append((res[0], res[1], res[2], res[3:]))
    (dfq, dfk, dfv, (dcq, dck)), (dmq, dmk, dmv, _), (dsq, dskd, dsvd, (dsink,)) = grads
    dls = _cumsum([_from_rows(dcq), _from_rows(dck)], True, tag + "decay_cumsum_bwd")
    res = _prep_bwd(sv["proj"], lw["prm"], consts, cos, sin,
                    (dfq, dfk, dfv, dmq, dmk, dmv, dsq, dskd, dsvd, dls), folds, tag + "prep_bwd")
    dother, pg = res[0], res[1:]
    dh, dg1 = _norm_matmul_bwd([dgates, dother], lw["w_in"], sv["h"], lw["g1"], dh2, tag + "in_proj_dx")
    dw_g = _matmul_tn(sv["xn"], dgates, GATES_W, tag + "in_proj_dw_gates")
    dw_o = _matmul_tn(sv["xn"], dother, OTHER_W, tag + "in_proj_dw_other")
    d_in = jnp.concatenate([
        dw_o[:, O_FQ:O_FV + 512], dw_o[:, O_MISC + FF_LANE:O_MISC + FF_LANE + 8], dw_o[:, O_CQ:O_CQ + 256],
        dw_o[:, O_CKV:O_CKV + 128], dw_o[:, O_MISC:O_MISC + 32], dw_o[:, O_SQ:O_SQ + 512], dw_o[:, O_SK:O_SK + 128],
        dw_o[:, O_SV:O_SV + 128], dw_g], axis=1)
    d_wq = pg[9].reshape(256, HEADS, 128)[:, :, :96].reshape(256, 768)
    d_wkv = jnp.concatenate([pg[10].reshape(128, HEADS, 128)[:, :, :64], pg[11].reshape(128, HEADS, 64)],
                            axis=2).reshape(128, 1024)
    g = dict(
        w_in=d_in, fox_forget_b=pg[4][0, FF_LANE:FF_LANE + 8], fox_q_g=pg[0][0, :64],
        fox_k_g=pg[1][0, :64], mla_q_a_g=pg[5][0], mla_w_q_up=d_wq, mla_kv_a_g=pg[6][0], mla_w_kv_up=d_wkv,
        mla_q_g=pg[7][0, :96], mla_k_g=pg[8][0, :96], swa_q_g=pg[2][0, :64], swa_k_g=pg[3][0, :64],
        swa_sinks=dsink[:, 0, 0:2 * PAIRS].reshape(HEADS), w_branch=dwb, w_o=dwo)
    if hook:
        hook(g)
    g["norm1_g"] = dg1[0]
    return dh, g


_MIX_BIG = ("w_in", "mla_w_q_up", "mla_w_kv_up", "w_branch", "w_o")
_FFN_BIG = ("ffn_w_up", "ffn_w_down")


def _local_step(x, target, w, hook=None, fetch=None):
    if fetch is None:
        fetch = lambda l, stage, after: {n: w[n][l] for n in (_MIX_BIG if stage == "mix" else _FFN_BIG)}
    seq = x.shape[0]
    length = N_META + seq
    lp = -(-length // ROW_ALIGN) * ROW_ALIGN
    pad = lp - length
    h = jnp.concatenate([w["meta_tokens"].astype(F32), x, jnp.zeros((pad, D_MODEL), F32)], axis=0)
    tgt = jnp.pad(target, ((N_META, pad), (0, 0)))
    consts = _consts()
    folds = (_fold_matrix(512, 64), _fold_matrix(1024, 128))
    cos, sin = _rope_tables(lp)
    slopes = jnp.asarray(2.0 ** (-8.0 * np.arange(1, HEADS + 1, dtype=np.float32) / HEADS), F32)
    lws, saved = [], []
    for l in range(DEPTH):
        lw = _mix_params(w, fetch(l, "mix", h), l)
        h, sv = _layer_fwd_mix(h, lw, consts, cos, sin, slopes, l)
        lw.update(_ffn_params(w, fetch(l, "ffn", h), l))
        h, sv_ffn = _layer_fwd_ffn(h, lw, l)
        lws.append(lw)
        saved.append({**sv, **sv_ffn})
    dh, loss = _loss_head(h, tgt, seq, "loss_head")
    grads = [None] * DEPTH
    for l in reversed(range(DEPTH)):
        dh, g_ffn = _layer_bwd_ffn(dh, lws[l], saved[l], l)
        tick = hook(l, "ffn", g_ffn) if hook else None
        if tick is not None:
            lws[l]["sinks"] = lws[l]["sinks"] + tick
        mix_hook = (lambda g, l=l, g_ffn=g_ffn: hook(l, "mix", {**g_ffn, **g})) if hook else None
        merge_hook = (lambda g, l=l: hook(l, "merge", g)) if hook else None
        dh, g_mix = _layer_bwd_mix(dh, lws[l], saved[l], consts, folds, cos, sin, slopes, l, mix_hook, merge_hook)
        grads[l] = {**g_ffn, **g_mix}
    return loss, dh[N_META:length], dh[:N_META], grads


def _place():
    return lax.axis_index("x"), lax.axis_index("y"), lax.axis_index("c")


def _flip(pos, k):
    x, y, c = pos
    return (1 - x if k & 4 else x, 1 - y if k & 2 else y, 1 - c if k & 1 else c)


def _index(pos):
    return 4 * pos[0] + 2 * pos[1] + pos[2]


def _gather(tensors, name):
    n_t = len(tensors)

    def body(*refs):
        ins, outs = refs[:n_t], refs[n_t:2 * n_t]
        send_sems, recv_sems, local_sems = refs[2 * n_t:]
        x, y, c = _place()
        me, sibling = (x, y, c), (x, y, 1 - c)
        chips = [(1 - x, y), (x, 1 - y), (1 - x, 1 - y)]

        def copy(t, k, block, to, src=None):
            dst = outs[t].at[_index(block)]
            return pltpu.make_async_remote_copy(
                src_ref=dst if src is None else src, dst_ref=dst, send_sem=send_sems.at[t, k],
                recv_sem=recv_sems.at[t, k], device_id=to, device_id_type=pl.DeviceIdType.MESH)

        local, sent = [], []
        for t in range(n_t):
            local.append(pltpu.make_async_copy(ins[t], outs[t].at[_index(me)], local_sems.at[t]))
            local[-1].start()
            sent.append(copy(t, 0, me, sibling, src=ins[t]))
            sent += [copy(t, 1 + j, me, (*chip, c), src=ins[t]) for j, chip in enumerate(chips)]
        for cp in sent:
            cp.start()
        for j, chip in enumerate(chips):
            for t in range(n_t):
                copy(t, 1 + j, (*chip, c), me).wait_recv()
                sent.append(copy(t, 4 + j, (*chip, c), sibling))
                sent[-1].start()
        for t in range(n_t):
            copy(t, 0, sibling, me).wait_recv()
            for j, chip in enumerate(chips):
                copy(t, 4 + j, (*chip, 1 - c), me).wait_recv()
        for cp in sent:
            cp.wait_send()
        for cp in local:
            cp.wait()

    any_spec = pl.BlockSpec(memory_space=pl.ANY)
    return pl.pallas_call(
        body, name=name, in_specs=[any_spec] * n_t, out_specs=[any_spec] * n_t,
        out_shape=[SDS((N_DEV,) + a.shape, a.dtype) for a in tensors],
        scratch_shapes=[pltpu.SemaphoreType.DMA((n_t, N_DEV - 1)), pltpu.SemaphoreType.DMA((n_t, N_DEV - 1)),
                        pltpu.SemaphoreType.DMA((n_t,))],
    )(*tensors)


def _exchange_start(tensors, name, gather=False, after=None):
    n_t = len(tensors)

    def body(*refs):
        ins, lands = refs[:n_t], refs[n_t:2 * n_t]
        send_sem, recv_sem = refs[2 * n_t + 1:2 * n_t + 3]
        token = refs[-1]
        me = _place()
        mine = _index(me)
        for t in range(n_t):
            for k in range(1, N_DEV):
                peer = _flip(me, k)
                pltpu.make_async_remote_copy(
                    src_ref=ins[t] if gather else ins[t].at[_index(peer)], dst_ref=lands[t].at[mine],
                    send_sem=send_sem, recv_sem=recv_sem, device_id=peer, device_id_type=pl.DeviceIdType.MESH).start()
        token[...] = jnp.zeros_like(token)

    hbm = pl.BlockSpec(memory_space=pltpu.HBM)
    sem = pl.BlockSpec(memory_space=pltpu.SEMAPHORE)
    one = pltpu.SemaphoreType.DMA(())
    land_shape = lambda a: ((N_DEV,) + a.shape) if gather else a.shape
    bufs = ([pltpu.HBM(a.shape, a.dtype) for a in tensors] + [pltpu.HBM(land_shape(a), a.dtype) for a in tensors])
    after = jnp.zeros((8, 128), F32) if after is None else after
    outs = pl.pallas_call(
        body, name=name, in_specs=[hbm] * (2 * n_t) + [pl.BlockSpec(memory_space=pl.ANY)],
        out_specs=[sem, sem] + [hbm] * (2 * n_t) + [pl.BlockSpec(memory_space=pltpu.VMEM)],
        out_shape=[one, one] + bufs + [SDS((8, 128), F32)],
        input_output_aliases={i: 2 + i for i in range(2 * n_t)},
        compiler_params=pltpu.CompilerParams(has_side_effects=pltpu.SideEffectType.DATAFLOW_SIDE_EFFECTING),
    )(*[pltpu.with_memory_space_constraint(a, pltpu.HBM) for a in tensors],
      *[pltpu.with_memory_space_constraint(lax.empty(land_shape(a), a.dtype), pltpu.HBM) for a in tensors], after)
    return outs[:-1], outs[-1][0, 0]


def _exchange_wait(state, after, name, gather=False):
    n_t = (len(state) - 2) // 2

    def body(*refs):
        send_sem, recv_sem = refs[0:2]
        ins, lands = refs[2:2 + n_t], refs[2 + n_t:2 + 2 * n_t]
        me = _place()
        for t in range(n_t):
            for k in range(1, N_DEV):
                peer = _flip(me, k)
                copy = pltpu.make_async_remote_copy(
                    src_ref=ins[t] if gather else ins[t].at[_index(peer)], dst_ref=lands[t].at[_index(peer)],
                    send_sem=send_sem, recv_sem=recv_sem, device_id=peer, device_id_type=pl.DeviceIdType.MESH)
                copy.wait_send()
                copy.wait_recv()

    hbm = pl.BlockSpec(memory_space=pltpu.HBM)
    sem = pl.BlockSpec(memory_space=pltpu.SEMAPHORE)
    bufs = [pltpu.HBM(a.shape, a.dtype) for a in state[2:]]
    outs = pl.pallas_call(
        body, name=name, in_specs=[sem, sem] + [hbm] * (2 * n_t) + [pl.BlockSpec(memory_space=pl.ANY)],
        out_specs=[hbm] * (2 * n_t), out_shape=bufs,
        input_output_aliases={2 + i: i for i in range(2 * n_t)},
        compiler_params=pltpu.CompilerParams(has_side_effects=pltpu.SideEffectType.DATAFLOW_SIDE_EFFECTING),
    )(*state, after)
    return outs[n_t:]


def _sum_slots(parts, name):
    n, rows, w = parts.shape
    tb = 8

    def body(p_ref, o_ref):
        acc = p_ref[0].astype(F32)
        for s in range(1, n):
            acc = acc + p_ref[s].astype(F32)
        o_ref[...] = acc

    return pl.pallas_call(
        body, name=name, grid=(rows // tb,),
        in_specs=[pl.BlockSpec((n, tb, w), lambda i: (0, i, 0))], out_specs=pl.BlockSpec((tb, w), lambda i: (i, 0)),
        out_shape=SDS((rows, w), F32), compiler_params=_params(("parallel",)),
    )(parts)


def _adamw(wt, m, v, parts, name, own=None):
    shape = wt.shape
    parts = parts if isinstance(parts, (list, tuple)) else [parts]
    n, w = parts[0].shape[0], shape[-1]
    rows = math.prod(shape[:-1])
    per = rows // len(parts)
    step = 16 if parts[0].dtype == BF16 else 8
    tb = max([t for t in range(step, 257, step) if per % t == 0] or [per])
    nb = per // tb
    c1 = 1.0 / (1.0 - ADAM_B1 ** ADAM_STEP)
    c2 = 1.0 / (1.0 - ADAM_B2 ** ADAM_STEP)
    state = [a.reshape(rows, w) for a in (wt, m, v)]
    n_in = 4 if own is None else 5
    outs = None
    for l in reversed(range(len(parts))):
        def body(*refs):
            idx_ref = None if own is None else refs[0]
            w_ref, m_ref, v_ref, p_ref = refs[n_in - 4:n_in] if own is None else refs[1:5]
            g_out, d_out, m_out, v_out = refs[-4:]
            g = None
            for s in range(n):
                term = p_ref[s] if own is None else jnp.where(idx_ref[0] == s, refs[5][0], p_ref[s])
                g = term.astype(F32) if g is None else g + term.astype(F32)
            m_new = ADAM_B1 * m_ref[...] + (1.0 - ADAM_B1) * g
            v_new = ADAM_B2 * v_ref[...] + (1.0 - ADAM_B2) * (g * g)
            g_out[...] = g
            m_out[...] = m_new
            v_out[...] = v_new
            d_out[...] = -ADAM_LR * ((m_new * c1) / (jnp.sqrt(v_new * c2) + ADAM_EPS) + ADAM_WD * w_ref[...])

        row = pl.BlockSpec((tb, w), lambda i, *_, l=l: (l * nb + i, 0))
        in_specs = [row, row, row, pl.BlockSpec((n, tb, w), lambda i, *_: (0, i, 0))]
        args = [*state, parts[l].reshape(n, per, w)]
        if own is not None:
            in_specs.append(pl.BlockSpec((1, tb, w), lambda i, idx: (idx[0], i, 0)))
            args.append(own[l].reshape(n, per, w))
        prev = [] if outs is None else list(outs)
        in_specs += [pl.BlockSpec(memory_space=pl.ANY)] * len(prev)
        n_pre = 0 if own is None else 1
        call = dict(name=f"{name}_{l}", out_shape=[SDS((rows, w), F32)] * 4,
                    input_output_aliases={n_pre + len(args) + k: k for k in range(len(prev))},
                    compiler_params=_params(("parallel",)))
        if own is None:
            outs = pl.pallas_call(body, grid=(nb,), in_specs=in_specs, out_specs=[row] * 4, **call)(*args, *prev)
        else:
            spec = pltpu.PrefetchScalarGridSpec(num_scalar_prefetch=1, grid=(nb,), in_specs=in_specs, out_specs=[row] * 4)
            idx = jnp.reshape(_index(_place()), (1,)).astype(jnp.int32)
            outs = pl.pallas_call(body, grid_spec=spec, **call)(idx, *args, *prev)
    return [o.reshape(shape) for o in outs]


_BIG = [("w_in", 2), ("mla_w_q_up", 2), ("mla_w_kv_up", 2), ("w_branch", 3), ("w_o", 1), ("ffn_w_up", 2), ("ffn_w_down", 1)]
_SMALL_SHARDED = [("meta_tokens", 1), ("ffn_conv_w", 2)]
_REPLICATED = ["norm1_g", "fox_forget_b", "fox_q_g", "fox_k_g", "mla_q_a_g", "mla_kv_a_g", "mla_q_g", "mla_k_g",
               "swa_q_g", "swa_k_g", "swa_sinks", "norm2_g", "ffn_conv_b"]
_ORDER = ["meta_tokens", "norm1_g", "w_in", "fox_forget_b", "fox_q_g", "fox_k_g", "mla_q_a_g", "mla_w_q_up",
          "mla_kv_a_g", "mla_w_kv_up", "mla_q_g", "mla_k_g", "swa_q_g", "swa_k_g", "swa_sinks", "w_branch", "w_o",
          "norm2_g", "ffn_w_up", "ffn_conv_w", "ffn_conv_b", "ffn_w_down"]


def _flat_rows(vecs, dtype, row_mult):
    flat = jnp.concatenate([a.reshape(-1).astype(dtype) for a in vecs])
    rows = -(-flat.shape[0] // (1024 * row_mult)) * row_mult
    return jnp.pad(flat, (0, rows * 1024 - flat.shape[0])).reshape(rows, 1024)


def _unflatten(flat, shapes):
    out, off = [], 0
    for s in shapes:
        n = math.prod(s)
        out.append(flat[off:off + n].reshape(s))
        off += n
    return out


def _to_full(blocks, axis):
    moved = jnp.moveaxis(blocks, 0, axis)
    s = moved.shape
    return moved.reshape(s[:axis] + (s[axis] * s[axis + 1],) + s[axis + 2:])


def _to_blocks(full, axis):
    s = full.shape
    split = full.reshape(s[:axis] + (N_DEV, s[axis] // N_DEV) + s[axis + 1:])
    return jnp.moveaxis(split, axis, 0)


def kernel(x, meta_tokens, norm1_g, w_in, fox_forget_b, fox_q_g, fox_k_g, mla_q_a_g, mla_w_q_up, mla_kv_a_g, mla_w_kv_up, mla_q_g, mla_k_g, swa_q_g, swa_k_g, swa_sinks, w_branch, w_o, norm2_g, ffn_w_up, ffn_conv_w, ffn_conv_b, ffn_w_down, loss_target, m_meta_tokens, m_norm1_g, m_w_in, m_fox_forget_b, m_fox_q_g, m_fox_k_g, m_mla_q_a_g, m_mla_w_q_up, m_mla_kv_a_g, m_mla_w_kv_up, m_mla_q_g, m_mla_k_g, m_swa_q_g, m_swa_k_g, m_swa_sinks, m_w_branch, m_w_o, m_norm2_g, m_ffn_w_up, m_ffn_conv_w, m_ffn_conv_b, m_ffn_w_down, v_meta_tokens, v_norm1_g, v_w_in, v_fox_forget_b, v_fox_q_g, v_fox_k_g, v_mla_q_a_g, v_mla_w_q_up, v_mla_kv_a_g, v_mla_w_kv_up, v_mla_q_g, v_mla_k_g, v_swa_q_g, v_swa_k_g, v_swa_sinks, v_w_branch, v_w_o, v_norm2_g, v_ffn_w_up, v_ffn_conv_w, v_ffn_conv_b, v_ffn_w_down):
    wl = dict(zip(_ORDER, (meta_tokens, norm1_g, w_in, fox_forget_b, fox_q_g, fox_k_g, mla_q_a_g, mla_w_q_up,
                           mla_kv_a_g, mla_w_kv_up, mla_q_g, mla_k_g, swa_q_g, swa_k_g, swa_sinks, w_branch, w_o,
                           norm2_g, ffn_w_up, ffn_conv_w, ffn_conv_b, ffn_w_down)))
    ml = dict(zip(_ORDER, (m_meta_tokens, m_norm1_g, m_w_in, m_fox_forget_b, m_fox_q_g, m_fox_k_g, m_mla_q_a_g,
                           m_mla_w_q_up, m_mla_kv_a_g, m_mla_w_kv_up, m_mla_q_g, m_mla_k_g, m_swa_q_g, m_swa_k_g,
                           m_swa_sinks, m_w_branch, m_w_o, m_norm2_g, m_ffn_w_up, m_ffn_conv_w, m_ffn_conv_b,
                           m_ffn_w_down)))
    vl = dict(zip(_ORDER, (v_meta_tokens, v_norm1_g, v_w_in, v_fox_forget_b, v_fox_q_g, v_fox_k_g, v_mla_q_a_g,
                           v_mla_w_q_up, v_mla_kv_a_g, v_mla_w_kv_up, v_mla_q_g, v_mla_k_g, v_swa_q_g, v_swa_k_g,
                           v_swa_sinks, v_w_branch, v_w_o, v_norm2_g, v_ffn_w_up, v_ffn_conv_w, v_ffn_conv_b,
                           v_ffn_w_down)))
    small_sh = [n for n, _ in _SMALL_SHARDED]
    big = [n for n, _ in _BIG]
    axis_of = dict(_BIG)
    idx = _index(_place())

    def to_full(n, blocks, own=None):
        if own is not None:
            sel = (jnp.arange(N_DEV) == idx).reshape((N_DEV,) + (1,) * own.ndim)
            blocks = jnp.where(sel, own[None], blocks)
        return _to_full(blocks, axis_of[n] - 1)

    local = {(n, l): wl[n][l].astype(BF16) for n in big for l in range(DEPTH)}
    got = _gather([local[(n, 0)] for n in _MIX_BIG] + [wl[n] for n in small_sh], "gather_weights_l0_mix")
    full = {n: wl[n] for n in _REPLICATED}
    for (n, axis), blocks in zip(_SMALL_SHARDED, got[len(_MIX_BIG):]):
        full[n] = _to_full(blocks, axis)
    ready = {(n, 0): to_full(n, blocks) for n, blocks in zip(_MIX_BIG, got)}
    later = {"l0_ffn": [(n, 0) for n in _FFN_BIG], "l1": [(n, 1) for n in big]}
    states = {}
    for key, names in later.items():
        states[key], tick = _exchange_start([local[e] for e in names], "gather_weights_" + key + "_start", True, got[0])
        full["norm1_g"] = full["norm1_g"] + tick

    def fetch(l, stage, after):
        key = "l0_ffn" if l == 0 else "l1"
        if (l, stage) != (0, "mix") and key in states:
            lands = _exchange_wait(states.pop(key), after, "gather_weights_" + key + "_wait", True)
            ready.update({e: to_full(e[0], blocks, local[e]) for e, blocks in zip(later[key], lands)})
        return {n: ready[(n, l)] for n in (_MIX_BIG if stage == "mix" else _FFN_BIG)}

    blocks_of = lambda g, names: [_to_blocks(g[n], axis_of[n] - 1).astype(BF16) for n in names]
    early = {}

    def hook(l, stage, g):
        if l == DEPTH - 1 and stage == "mix":
            key, names = "l1", big
        elif l == 0:
            merge = ("w_branch", "w_o")
            groups = {"ffn": _FFN_BIG, "merge": merge, "mix": tuple(n for n in _MIX_BIG if n not in merge)}
            key, names = "l0_" + stage, groups[stage]
        else:
            return None
        sends = blocks_of(g, names)
        state, tick = _exchange_start(sends, "exchange_grads_" + key + "_start")
        early[key] = (names, l, sends, state)
        return tick

    loss, grad_x, grad_meta, grads = _local_step(x[0], loss_target[0], full, hook, fetch)
    result = {kind: {} for kind in ("grad", "delta", "new_m", "new_v")}
    small_grads = {k: jnp.stack([grads[l][k] for l in range(DEPTH)]) for k in grads[0] if k not in big}
    small_grads["meta_tokens"] = grad_meta
    small_full = _REPLICATED + small_sh
    mine_small = _flat_rows([small_grads[n] for n in small_full] + [loss], F32, 8)
    small_state, _ = _exchange_start([mine_small], "gather_small_grads_start", True)
    landed, sent = {}, {}
    after = early["l0_mix"][2][0]
    for key in ("l1", "l0_ffn", "l0_merge"):
        names, l, sends, state = early[key]
        got = _exchange_wait(state, after, "exchange_grads_" + key + "_wait")
        landed.update({(n, l): p for n, p in zip(names, got)})
        sent.update({(n, l): p for n, p in zip(names, sends)})

    def update(names):
        for n in names:
            outs = _adamw(wl[n], ml[n], vl[n], [landed[(n, l)] for l in range(DEPTH)], "adamw_" + n,
                          [sent[(n, l)] for l in range(DEPTH)])
            for kind, val in zip(result, outs):
                result[kind][n] = val

    update(_FFN_BIG)
    done = result["delta"]["ffn_w_up"][0, 0, :8] + result["delta"]["ffn_w_down"][0, 0, :8]
    got_small = _exchange_wait(small_state, done, "gather_small_grads_wait", True)[0]
    sel = (jnp.arange(N_DEV) == idx).reshape(N_DEV, 1, 1)
    total_small = _sum_slots(jnp.where(sel, mine_small[None], got_small), "sum_small_grads").reshape(-1)
    pieces = _unflatten(total_small, [small_grads[n].shape for n in small_full] + [()])
    loss_total = pieces[-1]
    g_small = dict(zip(small_full, pieces[:-1]))
    for n, axis in _SMALL_SHARDED:
        size = wl[n].shape[axis]
        g_small[n] = lax.dynamic_slice_in_dim(g_small[n], idx * size, size, axis)
    flat = lambda d: _flat_rows([d[n] for n in small_full], F32, 8)
    small_out = _adamw(flat(wl), flat(ml), flat(vl), flat(g_small)[None], "adamw_small")
    for kind, fs in zip(result, small_out):
        result[kind].update(zip(small_full, _unflatten(fs.reshape(-1), [wl[n].shape for n in small_full])))
    names, l, sends, state = early["l0_mix"]
    got = _exchange_wait(state, small_out[0], "exchange_grads_l0_mix_wait")
    landed.update({(n, l): p for n, p in zip(names, got)})
    sent.update({(n, l): p for n, p in zip(names, sends)})
    update(_MIX_BIG)
    outs = [loss_total, grad_x[None]]
    for kind in ("grad", "delta", "new_m", "new_v"):
        outs += [result[kind][n] for n in _ORDER]
    return tuple(outs)
```

```python
import functools
import math

import numpy as np
import jax
import jax.numpy as jnp
from jax import lax
from jax.experimental import pallas as pl
from jax.experimental.pallas import tpu as pltpu

F32, BF16 = jnp.float32, jnp.bfloat16
SDS = jax.ShapeDtypeStruct

D_MODEL = 1024
N_META = 16
EPS = 1e-6
WINDOW = 128
ROPE_THETA = 10000.0
HEADS = 8
D_FF = 2816
DEPTH = 2
N_DEV = 8
ADAM_LR, ADAM_B1, ADAM_B2, ADAM_EPS, ADAM_WD, ADAM_STEP = 0.001, 0.9, 0.999, 1e-08, 0.01, 10

ROW_ALIGN = 384
TILE_MM = 384
TILE_ROW = 192
TILE_CONV_BWD = 128
TILE_ATT = 384
TILE_POST = 384
PAIRS = 2
VMEM_LIMIT = 56 * 1024 * 1024

GATES_W = 3072
OTHER_W = 2816
IN_W = GATES_W + OTHER_W
O_FQ, O_FK, O_FV, O_SQ, O_SK, O_SV, O_CQ, O_CKV, O_MISC = 0, 512, 1024, 1536, 2048, 2176, 2304, 2560, 2688
FF_LANE = 32

NEG = -1e30


def _dot(a, b):
    return jnp.dot(a, b, preferred_element_type=F32)


def _dot_nt(a, b):
    return lax.dot_general(a, b, (((1,), (1,)), ((), ())), preferred_element_type=F32)


def _dot_tn(a, b):
    return lax.dot_general(a, b, (((0,), (0,)), ((), ())), preferred_element_type=F32)


def _params(sem):
    return pltpu.CompilerParams(dimension_semantics=sem, vmem_limit_bytes=VMEM_LIMIT)


def _rms(x, g):
    return x * lax.rsqrt(jnp.mean(x * x, axis=-1, keepdims=True) + EPS) * g


def _split_dot(x, m, pieces=2):
    acc, rest = None, x
    for _ in range(pieces):
        part = rest.astype(BF16)
        rest = rest - part.astype(F32)
        acc = _dot(part, m) if acc is None else acc + _dot(part, m)
    return acc


@jax.custom_vjp
def _sel(x, m, mt):
    return _split_dot(x, m)


_sel.defvjp(lambda x, m, mt: (_split_dot(x, m), (m, mt)), lambda res, dy: (_split_dot(dy, res[1]), None, None))


@jax.custom_vjp
def _mm(x, w):
    return _dot(x.astype(BF16), w.astype(BF16))


def _mm_bwd(res, dy):
    x, w = res
    dyb = dy.astype(BF16)
    return _dot_nt(dyb, w.astype(BF16)), _dot_tn(x.astype(BF16), dyb)


_mm.defvjp(lambda x, w: (_mm(x, w), (x, w)), _mm_bwd)


def _rot_impl(x):
    w = x.shape[1]
    lane = lax.broadcasted_iota(jnp.int32, x.shape, 1) % 128
    lo = (lane >= 64) & (lane < 80)
    hi = (lane >= 80) & (lane < 96)
    return jnp.where(hi, pltpu.roll(x, 16, 1), 0.0) - jnp.where(lo, pltpu.roll(x, w - 16, 1), 0.0)


@jax.custom_vjp
def _rot(x):
    return _rot_impl(x)


_rot.defvjp(lambda x: (_rot_impl(x), None), lambda _, dy: (-_rot_impl(dy),))


def _gnorm(x, g, e, et, dim):
    inv = lax.rsqrt(_sel(x * x, e, et) * (1.0 / dim) + EPS)
    return x * _sel(inv, et, e) * g


def _indicator(width, period):
    m = np.zeros((width, 128), np.float32)
    m[np.arange(width), np.arange(width) // period] = 1.0
    return m


def _consts():
    e64 = _indicator(512, 64)
    e128 = _indicator(1024, 128)
    sk = np.zeros((128, 1024), np.float32)
    for h in range(HEADS):
        sk[np.arange(32), 128 * h + 64 + np.arange(32)] = 1.0
    dup = np.zeros((128, 256), np.float32)
    for g in range(2):
        for r in range(2):
            dup[64 * g + np.arange(64), 128 * g + 64 * r + np.arange(64)] = 1.0
    mats = [e64, e64.T, e128, e128.T, sk, sk.T, dup, dup.T]
    return [jnp.asarray(m, BF16) for m in mats]


def _fold_matrix(width, period):
    m = np.zeros((width, 128), np.float32)
    m[np.arange(width), np.arange(width) % period] = 1.0
    return jnp.asarray(m, BF16)


def _rope_tables(lp):
    half = 16
    freqs = ROPE_THETA ** (-np.arange(half, dtype=np.float32) / half)
    ang = np.arange(lp, dtype=np.float32)[:, None] * freqs[None, :]
    cos = np.ones((lp, 128), np.float32)
    sin = np.zeros((lp, 128), np.float32)
    cos[:, 64:80] = np.cos(ang)
    cos[:, 80:96] = np.cos(ang)
    sin[:, 64:80] = np.sin(ang)
    sin[:, 80:96] = np.sin(ang)
    return jnp.asarray(cos), jnp.asarray(sin)


def _norm_matmul(h, g, w, tn, name):
    lp, d = h.shape
    n = w.shape[1]
    tb = TILE_MM

    def body(h_ref, g_ref, w_ref, xn_ref, y_ref):
        @pl.when(pl.program_id(1) == 0)
        def _():
            xn_ref[...] = _rms(h_ref[...], g_ref[...]).astype(BF16)

        y_ref[...] = _dot(xn_ref[...], w_ref[...])

    return pl.pallas_call(
        body, name=name, grid=(lp // tb, n // tn),
        in_specs=[pl.BlockSpec((tb, d), lambda i, j: (i, 0)), pl.BlockSpec((1, d), lambda i, j: (0, 0)),
                  pl.BlockSpec((d, tn), lambda i, j: (0, j))],
        out_specs=[pl.BlockSpec((tb, d), lambda i, j: (i, 0)), pl.BlockSpec((tb, tn), lambda i, j: (i, j))],
        out_shape=[SDS((lp, d), BF16), SDS((lp, n), F32)],
        compiler_params=_params(("parallel", "arbitrary")),
    )(h, g, w)


def _matmul_residual(a, w, res, name):
    m, k = a.shape
    n = w.shape[1]
    tb = TILE_MM

    def body(a_ref, w_ref, r_ref, o_ref):
        o_ref[...] = r_ref[...] + _dot(a_ref[...], w_ref[...])

    return pl.pallas_call(
        body, name=name, grid=(m // tb,),
        in_specs=[pl.BlockSpec((tb, k), lambda i: (i, 0)), pl.BlockSpec((k, n), lambda i: (0, 0)),
                  pl.BlockSpec((tb, n), lambda i: (i, 0))],
        out_specs=pl.BlockSpec((tb, n), lambda i: (i, 0)),
        out_shape=SDS((m, n), F32),
        compiler_params=_params(("parallel",)),
    )(a, w, res)


def _matmul_nt(dy, w, tn, name):
    m, k = dy.shape
    n = w.shape[0]
    tb = TILE_MM

    def body(dy_ref, w_ref, o_ref):
        o_ref[...] = _dot_nt(dy_ref[...].astype(BF16), w_ref[...])

    return pl.pallas_call(
        body, name=name, grid=(m // tb, n // tn),
        in_specs=[pl.BlockSpec((tb, k), lambda i, j: (i, 0)), pl.BlockSpec((tn, k), lambda i, j: (j, 0))],
        out_specs=pl.BlockSpec((tb, tn), lambda i, j: (i, j)),
        out_shape=SDS((m, n), F32),
        compiler_params=_params(("parallel", "arbitrary")),
    )(dy, w)


def _matmul_tn(x, dy, tn, name):
    m, k = x.shape
    n = dy.shape[1]
    tb = TILE_MM
    nb = m // tb

    def body(x_ref, dy_ref, o_ref, acc):
        i = pl.program_id(1)

        @pl.when(i == 0)
        def _():
            acc[...] = jnp.zeros_like(acc)

        acc[...] += _dot_tn(x_ref[...].astype(BF16), dy_ref[...].astype(BF16))

        @pl.when(i == nb - 1)
        def _():
            o_ref[...] = acc[...].astype(BF16)

    return pl.pallas_call(
        body, name=name, grid=(n // tn, nb),
        in_specs=[pl.BlockSpec((tb, k), lambda j, i: (i, 0)), pl.BlockSpec((tb, tn), lambda j, i: (i, j))],
        out_specs=pl.BlockSpec((k, tn), lambda j, i: (0, j)),
        out_shape=SDS((k, n), BF16),
        scratch_shapes=[pltpu.VMEM((k, tn), F32)],
        compiler_params=_params(("parallel", "arbitrary")),
    )(x, dy)


def _norm_matmul_bwd(dys, w, x, g, dres, name):
    m, d = x.shape
    tb = TILE_MM
    widths = [a.shape[1] for a in dys]
    n_dy = len(dys)

    def body(*refs):
        w_ref, x_ref, g_ref, r_ref, o_ref, dg_ref = refs[n_dy:]

        @pl.when(pl.program_id(0) == 0)
        def _():
            dg_ref[...] = jnp.zeros_like(dg_ref)

        dxn, off = None, 0
        for dy_ref, width in zip(refs[:n_dy], widths):
            part = _dot_nt(dy_ref[...], w_ref[:, off:off + width])
            dxn = part if dxn is None else dxn + part
            off += width
        _, vjp = jax.vjp(_rms, x_ref[...], g_ref[...])
        dx, dg = vjp(dxn)
        o_ref[...] = r_ref[...] + dx
        dg_ref[...] += dg

    row = pl.BlockSpec((tb, d), lambda i: (i, 0))
    vec = pl.BlockSpec((1, d), lambda i: (0, 0))
    return pl.pallas_call(
        body, name=name, grid=(m // tb,),
        in_specs=[pl.BlockSpec((tb, wd), lambda i: (i, 0)) for wd in widths]
        + [pl.BlockSpec(w.shape, lambda i: (0, 0)), row, vec, row],
        out_specs=[row, vec],
        out_shape=[SDS((m, d), F32), SDS((1, d), F32)],
        compiler_params=_params(("arbitrary",)),
    )(*dys, w, x, g, dres)


def _prep_math(pieces, prm, consts, cos, sin):
    fq, fk, sq, sk, sv, cq, ckv, misc = pieces
    gfq, gfk, gsq, gsk, fb, gqa, gkva, gmq, gmk, wq, wkk, wkv = prm
    e64, e64t, e128, e128t, skm, skt, dup, dupt = consts
    cos8 = jnp.concatenate([cos] * HEADS, axis=1)
    sin8 = jnp.concatenate([sin] * HEADS, axis=1)
    fq_n = _gnorm(fq, gfq, e64, e64t, 64)
    fk_n = _gnorm(fk, gfk, e64, e64t, 64)
    ls = jax.nn.log_sigmoid(misc + fb)
    q = _gnorm(_mm(_rms(cq, gqa), wq), gmq, e128, e128t, 96)
    mq = q * cos8 + _rot(q) * sin8
    kva = _rms(ckv, gkva)
    k = _gnorm(_mm(kva, wkk) + _sel(misc, skm, skt), gmk, e128, e128t, 96)
    mk = k * cos8 + _rot(k) * sin8
    mv = _mm(kva, wkv)
    sq_n = _gnorm(sq, gsq, e64, e64t, 64)
    sk_n = _gnorm(sk, gsk, e64[0:128], e64t[:, 0:128], 64)
    skd = _sel(sk_n, dup, dupt)
    svd = _sel(sv, dup, dupt)
    return fq_n, fk_n, ls, mq, mk, mv, sq_n, skd, svd


_PIECES = [(O_FQ, 512), (O_FK, 512), (O_SQ, 512), (O_SK, 128), (O_SV, 128), (O_CQ, 256), (O_CKV, 128), (O_MISC, 128)]
_PRM_SHAPES = [(1, 512), (1, 512), (1, 512), (1, 128), (1, 128), (1, 256), (1, 128), (1, 1024), (1, 1024),
               (256, 1024), (128, 1024), (128, 512)]
_CONST_SHAPES = [(512, 128), (128, 512), (1024, 128), (128, 1024), (128, 1024), (1024, 128), (128, 256), (256, 128)]


def _piece_specs(tb):
    def spec(off, width):
        blk = (GATES_W + off) // width
        return pl.BlockSpec((tb, width), lambda i, blk=blk: (i, blk))
    return [spec(o, w) for o, w in _PIECES] + [spec(O_FV, 512)]


def _full_specs(shapes):
    return [pl.BlockSpec(s, lambda i: (0, 0)) for s in shapes]


def _prep_fwd(proj, prm, consts, cos, sin, name):
    lp = proj.shape[0]
    tb = TILE_ROW
    row = lambda w: pl.BlockSpec((tb, w), lambda i: (i, 0))

    def body(*refs):
        pieces = [r[...] for r in refs[0:8]]
        fv = refs[8][...]
        prm_v = [r[...] for r in refs[9:21]]
        consts_v = [r[...] for r in refs[21:29]]
        cos_v, sin_v = refs[29][...], refs[30][...]
        outs = refs[31:]
        fq_n, fk_n, ls, mq, mk, mv, sq_n, skd, svd = _prep_math(pieces, prm_v, consts_v, cos_v, sin_v)
        for ref, val in zip(outs, (fq_n, fk_n, fv, mq, mk, mv, sq_n, skd, svd)):
            ref[...] = val.astype(BF16)
        outs[9][...] = ls

    widths = [512, 512, 512, 1024, 1024, 512, 512, 256, 256]
    return pl.pallas_call(
        body, name=name, grid=(lp // tb,),
        in_specs=_piece_specs(tb) + _full_specs(_PRM_SHAPES) + _full_specs(_CONST_SHAPES) + [row(128), row(128)],
        out_specs=[row(w) for w in widths] + [row(128)],
        out_shape=[SDS((lp, w), BF16) for w in widths] + [SDS((lp, 128), F32)],
        compiler_params=_params(("parallel",)),
    )(*([proj] * 9), *prm, *consts, cos, sin)


def _prep_bwd(proj, prm, consts, cos, sin, cots, folds, name):
    lp = proj.shape[0]
    tb = TILE_ROW
    row = lambda w: pl.BlockSpec((tb, w), lambda i: (i, 0))
    fold64, fold128 = folds

    def body(*refs):
        pieces = [r[...] for r in refs[0:8]]
        prm_v = [r[...] for r in refs[9:21]]
        consts_v = [r[...] for r in refs[21:29]]
        cos_v, sin_v = refs[29][...], refs[30][...]
        dfq, dfk, dfv, dmq, dmk, dmv, dsq, dskd, dsvd, dls = [r[...] for r in refs[31:41]]
        f64, f128 = refs[41][...], refs[42][...]
        d_ref = refs[43]
        g_refs = refs[44:]

        @pl.when(pl.program_id(0) == 0)
        def _():
            for r in g_refs:
                r[...] = jnp.zeros_like(r)

        f = lambda pc, pr: _prep_math(pc, pr, consts_v, cos_v, sin_v)
        _, vjp = jax.vjp(f, pieces, prm_v)
        dpc, dprm = vjp((dfq, dfk, dls, dmq, dmk, dmv, dsq, dskd, dsvd))
        d_fq, d_fk, d_sq, d_sk, d_sv, d_cq, d_ckv, d_misc = dpc
        for off, val in ((O_FQ, d_fq), (O_FK, d_fk), (O_FV, dfv), (O_SQ, d_sq), (O_SK, d_sk), (O_SV, d_sv),
                         (O_CQ, d_cq), (O_CKV, d_ckv), (O_MISC, d_misc)):
            d_ref[:, off:off + val.shape[1]] = val.astype(BF16)
        folded = {0: f64, 1: f64, 2: f64, 3: f64[0:128], 7: f128, 8: f128}
        for idx, (ref, val) in enumerate(zip(g_refs, dprm)):
            if idx in folded:
                ref[...] += _split_dot(jnp.broadcast_to(val, (8, val.shape[1])), folded[idx], 3)
            elif val.shape[0] == 1:
                ref[...] += jnp.broadcast_to(val, ref.shape)
            else:
                ref[...] += val

    g_shapes = [(8, 128), (8, 128), (8, 128), (8, 128), (8, 128), (8, 256), (8, 128), (8, 128), (8, 128),
                (256, 1024), (128, 1024), (128, 512)]
    cot_widths = [512, 512, 512, 1024, 1024, 512, 512, 256, 256, 128]
    return pl.pallas_call(
        body, name=name, grid=(lp // tb,),
        in_specs=(_piece_specs(tb) + _full_specs(_PRM_SHAPES) + _full_specs(_CONST_SHAPES) + [row(128), row(128)]
                  + [row(w) for w in cot_widths] + _full_specs([(512, 128), (1024, 128)])),
        out_specs=[row(OTHER_W)] + _full_specs(g_shapes),
        out_shape=[SDS((lp, OTHER_W), BF16)] + [SDS(s, F32) for s in g_shapes],
        compiler_params=_params(("arbitrary",)),
    )(*([proj] * 9), *prm, *consts, cos, sin, *cots, fold64, fold128)


def _cumsum(xs, reverse, name):
    lp = xs[0].shape[0]
    tb = TILE_MM
    nb = lp // tb
    n_in = len(xs)
    idx = (lambda i: (nb - 1 - i, 0)) if reverse else (lambda i: (i, 0))

    def body(*refs):
        o_ref, carry = refs[n_in], refs[n_in + 1]

        @pl.when(pl.program_id(0) == 0)
        def _():
            carry[...] = jnp.zeros_like(carry)

        x = refs[0][...]
        for r in refs[1:n_in]:
            x = x + r[...]
        r_i = lax.broadcasted_iota(jnp.int32, (tb, tb), 0)
        c_i = lax.broadcasted_iota(jnp.int32, (tb, tb), 1)
        tri = ((c_i >= r_i) if reverse else (c_i <= r_i)).astype(BF16)
        acc, rest = None, x
        for _ in range(3):
            part = rest.astype(BF16)
            rest = rest - part.astype(F32)
            acc = _dot(tri, part) if acc is None else acc + _dot(tri, part)
        o_ref[...] = acc + carry[...]
        carry[...] += jnp.sum(x, axis=0, keepdims=True)

    return pl.pallas_call(
        body, name=name, grid=(nb,),
        in_specs=[pl.BlockSpec((tb, 128), idx)] * n_in,
        out_specs=pl.BlockSpec((tb, 128), idx),
        out_shape=SDS((lp, 128), F32),
        scratch_shapes=[pltpu.VMEM((1, 128), F32)],
        compiler_params=_params(("arbitrary",)),
    )(*xs)


class _Att:
    def __init__(self, mode):
        self.mode = mode
        self.wide = mode == "mla"
        self.qw = 256 if self.wide else 128
        self.scale = (96 if mode == "mla" else 64) ** -0.5

    def resident(self, x, lo, scaled):
        if self.wide:
            return x[:, 0:128], x[:, 128:256]
        if scaled:
            x = x * jnp.asarray(self.scale, x.dtype)
        zero = jnp.zeros_like(x)
        return jnp.where(lo, x, zero), jnp.where(lo, zero, x)

    def moving(self, x):
        return (x[:, 0:128], x[:, 128:256]) if self.wide else (x, x)

    def logits(self, a, b, qpos, kpos, key_decay, slope, masked):
        s = _dot_nt(a, b)
        if self.wide:
            s = s * self.scale
        if self.mode == "fox":
            s = s - key_decay
        if self.mode == "swa":
            s = s - slope * (qpos - kpos).astype(F32)
        if masked:
            ok = kpos <= qpos
            if self.mode == "swa":
                ok = ok & ((kpos < N_META) | (qpos - kpos < WINDOW))
            s = jnp.where(ok, s, NEG)
        return s


def _as_rows(col):
    return jnp.broadcast_to(col, (col.shape[0], 128)).T[0:8, :]


def _halves(x, lo):
    zero = jnp.zeros_like(x)
    return jnp.where(lo, x, zero), jnp.where(lo, zero, x)


def _kv_specs(att, lp, rows):
    if att.mode == "swa":
        return (pl.BlockSpec((rows, 128), lambda g, i: (i if rows != lp else 0, g)),) * 2
    return (pl.BlockSpec((rows, PAIRS * att.qw), lambda g, i: (i if rows != lp else 0, g)),
            pl.BlockSpec((rows, PAIRS * 128), lambda g, i: (i if rows != lp else 0, g)))


def _pair_cols(att, x, pp, width):
    return x if x.shape[1] == width else x[:, pp * width:(pp + 1) * width]


def _att_fwd(att, q, k, v, extra, name):
    lp = q.shape[0]
    t = TILE_ATT
    nq = lp // t
    qw = att.qw
    mode = att.mode
    nh = 2 * PAIRS

    def body(*refs):
        q_ref, k_ref, v_ref = refs[0:3]
        o_ref, lse_ref = refs[-2:]
        g, qi = pl.program_id(0), pl.program_id(1)
        lo = lax.broadcasted_iota(jnp.int32, (1, 128), 1) < 64
        q_all = q_ref[...]
        q_heads = [h for pp in range(PAIRS) for h in att.resident(_pair_cols(att, q_all, pp, qw), lo, True)]
        qpos = qi * t + lax.broadcasted_iota(jnp.int32, (t, 1), 0)

        def step(first, cols, carry, masked):
            ks = pl.multiple_of(first, 128)
            kc, vc = k_ref[pl.ds(ks, cols), :], v_ref[pl.ds(ks, cols), :]
            kpos = first + lax.broadcasted_iota(jnp.int32, (1, cols), 1)
            out = []
            for h in range(nh):
                pp = h // 2
                m, l, acc = carry[3 * h:3 * h + 3]
                k_h = att.moving(_pair_cols(att, kc, pp, qw))[h % 2]
                decay = refs[3][h, :, pl.ds(ks, cols)] if mode == "fox" else None
                slope = refs[4][nh * g + h] if mode == "swa" else None
                s = att.logits(q_heads[h], k_h, qpos, kpos, decay, slope, masked)
                m_new = jnp.maximum(m, jnp.max(s, axis=-1, keepdims=True))
                alpha = jnp.exp(m - m_new)
                pe = jnp.exp(s - m_new)
                l = alpha * l + jnp.sum(pe, axis=-1, keepdims=True)
                acc = alpha * acc + _dot(pe.astype(BF16), _pair_cols(att, vc, pp, 128))
                out += [m_new, l, acc]
            return tuple(out)

        init = []
        for h in range(nh):
            if mode == "swa":
                init += [jnp.full((t, 1), refs[3][nh * g + h], F32), jnp.ones((t, 1), F32)]
            else:
                init += [jnp.full((t, 1), NEG, F32), jnp.zeros((t, 1), F32)]
            init.append(jnp.zeros((t, 128), F32))
        if mode == "swa":
            band = jnp.maximum(qi * t - WINDOW, 0)
            carry = lax.fori_loop(0, (band >= 128).astype(jnp.int32), lambda j, c: step(0, 128, c, True), tuple(init))
            carry = step(band, t + WINDOW, carry, True)
        else:
            carry = lax.fori_loop(0, qi // 2, lambda j, c: step(2 * j * t, 2 * t, c, False), tuple(init))
            carry = lax.fori_loop(0, qi % 2, lambda j, c: step((qi - 1) * t, t, c, False), carry)
            carry = step(qi * t, t, carry, True)
        outs = []
        for pp in range(PAIRS):
            (ma, la, acca), (mb, lb, accb) = carry[6 * pp:6 * pp + 3], carry[6 * pp + 3:6 * pp + 6]
            outs.append(jnp.where(lo, acca / la, accb / lb).astype(BF16))
            lse_ref[2 * pp] = ma + jnp.log(la)
            lse_ref[2 * pp + 1] = mb + jnp.log(lb)
        o_ref[...] = jnp.concatenate(outs, axis=1)

    in_specs = [pl.BlockSpec((t, PAIRS * qw), lambda g, i: (i, g)), *_kv_specs(att, lp, lp)]
    if mode == "fox":
        in_specs += [pl.BlockSpec((nh, 1, lp), lambda g, i: (g, 0, 0))]
    if mode == "swa":
        in_specs += [pl.BlockSpec(memory_space=pltpu.SMEM)] * 2
    return pl.pallas_call(
        body, name=name, grid=(4 // PAIRS, nq), in_specs=in_specs,
        out_specs=[pl.BlockSpec((t, PAIRS * 128), lambda g, i: (i, g)), pl.BlockSpec((nh, t, 1), lambda g, i: (g, i, 0))],
        out_shape=[SDS((lp, 512), BF16), SDS((HEADS, lp, 1), F32)],
        compiler_params=_params(("parallel", "arbitrary")),
    )(q, k, v, *extra)


def _att_bwd(att, q, k, v, o, do, lse, extra, name):
    lp = q.shape[0]
    t = TILE_ATT
    nq = lp // t
    qw = att.qw
    mode = att.mode
    nh = 2 * PAIRS
    kw = 128 if mode == "swa" else PAIRS * qw
    vw = 128 if mode == "swa" else PAIRS * 128

    def body(*refs):
        q_ref, k_ref, v_ref, o_ref, do_ref, lse_ref = refs[0:6]
        n_out = {"fox": 5, "mla": 3, "swa": 4}[mode]
        outs = refs[len(refs) - n_out:]
        dq_ref, dk_ref, dv_ref = outs[0:3]
        g, qi = pl.program_id(0), pl.program_id(1)

        @pl.when(qi == 0)
        def _():
            dk_ref[...] = jnp.zeros_like(dk_ref)
            dv_ref[...] = jnp.zeros_like(dv_ref)
            if mode == "fox":
                outs[4][...] = jnp.zeros_like(outs[4])

        lo = lax.broadcasted_iota(jnp.int32, (1, 128), 1) < 64
        q_all, do_all = q_ref[...], do_ref[...]
        prod = do_all.astype(F32) * o_ref[...].astype(F32)
        q_heads, q_plain, do_heads, do_pairs, delta = [], [], [], [], []
        for pp in range(PAIRS):
            q_pp = _pair_cols(att, q_all, pp, qw)
            q_heads += att.resident(q_pp, lo, True)
            q_plain += att.moving(q_pp)
            do_pp = _pair_cols(att, do_all, pp, 128)
            do_pairs.append(do_pp)
            do_heads += _halves(do_pp, lo)
            pr_pp = _pair_cols(att, prod, pp, 128)
            delta += [jnp.sum(jnp.where(lo, pr_pp, 0.0), axis=-1, keepdims=True),
                      jnp.sum(jnp.where(lo, 0.0, pr_pp), axis=-1, keepdims=True)]
        lse_v = [lse_ref[h] for h in range(nh)]
        qpos = qi * t + lax.broadcasted_iota(jnp.int32, (t, 1), 0)

        def step(first, cols, carry, masked):
            ks = pl.multiple_of(first, 128)
            kc, vc = k_ref[pl.ds(ks, cols), :], v_ref[pl.ds(ks, cols), :]
            kpos = first + lax.broadcasted_iota(jnp.int32, (1, cols), 1)
            out, dk_parts, dv_parts = [], [], []
            for h in range(nh):
                pp = h // 2
                k_h = att.moving(_pair_cols(att, kc, pp, qw))[h % 2]
                decay = refs[6][h, :, pl.ds(ks, cols)] if mode == "fox" else None
                slope = refs[7][nh * g + h] if mode == "swa" else None
                s = att.logits(q_heads[h], k_h, qpos, kpos, decay, slope, masked)
                pr = jnp.exp(s - lse_v[h])
                ds = pr * (_dot_nt(do_heads[h], _pair_cols(att, vc, pp, 128)) - delta[h])
                dsb = ds.astype(BF16)
                out.append(carry[2 * h] + _dot(dsb, k_h))
                out.append(carry[2 * h + 1] + jnp.sum(ds, axis=-1, keepdims=True) if mode == "fox" else carry[2 * h + 1])
                dk_parts.append(_dot_tn(dsb, q_plain[h]))
                dv_parts.append(_dot_tn(pr.astype(BF16), do_pairs[pp]))
                if mode == "fox":
                    outs[4][h, 0:1, pl.ds(ks, cols)] -= jnp.sum(ds, axis=0, keepdims=True)
            rows = pl.ds(ks, cols)
            for pp in range(PAIRS):
                dv_pp = jnp.where(lo, dv_parts[2 * pp], dv_parts[2 * pp + 1])
                if att.wide:
                    dk_pp = jnp.concatenate(dk_parts[2 * pp:2 * pp + 2], axis=1) * att.scale
                else:
                    dk_pp = jnp.where(lo, dk_parts[2 * pp], dk_parts[2 * pp + 1]) * att.scale
                if mode == "swa":
                    dk_ref[rows, :] += dk_pp
                    dv_ref[rows, :] += dv_pp
                else:
                    dk_ref[rows, pp * qw:(pp + 1) * qw] += dk_pp
                    dv_ref[rows, pp * 128:(pp + 1) * 128] += dv_pp
            return tuple(out)

        init = (jnp.zeros((t, 128), F32), jnp.zeros((t, 1), F32)) * nh
        if mode == "swa":
            band = jnp.maximum(qi * t - WINDOW, 0)
            carry = lax.fori_loop(0, (band >= 128).astype(jnp.int32), lambda j, c: step(0, 128, c, True), init)
            carry = step(band, t + WINDOW, carry, True)
        else:
            carry = lax.fori_loop(0, qi // 2, lambda j, c: step(2 * j * t, 2 * t, c, False), init)
            carry = lax.fori_loop(0, qi % 2, lambda j, c: step((qi - 1) * t, t, c, False), carry)
            carry = step(qi * t, t, carry, True)
        dq = []
        for pp in range(PAIRS):
            dqa, dca, dqb, dcb = carry[4 * pp:4 * pp + 4]
            dq += [dqa, dqb] if att.wide else [jnp.where(lo, dqa, dqb)]
            if mode == "fox":
                outs[3][2 * pp] = _as_rows(dca)
                outs[3][2 * pp + 1] = _as_rows(dcb)
        dq_ref[...] = jnp.concatenate(dq, axis=1) * att.scale
        if mode == "swa":
            ds_ref = outs[3]

            @pl.when(qi == 0)
            def _():
                ds_ref[...] = jnp.zeros_like(ds_ref)

            lane = lax.broadcasted_iota(jnp.int32, (8, 128), 1)
            acc = jnp.zeros((8, 128), F32)
            for h in range(nh):
                tot = -jnp.sum(jnp.exp(refs[6][nh * g + h] - lse_v[h]) * delta[h])
                acc = acc + jnp.where(lane == h, tot, 0.0)
            ds_ref[0] += acc

    col = pl.BlockSpec((nh, t, 1), lambda g, i: (g, i, 0))
    in_specs = [pl.BlockSpec((t, PAIRS * qw), lambda g, i: (i, g)), *_kv_specs(att, lp, lp),
                pl.BlockSpec((t, PAIRS * 128), lambda g, i: (i, g)), pl.BlockSpec((t, PAIRS * 128), lambda g, i: (i, g)), col]
    out_specs = [pl.BlockSpec((t, PAIRS * qw), lambda g, i: (i, g)), pl.BlockSpec((lp, kw), lambda g, i: (0, g)),
                 pl.BlockSpec((lp, vw), lambda g, i: (0, g))]
    n_groups = 4 // PAIRS
    out_shape = [SDS((lp, 4 * qw), F32), SDS((lp, n_groups * kw), F32), SDS((lp, n_groups * vw), F32)]
    if mode == "fox":
        in_specs += [pl.BlockSpec((nh, 1, lp), lambda g, i: (g, 0, 0))]
        out_specs += [pl.BlockSpec((nh, 8, t), lambda g, i: (g, 0, i)), pl.BlockSpec((nh, 8, lp), lambda g, i: (g, 0, 0))]
        out_shape += [SDS((HEADS, 8, lp), F32)] * 2
    if mode == "swa":
        in_specs += [pl.BlockSpec(memory_space=pltpu.SMEM)] * 2
        out_specs.append(pl.BlockSpec((1, 8, 128), lambda g, i: (g, 0, 0)))
        out_shape.append(SDS((n_groups, 8, 128), F32))
    return pl.pallas_call(
        body, name=name, grid=(n_groups, nq), in_specs=in_specs, out_specs=out_specs, out_shape=out_shape,
        compiler_params=_params(("parallel", "arbitrary")),
    )(q, k, v, o, do, lse, *extra)


def _post_fwd(h, proj, outs, wb, wo, name):
    lp, d = h.shape
    tb = TILE_POST
    row = lambda w: pl.BlockSpec((tb, w), lambda i: (i, 0))

    def body(h_ref, g0, g1, g2, oa, ob, oc, wb_ref, wo_ref, o_ref):
        merged = jnp.zeros((tb, d), F32)
        for n, (g_ref, br) in enumerate(((g0, oa), (g1, ob), (g2, oc))):
            merged = merged + jax.nn.sigmoid(g_ref[...]) * _dot(br[...], wb_ref[n])
        o_ref[...] = h_ref[...] + _dot(merged.astype(BF16), wo_ref[...])

    gate = lambda n: pl.BlockSpec((tb, d), lambda i, n=n: (i, n))
    return pl.pallas_call(
        body, name=name, grid=(lp // tb,),
        in_specs=[row(d), gate(0), gate(1), gate(2), row(512), row(512), row(512),
                  pl.BlockSpec((3, 512, d), lambda i: (0, 0, 0)), pl.BlockSpec((d, d), lambda i: (0, 0))],
        out_specs=row(d), out_shape=SDS((lp, d), F32),
        compiler_params=_params(("parallel",)),
    )(h, proj, proj, proj, *outs, wb, wo)


def _post_bwd(dh, proj, outs, wb, wo, name):
    lp, d = dh.shape
    tb = TILE_POST
    row = lambda w: pl.BlockSpec((tb, w), lambda i: (i, 0))

    def body(dh_ref, g0, g1, g2, oa, ob, oc, wb_ref, wo_ref, dg_ref, doa, dob, doc, dwb_ref, dwo_ref):
        @pl.when(pl.program_id(0) == 0)
        def _():
            dwb_ref[...] = jnp.zeros_like(dwb_ref)
            dwo_ref[...] = jnp.zeros_like(dwo_ref)

        dhb = dh_ref[...].astype(BF16)
        dm = _dot_nt(dhb, wo_ref[...])
        merged = jnp.zeros((tb, d), F32)
        for n, (g_ref, br, do_ref) in enumerate(((g0, oa, doa), (g1, ob, dob), (g2, oc, doc))):
            gate = jax.nn.sigmoid(g_ref[...])
            o_n = br[...]
            y = _dot(o_n, wb_ref[n])
            merged = merged + gate * y
            dy = (dm * gate).astype(BF16)
            dg_ref[:, n * d:(n + 1) * d] = (dm * y * gate * (1.0 - gate)).astype(BF16)
            do_ref[...] = _dot_nt(dy, wb_ref[n]).astype(BF16)
            dwb_ref[n] += _dot_tn(o_n, dy)
        dwo_ref[...] += _dot_tn(merged.astype(BF16), dhb)

    gate = lambda n: pl.BlockSpec((tb, d), lambda i, n=n: (i, n))
    wb_spec = pl.BlockSpec((3, 512, d), lambda i: (0, 0, 0))
    wo_spec = pl.BlockSpec((d, d), lambda i: (0, 0))
    return pl.pallas_call(
        body, name=name, grid=(lp // tb,),
        in_specs=[row(d), gate(0), gate(1), gate(2), row(512), row(512), row(512), wb_spec, wo_spec],
        out_specs=[row(GATES_W), row(512), row(512), row(512), wb_spec, wo_spec],
        out_shape=[SDS((lp, GATES_W), BF16)] + [SDS((lp, 512), BF16)] * 3 + [SDS((3, 512, d), F32), SDS((d, d), F32)],
        compiler_params=_params(("arbitrary",)),
    )(dh, proj, proj, proj, *outs, wb, wo)


def _shift_down(x, halo, n, first):
    rows = lax.broadcasted_iota(jnp.int32, x.shape, 0)
    edge = jnp.concatenate([pltpu.roll(halo, n, 0), jnp.zeros((x.shape[0] - 8, x.shape[1]), F32)], axis=0)
    edge = jnp.where(first, 0.0, edge)
    return jnp.where(rows < n, edge, pltpu.roll(x, n, 0))


def _shift_up(x, halo, n, last):
    tb = x.shape[0]
    rows = lax.broadcasted_iota(jnp.int32, x.shape, 0)
    edge = jnp.concatenate([jnp.zeros((tb - 8, x.shape[1]), F32), pltpu.roll(halo, 8 - n, 0)], axis=0)
    edge = jnp.where(last, 0.0, edge)
    return jnp.where(rows >= tb - n, edge, pltpu.roll(x, tb - n, 0))


def _conv(u, halo, w_ref, b_ref, first):
    taps = (_shift_down(u, halo, 2, first), _shift_down(u, halo, 1, first), u)
    c = b_ref[...] + w_ref[0:1, :] * taps[0] + w_ref[1:2, :] * taps[1] + w_ref[2:3, :] * taps[2]
    return c, taps


def _ffn_specs(tb, f):
    hb = tb // 8
    cur = lambda c: pl.BlockSpec((tb, f), lambda i, c=c: (i, c))
    prev = lambda c: pl.BlockSpec((8, f), lambda i, c=c: (jnp.maximum(i * hb - 1, 0), c))
    vec = lambda r, c: pl.BlockSpec((r, f), lambda i, c=c: (0, c))
    return cur, prev, vec


def _ffn_act_fwd(u, cw, cb, name):
    lp = u.shape[0]
    f = D_FF
    tb = TILE_ROW
    cur, prev, vec = _ffn_specs(tb, f)

    def body(ug, uv, hg, hv, wg, wv, bg, bv, o_ref):
        first = pl.program_id(0) == 0
        cg, _ = _conv(ug[...], hg[...], wg, bg, first)
        cv, _ = _conv(uv[...], hv[...], wv, bv, first)
        o_ref[...] = (cg * jax.nn.sigmoid(cg) * cv).astype(BF16)

    return pl.pallas_call(
        body, name=name, grid=(lp // tb,),
        in_specs=[cur(0), cur(1), prev(0), prev(1), vec(8, 0), vec(8, 1), vec(1, 0), vec(1, 1)],
        out_specs=pl.BlockSpec((tb, f), lambda i: (i, 0)), out_shape=SDS((lp, f), BF16),
        compiler_params=_params(("parallel",)),
    )(u, u, u, u, cw, cw, cb, cb)


def _ffn_act_bwd_conv(u, dact, cw, cb, name):
    lp = u.shape[0]
    f = D_FF
    tb = TILE_CONV_BWD
    cur, prev, vec = _ffn_specs(tb, f)

    def body(ug, uv, hg, hv, wg, wv, bg, bv, da_ref, dc_ref, dw_ref, db_ref):
        first = pl.program_id(0) == 0

        @pl.when(first)
        def _():
            dw_ref[...] = jnp.zeros_like(dw_ref)
            db_ref[...] = jnp.zeros_like(db_ref)

        cg, tg = _conv(ug[...], hg[...], wg, bg, first)
        cv, tv = _conv(uv[...], hv[...], wv, bv, first)
        da = da_ref[...]
        sg = jax.nn.sigmoid(cg)
        dcg = da * cv * sg * (1.0 + cg * (1.0 - sg))
        dcv = da * cg * sg
        for c, (dc, taps) in enumerate(((dcg, tg), (dcv, tv))):
            dc_ref[:, c * f:(c + 1) * f] = dc
            for n in range(3):
                dw_ref[n:n + 1, c * f:(c + 1) * f] += jnp.sum(dc * taps[n], axis=0, keepdims=True)
            db_ref[0:1, c * f:(c + 1) * f] += jnp.sum(dc, axis=0, keepdims=True)

    acc = pl.BlockSpec((8, 2 * f), lambda i: (0, 0))
    return pl.pallas_call(
        body, name=name, grid=(lp // tb,),
        in_specs=[cur(0), cur(1), prev(0), prev(1), vec(8, 0), vec(8, 1), vec(1, 0), vec(1, 1),
                  pl.BlockSpec((tb, f), lambda i: (i, 0))],
        out_specs=[pl.BlockSpec((tb, 2 * f), lambda i: (i, 0)), acc, acc],
        out_shape=[SDS((lp, 2 * f), F32), SDS((8, 2 * f), F32), SDS((8, 2 * f), F32)],
        compiler_params=_params(("arbitrary",)),
    )(u, u, u, u, cw, cw, cb, cb, dact)


def _ffn_act_bwd_in(dc, cw, name):
    lp = dc.shape[0]
    f2 = 2 * D_FF
    tb = TILE_ROW
    nb = lp // tb
    hb = tb // 8

    def body(dc_ref, n_ref, w_ref, o_ref):
        last = pl.program_id(0) == nb - 1
        dcv, halo = dc_ref[...], n_ref[...]
        du = (w_ref[2:3, :] * dcv + w_ref[1:2, :] * _shift_up(dcv, halo, 1, last)
              + w_ref[0:1, :] * _shift_up(dcv, halo, 2, last))
        o_ref[...] = du.astype(BF16)

    cur = pl.BlockSpec((tb, f2), lambda i: (i, 0))
    return pl.pallas_call(
        body, name=name, grid=(nb,),
        in_specs=[cur, pl.BlockSpec((8, f2), lambda i: (jnp.minimum((i + 1) * hb, nb * hb - 1), 0)),
                  pl.BlockSpec((8, f2), lambda i: (0, 0))],
        out_specs=cur, out_shape=SDS((lp, f2), BF16),
        compiler_params=_params(("parallel",)),
    )(dc, dc, cw)


def _loss_head(y, target, n_real, name):
    lp, d = y.shape
    tb = TILE_MM

    def body(y_ref, t_ref, dy_ref, loss_ref):
        i = pl.program_id(0)

        @pl.when(i == 0)
        def _():
            loss_ref[...] = jnp.zeros_like(loss_ref)

        rows = i * tb + lax.broadcasted_iota(jnp.int32, (tb, 1), 0)
        real = (rows >= N_META) & (rows < N_META + n_real)
        diff = jnp.where(real, y_ref[...] - t_ref[...], 0.0)
        dy_ref[...] = diff * (1.0 / d)
        loss_ref[...] += (0.5 / d) * jnp.sum(diff * diff).reshape(1, 1)

    row = pl.BlockSpec((tb, d), lambda i: (i, 0))
    return pl.pallas_call(
        body, name=name, grid=(lp // tb,), in_specs=[row, row],
        out_specs=[row, pl.BlockSpec((1, 1), lambda i: (0, 0))],
        out_shape=[SDS((lp, d), F32), SDS((1, 1), F32)],
        compiler_params=_params(("arbitrary",)),
    )(y, target)


def _pad_lanes(v, width, at=0):
    return jnp.pad(v.astype(F32), (at, width - at - v.shape[0]))[None, :]


def _mix_params(w, big, l):
    b = lambda a: a.astype(BF16)
    win = big["w_in"]
    fq, fk, fv, ff, cq, ckv, kr, sq, sk, sv, gates = jnp.split(
        win, [512, 1024, 1536, 1544, 1800, 1928, 1960, 2472, 2600, 2728], axis=1)
    misc = jnp.concatenate([kr, ff, jnp.zeros((D_MODEL, 88), win.dtype)], axis=1)
    w_in = b(jnp.concatenate([gates, fq, fk, fv, sq, sk, sv, cq, ckv, misc], axis=1))
    wq = jnp.pad(big["mla_w_q_up"].reshape(256, HEADS, 96), ((0, 0), (0, 0), (0, 32))).reshape(256, 1024)
    wkv = big["mla_w_kv_up"].reshape(128, HEADS, 128)
    wkk = jnp.pad(wkv[:, :, :64], ((0, 0), (0, 0), (0, 64))).reshape(128, 1024)
    wkvv = wkv[:, :, 64:].reshape(128, 512)
    tile = lambda g, n: jnp.tile(g.astype(F32), n)[None, :]
    prm = [tile(w["fox_q_g"][l], 8), tile(w["fox_k_g"][l], 8), tile(w["swa_q_g"][l], 8), tile(w["swa_k_g"][l], 2),
           _pad_lanes(w["fox_forget_b"][l], 128, FF_LANE), w["mla_q_a_g"][l][None, :], w["mla_kv_a_g"][l][None, :],
           tile(jnp.pad(w["mla_q_g"][l], (0, 32)), 8), tile(jnp.pad(w["mla_k_g"][l], (0, 32)), 8),
           wq.astype(F32), wkk.astype(F32), wkvv.astype(F32)]
    return dict(g1=w["norm1_g"][l][None, :], w_in=w_in, prm=prm, sinks=w["swa_sinks"][l].astype(F32),
                wb=b(big["w_branch"]), wo=b(big["w_o"]))


def _ffn_params(w, big, l):
    cw = jnp.pad(w["ffn_conv_w"][l].astype(F32), ((0, 5), (0, 0)))
    return dict(g2=w["norm2_g"][l][None, :], w_up=big["ffn_w_up"].astype(BF16), cw=cw,
                cb=w["ffn_conv_b"][l][None, :].astype(F32), w_down=big["ffn_w_down"].astype(BF16))


def _decay_rows(c):
    return c[:, FF_LANE:FF_LANE + HEADS].T[:, None, :]


def _from_rows(row):
    return jnp.pad(row[:, 0, :].T, ((0, 0), (FF_LANE, 128 - FF_LANE - HEADS)))


def _layer_fwd_mix(h, lw, consts, cos, sin, slopes, l):
    tag = f"l{l}_"
    xn, proj = _norm_matmul(h, lw["g1"], lw["w_in"], IN_W, tag + "in_proj")
    fq, fk, fv, mq, mk, mv, sq, skd, svd, ls = _prep_fwd(proj, lw["prm"], consts, cos, sin, tag + "prep")
    c = _cumsum([ls], False, tag + "decay_cumsum")
    c_row = _decay_rows(c)
    oa, lse_a = _att_fwd(_Att("fox"), fq, fk, fv, (c_row,), tag + "fox_fwd")
    ob, lse_b = _att_fwd(_Att("mla"), mq, mk, mv, (), tag + "mla_fwd")
    oc, lse_c = _att_fwd(_Att("swa"), sq, skd, svd, (lw["sinks"], slopes), tag + "swa_fwd")
    h2 = _post_fwd(h, proj, (oa, ob, oc), lw["wb"], lw["wo"], tag + "merge")
    saved = dict(h=h, xn=xn, proj=proj, q=(fq, mq, sq), k=(fk, mk, skd), v=(fv, mv, svd), c=c_row,
                 o=(oa, ob, oc), lse=(lse_a, lse_b, lse_c), h2=h2)
    return h2, saved


def _layer_fwd_ffn(h2, lw, l):
    tag = f"l{l}_"
    xn2, u = _norm_matmul(h2, lw["g2"], lw["w_up"], 2 * D_FF, tag + "ffn_up")
    act = _ffn_act_fwd(u, lw["cw"], lw["cb"], tag + "ffn_act")
    h3 = _matmul_residual(act, lw["w_down"], h2, tag + "ffn_down")
    return h3, dict(xn2=xn2, u=u, act=act)


def _layer_bwd_ffn(dh3, lw, sv, l):
    tag = f"l{l}_"
    f = D_FF
    dact = _matmul_nt(dh3, lw["w_down"], f, tag + "ffn_down_dx")
    dw_down = _matmul_tn(sv["act"], dh3, D_MODEL, tag + "ffn_down_dw")
    dc, dcw, dcb = _ffn_act_bwd_conv(sv["u"], dact, lw["cw"], lw["cb"], tag + "ffn_act_dc")
    du = _ffn_act_bwd_in(dc, lw["cw"], tag + "ffn_act_du")
    dw_up = _matmul_tn(sv["xn2"], du, f, tag + "ffn_up_dw")
    dh2, dg2 = _norm_matmul_bwd([du], lw["w_up"], sv["h2"], lw["g2"], dh3, tag + "ffn_up_dx")
    g = dict(norm2_g=dg2[0], ffn_w_up=dw_up, ffn_conv_w=dcw[0:3], ffn_conv_b=dcb[0], ffn_w_down=dw_down)
    return dh2, g


def _layer_bwd_mix(dh2, lw, sv, consts, folds, cos, sin, slopes, l, hook=None, merge_hook=None):
    tag = f"l{l}_"
    dgates, doa, dob, doc, dwb, dwo = _post_bwd(dh2, sv["proj"], sv["o"], lw["wb"], lw["wo"], tag + "merge_bwd")
    c_row = sv["c"]
    tick = merge_hook({"w_branch": dwb, "w_o": dwo}) if merge_hook else None
    if tick is not None:
        c_row = c_row + tick
    extras = ((c_row,), (), (lw["sinks"], slopes))
    grads = []
    for n, (mode, do) in enumerate((("fox", doa), ("mla", dob), ("swa", doc))):
        res = _att_bwd(_Att(mode), sv["q"][n], sv["k"][n], sv["v"][n], sv["o"][n], do, sv["lse"][n], extras[n],
                       tag + mode + "_bwd")
        grads.append((res[0], res[1], res[2], res[3:]))
    (dfq, dfk, dfv, (dcq, dck)), (dmq, dmk, dmv, _), (dsq, dskd, dsvd, (dsink,)) = grads
    dls = _cumsum([_from_rows(dcq), _from_rows(dck)], True, tag + "decay_cumsum_bwd")
    res = _prep_bwd(sv["proj"], lw["prm"], consts, cos, sin,
                    (dfq, dfk, dfv, dmq, dmk, dmv, dsq, dskd, dsvd, dls), folds, tag + "prep_bwd")
    dother, pg = res[0], res[1:]
    dh, dg1 = _norm_matmul_bwd([dgates, dother], lw["w_in"], sv["h"], lw["g1"], dh2, tag + "in_proj_dx")
    dw_g = _matmul_tn(sv["xn"], dgates, GATES_W, tag + "in_proj_dw_gates")
    dw_o = _matmul_tn(sv["xn"], dother, OTHER_W, tag + "in_proj_dw_other")
    d_in = jnp.concatenate([
        dw_o[:, O_FQ:O_FV + 512], dw_o[:, O_MISC + FF_LANE:O_MISC + FF_LANE + 8], dw_o[:, O_CQ:O_CQ + 256],
        dw_o[:, O_CKV:O_CKV + 128], dw_o[:, O_MISC:O_MISC + 32], dw_o[:, O_SQ:O_SQ + 512], dw_o[:, O_SK:O_SK + 128],
        dw_o[:, O_SV:O_SV + 128], dw_g], axis=1)
    d_wq = pg[9].reshape(256, HEADS, 128)[:, :, :96].reshape(256, 768)
    d_wkv = jnp.concatenate([pg[10].reshape(128, HEADS, 128)[:, :, :64], pg[11].reshape(128, HEADS, 64)],
                            axis=2).reshape(128, 1024)
    g = dict(
        w_in=d_in, fox_forget_b=pg[4][0, FF_LANE:FF_LANE + 8], fox_q_g=pg[0][0, :64],
        fox_k_g=pg[1][0, :64], mla_q_a_g=pg[5][0], mla_w_q_up=d_wq, mla_kv_a_g=pg[6][0], mla_w_kv_up=d_wkv,
        mla_q_g=pg[7][0, :96], mla_k_g=pg[8][0, :96], swa_q_g=pg[2][0, :64], swa_k_g=pg[3][0, :64],
        swa_sinks=dsink[:, 0, 0:2 * PAIRS].reshape(HEADS), w_branch=dwb, w_o=dwo)
    tick = hook(g) if hook else None
    g["norm1_g"] = dg1[0]
    return dh, g, tick


_MIX_BIG = ("w_in", "mla_w_q_up", "mla_w_kv_up", "w_branch", "w_o")
_FFN_BIG = ("ffn_w_up", "ffn_w_down")


def _local_step(x, target, w, hook=None, fetch=None):
    if fetch is None:
        fetch = lambda l, stage, after: {n: w[n][l] for n in (_MIX_BIG if stage == "mix" else _FFN_BIG)}
    seq = x.shape[0]
    length = N_META + seq
    lp = -(-length // ROW_ALIGN) * ROW_ALIGN
    pad = lp - length
    h = jnp.concatenate([w["meta_tokens"].astype(F32), x, jnp.zeros((pad, D_MODEL), F32)], axis=0)
    tgt = jnp.pad(target, ((N_META, pad), (0, 0)))
    consts = _consts()
    folds = (_fold_matrix(512, 64), _fold_matrix(1024, 128))
    cos, sin = _rope_tables(lp)
    slopes = jnp.asarray(2.0 ** (-8.0 * np.arange(1, HEADS + 1, dtype=np.float32) / HEADS), F32)
    lws, saved = [], []
    for l in range(DEPTH):
        lw = _mix_params(w, fetch(l, "mix", h), l)
        h, sv = _layer_fwd_mix(h, lw, consts, cos, sin, slopes, l)
        lw.update(_ffn_params(w, fetch(l, "ffn", h), l))
        h, sv_ffn = _layer_fwd_ffn(h, lw, l)
        lws.append(lw)
        saved.append({**sv, **sv_ffn})
    dh, loss = _loss_head(h, tgt, seq, "loss_head")
    grads = [None] * DEPTH
    for l in reversed(range(DEPTH)):
        dh, g_ffn = _layer_bwd_ffn(dh, lws[l], saved[l], l)
        tick = hook(l, "ffn", g_ffn) if hook else None
        if tick is not None:
            lws[l]["sinks"] = lws[l]["sinks"] + tick
        mix_hook = (lambda g, l=l, g_ffn=g_ffn: hook(l, "mix", {**g_ffn, **g})) if hook else None
        merge_hook = (lambda g, l=l: hook(l, "merge", g)) if hook else None
        dh, g_mix, tick = _layer_bwd_mix(dh, lws[l], saved[l], consts, folds, cos, sin, slopes, l, mix_hook, merge_hook)
        grads[l] = {**g_ffn, **g_mix}
        if tick is not None and l > 0:
            lws[l - 1]["cw"] = lws[l - 1]["cw"] + tick
    return loss, dh[N_META:length], dh[:N_META], grads


def _place():
    return lax.axis_index("x"), lax.axis_index("y"), lax.axis_index("c")


def _flip(pos, k):
    x, y, c = pos
    return (1 - x if k & 4 else x, 1 - y if k & 2 else y, 1 - c if k & 1 else c)


def _index(pos):
    return 4 * pos[0] + 2 * pos[1] + pos[2]


def _gather(tensors, name):
    n_t = len(tensors)

    def body(*refs):
        ins, outs = refs[:n_t], refs[n_t:2 * n_t]
        send_sems, recv_sems, local_sems = refs[2 * n_t:]
        x, y, c = _place()
        me, sibling = (x, y, c), (x, y, 1 - c)
        chips = [(1 - x, y), (x, 1 - y), (1 - x, 1 - y)]

        def copy(t, k, block, to, src=None):
            dst = outs[t].at[_index(block)]
            return pltpu.make_async_remote_copy(
                src_ref=dst if src is None else src, dst_ref=dst, send_sem=send_sems.at[t, k],
                recv_sem=recv_sems.at[t, k], device_id=to, device_id_type=pl.DeviceIdType.MESH)

        local, sent = [], []
        for t in range(n_t):
            local.append(pltpu.make_async_copy(ins[t], outs[t].at[_index(me)], local_sems.at[t]))
            local[-1].start()
            sent.append(copy(t, 0, me, sibling, src=ins[t]))
            sent += [copy(t, 1 + j, me, (*chip, c), src=ins[t]) for j, chip in enumerate(chips)]
        for cp in sent:
            cp.start()
        for j, chip in enumerate(chips):
            for t in range(n_t):
                copy(t, 1 + j, (*chip, c), me).wait_recv()
                sent.append(copy(t, 4 + j, (*chip, c), sibling))
                sent[-1].start()
        for t in range(n_t):
            copy(t, 0, sibling, me).wait_recv()
            for j, chip in enumerate(chips):
                copy(t, 4 + j, (*chip, 1 - c), me).wait_recv()
        for cp in sent:
            cp.wait_send()
        for cp in local:
            cp.wait()

    any_spec = pl.BlockSpec(memory_space=pl.ANY)
    return pl.pallas_call(
        body, name=name, in_specs=[any_spec] * n_t, out_specs=[any_spec] * n_t,
        out_shape=[SDS((N_DEV,) + a.shape, a.dtype) for a in tensors],
        scratch_shapes=[pltpu.SemaphoreType.DMA((n_t, N_DEV - 1)), pltpu.SemaphoreType.DMA((n_t, N_DEV - 1)),
                        pltpu.SemaphoreType.DMA((n_t,))],
    )(*tensors)


def _exchange_start(tensors, name, gather=False, after=None):
    n_t = len(tensors)

    def body(*refs):
        ins, lands = refs[:n_t], refs[n_t:2 * n_t]
        send_sem, recv_sem = refs[2 * n_t + 1:2 * n_t + 3]
        token = refs[-1]
        me = _place()
        mine = _index(me)
        for t in range(n_t):
            for k in range(1, N_DEV):
                peer = _flip(me, k)
                pltpu.make_async_remote_copy(
                    src_ref=ins[t] if gather else ins[t].at[_index(peer)], dst_ref=lands[t].at[mine],
                    send_sem=send_sem, recv_sem=recv_sem, device_id=peer, device_id_type=pl.DeviceIdType.MESH).start()
        token[...] = jnp.zeros_like(token)

    hbm = pl.BlockSpec(memory_space=pltpu.HBM)
    sem = pl.BlockSpec(memory_space=pltpu.SEMAPHORE)
    one = pltpu.SemaphoreType.DMA(())
    land_shape = lambda a: ((N_DEV,) + a.shape) if gather else a.shape
    bufs = ([pltpu.HBM(a.shape, a.dtype) for a in tensors] + [pltpu.HBM(land_shape(a), a.dtype) for a in tensors])
    after = jnp.zeros((8, 128), F32) if after is None else after
    outs = pl.pallas_call(
        body, name=name, in_specs=[hbm] * (2 * n_t) + [pl.BlockSpec(memory_space=pl.ANY)],
        out_specs=[sem, sem] + [hbm] * (2 * n_t) + [pl.BlockSpec(memory_space=pltpu.VMEM)],
        out_shape=[one, one] + bufs + [SDS((8, 128), F32)],
        input_output_aliases={i: 2 + i for i in range(2 * n_t)},
        compiler_params=pltpu.CompilerParams(has_side_effects=pltpu.SideEffectType.DATAFLOW_SIDE_EFFECTING),
    )(*[pltpu.with_memory_space_constraint(a, pltpu.HBM) for a in tensors],
      *[pltpu.with_memory_space_constraint(lax.empty(land_shape(a), a.dtype), pltpu.HBM) for a in tensors], after)
    return outs[:-1], outs[-1][0, 0]


def _exchange_wait(state, after, name, gather=False):
    n_t = (len(state) - 2) // 2

    def body(*refs):
        send_sem, recv_sem = refs[0:2]
        ins, lands = refs[2:2 + n_t], refs[2 + n_t:2 + 2 * n_t]
        me = _place()
        for t in range(n_t):
            for k in range(1, N_DEV):
                peer = _flip(me, k)
                copy = pltpu.make_async_remote_copy(
                    src_ref=ins[t] if gather else ins[t].at[_index(peer)], dst_ref=lands[t].at[_index(peer)],
                    send_sem=send_sem, recv_sem=recv_sem, device_id=peer, device_id_type=pl.DeviceIdType.MESH)
                copy.wait_send()
                copy.wait_recv()

    hbm = pl.BlockSpec(memory_space=pltpu.HBM)
    sem = pl.BlockSpec(memory_space=pltpu.SEMAPHORE)
    bufs = [pltpu.HBM(a.shape, a.dtype) for a in state[2:]]
    outs = pl.pallas_call(
        body, name=name, in_specs=[sem, sem] + [hbm] * (2 * n_t) + [pl.BlockSpec(memory_space=pl.ANY)],
        out_specs=[hbm] * (2 * n_t), out_shape=bufs,
        input_output_aliases={2 + i: i for i in range(2 * n_t)},
        compiler_params=pltpu.CompilerParams(has_side_effects=pltpu.SideEffectType.DATAFLOW_SIDE_EFFECTING),
    )(*state, after)
    return outs[n_t:]


def _sum_slots(parts, name):
    n, rows, w = parts.shape
    tb = 8

    def body(p_ref, o_ref):
        acc = p_ref[0].astype(F32)
        for s in range(1, n):
            acc = acc + p_ref[s].astype(F32)
        o_ref[...] = acc

    return pl.pallas_call(
        body, name=name, grid=(rows // tb,),
        in_specs=[pl.BlockSpec((n, tb, w), lambda i: (0, i, 0))], out_specs=pl.BlockSpec((tb, w), lambda i: (i, 0)),
        out_shape=SDS((rows, w), F32), compiler_params=_params(("parallel",)),
    )(parts)


def _adamw(wt, m, v, parts, name, own=None, after=None):
    shape = wt.shape
    parts = parts if isinstance(parts, (list, tuple)) else [parts]
    n, w = parts[0].shape[0], shape[-1]
    rows = math.prod(shape[:-1])
    per = rows // len(parts)
    step = 16 if parts[0].dtype == BF16 else 8
    tb = max([t for t in range(step, 257, step) if per % t == 0] or [per])
    nb = per // tb
    c1 = 1.0 / (1.0 - ADAM_B1 ** ADAM_STEP)
    c2 = 1.0 / (1.0 - ADAM_B2 ** ADAM_STEP)
    state = [a.reshape(rows, w) for a in (wt, m, v)]
    n_in = 4 if own is None else 5
    outs = None
    for l in reversed(range(len(parts))):
        def body(*refs):
            idx_ref = None if own is None else refs[0]
            w_ref, m_ref, v_ref, p_ref = refs[n_in - 4:n_in] if own is None else refs[1:5]
            g_out, d_out, m_out, v_out = refs[-4:]
            g = None
            for s in range(n):
                term = p_ref[s] if own is None else jnp.where(idx_ref[0] == s, refs[5][0], p_ref[s])
                g = term.astype(F32) if g is None else g + term.astype(F32)
            m_new = ADAM_B1 * m_ref[...] + (1.0 - ADAM_B1) * g
            v_new = ADAM_B2 * v_ref[...] + (1.0 - ADAM_B2) * (g * g)
            g_out[...] = g
            m_out[...] = m_new
            v_out[...] = v_new
            d_out[...] = -ADAM_LR * ((m_new * c1) / (jnp.sqrt(v_new * c2) + ADAM_EPS) + ADAM_WD * w_ref[...])

        row = pl.BlockSpec((tb, w), lambda i, *_, l=l: (l * nb + i, 0))
        in_specs = [row, row, row, pl.BlockSpec((n, tb, w), lambda i, *_: (0, i, 0))]
        args = [*state, parts[l].reshape(n, per, w)]
        if own is not None:
            in_specs.append(pl.BlockSpec((1, tb, w), lambda i, idx: (idx[0], i, 0)))
            args.append(own[l].reshape(n, per, w))
        prev = [] if outs is None else list(outs)
        behind = [] if after is None else [after]
        in_specs += [pl.BlockSpec(memory_space=pl.ANY)] * (len(prev) + len(behind))
        n_pre = 0 if own is None else 1
        call = dict(name=f"{name}_{l}", out_shape=[SDS((rows, w), F32)] * 4,
                    input_output_aliases={n_pre + len(args) + k: k for k in range(len(prev))},
                    compiler_params=_params(("parallel",)))
        if own is None:
            outs = pl.pallas_call(body, grid=(nb,), in_specs=in_specs, out_specs=[row] * 4, **call)(*args, *prev, *behind)
        else:
            spec = pltpu.PrefetchScalarGridSpec(num_scalar_prefetch=1, grid=(nb,), in_specs=in_specs, out_specs=[row] * 4)
            idx = jnp.reshape(_index(_place()), (1,)).astype(jnp.int32)
            outs = pl.pallas_call(body, grid_spec=spec, **call)(idx, *args, *prev, *behind)
    return [o.reshape(shape) for o in outs]


_BIG = [("w_in", 2), ("mla_w_q_up", 2), ("mla_w_kv_up", 2), ("w_branch", 3), ("w_o", 1), ("ffn_w_up", 2), ("ffn_w_down", 1)]
_SMALL_SHARDED = [("meta_tokens", 1), ("ffn_conv_w", 2)]
_REPLICATED = ["norm1_g", "fox_forget_b", "fox_q_g", "fox_k_g", "mla_q_a_g", "mla_kv_a_g", "mla_q_g", "mla_k_g",
               "swa_q_g", "swa_k_g", "swa_sinks", "norm2_g", "ffn_conv_b"]
_ORDER = ["meta_tokens", "norm1_g", "w_in", "fox_forget_b", "fox_q_g", "fox_k_g", "mla_q_a_g", "mla_w_q_up",
          "mla_kv_a_g", "mla_w_kv_up", "mla_q_g", "mla_k_g", "swa_q_g", "swa_k_g", "swa_sinks", "w_branch", "w_o",
          "norm2_g", "ffn_w_up", "ffn_conv_w", "ffn_conv_b", "ffn_w_down"]


def _flat_rows(vecs, dtype, row_mult):
    flat = jnp.concatenate([a.reshape(-1).astype(dtype) for a in vecs])
    rows = -(-flat.shape[0] // (1024 * row_mult)) * row_mult
    return jnp.pad(flat, (0, rows * 1024 - flat.shape[0])).reshape(rows, 1024)


def _unflatten(flat, shapes):
    out, off = [], 0
    for s in shapes:
        n = math.prod(s)
        out.append(flat[off:off + n].reshape(s))
        off += n
    return out


def _to_full(blocks, axis):
    moved = jnp.moveaxis(blocks, 0, axis)
    s = moved.shape
    return moved.reshape(s[:axis] + (s[axis] * s[axis + 1],) + s[axis + 2:])


def _to_blocks(full, axis):
    s = full.shape
    split = full.reshape(s[:axis] + (N_DEV, s[axis] // N_DEV) + s[axis + 1:])
    return jnp.moveaxis(split, axis, 0)


def kernel(x, meta_tokens, norm1_g, w_in, fox_forget_b, fox_q_g, fox_k_g, mla_q_a_g, mla_w_q_up, mla_kv_a_g, mla_w_kv_up, mla_q_g, mla_k_g, swa_q_g, swa_k_g, swa_sinks, w_branch, w_o, norm2_g, ffn_w_up, ffn_conv_w, ffn_conv_b, ffn_w_down, loss_target, m_meta_tokens, m_norm1_g, m_w_in, m_fox_forget_b, m_fox_q_g, m_fox_k_g, m_mla_q_a_g, m_mla_w_q_up, m_mla_kv_a_g, m_mla_w_kv_up, m_mla_q_g, m_mla_k_g, m_swa_q_g, m_swa_k_g, m_swa_sinks, m_w_branch, m_w_o, m_norm2_g, m_ffn_w_up, m_ffn_conv_w, m_ffn_conv_b, m_ffn_w_down, v_meta_tokens, v_norm1_g, v_w_in, v_fox_forget_b, v_fox_q_g, v_fox_k_g, v_mla_q_a_g, v_mla_w_q_up, v_mla_kv_a_g, v_mla_w_kv_up, v_mla_q_g, v_mla_k_g, v_swa_q_g, v_swa_k_g, v_swa_sinks, v_w_branch, v_w_o, v_norm2_g, v_ffn_w_up, v_ffn_conv_w, v_ffn_conv_b, v_ffn_w_down):
    wl = dict(zip(_ORDER, (meta_tokens, norm1_g, w_in, fox_forget_b, fox_q_g, fox_k_g, mla_q_a_g, mla_w_q_up,
                           mla_kv_a_g, mla_w_kv_up, mla_q_g, mla_k_g, swa_q_g, swa_k_g, swa_sinks, w_branch, w_o,
                           norm2_g, ffn_w_up, ffn_conv_w, ffn_conv_b, ffn_w_down)))
    ml = dict(zip(_ORDER, (m_meta_tokens, m_norm1_g, m_w_in, m_fox_forget_b, m_fox_q_g, m_fox_k_g, m_mla_q_a_g,
                           m_mla_w_q_up, m_mla_kv_a_g, m_mla_w_kv_up, m_mla_q_g, m_mla_k_g, m_swa_q_g, m_swa_k_g,
                           m_swa_sinks, m_w_branch, m_w_o, m_norm2_g, m_ffn_w_up, m_ffn_conv_w, m_ffn_conv_b,
                           m_ffn_w_down)))
    vl = dict(zip(_ORDER, (v_meta_tokens, v_norm1_g, v_w_in, v_fox_forget_b, v_fox_q_g, v_fox_k_g, v_mla_q_a_g,
                           v_mla_w_q_up, v_mla_kv_a_g, v_mla_w_kv_up, v_mla_q_g, v_mla_k_g, v_swa_q_g, v_swa_k_g,
                           v_swa_sinks, v_w_branch, v_w_o, v_norm2_g, v_ffn_w_up, v_ffn_conv_w, v_ffn_conv_b,
                           v_ffn_w_down)))
    small_sh = [n for n, _ in _SMALL_SHARDED]
    big = [n for n, _ in _BIG]
    axis_of = dict(_BIG)
    idx = _index(_place())

    def to_full(n, blocks, own=None):
        if own is not None:
            sel = (jnp.arange(N_DEV) == idx).reshape((N_DEV,) + (1,) * own.ndim)
            blocks = jnp.where(sel, own[None], blocks)
        return _to_full(blocks, axis_of[n] - 1)

    local = {(n, l): wl[n][l].astype(BF16) for n in big for l in range(DEPTH)}
    got = _gather([local[(n, 0)] for n in _MIX_BIG] + [wl[n] for n in small_sh], "gather_weights_l0_mix")
    full = {n: wl[n] for n in _REPLICATED}
    for (n, axis), blocks in zip(_SMALL_SHARDED, got[len(_MIX_BIG):]):
        full[n] = _to_full(blocks, axis)
    ready = {(n, 0): to_full(n, blocks) for n, blocks in zip(_MIX_BIG, got)}
    later = {"l0_ffn": [(n, 0) for n in _FFN_BIG], "l1": [(n, 1) for n in big]}
    states = {}
    for key, names in later.items():
        states[key], tick = _exchange_start([local[e] for e in names], "gather_weights_" + key + "_start", True, got[0])
        full["norm1_g"] = full["norm1_g"] + tick

    def fetch(l, stage, after):
        key = "l0_ffn" if l == 0 else "l1"
        if (l, stage) != (0, "mix") and key in states:
            lands = _exchange_wait(states.pop(key), after, "gather_weights_" + key + "_wait", True)
            ready.update({e: to_full(e[0], blocks, local[e]) for e, blocks in zip(later[key], lands)})
        return {n: ready[(n, l)] for n in (_MIX_BIG if stage == "mix" else _FFN_BIG)}

    blocks_of = lambda g, names: [_to_blocks(g[n], axis_of[n] - 1).astype(BF16) for n in names]
    early = {}

    def hook(l, stage, g):
        if l == DEPTH - 1 and stage == "mix":
            key, names = "l1", big
        elif l == 0:
            merge = ("w_branch", "w_o")
            groups = {"ffn": _FFN_BIG, "merge": merge, "mix": tuple(n for n in _MIX_BIG if n not in merge)}
            key, names = "l0_" + stage, groups[stage]
        else:
            return None
        sends = blocks_of(g, names)
        if key == "l0_mix":
            early[key] = (names, l, sends)
            return None
        state, tick = _exchange_start(sends, "exchange_grads_" + key + "_start")
        early[key] = (names, l, sends, state)
        return tick

    loss, grad_x, grad_meta, grads = _local_step(x[0], loss_target[0], full, hook, fetch)
    result = {kind: {} for kind in ("grad", "delta", "new_m", "new_v")}
    small_grads = {k: jnp.stack([grads[l][k] for l in range(DEPTH)]) for k in grads[0] if k not in big}
    small_grads["meta_tokens"] = grad_meta
    small_full = _REPLICATED + small_sh
    mine_small = _flat_rows([small_grads[n] for n in small_full] + [loss], F32, 8)
    small_state, tick = _exchange_start([mine_small], "gather_small_grads_start", True)
    names, l, sends = early["l0_mix"]
    state, tick = _exchange_start(sends, "exchange_grads_l0_mix_start", after=jnp.reshape(tick, (1, 1)))
    early["l0_mix"] = (names, l, sends, state)
    started = jnp.reshape(tick, (1, 1))
    landed, sent = {}, {}
    after = sends[0]
    for key in ("l1", "l0_ffn", "l0_merge"):
        names, l, sends, state = early[key]
        got = _exchange_wait(state, after, "exchange_grads_" + key + "_wait")
        landed.update({(n, l): p for n, p in zip(names, got)})
        sent.update({(n, l): p for n, p in zip(names, sends)})

    def update(names):
        for n in names:
            outs = _adamw(wl[n], ml[n], vl[n], [landed[(n, l)] for l in range(DEPTH)], "adamw_" + n,
                          [sent[(n, l)] for l in range(DEPTH)], started)
            for kind, val in zip(result, outs):
                result[kind][n] = val

    update(_FFN_BIG)
    done = result["delta"]["ffn_w_up"][0, 0, :8] + result["delta"]["ffn_w_down"][0, 0, :8]
    got_small = _exchange_wait(small_state, done, "gather_small_grads_wait", True)[0]
    sel = (jnp.arange(N_DEV) == idx).reshape(N_DEV, 1, 1)
    total_small = _sum_slots(jnp.where(sel, mine_small[None], got_small), "sum_small_grads").reshape(-1)
    pieces = _unflatten(total_small, [small_grads[n].shape for n in small_full] + [()])
    loss_total = pieces[-1]
    g_small = dict(zip(small_full, pieces[:-1]))
    for n, axis in _SMALL_SHARDED:
        size = wl[n].shape[axis]
        g_small[n] = lax.dynamic_slice_in_dim(g_small[n], idx * size, size, axis)
    flat = lambda d: _flat_rows([d[n] for n in small_full], F32, 8)
    small_out = _adamw(flat(wl), flat(ml), flat(vl), flat(g_small)[None], "adamw_small")
    for kind, fs in zip(result, small_out):
        result[kind].update(zip(small_full, _unflatten(fs.reshape(-1), [wl[n].shape for n in small_full])))
    names, l, sends, state = early["l0_mix"]
    got = _exchange_wait(state, small_out[0], "exchange_grads_l0_mix_wait")
    landed.update({(n, l): p for n, p in zip(names, got)})
    sent.update({(n, l): p for n, p in zip(names, sends)})
    update(_MIX_BIG)
    outs = [loss_total, grad_x[None]]
    for kind in ("grad", "delta", "new_m", "new_v"):
        outs += [result[kind][n] for n in _ORDER]
    return tuple(outs)
```

```python
import functools
import math

import numpy as np
import jax
import jax.numpy as jnp
from jax import lax
from jax.experimental import pallas as pl
from jax.experimental.pallas import tpu as pltpu

F32, BF16 = jnp.float32, jnp.bfloat16
SDS = jax.ShapeDtypeStruct

D_MODEL = 1024
N_META = 16
EPS = 1e-6
WINDOW = 128
ROPE_THETA = 10000.0
HEADS = 8
D_FF = 2816
DEPTH = 2
N_DEV = 8
ADAM_LR, ADAM_B1, ADAM_B2, ADAM_EPS, ADAM_WD, ADAM_STEP = 0.001, 0.9, 0.999, 1e-08, 0.01, 10

ROW_ALIGN = 384
TILE_MM = 384
TILE_ROW = 192
TILE_CONV_BWD = 128
TILE_ATT = 384
TILE_POST = 384
PAIRS = 2
VMEM_LIMIT = 56 * 1024 * 1024

GATES_W = 3072
OTHER_W = 2816
IN_W = GATES_W + OTHER_W
O_FQ, O_FK, O_FV, O_SQ, O_SK, O_SV, O_CQ, O_CKV, O_MISC = 0, 512, 1024, 1536, 2048, 2176, 2304, 2560, 2688
FF_LANE = 32

NEG = -1e30


def _dot(a, b):
    return jnp.dot(a, b, preferred_element_type=F32)


def _dot_nt(a, b):
    return lax.dot_general(a, b, (((1,), (1,)), ((), ())), preferred_element_type=F32)


def _dot_tn(a, b):
    return lax.dot_general(a, b, (((0,), (0,)), ((), ())), preferred_element_type=F32)


def _params(sem):
    return pltpu.CompilerParams(dimension_semantics=sem, vmem_limit_bytes=VMEM_LIMIT)


def _rms(x, g):
    return x * lax.rsqrt(jnp.mean(x * x, axis=-1, keepdims=True) + EPS) * g


def _split_dot(x, m, pieces=2):
    acc, rest = None, x
    for _ in range(pieces):
        part = rest.astype(BF16)
        rest = rest - part.astype(F32)
        acc = _dot(part, m) if acc is None else acc + _dot(part, m)
    return acc


@jax.custom_vjp
def _sel(x, m, mt):
    return _split_dot(x, m)


_sel.defvjp(lambda x, m, mt: (_split_dot(x, m), (m, mt)), lambda res, dy: (_split_dot(dy, res[1]), None, None))


@jax.custom_vjp
def _mm(x, w):
    return _dot(x.astype(BF16), w.astype(BF16))


def _mm_bwd(res, dy):
    x, w = res
    dyb = dy.astype(BF16)
    return _dot_nt(dyb, w.astype(BF16)), _dot_tn(x.astype(BF16), dyb)


_mm.defvjp(lambda x, w: (_mm(x, w), (x, w)), _mm_bwd)


def _rot_impl(x):
    w = x.shape[1]
    lane = lax.broadcasted_iota(jnp.int32, x.shape, 1) % 128
    lo = (lane >= 64) & (lane < 80)
    hi = (lane >= 80) & (lane < 96)
    return jnp.where(hi, pltpu.roll(x, 16, 1), 0.0) - jnp.where(lo, pltpu.roll(x, w - 16, 1), 0.0)


@jax.custom_vjp
def _rot(x):
    return _rot_impl(x)


_rot.defvjp(lambda x: (_rot_impl(x), None), lambda _, dy: (-_rot_impl(dy),))


def _gnorm(x, g, e, et, dim):
    inv = lax.rsqrt(_sel(x * x, e, et) * (1.0 / dim) + EPS)
    return x * _sel(inv, et, e) * g


def _indicator(width, period):
    m = np.zeros((width, 128), np.float32)
    m[np.arange(width), np.arange(width) // period] = 1.0
    return m


def _consts():
    e64 = _indicator(512, 64)
    e128 = _indicator(1024, 128)
    sk = np.zeros((128, 1024), np.float32)
    for h in range(HEADS):
        sk[np.arange(32), 128 * h + 64 + np.arange(32)] = 1.0
    dup = np.zeros((128, 256), np.float32)
    for g in range(2):
        for r in range(2):
            dup[64 * g + np.arange(64), 128 * g + 64 * r + np.arange(64)] = 1.0
    mats = [e64, e64.T, e128, e128.T, sk, sk.T, dup, dup.T]
    return [jnp.asarray(m, BF16) for m in mats]


def _fold_matrix(width, period):
    m = np.zeros((width, 128), np.float32)
    m[np.arange(width), np.arange(width) % period] = 1.0
    return jnp.asarray(m, BF16)


def _rope_tables(lp):
    half = 16
    freqs = ROPE_THETA ** (-np.arange(half, dtype=np.float32) / half)
    ang = np.arange(lp, dtype=np.float32)[:, None] * freqs[None, :]
    cos = np.ones((lp, 128), np.float32)
    sin = np.zeros((lp, 128), np.float32)
    cos[:, 64:80] = np.cos(ang)
    cos[:, 80:96] = np.cos(ang)
    sin[:, 64:80] = np.sin(ang)
    sin[:, 80:96] = np.sin(ang)
    return jnp.asarray(cos), jnp.asarray(sin)


def _norm_matmul(h, g, w, tn, name):
    lp, d = h.shape
    n = w.shape[1]
    tb = TILE_MM

    def body(h_ref, g_ref, w_ref, xn_ref, y_ref):
        @pl.when(pl.program_id(1) == 0)
        def _():
            xn_ref[...] = _rms(h_ref[...], g_ref[...]).astype(BF16)

        y_ref[...] = _dot(xn_ref[...], w_ref[...])

    return pl.pallas_call(
        body, name=name, grid=(lp // tb, n // tn),
        in_specs=[pl.BlockSpec((tb, d), lambda i, j: (i, 0)), pl.BlockSpec((1, d), lambda i, j: (0, 0)),
                  pl.BlockSpec((d, tn), lambda i, j: (0, j))],
        out_specs=[pl.BlockSpec((tb, d), lambda i, j: (i, 0)), pl.BlockSpec((tb, tn), lambda i, j: (i, j))],
        out_shape=[SDS((lp, d), BF16), SDS((lp, n), F32)],
        compiler_params=_params(("parallel", "arbitrary")),
    )(h, g, w)


def _matmul_residual(a, w, res, name):
    m, k = a.shape
    n = w.shape[1]
    tb = TILE_MM

    def body(a_ref, w_ref, r_ref, o_ref):
        o_ref[...] = r_ref[...] + _dot(a_ref[...], w_ref[...])

    return pl.pallas_call(
        body, name=name, grid=(m // tb,),
        in_specs=[pl.BlockSpec((tb, k), lambda i: (i, 0)), pl.BlockSpec((k, n), lambda i: (0, 0)),
                  pl.BlockSpec((tb, n), lambda i: (i, 0))],
        out_specs=pl.BlockSpec((tb, n), lambda i: (i, 0)),
        out_shape=SDS((m, n), F32),
        compiler_params=_params(("parallel",)),
    )(a, w, res)


def _matmul_nt(dy, w, tn, name):
    m, k = dy.shape
    n = w.shape[0]
    tb = TILE_MM

    def body(dy_ref, w_ref, o_ref):
        o_ref[...] = _dot_nt(dy_ref[...].astype(BF16), w_ref[...])

    return pl.pallas_call(
        body, name=name, grid=(m // tb, n // tn),
        in_specs=[pl.BlockSpec((tb, k), lambda i, j: (i, 0)), pl.BlockSpec((tn, k), lambda i, j: (j, 0))],
        out_specs=pl.BlockSpec((tb, tn), lambda i, j: (i, j)),
        out_shape=SDS((m, n), F32),
        compiler_params=_params(("parallel", "arbitrary")),
    )(dy, w)


def _matmul_tn(x, dy, tn, name):
    m, k = x.shape
    n = dy.shape[1]
    tb = TILE_MM
    nb = m // tb

    def body(x_ref, dy_ref, o_ref, acc):
        i = pl.program_id(1)

        @pl.when(i == 0)
        def _():
            acc[...] = jnp.zeros_like(acc)

        acc[...] += _dot_tn(x_ref[...].astype(BF16), dy_ref[...].astype(BF16))

        @pl.when(i == nb - 1)
        def _():
            o_ref[...] = acc[...].astype(BF16)

    return pl.pallas_call(
        body, name=name, grid=(n // tn, nb),
        in_specs=[pl.BlockSpec((tb, k), lambda j, i: (i, 0)), pl.BlockSpec((tb, tn), lambda j, i: (i, j))],
        out_specs=pl.BlockSpec((k, tn), lambda j, i: (0, j)),
        out_shape=SDS((k, n), BF16),
        scratch_shapes=[pltpu.VMEM((k, tn), F32)],
        compiler_params=_params(("parallel", "arbitrary")),
    )(x, dy)


def _norm_matmul_bwd(dys, w, x, g, dres, name):
    m, d = x.shape
    tb = TILE_MM
    widths = [a.shape[1] for a in dys]
    n_dy = len(dys)

    def body(*refs):
        w_ref, x_ref, g_ref, r_ref, o_ref, dg_ref = refs[n_dy:]

        @pl.when(pl.program_id(0) == 0)
        def _():
            dg_ref[...] = jnp.zeros_like(dg_ref)

        dxn, off = None, 0
        for dy_ref, width in zip(refs[:n_dy], widths):
            part = _dot_nt(dy_ref[...], w_ref[:, off:off + width])
            dxn = part if dxn is None else dxn + part
            off += width
        _, vjp = jax.vjp(_rms, x_ref[...], g_ref[...])
        dx, dg = vjp(dxn)
        o_ref[...] = r_ref[...] + dx
        dg_ref[...] += dg

    row = pl.BlockSpec((tb, d), lambda i: (i, 0))
    vec = pl.BlockSpec((1, d), lambda i: (0, 0))
    return pl.pallas_call(
        body, name=name, grid=(m // tb,),
        in_specs=[pl.BlockSpec((tb, wd), lambda i: (i, 0)) for wd in widths]
        + [pl.BlockSpec(w.shape, lambda i: (0, 0)), row, vec, row],
        out_specs=[row, vec],
        out_shape=[SDS((m, d), F32), SDS((1, d), F32)],
        compiler_params=_params(("arbitrary",)),
    )(*dys, w, x, g, dres)


def _prep_math(pieces, prm, consts, cos, sin):
    fq, fk, sq, sk, sv, cq, ckv, misc = pieces
    gfq, gfk, gsq, gsk, fb, gqa, gkva, gmq, gmk, wq, wkk, wkv = prm
    e64, e64t, e128, e128t, skm, skt, dup, dupt = consts
    cos8 = jnp.concatenate([cos] * HEADS, axis=1)
    sin8 = jnp.concatenate([sin] * HEADS, axis=1)
    fq_n = _gnorm(fq, gfq, e64, e64t, 64)
    fk_n = _gnorm(fk, gfk, e64, e64t, 64)
    ls = jax.nn.log_sigmoid(misc + fb)
    q = _gnorm(_mm(_rms(cq, gqa), wq), gmq, e128, e128t, 96)
    mq = q * cos8 + _rot(q) * sin8
    kva = _rms(ckv, gkva)
    k = _gnorm(_mm(kva, wkk) + _sel(misc, skm, skt), gmk, e128, e128t, 96)
    mk = k * cos8 + _rot(k) * sin8
    mv = _mm(kva, wkv)
    sq_n = _gnorm(sq, gsq, e64, e64t, 64)
    sk_n = _gnorm(sk, gsk, e64[0:128], e64t[:, 0:128], 64)
    skd = _sel(sk_n, dup, dupt)
    svd = _sel(sv, dup, dupt)
    return fq_n, fk_n, ls, mq, mk, mv, sq_n, skd, svd


_PIECES = [(O_FQ, 512), (O_FK, 512), (O_SQ, 512), (O_SK, 128), (O_SV, 128), (O_CQ, 256), (O_CKV, 128), (O_MISC, 128)]
_PRM_SHAPES = [(1, 512), (1, 512), (1, 512), (1, 128), (1, 128), (1, 256), (1, 128), (1, 1024), (1, 1024),
               (256, 1024), (128, 1024), (128, 512)]
_CONST_SHAPES = [(512, 128), (128, 512), (1024, 128), (128, 1024), (128, 1024), (1024, 128), (128, 256), (256, 128)]


def _piece_specs(tb):
    def spec(off, width):
        blk = (GATES_W + off) // width
        return pl.BlockSpec((tb, width), lambda i, blk=blk: (i, blk))
    return [spec(o, w) for o, w in _PIECES] + [spec(O_FV, 512)]


def _full_specs(shapes):
    return [pl.BlockSpec(s, lambda i: (0, 0)) for s in shapes]


def _prep_fwd(proj, prm, consts, cos, sin, name):
    lp = proj.shape[0]
    tb = TILE_ROW
    row = lambda w: pl.BlockSpec((tb, w), lambda i: (i, 0))

    def body(*refs):
        pieces = [r[...] for r in refs[0:8]]
        fv = refs[8][...]
        prm_v = [r[...] for r in refs[9:21]]
        consts_v = [r[...] for r in refs[21:29]]
        cos_v, sin_v = refs[29][...], refs[30][...]
        outs = refs[31:]
        fq_n, fk_n, ls, mq, mk, mv, sq_n, skd, svd = _prep_math(pieces, prm_v, consts_v, cos_v, sin_v)
        for ref, val in zip(outs, (fq_n, fk_n, fv, mq, mk, mv, sq_n, skd, svd)):
            ref[...] = val.astype(BF16)
        outs[9][...] = ls

    widths = [512, 512, 512, 1024, 1024, 512, 512, 256, 256]
    return pl.pallas_call(
        body, name=name, grid=(lp // tb,),
        in_specs=_piece_specs(tb) + _full_specs(_PRM_SHAPES) + _full_specs(_CONST_SHAPES) + [row(128), row(128)],
        out_specs=[row(w) for w in widths] + [row(128)],
        out_shape=[SDS((lp, w), BF16) for w in widths] + [SDS((lp, 128), F32)],
        compiler_params=_params(("parallel",)),
    )(*([proj] * 9), *prm, *consts, cos, sin)


def _prep_bwd(proj, prm, consts, cos, sin, cots, folds, name):
    lp = proj.shape[0]
    tb = TILE_ROW
    row = lambda w: pl.BlockSpec((tb, w), lambda i: (i, 0))
    fold64, fold128 = folds

    def body(*refs):
        pieces = [r[...] for r in refs[0:8]]
        prm_v = [r[...] for r in refs[9:21]]
        consts_v = [r[...] for r in refs[21:29]]
        cos_v, sin_v = refs[29][...], refs[30][...]
        dfq, dfk, dfv, dmq, dmk, dmv, dsq, dskd, dsvd, dls = [r[...] for r in refs[31:41]]
        f64, f128 = refs[41][...], refs[42][...]
        d_ref = refs[43]
        g_refs = refs[44:]

        @pl.when(pl.program_id(0) == 0)
        def _():
            for r in g_refs:
                r[...] = jnp.zeros_like(r)

        f = lambda pc, pr: _prep_math(pc, pr, consts_v, cos_v, sin_v)
        _, vjp = jax.vjp(f, pieces, prm_v)
        dpc, dprm = vjp((dfq, dfk, dls, dmq, dmk, dmv, dsq, dskd, dsvd))
        d_fq, d_fk, d_sq, d_sk, d_sv, d_cq, d_ckv, d_misc = dpc
        for off, val in ((O_FQ, d_fq), (O_FK, d_fk), (O_FV, dfv), (O_SQ, d_sq), (O_SK, d_sk), (O_SV, d_sv),
                         (O_CQ, d_cq), (O_CKV, d_ckv), (O_MISC, d_misc)):
            d_ref[:, off:off + val.shape[1]] = val.astype(BF16)
        folded = {0: f64, 1: f64, 2: f64, 3: f64[0:128], 7: f128, 8: f128}
        for idx, (ref, val) in enumerate(zip(g_refs, dprm)):
            if idx in folded:
                ref[...] += _split_dot(jnp.broadcast_to(val, (8, val.shape[1])), folded[idx], 3)
            elif val.shape[0] == 1:
                ref[...] += jnp.broadcast_to(val, ref.shape)
            else:
                ref[...] += val

    g_shapes = [(8, 128), (8, 128), (8, 128), (8, 128), (8, 128), (8, 256), (8, 128), (8, 128), (8, 128),
                (256, 1024), (128, 1024), (128, 512)]
    cot_widths = [512, 512, 512, 1024, 1024, 512, 512, 256, 256, 128]
    return pl.pallas_call(
        body, name=name, grid=(lp // tb,),
        in_specs=(_piece_specs(tb) + _full_specs(_PRM_SHAPES) + _full_specs(_CONST_SHAPES) + [row(128), row(128)]
                  + [row(w) for w in cot_widths] + _full_specs([(512, 128), (1024, 128)])),
        out_specs=[row(OTHER_W)] + _full_specs(g_shapes),
        out_shape=[SDS((lp, OTHER_W), BF16)] + [SDS(s, F32) for s in g_shapes],
        compiler_params=_params(("arbitrary",)),
    )(*([proj] * 9), *prm, *consts, cos, sin, *cots, fold64, fold128)


def _cumsum(xs, reverse, name):
    lp = xs[0].shape[0]
    tb = TILE_MM
    nb = lp // tb
    n_in = len(xs)
    idx = (lambda i: (nb - 1 - i, 0)) if reverse else (lambda i: (i, 0))

    def body(*refs):
        o_ref, carry = refs[n_in], refs[n_in + 1]

        @pl.when(pl.program_id(0) == 0)
        def _():
            carry[...] = jnp.zeros_like(carry)

        x = refs[0][...]
        for r in refs[1:n_in]:
            x = x + r[...]
        r_i = lax.broadcasted_iota(jnp.int32, (tb, tb), 0)
        c_i = lax.broadcasted_iota(jnp.int32, (tb, tb), 1)
        tri = ((c_i >= r_i) if reverse else (c_i <= r_i)).astype(BF16)
        acc, rest = None, x
        for _ in range(3):
            part = rest.astype(BF16)
            rest = rest - part.astype(F32)
            acc = _dot(tri, part) if acc is None else acc + _dot(tri, part)
        o_ref[...] = acc + carry[...]
        carry[...] += jnp.sum(x, axis=0, keepdims=True)

    return pl.pallas_call(
        body, name=name, grid=(nb,),
        in_specs=[pl.BlockSpec((tb, 128), idx)] * n_in,
        out_specs=pl.BlockSpec((tb, 128), idx),
        out_shape=SDS((lp, 128), F32),
        scratch_shapes=[pltpu.VMEM((1, 128), F32)],
        compiler_params=_params(("arbitrary",)),
    )(*xs)


class _Att:
    def __init__(self, mode):
        self.mode = mode
        self.wide = mode == "mla"
        self.qw = 256 if self.wide else 128
        self.scale = (96 if mode == "mla" else 64) ** -0.5

    def resident(self, x, lo, scaled):
        if self.wide:
            return x[:, 0:128], x[:, 128:256]
        if scaled:
            x = x * jnp.asarray(self.scale, x.dtype)
        zero = jnp.zeros_like(x)
        return jnp.where(lo, x, zero), jnp.where(lo, zero, x)

    def moving(self, x):
        return (x[:, 0:128], x[:, 128:256]) if self.wide else (x, x)

    def logits(self, a, b, qpos, kpos, key_decay, slope, masked):
        s = _dot_nt(a, b)
        if self.wide:
            s = s * self.scale
        if self.mode == "fox":
            s = s - key_decay
        if self.mode == "swa":
            s = s - slope * (qpos - kpos).astype(F32)
        if masked:
            ok = kpos <= qpos
            if self.mode == "swa":
                ok = ok & ((kpos < N_META) | (qpos - kpos < WINDOW))
            s = jnp.where(ok, s, NEG)
        return s


def _as_rows(col):
    return jnp.broadcast_to(col, (col.shape[0], 128)).T[0:8, :]


def _halves(x, lo):
    zero = jnp.zeros_like(x)
    return jnp.where(lo, x, zero), jnp.where(lo, zero, x)


def _kv_specs(att, lp, rows):
    if att.mode == "swa":
        return (pl.BlockSpec((rows, 128), lambda g, i: (i if rows != lp else 0, g)),) * 2
    return (pl.BlockSpec((rows, PAIRS * att.qw), lambda g, i: (i if rows != lp else 0, g)),
            pl.BlockSpec((rows, PAIRS * 128), lambda g, i: (i if rows != lp else 0, g)))


def _pair_cols(att, x, pp, width):
    return x if x.shape[1] == width else x[:, pp * width:(pp + 1) * width]


def _att_fwd(att, q, k, v, extra, name):
    lp = q.shape[0]
    t = TILE_ATT
    nq = lp // t
    qw = att.qw
    mode = att.mode
    nh = 2 * PAIRS

    def body(*refs):
        q_ref, k_ref, v_ref = refs[0:3]
        o_ref, lse_ref = refs[-2:]
        g, qi = pl.program_id(0), pl.program_id(1)
        lo = lax.broadcasted_iota(jnp.int32, (1, 128), 1) < 64
        q_all = q_ref[...]
        q_heads = [h for pp in range(PAIRS) for h in att.resident(_pair_cols(att, q_all, pp, qw), lo, True)]
        qpos = qi * t + lax.broadcasted_iota(jnp.int32, (t, 1), 0)

        def step(first, cols, carry, masked):
            ks = pl.multiple_of(first, 128)
            kc, vc = k_ref[pl.ds(ks, cols), :], v_ref[pl.ds(ks, cols), :]
            kpos = first + lax.broadcasted_iota(jnp.int32, (1, cols), 1)
            out = []
            for h in range(nh):
                pp = h // 2
                m, l, acc = carry[3 * h:3 * h + 3]
                k_h = att.moving(_pair_cols(att, kc, pp, qw))[h % 2]
                decay = refs[3][h, :, pl.ds(ks, cols)] if mode == "fox" else None
                slope = refs[4][nh * g + h] if mode == "swa" else None
                s = att.logits(q_heads[h], k_h, qpos, kpos, decay, slope, masked)
                m_new = jnp.maximum(m, jnp.max(s, axis=-1, keepdims=True))
                alpha = jnp.exp(m - m_new)
                pe = jnp.exp(s - m_new)
                l = alpha * l + jnp.sum(pe, axis=-1, keepdims=True)
                acc = alpha * acc + _dot(pe.astype(BF16), _pair_cols(att, vc, pp, 128))
                out += [m_new, l, acc]
            return tuple(out)

        init = []
        for h in range(nh):
            if mode == "swa":
                init += [jnp.full((t, 1), refs[3][nh * g + h], F32), jnp.ones((t, 1), F32)]
            else:
                init += [jnp.full((t, 1), NEG, F32), jnp.zeros((t, 1), F32)]
            init.append(jnp.zeros((t, 128), F32))
        if mode == "swa":
            band = jnp.maximum(qi * t - WINDOW, 0)
            carry = lax.fori_loop(0, (band >= 128).astype(jnp.int32), lambda j, c: step(0, 128, c, True), tuple(init))
            carry = step(band, t + WINDOW, carry, True)
        else:
            carry = lax.fori_loop(0, qi // 2, lambda j, c: step(2 * j * t, 2 * t, c, False), tuple(init))
            carry = lax.fori_loop(0, qi % 2, lambda j, c: step((qi - 1) * t, t, c, False), carry)
            carry = step(qi * t, t, carry, True)
        outs = []
        for pp in range(PAIRS):
            (ma, la, acca), (mb, lb, accb) = carry[6 * pp:6 * pp + 3], carry[6 * pp + 3:6 * pp + 6]
            outs.append(jnp.where(lo, acca / la, accb / lb).astype(BF16))
            lse_ref[2 * pp] = ma + jnp.log(la)
            lse_ref[2 * pp + 1] = mb + jnp.log(lb)
        o_ref[...] = jnp.concatenate(outs, axis=1)

    in_specs = [pl.BlockSpec((t, PAIRS * qw), lambda g, i: (i, g)), *_kv_specs(att, lp, lp)]
    if mode == "fox":
        in_specs += [pl.BlockSpec((nh, 1, lp), lambda g, i: (g, 0, 0))]
    if mode == "swa":
        in_specs += [pl.BlockSpec(memory_space=pltpu.SMEM)] * 2
    return pl.pallas_call(
        body, name=name, grid=(4 // PAIRS, nq), in_specs=in_specs,
        out_specs=[pl.BlockSpec((t, PAIRS * 128), lambda g, i: (i, g)), pl.BlockSpec((nh, t, 1), lambda g, i: (g, i, 0))],
        out_shape=[SDS((lp, 512), BF16), SDS((HEADS, lp, 1), F32)],
        compiler_params=_params(("parallel", "arbitrary")),
    )(q, k, v, *extra)


def _att_bwd(att, q, k, v, o, do, lse, extra, name):
    lp = q.shape[0]
    t = TILE_ATT
    nq = lp // t
    qw = att.qw
    mode = att.mode
    nh = 2 * PAIRS
    kw = 128 if mode == "swa" else PAIRS * qw
    vw = 128 if mode == "swa" else PAIRS * 128

    def body(*refs):
        q_ref, k_ref, v_ref, o_ref, do_ref, lse_ref = refs[0:6]
        n_out = {"fox": 5, "mla": 3, "swa": 4}[mode]
        outs = refs[len(refs) - n_out:]
        dq_ref, dk_ref, dv_ref = outs[0:3]
        g, qi = pl.program_id(0), pl.program_id(1)

        @pl.when(qi == 0)
        def _():
            dk_ref[...] = jnp.zeros_like(dk_ref)
            dv_ref[...] = jnp.zeros_like(dv_ref)
            if mode == "fox":
                outs[4][...] = jnp.zeros_like(outs[4])

        lo = lax.broadcasted_iota(jnp.int32, (1, 128), 1) < 64
        q_all, do_all = q_ref[...], do_ref[...]
        prod = do_all.astype(F32) * o_ref[...].astype(F32)
        q_heads, q_plain, do_heads, do_pairs, delta = [], [], [], [], []
        for pp in range(PAIRS):
            q_pp = _pair_cols(att, q_all, pp, qw)
            q_heads += att.resident(q_pp, lo, True)
            q_plain += att.moving(q_pp)
            do_pp = _pair_cols(att, do_all, pp, 128)
            do_pairs.append(do_pp)
            do_heads += _halves(do_pp, lo)
            pr_pp = _pair_cols(att, prod, pp, 128)
            delta += [jnp.sum(jnp.where(lo, pr_pp, 0.0), axis=-1, keepdims=True),
                      jnp.sum(jnp.where(lo, 0.0, pr_pp), axis=-1, keepdims=True)]
        lse_v = [lse_ref[h] for h in range(nh)]
        qpos = qi * t + lax.broadcasted_iota(jnp.int32, (t, 1), 0)

        def step(first, cols, carry, masked):
            ks = pl.multiple_of(first, 128)
            kc, vc = k_ref[pl.ds(ks, cols), :], v_ref[pl.ds(ks, cols), :]
            kpos = first + lax.broadcasted_iota(jnp.int32, (1, cols), 1)
            out, dk_parts, dv_parts = [], [], []
            for h in range(nh):
                pp = h // 2
                k_h = att.moving(_pair_cols(att, kc, pp, qw))[h % 2]
                decay = refs[6][h, :, pl.ds(ks, cols)] if mode == "fox" else None
                slope = refs[7][nh * g + h] if mode == "swa" else None
                s = att.logits(q_heads[h], k_h, qpos, kpos, decay, slope, masked)
                pr = jnp.exp(s - lse_v[h])
                ds = pr * (_dot_nt(do_heads[h], _pair_cols(att, vc, pp, 128)) - delta[h])
                dsb = ds.astype(BF16)
                out.append(carry[2 * h] + _dot(dsb, k_h))
                out.append(carry[2 * h + 1] + jnp.sum(ds, axis=-1, keepdims=True) if mode == "fox" else carry[2 * h + 1])
                dk_parts.append(_dot_tn(dsb, q_plain[h]))
                dv_parts.append(_dot_tn(pr.astype(BF16), do_pairs[pp]))
                if mode == "fox":
                    outs[4][h, 0:1, pl.ds(ks, cols)] -= jnp.sum(ds, axis=0, keepdims=True)
            rows = pl.ds(ks, cols)
            for pp in range(PAIRS):
                dv_pp = jnp.where(lo, dv_parts[2 * pp], dv_parts[2 * pp + 1])
                if att.wide:
                    dk_pp = jnp.concatenate(dk_parts[2 * pp:2 * pp + 2], axis=1) * att.scale
                else:
                    dk_pp = jnp.where(lo, dk_parts[2 * pp], dk_parts[2 * pp + 1]) * att.scale
                if mode == "swa":
                    dk_ref[rows, :] += dk_pp
                    dv_ref[rows, :] += dv_pp
                else:
                    dk_ref[rows, pp * qw:(pp + 1) * qw] += dk_pp
                    dv_ref[rows, pp * 128:(pp + 1) * 128] += dv_pp
            return tuple(out)

        init = (jnp.zeros((t, 128), F32), jnp.zeros((t, 1), F32)) * nh
        if mode == "swa":
            band = jnp.maximum(qi * t - WINDOW, 0)
            carry = lax.fori_loop(0, (band >= 128).astype(jnp.int32), lambda j, c: step(0, 128, c, True), init)
            carry = step(band, t + WINDOW, carry, True)
        else:
            carry = lax.fori_loop(0, qi // 2, lambda j, c: step(2 * j * t, 2 * t, c, False), init)
            carry = lax.fori_loop(0, qi % 2, lambda j, c: step((qi - 1) * t, t, c, False), carry)
            carry = step(qi * t, t, carry, True)
        dq = []
        for pp in range(PAIRS):
            dqa, dca, dqb, dcb = carry[4 * pp:4 * pp + 4]
            dq += [dqa, dqb] if att.wide else [jnp.where(lo, dqa, dqb)]
            if mode == "fox":
                outs[3][2 * pp] = _as_rows(dca)
                outs[3][2 * pp + 1] = _as_rows(dcb)
        dq_ref[...] = jnp.concatenate(dq, axis=1) * att.scale
        if mode == "swa":
            ds_ref = outs[3]

            @pl.when(qi == 0)
            def _():
                ds_ref[...] = jnp.zeros_like(ds_ref)

            lane = lax.broadcasted_iota(jnp.int32, (8, 128), 1)
            acc = jnp.zeros((8, 128), F32)
            for h in range(nh):
                tot = -jnp.sum(jnp.exp(refs[6][nh * g + h] - lse_v[h]) * delta[h])
                acc = acc + jnp.where(lane == h, tot, 0.0)
            ds_ref[0] += acc

    col = pl.BlockSpec((nh, t, 1), lambda g, i: (g, i, 0))
    in_specs = [pl.BlockSpec((t, PAIRS * qw), lambda g, i: (i, g)), *_kv_specs(att, lp, lp),
                pl.BlockSpec((t, PAIRS * 128), lambda g, i: (i, g)), pl.BlockSpec((t, PAIRS * 128), lambda g, i: (i, g)), col]
    out_specs = [pl.BlockSpec((t, PAIRS * qw), lambda g, i: (i, g)), pl.BlockSpec((lp, kw), lambda g, i: (0, g)),
                 pl.BlockSpec((lp, vw), lambda g, i: (0, g))]
    n_groups = 4 // PAIRS
    out_shape = [SDS((lp, 4 * qw), F32), SDS((lp, n_groups * kw), F32), SDS((lp, n_groups * vw), F32)]
    if mode == "fox":
        in_specs += [pl.BlockSpec((nh, 1, lp), lambda g, i: (g, 0, 0))]
        out_specs += [pl.BlockSpec((nh, 8, t), lambda g, i: (g, 0, i)), pl.BlockSpec((nh, 8, lp), lambda g, i: (g, 0, 0))]
        out_shape += [SDS((HEADS, 8, lp), F32)] * 2
    if mode == "swa":
        in_specs += [pl.BlockSpec(memory_space=pltpu.SMEM)] * 2
        out_specs.append(pl.BlockSpec((1, 8, 128), lambda g, i: (g, 0, 0)))
        out_shape.append(SDS((n_groups, 8, 128), F32))
    return pl.pallas_call(
        body, name=name, grid=(n_groups, nq), in_specs=in_specs, out_specs=out_specs, out_shape=out_shape,
        compiler_params=_params(("parallel", "arbitrary")),
    )(q, k, v, o, do, lse, *extra)


def _post_fwd(h, proj, outs, wb, wo, name):
    lp, d = h.shape
    tb = TILE_POST
    row = lambda w: pl.BlockSpec((tb, w), lambda i: (i, 0))

    def body(h_ref, g0, g1, g2, oa, ob, oc, wb_ref, wo_ref, o_ref):
        merged = jnp.zeros((tb, d), F32)
        for n, (g_ref, br) in enumerate(((g0, oa), (g1, ob), (g2, oc))):
            merged = merged + jax.nn.sigmoid(g_ref[...]) * _dot(br[...], wb_ref[n])
        o_ref[...] = h_ref[...] + _dot(merged.astype(BF16), wo_ref[...])

    gate = lambda n: pl.BlockSpec((tb, d), lambda i, n=n: (i, n))
    return pl.pallas_call(
        body, name=name, grid=(lp // tb,),
        in_specs=[row(d), gate(0), gate(1), gate(2), row(512), row(512), row(512),
                  pl.BlockSpec((3, 512, d), lambda i: (0, 0, 0)), pl.BlockSpec((d, d), lambda i: (0, 0))],
        out_specs=row(d), out_shape=SDS((lp, d), F32),
        compiler_params=_params(("parallel",)),
    )(h, proj, proj, proj, *outs, wb, wo)


def _post_bwd(dh, proj, outs, wb, wo, name):
    lp, d = dh.shape
    tb = TILE_POST
    row = lambda w: pl.BlockSpec((tb, w), lambda i: (i, 0))

    def body(dh_ref, g0, g1, g2, oa, ob, oc, wb_ref, wo_ref, dg_ref, doa, dob, doc, dwb_ref, dwo_ref):
        @pl.when(pl.program_id(0) == 0)
        def _():
            dwb_ref[...] = jnp.zeros_like(dwb_ref)
            dwo_ref[...] = jnp.zeros_like(dwo_ref)

        dhb = dh_ref[...].astype(BF16)
        dm = _dot_nt(dhb, wo_ref[...])
        merged = jnp.zeros((tb, d), F32)
        for n, (g_ref, br, do_ref) in enumerate(((g0, oa, doa), (g1, ob, dob), (g2, oc, doc))):
            gate = jax.nn.sigmoid(g_ref[...])
            o_n = br[...]
            y = _dot(o_n, wb_ref[n])
            merged = merged + gate * y
            dy = (dm * gate).astype(BF16)
            dg_ref[:, n * d:(n + 1) * d] = (dm * y * gate * (1.0 - gate)).astype(BF16)
            do_ref[...] = _dot_nt(dy, wb_ref[n]).astype(BF16)
            dwb_ref[n] += _dot_tn(o_n, dy)
        dwo_ref[...] += _dot_tn(merged.astype(BF16), dhb)

    gate = lambda n: pl.BlockSpec((tb, d), lambda i, n=n: (i, n))
    wb_spec = pl.BlockSpec((3, 512, d), lambda i: (0, 0, 0))
    wo_spec = pl.BlockSpec((d, d), lambda i: (0, 0))
    return pl.pallas_call(
        body, name=name, grid=(lp // tb,),
        in_specs=[row(d), gate(0), gate(1), gate(2), row(512), row(512), row(512), wb_spec, wo_spec],
        out_specs=[row(GATES_W), row(512), row(512), row(512), wb_spec, wo_spec],
        out_shape=[SDS((lp, GATES_W), BF16)] + [SDS((lp, 512), BF16)] * 3 + [SDS((3, 512, d), F32), SDS((d, d), F32)],
        compiler_params=_params(("arbitrary",)),
    )(dh, proj, proj, proj, *outs, wb, wo)


def _shift_down(x, halo, n, first):
    rows = lax.broadcasted_iota(jnp.int32, x.shape, 0)
    edge = jnp.concatenate([pltpu.roll(halo, n, 0), jnp.zeros((x.shape[0] - 8, x.shape[1]), F32)], axis=0)
    edge = jnp.where(first, 0.0, edge)
    return jnp.where(rows < n, edge, pltpu.roll(x, n, 0))


def _shift_up(x, halo, n, last):
    tb = x.shape[0]
    rows = lax.broadcasted_iota(jnp.int32, x.shape, 0)
    edge = jnp.concatenate([jnp.zeros((tb - 8, x.shape[1]), F32), pltpu.roll(halo, 8 - n, 0)], axis=0)
    edge = jnp.where(last, 0.0, edge)
    return jnp.where(rows >= tb - n, edge, pltpu.roll(x, tb - n, 0))


def _conv(u, halo, w_ref, b_ref, first):
    taps = (_shift_down(u, halo, 2, first), _shift_down(u, halo, 1, first), u)
    c = b_ref[...] + w_ref[0:1, :] * taps[0] + w_ref[1:2, :] * taps[1] + w_ref[2:3, :] * taps[2]
    return c, taps


def _ffn_specs(tb, f):
    hb = tb // 8
    cur = lambda c: pl.BlockSpec((tb, f), lambda i, c=c: (i, c))
    prev = lambda c: pl.BlockSpec((8, f), lambda i, c=c: (jnp.maximum(i * hb - 1, 0), c))
    vec = lambda r, c: pl.BlockSpec((r, f), lambda i, c=c: (0, c))
    return cur, prev, vec


def _ffn_act_fwd(u, cw, cb, name):
    lp = u.shape[0]
    f = D_FF
    tb = TILE_ROW
    cur, prev, vec = _ffn_specs(tb, f)

    def body(ug, uv, hg, hv, wg, wv, bg, bv, o_ref):
        first = pl.program_id(0) == 0
        cg, _ = _conv(ug[...], hg[...], wg, bg, first)
        cv, _ = _conv(uv[...], hv[...], wv, bv, first)
        o_ref[...] = (cg * jax.nn.sigmoid(cg) * cv).astype(BF16)

    return pl.pallas_call(
        body, name=name, grid=(lp // tb,),
        in_specs=[cur(0), cur(1), prev(0), prev(1), vec(8, 0), vec(8, 1), vec(1, 0), vec(1, 1)],
        out_specs=pl.BlockSpec((tb, f), lambda i: (i, 0)), out_shape=SDS((lp, f), BF16),
        compiler_params=_params(("parallel",)),
    )(u, u, u, u, cw, cw, cb, cb)


def _ffn_act_bwd_conv(u, dact, cw, cb, name):
    lp = u.shape[0]
    f = D_FF
    tb = TILE_CONV_BWD
    cur, prev, vec = _ffn_specs(tb, f)

    def body(ug, uv, hg, hv, wg, wv, bg, bv, da_ref, dc_ref, dw_ref, db_ref):
        first = pl.program_id(0) == 0

        @pl.when(first)
        def _():
            dw_ref[...] = jnp.zeros_like(dw_ref)
            db_ref[...] = jnp.zeros_like(db_ref)

        cg, tg = _conv(ug[...], hg[...], wg, bg, first)
        cv, tv = _conv(uv[...], hv[...], wv, bv, first)
        da = da_ref[...]
        sg = jax.nn.sigmoid(cg)
        dcg = da * cv * sg * (1.0 + cg * (1.0 - sg))
        dcv = da * cg * sg
        for c, (dc, taps) in enumerate(((dcg, tg), (dcv, tv))):
            dc_ref[:, c * f:(c + 1) * f] = dc
            for n in range(3):
                dw_ref[n:n + 1, c * f:(c + 1) * f] += jnp.sum(dc * taps[n], axis=0, keepdims=True)
            db_ref[0:1, c * f:(c + 1) * f] += jnp.sum(dc, axis=0, keepdims=True)

    acc = pl.BlockSpec((8, 2 * f), lambda i: (0, 0))
    return pl.pallas_call(
        body, name=name, grid=(lp // tb,),
        in_specs=[cur(0), cur(1), prev(0), prev(1), vec(8, 0), vec(8, 1), vec(1, 0), vec(1, 1),
                  pl.BlockSpec((tb, f), lambda i: (i, 0))],
        out_specs=[pl.BlockSpec((tb, 2 * f), lambda i: (i, 0)), acc, acc],
        out_shape=[SDS((lp, 2 * f), F32), SDS((8, 2 * f), F32), SDS((8, 2 * f), F32)],
        compiler_params=_params(("arbitrary",)),
    )(u, u, u, u, cw, cw, cb, cb, dact)


def _ffn_act_bwd_in(dc, cw, name):
    lp = dc.shape[0]
    f2 = 2 * D_FF
    tb = TILE_ROW
    nb = lp // tb
    hb = tb // 8

    def body(dc_ref, n_ref, w_ref, o_ref):
        last = pl.program_id(0) == nb - 1
        dcv, halo = dc_ref[...], n_ref[...]
        du = (w_ref[2:3, :] * dcv + w_ref[1:2, :] * _shift_up(dcv, halo, 1, last)
              + w_ref[0:1, :] * _shift_up(dcv, halo, 2, last))
        o_ref[...] = du.astype(BF16)

    cur = pl.BlockSpec((tb, f2), lambda i: (i, 0))
    return pl.pallas_call(
        body, name=name, grid=(nb,),
        in_specs=[cur, pl.BlockSpec((8, f2), lambda i: (jnp.minimum((i + 1) * hb, nb * hb - 1), 0)),
                  pl.BlockSpec((8, f2), lambda i: (0, 0))],
        out_specs=cur, out_shape=SDS((lp, f2), BF16),
        compiler_params=_params(("parallel",)),
    )(dc, dc, cw)


def _loss_head(y, target, n_real, name):
    lp, d = y.shape
    tb = TILE_MM

    def body(y_ref, t_ref, dy_ref, loss_ref):
        i = pl.program_id(0)

        @pl.when(i == 0)
        def _():
            loss_ref[...] = jnp.zeros_like(loss_ref)

        rows = i * tb + lax.broadcasted_iota(jnp.int32, (tb, 1), 0)
        real = (rows >= N_META) & (rows < N_META + n_real)
        diff = jnp.where(real, y_ref[...] - t_ref[...], 0.0)
        dy_ref[...] = diff * (1.0 / d)
        loss_ref[...] += (0.5 / d) * jnp.sum(diff * diff).reshape(1, 1)

    row = pl.BlockSpec((tb, d), lambda i: (i, 0))
    return pl.pallas_call(
        body, name=name, grid=(lp // tb,), in_specs=[row, row],
        out_specs=[row, pl.BlockSpec((1, 1), lambda i: (0, 0))],
        out_shape=[SDS((lp, d), F32), SDS((1, 1), F32)],
        compiler_params=_params(("arbitrary",)),
    )(y, target)


def _pad_lanes(v, width, at=0):
    return jnp.pad(v.astype(F32), (at, width - at - v.shape[0]))[None, :]


_IN_COLS = dict(fq=(0, 512), fk=(512, 512), fv=(1024, 512), ff=(1536, 8), cq=(1544, 256), ckv=(1800, 128),
                kr=(1928, 32), sq=(1960, 512), sk=(2472, 128), sv=(2600, 128), gates=(2728, 3072))


def _orig_cols(src, start, width):
    if src.ndim == 2:
        return [src[:, start:start + width]]
    per, out, pos = src.shape[2], [], start
    while pos < start + width:
        d, off = divmod(pos, per)
        take = min(per - off, start + width - pos)
        out.append(src[d, :, off:off + take])
        pos += take
    return out


class _ColumnSegments:
    def __init__(self, segments):
        self.segments = segments

    def full(self):
        return jnp.concatenate([a[:, s:s + w] for a, s, w in self.segments], axis=1)

    def blocks(self, n):
        per = sum(w for _, _, w in self.segments) // n
        out, seg, used = [], 0, 0
        for _ in range(n):
            pieces, need = [], per
            while need:
                a, s, w = self.segments[seg]
                take = min(w - used, need)
                pieces.append(a[:, s + used:s + used + take])
                used, need = used + take, need - take
                if used == w:
                    seg, used = seg + 1, 0
            out.append(jnp.concatenate(pieces, axis=1))
        return jnp.stack(out)


def _mix_params(w, big, l):
    b = lambda a: a.astype(BF16)
    win = big["w_in"]
    order = ("gates", "fq", "fk", "fv", "sq", "sk", "sv", "cq", "ckv", "kr", "ff")
    pieces = [p for name in order for p in _orig_cols(win, *_IN_COLS[name])]
    w_in = b(jnp.concatenate(pieces + [jnp.zeros((D_MODEL, 88), win.dtype)], axis=1))
    wq = jnp.pad(big["mla_w_q_up"].reshape(256, HEADS, 96), ((0, 0), (0, 0), (0, 32))).reshape(256, 1024)
    wkv = big["mla_w_kv_up"].reshape(128, HEADS, 128)
    wkk = jnp.pad(wkv[:, :, :64], ((0, 0), (0, 0), (0, 64))).reshape(128, 1024)
    wkvv = wkv[:, :, 64:].reshape(128, 512)
    tile = lambda g, n: jnp.tile(g.astype(F32), n)[None, :]
    prm = [tile(w["fox_q_g"][l], 8), tile(w["fox_k_g"][l], 8), tile(w["swa_q_g"][l], 8), tile(w["swa_k_g"][l], 2),
           _pad_lanes(w["fox_forget_b"][l], 128, FF_LANE), w["mla_q_a_g"][l][None, :], w["mla_kv_a_g"][l][None, :],
           tile(jnp.pad(w["mla_q_g"][l], (0, 32)), 8), tile(jnp.pad(w["mla_k_g"][l], (0, 32)), 8),
           wq.astype(F32), wkk.astype(F32), wkvv.astype(F32)]
    return dict(g1=w["norm1_g"][l][None, :], w_in=w_in, prm=prm, sinks=w["swa_sinks"][l].astype(F32),
                wb=b(big["w_branch"]), wo=b(big["w_o"]))


def _ffn_params(w, big, l):
    cw = jnp.pad(w["ffn_conv_w"][l].astype(F32), ((0, 5), (0, 0)))
    return dict(g2=w["norm2_g"][l][None, :], w_up=big["ffn_w_up"].astype(BF16), cw=cw,
                cb=w["ffn_conv_b"][l][None, :].astype(F32), w_down=big["ffn_w_down"].astype(BF16))


def _decay_rows(c):
    return c[:, FF_LANE:FF_LANE + HEADS].T[:, None, :]


def _from_rows(row):
    return jnp.pad(row[:, 0, :].T, ((0, 0), (FF_LANE, 128 - FF_LANE - HEADS)))


def _layer_fwd_mix(h, lw, consts, cos, sin, slopes, l):
    tag = f"l{l}_"
    xn, proj = _norm_matmul(h, lw["g1"], lw["w_in"], IN_W, tag + "in_proj")
    fq, fk, fv, mq, mk, mv, sq, skd, svd, ls = _prep_fwd(proj, lw["prm"], consts, cos, sin, tag + "prep")
    c = _cumsum([ls], False, tag + "decay_cumsum")
    c_row = _decay_rows(c)
    oa, lse_a = _att_fwd(_Att("fox"), fq, fk, fv, (c_row,), tag + "fox_fwd")
    ob, lse_b = _att_fwd(_Att("mla"), mq, mk, mv, (), tag + "mla_fwd")
    oc, lse_c = _att_fwd(_Att("swa"), sq, skd, svd, (lw["sinks"], slopes), tag + "swa_fwd")
    h2 = _post_fwd(h, proj, (oa, ob, oc), lw["wb"], lw["wo"], tag + "merge")
    saved = dict(h=h, xn=xn, proj=proj, q=(fq, mq, sq), k=(fk, mk, skd), v=(fv, mv, svd), c=c_row,
                 o=(oa, ob, oc), lse=(lse_a, lse_b, lse_c), h2=h2)
    return h2, saved


def _layer_fwd_ffn(h2, lw, l):
    tag = f"l{l}_"
    xn2, u = _norm_matmul(h2, lw["g2"], lw["w_up"], 2 * D_FF, tag + "ffn_up")
    act = _ffn_act_fwd(u, lw["cw"], lw["cb"], tag + "ffn_act")
    h3 = _matmul_residual(act, lw["w_down"], h2, tag + "ffn_down")
    return h3, dict(xn2=xn2, u=u, act=act)


def _layer_bwd_ffn(dh3, lw, sv, l):
    tag = f"l{l}_"
    f = D_FF
    dact = _matmul_nt(dh3, lw["w_down"], f, tag + "ffn_down_dx")
    dw_down = _matmul_tn(sv["act"], dh3, D_MODEL, tag + "ffn_down_dw")
    dc, dcw, dcb = _ffn_act_bwd_conv(sv["u"], dact, lw["cw"], lw["cb"], tag + "ffn_act_dc")
    du = _ffn_act_bwd_in(dc, lw["cw"], tag + "ffn_act_du")
    dw_up = _matmul_tn(sv["xn2"], du, f, tag + "ffn_up_dw")
    dh2, dg2 = _norm_matmul_bwd([du], lw["w_up"], sv["h2"], lw["g2"], dh3, tag + "ffn_up_dx")
    g = dict(norm2_g=dg2[0], ffn_w_up=dw_up, ffn_conv_w=dcw[0:3], ffn_conv_b=dcb[0], ffn_w_down=dw_down)
    return dh2, g


def _layer_bwd_mix(dh2, lw, sv, consts, folds, cos, sin, slopes, l, hook=None, merge_hook=None):
    tag = f"l{l}_"
    dgates, doa, dob, doc, dwb, dwo = _post_bwd(dh2, sv["proj"], sv["o"], lw["wb"], lw["wo"], tag + "merge_bwd")
    c_row = sv["c"]
    tick = merge_hook({"w_branch": dwb, "w_o": dwo}) if merge_hook else None
    if tick is not None:
        c_row = c_row + tick
    extras = ((c_row,), (), (lw["sinks"], slopes))
    grads = []
    for n, (mode, do) in enumerate((("fox", doa), ("mla", dob), ("swa", doc))):
        res = _att_bwd(_Att(mode), sv["q"][n], sv["k"][n], sv["v"][n], sv["o"][n], do, sv["lse"][n], extras[n],
                       tag + mode + "_bwd")
        grads.append((res[0], res[1], res[2], res[3:]))
    (dfq, dfk, dfv, (dcq, dck)), (dmq, dmk, dmv, _), (dsq, dskd, dsvd, (dsink,)) = grads
    dls = _cumsum([_from_rows(dcq), _from_rows(dck)], True, tag + "decay_cumsum_bwd")
    res = _prep_bwd(sv["proj"], lw["prm"], consts, cos, sin,
                    (dfq, dfk, dfv, dmq, dmk, dmv, dsq, dskd, dsvd, dls), folds, tag + "prep_bwd")
    dother, pg = res[0], res[1:]
    dh, dg1 = _norm_matmul_bwd([dgates, dother], lw["w_in"], sv["h"], lw["g1"], dh2, tag + "in_proj_dx")
    dw_g = _matmul_tn(sv["xn"], dgates, GATES_W, tag + "in_proj_dw_gates")
    dw_o = _matmul_tn(sv["xn"], dother, OTHER_W, tag + "in_proj_dw_other")
    d_in = _ColumnSegments([
        (dw_o, O_FQ, 1536), (dw_o, O_MISC + FF_LANE, 8), (dw_o, O_CQ, 256), (dw_o, O_CKV, 128), (dw_o, O_MISC, 32),
        (dw_o, O_SQ, 512), (dw_o, O_SK, 128), (dw_o, O_SV, 128), (dw_g, 0, GATES_W)])
    d_wq = pg[9].reshape(256, HEADS, 128)[:, :, :96].reshape(256, 768)
    d_wkv = jnp.concatenate([pg[10].reshape(128, HEADS, 128)[:, :, :64], pg[11].reshape(128, HEADS, 64)],
                            axis=2).reshape(128, 1024)
    g = dict(
        w_in=d_in, fox_forget_b=pg[4][0, FF_LANE:FF_LANE + 8], fox_q_g=pg[0][0, :64],
        fox_k_g=pg[1][0, :64], mla_q_a_g=pg[5][0], mla_w_q_up=d_wq, mla_kv_a_g=pg[6][0], mla_w_kv_up=d_wkv,
        mla_q_g=pg[7][0, :96], mla_k_g=pg[8][0, :96], swa_q_g=pg[2][0, :64], swa_k_g=pg[3][0, :64],
        swa_sinks=dsink[:, 0, 0:2 * PAIRS].reshape(HEADS), w_branch=dwb, w_o=dwo)
    tick = hook(g) if hook else None
    g["norm1_g"] = dg1[0]
    return dh, g, tick


_MIX_BIG = ("w_in", "mla_w_q_up", "mla_w_kv_up", "w_branch", "w_o")
_FFN_BIG = ("ffn_w_up", "ffn_w_down")


def _local_step(x, target, w, hook=None, fetch=None):
    if fetch is None:
        fetch = lambda l, stage, after: {n: w[n][l] for n in (_MIX_BIG if stage == "mix" else _FFN_BIG)}
    seq = x.shape[0]
    length = N_META + seq
    lp = -(-length // ROW_ALIGN) * ROW_ALIGN
    pad = lp - length
    h = jnp.concatenate([w["meta_tokens"].astype(F32), x, jnp.zeros((pad, D_MODEL), F32)], axis=0)
    tgt = jnp.pad(target, ((N_META, pad), (0, 0)))
    consts = _consts()
    folds = (_fold_matrix(512, 64), _fold_matrix(1024, 128))
    cos, sin = _rope_tables(lp)
    slopes = jnp.asarray(2.0 ** (-8.0 * np.arange(1, HEADS + 1, dtype=np.float32) / HEADS), F32)
    lws, saved = [], []
    for l in range(DEPTH):
        lw = _mix_params(w, fetch(l, "mix", h), l)
        h, sv = _layer_fwd_mix(h, lw, consts, cos, sin, slopes, l)
        lw.update(_ffn_params(w, fetch(l, "ffn", h), l))
        h, sv_ffn = _layer_fwd_ffn(h, lw, l)
        lws.append(lw)
        saved.append({**sv, **sv_ffn})
    dh, loss = _loss_head(h, tgt, seq, "loss_head")
    grads = [None] * DEPTH
    for l in reversed(range(DEPTH)):
        dh, g_ffn = _layer_bwd_ffn(dh, lws[l], saved[l], l)
        tick = hook(l, "ffn", g_ffn) if hook else None
        if tick is not None:
            lws[l]["sinks"] = lws[l]["sinks"] + tick
        mix_hook = (lambda g, l=l, g_ffn=g_ffn: hook(l, "mix", {**g_ffn, **g})) if hook else None
        merge_hook = (lambda g, l=l: hook(l, "merge", g)) if hook else None
        dh, g_mix, tick = _layer_bwd_mix(dh, lws[l], saved[l], consts, folds, cos, sin, slopes, l, mix_hook, merge_hook)
        grads[l] = {**g_ffn, **g_mix}
        if tick is not None and l > 0:
            lws[l - 1]["cw"] = lws[l - 1]["cw"] + tick
    return loss, dh[N_META:length], dh[:N_META], grads


def _place():
    return lax.axis_index("x"), lax.axis_index("y"), lax.axis_index("c")


def _flip(pos, k):
    x, y, c = pos
    return (1 - x if k & 4 else x, 1 - y if k & 2 else y, 1 - c if k & 1 else c)


def _index(pos):
    return 4 * pos[0] + 2 * pos[1] + pos[2]


def _gather(tensors, name):
    n_t = len(tensors)

    def body(*refs):
        ins, outs = refs[:n_t], refs[n_t:2 * n_t]
        send_sems, recv_sems, local_sems = refs[2 * n_t:]
        x, y, c = _place()
        me, sibling = (x, y, c), (x, y, 1 - c)
        chips = [(1 - x, y), (x, 1 - y), (1 - x, 1 - y)]

        def copy(t, k, block, to, src=None):
            dst = outs[t].at[_index(block)]
            return pltpu.make_async_remote_copy(
                src_ref=dst if src is None else src, dst_ref=dst, send_sem=send_sems.at[t, k],
                recv_sem=recv_sems.at[t, k], device_id=to, device_id_type=pl.DeviceIdType.MESH)

        local, sent = [], []
        for t in range(n_t):
            local.append(pltpu.make_async_copy(ins[t], outs[t].at[_index(me)], local_sems.at[t]))
            local[-1].start()
            sent.append(copy(t, 0, me, sibling, src=ins[t]))
            sent += [copy(t, 1 + j, me, (*chip, c), src=ins[t]) for j, chip in enumerate(chips)]
        for cp in sent:
            cp.start()
        for j, chip in enumerate(chips):
            for t in range(n_t):
                copy(t, 1 + j, (*chip, c), me).wait_recv()
                sent.append(copy(t, 4 + j, (*chip, c), sibling))
                sent[-1].start()
        for t in range(n_t):
            copy(t, 0, sibling, me).wait_recv()
            for j, chip in enumerate(chips):
                copy(t, 4 + j, (*chip, 1 - c), me).wait_recv()
        for cp in sent:
            cp.wait_send()
        for cp in local:
            cp.wait()

    any_spec = pl.BlockSpec(memory_space=pl.ANY)
    return pl.pallas_call(
        body, name=name, in_specs=[any_spec] * n_t, out_specs=[any_spec] * n_t,
        out_shape=[SDS((N_DEV,) + a.shape, a.dtype) for a in tensors],
        scratch_shapes=[pltpu.SemaphoreType.DMA((n_t, N_DEV - 1)), pltpu.SemaphoreType.DMA((n_t, N_DEV - 1)),
                        pltpu.SemaphoreType.DMA((n_t,))],
    )(*tensors)


def _exchange_start(tensors, name, gather=False, after=None):
    n_t = len(tensors)

    def body(*refs):
        ins, lands = refs[:n_t], refs[n_t:2 * n_t]
        send_sem, recv_sem = refs[2 * n_t + 1:2 * n_t + 3]
        token = refs[-1]
        me = _place()
        mine = _index(me)
        for t in range(n_t):
            for k in range(1, N_DEV):
                peer = _flip(me, k)
                pltpu.make_async_remote_copy(
                    src_ref=ins[t] if gather else ins[t].at[_index(peer)], dst_ref=lands[t].at[mine],
                    send_sem=send_sem, recv_sem=recv_sem, device_id=peer, device_id_type=pl.DeviceIdType.MESH).start()
        token[...] = jnp.zeros_like(token)

    hbm = pl.BlockSpec(memory_space=pltpu.HBM)
    sem = pl.BlockSpec(memory_space=pltpu.SEMAPHORE)
    one = pltpu.SemaphoreType.DMA(())
    land_shape = lambda a: ((N_DEV,) + a.shape) if gather else a.shape
    bufs = ([pltpu.HBM(a.shape, a.dtype) for a in tensors] + [pltpu.HBM(land_shape(a), a.dtype) for a in tensors])
    after = jnp.zeros((8, 128), F32) if after is None else after
    outs = pl.pallas_call(
        body, name=name, in_specs=[hbm] * (2 * n_t) + [pl.BlockSpec(memory_space=pl.ANY)],
        out_specs=[sem, sem] + [hbm] * (2 * n_t) + [pl.BlockSpec(memory_space=pltpu.VMEM)],
        out_shape=[one, one] + bufs + [SDS((8, 128), F32)],
        input_output_aliases={i: 2 + i for i in range(2 * n_t)},
        compiler_params=pltpu.CompilerParams(has_side_effects=pltpu.SideEffectType.DATAFLOW_SIDE_EFFECTING),
    )(*[pltpu.with_memory_space_constraint(a, pltpu.HBM) for a in tensors],
      *[pltpu.with_memory_space_constraint(lax.empty(land_shape(a), a.dtype), pltpu.HBM) for a in tensors], after)
    return outs[:-1], outs[-1][0, 0]


def _exchange_wait(state, after, name, gather=False):
    n_t = (len(state) - 2) // 2

    def body(*refs):
        send_sem, recv_sem = refs[0:2]
        ins, lands = refs[2:2 + n_t], refs[2 + n_t:2 + 2 * n_t]
        me = _place()
        for t in range(n_t):
            for k in range(1, N_DEV):
                peer = _flip(me, k)
                copy = pltpu.make_async_remote_copy(
                    src_ref=ins[t] if gather else ins[t].at[_index(peer)], dst_ref=lands[t].at[_index(peer)],
                    send_sem=send_sem, recv_sem=recv_sem, device_id=peer, device_id_type=pl.DeviceIdType.MESH)
                copy.wait_send()
                copy.wait_recv()

    hbm = pl.BlockSpec(memory_space=pltpu.HBM)
    sem = pl.BlockSpec(memory_space=pltpu.SEMAPHORE)
    bufs = [pltpu.HBM(a.shape, a.dtype) for a in state[2:]]
    outs = pl.pallas_call(
        body, name=name, in_specs=[sem, sem] + [hbm] * (2 * n_t) + [pl.BlockSpec(memory_space=pl.ANY)],
        out_specs=[hbm] * (2 * n_t), out_shape=bufs,
        input_output_aliases={2 + i: i for i in range(2 * n_t)},
        compiler_params=pltpu.CompilerParams(has_side_effects=pltpu.SideEffectType.DATAFLOW_SIDE_EFFECTING),
    )(*state, after)
    return outs[n_t:]


def _sum_slots(parts, name):
    n, rows, w = parts.shape
    tb = 8

    def body(p_ref, o_ref):
        acc = p_ref[0].astype(F32)
        for s in range(1, n):
            acc = acc + p_ref[s].astype(F32)
        o_ref[...] = acc

    return pl.pallas_call(
        body, name=name, grid=(rows // tb,),
        in_specs=[pl.BlockSpec((n, tb, w), lambda i: (0, i, 0))], out_specs=pl.BlockSpec((tb, w), lambda i: (i, 0)),
        out_shape=SDS((rows, w), F32), compiler_params=_params(("parallel",)),
    )(parts)


def _adamw(wt, m, v, parts, name, own=None, after=None):
    shape = wt.shape
    parts = parts if isinstance(parts, (list, tuple)) else [parts]
    n, w = parts[0].shape[0], shape[-1]
    rows = math.prod(shape[:-1])
    per = rows // len(parts)
    step = 16 if parts[0].dtype == BF16 else 8
    tb = max([t for t in range(step, 257, step) if per % t == 0] or [per])
    nb = per // tb
    c1 = 1.0 / (1.0 - ADAM_B1 ** ADAM_STEP)
    c2 = 1.0 / (1.0 - ADAM_B2 ** ADAM_STEP)
    state = [a.reshape(rows, w) for a in (wt, m, v)]
    n_in = 4 if own is None else 5
    outs = None
    for l in reversed(range(len(parts))):
        def body(*refs):
            idx_ref = None if own is None else refs[0]
            w_ref, m_ref, v_ref, p_ref = refs[n_in - 4:n_in] if own is None else refs[1:5]
            g_out, d_out, m_out, v_out = refs[-4:]
            g = None
            for s in range(n):
                term = p_ref[s] if own is None else jnp.where(idx_ref[0] == s, refs[5][0], p_ref[s])
                g = term.astype(F32) if g is None else g + term.astype(F32)
            m_new = ADAM_B1 * m_ref[...] + (1.0 - ADAM_B1) * g
            v_new = ADAM_B2 * v_ref[...] + (1.0 - ADAM_B2) * (g * g)
            g_out[...] = g
            m_out[...] = m_new
            v_out[...] = v_new
            d_out[...] = -ADAM_LR * ((m_new * c1) / (jnp.sqrt(v_new * c2) + ADAM_EPS) + ADAM_WD * w_ref[...])

        row = pl.BlockSpec((tb, w), lambda i, *_, l=l: (l * nb + i, 0))
        in_specs = [row, row, row, pl.BlockSpec((n, tb, w), lambda i, *_: (0, i, 0))]
        args = [*state, parts[l].reshape(n, per, w)]
        if own is not None:
            in_specs.append(pl.BlockSpec((1, tb, w), lambda i, idx: (idx[0], i, 0)))
            args.append(own[l].reshape(n, per, w))
        prev = [] if outs is None else list(outs)
        behind = [] if after is None else [after]
        in_specs += [pl.BlockSpec(memory_space=pl.ANY)] * (len(prev) + len(behind))
        n_pre = 0 if own is None else 1
        call = dict(name=f"{name}_{l}", out_shape=[SDS((rows, w), F32)] * 4,
                    input_output_aliases={n_pre + len(args) + k: k for k in range(len(prev))},
                    compiler_params=_params(("parallel",)))
        if own is None:
            outs = pl.pallas_call(body, grid=(nb,), in_specs=in_specs, out_specs=[row] * 4, **call)(*args, *prev, *behind)
        else:
            spec = pltpu.PrefetchScalarGridSpec(num_scalar_prefetch=1, grid=(nb,), in_specs=in_specs, out_specs=[row] * 4)
            idx = jnp.reshape(_index(_place()), (1,)).astype(jnp.int32)
            outs = pl.pallas_call(body, grid_spec=spec, **call)(idx, *args, *prev, *behind)
    return [o.reshape(shape) for o in outs]


_BIG = [("w_in", 2), ("mla_w_q_up", 2), ("mla_w_kv_up", 2), ("w_branch", 3), ("w_o", 1), ("ffn_w_up", 2), ("ffn_w_down", 1)]
_SMALL_SHARDED = [("meta_tokens", 1), ("ffn_conv_w", 2)]
_REPLICATED = ["norm1_g", "fox_forget_b", "fox_q_g", "fox_k_g", "mla_q_a_g", "mla_kv_a_g", "mla_q_g", "mla_k_g",
               "swa_q_g", "swa_k_g", "swa_sinks", "norm2_g", "ffn_conv_b"]
_ORDER = ["meta_tokens", "norm1_g", "w_in", "fox_forget_b", "fox_q_g", "fox_k_g", "mla_q_a_g", "mla_w_q_up",
          "mla_kv_a_g", "mla_w_kv_up", "mla_q_g", "mla_k_g", "swa_q_g", "swa_k_g", "swa_sinks", "w_branch", "w_o",
          "norm2_g", "ffn_w_up", "ffn_conv_w", "ffn_conv_b", "ffn_w_down"]


def _flat_rows(vecs, dtype, row_mult):
    flat = jnp.concatenate([a.reshape(-1).astype(dtype) for a in vecs])
    rows = -(-flat.shape[0] // (1024 * row_mult)) * row_mult
    return jnp.pad(flat, (0, rows * 1024 - flat.shape[0])).reshape(rows, 1024)


def _unflatten(flat, shapes):
    out, off = [], 0
    for s in shapes:
        n = math.prod(s)
        out.append(flat[off:off + n].reshape(s))
        off += n
    return out


def _to_full(blocks, axis):
    moved = jnp.moveaxis(blocks, 0, axis)
    s = moved.shape
    return moved.reshape(s[:axis] + (s[axis] * s[axis + 1],) + s[axis + 2:])


def _to_blocks(full, axis):
    s = full.shape
    split = full.reshape(s[:axis] + (N_DEV, s[axis] // N_DEV) + s[axis + 1:])
    return jnp.moveaxis(split, axis, 0)


def kernel(x, meta_tokens, norm1_g, w_in, fox_forget_b, fox_q_g, fox_k_g, mla_q_a_g, mla_w_q_up, mla_kv_a_g, mla_w_kv_up, mla_q_g, mla_k_g, swa_q_g, swa_k_g, swa_sinks, w_branch, w_o, norm2_g, ffn_w_up, ffn_conv_w, ffn_conv_b, ffn_w_down, loss_target, m_meta_tokens, m_norm1_g, m_w_in, m_fox_forget_b, m_fox_q_g, m_fox_k_g, m_mla_q_a_g, m_mla_w_q_up, m_mla_kv_a_g, m_mla_w_kv_up, m_mla_q_g, m_mla_k_g, m_swa_q_g, m_swa_k_g, m_swa_sinks, m_w_branch, m_w_o, m_norm2_g, m_ffn_w_up, m_ffn_conv_w, m_ffn_conv_b, m_ffn_w_down, v_meta_tokens, v_norm1_g, v_w_in, v_fox_forget_b, v_fox_q_g, v_fox_k_g, v_mla_q_a_g, v_mla_w_q_up, v_mla_kv_a_g, v_mla_w_kv_up, v_mla_q_g, v_mla_k_g, v_swa_q_g, v_swa_k_g, v_swa_sinks, v_w_branch, v_w_o, v_norm2_g, v_ffn_w_up, v_ffn_conv_w, v_ffn_conv_b, v_ffn_w_down):
    wl = dict(zip(_ORDER, (meta_tokens, norm1_g, w_in, fox_forget_b, fox_q_g, fox_k_g, mla_q_a_g, mla_w_q_up,
                           mla_kv_a_g, mla_w_kv_up, mla_q_g, mla_k_g, swa_q_g, swa_k_g, swa_sinks, w_branch, w_o,
                           norm2_g, ffn_w_up, ffn_conv_w, ffn_conv_b, ffn_w_down)))
    ml = dict(zip(_ORDER, (m_meta_tokens, m_norm1_g, m_w_in, m_fox_forget_b, m_fox_q_g, m_fox_k_g, m_mla_q_a_g,
                           m_mla_w_q_up, m_mla_kv_a_g, m_mla_w_kv_up, m_mla_q_g, m_mla_k_g, m_swa_q_g, m_swa_k_g,
                           m_swa_sinks, m_w_branch, m_w_o, m_norm2_g, m_ffn_w_up, m_ffn_conv_w, m_ffn_conv_b,
                           m_ffn_w_down)))
    vl = dict(zip(_ORDER, (v_meta_tokens, v_norm1_g, v_w_in, v_fox_forget_b, v_fox_q_g, v_fox_k_g, v_mla_q_a_g,
                           v_mla_w_q_up, v_mla_kv_a_g, v_mla_w_kv_up, v_mla_q_g, v_mla_k_g, v_swa_q_g, v_swa_k_g,
                           v_swa_sinks, v_w_branch, v_w_o, v_norm2_g, v_ffn_w_up, v_ffn_conv_w, v_ffn_conv_b,
                           v_ffn_w_down)))
    small_sh = [n for n, _ in _SMALL_SHARDED]
    big = [n for n, _ in _BIG]
    axis_of = dict(_BIG)
    idx = _index(_place())

    def to_full(n, blocks, own=None):
        if own is not None:
            sel = (jnp.arange(N_DEV) == idx).reshape((N_DEV,) + (1,) * own.ndim)
            blocks = jnp.where(sel, own[None], blocks)
        return blocks if n == "w_in" else _to_full(blocks, axis_of[n] - 1)

    local = {(n, l): wl[n][l].astype(BF16) for n in big for l in range(DEPTH)}
    got = _gather([local[(n, 0)] for n in _MIX_BIG] + [wl[n] for n in small_sh], "gather_weights_l0_mix")
    full = {n: wl[n] for n in _REPLICATED}
    for (n, axis), blocks in zip(_SMALL_SHARDED, got[len(_MIX_BIG):]):
        full[n] = _to_full(blocks, axis)
    ready = {(n, 0): to_full(n, blocks) for n, blocks in zip(_MIX_BIG, got)}
    later = {"l0_ffn": [(n, 0) for n in _FFN_BIG], "l1": [(n, 1) for n in big]}
    states = {}
    for key, names in later.items():
        states[key], tick = _exchange_start([local[e] for e in names], "gather_weights_" + key + "_start", True, got[0])
        full["norm1_g"] = full["norm1_g"] + tick

    def fetch(l, stage, after):
        key = "l0_ffn" if l == 0 else "l1"
        if (l, stage) != (0, "mix") and key in states:
            lands = _exchange_wait(states.pop(key), after, "gather_weights_" + key + "_wait", True)
            ready.update({e: to_full(e[0], blocks, local[e]) for e, blocks in zip(later[key], lands)})
        return {n: ready[(n, l)] for n in (_MIX_BIG if stage == "mix" else _FFN_BIG)}

    blocks_of = lambda g, names: [(g[n].blocks(N_DEV) if isinstance(g[n], _ColumnSegments)
                                   else _to_blocks(g[n], axis_of[n] - 1)).astype(BF16) for n in names]
    early = {}

    def hook(l, stage, g):
        if l == DEPTH - 1 and stage == "mix":
            key, names = "l1", big
        elif l == 0:
            merge = ("w_branch", "w_o")
            groups = {"ffn": _FFN_BIG, "merge": merge, "mix": tuple(n for n in _MIX_BIG if n not in merge)}
            key, names = "l0_" + stage, groups[stage]
        else:
            return None
        sends = blocks_of(g, names)
        if key == "l0_mix":
            early[key] = (names, l, sends)
            return None
        state, tick = _exchange_start(sends, "exchange_grads_" + key + "_start")
        early[key] = (names, l, sends, state)
        return tick

    loss, grad_x, grad_meta, grads = _local_step(x[0], loss_target[0], full, hook, fetch)
    result = {kind: {} for kind in ("grad", "delta", "new_m", "new_v")}
    small_grads = {k: jnp.stack([grads[l][k] for l in range(DEPTH)]) for k in grads[0] if k not in big}
    small_grads["meta_tokens"] = grad_meta
    small_full = _REPLICATED + small_sh
    mine_small = _flat_rows([small_grads[n] for n in small_full] + [loss], F32, 8)
    small_state, tick = _exchange_start([mine_small], "gather_small_grads_start", True)
    names, l, sends = early["l0_mix"]
    state, tick = _exchange_start(sends, "exchange_grads_l0_mix_start", after=jnp.reshape(tick, (1, 1)))
    early["l0_mix"] = (names, l, sends, state)
    started = jnp.reshape(tick, (1, 1))
    landed, sent = {}, {}
    after = sends[0]
    for key in ("l1", "l0_ffn", "l0_merge"):
        names, l, sends, state = early[key]
        got = _exchange_wait(state, after, "exchange_grads_" + key + "_wait")
        landed.update({(n, l): p for n, p in zip(names, got)})
        sent.update({(n, l): p for n, p in zip(names, sends)})

    def update(names):
        for n in names:
            outs = _adamw(wl[n], ml[n], vl[n], [landed[(n, l)] for l in range(DEPTH)], "adamw_" + n,
                          [sent[(n, l)] for l in range(DEPTH)], started)
            for kind, val in zip(result, outs):
                result[kind][n] = val

    update(_FFN_BIG)
    done = result["delta"]["ffn_w_up"][0, 0, :8] + result["delta"]["ffn_w_down"][0, 0, :8]
    got_small = _exchange_wait(small_state, done, "gather_small_grads_wait", True)[0]
    sel = (jnp.arange(N_DEV) == idx).reshape(N_DEV, 1, 1)
    total_small = _sum_slots(jnp.where(sel, mine_small[None], got_small), "sum_small_grads").reshape(-1)
    pieces = _unflatten(total_small, [small_grads[n].shape for n in small_full] + [()])
    loss_total = pieces[-1]
    g_small = dict(zip(small_full, pieces[:-1]))
    for n, axis in _SMALL_SHARDED:
        size = wl[n].shape[axis]
        g_small[n] = lax.dynamic_slice_in_dim(g_small[n], idx * size, size, axis)
    flat = lambda d: _flat_rows([d[n] for n in small_full], F32, 8)
    small_out = _adamw(flat(wl), flat(ml), flat(vl), flat(g_small)[None], "adamw_small")
    for kind, fs in zip(result, small_out):
        result[kind].update(zip(small_full, _unflatten(fs.reshape(-1), [wl[n].shape for n in small_full])))
    names, l, sends, state = early["l0_mix"]
    got = _exchange_wait(state, small_out[0], "exchange_grads_l0_mix_wait")
    landed.update({(n, l): p for n, p in zip(names, got)})
    sent.update({(n, l): p for n, p in zip(names, sends)})
    update(_MIX_BIG)
    outs = [loss_total, grad_x[None]]
    for kind in ("grad", "delta", "new_m", "new_v"):
        outs += [result[kind][n] for n in _ORDER]
    return tuple(outs)
```

```python
import functools
import math

import numpy as np
import jax
import jax.numpy as jnp
from jax import lax
from jax.experimental import pallas as pl
from jax.experimental.pallas import tpu as pltpu

F32, BF16 = jnp.float32, jnp.bfloat16
SDS = jax.ShapeDtypeStruct

D_MODEL = 1024
N_META = 16
EPS = 1e-6
WINDOW = 128
ROPE_THETA = 10000.0
HEADS = 8
D_FF = 2816
DEPTH = 2
N_DEV = 8
ADAM_LR, ADAM_B1, ADAM_B2, ADAM_EPS, ADAM_WD, ADAM_STEP = 0.001, 0.9, 0.999, 1e-08, 0.01, 10

ROW_ALIGN = 384
TILE_MM = 384
TILE_ROW = 192
TILE_CONV_BWD = 128
TILE_ATT = 384
TILE_POST = 384
PAIRS = 2
VMEM_LIMIT = 56 * 1024 * 1024

GATES_W = 3072
OTHER_W = 2816
IN_W = GATES_W + OTHER_W
O_FQ, O_FK, O_FV, O_SQ, O_SK, O_SV, O_CQ, O_CKV, O_MISC = 0, 512, 1024, 1536, 2048, 2176, 2304, 2560, 2688
FF_LANE = 32

NEG = -1e30


def _dot(a, b):
    return jnp.dot(a, b, preferred_element_type=F32)


def _dot_nt(a, b):
    return lax.dot_general(a, b, (((1,), (1,)), ((), ())), preferred_element_type=F32)


def _dot_tn(a, b):
    return lax.dot_general(a, b, (((0,), (0,)), ((), ())), preferred_element_type=F32)


def _params(sem):
    return pltpu.CompilerParams(dimension_semantics=sem, vmem_limit_bytes=VMEM_LIMIT)


def _rms(x, g):
    return x * lax.rsqrt(jnp.mean(x * x, axis=-1, keepdims=True) + EPS) * g


def _split_dot(x, m, pieces=2):
    acc, rest = None, x
    for _ in range(pieces):
        part = rest.astype(BF16)
        rest = rest - part.astype(F32)
        acc = _dot(part, m) if acc is None else acc + _dot(part, m)
    return acc


@jax.custom_vjp
def _sel(x, m, mt):
    return _split_dot(x, m)


_sel.defvjp(lambda x, m, mt: (_split_dot(x, m), (m, mt)), lambda res, dy: (_split_dot(dy, res[1]), None, None))


@jax.custom_vjp
def _mm(x, w):
    return _dot(x.astype(BF16), w.astype(BF16))


def _mm_bwd(res, dy):
    x, w = res
    dyb = dy.astype(BF16)
    return _dot_nt(dyb, w.astype(BF16)), _dot_tn(x.astype(BF16), dyb)


_mm.defvjp(lambda x, w: (_mm(x, w), (x, w)), _mm_bwd)


def _rot_impl(x):
    w = x.shape[1]
    lane = lax.broadcasted_iota(jnp.int32, x.shape, 1) % 128
    lo = (lane >= 64) & (lane < 80)
    hi = (lane >= 80) & (lane < 96)
    return jnp.where(hi, pltpu.roll(x, 16, 1), 0.0) - jnp.where(lo, pltpu.roll(x, w - 16, 1), 0.0)


@jax.custom_vjp
def _rot(x):
    return _rot_impl(x)


_rot.defvjp(lambda x: (_rot_impl(x), None), lambda _, dy: (-_rot_impl(dy),))


def _gnorm(x, g, e, et, dim):
    inv = lax.rsqrt(_sel(x * x, e, et) * (1.0 / dim) + EPS)
    return x * _sel(inv, et, e) * g


def _indicator(width, period):
    m = np.zeros((width, 128), np.float32)
    m[np.arange(width), np.arange(width) // period] = 1.0
    return m


def _consts():
    e64 = _indicator(512, 64)
    e128 = _indicator(1024, 128)
    sk = np.zeros((128, 1024), np.float32)
    for h in range(HEADS):
        sk[np.arange(32), 128 * h + 64 + np.arange(32)] = 1.0
    dup = np.zeros((128, 256), np.float32)
    for g in range(2):
        for r in range(2):
            dup[64 * g + np.arange(64), 128 * g + 64 * r + np.arange(64)] = 1.0
    mats = [e64, e64.T, e128, e128.T, sk, sk.T, dup, dup.T]
    return [jnp.asarray(m, BF16) for m in mats]


def _fold_matrix(width, period):
    m = np.zeros((width, 128), np.float32)
    m[np.arange(width), np.arange(width) % period] = 1.0
    return jnp.asarray(m, BF16)


def _rope_tables(lp):
    half = 16
    freqs = ROPE_THETA ** (-np.arange(half, dtype=np.float32) / half)
    ang = np.arange(lp, dtype=np.float32)[:, None] * freqs[None, :]
    cos = np.ones((lp, 128), np.float32)
    sin = np.zeros((lp, 128), np.float32)
    cos[:, 64:80] = np.cos(ang)
    cos[:, 80:96] = np.cos(ang)
    sin[:, 64:80] = np.sin(ang)
    sin[:, 80:96] = np.sin(ang)
    return jnp.asarray(cos), jnp.asarray(sin)


def _norm_matmul(h, g, w, tn, name):
    lp, d = h.shape
    n = w.shape[1]
    tb = TILE_MM

    def body(h_ref, g_ref, w_ref, xn_ref, y_ref):
        @pl.when(pl.program_id(1) == 0)
        def _():
            xn_ref[...] = _rms(h_ref[...], g_ref[...]).astype(BF16)

        y_ref[...] = _dot(xn_ref[...], w_ref[...])

    return pl.pallas_call(
        body, name=name, grid=(lp // tb, n // tn),
        in_specs=[pl.BlockSpec((tb, d), lambda i, j: (i, 0)), pl.BlockSpec((1, d), lambda i, j: (0, 0)),
                  pl.BlockSpec((d, tn), lambda i, j: (0, j))],
        out_specs=[pl.BlockSpec((tb, d), lambda i, j: (i, 0)), pl.BlockSpec((tb, tn), lambda i, j: (i, j))],
        out_shape=[SDS((lp, d), BF16), SDS((lp, n), F32)],
        compiler_params=_params(("parallel", "arbitrary")),
    )(h, g, w)


def _matmul_residual(a, w, res, name):
    m, k = a.shape
    n = w.shape[1]
    tb = TILE_MM

    def body(a_ref, w_ref, r_ref, o_ref):
        o_ref[...] = r_ref[...] + _dot(a_ref[...], w_ref[...])

    return pl.pallas_call(
        body, name=name, grid=(m // tb,),
        in_specs=[pl.BlockSpec((tb, k), lambda i: (i, 0)), pl.BlockSpec((k, n), lambda i: (0, 0)),
                  pl.BlockSpec((tb, n), lambda i: (i, 0))],
        out_specs=pl.BlockSpec((tb, n), lambda i: (i, 0)),
        out_shape=SDS((m, n), F32),
        compiler_params=_params(("parallel",)),
    )(a, w, res)


def _matmul_nt(dy, w, tn, name):
    m, k = dy.shape
    n = w.shape[0]
    tb = TILE_MM

    def body(dy_ref, w_ref, o_ref):
        o_ref[...] = _dot_nt(dy_ref[...].astype(BF16), w_ref[...])

    return pl.pallas_call(
        body, name=name, grid=(m // tb, n // tn),
        in_specs=[pl.BlockSpec((tb, k), lambda i, j: (i, 0)), pl.BlockSpec((tn, k), lambda i, j: (j, 0))],
        out_specs=pl.BlockSpec((tb, tn), lambda i, j: (i, j)),
        out_shape=SDS((m, n), F32),
        compiler_params=_params(("parallel", "arbitrary")),
    )(dy, w)


def _matmul_tn(x, dy, tn, name):
    m, k = x.shape
    n = dy.shape[1]
    tb = TILE_MM
    nb = m // tb

    def body(x_ref, dy_ref, o_ref, acc):
        i = pl.program_id(1)

        @pl.when(i == 0)
        def _():
            acc[...] = jnp.zeros_like(acc)

        acc[...] += _dot_tn(x_ref[...].astype(BF16), dy_ref[...].astype(BF16))

        @pl.when(i == nb - 1)
        def _():
            o_ref[...] = acc[...].astype(BF16)

    return pl.pallas_call(
        body, name=name, grid=(n // tn, nb),
        in_specs=[pl.BlockSpec((tb, k), lambda j, i: (i, 0)), pl.BlockSpec((tb, tn), lambda j, i: (i, j))],
        out_specs=pl.BlockSpec((k, tn), lambda j, i: (0, j)),
        out_shape=SDS((k, n), BF16),
        scratch_shapes=[pltpu.VMEM((k, tn), F32)],
        compiler_params=_params(("parallel", "arbitrary")),
    )(x, dy)


def _norm_matmul_bwd(dys, w, x, g, dres, name):
    m, d = x.shape
    tb = TILE_MM
    widths = [a.shape[1] for a in dys]
    n_dy = len(dys)

    def body(*refs):
        w_ref, x_ref, g_ref, r_ref, o_ref, dg_ref = refs[n_dy:]

        @pl.when(pl.program_id(0) == 0)
        def _():
            dg_ref[...] = jnp.zeros_like(dg_ref)

        dxn, off = None, 0
        for dy_ref, width in zip(refs[:n_dy], widths):
            part = _dot_nt(dy_ref[...], w_ref[:, off:off + width])
            dxn = part if dxn is None else dxn + part
            off += width
        _, vjp = jax.vjp(_rms, x_ref[...], g_ref[...])
        dx, dg = vjp(dxn)
        o_ref[...] = r_ref[...] + dx
        dg_ref[...] += dg

    row = pl.BlockSpec((tb, d), lambda i: (i, 0))
    vec = pl.BlockSpec((1, d), lambda i: (0, 0))
    return pl.pallas_call(
        body, name=name, grid=(m // tb,),
        in_specs=[pl.BlockSpec((tb, wd), lambda i: (i, 0)) for wd in widths]
        + [pl.BlockSpec(w.shape, lambda i: (0, 0)), row, vec, row],
        out_specs=[row, vec],
        out_shape=[SDS((m, d), F32), SDS((1, d), F32)],
        compiler_params=_params(("arbitrary",)),
    )(*dys, w, x, g, dres)


def _prep_math(pieces, prm, consts, cos, sin):
    fq, fk, sq, sk, sv, cq, ckv, misc = pieces
    gfq, gfk, gsq, gsk, fb, gqa, gkva, gmq, gmk, wq, wkk, wkv = prm
    e64, e64t, e128, e128t, skm, skt, dup, dupt = consts
    cos8 = jnp.concatenate([cos] * HEADS, axis=1)
    sin8 = jnp.concatenate([sin] * HEADS, axis=1)
    fq_n = _gnorm(fq, gfq, e64, e64t, 64)
    fk_n = _gnorm(fk, gfk, e64, e64t, 64)
    ls = jax.nn.log_sigmoid(misc + fb)
    q = _gnorm(_mm(_rms(cq, gqa), wq), gmq, e128, e128t, 96)
    mq = q * cos8 + _rot(q) * sin8
    kva = _rms(ckv, gkva)
    k = _gnorm(_mm(kva, wkk) + _sel(misc, skm, skt), gmk, e128, e128t, 96)
    mk = k * cos8 + _rot(k) * sin8
    mv = _mm(kva, wkv)
    sq_n = _gnorm(sq, gsq, e64, e64t, 64)
    sk_n = _gnorm(sk, gsk, e64[0:128], e64t[:, 0:128], 64)
    skd = _sel(sk_n, dup, dupt)
    svd = _sel(sv, dup, dupt)
    return fq_n, fk_n, ls, mq, mk, mv, sq_n, skd, svd


_PIECES = [(O_FQ, 512), (O_FK, 512), (O_SQ, 512), (O_SK, 128), (O_SV, 128), (O_CQ, 256), (O_CKV, 128), (O_MISC, 128)]
_PRM_SHAPES = [(1, 512), (1, 512), (1, 512), (1, 128), (1, 128), (1, 256), (1, 128), (1, 1024), (1, 1024),
               (256, 1024), (128, 1024), (128, 512)]
_CONST_SHAPES = [(512, 128), (128, 512), (1024, 128), (128, 1024), (128, 1024), (1024, 128), (128, 256), (256, 128)]


def _piece_specs(tb):
    def spec(off, width):
        blk = (GATES_W + off) // width
        return pl.BlockSpec((tb, width), lambda i, blk=blk: (i, blk))
    return [spec(o, w) for o, w in _PIECES] + [spec(O_FV, 512)]


def _full_specs(shapes):
    return [pl.BlockSpec(s, lambda i: (0, 0)) for s in shapes]


def _prep_fwd(proj, prm, consts, cos, sin, name):
    lp = proj.shape[0]
    tb = TILE_ROW
    row = lambda w: pl.BlockSpec((tb, w), lambda i: (i, 0))

    def body(*refs):
        pieces = [r[...] for r in refs[0:8]]
        fv = refs[8][...]
        prm_v = [r[...] for r in refs[9:21]]
        consts_v = [r[...] for r in refs[21:29]]
        cos_v, sin_v = refs[29][...], refs[30][...]
        outs = refs[31:]
        fq_n, fk_n, ls, mq, mk, mv, sq_n, skd, svd = _prep_math(pieces, prm_v, consts_v, cos_v, sin_v)
        for ref, val in zip(outs, (fq_n, fk_n, fv, mq, mk, mv, sq_n, skd, svd)):
            ref[...] = val.astype(BF16)
        outs[9][...] = ls

    widths = [512, 512, 512, 1024, 1024, 512, 512, 256, 256]
    return pl.pallas_call(
        body, name=name, grid=(lp // tb,),
        in_specs=_piece_specs(tb) + _full_specs(_PRM_SHAPES) + _full_specs(_CONST_SHAPES) + [row(128), row(128)],
        out_specs=[row(w) for w in widths] + [row(128)],
        out_shape=[SDS((lp, w), BF16) for w in widths] + [SDS((lp, 128), F32)],
        compiler_params=_params(("parallel",)),
    )(*([proj] * 9), *prm, *consts, cos, sin)


def _prep_bwd(proj, prm, consts, cos, sin, cots, folds, name):
    lp = proj.shape[0]
    tb = TILE_ROW
    row = lambda w: pl.BlockSpec((tb, w), lambda i: (i, 0))
    fold64, fold128 = folds

    def body(*refs):
        pieces = [r[...] for r in refs[0:8]]
        prm_v = [r[...] for r in refs[9:21]]
        consts_v = [r[...] for r in refs[21:29]]
        cos_v, sin_v = refs[29][...], refs[30][...]
        dfq, dfk, dfv, dmq, dmk, dmv, dsq, dskd, dsvd, dls = [r[...] for r in refs[31:41]]
        f64, f128 = refs[41][...], refs[42][...]
        d_ref = refs[43]
        g_refs = refs[44:]

        @pl.when(pl.program_id(0) == 0)
        def _():
            for r in g_refs:
                r[...] = jnp.zeros_like(r)

        f = lambda pc, pr: _prep_math(pc, pr, consts_v, cos_v, sin_v)
        _, vjp = jax.vjp(f, pieces, prm_v)
        dpc, dprm = vjp((dfq, dfk, dls, dmq, dmk, dmv, dsq, dskd, dsvd))
        d_fq, d_fk, d_sq, d_sk, d_sv, d_cq, d_ckv, d_misc = dpc
        for off, val in ((O_FQ, d_fq), (O_FK, d_fk), (O_FV, dfv), (O_SQ, d_sq), (O_SK, d_sk), (O_SV, d_sv),
                         (O_CQ, d_cq), (O_CKV, d_ckv), (O_MISC, d_misc)):
            d_ref[:, off:off + val.shape[1]] = val.astype(BF16)
        folded = {0: f64, 1: f64, 2: f64, 3: f64[0:128], 7: f128, 8: f128}
        for idx, (ref, val) in enumerate(zip(g_refs, dprm)):
            if idx in folded:
                ref[...] += _split_dot(jnp.broadcast_to(val, (8, val.shape[1])), folded[idx], 3)
            elif val.shape[0] == 1:
                ref[...] += jnp.broadcast_to(val, ref.shape)
            else:
                ref[...] += val

    g_shapes = [(8, 128), (8, 128), (8, 128), (8, 128), (8, 128), (8, 256), (8, 128), (8, 128), (8, 128),
                (256, 1024), (128, 1024), (128, 512)]
    cot_widths = [512, 512, 512, 1024, 1024, 512, 512, 256, 256, 128]
    return pl.pallas_call(
        body, name=name, grid=(lp // tb,),
        in_specs=(_piece_specs(tb) + _full_specs(_PRM_SHAPES) + _full_specs(_CONST_SHAPES) + [row(128), row(128)]
                  + [row(w) for w in cot_widths] + _full_specs([(512, 128), (1024, 128)])),
        out_specs=[row(OTHER_W)] + _full_specs(g_shapes),
        out_shape=[SDS((lp, OTHER_W), BF16)] + [SDS(s, F32) for s in g_shapes],
        compiler_params=_params(("arbitrary",)),
    )(*([proj] * 9), *prm, *consts, cos, sin, *cots, fold64, fold128)


def _cumsum(xs, reverse, name):
    lp = xs[0].shape[0]
    tb = TILE_MM
    nb = lp // tb
    n_in = len(xs)
    idx = (lambda i: (nb - 1 - i, 0)) if reverse else (lambda i: (i, 0))

    def body(*refs):
        o_ref, carry = refs[n_in], refs[n_in + 1]

        @pl.when(pl.program_id(0) == 0)
        def _():
            carry[...] = jnp.zeros_like(carry)

        x = refs[0][...]
        for r in refs[1:n_in]:
            x = x + r[...]
        r_i = lax.broadcasted_iota(jnp.int32, (tb, tb), 0)
        c_i = lax.broadcasted_iota(jnp.int32, (tb, tb), 1)
        tri = ((c_i >= r_i) if reverse else (c_i <= r_i)).astype(BF16)
        acc, rest = None, x
        for _ in range(3):
            part = rest.astype(BF16)
            rest = rest - part.astype(F32)
            acc = _dot(tri, part) if acc is None else acc + _dot(tri, part)
        o_ref[...] = acc + carry[...]
        carry[...] += jnp.sum(x, axis=0, keepdims=True)

    return pl.pallas_call(
        body, name=name, grid=(nb,),
        in_specs=[pl.BlockSpec((tb, 128), idx)] * n_in,
        out_specs=pl.BlockSpec((tb, 128), idx),
        out_shape=SDS((lp, 128), F32),
        scratch_shapes=[pltpu.VMEM((1, 128), F32)],
        compiler_params=_params(("arbitrary",)),
    )(*xs)


class _Att:
    def __init__(self, mode):
        self.mode = mode
        self.wide = mode == "mla"
        self.qw = 256 if self.wide else 128
        self.scale = (96 if mode == "mla" else 64) ** -0.5

    def resident(self, x, lo, scaled):
        if self.wide:
            return x[:, 0:128], x[:, 128:256]
        if scaled:
            x = x * jnp.asarray(self.scale, x.dtype)
        zero = jnp.zeros_like(x)
        return jnp.where(lo, x, zero), jnp.where(lo, zero, x)

    def moving(self, x):
        return (x[:, 0:128], x[:, 128:256]) if self.wide else (x, x)

    def logits(self, a, b, qpos, kpos, key_decay, slope, masked):
        s = _dot_nt(a, b)
        if self.wide:
            s = s * self.scale
        if self.mode == "fox":
            s = s - key_decay
        if self.mode == "swa":
            s = s - slope * (qpos - kpos).astype(F32)
        if masked:
            ok = kpos <= qpos
            if self.mode == "swa":
                ok = ok & ((kpos < N_META) | (qpos - kpos < WINDOW))
            s = jnp.where(ok, s, NEG)
        return s


def _as_rows(col):
    return jnp.broadcast_to(col, (col.shape[0], 128)).T[0:8, :]


def _halves(x, lo):
    zero = jnp.zeros_like(x)
    return jnp.where(lo, x, zero), jnp.where(lo, zero, x)


def _kv_specs(att, lp, rows):
    if att.mode == "swa":
        return (pl.BlockSpec((rows, 128), lambda g, i: (i if rows != lp else 0, g)),) * 2
    return (pl.BlockSpec((rows, PAIRS * att.qw), lambda g, i: (i if rows != lp else 0, g)),
            pl.BlockSpec((rows, PAIRS * 128), lambda g, i: (i if rows != lp else 0, g)))


def _pair_cols(att, x, pp, width):
    return x if x.shape[1] == width else x[:, pp * width:(pp + 1) * width]


def _att_fwd(att, q, k, v, extra, name):
    lp = q.shape[0]
    t = TILE_ATT
    nq = lp // t
    qw = att.qw
    mode = att.mode
    nh = 2 * PAIRS

    def body(*refs):
        q_ref, k_ref, v_ref = refs[0:3]
        o_ref, lse_ref = refs[-2:]
        g, qi = pl.program_id(0), pl.program_id(1)
        lo = lax.broadcasted_iota(jnp.int32, (1, 128), 1) < 64
        q_all = q_ref[...]
        q_heads = [h for pp in range(PAIRS) for h in att.resident(_pair_cols(att, q_all, pp, qw), lo, True)]
        qpos = qi * t + lax.broadcasted_iota(jnp.int32, (t, 1), 0)

        def step(first, cols, carry, masked):
            ks = pl.multiple_of(first, 128)
            kc, vc = k_ref[pl.ds(ks, cols), :], v_ref[pl.ds(ks, cols), :]
            kpos = first + lax.broadcasted_iota(jnp.int32, (1, cols), 1)
            out = []
            for h in range(nh):
                pp = h // 2
                m, l, acc = carry[3 * h:3 * h + 3]
                k_h = att.moving(_pair_cols(att, kc, pp, qw))[h % 2]
                decay = refs[3][h, :, pl.ds(ks, cols)] if mode == "fox" else None
                slope = refs[4][nh * g + h] if mode == "swa" else None
                s = att.logits(q_heads[h], k_h, qpos, kpos, decay, slope, masked)
                m_new = jnp.maximum(m, jnp.max(s, axis=-1, keepdims=True))
                alpha = jnp.exp(m - m_new)
                pe = jnp.exp(s - m_new)
                l = alpha * l + jnp.sum(pe, axis=-1, keepdims=True)
                acc = alpha * acc + _dot(pe.astype(BF16), _pair_cols(att, vc, pp, 128))
                out += [m_new, l, acc]
            return tuple(out)

        init = []
        for h in range(nh):
            if mode == "swa":
                init += [jnp.full((t, 1), refs[3][nh * g + h], F32), jnp.ones((t, 1), F32)]
            else:
                init += [jnp.full((t, 1), NEG, F32), jnp.zeros((t, 1), F32)]
            init.append(jnp.zeros((t, 128), F32))
        if mode == "swa":
            band = jnp.maximum(qi * t - WINDOW, 0)
            carry = lax.fori_loop(0, (band >= 128).astype(jnp.int32), lambda j, c: step(0, 128, c, True), tuple(init))
            carry = step(band, t + WINDOW, carry, True)
        else:
            carry = lax.fori_loop(0, qi // 2, lambda j, c: step(2 * j * t, 2 * t, c, False), tuple(init))
            carry = lax.fori_loop(0, qi % 2, lambda j, c: step((qi - 1) * t, t, c, False), carry)
            carry = step(qi * t, t, carry, True)
        outs = []
        for pp in range(PAIRS):
            (ma, la, acca), (mb, lb, accb) = carry[6 * pp:6 * pp + 3], carry[6 * pp + 3:6 * pp + 6]
            outs.append(jnp.where(lo, acca / la, accb / lb).astype(BF16))
            lse_ref[2 * pp] = ma + jnp.log(la)
            lse_ref[2 * pp + 1] = mb + jnp.log(lb)
        o_ref[...] = jnp.concatenate(outs, axis=1)

    in_specs = [pl.BlockSpec((t, PAIRS * qw), lambda g, i: (i, g)), *_kv_specs(att, lp, lp)]
    if mode == "fox":
        in_specs += [pl.BlockSpec((nh, 1, lp), lambda g, i: (g, 0, 0))]
    if mode == "swa":
        in_specs += [pl.BlockSpec(memory_space=pltpu.SMEM)] * 2
    return pl.pallas_call(
        body, name=name, grid=(4 // PAIRS, nq), in_specs=in_specs,
        out_specs=[pl.BlockSpec((t, PAIRS * 128), lambda g, i: (i, g)), pl.BlockSpec((nh, t, 1), lambda g, i: (g, i, 0))],
        out_shape=[SDS((lp, 512), BF16), SDS((HEADS, lp, 1), F32)],
        compiler_params=_params(("parallel", "arbitrary")),
    )(q, k, v, *extra)


def _att_bwd(att, q, k, v, o, do, lse, extra, name):
    lp = q.shape[0]
    t = TILE_ATT
    nq = lp // t
    qw = att.qw
    mode = att.mode
    nh = 2 * PAIRS
    kw = 128 if mode == "swa" else PAIRS * qw
    vw = 128 if mode == "swa" else PAIRS * 128

    def body(*refs):
        q_ref, k_ref, v_ref, o_ref, do_ref, lse_ref = refs[0:6]
        n_out = {"fox": 5, "mla": 3, "swa": 4}[mode]
        outs = refs[len(refs) - n_out:]
        dq_ref, dk_ref, dv_ref = outs[0:3]
        g, qi = pl.program_id(0), pl.program_id(1)

        @pl.when(qi == 0)
        def _():
            dk_ref[...] = jnp.zeros_like(dk_ref)
            dv_ref[...] = jnp.zeros_like(dv_ref)
            if mode == "fox":
                outs[4][...] = jnp.zeros_like(outs[4])

        lo = lax.broadcasted_iota(jnp.int32, (1, 128), 1) < 64
        q_all, do_all = q_ref[...], do_ref[...]
        prod = do_all.astype(F32) * o_ref[...].astype(F32)
        q_heads, q_plain, do_heads, do_pairs, delta = [], [], [], [], []
        for pp in range(PAIRS):
            q_pp = _pair_cols(att, q_all, pp, qw)
            q_heads += att.resident(q_pp, lo, True)
            q_plain += att.moving(q_pp)
            do_pp = _pair_cols(att, do_all, pp, 128)
            do_pairs.append(do_pp)
            do_heads += _halves(do_pp, lo)
            pr_pp = _pair_cols(att, prod, pp, 128)
            delta += [jnp.sum(jnp.where(lo, pr_pp, 0.0), axis=-1, keepdims=True),
                      jnp.sum(jnp.where(lo, 0.0, pr_pp), axis=-1, keepdims=True)]
        lse_v = [lse_ref[h] for h in range(nh)]
        qpos = qi * t + lax.broadcasted_iota(jnp.int32, (t, 1), 0)

        def step(first, cols, carry, masked):
            ks = pl.multiple_of(first, 128)
            kc, vc = k_ref[pl.ds(ks, cols), :], v_ref[pl.ds(ks, cols), :]
            kpos = first + lax.broadcasted_iota(jnp.int32, (1, cols), 1)
            out, dk_parts, dv_parts = [], [], []
            for h in range(nh):
                pp = h // 2
                k_h = att.moving(_pair_cols(att, kc, pp, qw))[h % 2]
                decay = refs[6][h, :, pl.ds(ks, cols)] if mode == "fox" else None
                slope = refs[7][nh * g + h] if mode == "swa" else None
                s = att.logits(q_heads[h], k_h, qpos, kpos, decay, slope, masked)
                pr = jnp.exp(s - lse_v[h])
                ds = pr * (_dot_nt(do_heads[h], _pair_cols(att, vc, pp, 128)) - delta[h])
                dsb = ds.astype(BF16)
                out.append(carry[2 * h] + _dot(dsb, k_h))
                out.append(carry[2 * h + 1] + jnp.sum(ds, axis=-1, keepdims=True) if mode == "fox" else carry[2 * h + 1])
                dk_parts.append(_dot_tn(dsb, q_plain[h]))
                dv_parts.append(_dot_tn(pr.astype(BF16), do_pairs[pp]))
                if mode == "fox":
                    outs[4][h, 0:1, pl.ds(ks, cols)] -= jnp.sum(ds, axis=0, keepdims=True)
            rows = pl.ds(ks, cols)
            for pp in range(PAIRS):
                dv_pp = jnp.where(lo, dv_parts[2 * pp], dv_parts[2 * pp + 1])
                if att.wide:
                    dk_pp = jnp.concatenate(dk_parts[2 * pp:2 * pp + 2], axis=1) * att.scale
                else:
                    dk_pp = jnp.where(lo, dk_parts[2 * pp], dk_parts[2 * pp + 1]) * att.scale
                if mode == "swa":
                    dk_ref[rows, :] += dk_pp
                    dv_ref[rows, :] += dv_pp
                else:
                    dk_ref[rows, pp * qw:(pp + 1) * qw] += dk_pp
                    dv_ref[rows, pp * 128:(pp + 1) * 128] += dv_pp
            return tuple(out)

        init = (jnp.zeros((t, 128), F32), jnp.zeros((t, 1), F32)) * nh
        if mode == "swa":
            band = jnp.maximum(qi * t - WINDOW, 0)
            carry = lax.fori_loop(0, (band >= 128).astype(jnp.int32), lambda j, c: step(0, 128, c, True), init)
            carry = step(band, t + WINDOW, carry, True)
        else:
            carry = lax.fori_loop(0, qi // 2, lambda j, c: step(2 * j * t, 2 * t, c, False), init)
            carry = lax.fori_loop(0, qi % 2, lambda j, c: step((qi - 1) * t, t, c, False), carry)
            carry = step(qi * t, t, carry, True)
        dq = []
        for pp in range(PAIRS):
            dqa, dca, dqb, dcb = carry[4 * pp:4 * pp + 4]
            dq += [dqa, dqb] if att.wide else [jnp.where(lo, dqa, dqb)]
            if mode == "fox":
                outs[3][2 * pp] = _as_rows(dca)
                outs[3][2 * pp + 1] = _as_rows(dcb)
        dq_ref[...] = jnp.concatenate(dq, axis=1) * att.scale
        if mode == "swa":
            ds_ref = outs[3]

            @pl.when(qi == 0)
            def _():
                ds_ref[...] = jnp.zeros_like(ds_ref)

            lane = lax.broadcasted_iota(jnp.int32, (8, 128), 1)
            acc = jnp.zeros((8, 128), F32)
            for h in range(nh):
                tot = -jnp.sum(jnp.exp(refs[6][nh * g + h] - lse_v[h]) * delta[h])
                acc = acc + jnp.where(lane == h, tot, 0.0)
            ds_ref[0] += acc

    col = pl.BlockSpec((nh, t, 1), lambda g, i: (g, i, 0))
    in_specs = [pl.BlockSpec((t, PAIRS * qw), lambda g, i: (i, g)), *_kv_specs(att, lp, lp),
                pl.BlockSpec((t, PAIRS * 128), lambda g, i: (i, g)), pl.BlockSpec((t, PAIRS * 128), lambda g, i: (i, g)), col]
    out_specs = [pl.BlockSpec((t, PAIRS * qw), lambda g, i: (i, g)), pl.BlockSpec((lp, kw), lambda g, i: (0, g)),
                 pl.BlockSpec((lp, vw), lambda g, i: (0, g))]
    n_groups = 4 // PAIRS
    out_shape = [SDS((lp, 4 * qw), F32), SDS((lp, n_groups * kw), F32), SDS((lp, n_groups * vw), F32)]
    if mode == "fox":
        in_specs += [pl.BlockSpec((nh, 1, lp), lambda g, i: (g, 0, 0))]
        out_specs += [pl.BlockSpec((nh, 8, t), lambda g, i: (g, 0, i)), pl.BlockSpec((nh, 8, lp), lambda g, i: (g, 0, 0))]
        out_shape += [SDS((HEADS, 8, lp), F32)] * 2
    if mode == "swa":
        in_specs += [pl.BlockSpec(memory_space=pltpu.SMEM)] * 2
        out_specs.append(pl.BlockSpec((1, 8, 128), lambda g, i: (g, 0, 0)))
        out_shape.append(SDS((n_groups, 8, 128), F32))
    return pl.pallas_call(
        body, name=name, grid=(n_groups, nq), in_specs=in_specs, out_specs=out_specs, out_shape=out_shape,
        compiler_params=_params(("parallel", "arbitrary")),
    )(q, k, v, o, do, lse, *extra)


def _post_fwd(h, proj, outs, wb, wo, name):
    lp, d = h.shape
    tb = TILE_POST
    row = lambda w: pl.BlockSpec((tb, w), lambda i: (i, 0))

    def body(h_ref, g0, g1, g2, oa, ob, oc, wb_ref, wo_ref, o_ref):
        merged = jnp.zeros((tb, d), F32)
        for n, (g_ref, br) in enumerate(((g0, oa), (g1, ob), (g2, oc))):
            merged = merged + jax.nn.sigmoid(g_ref[...]) * _dot(br[...], wb_ref[n])
        o_ref[...] = h_ref[...] + _dot(merged.astype(BF16), wo_ref[...])

    gate = lambda n: pl.BlockSpec((tb, d), lambda i, n=n: (i, n))
    return pl.pallas_call(
        body, name=name, grid=(lp // tb,),
        in_specs=[row(d), gate(0), gate(1), gate(2), row(512), row(512), row(512),
                  pl.BlockSpec((3, 512, d), lambda i: (0, 0, 0)), pl.BlockSpec((d, d), lambda i: (0, 0))],
        out_specs=row(d), out_shape=SDS((lp, d), F32),
        compiler_params=_params(("parallel",)),
    )(h, proj, proj, proj, *outs, wb, wo)


def _post_bwd(dh, proj, outs, wb, wo, name):
    lp, d = dh.shape
    tb = TILE_POST
    row = lambda w: pl.BlockSpec((tb, w), lambda i: (i, 0))

    def body(dh_ref, g0, g1, g2, oa, ob, oc, wb_ref, wo_ref, dg_ref, doa, dob, doc, dwb_ref, dwo_ref):
        @pl.when(pl.program_id(0) == 0)
        def _():
            dwb_ref[...] = jnp.zeros_like(dwb_ref)
            dwo_ref[...] = jnp.zeros_like(dwo_ref)

        dhb = dh_ref[...].astype(BF16)
        dm = _dot_nt(dhb, wo_ref[...])
        merged = jnp.zeros((tb, d), F32)
        for n, (g_ref, br, do_ref) in enumerate(((g0, oa, doa), (g1, ob, dob), (g2, oc, doc))):
            gate = jax.nn.sigmoid(g_ref[...])
            o_n = br[...]
            y = _dot(o_n, wb_ref[n])
            merged = merged + gate * y
            dy = (dm * gate).astype(BF16)
            dg_ref[:, n * d:(n + 1) * d] = (dm * y * gate * (1.0 - gate)).astype(BF16)
            do_ref[...] = _dot_nt(dy, wb_ref[n]).astype(BF16)
            dwb_ref[n] += _dot_tn(o_n, dy)
        dwo_ref[...] += _dot_tn(merged.astype(BF16), dhb)

    gate = lambda n: pl.BlockSpec((tb, d), lambda i, n=n: (i, n))
    wb_spec = pl.BlockSpec((3, 512, d), lambda i: (0, 0, 0))
    wo_spec = pl.BlockSpec((d, d), lambda i: (0, 0))
    return pl.pallas_call(
        body, name=name, grid=(lp // tb,),
        in_specs=[row(d), gate(0), gate(1), gate(2), row(512), row(512), row(512), wb_spec, wo_spec],
        out_specs=[row(GATES_W), row(512), row(512), row(512), wb_spec, wo_spec],
        out_shape=[SDS((lp, GATES_W), BF16)] + [SDS((lp, 512), BF16)] * 3 + [SDS((3, 512, d), F32), SDS((d, d), F32)],
        compiler_params=_params(("arbitrary",)),
    )(dh, proj, proj, proj, *outs, wb, wo)


def _shift_down(x, halo, n, first):
    rows = lax.broadcasted_iota(jnp.int32, x.shape, 0)
    edge = jnp.concatenate([pltpu.roll(halo, n, 0), jnp.zeros((x.shape[0] - 8, x.shape[1]), F32)], axis=0)
    edge = jnp.where(first, 0.0, edge)
    return jnp.where(rows < n, edge, pltpu.roll(x, n, 0))


def _shift_up(x, halo, n, last):
    tb = x.shape[0]
    rows = lax.broadcasted_iota(jnp.int32, x.shape, 0)
    edge = jnp.concatenate([jnp.zeros((tb - 8, x.shape[1]), F32), pltpu.roll(halo, 8 - n, 0)], axis=0)
    edge = jnp.where(last, 0.0, edge)
    return jnp.where(rows >= tb - n, edge, pltpu.roll(x, tb - n, 0))


def _conv(u, halo, w_ref, b_ref, first):
    taps = (_shift_down(u, halo, 2, first), _shift_down(u, halo, 1, first), u)
    c = b_ref[...] + w_ref[0:1, :] * taps[0] + w_ref[1:2, :] * taps[1] + w_ref[2:3, :] * taps[2]
    return c, taps


def _ffn_specs(tb, f):
    hb = tb // 8
    cur = lambda c: pl.BlockSpec((tb, f), lambda i, c=c: (i, c))
    prev = lambda c: pl.BlockSpec((8, f), lambda i, c=c: (jnp.maximum(i * hb - 1, 0), c))
    vec = lambda r, c: pl.BlockSpec((r, f), lambda i, c=c: (0, c))
    return cur, prev, vec


def _ffn_act_fwd(u, cw, cb, name):
    lp = u.shape[0]
    f = D_FF
    tb = TILE_ROW
    cur, prev, vec = _ffn_specs(tb, f)

    def body(ug, uv, hg, hv, wg, wv, bg, bv, o_ref):
        first = pl.program_id(0) == 0
        cg, _ = _conv(ug[...], hg[...], wg, bg, first)
        cv, _ = _conv(uv[...], hv[...], wv, bv, first)
        o_ref[...] = (cg * jax.nn.sigmoid(cg) * cv).astype(BF16)

    return pl.pallas_call(
        body, name=name, grid=(lp // tb,),
        in_specs=[cur(0), cur(1), prev(0), prev(1), vec(8, 0), vec(8, 1), vec(1, 0), vec(1, 1)],
        out_specs=pl.BlockSpec((tb, f), lambda i: (i, 0)), out_shape=SDS((lp, f), BF16),
        compiler_params=_params(("parallel",)),
    )(u, u, u, u, cw, cw, cb, cb)


def _ffn_act_bwd_conv(u, dact, cw, cb, name):
    lp = u.shape[0]
    f = D_FF
    tb = TILE_CONV_BWD
    cur, prev, vec = _ffn_specs(tb, f)

    def body(ug, uv, hg, hv, wg, wv, bg, bv, da_ref, dcg_ref, dcv_ref, dwg, dwv, dbg, dbv):
        first = pl.program_id(0) == 0

        @pl.when(first)
        def _():
            for r in (dwg, dwv, dbg, dbv):
                r[...] = jnp.zeros_like(r)

        cg, tg = _conv(ug[...], hg[...], wg, bg, first)
        cv, tv = _conv(uv[...], hv[...], wv, bv, first)
        da = da_ref[...]
        sg = jax.nn.sigmoid(cg)
        dcg = da * cv * sg * (1.0 + cg * (1.0 - sg))
        dcv = da * cg * sg
        dcg_ref[...] = dcg
        dcv_ref[...] = dcv
        for dc, taps, dw, db in ((dcg, tg, dwg, dbg), (dcv, tv, dwv, dbv)):
            for n in range(3):
                dw[n:n + 1, :] += jnp.sum(dc * taps[n], axis=0, keepdims=True)
            db[0:1, :] += jnp.sum(dc, axis=0, keepdims=True)

    row = pl.BlockSpec((tb, f), lambda i: (i, 0))
    acc = pl.BlockSpec((8, f), lambda i: (0, 0))
    return pl.pallas_call(
        body, name=name, grid=(lp // tb,),
        in_specs=[cur(0), cur(1), prev(0), prev(1), vec(8, 0), vec(8, 1), vec(1, 0), vec(1, 1), row],
        out_specs=[row, row, acc, acc, acc, acc],
        out_shape=[SDS((lp, f), F32)] * 2 + [SDS((8, f), F32)] * 4,
        compiler_params=_params(("arbitrary",)),
    )(u, u, u, u, cw, cw, cb, cb, dact)


def _ffn_act_bwd_in(dcg, dcv, cw, name):
    lp = dcg.shape[0]
    f = D_FF
    tb = TILE_ROW
    nb = lp // tb
    hb = tb // 8
    cur = pl.BlockSpec((tb, f), lambda i: (i, 0))
    nxt = pl.BlockSpec((8, f), lambda i: (jnp.minimum((i + 1) * hb, nb * hb - 1), 0))
    vec = lambda c: pl.BlockSpec((8, f), lambda i, c=c: (0, c))

    def body(dg, dv, ng, nv, wg, wv, o_ref):
        last = pl.program_id(0) == nb - 1
        for c, (dc_ref, n_ref, w_ref) in enumerate(((dg, ng, wg), (dv, nv, wv))):
            dc, halo = dc_ref[...], n_ref[...]
            du = (w_ref[2:3, :] * dc + w_ref[1:2, :] * _shift_up(dc, halo, 1, last)
                  + w_ref[0:1, :] * _shift_up(dc, halo, 2, last))
            o_ref[:, c * f:(c + 1) * f] = du.astype(BF16)

    return pl.pallas_call(
        body, name=name, grid=(nb,),
        in_specs=[cur, cur, nxt, nxt, vec(0), vec(1)],
        out_specs=pl.BlockSpec((tb, 2 * f), lambda i: (i, 0)), out_shape=SDS((lp, 2 * f), BF16),
        compiler_params=_params(("parallel",)),
    )(dcg, dcv, dcg, dcv, cw, cw)


def _loss_head(y, target, n_real, name):
    lp, d = y.shape
    tb = TILE_MM

    def body(y_ref, t_ref, dy_ref, loss_ref):
        i = pl.program_id(0)

        @pl.when(i == 0)
        def _():
            loss_ref[...] = jnp.zeros_like(loss_ref)

        rows = i * tb + lax.broadcasted_iota(jnp.int32, (tb, 1), 0)
        real = (rows >= N_META) & (rows < N_META + n_real)
        diff = jnp.where(real, y_ref[...] - t_ref[...], 0.0)
        dy_ref[...] = diff * (1.0 / d)
        loss_ref[...] += (0.5 / d) * jnp.sum(diff * diff).reshape(1, 1)

    row = pl.BlockSpec((tb, d), lambda i: (i, 0))
    return pl.pallas_call(
        body, name=name, grid=(lp // tb,), in_specs=[row, row],
        out_specs=[row, pl.BlockSpec((1, 1), lambda i: (0, 0))],
        out_shape=[SDS((lp, d), F32), SDS((1, 1), F32)],
        compiler_params=_params(("arbitrary",)),
    )(y, target)


def _pad_lanes(v, width, at=0):
    return jnp.pad(v.astype(F32), (at, width - at - v.shape[0]))[None, :]


_IN_COLS = dict(fq=(0, 512), fk=(512, 512), fv=(1024, 512), ff=(1536, 8), cq=(1544, 256), ckv=(1800, 128),
                kr=(1928, 32), sq=(1960, 512), sk=(2472, 128), sv=(2600, 128), gates=(2728, 3072))


def _orig_cols(src, start, width):
    if src.ndim == 2:
        return [src[:, start:start + width]]
    per, out, pos = src.shape[2], [], start
    while pos < start + width:
        d, off = divmod(pos, per)
        take = min(per - off, start + width - pos)
        out.append(src[d, :, off:off + take])
        pos += take
    return out


class _ColumnSegments:
    def __init__(self, segments):
        self.segments = segments

    def full(self):
        return jnp.concatenate([a[:, s:s + w] for a, s, w in self.segments], axis=1)

    def blocks(self, n):
        per = sum(w for _, _, w in self.segments) // n
        out, seg, used = [], 0, 0
        for _ in range(n):
            pieces, need = [], per
            while need:
                a, s, w = self.segments[seg]
                take = min(w - used, need)
                pieces.append(a[:, s + used:s + used + take])
                used, need = used + take, need - take
                if used == w:
                    seg, used = seg + 1, 0
            out.append(jnp.concatenate(pieces, axis=1))
        return jnp.stack(out)


def _mix_params(w, big, l):
    b = lambda a: a.astype(BF16)
    win = big["w_in"]
    order = ("gates", "fq", "fk", "fv", "sq", "sk", "sv", "cq", "ckv", "kr", "ff")
    pieces = [p for name in order for p in _orig_cols(win, *_IN_COLS[name])]
    w_in = b(jnp.concatenate(pieces + [jnp.zeros((D_MODEL, 88), win.dtype)], axis=1))
    wq = jnp.pad(big["mla_w_q_up"].reshape(256, HEADS, 96), ((0, 0), (0, 0), (0, 32))).reshape(256, 1024)
    wkv = big["mla_w_kv_up"].reshape(128, HEADS, 128)
    wkk = jnp.pad(wkv[:, :, :64], ((0, 0), (0, 0), (0, 64))).reshape(128, 1024)
    wkvv = wkv[:, :, 64:].reshape(128, 512)
    tile = lambda g, n: jnp.tile(g.astype(F32), n)[None, :]
    prm = [tile(w["fox_q_g"][l], 8), tile(w["fox_k_g"][l], 8), tile(w["swa_q_g"][l], 8), tile(w["swa_k_g"][l], 2),
           _pad_lanes(w["fox_forget_b"][l], 128, FF_LANE), w["mla_q_a_g"][l][None, :], w["mla_kv_a_g"][l][None, :],
           tile(jnp.pad(w["mla_q_g"][l], (0, 32)), 8), tile(jnp.pad(w["mla_k_g"][l], (0, 32)), 8),
           wq.astype(F32), wkk.astype(F32), wkvv.astype(F32)]
    return dict(g1=w["norm1_g"][l][None, :], w_in=w_in, prm=prm, sinks=w["swa_sinks"][l].astype(F32),
                wb=b(big["w_branch"]), wo=b(big["w_o"]))


def _ffn_params(w, big, l):
    cw = jnp.pad(w["ffn_conv_w"][l].astype(F32), ((0, 5), (0, 0)))
    return dict(g2=w["norm2_g"][l][None, :], w_up=big["ffn_w_up"].astype(BF16), cw=cw,
                cb=w["ffn_conv_b"][l][None, :].astype(F32), w_down=big["ffn_w_down"].astype(BF16))


def _decay_rows(c):
    return c[:, FF_LANE:FF_LANE + HEADS].T[:, None, :]


def _from_rows(row):
    return jnp.pad(row[:, 0, :].T, ((0, 0), (FF_LANE, 128 - FF_LANE - HEADS)))


def _layer_fwd_mix(h, lw, consts, cos, sin, slopes, l):
    tag = f"l{l}_"
    xn, proj = _norm_matmul(h, lw["g1"], lw["w_in"], IN_W, tag + "in_proj")
    fq, fk, fv, mq, mk, mv, sq, skd, svd, ls = _prep_fwd(proj, lw["prm"], consts, cos, sin, tag + "prep")
    c = _cumsum([ls], False, tag + "decay_cumsum")
    c_row = _decay_rows(c)
    oa, lse_a = _att_fwd(_Att("fox"), fq, fk, fv, (c_row,), tag + "fox_fwd")
    ob, lse_b = _att_fwd(_Att("mla"), mq, mk, mv, (), tag + "mla_fwd")
    oc, lse_c = _att_fwd(_Att("swa"), sq, skd, svd, (lw["sinks"], slopes), tag + "swa_fwd")
    h2 = _post_fwd(h, proj, (oa, ob, oc), lw["wb"], lw["wo"], tag + "merge")
    saved = dict(h=h, xn=xn, proj=proj, q=(fq, mq, sq), k=(fk, mk, skd), v=(fv, mv, svd), c=c_row,
                 o=(oa, ob, oc), lse=(lse_a, lse_b, lse_c), h2=h2)
    return h2, saved


def _layer_fwd_ffn(h2, lw, l):
    tag = f"l{l}_"
    xn2, u = _norm_matmul(h2, lw["g2"], lw["w_up"], 2 * D_FF, tag + "ffn_up")
    act = _ffn_act_fwd(u, lw["cw"], lw["cb"], tag + "ffn_act")
    h3 = _matmul_residual(act, lw["w_down"], h2, tag + "ffn_down")
    return h3, dict(xn2=xn2, u=u, act=act)


def _layer_bwd_ffn(dh3, lw, sv, l):
    tag = f"l{l}_"
    f = D_FF
    dact = _matmul_nt(dh3, lw["w_down"], f, tag + "ffn_down_dx")
    dw_down = _matmul_tn(sv["act"], dh3, D_MODEL, tag + "ffn_down_dw")
    dcg, dcv, dwg, dwv, dbg, dbv = _ffn_act_bwd_conv(sv["u"], dact, lw["cw"], lw["cb"], tag + "ffn_act_dc")
    du = _ffn_act_bwd_in(dcg, dcv, lw["cw"], tag + "ffn_act_du")
    dw_up = _matmul_tn(sv["xn2"], du, f, tag + "ffn_up_dw")
    dh2, dg2 = _norm_matmul_bwd([du], lw["w_up"], sv["h2"], lw["g2"], dh3, tag + "ffn_up_dx")
    g = dict(norm2_g=dg2[0], ffn_w_up=dw_up, ffn_conv_w=jnp.concatenate([dwg[0:3], dwv[0:3]], axis=1),
             ffn_conv_b=jnp.concatenate([dbg[0], dbv[0]]), ffn_w_down=dw_down)
    return dh2, g


def _layer_bwd_mix(dh2, lw, sv, consts, folds, cos, sin, slopes, l, hook=None, merge_hook=None):
    tag = f"l{l}_"
    dgates, doa, dob, doc, dwb, dwo = _post_bwd(dh2, sv["proj"], sv["o"], lw["wb"], lw["wo"], tag + "merge_bwd")
    c_row = sv["c"]
    tick = merge_hook({"w_branch": dwb, "w_o": dwo}) if merge_hook else None
    if tick is not None:
        c_row = c_row + tick
    extras = ((c_row,), (), (lw["sinks"], slopes))
    grads = []
    for n, (mode, do) in enumerate((("fox", doa), ("mla", dob), ("swa", doc))):
        res = _att_bwd(_Att(mode), sv["q"][n], sv["k"][n], sv["v"][n], sv["o"][n], do, sv["lse"][n], extras[n],
                       tag + mode + "_bwd")
        grads.append((res[0], res[1], res[2], res[3:]))
    (dfq, dfk, dfv, (dcq, dck)), (dmq, dmk, dmv, _), (dsq, dskd, dsvd, (dsink,)) = grads
    dls = _cumsum([_from_rows(dcq), _from_rows(dck)], True, tag + "decay_cumsum_bwd")
    res = _prep_bwd(sv["proj"], lw["prm"], consts, cos, sin,
                    (dfq, dfk, dfv, dmq, dmk, dmv, dsq, dskd, dsvd, dls), folds, tag + "prep_bwd")
    dother, pg = res[0], res[1:]
    dh, dg1 = _norm_matmul_bwd([dgates, dother], lw["w_in"], sv["h"], lw["g1"], dh2, tag + "in_proj_dx")
    dw_g = _matmul_tn(sv["xn"], dgates, GATES_W, tag + "in_proj_dw_gates")
    dw_o = _matmul_tn(sv["xn"], dother, OTHER_W, tag + "in_proj_dw_other")
    d_in = _ColumnSegments([
        (dw_o, O_FQ, 1536), (dw_o, O_MISC + FF_LANE, 8), (dw_o, O_CQ, 256), (dw_o, O_CKV, 128), (dw_o, O_MISC, 32),
        (dw_o, O_SQ, 512), (dw_o, O_SK, 128), (dw_o, O_SV, 128), (dw_g, 0, GATES_W)])
    d_wq = pg[9].reshape(256, HEADS, 128)[:, :, :96].reshape(256, 768)
    d_wkv = jnp.concatenate([pg[10].reshape(128, HEADS, 128)[:, :, :64], pg[11].reshape(128, HEADS, 64)],
                            axis=2).reshape(128, 1024)
    g = dict(
        w_in=d_in, fox_forget_b=pg[4][0, FF_LANE:FF_LANE + 8], fox_q_g=pg[0][0, :64],
        fox_k_g=pg[1][0, :64], mla_q_a_g=pg[5][0], mla_w_q_up=d_wq, mla_kv_a_g=pg[6][0], mla_w_kv_up=d_wkv,
        mla_q_g=pg[7][0, :96], mla_k_g=pg[8][0, :96], swa_q_g=pg[2][0, :64], swa_k_g=pg[3][0, :64],
        swa_sinks=dsink[:, 0, 0:2 * PAIRS].reshape(HEADS), w_branch=dwb, w_o=dwo)
    tick = hook(g) if hook else None
    g["norm1_g"] = dg1[0]
    return dh, g, tick


_MIX_BIG = ("w_in", "mla_w_q_up", "mla_w_kv_up", "w_branch", "w_o")
_FFN_BIG = ("ffn_w_up", "ffn_w_down")


def _local_step(x, target, w, hook=None, fetch=None):
    if fetch is None:
        fetch = lambda l, stage, after: {n: w[n][l] for n in (_MIX_BIG if stage == "mix" else _FFN_BIG)}
    seq = x.shape[0]
    length = N_META + seq
    lp = -(-length // ROW_ALIGN) * ROW_ALIGN
    pad = lp - length
    h = jnp.concatenate([w["meta_tokens"].astype(F32), x, jnp.zeros((pad, D_MODEL), F32)], axis=0)
    tgt = jnp.pad(target, ((N_META, pad), (0, 0)))
    consts = _consts()
    folds = (_fold_matrix(512, 64), _fold_matrix(1024, 128))
    cos, sin = _rope_tables(lp)
    slopes = jnp.asarray(2.0 ** (-8.0 * np.arange(1, HEADS + 1, dtype=np.float32) / HEADS), F32)
    lws, saved = [], []
    for l in range(DEPTH):
        lw = _mix_params(w, fetch(l, "mix", h), l)
        h, sv = _layer_fwd_mix(h, lw, consts, cos, sin, slopes, l)
        lw.update(_ffn_params(w, fetch(l, "ffn", h), l))
        h, sv_ffn = _layer_fwd_ffn(h, lw, l)
        lws.append(lw)
        saved.append({**sv, **sv_ffn})
    dh, loss = _loss_head(h, tgt, seq, "loss_head")
    grads = [None] * DEPTH
    for l in reversed(range(DEPTH)):
        dh, g_ffn = _layer_bwd_ffn(dh, lws[l], saved[l], l)
        tick = hook(l, "ffn", g_ffn) if hook else None
        if tick is not None:
            lws[l]["sinks"] = lws[l]["sinks"] + tick
        mix_hook = (lambda g, l=l, g_ffn=g_ffn: hook(l, "mix", {**g_ffn, **g})) if hook else None
        merge_hook = (lambda g, l=l: hook(l, "merge", g)) if hook else None
        dh, g_mix, tick = _layer_bwd_mix(dh, lws[l], saved[l], consts, folds, cos, sin, slopes, l, mix_hook, merge_hook)
        grads[l] = {**g_ffn, **g_mix}
        if tick is not None and l > 0:
            lws[l - 1]["cw"] = lws[l - 1]["cw"] + tick
    return loss, dh[N_META:length], dh[:N_META], grads


def _place():
    return lax.axis_index("x"), lax.axis_index("y"), lax.axis_index("c")


def _flip(pos, k):
    x, y, c = pos
    return (1 - x if k & 4 else x, 1 - y if k & 2 else y, 1 - c if k & 1 else c)


def _index(pos):
    return 4 * pos[0] + 2 * pos[1] + pos[2]


def _gather(tensors, name):
    n_t = len(tensors)

    def body(*refs):
        ins, outs = refs[:n_t], refs[n_t:2 * n_t]
        send_sems, recv_sems, local_sems = refs[2 * n_t:]
        x, y, c = _place()
        me, sibling = (x, y, c), (x, y, 1 - c)
        chips = [(1 - x, y), (x, 1 - y), (1 - x, 1 - y)]

        def copy(t, k, block, to, src=None):
            dst = outs[t].at[_index(block)]
            return pltpu.make_async_remote_copy(
                src_ref=dst if src is None else src, dst_ref=dst, send_sem=send_sems.at[t, k],
                recv_sem=recv_sems.at[t, k], device_id=to, device_id_type=pl.DeviceIdType.MESH)

        local, sent = [], []
        for t in range(n_t):
            local.append(pltpu.make_async_copy(ins[t], outs[t].at[_index(me)], local_sems.at[t]))
            local[-1].start()
            sent.append(copy(t, 0, me, sibling, src=ins[t]))
            sent += [copy(t, 1 + j, me, (*chip, c), src=ins[t]) for j, chip in enumerate(chips)]
        for cp in sent:
            cp.start()
        for j, chip in enumerate(chips):
            for t in range(n_t):
                copy(t, 1 + j, (*chip, c), me).wait_recv()
                sent.append(copy(t, 4 + j, (*chip, c), sibling))
                sent[-1].start()
        for t in range(n_t):
            copy(t, 0, sibling, me).wait_recv()
            for j, chip in enumerate(chips):
                copy(t, 4 + j, (*chip, 1 - c), me).wait_recv()
        for cp in sent:
            cp.wait_send()
        for cp in local:
            cp.wait()

    any_spec = pl.BlockSpec(memory_space=pl.ANY)
    return pl.pallas_call(
        body, name=name, in_specs=[any_spec] * n_t, out_specs=[any_spec] * n_t,
        out_shape=[SDS((N_DEV,) + a.shape, a.dtype) for a in tensors],
        scratch_shapes=[pltpu.SemaphoreType.DMA((n_t, N_DEV - 1)), pltpu.SemaphoreType.DMA((n_t, N_DEV - 1)),
                        pltpu.SemaphoreType.DMA((n_t,))],
    )(*tensors)


def _exchange_start(tensors, name, gather=False, after=None):
    n_t = len(tensors)

    def body(*refs):
        ins, lands = refs[:n_t], refs[n_t:2 * n_t]
        send_sem, recv_sem = refs[2 * n_t + 1:2 * n_t + 3]
        token = refs[-1]
        me = _place()
        mine = _index(me)
        for t in range(n_t):
            for k in range(1, N_DEV):
                peer = _flip(me, k)
                pltpu.make_async_remote_copy(
                    src_ref=ins[t] if gather else ins[t].at[_index(peer)], dst_ref=lands[t].at[mine],
                    send_sem=send_sem, recv_sem=recv_sem, device_id=peer, device_id_type=pl.DeviceIdType.MESH).start()
        token[...] = jnp.zeros_like(token)

    hbm = pl.BlockSpec(memory_space=pltpu.HBM)
    sem = pl.BlockSpec(memory_space=pltpu.SEMAPHORE)
    one = pltpu.SemaphoreType.DMA(())
    land_shape = lambda a: ((N_DEV,) + a.shape) if gather else a.shape
    bufs = ([pltpu.HBM(a.shape, a.dtype) for a in tensors] + [pltpu.HBM(land_shape(a), a.dtype) for a in tensors])
    after = jnp.zeros((8, 128), F32) if after is None else after
    outs = pl.pallas_call(
        body, name=name, in_specs=[hbm] * (2 * n_t) + [pl.BlockSpec(memory_space=pl.ANY)],
        out_specs=[sem, sem] + [hbm] * (2 * n_t) + [pl.BlockSpec(memory_space=pltpu.VMEM)],
        out_shape=[one, one] + bufs + [SDS((8, 128), F32)],
        input_output_aliases={i: 2 + i for i in range(2 * n_t)},
        compiler_params=pltpu.CompilerParams(has_side_effects=pltpu.SideEffectType.DATAFLOW_SIDE_EFFECTING),
    )(*[pltpu.with_memory_space_constraint(a, pltpu.HBM) for a in tensors],
      *[pltpu.with_memory_space_constraint(lax.empty(land_shape(a), a.dtype), pltpu.HBM) for a in tensors], after)
    return outs[:-1], outs[-1][0, 0]


def _exchange_wait(state, after, name, gather=False):
    n_t = (len(state) - 2) // 2

    def body(*refs):
        send_sem, recv_sem = refs[0:2]
        ins, lands = refs[2:2 + n_t], refs[2 + n_t:2 + 2 * n_t]
        me = _place()
        for t in range(n_t):
            for k in range(1, N_DEV):
                peer = _flip(me, k)
                copy = pltpu.make_async_remote_copy(
                    src_ref=ins[t] if gather else ins[t].at[_index(peer)], dst_ref=lands[t].at[_index(peer)],
                    send_sem=send_sem, recv_sem=recv_sem, device_id=peer, device_id_type=pl.DeviceIdType.MESH)
                copy.wait_send()
                copy.wait_recv()

    hbm = pl.BlockSpec(memory_space=pltpu.HBM)
    sem = pl.BlockSpec(memory_space=pltpu.SEMAPHORE)
    bufs = [pltpu.HBM(a.shape, a.dtype) for a in state[2:]]
    outs = pl.pallas_call(
        body, name=name, in_specs=[sem, sem] + [hbm] * (2 * n_t) + [pl.BlockSpec(memory_space=pl.ANY)],
        out_specs=[hbm] * (2 * n_t), out_shape=bufs,
        input_output_aliases={2 + i: i for i in range(2 * n_t)},
        compiler_params=pltpu.CompilerParams(has_side_effects=pltpu.SideEffectType.DATAFLOW_SIDE_EFFECTING),
    )(*state, after)
    return outs[n_t:]


def _sum_slots(parts, name):
    n, rows, w = parts.shape
    tb = 8

    def body(p_ref, o_ref):
        acc = p_ref[0].astype(F32)
        for s in range(1, n):
            acc = acc + p_ref[s].astype(F32)
        o_ref[...] = acc

    return pl.pallas_call(
        body, name=name, grid=(rows // tb,),
        in_specs=[pl.BlockSpec((n, tb, w), lambda i: (0, i, 0))], out_specs=pl.BlockSpec((tb, w), lambda i: (i, 0)),
        out_shape=SDS((rows, w), F32), compiler_params=_params(("parallel",)),
    )(parts)


def _adamw(wt, m, v, parts, name, own=None, after=None):
    shape = wt.shape
    parts = parts if isinstance(parts, (list, tuple)) else [parts]
    n, w = parts[0].shape[0], shape[-1]
    rows = math.prod(shape[:-1])
    per = rows // len(parts)
    step = 16 if parts[0].dtype == BF16 else 8
    tb = max([t for t in range(step, 257, step) if per % t == 0] or [per])
    nb = per // tb
    c1 = 1.0 / (1.0 - ADAM_B1 ** ADAM_STEP)
    c2 = 1.0 / (1.0 - ADAM_B2 ** ADAM_STEP)
    state = [a.reshape(rows, w) for a in (wt, m, v)]
    n_in = 4 if own is None else 5
    outs = None
    for l in reversed(range(len(parts))):
        def body(*refs):
            idx_ref = None if own is None else refs[0]
            w_ref, m_ref, v_ref, p_ref = refs[n_in - 4:n_in] if own is None else refs[1:5]
            g_out, d_out, m_out, v_out = refs[-4:]
            g = None
            for s in range(n):
                term = p_ref[s] if own is None else jnp.where(idx_ref[0] == s, refs[5][0], p_ref[s])
                g = term.astype(F32) if g is None else g + term.astype(F32)
            m_new = ADAM_B1 * m_ref[...] + (1.0 - ADAM_B1) * g
            v_new = ADAM_B2 * v_ref[...] + (1.0 - ADAM_B2) * (g * g)
            g_out[...] = g
            m_out[...] = m_new
            v_out[...] = v_new
            d_out[...] = -ADAM_LR * ((m_new * c1) / (jnp.sqrt(v_new * c2) + ADAM_EPS) + ADAM_WD * w_ref[...])

        row = pl.BlockSpec((tb, w), lambda i, *_, l=l: (l * nb + i, 0))
        in_specs = [row, row, row, pl.BlockSpec((n, tb, w), lambda i, *_: (0, i, 0))]
        args = [*state, parts[l].reshape(n, per, w)]
        if own is not None:
            in_specs.append(pl.BlockSpec((1, tb, w), lambda i, idx: (idx[0], i, 0)))
            args.append(own[l].reshape(n, per, w))
        prev = [] if outs is None else list(outs)
        behind = [] if after is None else [after]
        in_specs += [pl.BlockSpec(memory_space=pl.ANY)] * (len(prev) + len(behind))
        n_pre = 0 if own is None else 1
        call = dict(name=f"{name}_{l}", out_shape=[SDS((rows, w), F32)] * 4,
                    input_output_aliases={n_pre + len(args) + k: k for k in range(len(prev))},
                    compiler_params=_params(("parallel",)))
        if own is None:
            outs = pl.pallas_call(body, grid=(nb,), in_specs=in_specs, out_specs=[row] * 4, **call)(*args, *prev, *behind)
        else:
            spec = pltpu.PrefetchScalarGridSpec(num_scalar_prefetch=1, grid=(nb,), in_specs=in_specs, out_specs=[row] * 4)
            idx = jnp.reshape(_index(_place()), (1,)).astype(jnp.int32)
            outs = pl.pallas_call(body, grid_spec=spec, **call)(idx, *args, *prev, *behind)
    return [o.reshape(shape) for o in outs]


_BIG = [("w_in", 2), ("mla_w_q_up", 2), ("mla_w_kv_up", 2), ("w_branch", 3), ("w_o", 1), ("ffn_w_up", 2), ("ffn_w_down", 1)]
_SMALL_SHARDED = [("meta_tokens", 1), ("ffn_conv_w", 2)]
_REPLICATED = ["norm1_g", "fox_forget_b", "fox_q_g", "fox_k_g", "mla_q_a_g", "mla_kv_a_g", "mla_q_g", "mla_k_g",
               "swa_q_g", "swa_k_g", "swa_sinks", "norm2_g", "ffn_conv_b"]
_ORDER = ["meta_tokens", "norm1_g", "w_in", "fox_forget_b", "fox_q_g", "fox_k_g", "mla_q_a_g", "mla_w_q_up",
          "mla_kv_a_g", "mla_w_kv_up", "mla_q_g", "mla_k_g", "swa_q_g", "swa_k_g", "swa_sinks", "w_branch", "w_o",
          "norm2_g", "ffn_w_up", "ffn_conv_w", "ffn_conv_b", "ffn_w_down"]


def _flat_rows(vecs, dtype, row_mult):
    flat = jnp.concatenate([a.reshape(-1).astype(dtype) for a in vecs])
    rows = -(-flat.shape[0] // (1024 * row_mult)) * row_mult
    return jnp.pad(flat, (0, rows * 1024 - flat.shape[0])).reshape(rows, 1024)


def _unflatten(flat, shapes):
    out, off = [], 0
    for s in shapes:
        n = math.prod(s)
        out.append(flat[off:off + n].reshape(s))
        off += n
    return out


def _to_full(blocks, axis):
    moved = jnp.moveaxis(blocks, 0, axis)
    s = moved.shape
    return moved.reshape(s[:axis] + (s[axis] * s[axis + 1],) + s[axis + 2:])


def _to_blocks(full, axis):
    s = full.shape
    split = full.reshape(s[:axis] + (N_DEV, s[axis] // N_DEV) + s[axis + 1:])
    return jnp.moveaxis(split, axis, 0)


def kernel(x, meta_tokens, norm1_g, w_in, fox_forget_b, fox_q_g, fox_k_g, mla_q_a_g, mla_w_q_up, mla_kv_a_g, mla_w_kv_up, mla_q_g, mla_k_g, swa_q_g, swa_k_g, swa_sinks, w_branch, w_o, norm2_g, ffn_w_up, ffn_conv_w, ffn_conv_b, ffn_w_down, loss_target, m_meta_tokens, m_norm1_g, m_w_in, m_fox_forget_b, m_fox_q_g, m_fox_k_g, m_mla_q_a_g, m_mla_w_q_up, m_mla_kv_a_g, m_mla_w_kv_up, m_mla_q_g, m_mla_k_g, m_swa_q_g, m_swa_k_g, m_swa_sinks, m_w_branch, m_w_o, m_norm2_g, m_ffn_w_up, m_ffn_conv_w, m_ffn_conv_b, m_ffn_w_down, v_meta_tokens, v_norm1_g, v_w_in, v_fox_forget_b, v_fox_q_g, v_fox_k_g, v_mla_q_a_g, v_mla_w_q_up, v_mla_kv_a_g, v_mla_w_kv_up, v_mla_q_g, v_mla_k_g, v_swa_q_g, v_swa_k_g, v_swa_sinks, v_w_branch, v_w_o, v_norm2_g, v_ffn_w_up, v_ffn_conv_w, v_ffn_conv_b, v_ffn_w_down):
    wl = dict(zip(_ORDER, (meta_tokens, norm1_g, w_in, fox_forget_b, fox_q_g, fox_k_g, mla_q_a_g, mla_w_q_up,
                           mla_kv_a_g, mla_w_kv_up, mla_q_g, mla_k_g, swa_q_g, swa_k_g, swa_sinks, w_branch, w_o,
                           norm2_g, ffn_w_up, ffn_conv_w, ffn_conv_b, ffn_w_down)))
    ml = dict(zip(_ORDER, (m_meta_tokens, m_norm1_g, m_w_in, m_fox_forget_b, m_fox_q_g, m_fox_k_g, m_mla_q_a_g,
                           m_mla_w_q_up, m_mla_kv_a_g, m_mla_w_kv_up, m_mla_q_g, m_mla_k_g, m_swa_q_g, m_swa_k_g,
                           m_swa_sinks, m_w_branch, m_w_o, m_norm2_g, m_ffn_w_up, m_ffn_conv_w, m_ffn_conv_b,
                           m_ffn_w_down)))
    vl = dict(zip(_ORDER, (v_meta_tokens, v_norm1_g, v_w_in, v_fox_forget_b, v_fox_q_g, v_fox_k_g, v_mla_q_a_g,
                           v_mla_w_q_up, v_mla_kv_a_g, v_mla_w_kv_up, v_mla_q_g, v_mla_k_g, v_swa_q_g, v_swa_k_g,
                           v_swa_sinks, v_w_branch, v_w_o, v_norm2_g, v_ffn_w_up, v_ffn_conv_w, v_ffn_conv_b,
                           v_ffn_w_down)))
    small_sh = [n for n, _ in _SMALL_SHARDED]
    big = [n for n, _ in _BIG]
    axis_of = dict(_BIG)
    idx = _index(_place())

    def to_full(n, blocks, own=None):
        if own is not None:
            sel = (jnp.arange(N_DEV) == idx).reshape((N_DEV,) + (1,) * own.ndim)
            blocks = jnp.where(sel, own[None], blocks)
        return blocks if n == "w_in" else _to_full(blocks, axis_of[n] - 1)

    local = {(n, l): wl[n][l].astype(BF16) for n in big for l in range(DEPTH)}
    got = _gather([local[(n, 0)] for n in _MIX_BIG] + [wl[n] for n in small_sh], "gather_weights_l0_mix")
    full = {n: wl[n] for n in _REPLICATED}
    for (n, axis), blocks in zip(_SMALL_SHARDED, got[len(_MIX_BIG):]):
        full[n] = _to_full(blocks, axis)
    ready = {(n, 0): to_full(n, blocks) for n, blocks in zip(_MIX_BIG, got)}
    later = {"l0_ffn": [(n, 0) for n in _FFN_BIG], "l1": [(n, 1) for n in big]}
    states = {}
    for key, names in later.items():
        states[key], tick = _exchange_start([local[e] for e in names], "gather_weights_" + key + "_start", True, got[0])
        full["norm1_g"] = full["norm1_g"] + tick

    def fetch(l, stage, after):
        key = "l0_ffn" if l == 0 else "l1"
        if (l, stage) != (0, "mix") and key in states:
            lands = _exchange_wait(states.pop(key), after, "gather_weights_" + key + "_wait", True)
            ready.update({e: to_full(e[0], blocks, local[e]) for e, blocks in zip(later[key], lands)})
        return {n: ready[(n, l)] for n in (_MIX_BIG if stage == "mix" else _FFN_BIG)}

    blocks_of = lambda g, names: [(g[n].blocks(N_DEV) if isinstance(g[n], _ColumnSegments)
                                   else _to_blocks(g[n], axis_of[n] - 1)).astype(BF16) for n in names]
    early = {}

    def hook(l, stage, g):
        if l == DEPTH - 1 and stage == "mix":
            key, names = "l1", big
        elif l == 0:
            merge = ("w_branch", "w_o")
            groups = {"ffn": _FFN_BIG, "merge": merge, "mix": tuple(n for n in _MIX_BIG if n not in merge)}
            key, names = "l0_" + stage, groups[stage]
        else:
            return None
        sends = blocks_of(g, names)
        if key == "l0_mix":
            early[key] = (names, l, sends)
            return None
        state, tick = _exchange_start(sends, "exchange_grads_" + key + "_start")
        early[key] = (names, l, sends, state)
        return tick

    loss, grad_x, grad_meta, grads = _local_step(x[0], loss_target[0], full, hook, fetch)
    result = {kind: {} for kind in ("grad", "delta", "new_m", "new_v")}
    small_grads = {k: jnp.stack([grads[l][k] for l in range(DEPTH)]) for k in grads[0] if k not in big}
    small_grads["meta_tokens"] = grad_meta
    small_full = _REPLICATED + small_sh
    mine_small = _flat_rows([small_grads[n] for n in small_full] + [loss], F32, 8)
    small_state, tick = _exchange_start([mine_small], "gather_small_grads_start", True)
    names, l, sends = early["l0_mix"]
    state, tick = _exchange_start(sends, "exchange_grads_l0_mix_start", after=jnp.reshape(tick, (1, 1)))
    early["l0_mix"] = (names, l, sends, state)
    started = jnp.reshape(tick, (1, 1))
    landed, sent = {}, {}
    after = sends[0]
    for key in ("l1", "l0_ffn", "l0_merge"):
        names, l, sends, state = early[key]
        got = _exchange_wait(state, after, "exchange_grads_" + key + "_wait")
        landed.update({(n, l): p for n, p in zip(names, got)})
        sent.update({(n, l): p for n, p in zip(names, sends)})

    def update(names):
        for n in names:
            outs = _adamw(wl[n], ml[n], vl[n], [landed[(n, l)] for l in range(DEPTH)], "adamw_" + n,
                          [sent[(n, l)] for l in range(DEPTH)], started)
            for kind, val in zip(result, outs):
                result[kind][n] = val

    update(_FFN_BIG)
    done = result["delta"]["ffn_w_up"][0, 0, :8] + result["delta"]["ffn_w_down"][0, 0, :8]
    got_small = _exchange_wait(small_state, done, "gather_small_grads_wait", True)[0]
    sel = (jnp.arange(N_DEV) == idx).reshape(N_DEV, 1, 1)
    total_small = _sum_slots(jnp.where(sel, mine_small[None], got_small), "sum_small_grads").reshape(-1)
    pieces = _unflatten(total_small, [small_grads[n].shape for n in small_full] + [()])
    loss_total = pieces[-1]
    g_small = dict(zip(small_full, pieces[:-1]))
    for n, axis in _SMALL_SHARDED:
        size = wl[n].shape[axis]
        g_small[n] = lax.dynamic_slice_in_dim(g_small[n], idx * size, size, axis)
    flat = lambda d: _flat_rows([d[n] for n in small_full], F32, 8)
    small_out = _adamw(flat(wl), flat(ml), flat(vl), flat(g_small)[None], "adamw_small")
    for kind, fs in zip(result, small_out):
        result[kind].update(zip(small_full, _unflatten(fs.reshape(-1), [wl[n].shape for n in small_full])))
    names, l, sends, state = early["l0_mix"]
    got = _exchange_wait(state, small_out[0], "exchange_grads_l0_mix_wait")
    landed.update({(n, l): p for n, p in zip(names, got)})
    sent.update({(n, l): p for n, p in zip(names, sends)})
    update(_MIX_BIG)
    outs = [loss_total, grad_x[None]]
    for kind in ("grad", "delta", "new_m", "new_v"):
        outs += [result[kind][n] for n in _ORDER]
    return tuple(outs)
```

```python
import functools
import math

import numpy as np
import jax
import jax.numpy as jnp
from jax import lax
from jax.experimental import pallas as pl
from jax.experimental.pallas import tpu as pltpu

F32, BF16 = jnp.float32, jnp.bfloat16
SDS = jax.ShapeDtypeStruct

D_MODEL = 1024
N_META = 16
EPS = 1e-6
WINDOW = 128
ROPE_THETA = 10000.0
HEADS = 8
D_FF = 2816
DEPTH = 2
N_DEV = 8
ADAM_LR, ADAM_B1, ADAM_B2, ADAM_EPS, ADAM_WD, ADAM_STEP = 0.001, 0.9, 0.999, 1e-08, 0.01, 10

ROW_ALIGN = 384
TILE_MM = 384
TILE_ROW = 192
TILE_CONV_BWD = 128
TILE_ATT = 384
TILE_POST = 384
PAIRS = 2
VMEM_LIMIT = 56 * 1024 * 1024

GATES_W = 3072
OTHER_W = 2816
IN_W = GATES_W + OTHER_W
O_FQ, O_FK, O_FV, O_SQ, O_SK, O_SV, O_CQ, O_CKV, O_MISC = 0, 512, 1024, 1536, 2048, 2176, 2304, 2560, 2688
FF_LANE = 32

NEG = -1e30


def _dot(a, b):
    return jnp.dot(a, b, preferred_element_type=F32)


def _dot_nt(a, b):
    return lax.dot_general(a, b, (((1,), (1,)), ((), ())), preferred_element_type=F32)


def _dot_tn(a, b):
    return lax.dot_general(a, b, (((0,), (0,)), ((), ())), preferred_element_type=F32)


def _params(sem):
    return pltpu.CompilerParams(dimension_semantics=sem, vmem_limit_bytes=VMEM_LIMIT)


def _rms(x, g):
    return x * lax.rsqrt(jnp.mean(x * x, axis=-1, keepdims=True) + EPS) * g


def _split_dot(x, m, pieces=2):
    acc, rest = None, x
    for _ in range(pieces):
        part = rest.astype(BF16)
        rest = rest - part.astype(F32)
        acc = _dot(part, m) if acc is None else acc + _dot(part, m)
    return acc


@jax.custom_vjp
def _sel(x, m, mt):
    return _split_dot(x, m)


_sel.defvjp(lambda x, m, mt: (_split_dot(x, m), (m, mt)), lambda res, dy: (_split_dot(dy, res[1]), None, None))


@jax.custom_vjp
def _mm(x, w):
    return _dot(x.astype(BF16), w.astype(BF16))


def _mm_bwd(res, dy):
    x, w = res
    dyb = dy.astype(BF16)
    return _dot_nt(dyb, w.astype(BF16)), _dot_tn(x.astype(BF16), dyb)


_mm.defvjp(lambda x, w: (_mm(x, w), (x, w)), _mm_bwd)


def _rot_impl(x):
    w = x.shape[1]
    lane = lax.broadcasted_iota(jnp.int32, x.shape, 1) % 128
    lo = (lane >= 64) & (lane < 80)
    hi = (lane >= 80) & (lane < 96)
    return jnp.where(hi, pltpu.roll(x, 16, 1), 0.0) - jnp.where(lo, pltpu.roll(x, w - 16, 1), 0.0)


@jax.custom_vjp
def _rot(x):
    return _rot_impl(x)


_rot.defvjp(lambda x: (_rot_impl(x), None), lambda _, dy: (-_rot_impl(dy),))


def _gnorm(x, g, e, et, dim):
    inv = lax.rsqrt(_sel(x * x, e, et) * (1.0 / dim) + EPS)
    return x * _sel(inv, et, e) * g


def _indicator(width, period):
    m = np.zeros((width, 128), np.float32)
    m[np.arange(width), np.arange(width) // period] = 1.0
    return m


def _consts():
    e64 = _indicator(512, 64)
    e128 = _indicator(1024, 128)
    sk = np.zeros((128, 1024), np.float32)
    for h in range(HEADS):
        sk[np.arange(32), 128 * h + 64 + np.arange(32)] = 1.0
    dup = np.zeros((128, 256), np.float32)
    for g in range(2):
        for r in range(2):
            dup[64 * g + np.arange(64), 128 * g + 64 * r + np.arange(64)] = 1.0
    mats = [e64, e64.T, e128, e128.T, sk, sk.T, dup, dup.T]
    return [jnp.asarray(m, BF16) for m in mats]


def _fold_matrix(width, period):
    m = np.zeros((width, 128), np.float32)
    m[np.arange(width), np.arange(width) % period] = 1.0
    return jnp.asarray(m, BF16)


def _rope_tables(lp):
    half = 16
    freqs = ROPE_THETA ** (-np.arange(half, dtype=np.float32) / half)
    ang = np.arange(lp, dtype=np.float32)[:, None] * freqs[None, :]
    cos = np.ones((lp, 128), np.float32)
    sin = np.zeros((lp, 128), np.float32)
    cos[:, 64:80] = np.cos(ang)
    cos[:, 80:96] = np.cos(ang)
    sin[:, 64:80] = np.sin(ang)
    sin[:, 80:96] = np.sin(ang)
    return jnp.asarray(cos), jnp.asarray(sin)


def _norm_matmul(h, g, w, tn, name):
    lp, d = h.shape
    n = w.shape[1]
    tb = TILE_MM

    def body(h_ref, g_ref, w_ref, xn_ref, y_ref):
        @pl.when(pl.program_id(1) == 0)
        def _():
            xn_ref[...] = _rms(h_ref[...], g_ref[...]).astype(BF16)

        y_ref[...] = _dot(xn_ref[...], w_ref[...])

    return pl.pallas_call(
        body, name=name, grid=(lp // tb, n // tn),
        in_specs=[pl.BlockSpec((tb, d), lambda i, j: (i, 0)), pl.BlockSpec((1, d), lambda i, j: (0, 0)),
                  pl.BlockSpec((d, tn), lambda i, j: (0, j))],
        out_specs=[pl.BlockSpec((tb, d), lambda i, j: (i, 0)), pl.BlockSpec((tb, tn), lambda i, j: (i, j))],
        out_shape=[SDS((lp, d), BF16), SDS((lp, n), F32)],
        compiler_params=_params(("parallel", "arbitrary")),
    )(h, g, w)


def _matmul_residual(a, w, res, name):
    m, k = a.shape
    n = w.shape[1]
    tb = TILE_MM

    def body(a_ref, w_ref, r_ref, o_ref):
        o_ref[...] = r_ref[...] + _dot(a_ref[...], w_ref[...])

    return pl.pallas_call(
        body, name=name, grid=(m // tb,),
        in_specs=[pl.BlockSpec((tb, k), lambda i: (i, 0)), pl.BlockSpec((k, n), lambda i: (0, 0)),
                  pl.BlockSpec((tb, n), lambda i: (i, 0))],
        out_specs=pl.BlockSpec((tb, n), lambda i: (i, 0)),
        out_shape=SDS((m, n), F32),
        compiler_params=_params(("parallel",)),
    )(a, w, res)


def _matmul_nt(dy, w, tn, name):
    m, k = dy.shape
    n = w.shape[0]
    tb = TILE_MM

    def body(dy_ref, w_ref, o_ref):
        o_ref[...] = _dot_nt(dy_ref[...].astype(BF16), w_ref[...])

    return pl.pallas_call(
        body, name=name, grid=(m // tb, n // tn),
        in_specs=[pl.BlockSpec((tb, k), lambda i, j: (i, 0)), pl.BlockSpec((tn, k), lambda i, j: (j, 0))],
        out_specs=pl.BlockSpec((tb, tn), lambda i, j: (i, j)),
        out_shape=SDS((m, n), F32),
        compiler_params=_params(("parallel", "arbitrary")),
    )(dy, w)


def _matmul_tn(x, dy, tn, name):
    m, k = x.shape
    n = dy.shape[1]
    tb = TILE_MM
    nb = m // tb

    def body(x_ref, dy_ref, o_ref, acc):
        i = pl.program_id(1)

        @pl.when(i == 0)
        def _():
            acc[...] = jnp.zeros_like(acc)

        acc[...] += _dot_tn(x_ref[...].astype(BF16), dy_ref[...].astype(BF16))

        @pl.when(i == nb - 1)
        def _():
            o_ref[...] = acc[...].astype(BF16)

    return pl.pallas_call(
        body, name=name, grid=(n // tn, nb),
        in_specs=[pl.BlockSpec((tb, k), lambda j, i: (i, 0)), pl.BlockSpec((tb, tn), lambda j, i: (i, j))],
        out_specs=pl.BlockSpec((k, tn), lambda j, i: (0, j)),
        out_shape=SDS((k, n), BF16),
        scratch_shapes=[pltpu.VMEM((k, tn), F32)],
        compiler_params=_params(("parallel", "arbitrary")),
    )(x, dy)


def _norm_matmul_bwd(dys, w, x, g, dres, name):
    m, d = x.shape
    tb = TILE_MM
    widths = [a.shape[1] for a in dys]
    n_dy = len(dys)

    def body(*refs):
        w_ref, x_ref, g_ref, r_ref, o_ref, dg_ref = refs[n_dy:]

        @pl.when(pl.program_id(0) == 0)
        def _():
            dg_ref[...] = jnp.zeros_like(dg_ref)

        dxn, off = None, 0
        for dy_ref, width in zip(refs[:n_dy], widths):
            part = _dot_nt(dy_ref[...], w_ref[:, off:off + width])
            dxn = part if dxn is None else dxn + part
            off += width
        _, vjp = jax.vjp(_rms, x_ref[...], g_ref[...])
        dx, dg = vjp(dxn)
        o_ref[...] = r_ref[...] + dx
        dg_ref[...] += dg

    row = pl.BlockSpec((tb, d), lambda i: (i, 0))
    vec = pl.BlockSpec((1, d), lambda i: (0, 0))
    return pl.pallas_call(
        body, name=name, grid=(m // tb,),
        in_specs=[pl.BlockSpec((tb, wd), lambda i: (i, 0)) for wd in widths]
        + [pl.BlockSpec(w.shape, lambda i: (0, 0)), row, vec, row],
        out_specs=[row, vec],
        out_shape=[SDS((m, d), F32), SDS((1, d), F32)],
        compiler_params=_params(("arbitrary",)),
    )(*dys, w, x, g, dres)


def _prep_math(pieces, prm, consts, cos, sin):
    fq, fk, sq, sk, sv, cq, ckv, misc = pieces
    gfq, gfk, gsq, gsk, fb, gqa, gkva, gmq, gmk, wq, wkk, wkv = prm
    e64, e64t, e128, e128t, skm, skt, dup, dupt = consts
    cos8 = jnp.concatenate([cos] * HEADS, axis=1)
    sin8 = jnp.concatenate([sin] * HEADS, axis=1)
    fq_n = _gnorm(fq, gfq, e64, e64t, 64)
    fk_n = _gnorm(fk, gfk, e64, e64t, 64)
    ls = jax.nn.log_sigmoid(misc + fb)
    q = _gnorm(_mm(_rms(cq, gqa), wq), gmq, e128, e128t, 96)
    mq = q * cos8 + _rot(q) * sin8
    kva = _rms(ckv, gkva)
    k = _gnorm(_mm(kva, wkk) + _sel(misc, skm, skt), gmk, e128, e128t, 96)
    mk = k * cos8 + _rot(k) * sin8
    mv = _mm(kva, wkv)
    sq_n = _gnorm(sq, gsq, e64, e64t, 64)
    sk_n = _gnorm(sk, gsk, e64[0:128], e64t[:, 0:128], 64)
    skd = _sel(sk_n, dup, dupt)
    svd = _sel(sv, dup, dupt)
    return fq_n, fk_n, ls, mq, mk, mv, sq_n, skd, svd


_PIECES = [(O_FQ, 512), (O_FK, 512), (O_SQ, 512), (O_SK, 128), (O_SV, 128), (O_CQ, 256), (O_CKV, 128), (O_MISC, 128)]
_PRM_SHAPES = [(1, 512), (1, 512), (1, 512), (1, 128), (1, 128), (1, 256), (1, 128), (1, 1024), (1, 1024),
               (256, 1024), (128, 1024), (128, 512)]
_CONST_SHAPES = [(512, 128), (128, 512), (1024, 128), (128, 1024), (128, 1024), (1024, 128), (128, 256), (256, 128)]


def _piece_specs(tb):
    def spec(off, width):
        blk = (GATES_W + off) // width
        return pl.BlockSpec((tb, width), lambda i, blk=blk: (i, blk))
    return [spec(o, w) for o, w in _PIECES] + [spec(O_FV, 512)]


def _full_specs(shapes):
    return [pl.BlockSpec(s, lambda i: (0, 0)) for s in shapes]


def _prep_fwd(proj, prm, consts, cos, sin, name):
    lp = proj.shape[0]
    tb = TILE_ROW
    row = lambda w: pl.BlockSpec((tb, w), lambda i: (i, 0))

    def body(*refs):
        pieces = [r[...] for r in refs[0:8]]
        fv = refs[8][...]
        prm_v = [r[...] for r in refs[9:21]]
        consts_v = [r[...] for r in refs[21:29]]
        cos_v, sin_v = refs[29][...], refs[30][...]
        outs = refs[31:]
        fq_n, fk_n, ls, mq, mk, mv, sq_n, skd, svd = _prep_math(pieces, prm_v, consts_v, cos_v, sin_v)
        for ref, val in zip(outs, (fq_n, fk_n, fv, mq, mk, mv, sq_n, skd, svd)):
            ref[...] = val.astype(BF16)
        outs[9][...] = ls

    widths = [512, 512, 512, 1024, 1024, 512, 512, 256, 256]
    return pl.pallas_call(
        body, name=name, grid=(lp // tb,),
        in_specs=_piece_specs(tb) + _full_specs(_PRM_SHAPES) + _full_specs(_CONST_SHAPES) + [row(128), row(128)],
        out_specs=[row(w) for w in widths] + [row(128)],
        out_shape=[SDS((lp, w), BF16) for w in widths] + [SDS((lp, 128), F32)],
        compiler_params=_params(("parallel",)),
    )(*([proj] * 9), *prm, *consts, cos, sin)


def _prep_bwd(proj, prm, consts, cos, sin, cots, folds, name):
    lp = proj.shape[0]
    tb = TILE_ROW
    row = lambda w: pl.BlockSpec((tb, w), lambda i: (i, 0))
    fold64, fold128 = folds

    def body(*refs):
        pieces = [r[...] for r in refs[0:8]]
        prm_v = [r[...] for r in refs[9:21]]
        consts_v = [r[...] for r in refs[21:29]]
        cos_v, sin_v = refs[29][...], refs[30][...]
        dfq, dfk, dfv, dmq, dmk, dmv, dsq, dskd, dsvd, dls = [r[...] for r in refs[31:41]]
        f64, f128 = refs[41][...], refs[42][...]
        d_ref = refs[43]
        g_refs = refs[44:]

        @pl.when(pl.program_id(0) == 0)
        def _():
            for r in g_refs:
                r[...] = jnp.zeros_like(r)

        f = lambda pc, pr: _prep_math(pc, pr, consts_v, cos_v, sin_v)
        _, vjp = jax.vjp(f, pieces, prm_v)
        dpc, dprm = vjp((dfq, dfk, dls, dmq, dmk, dmv, dsq, dskd, dsvd))
        d_fq, d_fk, d_sq, d_sk, d_sv, d_cq, d_ckv, d_misc = dpc
        for off, val in ((O_FQ, d_fq), (O_FK, d_fk), (O_FV, dfv), (O_SQ, d_sq), (O_SK, d_sk), (O_SV, d_sv),
                         (O_CQ, d_cq), (O_CKV, d_ckv), (O_MISC, d_misc)):
            d_ref[:, off:off + val.shape[1]] = val.astype(BF16)
        folded = {0: f64, 1: f64, 2: f64, 3: f64[0:128], 7: f128, 8: f128}
        for idx, (ref, val) in enumerate(zip(g_refs, dprm)):
            if idx in folded:
                ref[...] += _split_dot(jnp.broadcast_to(val, (8, val.shape[1])), folded[idx], 3)
            elif val.shape[0] == 1:
                ref[...] += jnp.broadcast_to(val, ref.shape)
            else:
                ref[...] += val

    g_shapes = [(8, 128), (8, 128), (8, 128), (8, 128), (8, 128), (8, 256), (8, 128), (8, 128), (8, 128),
                (256, 1024), (128, 1024), (128, 512)]
    cot_widths = [512, 512, 512, 1024, 1024, 512, 512, 256, 256, 128]
    return pl.pallas_call(
        body, name=name, grid=(lp // tb,),
        in_specs=(_piece_specs(tb) + _full_specs(_PRM_SHAPES) + _full_specs(_CONST_SHAPES) + [row(128), row(128)]
                  + [row(w) for w in cot_widths] + _full_specs([(512, 128), (1024, 128)])),
        out_specs=[row(OTHER_W)] + _full_specs(g_shapes),
        out_shape=[SDS((lp, OTHER_W), BF16)] + [SDS(s, F32) for s in g_shapes],
        compiler_params=_params(("arbitrary",)),
    )(*([proj] * 9), *prm, *consts, cos, sin, *cots, fold64, fold128)


def _cumsum(xs, reverse, name):
    lp = xs[0].shape[0]
    tb = TILE_MM
    nb = lp // tb
    n_in = len(xs)
    idx = (lambda i: (nb - 1 - i, 0)) if reverse else (lambda i: (i, 0))

    def body(*refs):
        o_ref, carry = refs[n_in], refs[n_in + 1]

        @pl.when(pl.program_id(0) == 0)
        def _():
            carry[...] = jnp.zeros_like(carry)

        x = refs[0][...]
        for r in refs[1:n_in]:
            x = x + r[...]
        r_i = lax.broadcasted_iota(jnp.int32, (tb, tb), 0)
        c_i = lax.broadcasted_iota(jnp.int32, (tb, tb), 1)
        tri = ((c_i >= r_i) if reverse else (c_i <= r_i)).astype(BF16)
        acc, rest = None, x
        for _ in range(3):
            part = rest.astype(BF16)
            rest = rest - part.astype(F32)
            acc = _dot(tri, part) if acc is None else acc + _dot(tri, part)
        o_ref[...] = acc + carry[...]
        carry[...] += jnp.sum(x, axis=0, keepdims=True)

    return pl.pallas_call(
        body, name=name, grid=(nb,),
        in_specs=[pl.BlockSpec((tb, 128), idx)] * n_in,
        out_specs=pl.BlockSpec((tb, 128), idx),
        out_shape=SDS((lp, 128), F32),
        scratch_shapes=[pltpu.VMEM((1, 128), F32)],
        compiler_params=_params(("arbitrary",)),
    )(*xs)


class _Att:
    def __init__(self, mode):
        self.mode = mode
        self.wide = mode == "mla"
        self.qw = 256 if self.wide else 128
        self.scale = (96 if mode == "mla" else 64) ** -0.5
        self.pairs = 2 * PAIRS if mode == "swa" else PAIRS

    def resident(self, x, lo, scaled):
        if self.wide:
            return x[:, 0:128], x[:, 128:256]
        if scaled:
            x = x * jnp.asarray(self.scale, x.dtype)
        zero = jnp.zeros_like(x)
        return jnp.where(lo, x, zero), jnp.where(lo, zero, x)

    def moving(self, x):
        return (x[:, 0:128], x[:, 128:256]) if self.wide else (x, x)

    def logits(self, a, b, qpos, kpos, key_decay, slope, masked):
        s = _dot_nt(a, b)
        if self.wide:
            s = s * self.scale
        if self.mode == "fox":
            s = s - key_decay
        if self.mode == "swa":
            s = s - slope * (qpos - kpos).astype(F32)
        if masked:
            ok = kpos <= qpos
            if self.mode == "swa":
                ok = ok & ((kpos < N_META) | (qpos - kpos < WINDOW))
            s = jnp.where(ok, s, NEG)
        return s


def _as_rows(col):
    return jnp.broadcast_to(col, (col.shape[0], 128)).T[0:8, :]


def _halves(x, lo):
    zero = jnp.zeros_like(x)
    return jnp.where(lo, x, zero), jnp.where(lo, zero, x)


def _kv_specs(att, lp, rows):
    row = lambda g, i: (i if rows != lp else 0, g)
    if att.mode == "swa":
        return (pl.BlockSpec((rows, 64 * att.pairs), row),) * 2
    return pl.BlockSpec((rows, att.pairs * att.qw), row), pl.BlockSpec((rows, att.pairs * 128), row)


def _pair_cols(att, x, pp, width):
    return x if x.shape[1] == width else x[:, pp * width:(pp + 1) * width]


def _kv_cols(att, x, pp, width):
    return _pair_cols(att, x, pp // 2 if att.mode == "swa" else pp, width)


def _att_fwd(att, q, k, v, extra, name):
    lp = q.shape[0]
    t = TILE_ATT
    nq = lp // t
    qw = att.qw
    mode = att.mode
    pairs = att.pairs
    nh = 2 * pairs

    def body(*refs):
        q_ref, k_ref, v_ref = refs[0:3]
        o_ref, lse_ref = refs[-2:]
        g, qi = pl.program_id(0), pl.program_id(1)
        lo = lax.broadcasted_iota(jnp.int32, (1, 128), 1) < 64
        q_all = q_ref[...]
        q_heads = [h for pp in range(pairs) for h in att.resident(_pair_cols(att, q_all, pp, qw), lo, True)]
        qpos = qi * t + lax.broadcasted_iota(jnp.int32, (t, 1), 0)

        def step(first, cols, carry, masked):
            ks = pl.multiple_of(first, 128)
            kc, vc = k_ref[pl.ds(ks, cols), :], v_ref[pl.ds(ks, cols), :]
            kpos = first + lax.broadcasted_iota(jnp.int32, (1, cols), 1)
            out = []
            for h in range(nh):
                pp = h // 2
                m, l, acc = carry[3 * h:3 * h + 3]
                k_h = att.moving(_kv_cols(att, kc, pp, qw))[h % 2]
                decay = refs[3][h, :, pl.ds(ks, cols)] if mode == "fox" else None
                slope = refs[4][nh * g + h] if mode == "swa" else None
                s = att.logits(q_heads[h], k_h, qpos, kpos, decay, slope, masked)
                m_new = jnp.maximum(m, jnp.max(s, axis=-1, keepdims=True))
                alpha = jnp.exp(m - m_new)
                pe = jnp.exp(s - m_new)
                l = alpha * l + jnp.sum(pe, axis=-1, keepdims=True)
                acc = alpha * acc + _dot(pe.astype(BF16), _kv_cols(att, vc, pp, 128))
                out += [m_new, l, acc]
            return tuple(out)

        init = []
        for h in range(nh):
            if mode == "swa":
                init += [jnp.full((t, 1), refs[3][nh * g + h], F32), jnp.ones((t, 1), F32)]
            else:
                init += [jnp.full((t, 1), NEG, F32), jnp.zeros((t, 1), F32)]
            init.append(jnp.zeros((t, 128), F32))
        if mode == "swa":
            band = jnp.maximum(qi * t - WINDOW, 0)
            carry = lax.fori_loop(0, (band >= 128).astype(jnp.int32), lambda j, c: step(0, 128, c, True), tuple(init))
            carry = step(band, t + WINDOW, carry, True)
        else:
            carry = lax.fori_loop(0, qi // 2, lambda j, c: step(2 * j * t, 2 * t, c, False), tuple(init))
            carry = lax.fori_loop(0, qi % 2, lambda j, c: step((qi - 1) * t, t, c, False), carry)
            carry = step(qi * t, t, carry, True)
        outs = []
        for pp in range(pairs):
            (ma, la, acca), (mb, lb, accb) = carry[6 * pp:6 * pp + 3], carry[6 * pp + 3:6 * pp + 6]
            outs.append(jnp.where(lo, acca / la, accb / lb).astype(BF16))
            lse_ref[2 * pp] = ma + jnp.log(la)
            lse_ref[2 * pp + 1] = mb + jnp.log(lb)
        o_ref[...] = jnp.concatenate(outs, axis=1)

    in_specs = [pl.BlockSpec((t, pairs * qw), lambda g, i: (i, g)), *_kv_specs(att, lp, lp)]
    if mode == "fox":
        in_specs += [pl.BlockSpec((nh, 1, lp), lambda g, i: (g, 0, 0))]
    if mode == "swa":
        in_specs += [pl.BlockSpec(memory_space=pltpu.SMEM)] * 2
    return pl.pallas_call(
        body, name=name, grid=(4 // pairs, nq), in_specs=in_specs,
        out_specs=[pl.BlockSpec((t, pairs * 128), lambda g, i: (i, g)), pl.BlockSpec((nh, t, 1), lambda g, i: (g, i, 0))],
        out_shape=[SDS((lp, 512), BF16), SDS((HEADS, lp, 1), F32)],
        compiler_params=_params(("parallel", "arbitrary")),
    )(q, k, v, *extra)


def _att_bwd(att, q, k, v, o, do, lse, extra, name):
    lp = q.shape[0]
    t = TILE_ATT
    nq = lp // t
    qw = att.qw
    mode = att.mode
    pairs = att.pairs
    nh = 2 * pairs
    kw = 64 * pairs if mode == "swa" else pairs * qw
    vw = 64 * pairs if mode == "swa" else pairs * 128

    def body(*refs):
        q_ref, k_ref, v_ref, o_ref, do_ref, lse_ref = refs[0:6]
        n_out = {"fox": 5, "mla": 3, "swa": 4}[mode]
        outs = refs[len(refs) - n_out:]
        dq_ref, dk_ref, dv_ref = outs[0:3]
        g, qi = pl.program_id(0), pl.program_id(1)

        @pl.when(qi == 0)
        def _():
            dk_ref[...] = jnp.zeros_like(dk_ref)
            dv_ref[...] = jnp.zeros_like(dv_ref)
            if mode == "fox":
                outs[4][...] = jnp.zeros_like(outs[4])

        lo = lax.broadcasted_iota(jnp.int32, (1, 128), 1) < 64
        q_all, do_all = q_ref[...], do_ref[...]
        prod = do_all.astype(F32) * o_ref[...].astype(F32)
        q_heads, q_plain, do_heads, do_pairs, delta = [], [], [], [], []
        for pp in range(pairs):
            q_pp = _pair_cols(att, q_all, pp, qw)
            q_heads += att.resident(q_pp, lo, True)
            q_plain += att.moving(q_pp)
            do_pp = _pair_cols(att, do_all, pp, 128)
            do_pairs.append(do_pp)
            do_heads += _halves(do_pp, lo)
            pr_pp = _pair_cols(att, prod, pp, 128)
            delta += [jnp.sum(jnp.where(lo, pr_pp, 0.0), axis=-1, keepdims=True),
                      jnp.sum(jnp.where(lo, 0.0, pr_pp), axis=-1, keepdims=True)]
        lse_v = [lse_ref[h] for h in range(nh)]
        qpos = qi * t + lax.broadcasted_iota(jnp.int32, (t, 1), 0)

        def step(first, cols, carry, masked):
            ks = pl.multiple_of(first, 128)
            kc, vc = k_ref[pl.ds(ks, cols), :], v_ref[pl.ds(ks, cols), :]
            kpos = first + lax.broadcasted_iota(jnp.int32, (1, cols), 1)
            out, dk_parts, dv_parts = [], [], []
            for h in range(nh):
                pp = h // 2
                k_h = att.moving(_kv_cols(att, kc, pp, qw))[h % 2]
                decay = refs[6][h, :, pl.ds(ks, cols)] if mode == "fox" else None
                slope = refs[7][nh * g + h] if mode == "swa" else None
                s = att.logits(q_heads[h], k_h, qpos, kpos, decay, slope, masked)
                pr = jnp.exp(s - lse_v[h])
                ds = pr * (_dot_nt(do_heads[h], _kv_cols(att, vc, pp, 128)) - delta[h])
                dsb = ds.astype(BF16)
                out.append(carry[2 * h] + _dot(dsb, k_h))
                out.append(carry[2 * h + 1] + jnp.sum(ds, axis=-1, keepdims=True) if mode == "fox" else carry[2 * h + 1])
                dk_parts.append(_dot_tn(dsb, q_plain[h]))
                dv_parts.append(_dot_tn(pr.astype(BF16), do_pairs[pp]))
                if mode == "fox":
                    outs[4][h, 0:1, pl.ds(ks, cols)] -= jnp.sum(ds, axis=0, keepdims=True)
            rows = pl.ds(ks, cols)
            for pp in range(pairs):
                dv_pp = jnp.where(lo, dv_parts[2 * pp], dv_parts[2 * pp + 1])
                if att.wide:
                    dk_pp = jnp.concatenate(dk_parts[2 * pp:2 * pp + 2], axis=1) * att.scale
                else:
                    dk_pp = jnp.where(lo, dk_parts[2 * pp], dk_parts[2 * pp + 1]) * att.scale
                if mode == "swa":
                    dk_ref[rows, (pp // 2) * 128:(pp // 2 + 1) * 128] += dk_pp
                    dv_ref[rows, (pp // 2) * 128:(pp // 2 + 1) * 128] += dv_pp
                else:
                    dk_ref[rows, pp * qw:(pp + 1) * qw] += dk_pp
                    dv_ref[rows, pp * 128:(pp + 1) * 128] += dv_pp
            return tuple(out)

        init = (jnp.zeros((t, 128), F32), jnp.zeros((t, 1), F32)) * nh
        if mode == "swa":
            band = jnp.maximum(qi * t - WINDOW, 0)
            carry = lax.fori_loop(0, (band >= 128).astype(jnp.int32), lambda j, c: step(0, 128, c, True), init)
            carry = step(band, t + WINDOW, carry, True)
        else:
            carry = lax.fori_loop(0, qi // 2, lambda j, c: step(2 * j * t, 2 * t, c, False), init)
            carry = lax.fori_loop(0, qi % 2, lambda j, c: step((qi - 1) * t, t, c, False), carry)
            carry = step(qi * t, t, carry, True)
        dq = []
        for pp in range(pairs):
            dqa, dca, dqb, dcb = carry[4 * pp:4 * pp + 4]
            dq += [dqa, dqb] if att.wide else [jnp.where(lo, dqa, dqb)]
            if mode == "fox":
                outs[3][2 * pp] = _as_rows(dca)
                outs[3][2 * pp + 1] = _as_rows(dcb)
        dq_ref[...] = jnp.concatenate(dq, axis=1) * att.scale
        if mode == "swa":
            ds_ref = outs[3]

            @pl.when(qi == 0)
            def _():
                ds_ref[...] = jnp.zeros_like(ds_ref)

            lane = lax.broadcasted_iota(jnp.int32, (8, 128), 1)
            acc = jnp.zeros((8, 128), F32)
            for h in range(nh):
                tot = -jnp.sum(jnp.exp(refs[6][nh * g + h] - lse_v[h]) * delta[h])
                acc = acc + jnp.where(lane == h, tot, 0.0)
            ds_ref[0] += acc

    col = pl.BlockSpec((nh, t, 1), lambda g, i: (g, i, 0))
    in_specs = [pl.BlockSpec((t, pairs * qw), lambda g, i: (i, g)), *_kv_specs(att, lp, lp),
                pl.BlockSpec((t, pairs * 128), lambda g, i: (i, g)), pl.BlockSpec((t, pairs * 128), lambda g, i: (i, g)), col]
    out_specs = [pl.BlockSpec((t, pairs * qw), lambda g, i: (i, g)), pl.BlockSpec((lp, kw), lambda g, i: (0, g)),
                 pl.BlockSpec((lp, vw), lambda g, i: (0, g))]
    n_groups = 4 // pairs
    out_shape = [SDS((lp, 4 * qw), F32), SDS((lp, n_groups * kw), F32), SDS((lp, n_groups * vw), F32)]
    if mode == "fox":
        in_specs += [pl.BlockSpec((nh, 1, lp), lambda g, i: (g, 0, 0))]
        out_specs += [pl.BlockSpec((nh, 8, t), lambda g, i: (g, 0, i)), pl.BlockSpec((nh, 8, lp), lambda g, i: (g, 0, 0))]
        out_shape += [SDS((HEADS, 8, lp), F32)] * 2
    if mode == "swa":
        in_specs += [pl.BlockSpec(memory_space=pltpu.SMEM)] * 2
        out_specs.append(pl.BlockSpec((1, 8, 128), lambda g, i: (g, 0, 0)))
        out_shape.append(SDS((n_groups, 8, 128), F32))
    return pl.pallas_call(
        body, name=name, grid=(n_groups, nq), in_specs=in_specs, out_specs=out_specs, out_shape=out_shape,
        compiler_params=_params(("parallel", "arbitrary")),
    )(q, k, v, o, do, lse, *extra)


def _post_fwd(h, proj, outs, wb, wo, name):
    lp, d = h.shape
    tb = TILE_POST
    row = lambda w: pl.BlockSpec((tb, w), lambda i: (i, 0))

    def body(h_ref, g0, g1, g2, oa, ob, oc, wb_ref, wo_ref, o_ref):
        merged = jnp.zeros((tb, d), F32)
        for n, (g_ref, br) in enumerate(((g0, oa), (g1, ob), (g2, oc))):
            merged = merged + jax.nn.sigmoid(g_ref[...]) * _dot(br[...], wb_ref[n])
        o_ref[...] = h_ref[...] + _dot(merged.astype(BF16), wo_ref[...])

    gate = lambda n: pl.BlockSpec((tb, d), lambda i, n=n: (i, n))
    return pl.pallas_call(
        body, name=name, grid=(lp // tb,),
        in_specs=[row(d), gate(0), gate(1), gate(2), row(512), row(512), row(512),
                  pl.BlockSpec((3, 512, d), lambda i: (0, 0, 0)), pl.BlockSpec((d, d), lambda i: (0, 0))],
        out_specs=row(d), out_shape=SDS((lp, d), F32),
        compiler_params=_params(("parallel",)),
    )(h, proj, proj, proj, *outs, wb, wo)


def _post_bwd(dh, proj, outs, wb, wo, name):
    lp, d = dh.shape
    tb = TILE_POST
    row = lambda w: pl.BlockSpec((tb, w), lambda i: (i, 0))

    def body(dh_ref, g0, g1, g2, oa, ob, oc, wb_ref, wo_ref, dg_ref, doa, dob, doc, dwb_ref, dwo_ref):
        @pl.when(pl.program_id(0) == 0)
        def _():
            dwb_ref[...] = jnp.zeros_like(dwb_ref)
            dwo_ref[...] = jnp.zeros_like(dwo_ref)

        dhb = dh_ref[...].astype(BF16)
        dm = _dot_nt(dhb, wo_ref[...])
        merged = jnp.zeros((tb, d), F32)
        for n, (g_ref, br, do_ref) in enumerate(((g0, oa, doa), (g1, ob, dob), (g2, oc, doc))):
            gate = jax.nn.sigmoid(g_ref[...])
            o_n = br[...]
            y = _dot(o_n, wb_ref[n])
            merged = merged + gate * y
            dy = (dm * gate).astype(BF16)
            dg_ref[:, n * d:(n + 1) * d] = (dm * y * gate * (1.0 - gate)).astype(BF16)
            do_ref[...] = _dot_nt(dy, wb_ref[n]).astype(BF16)
            dwb_ref[n] += _dot_tn(o_n, dy)
        dwo_ref[...] += _dot_tn(merged.astype(BF16), dhb)

    gate = lambda n: pl.BlockSpec((tb, d), lambda i, n=n: (i, n))
    wb_spec = pl.BlockSpec((3, 512, d), lambda i: (0, 0, 0))
    wo_spec = pl.BlockSpec((d, d), lambda i: (0, 0))
    return pl.pallas_call(
        body, name=name, grid=(lp // tb,),
        in_specs=[row(d), gate(0), gate(1), gate(2), row(512), row(512), row(512), wb_spec, wo_spec],
        out_specs=[row(GATES_W), row(512), row(512), row(512), wb_spec, wo_spec],
        out_shape=[SDS((lp, GATES_W), BF16)] + [SDS((lp, 512), BF16)] * 3 + [SDS((3, 512, d), F32), SDS((d, d), F32)],
        compiler_params=_params(("arbitrary",)),
    )(dh, proj, proj, proj, *outs, wb, wo)


def _shift_down(x, halo, n, first):
    rows = lax.broadcasted_iota(jnp.int32, x.shape, 0)
    edge = jnp.concatenate([pltpu.roll(halo, n, 0), jnp.zeros((x.shape[0] - 8, x.shape[1]), F32)], axis=0)
    edge = jnp.where(first, 0.0, edge)
    return jnp.where(rows < n, edge, pltpu.roll(x, n, 0))


def _shift_up(x, halo, n, last):
    tb = x.shape[0]
    rows = lax.broadcasted_iota(jnp.int32, x.shape, 0)
    edge = jnp.concatenate([jnp.zeros((tb - 8, x.shape[1]), F32), pltpu.roll(halo, 8 - n, 0)], axis=0)
    edge = jnp.where(last, 0.0, edge)
    return jnp.where(rows >= tb - n, edge, pltpu.roll(x, tb - n, 0))


def _conv(u, halo, w_ref, b_ref, first):
    taps = (_shift_down(u, halo, 2, first), _shift_down(u, halo, 1, first), u)
    c = b_ref[...] + w_ref[0:1, :] * taps[0] + w_ref[1:2, :] * taps[1] + w_ref[2:3, :] * taps[2]
    return c, taps


def _ffn_specs(tb, f):
    hb = tb // 8
    cur = lambda c: pl.BlockSpec((tb, f), lambda i, c=c: (i, c))
    prev = lambda c: pl.BlockSpec((8, f), lambda i, c=c: (jnp.maximum(i * hb - 1, 0), c))
    vec = lambda r, c: pl.BlockSpec((r, f), lambda i, c=c: (0, c))
    return cur, prev, vec


def _ffn_act_fwd(u, cw, cb, name):
    lp = u.shape[0]
    f = D_FF
    tb = TILE_ROW
    cur, prev, vec = _ffn_specs(tb, f)

    def body(ug, uv, hg, hv, wg, wv, bg, bv, o_ref):
        first = pl.program_id(0) == 0
        cg, _ = _conv(ug[...], hg[...], wg, bg, first)
        cv, _ = _conv(uv[...], hv[...], wv, bv, first)
        o_ref[...] = (cg * jax.nn.sigmoid(cg) * cv).astype(BF16)

    return pl.pallas_call(
        body, name=name, grid=(lp // tb,),
        in_specs=[cur(0), cur(1), prev(0), prev(1), vec(8, 0), vec(8, 1), vec(1, 0), vec(1, 1)],
        out_specs=pl.BlockSpec((tb, f), lambda i: (i, 0)), out_shape=SDS((lp, f), BF16),
        compiler_params=_params(("parallel",)),
    )(u, u, u, u, cw, cw, cb, cb)


def _ffn_act_bwd_conv(u, dact, cw, cb, name):
    lp = u.shape[0]
    f = D_FF
    tb = TILE_CONV_BWD
    cur, prev, vec = _ffn_specs(tb, f)

    def body(ug, uv, hg, hv, wg, wv, bg, bv, da_ref, dcg_ref, dcv_ref, dwg, dwv, dbg, dbv):
        first = pl.program_id(0) == 0

        @pl.when(first)
        def _():
            for r in (dwg, dwv, dbg, dbv):
                r[...] = jnp.zeros_like(r)

        cg, tg = _conv(ug[...], hg[...], wg, bg, first)
        cv, tv = _conv(uv[...], hv[...], wv, bv, first)
        da = da_ref[...]
        sg = jax.nn.sigmoid(cg)
        dcg = da * cv * sg * (1.0 + cg * (1.0 - sg))
        dcv = da * cg * sg
        dcg_ref[...] = dcg
        dcv_ref[...] = dcv
        for dc, taps, dw, db in ((dcg, tg, dwg, dbg), (dcv, tv, dwv, dbv)):
            for n in range(3):
                dw[n:n + 1, :] += jnp.sum(dc * taps[n], axis=0, keepdims=True)
            db[0:1, :] += jnp.sum(dc, axis=0, keepdims=True)

    row = pl.BlockSpec((tb, f), lambda i: (i, 0))
    acc = pl.BlockSpec((8, f), lambda i: (0, 0))
    return pl.pallas_call(
        body, name=name, grid=(lp // tb,),
        in_specs=[cur(0), cur(1), prev(0), prev(1), vec(8, 0), vec(8, 1), vec(1, 0), vec(1, 1), row],
        out_specs=[row, row, acc, acc, acc, acc],
        out_shape=[SDS((lp, f), F32)] * 2 + [SDS((8, f), F32)] * 4,
        compiler_params=_params(("arbitrary",)),
    )(u, u, u, u, cw, cw, cb, cb, dact)


def _ffn_act_bwd_in(dcg, dcv, cw, name):
    lp = dcg.shape[0]
    f = D_FF
    tb = TILE_ROW
    nb = lp // tb
    hb = tb // 8
    cur = pl.BlockSpec((tb, f), lambda i: (i, 0))
    nxt = pl.BlockSpec((8, f), lambda i: (jnp.minimum((i + 1) * hb, nb * hb - 1), 0))
    vec = lambda c: pl.BlockSpec((8, f), lambda i, c=c: (0, c))

    def body(dg, dv, ng, nv, wg, wv, o_ref):
        last = pl.program_id(0) == nb - 1
        for c, (dc_ref, n_ref, w_ref) in enumerate(((dg, ng, wg), (dv, nv, wv))):
            dc, halo = dc_ref[...], n_ref[...]
            du = (w_ref[2:3, :] * dc + w_ref[1:2, :] * _shift_up(dc, halo, 1, last)
                  + w_ref[0:1, :] * _shift_up(dc, halo, 2, last))
            o_ref[:, c * f:(c + 1) * f] = du.astype(BF16)

    return pl.pallas_call(
        body, name=name, grid=(nb,),
        in_specs=[cur, cur, nxt, nxt, vec(0), vec(1)],
        out_specs=pl.BlockSpec((tb, 2 * f), lambda i: (i, 0)), out_shape=SDS((lp, 2 * f), BF16),
        compiler_params=_params(("parallel",)),
    )(dcg, dcv, dcg, dcv, cw, cw)


def _loss_head(y, target, n_real, name):
    lp, d = y.shape
    tb = TILE_MM

    def body(y_ref, t_ref, dy_ref, loss_ref):
        i = pl.program_id(0)

        @pl.when(i == 0)
        def _():
            loss_ref[...] = jnp.zeros_like(loss_ref)

        rows = i * tb + lax.broadcasted_iota(jnp.int32, (tb, 1), 0)
        real = (rows >= N_META) & (rows < N_META + n_real)
        diff = jnp.where(real, y_ref[...] - t_ref[...], 0.0)
        dy_ref[...] = diff * (1.0 / d)
        loss_ref[...] += (0.5 / d) * jnp.sum(diff * diff).reshape(1, 1)

    row = pl.BlockSpec((tb, d), lambda i: (i, 0))
    return pl.pallas_call(
        body, name=name, grid=(lp // tb,), in_specs=[row, row],
        out_specs=[row, pl.BlockSpec((1, 1), lambda i: (0, 0))],
        out_shape=[SDS((lp, d), F32), SDS((1, 1), F32)],
        compiler_params=_params(("arbitrary",)),
    )(y, target)


def _pad_lanes(v, width, at=0):
    return jnp.pad(v.astype(F32), (at, width - at - v.shape[0]))[None, :]


_IN_COLS = dict(fq=(0, 512), fk=(512, 512), fv=(1024, 512), ff=(1536, 8), cq=(1544, 256), ckv=(1800, 128),
                kr=(1928, 32), sq=(1960, 512), sk=(2472, 128), sv=(2600, 128), gates=(2728, 3072))


def _orig_cols(src, start, width):
    if src.ndim == 2:
        return [src[:, start:start + width]]
    per, out, pos = src.shape[2], [], start
    while pos < start + width:
        d, off = divmod(pos, per)
        take = min(per - off, start + width - pos)
        out.append(src[d, :, off:off + take])
        pos += take
    return out


class _ColumnSegments:
    def __init__(self, segments):
        self.segments = segments

    def full(self):
        return jnp.concatenate([a[:, s:s + w] for a, s, w in self.segments], axis=1)

    def blocks(self, n):
        per = sum(w for _, _, w in self.segments) // n
        out, seg, used = [], 0, 0
        for _ in range(n):
            pieces, need = [], per
            while need:
                a, s, w = self.segments[seg]
                take = min(w - used, need)
                pieces.append(a[:, s + used:s + used + take])
                used, need = used + take, need - take
                if used == w:
                    seg, used = seg + 1, 0
            out.append(jnp.concatenate(pieces, axis=1))
        return jnp.stack(out)


def _mix_params(w, big, l):
    b = lambda a: a.astype(BF16)
    win = big["w_in"]
    order = ("gates", "fq", "fk", "fv", "sq", "sk", "sv", "cq", "ckv", "kr", "ff")
    pieces = [p for name in order for p in _orig_cols(win, *_IN_COLS[name])]
    w_in = b(jnp.concatenate(pieces + [jnp.zeros((D_MODEL, 88), win.dtype)], axis=1))
    wq = jnp.pad(big["mla_w_q_up"].reshape(256, HEADS, 96), ((0, 0), (0, 0), (0, 32))).reshape(256, 1024)
    wkv = big["mla_w_kv_up"].reshape(128, HEADS, 128)
    wkk = jnp.pad(wkv[:, :, :64], ((0, 0), (0, 0), (0, 64))).reshape(128, 1024)
    wkvv = wkv[:, :, 64:].reshape(128, 512)
    tile = lambda g, n: jnp.tile(g.astype(F32), n)[None, :]
    prm = [tile(w["fox_q_g"][l], 8), tile(w["fox_k_g"][l], 8), tile(w["swa_q_g"][l], 8), tile(w["swa_k_g"][l], 2),
           _pad_lanes(w["fox_forget_b"][l], 128, FF_LANE), w["mla_q_a_g"][l][None, :], w["mla_kv_a_g"][l][None, :],
           tile(jnp.pad(w["mla_q_g"][l], (0, 32)), 8), tile(jnp.pad(w["mla_k_g"][l], (0, 32)), 8),
           wq.astype(F32), wkk.astype(F32), wkvv.astype(F32)]
    return dict(g1=w["norm1_g"][l][None, :], w_in=w_in, prm=prm, sinks=w["swa_sinks"][l].astype(F32),
                wb=b(big["w_branch"]), wo=b(big["w_o"]))


def _ffn_params(w, big, l):
    cw = jnp.pad(w["ffn_conv_w"][l].astype(F32), ((0, 5), (0, 0)))
    return dict(g2=w["norm2_g"][l][None, :], w_up=big["ffn_w_up"].astype(BF16), cw=cw,
                cb=w["ffn_conv_b"][l][None, :].astype(F32), w_down=big["ffn_w_down"].astype(BF16))


def _decay_rows(c):
    return c[:, FF_LANE:FF_LANE + HEADS].T[:, None, :]


def _from_rows(row):
    return jnp.pad(row[:, 0, :].T, ((0, 0), (FF_LANE, 128 - FF_LANE - HEADS)))


def _layer_fwd_mix(h, lw, consts, cos, sin, slopes, l):
    tag = f"l{l}_"
    xn, proj = _norm_matmul(h, lw["g1"], lw["w_in"], IN_W, tag + "in_proj")
    fq, fk, fv, mq, mk, mv, sq, skd, svd, ls = _prep_fwd(proj, lw["prm"], consts, cos, sin, tag + "prep")
    c = _cumsum([ls], False, tag + "decay_cumsum")
    c_row = _decay_rows(c)
    oa, lse_a = _att_fwd(_Att("fox"), fq, fk, fv, (c_row,), tag + "fox_fwd")
    ob, lse_b = _att_fwd(_Att("mla"), mq, mk, mv, (), tag + "mla_fwd")
    oc, lse_c = _att_fwd(_Att("swa"), sq, skd, svd, (lw["sinks"], slopes), tag + "swa_fwd")
    h2 = _post_fwd(h, proj, (oa, ob, oc), lw["wb"], lw["wo"], tag + "merge")
    saved = dict(h=h, xn=xn, proj=proj, q=(fq, mq, sq), k=(fk, mk, skd), v=(fv, mv, svd), c=c_row,
                 o=(oa, ob, oc), lse=(lse_a, lse_b, lse_c), h2=h2)
    return h2, saved


def _layer_fwd_ffn(h2, lw, l):
    tag = f"l{l}_"
    xn2, u = _norm_matmul(h2, lw["g2"], lw["w_up"], 2 * D_FF, tag + "ffn_up")
    act = _ffn_act_fwd(u, lw["cw"], lw["cb"], tag + "ffn_act")
    h3 = _matmul_residual(act, lw["w_down"], h2, tag + "ffn_down")
    return h3, dict(xn2=xn2, u=u, act=act)


def _layer_bwd_ffn(dh3, lw, sv, l):
    tag = f"l{l}_"
    f = D_FF
    dact = _matmul_nt(dh3, lw["w_down"], f, tag + "ffn_down_dx")
    dw_down = _matmul_tn(sv["act"], dh3, D_MODEL, tag + "ffn_down_dw")
    dcg, dcv, dwg, dwv, dbg, dbv = _ffn_act_bwd_conv(sv["u"], dact, lw["cw"], lw["cb"], tag + "ffn_act_dc")
    du = _ffn_act_bwd_in(dcg, dcv, lw["cw"], tag + "ffn_act_du")
    dw_up = _matmul_tn(sv["xn2"], du, f, tag + "ffn_up_dw")
    dh2, dg2 = _norm_matmul_bwd([du], lw["w_up"], sv["h2"], lw["g2"], dh3, tag + "ffn_up_dx")
    g = dict(norm2_g=dg2[0], ffn_w_up=dw_up, ffn_conv_w=jnp.concatenate([dwg[0:3], dwv[0:3]], axis=1),
             ffn_conv_b=jnp.concatenate([dbg[0], dbv[0]]), ffn_w_down=dw_down)
    return dh2, g


def _layer_bwd_mix(dh2, lw, sv, consts, folds, cos, sin, slopes, l, hook=None, merge_hook=None):
    tag = f"l{l}_"
    dgates, doa, dob, doc, dwb, dwo = _post_bwd(dh2, sv["proj"], sv["o"], lw["wb"], lw["wo"], tag + "merge_bwd")
    c_row = sv["c"]
    tick = merge_hook({"w_branch": dwb, "w_o": dwo}) if merge_hook else None
    if tick is not None:
        c_row = c_row + tick
    extras = ((c_row,), (), (lw["sinks"], slopes))
    grads = []
    for n, (mode, do) in enumerate((("fox", doa), ("mla", dob), ("swa", doc))):
        res = _att_bwd(_Att(mode), sv["q"][n], sv["k"][n], sv["v"][n], sv["o"][n], do, sv["lse"][n], extras[n],
                       tag + mode + "_bwd")
        grads.append((res[0], res[1], res[2], res[3:]))
    (dfq, dfk, dfv, (dcq, dck)), (dmq, dmk, dmv, _), (dsq, dskd, dsvd, (dsink,)) = grads
    dls = _cumsum([_from_rows(dcq), _from_rows(dck)], True, tag + "decay_cumsum_bwd")
    res = _prep_bwd(sv["proj"], lw["prm"], consts, cos, sin,
                    (dfq, dfk, dfv, dmq, dmk, dmv, dsq, dskd, dsvd, dls), folds, tag + "prep_bwd")
    dother, pg = res[0], res[1:]
    dh, dg1 = _norm_matmul_bwd([dgates, dother], lw["w_in"], sv["h"], lw["g1"], dh2, tag + "in_proj_dx")
    dw_g = _matmul_tn(sv["xn"], dgates, GATES_W, tag + "in_proj_dw_gates")
    dw_o = _matmul_tn(sv["xn"], dother, OTHER_W, tag + "in_proj_dw_other")
    d_in = _ColumnSegments([
        (dw_o, O_FQ, 1536), (dw_o, O_MISC + FF_LANE, 8), (dw_o, O_CQ, 256), (dw_o, O_CKV, 128), (dw_o, O_MISC, 32),
        (dw_o, O_SQ, 512), (dw_o, O_SK, 128), (dw_o, O_SV, 128), (dw_g, 0, GATES_W)])
    d_wq = pg[9].reshape(256, HEADS, 128)[:, :, :96].reshape(256, 768)
    d_wkv = jnp.concatenate([pg[10].reshape(128, HEADS, 128)[:, :, :64], pg[11].reshape(128, HEADS, 64)],
                            axis=2).reshape(128, 1024)
    g = dict(
        w_in=d_in, fox_forget_b=pg[4][0, FF_LANE:FF_LANE + 8], fox_q_g=pg[0][0, :64],
        fox_k_g=pg[1][0, :64], mla_q_a_g=pg[5][0], mla_w_q_up=d_wq, mla_kv_a_g=pg[6][0], mla_w_kv_up=d_wkv,
        mla_q_g=pg[7][0, :96], mla_k_g=pg[8][0, :96], swa_q_g=pg[2][0, :64], swa_k_g=pg[3][0, :64],
        swa_sinks=dsink[:, 0, :HEADS // dsink.shape[0]].reshape(HEADS), w_branch=dwb, w_o=dwo)
    tick = hook(g) if hook else None
    g["norm1_g"] = dg1[0]
    return dh, g, tick


_MIX_BIG = ("w_in", "mla_w_q_up", "mla_w_kv_up", "w_branch", "w_o")
_FFN_BIG = ("ffn_w_up", "ffn_w_down")


def _local_step(x, target, w, hook=None, fetch=None):
    if fetch is None:
        fetch = lambda l, stage, after: {n: w[n][l] for n in (_MIX_BIG if stage == "mix" else _FFN_BIG)}
    seq = x.shape[0]
    length = N_META + seq
    lp = -(-length // ROW_ALIGN) * ROW_ALIGN
    pad = lp - length
    h = jnp.concatenate([w["meta_tokens"].astype(F32), x, jnp.zeros((pad, D_MODEL), F32)], axis=0)
    tgt = jnp.pad(target, ((N_META, pad), (0, 0)))
    consts = _consts()
    folds = (_fold_matrix(512, 64), _fold_matrix(1024, 128))
    cos, sin = _rope_tables(lp)
    slopes = jnp.asarray(2.0 ** (-8.0 * np.arange(1, HEADS + 1, dtype=np.float32) / HEADS), F32)
    lws, saved = [], []
    for l in range(DEPTH):
        lw = _mix_params(w, fetch(l, "mix", h), l)
        h, sv = _layer_fwd_mix(h, lw, consts, cos, sin, slopes, l)
        lw.update(_ffn_params(w, fetch(l, "ffn", h), l))
        h, sv_ffn = _layer_fwd_ffn(h, lw, l)
        lws.append(lw)
        saved.append({**sv, **sv_ffn})
    dh, loss = _loss_head(h, tgt, seq, "loss_head")
    grads = [None] * DEPTH
    for l in reversed(range(DEPTH)):
        dh, g_ffn = _layer_bwd_ffn(dh, lws[l], saved[l], l)
        tick = hook(l, "ffn", g_ffn) if hook else None
        if tick is not None:
            lws[l]["sinks"] = lws[l]["sinks"] + tick
        mix_hook = (lambda g, l=l, g_ffn=g_ffn: hook(l, "mix", {**g_ffn, **g})) if hook else None
        merge_hook = (lambda g, l=l: hook(l, "merge", g)) if hook else None
        dh, g_mix, tick = _layer_bwd_mix(dh, lws[l], saved[l], consts, folds, cos, sin, slopes, l, mix_hook, merge_hook)
        grads[l] = {**g_ffn, **g_mix}
        if tick is not None and l > 0:
            lws[l - 1]["cw"] = lws[l - 1]["cw"] + tick
    return loss, dh[N_META:length], dh[:N_META], grads


def _place():
    return lax.axis_index("x"), lax.axis_index("y"), lax.axis_index("c")


def _flip(pos, k):
    x, y, c = pos
    return (1 - x if k & 4 else x, 1 - y if k & 2 else y, 1 - c if k & 1 else c)


def _index(pos):
    return 4 * pos[0] + 2 * pos[1] + pos[2]


def _gather(tensors, name):
    n_t = len(tensors)

    def body(*refs):
        ins, outs = refs[:n_t], refs[n_t:2 * n_t]
        send_sems, recv_sems, local_sems = refs[2 * n_t:]
        x, y, c = _place()
        me, sibling = (x, y, c), (x, y, 1 - c)
        chips = [(1 - x, y), (x, 1 - y), (1 - x, 1 - y)]

        def copy(t, k, block, to, src=None):
            dst = outs[t].at[_index(block)]
            return pltpu.make_async_remote_copy(
                src_ref=dst if src is None else src, dst_ref=dst, send_sem=send_sems.at[t, k],
                recv_sem=recv_sems.at[t, k], device_id=to, device_id_type=pl.DeviceIdType.MESH)

        local, sent = [], []
        for t in range(n_t):
            local.append(pltpu.make_async_copy(ins[t], outs[t].at[_index(me)], local_sems.at[t]))
            local[-1].start()
            sent.append(copy(t, 0, me, sibling, src=ins[t]))
            sent += [copy(t, 1 + j, me, (*chip, c), src=ins[t]) for j, chip in enumerate(chips)]
        for cp in sent:
            cp.start()
        for j, chip in enumerate(chips):
            for t in range(n_t):
                copy(t, 1 + j, (*chip, c), me).wait_recv()
                sent.append(copy(t, 4 + j, (*chip, c), sibling))
                sent[-1].start()
        for t in range(n_t):
            copy(t, 0, sibling, me).wait_recv()
            for j, chip in enumerate(chips):
                copy(t, 4 + j, (*chip, 1 - c), me).wait_recv()
        for cp in sent:
            cp.wait_send()
        for cp in local:
            cp.wait()

    any_spec = pl.BlockSpec(memory_space=pl.ANY)
    return pl.pallas_call(
        body, name=name, in_specs=[any_spec] * n_t, out_specs=[any_spec] * n_t,
        out_shape=[SDS((N_DEV,) + a.shape, a.dtype) for a in tensors],
        scratch_shapes=[pltpu.SemaphoreType.DMA((n_t, N_DEV - 1)), pltpu.SemaphoreType.DMA((n_t, N_DEV - 1)),
                        pltpu.SemaphoreType.DMA((n_t,))],
    )(*tensors)


def _exchange_start(tensors, name, gather=False, after=None):
    n_t = len(tensors)

    def body(*refs):
        ins, lands = refs[:n_t], refs[n_t:2 * n_t]
        send_sem, recv_sem = refs[2 * n_t + 1:2 * n_t + 3]
        token = refs[-1]
        me = _place()
        mine = _index(me)
        for t in range(n_t):
            for k in range(1, N_DEV):
                peer = _flip(me, k)
                pltpu.make_async_remote_copy(
                    src_ref=ins[t] if gather else ins[t].at[_index(peer)], dst_ref=lands[t].at[mine],
                    send_sem=send_sem, recv_sem=recv_sem, device_id=peer, device_id_type=pl.DeviceIdType.MESH).start()
        token[...] = jnp.zeros_like(token)

    hbm = pl.BlockSpec(memory_space=pltpu.HBM)
    sem = pl.BlockSpec(memory_space=pltpu.SEMAPHORE)
    one = pltpu.SemaphoreType.DMA(())
    land_shape = lambda a: ((N_DEV,) + a.shape) if gather else a.shape
    bufs = ([pltpu.HBM(a.shape, a.dtype) for a in tensors] + [pltpu.HBM(land_shape(a), a.dtype) for a in tensors])
    after = jnp.zeros((8, 128), F32) if after is None else after
    outs = pl.pallas_call(
        body, name=name, in_specs=[hbm] * (2 * n_t) + [pl.BlockSpec(memory_space=pl.ANY)],
        out_specs=[sem, sem] + [hbm] * (2 * n_t) + [pl.BlockSpec(memory_space=pltpu.VMEM)],
        out_shape=[one, one] + bufs + [SDS((8, 128), F32)],
        input_output_aliases={i: 2 + i for i in range(2 * n_t)},
        compiler_params=pltpu.CompilerParams(has_side_effects=pltpu.SideEffectType.DATAFLOW_SIDE_EFFECTING),
    )(*[pltpu.with_memory_space_constraint(a, pltpu.HBM) for a in tensors],
      *[pltpu.with_memory_space_constraint(lax.empty(land_shape(a), a.dtype), pltpu.HBM) for a in tensors], after)
    return outs[:-1], outs[-1][0, 0]


def _exchange_wait(state, after, name, gather=False):
    n_t = (len(state) - 2) // 2

    def body(*refs):
        send_sem, recv_sem = refs[0:2]
        ins, lands = refs[2:2 + n_t], refs[2 + n_t:2 + 2 * n_t]
        me = _place()
        for t in range(n_t):
            for k in range(1, N_DEV):
                peer = _flip(me, k)
                copy = pltpu.make_async_remote_copy(
                    src_ref=ins[t] if gather else ins[t].at[_index(peer)], dst_ref=lands[t].at[_index(peer)],
                    send_sem=send_sem, recv_sem=recv_sem, device_id=peer, device_id_type=pl.DeviceIdType.MESH)
                copy.wait_send()
                copy.wait_recv()

    hbm = pl.BlockSpec(memory_space=pltpu.HBM)
    sem = pl.BlockSpec(memory_space=pltpu.SEMAPHORE)
    bufs = [pltpu.HBM(a.shape, a.dtype) for a in state[2:]]
    outs = pl.pallas_call(
        body, name=name, in_specs=[sem, sem] + [hbm] * (2 * n_t) + [pl.BlockSpec(memory_space=pl.ANY)],
        out_specs=[hbm] * (2 * n_t), out_shape=bufs,
        input_output_aliases={2 + i: i for i in range(2 * n_t)},
        compiler_params=pltpu.CompilerParams(has_side_effects=pltpu.SideEffectType.DATAFLOW_SIDE_EFFECTING),
    )(*state, after)
    return outs[n_t:]


def _sum_slots(parts, name):
    n, rows, w = parts.shape
    tb = 8

    def body(p_ref, o_ref):
        acc = p_ref[0].astype(F32)
        for s in range(1, n):
            acc = acc + p_ref[s].astype(F32)
        o_ref[...] = acc

    return pl.pallas_call(
        body, name=name, grid=(rows // tb,),
        in_specs=[pl.BlockSpec((n, tb, w), lambda i: (0, i, 0))], out_specs=pl.BlockSpec((tb, w), lambda i: (i, 0)),
        out_shape=SDS((rows, w), F32), compiler_params=_params(("parallel",)),
    )(parts)


def _adamw(wt, m, v, parts, name, own=None, after=None):
    shape = wt.shape
    parts = parts if isinstance(parts, (list, tuple)) else [parts]
    n, w = parts[0].shape[0], shape[-1]
    rows = math.prod(shape[:-1])
    per = rows // len(parts)
    step = 16 if parts[0].dtype == BF16 else 8
    tb = max([t for t in range(step, 257, step) if per % t == 0] or [per])
    nb = per // tb
    c1 = 1.0 / (1.0 - ADAM_B1 ** ADAM_STEP)
    c2 = 1.0 / (1.0 - ADAM_B2 ** ADAM_STEP)
    state = [a.reshape(rows, w) for a in (wt, m, v)]
    n_in = 4 if own is None else 5
    outs = None
    for l in reversed(range(len(parts))):
        def body(*refs):
            idx_ref = None if own is None else refs[0]
            w_ref, m_ref, v_ref, p_ref = refs[n_in - 4:n_in] if own is None else refs[1:5]
            g_out, d_out, m_out, v_out = refs[-4:]
            g = None
            for s in range(n):
                term = p_ref[s] if own is None else jnp.where(idx_ref[0] == s, refs[5][0], p_ref[s])
                g = term.astype(F32) if g is None else g + term.astype(F32)
            m_new = ADAM_B1 * m_ref[...] + (1.0 - ADAM_B1) * g
            v_new = ADAM_B2 * v_ref[...] + (1.0 - ADAM_B2) * (g * g)
            g_out[...] = g
            m_out[...] = m_new
            v_out[...] = v_new
            d_out[...] = -ADAM_LR * ((m_new * c1) / (jnp.sqrt(v_new * c2) + ADAM_EPS) + ADAM_WD * w_ref[...])

        row = pl.BlockSpec((tb, w), lambda i, *_, l=l: (l * nb + i, 0))
        in_specs = [row, row, row, pl.BlockSpec((n, tb, w), lambda i, *_: (0, i, 0))]
        args = [*state, parts[l].reshape(n, per, w)]
        if own is not None:
            in_specs.append(pl.BlockSpec((1, tb, w), lambda i, idx: (idx[0], i, 0)))
            args.append(own[l].reshape(n, per, w))
        prev = [] if outs is None else list(outs)
        behind = [] if after is None else [after]
        in_specs += [pl.BlockSpec(memory_space=pl.ANY)] * (len(prev) + len(behind))
        n_pre = 0 if own is None else 1
        call = dict(name=f"{name}_{l}", out_shape=[SDS((rows, w), F32)] * 4,
                    input_output_aliases={n_pre + len(args) + k: k for k in range(len(prev))},
                    compiler_params=_params(("parallel",)))
        if own is None:
            outs = pl.pallas_call(body, grid=(nb,), in_specs=in_specs, out_specs=[row] * 4, **call)(*args, *prev, *behind)
        else:
            spec = pltpu.PrefetchScalarGridSpec(num_scalar_prefetch=1, grid=(nb,), in_specs=in_specs, out_specs=[row] * 4)
            idx = jnp.reshape(_index(_place()), (1,)).astype(jnp.int32)
            outs = pl.pallas_call(body, grid_spec=spec, **call)(idx, *args, *prev, *behind)
    return [o.reshape(shape) for o in outs]


_BIG = [("w_in", 2), ("mla_w_q_up", 2), ("mla_w_kv_up", 2), ("w_branch", 3), ("w_o", 1), ("ffn_w_up", 2), ("ffn_w_down", 1)]
_SMALL_SHARDED = [("meta_tokens", 1), ("ffn_conv_w", 2)]
_REPLICATED = ["norm1_g", "fox_forget_b", "fox_q_g", "fox_k_g", "mla_q_a_g", "mla_kv_a_g", "mla_q_g", "mla_k_g",
               "swa_q_g", "swa_k_g", "swa_sinks", "norm2_g", "ffn_conv_b"]
_ORDER = ["meta_tokens", "norm1_g", "w_in", "fox_forget_b", "fox_q_g", "fox_k_g", "mla_q_a_g", "mla_w_q_up",
          "mla_kv_a_g", "mla_w_kv_up", "mla_q_g", "mla_k_g", "swa_q_g", "swa_k_g", "swa_sinks", "w_branch", "w_o",
          "norm2_g", "ffn_w_up", "ffn_conv_w", "ffn_conv_b", "ffn_w_down"]


def _flat_rows(vecs, dtype, row_mult):
    flat = jnp.concatenate([a.reshape(-1).astype(dtype) for a in vecs])
    rows = -(-flat.shape[0] // (1024 * row_mult)) * row_mult
    return jnp.pad(flat, (0, rows * 1024 - flat.shape[0])).reshape(rows, 1024)


def _unflatten(flat, shapes):
    out, off = [], 0
    for s in shapes:
        n = math.prod(s)
        out.append(flat[off:off + n].reshape(s))
        off += n
    return out


def _to_full(blocks, axis):
    moved = jnp.moveaxis(blocks, 0, axis)
    s = moved.shape
    return moved.reshape(s[:axis] + (s[axis] * s[axis + 1],) + s[axis + 2:])


def _to_blocks(full, axis):
    s = full.shape
    split = full.reshape(s[:axis] + (N_DEV, s[axis] // N_DEV) + s[axis + 1:])
    return jnp.moveaxis(split, axis, 0)


def kernel(x, meta_tokens, norm1_g, w_in, fox_forget_b, fox_q_g, fox_k_g, mla_q_a_g, mla_w_q_up, mla_kv_a_g, mla_w_kv_up, mla_q_g, mla_k_g, swa_q_g, swa_k_g, swa_sinks, w_branch, w_o, norm2_g, ffn_w_up, ffn_conv_w, ffn_conv_b, ffn_w_down, loss_target, m_meta_tokens, m_norm1_g, m_w_in, m_fox_forget_b, m_fox_q_g, m_fox_k_g, m_mla_q_a_g, m_mla_w_q_up, m_mla_kv_a_g, m_mla_w_kv_up, m_mla_q_g, m_mla_k_g, m_swa_q_g, m_swa_k_g, m_swa_sinks, m_w_branch, m_w_o, m_norm2_g, m_ffn_w_up, m_ffn_conv_w, m_ffn_conv_b, m_ffn_w_down, v_meta_tokens, v_norm1_g, v_w_in, v_fox_forget_b, v_fox_q_g, v_fox_k_g, v_mla_q_a_g, v_mla_w_q_up, v_mla_kv_a_g, v_mla_w_kv_up, v_mla_q_g, v_mla_k_g, v_swa_q_g, v_swa_k_g, v_swa_sinks, v_w_branch, v_w_o, v_norm2_g, v_ffn_w_up, v_ffn_conv_w, v_ffn_conv_b, v_ffn_w_down):
    wl = dict(zip(_ORDER, (meta_tokens, norm1_g, w_in, fox_forget_b, fox_q_g, fox_k_g, mla_q_a_g, mla_w_q_up,
                           mla_kv_a_g, mla_w_kv_up, mla_q_g, mla_k_g, swa_q_g, swa_k_g, swa_sinks, w_branch, w_o,
                           norm2_g, ffn_w_up, ffn_conv_w, ffn_conv_b, ffn_w_down)))
    ml = dict(zip(_ORDER, (m_meta_tokens, m_norm1_g, m_w_in, m_fox_forget_b, m_fox_q_g, m_fox_k_g, m_mla_q_a_g,
                           m_mla_w_q_up, m_mla_kv_a_g, m_mla_w_kv_up, m_mla_q_g, m_mla_k_g, m_swa_q_g, m_swa_k_g,
                           m_swa_sinks, m_w_branch, m_w_o, m_norm2_g, m_ffn_w_up, m_ffn_conv_w, m_ffn_conv_b,
                           m_ffn_w_down)))
    vl = dict(zip(_ORDER, (v_meta_tokens, v_norm1_g, v_w_in, v_fox_forget_b, v_fox_q_g, v_fox_k_g, v_mla_q_a_g,
                           v_mla_w_q_up, v_mla_kv_a_g, v_mla_w_kv_up, v_mla_q_g, v_mla_k_g, v_swa_q_g, v_swa_k_g,
                           v_swa_sinks, v_w_branch, v_w_o, v_norm2_g, v_ffn_w_up, v_ffn_conv_w, v_ffn_conv_b,
                           v_ffn_w_down)))
    small_sh = [n for n, _ in _SMALL_SHARDED]
    big = [n for n, _ in _BIG]
    axis_of = dict(_BIG)
    idx = _index(_place())

    def to_full(n, blocks, own=None):
        if own is not None:
            sel = (jnp.arange(N_DEV) == idx).reshape((N_DEV,) + (1,) * own.ndim)
            blocks = jnp.where(sel, own[None], blocks)
        return blocks if n == "w_in" else _to_full(blocks, axis_of[n] - 1)

    local = {(n, l): wl[n][l].astype(BF16) for n in big for l in range(DEPTH)}
    got = _gather([local[(n, 0)] for n in _MIX_BIG] + [wl[n] for n in small_sh], "gather_weights_l0_mix")
    full = {n: wl[n] for n in _REPLICATED}
    for (n, axis), blocks in zip(_SMALL_SHARDED, got[len(_MIX_BIG):]):
        full[n] = _to_full(blocks, axis)
    ready = {(n, 0): to_full(n, blocks) for n, blocks in zip(_MIX_BIG, got)}
    later = {"l0_ffn": [(n, 0) for n in _FFN_BIG], "l1": [(n, 1) for n in big]}
    states = {}
    for key, names in later.items():
        states[key], tick = _exchange_start([local[e] for e in names], "gather_weights_" + key + "_start", True, got[0])
        full["norm1_g"] = full["norm1_g"] + tick

    def fetch(l, stage, after):
        key = "l0_ffn" if l == 0 else "l1"
        if (l, stage) != (0, "mix") and key in states:
            lands = _exchange_wait(states.pop(key), after, "gather_weights_" + key + "_wait", True)
            ready.update({e: to_full(e[0], blocks, local[e]) for e, blocks in zip(later[key], lands)})
        return {n: ready[(n, l)] for n in (_MIX_BIG if stage == "mix" else _FFN_BIG)}

    blocks_of = lambda g, names: [(g[n].blocks(N_DEV) if isinstance(g[n], _ColumnSegments)
                                   else _to_blocks(g[n], axis_of[n] - 1)).astype(BF16) for n in names]
    early = {}

    def hook(l, stage, g):
        if l == DEPTH - 1 and stage == "mix":
            key, names = "l1", big
        elif l == 0:
            merge = ("w_branch", "w_o")
            groups = {"ffn": _FFN_BIG, "merge": merge, "mix": tuple(n for n in _MIX_BIG if n not in merge)}
            key, names = "l0_" + stage, groups[stage]
        else:
            return None
        sends = blocks_of(g, names)
        if key == "l0_mix":
            early[key] = (names, l, sends)
            return None
        state, tick = _exchange_start(sends, "exchange_grads_" + key + "_start")
        early[key] = (names, l, sends, state)
        return tick

    loss, grad_x, grad_meta, grads = _local_step(x[0], loss_target[0], full, hook, fetch)
    result = {kind: {} for kind in ("grad", "delta", "new_m", "new_v")}
    small_grads = {k: jnp.stack([grads[l][k] for l in range(DEPTH)]) for k in grads[0] if k not in big}
    small_grads["meta_tokens"] = grad_meta
    small_full = _REPLICATED + small_sh
    mine_small = _flat_rows([small_grads[n] for n in small_full] + [loss], F32, 8)
    small_state, tick = _exchange_start([mine_small], "gather_small_grads_start", True)
    names, l, sends = early["l0_mix"]
    state, tick = _exchange_start(sends, "exchange_grads_l0_mix_start", after=jnp.reshape(tick, (1, 1)))
    early["l0_mix"] = (names, l, sends, state)
    started = jnp.reshape(tick, (1, 1))
    landed, sent = {}, {}
    after = sends[0]
    for key in ("l1", "l0_ffn", "l0_merge"):
        names, l, sends, state = early[key]
        got = _exchange_wait(state, after, "exchange_grads_" + key + "_wait")
        landed.update({(n, l): p for n, p in zip(names, got)})
        sent.update({(n, l): p for n, p in zip(names, sends)})

    def update(names):
        for n in names:
            outs = _adamw(wl[n], ml[n], vl[n], [landed[(n, l)] for l in range(DEPTH)], "adamw_" + n,
                          [sent[(n, l)] for l in range(DEPTH)], started)
            for kind, val in zip(result, outs):
                result[kind][n] = val

    update(_FFN_BIG)
    done = result["delta"]["ffn_w_up"][0, 0, :8] + result["delta"]["ffn_w_down"][0, 0, :8]
    got_small = _exchange_wait(small_state, done, "gather_small_grads_wait", True)[0]
    sel = (jnp.arange(N_DEV) == idx).reshape(N_DEV, 1, 1)
    total_small = _sum_slots(jnp.where(sel, mine_small[None], got_small), "sum_small_grads").reshape(-1)
    pieces = _unflatten(total_small, [small_grads[n].shape for n in small_full] + [()])
    loss_total = pieces[-1]
    g_small = dict(zip(small_full, pieces[:-1]))
    for n, axis in _SMALL_SHARDED:
        size = wl[n].shape[axis]
        g_small[n] = lax.dynamic_slice_in_dim(g_small[n], idx * size, size, axis)
    flat = lambda d: _flat_rows([d[n] for n in small_full], F32, 8)
    small_out = _adamw(flat(wl), flat(ml), flat(vl), flat(g_small)[None], "adamw_small")
    for kind, fs in zip(result, small_out):
        result[kind].update(zip(small_full, _unflatten(fs.reshape(-1), [wl[n].shape for n in small_full])))
    names, l, sends, state = early["l0_mix"]
    got = _exchange_wait(state, small_out[0], "exchange_grads_l0_mix_wait")
    landed.update({(n, l): p for n, p in zip(names, got)})
    sent.update({(n, l): p for n, p in zip(names, sends)})
    update(_MIX_BIG)
    outs = [loss_total, grad_x[None]]
    for kind in ("grad", "delta", "new_m", "new_v"):
        outs += [result[kind][n] for n in _ORDER]
    return tuple(outs)
```

```python
import functools
import math

import numpy as np
import jax
import jax.numpy as jnp
from jax import lax
from jax.experimental import pallas as pl
from jax.experimental.pallas import tpu as pltpu

F32, BF16 = jnp.float32, jnp.bfloat16
SDS = jax.ShapeDtypeStruct

D_MODEL = 1024
N_META = 16
EPS = 1e-6
WINDOW = 128
ROPE_THETA = 10000.0
HEADS = 8
D_FF = 2816
DEPTH = 2
N_DEV = 8
ADAM_LR, ADAM_B1, ADAM_B2, ADAM_EPS, ADAM_WD, ADAM_STEP = 0.001, 0.9, 0.999, 1e-08, 0.01, 10

ROW_ALIGN = 384
TILE_MM = 384
TILE_ROW = 192
TILE_CONV_BWD = 128
TILE_ATT = 384
TILE_POST = 384
PAIRS = 2
VMEM_LIMIT = 56 * 1024 * 1024

GATES_W = 3072
OTHER_W = 2816
IN_W = GATES_W + OTHER_W
O_FQ, O_FK, O_FV, O_SQ, O_SK, O_SV, O_CQ, O_CKV, O_MISC = 0, 512, 1024, 1536, 2048, 2176, 2304, 2560, 2688
FF_LANE = 32

NEG = -1e30


def _dot(a, b):
    return jnp.dot(a, b, preferred_element_type=F32)


def _dot_nt(a, b):
    return lax.dot_general(a, b, (((1,), (1,)), ((), ())), preferred_element_type=F32)


def _dot_tn(a, b):
    return lax.dot_general(a, b, (((0,), (0,)), ((), ())), preferred_element_type=F32)


def _params(sem):
    return pltpu.CompilerParams(dimension_semantics=sem, vmem_limit_bytes=VMEM_LIMIT)


def _rms(x, g):
    return x * lax.rsqrt(jnp.mean(x * x, axis=-1, keepdims=True) + EPS) * g


def _split_dot(x, m, pieces=2):
    acc, rest = None, x
    for _ in range(pieces):
        part = rest.astype(BF16)
        rest = rest - part.astype(F32)
        acc = _dot(part, m) if acc is None else acc + _dot(part, m)
    return acc


@jax.custom_vjp
def _sel(x, m, mt):
    return _split_dot(x, m)


_sel.defvjp(lambda x, m, mt: (_split_dot(x, m), (m, mt)), lambda res, dy: (_split_dot(dy, res[1]), None, None))


@jax.custom_vjp
def _mm(x, w):
    return _dot(x.astype(BF16), w.astype(BF16))


def _mm_bwd(res, dy):
    x, w = res
    dyb = dy.astype(BF16)
    return _dot_nt(dyb, w.astype(BF16)), _dot_tn(x.astype(BF16), dyb)


_mm.defvjp(lambda x, w: (_mm(x, w), (x, w)), _mm_bwd)


def _rot_impl(x):
    w = x.shape[1]
    lane = lax.broadcasted_iota(jnp.int32, x.shape, 1) % 128
    lo = (lane >= 64) & (lane < 80)
    hi = (lane >= 80) & (lane < 96)
    return jnp.where(hi, pltpu.roll(x, 16, 1), 0.0) - jnp.where(lo, pltpu.roll(x, w - 16, 1), 0.0)


@jax.custom_vjp
def _rot(x):
    return _rot_impl(x)


_rot.defvjp(lambda x: (_rot_impl(x), None), lambda _, dy: (-_rot_impl(dy),))


def _gnorm(x, g, e, et, dim):
    inv = lax.rsqrt(_sel(x * x, e, et) * (1.0 / dim) + EPS)
    return x * _sel(inv, et, e) * g


def _indicator(width, period):
    m = np.zeros((width, 128), np.float32)
    m[np.arange(width), np.arange(width) // period] = 1.0
    return m


def _consts():
    e64 = _indicator(512, 64)
    e128 = _indicator(1024, 128)
    sk = np.zeros((128, 1024), np.float32)
    for h in range(HEADS):
        sk[np.arange(32), 128 * h + 64 + np.arange(32)] = 1.0
    dup = np.zeros((128, 256), np.float32)
    for g in range(2):
        for r in range(2):
            dup[64 * g + np.arange(64), 128 * g + 64 * r + np.arange(64)] = 1.0
    mats = [e64, e64.T, e128, e128.T, sk, sk.T, dup, dup.T]
    return [jnp.asarray(m, BF16) for m in mats]


def _fold_matrix(width, period):
    m = np.zeros((width, 128), np.float32)
    m[np.arange(width), np.arange(width) % period] = 1.0
    return jnp.asarray(m, BF16)


def _rope_tables(lp):
    half = 16
    freqs = ROPE_THETA ** (-np.arange(half, dtype=np.float32) / half)
    ang = np.arange(lp, dtype=np.float32)[:, None] * freqs[None, :]
    cos = np.ones((lp, 128), np.float32)
    sin = np.zeros((lp, 128), np.float32)
    cos[:, 64:80] = np.cos(ang)
    cos[:, 80:96] = np.cos(ang)
    sin[:, 64:80] = np.sin(ang)
    sin[:, 80:96] = np.sin(ang)
    return jnp.asarray(cos), jnp.asarray(sin)


def _norm_matmul(h, g, w, tn, name):
    lp, d = h.shape
    n = w.shape[1]
    tb = TILE_MM

    def body(h_ref, g_ref, w_ref, xn_ref, y_ref):
        @pl.when(pl.program_id(1) == 0)
        def _():
            xn_ref[...] = _rms(h_ref[...], g_ref[...]).astype(BF16)

        y_ref[...] = _dot(xn_ref[...], w_ref[...])

    return pl.pallas_call(
        body, name=name, grid=(lp // tb, n // tn),
        in_specs=[pl.BlockSpec((tb, d), lambda i, j: (i, 0)), pl.BlockSpec((1, d), lambda i, j: (0, 0)),
                  pl.BlockSpec((d, tn), lambda i, j: (0, j))],
        out_specs=[pl.BlockSpec((tb, d), lambda i, j: (i, 0)), pl.BlockSpec((tb, tn), lambda i, j: (i, j))],
        out_shape=[SDS((lp, d), BF16), SDS((lp, n), F32)],
        compiler_params=_params(("parallel", "arbitrary")),
    )(h, g, w)


def _matmul_residual(a, w, res, name):
    m, k = a.shape
    n = w.shape[1]
    tb = TILE_MM

    def body(a_ref, w_ref, r_ref, o_ref):
        o_ref[...] = r_ref[...] + _dot(a_ref[...], w_ref[...])

    return pl.pallas_call(
        body, name=name, grid=(m // tb,),
        in_specs=[pl.BlockSpec((tb, k), lambda i: (i, 0)), pl.BlockSpec((k, n), lambda i: (0, 0)),
                  pl.BlockSpec((tb, n), lambda i: (i, 0))],
        out_specs=pl.BlockSpec((tb, n), lambda i: (i, 0)),
        out_shape=SDS((m, n), F32),
        compiler_params=_params(("parallel",)),
    )(a, w, res)


def _matmul_nt(dy, w, tn, name):
    m, k = dy.shape
    n = w.shape[0]
    tb = TILE_MM

    def body(dy_ref, w_ref, o_ref):
        o_ref[...] = _dot_nt(dy_ref[...].astype(BF16), w_ref[...])

    return pl.pallas_call(
        body, name=name, grid=(m // tb, n // tn),
        in_specs=[pl.BlockSpec((tb, k), lambda i, j: (i, 0)), pl.BlockSpec((tn, k), lambda i, j: (j, 0))],
        out_specs=pl.BlockSpec((tb, tn), lambda i, j: (i, j)),
        out_shape=SDS((m, n), F32),
        compiler_params=_params(("parallel", "arbitrary")),
    )(dy, w)


def _matmul_tn(x, dy, tn, name):
    m, k = x.shape
    n = dy.shape[1]
    tb = TILE_MM
    nb = m // tb

    def body(x_ref, dy_ref, o_ref, acc):
        i = pl.program_id(1)

        @pl.when(i == 0)
        def _():
            acc[...] = jnp.zeros_like(acc)

        acc[...] += _dot_tn(x_ref[...].astype(BF16), dy_ref[...].astype(BF16))

        @pl.when(i == nb - 1)
        def _():
            o_ref[...] = acc[...].astype(BF16)

    return pl.pallas_call(
        body, name=name, grid=(n // tn, nb),
        in_specs=[pl.BlockSpec((tb, k), lambda j, i: (i, 0)), pl.BlockSpec((tb, tn), lambda j, i: (i, j))],
        out_specs=pl.BlockSpec((k, tn), lambda j, i: (0, j)),
        out_shape=SDS((k, n), BF16),
        scratch_shapes=[pltpu.VMEM((k, tn), F32)],
        compiler_params=_params(("parallel", "arbitrary")),
    )(x, dy)


def _norm_matmul_bwd(dys, w, x, g, dres, name):
    m, d = x.shape
    tb = TILE_MM
    widths = [a.shape[1] for a in dys]
    n_dy = len(dys)

    def body(*refs):
        w_ref, x_ref, g_ref, r_ref, o_ref, dg_ref = refs[n_dy:]

        @pl.when(pl.program_id(0) == 0)
        def _():
            dg_ref[...] = jnp.zeros_like(dg_ref)

        dxn, off = None, 0
        for dy_ref, width in zip(refs[:n_dy], widths):
            part = _dot_nt(dy_ref[...], w_ref[:, off:off + width])
            dxn = part if dxn is None else dxn + part
            off += width
        _, vjp = jax.vjp(_rms, x_ref[...], g_ref[...])
        dx, dg = vjp(dxn)
        o_ref[...] = r_ref[...] + dx
        dg_ref[...] += dg

    row = pl.BlockSpec((tb, d), lambda i: (i, 0))
    vec = pl.BlockSpec((1, d), lambda i: (0, 0))
    return pl.pallas_call(
        body, name=name, grid=(m // tb,),
        in_specs=[pl.BlockSpec((tb, wd), lambda i: (i, 0)) for wd in widths]
        + [pl.BlockSpec(w.shape, lambda i: (0, 0)), row, vec, row],
        out_specs=[row, vec],
        out_shape=[SDS((m, d), F32), SDS((1, d), F32)],
        compiler_params=_params(("arbitrary",)),
    )(*dys, w, x, g, dres)


def _prep_math(pieces, prm, consts, cos, sin):
    fq, fk, sq, sk, sv, cq, ckv, misc = pieces
    gfq, gfk, gsq, gsk, fb, gqa, gkva, gmq, gmk, wq, wkk, wkv = prm
    e64, e64t, e128, e128t, skm, skt, dup, dupt = consts
    cos8 = jnp.concatenate([cos] * HEADS, axis=1)
    sin8 = jnp.concatenate([sin] * HEADS, axis=1)
    fq_n = _gnorm(fq, gfq, e64, e64t, 64)
    fk_n = _gnorm(fk, gfk, e64, e64t, 64)
    ls = jax.nn.log_sigmoid(misc + fb)
    q = _gnorm(_mm(_rms(cq, gqa), wq), gmq, e128, e128t, 96)
    mq = q * cos8 + _rot(q) * sin8
    kva = _rms(ckv, gkva)
    k = _gnorm(_mm(kva, wkk) + _sel(misc, skm, skt), gmk, e128, e128t, 96)
    mk = k * cos8 + _rot(k) * sin8
    mv = _mm(kva, wkv)
    sq_n = _gnorm(sq, gsq, e64, e64t, 64)
    sk_n = _gnorm(sk, gsk, e64[0:128], e64t[:, 0:128], 64)
    skd = _sel(sk_n, dup, dupt)
    svd = _sel(sv, dup, dupt)
    return fq_n, fk_n, ls, mq, mk, mv, sq_n, skd, svd


_PIECES = [(O_FQ, 512), (O_FK, 512), (O_SQ, 512), (O_SK, 128), (O_SV, 128), (O_CQ, 256), (O_CKV, 128), (O_MISC, 128)]
_PRM_SHAPES = [(1, 512), (1, 512), (1, 512), (1, 128), (1, 128), (1, 256), (1, 128), (1, 1024), (1, 1024),
               (256, 1024), (128, 1024), (128, 512)]
_CONST_SHAPES = [(512, 128), (128, 512), (1024, 128), (128, 1024), (128, 1024), (1024, 128), (128, 256), (256, 128)]


def _piece_specs(tb):
    def spec(off, width):
        blk = (GATES_W + off) // width
        return pl.BlockSpec((tb, width), lambda i, blk=blk: (i, blk))
    return [spec(o, w) for o, w in _PIECES] + [spec(O_FV, 512)]


def _full_specs(shapes):
    return [pl.BlockSpec(s, lambda i: (0, 0)) for s in shapes]


def _prep_fwd(proj, prm, consts, cos, sin, name):
    lp = proj.shape[0]
    tb = TILE_ROW
    row = lambda w: pl.BlockSpec((tb, w), lambda i: (i, 0))

    def body(*refs):
        pieces = [r[...] for r in refs[0:8]]
        fv = refs[8][...]
        prm_v = [r[...] for r in refs[9:21]]
        consts_v = [r[...] for r in refs[21:29]]
        cos_v, sin_v = refs[29][...], refs[30][...]
        outs = refs[31:]
        fq_n, fk_n, ls, mq, mk, mv, sq_n, skd, svd = _prep_math(pieces, prm_v, consts_v, cos_v, sin_v)
        for ref, val in zip(outs, (fq_n, fk_n, fv, mq, mk, mv, sq_n, skd, svd)):
            ref[...] = val.astype(BF16)
        outs[9][...] = ls

    widths = [512, 512, 512, 1024, 1024, 512, 512, 256, 256]
    return pl.pallas_call(
        body, name=name, grid=(lp // tb,),
        in_specs=_piece_specs(tb) + _full_specs(_PRM_SHAPES) + _full_specs(_CONST_SHAPES) + [row(128), row(128)],
        out_specs=[row(w) for w in widths] + [row(128)],
        out_shape=[SDS((lp, w), BF16) for w in widths] + [SDS((lp, 128), F32)],
        compiler_params=_params(("parallel",)),
    )(*([proj] * 9), *prm, *consts, cos, sin)


def _prep_bwd(proj, prm, consts, cos, sin, cots, folds, name):
    lp = proj.shape[0]
    tb = TILE_ROW
    row = lambda w: pl.BlockSpec((tb, w), lambda i: (i, 0))
    fold64, fold128 = folds

    def body(*refs):
        pieces = [r[...] for r in refs[0:8]]
        prm_v = [r[...] for r in refs[9:21]]
        consts_v = [r[...] for r in refs[21:29]]
        cos_v, sin_v = refs[29][...], refs[30][...]
        dfq, dfk, dfv, dmq, dmk, dmv, dsq, dskd, dsvd, dls = [r[...] for r in refs[31:41]]
        f64, f128 = refs[41][...], refs[42][...]
        d_ref = refs[43]
        g_refs = refs[44:]

        @pl.when(pl.program_id(0) == 0)
        def _():
            for r in g_refs:
                r[...] = jnp.zeros_like(r)

        f = lambda pc, pr: _prep_math(pc, pr, consts_v, cos_v, sin_v)
        _, vjp = jax.vjp(f, pieces, prm_v)
        dpc, dprm = vjp((dfq, dfk, dls, dmq, dmk, dmv, dsq, dskd, dsvd))
        d_fq, d_fk, d_sq, d_sk, d_sv, d_cq, d_ckv, d_misc = dpc
        for off, val in ((O_FQ, d_fq), (O_FK, d_fk), (O_FV, dfv), (O_SQ, d_sq), (O_SK, d_sk), (O_SV, d_sv),
                         (O_CQ, d_cq), (O_CKV, d_ckv), (O_MISC, d_misc)):
            d_ref[:, off:off + val.shape[1]] = val.astype(BF16)
        folded = {0: f64, 1: f64, 2: f64, 3: f64[0:128], 7: f128, 8: f128}
        for idx, (ref, val) in enumerate(zip(g_refs, dprm)):
            if idx in folded:
                ref[...] += _split_dot(jnp.broadcast_to(val, (8, val.shape[1])), folded[idx], 3)
            elif val.shape[0] == 1:
                ref[...] += jnp.broadcast_to(val, ref.shape)
            else:
                ref[...] += val

    g_shapes = [(8, 128), (8, 128), (8, 128), (8, 128), (8, 128), (8, 256), (8, 128), (8, 128), (8, 128),
                (256, 1024), (128, 1024), (128, 512)]
    cot_widths = [512, 512, 512, 1024, 1024, 512, 512, 256, 256, 128]
    return pl.pallas_call(
        body, name=name, grid=(lp // tb,),
        in_specs=(_piece_specs(tb) + _full_specs(_PRM_SHAPES) + _full_specs(_CONST_SHAPES) + [row(128), row(128)]
                  + [row(w) for w in cot_widths] + _full_specs([(512, 128), (1024, 128)])),
        out_specs=[row(OTHER_W)] + _full_specs(g_shapes),
        out_shape=[SDS((lp, OTHER_W), BF16)] + [SDS(s, F32) for s in g_shapes],
        compiler_params=_params(("arbitrary",)),
    )(*([proj] * 9), *prm, *consts, cos, sin, *cots, fold64, fold128)


def _cumsum(xs, reverse, name):
    lp = xs[0].shape[0]
    tb = TILE_MM
    nb = lp // tb
    n_in = len(xs)
    idx = (lambda i: (nb - 1 - i, 0)) if reverse else (lambda i: (i, 0))

    def body(*refs):
        o_ref, carry = refs[n_in], refs[n_in + 1]

        @pl.when(pl.program_id(0) == 0)
        def _():
            carry[...] = jnp.zeros_like(carry)

        x = refs[0][...]
        for r in refs[1:n_in]:
            x = x + r[...]
        r_i = lax.broadcasted_iota(jnp.int32, (tb, tb), 0)
        c_i = lax.broadcasted_iota(jnp.int32, (tb, tb), 1)
        tri = ((c_i >= r_i) if reverse else (c_i <= r_i)).astype(BF16)
        acc, rest = None, x
        for _ in range(3):
            part = rest.astype(BF16)
            rest = rest - part.astype(F32)
            acc = _dot(tri, part) if acc is None else acc + _dot(tri, part)
        o_ref[...] = acc + carry[...]
        carry[...] += jnp.sum(x, axis=0, keepdims=True)

    return pl.pallas_call(
        body, name=name, grid=(nb,),
        in_specs=[pl.BlockSpec((tb, 128), idx)] * n_in,
        out_specs=pl.BlockSpec((tb, 128), idx),
        out_shape=SDS((lp, 128), F32),
        scratch_shapes=[pltpu.VMEM((1, 128), F32)],
        compiler_params=_params(("arbitrary",)),
    )(*xs)


class _Att:
    def __init__(self, mode):
        self.mode = mode
        self.wide = mode == "mla"
        self.qw = 256 if self.wide else 128
        self.scale = (96 if mode == "mla" else 64) ** -0.5
        self.pairs = 2 * PAIRS if mode == "swa" else PAIRS

    def resident(self, x, lo, scaled):
        if self.wide:
            return x[:, 0:128], x[:, 128:256]
        if scaled:
            x = x * jnp.asarray(self.scale, x.dtype)
        zero = jnp.zeros_like(x)
        return jnp.where(lo, x, zero), jnp.where(lo, zero, x)

    def moving(self, x):
        return (x[:, 0:128], x[:, 128:256]) if self.wide else (x, x)

    def logits(self, a, b, qpos, kpos, key_decay, slope, masked):
        s = _dot_nt(a, b)
        if self.wide:
            s = s * self.scale
        if self.mode == "fox":
            s = s - key_decay
        if self.mode == "swa":
            s = s - slope * (qpos - kpos).astype(F32)
        if masked:
            ok = kpos <= qpos
            if self.mode == "swa":
                ok = ok & ((kpos < N_META) | (qpos - kpos < WINDOW))
            s = jnp.where(ok, s, NEG)
        return s


def _as_rows(col):
    return jnp.broadcast_to(col, (col.shape[0], 128)).T[0:8, :]


def _halves(x, lo):
    zero = jnp.zeros_like(x)
    return jnp.where(lo, x, zero), jnp.where(lo, zero, x)


def _kv_specs(att, lp, rows):
    row = lambda g, i: (i if rows != lp else 0, g)
    if att.mode == "swa":
        return (pl.BlockSpec((rows, 64 * att.pairs), row),) * 2
    return pl.BlockSpec((rows, att.pairs * att.qw), row), pl.BlockSpec((rows, att.pairs * 128), row)


def _pair_cols(att, x, pp, width):
    return x if x.shape[1] == width else x[:, pp * width:(pp + 1) * width]


def _kv_cols(att, x, pp, width):
    return _pair_cols(att, x, pp // 2 if att.mode == "swa" else pp, width)


def _att_fwd(att, q, k, v, extra, name):
    lp = q.shape[0]
    t = TILE_ATT
    nq = lp // t
    qw = att.qw
    mode = att.mode
    pairs = att.pairs
    nh = 2 * pairs

    def body(*refs):
        q_ref, k_ref, v_ref = refs[0:3]
        o_ref, lse_ref = refs[-2:]
        g, qi = pl.program_id(0), pl.program_id(1)
        lo = lax.broadcasted_iota(jnp.int32, (1, 128), 1) < 64
        q_all = q_ref[...]
        q_heads = [h for pp in range(pairs) for h in att.resident(_pair_cols(att, q_all, pp, qw), lo, True)]
        qpos = qi * t + lax.broadcasted_iota(jnp.int32, (t, 1), 0)

        def step(first, cols, carry, masked):
            ks = pl.multiple_of(first, 128)
            kc, vc = k_ref[pl.ds(ks, cols), :], v_ref[pl.ds(ks, cols), :]
            kpos = first + lax.broadcasted_iota(jnp.int32, (1, cols), 1)
            out = []
            for h in range(nh):
                pp = h // 2
                m, l, acc = carry[3 * h:3 * h + 3]
                k_h = att.moving(_kv_cols(att, kc, pp, qw))[h % 2]
                decay = refs[3][h, :, pl.ds(ks, cols)] if mode == "fox" else None
                slope = refs[4][nh * g + h] if mode == "swa" else None
                s = att.logits(q_heads[h], k_h, qpos, kpos, decay, slope, masked)
                m_new = jnp.maximum(m, jnp.max(s, axis=-1, keepdims=True))
                alpha = jnp.exp(m - m_new)
                pe = jnp.exp(s - m_new)
                l = alpha * l + jnp.sum(pe, axis=-1, keepdims=True)
                acc = alpha * acc + _dot(pe.astype(BF16), _kv_cols(att, vc, pp, 128))
                out += [m_new, l, acc]
            return tuple(out)

        init = []
        for h in range(nh):
            if mode == "swa":
                init += [jnp.full((t, 1), refs[3][nh * g + h], F32), jnp.ones((t, 1), F32)]
            else:
                init += [jnp.full((t, 1), NEG, F32), jnp.zeros((t, 1), F32)]
            init.append(jnp.zeros((t, 128), F32))
        if mode == "swa":
            band = jnp.maximum(qi * t - WINDOW, 0)
            carry = lax.fori_loop(0, (band >= 128).astype(jnp.int32), lambda j, c: step(0, 128, c, True), tuple(init))
            carry = step(band, t + WINDOW, carry, True)
        else:
            carry = lax.fori_loop(0, qi // 2, lambda j, c: step(2 * j * t, 2 * t, c, False), tuple(init))
            carry = lax.fori_loop(0, qi % 2, lambda j, c: step((qi - 1) * t, t, c, False), carry)
            carry = step(qi * t, t, carry, True)
        outs = []
        for pp in range(pairs):
            (ma, la, acca), (mb, lb, accb) = carry[6 * pp:6 * pp + 3], carry[6 * pp + 3:6 * pp + 6]
            outs.append(jnp.where(lo, acca / la, accb / lb).astype(BF16))
            lse_ref[2 * pp] = ma + jnp.log(la)
            lse_ref[2 * pp + 1] = mb + jnp.log(lb)
        o_ref[...] = jnp.concatenate(outs, axis=1)

    in_specs = [pl.BlockSpec((t, pairs * qw), lambda g, i: (i, g)), *_kv_specs(att, lp, lp)]
    if mode == "fox":
        in_specs += [pl.BlockSpec((nh, 1, lp), lambda g, i: (g, 0, 0))]
    if mode == "swa":
        in_specs += [pl.BlockSpec(memory_space=pltpu.SMEM)] * 2
    return pl.pallas_call(
        body, name=name, grid=(4 // pairs, nq), in_specs=in_specs,
        out_specs=[pl.BlockSpec((t, pairs * 128), lambda g, i: (i, g)), pl.BlockSpec((nh, t, 1), lambda g, i: (g, i, 0))],
        out_shape=[SDS((lp, 512), BF16), SDS((HEADS, lp, 1), F32)],
        compiler_params=_params(("parallel", "arbitrary")),
    )(q, k, v, *extra)


def _att_bwd(att, q, k, v, o, do, lse, extra, name):
    lp = q.shape[0]
    t = TILE_ATT
    nq = lp // t
    qw = att.qw
    mode = att.mode
    pairs = att.pairs
    nh = 2 * pairs
    kw = 64 * pairs if mode == "swa" else pairs * qw
    vw = 64 * pairs if mode == "swa" else pairs * 128

    def body(*refs):
        q_ref, k_ref, v_ref, o_ref, do_ref, lse_ref = refs[0:6]
        n_out = {"fox": 5, "mla": 3, "swa": 4}[mode]
        outs = refs[len(refs) - n_out:]
        dq_ref, dk_ref, dv_ref = outs[0:3]
        g, qi = pl.program_id(0), pl.program_id(1)

        @pl.when(qi == 0)
        def _():
            dk_ref[...] = jnp.zeros_like(dk_ref)
            dv_ref[...] = jnp.zeros_like(dv_ref)
            if mode == "fox":
                outs[4][...] = jnp.zeros_like(outs[4])

        lo = lax.broadcasted_iota(jnp.int32, (1, 128), 1) < 64
        q_all, do_all = q_ref[...], do_ref[...]
        prod = do_all.astype(F32) * o_ref[...].astype(F32)
        q_heads, q_plain, do_heads, do_pairs, delta = [], [], [], [], []
        for pp in range(pairs):
            q_pp = _pair_cols(att, q_all, pp, qw)
            q_heads += att.resident(q_pp, lo, True)
            q_plain += att.moving(q_pp)
            do_pp = _pair_cols(att, do_all, pp, 128)
            do_pairs.append(do_pp)
            do_heads += _halves(do_pp, lo)
            pr_pp = _pair_cols(att, prod, pp, 128)
            delta += [jnp.sum(jnp.where(lo, pr_pp, 0.0), axis=-1, keepdims=True),
                      jnp.sum(jnp.where(lo, 0.0, pr_pp), axis=-1, keepdims=True)]
        lse_v = [lse_ref[h] for h in range(nh)]
        qpos = qi * t + lax.broadcasted_iota(jnp.int32, (t, 1), 0)

        def step(first, cols, carry, masked):
            ks = pl.multiple_of(first, 128)
            kc, vc = k_ref[pl.ds(ks, cols), :], v_ref[pl.ds(ks, cols), :]
            kpos = first + lax.broadcasted_iota(jnp.int32, (1, cols), 1)
            out, dk_parts, dv_parts = [], [], []
            for h in range(nh):
                pp = h // 2
                k_h = att.moving(_kv_cols(att, kc, pp, qw))[h % 2]
                decay = refs[6][h, :, pl.ds(ks, cols)] if mode == "fox" else None
                slope = refs[7][nh * g + h] if mode == "swa" else None
                s = att.logits(q_heads[h], k_h, qpos, kpos, decay, slope, masked)
                pr = jnp.exp(s - lse_v[h])
                ds = pr * (_dot_nt(do_heads[h], _kv_cols(att, vc, pp, 128)) - delta[h])
                dsb = ds.astype(BF16)
                out.append(carry[2 * h] + _dot(dsb, k_h))
                out.append(carry[2 * h + 1] + jnp.sum(ds, axis=-1, keepdims=True) if mode == "fox" else carry[2 * h + 1])
                dk_parts.append(_dot_tn(dsb, q_plain[h]))
                dv_parts.append(_dot_tn(pr.astype(BF16), do_pairs[pp]))
                if mode == "fox":
                    outs[4][h, 0:1, pl.ds(ks, cols)] -= jnp.sum(ds, axis=0, keepdims=True)
            rows = pl.ds(ks, cols)
            for pp in range(pairs):
                dv_pp = jnp.where(lo, dv_parts[2 * pp], dv_parts[2 * pp + 1])
                if att.wide:
                    dk_pp = jnp.concatenate(dk_parts[2 * pp:2 * pp + 2], axis=1) * att.scale
                else:
                    dk_pp = jnp.where(lo, dk_parts[2 * pp], dk_parts[2 * pp + 1]) * att.scale
                if mode == "swa":
                    dk_ref[rows, (pp // 2) * 128:(pp // 2 + 1) * 128] += dk_pp
                    dv_ref[rows, (pp // 2) * 128:(pp // 2 + 1) * 128] += dv_pp
                else:
                    dk_ref[rows, pp * qw:(pp + 1) * qw] += dk_pp
                    dv_ref[rows, pp * 128:(pp + 1) * 128] += dv_pp
            return tuple(out)

        init = (jnp.zeros((t, 128), F32), jnp.zeros((t, 1), F32)) * nh
        if mode == "swa":
            band = jnp.maximum(qi * t - WINDOW, 0)
            carry = lax.fori_loop(0, (band >= 128).astype(jnp.int32), lambda j, c: step(0, 128, c, True), init)
            carry = step(band, t + WINDOW, carry, True)
        else:
            carry = lax.fori_loop(0, qi // 2, lambda j, c: step(2 * j * t, 2 * t, c, False), init)
            carry = lax.fori_loop(0, qi % 2, lambda j, c: step((qi - 1) * t, t, c, False), carry)
            carry = step(qi * t, t, carry, True)
        dq = []
        for pp in range(pairs):
            dqa, dca, dqb, dcb = carry[4 * pp:4 * pp + 4]
            dq += [dqa, dqb] if att.wide else [jnp.where(lo, dqa, dqb)]
            if mode == "fox":
                outs[3][2 * pp] = _as_rows(dca)
                outs[3][2 * pp + 1] = _as_rows(dcb)
        dq_ref[...] = jnp.concatenate(dq, axis=1) * att.scale
        if mode == "swa":
            ds_ref = outs[3]

            @pl.when(qi == 0)
            def _():
                ds_ref[...] = jnp.zeros_like(ds_ref)

            lane = lax.broadcasted_iota(jnp.int32, (8, 128), 1)
            acc = jnp.zeros((8, 128), F32)
            for h in range(nh):
                tot = -jnp.sum(jnp.exp(refs[6][nh * g + h] - lse_v[h]) * delta[h])
                acc = acc + jnp.where(lane == h, tot, 0.0)
            ds_ref[0] += acc

    col = pl.BlockSpec((nh, t, 1), lambda g, i: (g, i, 0))
    in_specs = [pl.BlockSpec((t, pairs * qw), lambda g, i: (i, g)), *_kv_specs(att, lp, lp),
                pl.BlockSpec((t, pairs * 128), lambda g, i: (i, g)), pl.BlockSpec((t, pairs * 128), lambda g, i: (i, g)), col]
    out_specs = [pl.BlockSpec((t, pairs * qw), lambda g, i: (i, g)), pl.BlockSpec((lp, kw), lambda g, i: (0, g)),
                 pl.BlockSpec((lp, vw), lambda g, i: (0, g))]
    n_groups = 4 // pairs
    out_shape = [SDS((lp, 4 * qw), F32), SDS((lp, n_groups * kw), F32), SDS((lp, n_groups * vw), F32)]
    if mode == "fox":
        in_specs += [pl.BlockSpec((nh, 1, lp), lambda g, i: (g, 0, 0))]
        out_specs += [pl.BlockSpec((nh, 8, t), lambda g, i: (g, 0, i)), pl.BlockSpec((nh, 8, lp), lambda g, i: (g, 0, 0))]
        out_shape += [SDS((HEADS, 8, lp), F32)] * 2
    if mode == "swa":
        in_specs += [pl.BlockSpec(memory_space=pltpu.SMEM)] * 2
        out_specs.append(pl.BlockSpec((1, 8, 128), lambda g, i: (g, 0, 0)))
        out_shape.append(SDS((n_groups, 8, 128), F32))
    return pl.pallas_call(
        body, name=name, grid=(n_groups, nq), in_specs=in_specs, out_specs=out_specs, out_shape=out_shape,
        compiler_params=_params(("parallel", "arbitrary")),
    )(q, k, v, o, do, lse, *extra)


def _post_fwd(h, proj, outs, wb, wo, name):
    lp, d = h.shape
    tb = TILE_POST
    row = lambda w: pl.BlockSpec((tb, w), lambda i: (i, 0))

    def body(h_ref, g0, g1, g2, oa, ob, oc, wb_ref, wo_ref, o_ref):
        merged = jnp.zeros((tb, d), F32)
        for n, (g_ref, br) in enumerate(((g0, oa), (g1, ob), (g2, oc))):
            merged = merged + jax.nn.sigmoid(g_ref[...]) * _dot(br[...], wb_ref[n])
        o_ref[...] = h_ref[...] + _dot(merged.astype(BF16), wo_ref[...])

    gate = lambda n: pl.BlockSpec((tb, d), lambda i, n=n: (i, n))
    return pl.pallas_call(
        body, name=name, grid=(lp // tb,),
        in_specs=[row(d), gate(0), gate(1), gate(2), row(512), row(512), row(512),
                  pl.BlockSpec((3, 512, d), lambda i: (0, 0, 0)), pl.BlockSpec((d, d), lambda i: (0, 0))],
        out_specs=row(d), out_shape=SDS((lp, d), F32),
        compiler_params=_params(("parallel",)),
    )(h, proj, proj, proj, *outs, wb, wo)


def _post_bwd(dh, proj, outs, wb, wo, name):
    lp, d = dh.shape
    tb = TILE_POST
    row = lambda w: pl.BlockSpec((tb, w), lambda i: (i, 0))

    def body(dh_ref, g0, g1, g2, oa, ob, oc, wb_ref, wo_ref, dg_ref, doa, dob, doc, dwb_ref, dwo_ref):
        @pl.when(pl.program_id(0) == 0)
        def _():
            dwb_ref[...] = jnp.zeros_like(dwb_ref)
            dwo_ref[...] = jnp.zeros_like(dwo_ref)

        dhb = dh_ref[...].astype(BF16)
        dm = _dot_nt(dhb, wo_ref[...])
        merged = jnp.zeros((tb, d), F32)
        for n, (g_ref, br, do_ref) in enumerate(((g0, oa, doa), (g1, ob, dob), (g2, oc, doc))):
            gate = jax.nn.sigmoid(g_ref[...])
            o_n = br[...]
            y = _dot(o_n, wb_ref[n])
            merged = merged + gate * y
            dy = (dm * gate).astype(BF16)
            dg_ref[:, n * d:(n + 1) * d] = (dm * y * gate * (1.0 - gate)).astype(BF16)
            do_ref[...] = _dot_nt(dy, wb_ref[n]).astype(BF16)
            dwb_ref[n] += _dot_tn(o_n, dy)
        dwo_ref[...] += _dot_tn(merged.astype(BF16), dhb)

    gate = lambda n: pl.BlockSpec((tb, d), lambda i, n=n: (i, n))
    wb_spec = pl.BlockSpec((3, 512, d), lambda i: (0, 0, 0))
    wo_spec = pl.BlockSpec((d, d), lambda i: (0, 0))
    return pl.pallas_call(
        body, name=name, grid=(lp // tb,),
        in_specs=[row(d), gate(0), gate(1), gate(2), row(512), row(512), row(512), wb_spec, wo_spec],
        out_specs=[row(GATES_W), row(512), row(512), row(512), wb_spec, wo_spec],
        out_shape=[SDS((lp, GATES_W), BF16)] + [SDS((lp, 512), BF16)] * 3 + [SDS((3, 512, d), F32), SDS((d, d), F32)],
        compiler_params=_params(("arbitrary",)),
    )(dh, proj, proj, proj, *outs, wb, wo)


def _shift_down(x, halo, n, first):
    rows = lax.broadcasted_iota(jnp.int32, x.shape, 0)
    halo = jnp.where(first, 0.0, halo)
    edge = jnp.concatenate([pltpu.roll(halo, n, 0), jnp.zeros((x.shape[0] - 8, x.shape[1]), F32)], axis=0)
    return jnp.where(rows < n, edge, pltpu.roll(x, n, 0))


def _shift_up(x, halo, n, last):
    tb = x.shape[0]
    rows = lax.broadcasted_iota(jnp.int32, x.shape, 0)
    halo = jnp.where(last, 0.0, halo)
    edge = jnp.concatenate([jnp.zeros((tb - 8, x.shape[1]), F32), pltpu.roll(halo, 8 - n, 0)], axis=0)
    return jnp.where(rows >= tb - n, edge, pltpu.roll(x, tb - n, 0))


def _conv(u, halo, w_ref, b_ref, first):
    taps = (_shift_down(u, halo, 2, first), _shift_down(u, halo, 1, first), u)
    c = b_ref[...] + w_ref[0:1, :] * taps[0] + w_ref[1:2, :] * taps[1] + w_ref[2:3, :] * taps[2]
    return c, taps


def _ffn_specs(tb, f):
    hb = tb // 8
    cur = lambda c: pl.BlockSpec((tb, f), lambda i, c=c: (i, c))
    prev = lambda c: pl.BlockSpec((8, f), lambda i, c=c: (jnp.maximum(i * hb - 1, 0), c))
    vec = lambda r, c: pl.BlockSpec((r, f), lambda i, c=c: (0, c))
    return cur, prev, vec


def _ffn_act_fwd(u, cw, cb, name):
    lp = u.shape[0]
    f = D_FF
    tb = TILE_ROW
    cur, prev, vec = _ffn_specs(tb, f)

    def body(ug, uv, hg, hv, wg, wv, bg, bv, o_ref):
        first = pl.program_id(0) == 0
        cg, _ = _conv(ug[...], hg[...], wg, bg, first)
        cv, _ = _conv(uv[...], hv[...], wv, bv, first)
        o_ref[...] = (cg * jax.nn.sigmoid(cg) * cv).astype(BF16)

    return pl.pallas_call(
        body, name=name, grid=(lp // tb,),
        in_specs=[cur(0), cur(1), prev(0), prev(1), vec(8, 0), vec(8, 1), vec(1, 0), vec(1, 1)],
        out_specs=pl.BlockSpec((tb, f), lambda i: (i, 0)), out_shape=SDS((lp, f), BF16),
        compiler_params=_params(("parallel",)),
    )(u, u, u, u, cw, cw, cb, cb)


def _ffn_act_bwd_conv(u, dact, cw, cb, name):
    lp = u.shape[0]
    f = D_FF
    tb = TILE_CONV_BWD
    cur, prev, vec = _ffn_specs(tb, f)

    def body(ug, uv, hg, hv, wg, wv, bg, bv, da_ref, dcg_ref, dcv_ref, dwg, dwv, dbg, dbv):
        first = pl.program_id(0) == 0

        @pl.when(first)
        def _():
            for r in (dwg, dwv, dbg, dbv):
                r[...] = jnp.zeros_like(r)

        cg, tg = _conv(ug[...], hg[...], wg, bg, first)
        cv, tv = _conv(uv[...], hv[...], wv, bv, first)
        da = da_ref[...]
        sg = jax.nn.sigmoid(cg)
        dcg = da * cv * sg * (1.0 + cg * (1.0 - sg))
        dcv = da * cg * sg
        dcg_ref[...] = dcg
        dcv_ref[...] = dcv
        for dc, taps, dw, db in ((dcg, tg, dwg, dbg), (dcv, tv, dwv, dbv)):
            for n in range(3):
                dw[n:n + 1, :] += jnp.sum(dc * taps[n], axis=0, keepdims=True)
            db[0:1, :] += jnp.sum(dc, axis=0, keepdims=True)

    row = pl.BlockSpec((tb, f), lambda i: (i, 0))
    acc = pl.BlockSpec((8, f), lambda i: (0, 0))
    return pl.pallas_call(
        body, name=name, grid=(lp // tb,),
        in_specs=[cur(0), cur(1), prev(0), prev(1), vec(8, 0), vec(8, 1), vec(1, 0), vec(1, 1), row],
        out_specs=[row, row, acc, acc, acc, acc],
        out_shape=[SDS((lp, f), F32)] * 2 + [SDS((8, f), F32)] * 4,
        compiler_params=_params(("arbitrary",)),
    )(u, u, u, u, cw, cw, cb, cb, dact)


def _ffn_act_bwd_in(dcg, dcv, cw, name):
    lp = dcg.shape[0]
    f = D_FF
    tb = TILE_ROW
    nb = lp // tb
    hb = tb // 8
    cur = pl.BlockSpec((tb, f), lambda i: (i, 0))
    nxt = pl.BlockSpec((8, f), lambda i: (jnp.minimum((i + 1) * hb, nb * hb - 1), 0))
    vec = lambda c: pl.BlockSpec((8, f), lambda i, c=c: (0, c))

    def body(dg, dv, ng, nv, wg, wv, o_ref):
        last = pl.program_id(0) == nb - 1
        for c, (dc_ref, n_ref, w_ref) in enumerate(((dg, ng, wg), (dv, nv, wv))):
            dc, halo = dc_ref[...], n_ref[...]
            du = (w_ref[2:3, :] * dc + w_ref[1:2, :] * _shift_up(dc, halo, 1, last)
                  + w_ref[0:1, :] * _shift_up(dc, halo, 2, last))
            o_ref[:, c * f:(c + 1) * f] = du.astype(BF16)

    return pl.pallas_call(
        body, name=name, grid=(nb,),
        in_specs=[cur, cur, nxt, nxt, vec(0), vec(1)],
        out_specs=pl.BlockSpec((tb, 2 * f), lambda i: (i, 0)), out_shape=SDS((lp, 2 * f), BF16),
        compiler_params=_params(("parallel",)),
    )(dcg, dcv, dcg, dcv, cw, cw)


def _loss_head(y, target, n_real, name):
    lp, d = y.shape
    tb = TILE_MM

    def body(y_ref, t_ref, dy_ref, loss_ref):
        i = pl.program_id(0)

        @pl.when(i == 0)
        def _():
            loss_ref[...] = jnp.zeros_like(loss_ref)

        rows = i * tb + lax.broadcasted_iota(jnp.int32, (tb, 1), 0)
        real = (rows >= N_META) & (rows < N_META + n_real)
        diff = jnp.where(real, y_ref[...] - t_ref[...], 0.0)
        dy_ref[...] = diff * (1.0 / d)
        loss_ref[...] += (0.5 / d) * jnp.sum(diff * diff).reshape(1, 1)

    row = pl.BlockSpec((tb, d), lambda i: (i, 0))
    return pl.pallas_call(
        body, name=name, grid=(lp // tb,), in_specs=[row, row],
        out_specs=[row, pl.BlockSpec((1, 1), lambda i: (0, 0))],
        out_shape=[SDS((lp, d), F32), SDS((1, 1), F32)],
        compiler_params=_params(("arbitrary",)),
    )(y, target)


def _pad_lanes(v, width, at=0):
    return jnp.pad(v.astype(F32), (at, width - at - v.shape[0]))[None, :]


_IN_COLS = dict(fq=(0, 512), fk=(512, 512), fv=(1024, 512), ff=(1536, 8), cq=(1544, 256), ckv=(1800, 128),
                kr=(1928, 32), sq=(1960, 512), sk=(2472, 128), sv=(2600, 128), gates=(2728, 3072))


def _orig_cols(src, start, width):
    if src.ndim == 2:
        return [src[:, start:start + width]]
    per, out, pos = src.shape[2], [], start
    while pos < start + width:
        d, off = divmod(pos, per)
        take = min(per - off, start + width - pos)
        out.append(src[d, :, off:off + take])
        pos += take
    return out


class _ColumnSegments:
    def __init__(self, segments):
        self.segments = segments

    def full(self):
        return jnp.concatenate([a[:, s:s + w] for a, s, w in self.segments], axis=1)

    def blocks(self, n):
        per = sum(w for _, _, w in self.segments) // n
        out, seg, used = [], 0, 0
        for _ in range(n):
            pieces, need = [], per
            while need:
                a, s, w = self.segments[seg]
                take = min(w - used, need)
                pieces.append(a[:, s + used:s + used + take])
                used, need = used + take, need - take
                if used == w:
                    seg, used = seg + 1, 0
            out.append(jnp.concatenate(pieces, axis=1))
        return jnp.stack(out)


def _mix_params(w, big, l):
    b = lambda a: a.astype(BF16)
    win = big["w_in"]
    order = ("gates", "fq", "fk", "fv", "sq", "sk", "sv", "cq", "ckv", "kr", "ff")
    pieces = [p for name in order for p in _orig_cols(win, *_IN_COLS[name])]
    w_in = b(jnp.concatenate(pieces + [jnp.zeros((D_MODEL, 88), win.dtype)], axis=1))
    wq = jnp.pad(big["mla_w_q_up"].reshape(256, HEADS, 96), ((0, 0), (0, 0), (0, 32))).reshape(256, 1024)
    wkv = big["mla_w_kv_up"].reshape(128, HEADS, 128)
    wkk = jnp.pad(wkv[:, :, :64], ((0, 0), (0, 0), (0, 64))).reshape(128, 1024)
    wkvv = wkv[:, :, 64:].reshape(128, 512)
    tile = lambda g, n: jnp.tile(g.astype(F32), n)[None, :]
    prm = [tile(w["fox_q_g"][l], 8), tile(w["fox_k_g"][l], 8), tile(w["swa_q_g"][l], 8), tile(w["swa_k_g"][l], 2),
           _pad_lanes(w["fox_forget_b"][l], 128, FF_LANE), w["mla_q_a_g"][l][None, :], w["mla_kv_a_g"][l][None, :],
           tile(jnp.pad(w["mla_q_g"][l], (0, 32)), 8), tile(jnp.pad(w["mla_k_g"][l], (0, 32)), 8),
           wq.astype(F32), wkk.astype(F32), wkvv.astype(F32)]
    return dict(g1=w["norm1_g"][l][None, :], w_in=w_in, prm=prm, sinks=w["swa_sinks"][l].astype(F32),
                wb=b(big["w_branch"]), wo=b(big["w_o"]))


def _ffn_params(w, big, l):
    cw = jnp.pad(w["ffn_conv_w"][l].astype(F32), ((0, 5), (0, 0)))
    return dict(g2=w["norm2_g"][l][None, :], w_up=big["ffn_w_up"].astype(BF16), cw=cw,
                cb=w["ffn_conv_b"][l][None, :].astype(F32), w_down=big["ffn_w_down"].astype(BF16))


def _decay_rows(c):
    return c[:, FF_LANE:FF_LANE + HEADS].T[:, None, :]


def _from_rows(row):
    return jnp.pad(row[:, 0, :].T, ((0, 0), (FF_LANE, 128 - FF_LANE - HEADS)))


def _layer_fwd_mix(h, lw, consts, cos, sin, slopes, l):
    tag = f"l{l}_"
    xn, proj = _norm_matmul(h, lw["g1"], lw["w_in"], IN_W, tag + "in_proj")
    fq, fk, fv, mq, mk, mv, sq, skd, svd, ls = _prep_fwd(proj, lw["prm"], consts, cos, sin, tag + "prep")
    c = _cumsum([ls], False, tag + "decay_cumsum")
    c_row = _decay_rows(c)
    oa, lse_a = _att_fwd(_Att("fox"), fq, fk, fv, (c_row,), tag + "fox_fwd")
    ob, lse_b = _att_fwd(_Att("mla"), mq, mk, mv, (), tag + "mla_fwd")
    oc, lse_c = _att_fwd(_Att("swa"), sq, skd, svd, (lw["sinks"], slopes), tag + "swa_fwd")
    h2 = _post_fwd(h, proj, (oa, ob, oc), lw["wb"], lw["wo"], tag + "merge")
    saved = dict(h=h, xn=xn, proj=proj, q=(fq, mq, sq), k=(fk, mk, skd), v=(fv, mv, svd), c=c_row,
                 o=(oa, ob, oc), lse=(lse_a, lse_b, lse_c), h2=h2)
    return h2, saved


def _layer_fwd_ffn(h2, lw, l):
    tag = f"l{l}_"
    xn2, u = _norm_matmul(h2, lw["g2"], lw["w_up"], 2 * D_FF, tag + "ffn_up")
    act = _ffn_act_fwd(u, lw["cw"], lw["cb"], tag + "ffn_act")
    h3 = _matmul_residual(act, lw["w_down"], h2, tag + "ffn_down")
    return h3, dict(xn2=xn2, u=u, act=act)


def _layer_bwd_ffn(dh3, lw, sv, l):
    tag = f"l{l}_"
    f = D_FF
    dact = _matmul_nt(dh3, lw["w_down"], f, tag + "ffn_down_dx")
    dw_down = _matmul_tn(sv["act"], dh3, D_MODEL, tag + "ffn_down_dw")
    dcg, dcv, dwg, dwv, dbg, dbv = _ffn_act_bwd_conv(sv["u"], dact, lw["cw"], lw["cb"], tag + "ffn_act_dc")
    du = _ffn_act_bwd_in(dcg, dcv, lw["cw"], tag + "ffn_act_du")
    dw_up = _matmul_tn(sv["xn2"], du, f, tag + "ffn_up_dw")
    dh2, dg2 = _norm_matmul_bwd([du], lw["w_up"], sv["h2"], lw["g2"], dh3, tag + "ffn_up_dx")
    g = dict(norm2_g=dg2[0], ffn_w_up=dw_up, ffn_conv_w=jnp.concatenate([dwg[0:3], dwv[0:3]], axis=1),
             ffn_conv_b=jnp.concatenate([dbg[0], dbv[0]]), ffn_w_down=dw_down)
    return dh2, g


def _layer_bwd_mix(dh2, lw, sv, consts, folds, cos, sin, slopes, l, hook=None, merge_hook=None):
    tag = f"l{l}_"
    dgates, doa, dob, doc, dwb, dwo = _post_bwd(dh2, sv["proj"], sv["o"], lw["wb"], lw["wo"], tag + "merge_bwd")
    c_row = sv["c"]
    tick = merge_hook({"w_branch": dwb, "w_o": dwo}) if merge_hook else None
    if tick is not None:
        c_row = c_row + tick
    extras = ((c_row,), (), (lw["sinks"], slopes))
    grads = []
    for n, (mode, do) in enumerate((("fox", doa), ("mla", dob), ("swa", doc))):
        res = _att_bwd(_Att(mode), sv["q"][n], sv["k"][n], sv["v"][n], sv["o"][n], do, sv["lse"][n], extras[n],
                       tag + mode + "_bwd")
        grads.append((res[0], res[1], res[2], res[3:]))
    (dfq, dfk, dfv, (dcq, dck)), (dmq, dmk, dmv, _), (dsq, dskd, dsvd, (dsink,)) = grads
    dls = _cumsum([_from_rows(dcq), _from_rows(dck)], True, tag + "decay_cumsum_bwd")
    res = _prep_bwd(sv["proj"], lw["prm"], consts, cos, sin,
                    (dfq, dfk, dfv, dmq, dmk, dmv, dsq, dskd, dsvd, dls), folds, tag + "prep_bwd")
    dother, pg = res[0], res[1:]
    dh, dg1 = _norm_matmul_bwd([dgates, dother], lw["w_in"], sv["h"], lw["g1"], dh2, tag + "in_proj_dx")
    dw_g = _matmul_tn(sv["xn"], dgates, GATES_W, tag + "in_proj_dw_gates")
    dw_o = _matmul_tn(sv["xn"], dother, OTHER_W, tag + "in_proj_dw_other")
    d_in = _ColumnSegments([
        (dw_o, O_FQ, 1536), (dw_o, O_MISC + FF_LANE, 8), (dw_o, O_CQ, 256), (dw_o, O_CKV, 128), (dw_o, O_MISC, 32),
        (dw_o, O_SQ, 512), (dw_o, O_SK, 128), (dw_o, O_SV, 128), (dw_g, 0, GATES_W)])
    d_wq = pg[9].reshape(256, HEADS, 128)[:, :, :96].reshape(256, 768)
    d_wkv = jnp.concatenate([pg[10].reshape(128, HEADS, 128)[:, :, :64], pg[11].reshape(128, HEADS, 64)],
                            axis=2).reshape(128, 1024)
    g = dict(
        w_in=d_in, fox_forget_b=pg[4][0, FF_LANE:FF_LANE + 8], fox_q_g=pg[0][0, :64],
        fox_k_g=pg[1][0, :64], mla_q_a_g=pg[5][0], mla_w_q_up=d_wq, mla_kv_a_g=pg[6][0], mla_w_kv_up=d_wkv,
        mla_q_g=pg[7][0, :96], mla_k_g=pg[8][0, :96], swa_q_g=pg[2][0, :64], swa_k_g=pg[3][0, :64],
        swa_sinks=dsink[:, 0, :HEADS // dsink.shape[0]].reshape(HEADS), w_branch=dwb, w_o=dwo)
    tick = hook(g) if hook else None
    g["norm1_g"] = dg1[0]
    return dh, g, tick


_MIX_BIG = ("w_in", "mla_w_q_up", "mla_w_kv_up", "w_branch", "w_o")
_FFN_BIG = ("ffn_w_up", "ffn_w_down")


def _local_step(x, target, w, hook=None, fetch=None):
    if fetch is None:
        fetch = lambda l, stage, after: {n: w[n][l] for n in (_MIX_BIG if stage == "mix" else _FFN_BIG)}
    seq = x.shape[0]
    length = N_META + seq
    lp = -(-length // ROW_ALIGN) * ROW_ALIGN
    pad = lp - length
    h = jnp.concatenate([w["meta_tokens"].astype(F32), x, jnp.zeros((pad, D_MODEL), F32)], axis=0)
    tgt = jnp.pad(target, ((N_META, pad), (0, 0)))
    consts = _consts()
    folds = (_fold_matrix(512, 64), _fold_matrix(1024, 128))
    cos, sin = _rope_tables(lp)
    slopes = jnp.asarray(2.0 ** (-8.0 * np.arange(1, HEADS + 1, dtype=np.float32) / HEADS), F32)
    lws, saved = [], []
    for l in range(DEPTH):
        lw = _mix_params(w, fetch(l, "mix", h), l)
        h, sv = _layer_fwd_mix(h, lw, consts, cos, sin, slopes, l)
        lw.update(_ffn_params(w, fetch(l, "ffn", h), l))
        h, sv_ffn = _layer_fwd_ffn(h, lw, l)
        lws.append(lw)
        saved.append({**sv, **sv_ffn})
    dh, loss = _loss_head(h, tgt, seq, "loss_head")
    grads = [None] * DEPTH
    for l in reversed(range(DEPTH)):
        dh, g_ffn = _layer_bwd_ffn(dh, lws[l], saved[l], l)
        tick = hook(l, "ffn", g_ffn) if hook else None
        if tick is not None:
            lws[l]["sinks"] = lws[l]["sinks"] + tick
        mix_hook = (lambda g, l=l, g_ffn=g_ffn: hook(l, "mix", {**g_ffn, **g})) if hook else None
        merge_hook = (lambda g, l=l: hook(l, "merge", g)) if hook else None
        dh, g_mix, tick = _layer_bwd_mix(dh, lws[l], saved[l], consts, folds, cos, sin, slopes, l, mix_hook, merge_hook)
        grads[l] = {**g_ffn, **g_mix}
        if tick is not None and l > 0:
            lws[l - 1]["cw"] = lws[l - 1]["cw"] + tick
    return loss, dh[N_META:length], dh[:N_META], grads


def _place():
    return lax.axis_index("x"), lax.axis_index("y"), lax.axis_index("c")


def _flip(pos, k):
    x, y, c = pos
    return (1 - x if k & 4 else x, 1 - y if k & 2 else y, 1 - c if k & 1 else c)


def _index(pos):
    return 4 * pos[0] + 2 * pos[1] + pos[2]


def _gather(tensors, name):
    n_t = len(tensors)

    def body(*refs):
        ins, outs = refs[:n_t], refs[n_t:2 * n_t]
        send_sems, recv_sems, local_sems = refs[2 * n_t:]
        x, y, c = _place()
        me, sibling = (x, y, c), (x, y, 1 - c)
        chips = [(1 - x, y), (x, 1 - y), (1 - x, 1 - y)]

        def copy(t, k, block, to, src=None):
            dst = outs[t].at[_index(block)]
            return pltpu.make_async_remote_copy(
                src_ref=dst if src is None else src, dst_ref=dst, send_sem=send_sems.at[t, k],
                recv_sem=recv_sems.at[t, k], device_id=to, device_id_type=pl.DeviceIdType.MESH)

        local, sent = [], []
        for t in range(n_t):
            local.append(pltpu.make_async_copy(ins[t], outs[t].at[_index(me)], local_sems.at[t]))
            local[-1].start()
            sent.append(copy(t, 0, me, sibling, src=ins[t]))
            sent += [copy(t, 1 + j, me, (*chip, c), src=ins[t]) for j, chip in enumerate(chips)]
        for cp in sent:
            cp.start()
        for j, chip in enumerate(chips):
            for t in range(n_t):
                copy(t, 1 + j, (*chip, c), me).wait_recv()
                sent.append(copy(t, 4 + j, (*chip, c), sibling))
                sent[-1].start()
        for t in range(n_t):
            copy(t, 0, sibling, me).wait_recv()
            for j, chip in enumerate(chips):
                copy(t, 4 + j, (*chip, 1 - c), me).wait_recv()
        for cp in sent:
            cp.wait_send()
        for cp in local:
            cp.wait()

    any_spec = pl.BlockSpec(memory_space=pl.ANY)
    return pl.pallas_call(
        body, name=name, in_specs=[any_spec] * n_t, out_specs=[any_spec] * n_t,
        out_shape=[SDS((N_DEV,) + a.shape, a.dtype) for a in tensors],
        scratch_shapes=[pltpu.SemaphoreType.DMA((n_t, N_DEV - 1)), pltpu.SemaphoreType.DMA((n_t, N_DEV - 1)),
                        pltpu.SemaphoreType.DMA((n_t,))],
    )(*tensors)


def _exchange_start(tensors, name, gather=False, after=None):
    n_t = len(tensors)

    def body(*refs):
        ins, lands = refs[:n_t], refs[n_t:2 * n_t]
        send_sem, recv_sem = refs[2 * n_t + 1:2 * n_t + 3]
        token = refs[-1]
        me = _place()
        mine = _index(me)
        for t in range(n_t):
            for k in range(1, N_DEV):
                peer = _flip(me, k)
                pltpu.make_async_remote_copy(
                    src_ref=ins[t] if gather else ins[t].at[_index(peer)], dst_ref=lands[t].at[mine],
                    send_sem=send_sem, recv_sem=recv_sem, device_id=peer, device_id_type=pl.DeviceIdType.MESH).start()
        token[...] = jnp.zeros_like(token)

    hbm = pl.BlockSpec(memory_space=pltpu.HBM)
    sem = pl.BlockSpec(memory_space=pltpu.SEMAPHORE)
    one = pltpu.SemaphoreType.DMA(())
    land_shape = lambda a: ((N_DEV,) + a.shape) if gather else a.shape
    bufs = ([pltpu.HBM(a.shape, a.dtype) for a in tensors] + [pltpu.HBM(land_shape(a), a.dtype) for a in tensors])
    after = jnp.zeros((8, 128), F32) if after is None else after
    outs = pl.pallas_call(
        body, name=name, in_specs=[hbm] * (2 * n_t) + [pl.BlockSpec(memory_space=pl.ANY)],
        out_specs=[sem, sem] + [hbm] * (2 * n_t) + [pl.BlockSpec(memory_space=pltpu.VMEM)],
        out_shape=[one, one] + bufs + [SDS((8, 128), F32)],
        input_output_aliases={i: 2 + i for i in range(2 * n_t)},
        compiler_params=pltpu.CompilerParams(has_side_effects=pltpu.SideEffectType.DATAFLOW_SIDE_EFFECTING),
    )(*[pltpu.with_memory_space_constraint(a, pltpu.HBM) for a in tensors],
      *[pltpu.with_memory_space_constraint(lax.empty(land_shape(a), a.dtype), pltpu.HBM) for a in tensors], after)
    return outs[:-1], outs[-1][0, 0]


def _exchange_wait(state, after, name, gather=False):
    n_t = (len(state) - 2) // 2

    def body(*refs):
        send_sem, recv_sem = refs[0:2]
        ins, lands = refs[2:2 + n_t], refs[2 + n_t:2 + 2 * n_t]
        me = _place()
        for t in range(n_t):
            for k in range(1, N_DEV):
                peer = _flip(me, k)
                copy = pltpu.make_async_remote_copy(
                    src_ref=ins[t] if gather else ins[t].at[_index(peer)], dst_ref=lands[t].at[_index(peer)],
                    send_sem=send_sem, recv_sem=recv_sem, device_id=peer, device_id_type=pl.DeviceIdType.MESH)
                copy.wait_send()
                copy.wait_recv()

    hbm = pl.BlockSpec(memory_space=pltpu.HBM)
    sem = pl.BlockSpec(memory_space=pltpu.SEMAPHORE)
    bufs = [pltpu.HBM(a.shape, a.dtype) for a in state[2:]]
    outs = pl.pallas_call(
        body, name=name, in_specs=[sem, sem] + [hbm] * (2 * n_t) + [pl.BlockSpec(memory_space=pl.ANY)],
        out_specs=[hbm] * (2 * n_t), out_shape=bufs,
        input_output_aliases={2 + i: i for i in range(2 * n_t)},
        compiler_params=pltpu.CompilerParams(has_side_effects=pltpu.SideEffectType.DATAFLOW_SIDE_EFFECTING),
    )(*state, after)
    return outs[n_t:]


def _sum_slots(parts, name):
    n, rows, w = parts.shape
    tb = 8

    def body(p_ref, o_ref):
        acc = p_ref[0].astype(F32)
        for s in range(1, n):
            acc = acc + p_ref[s].astype(F32)
        o_ref[...] = acc

    return pl.pallas_call(
        body, name=name, grid=(rows // tb,),
        in_specs=[pl.BlockSpec((n, tb, w), lambda i: (0, i, 0))], out_specs=pl.BlockSpec((tb, w), lambda i: (i, 0)),
        out_shape=SDS((rows, w), F32), compiler_params=_params(("parallel",)),
    )(parts)


def _adamw(wt, m, v, parts, name, own=None, after=None):
    shape = wt.shape
    parts = parts if isinstance(parts, (list, tuple)) else [parts]
    n, w = parts[0].shape[0], shape[-1]
    rows = math.prod(shape[:-1])
    per = rows // len(parts)
    step = 16 if parts[0].dtype == BF16 else 8
    tb = max([t for t in range(step, 257, step) if per % t == 0] or [per])
    nb = per // tb
    c1 = 1.0 / (1.0 - ADAM_B1 ** ADAM_STEP)
    c2 = 1.0 / (1.0 - ADAM_B2 ** ADAM_STEP)
    state = [a.reshape(rows, w) for a in (wt, m, v)]
    n_in = 4 if own is None else 5
    outs = None
    for l in reversed(range(len(parts))):
        def body(*refs):
            idx_ref = None if own is None else refs[0]
            w_ref, m_ref, v_ref, p_ref = refs[n_in - 4:n_in] if own is None else refs[1:5]
            g_out, d_out, m_out, v_out = refs[-4:]
            g = None
            for s in range(n):
                term = p_ref[s] if own is None else jnp.where(idx_ref[0] == s, refs[5][0], p_ref[s])
                g = term.astype(F32) if g is None else g + term.astype(F32)
            m_new = ADAM_B1 * m_ref[...] + (1.0 - ADAM_B1) * g
            v_new = ADAM_B2 * v_ref[...] + (1.0 - ADAM_B2) * (g * g)
            g_out[...] = g
            m_out[...] = m_new
            v_out[...] = v_new
            d_out[...] = -ADAM_LR * ((m_new * c1) / (jnp.sqrt(v_new * c2) + ADAM_EPS) + ADAM_WD * w_ref[...])

        row = pl.BlockSpec((tb, w), lambda i, *_, l=l: (l * nb + i, 0))
        in_specs = [row, row, row, pl.BlockSpec((n, tb, w), lambda i, *_: (0, i, 0))]
        args = [*state, parts[l].reshape(n, per, w)]
        if own is not None:
            in_specs.append(pl.BlockSpec((1, tb, w), lambda i, idx: (idx[0], i, 0)))
            args.append(own[l].reshape(n, per, w))
        prev = [] if outs is None else list(outs)
        behind = [] if after is None else [after]
        in_specs += [pl.BlockSpec(memory_space=pl.ANY)] * (len(prev) + len(behind))
        n_pre = 0 if own is None else 1
        call = dict(name=f"{name}_{l}", out_shape=[SDS((rows, w), F32)] * 4,
                    input_output_aliases={n_pre + len(args) + k: k for k in range(len(prev))},
                    compiler_params=_params(("parallel",)))
        if own is None:
            outs = pl.pallas_call(body, grid=(nb,), in_specs=in_specs, out_specs=[row] * 4, **call)(*args, *prev, *behind)
        else:
            spec = pltpu.PrefetchScalarGridSpec(num_scalar_prefetch=1, grid=(nb,), in_specs=in_specs, out_specs=[row] * 4)
            idx = jnp.reshape(_index(_place()), (1,)).astype(jnp.int32)
            outs = pl.pallas_call(body, grid_spec=spec, **call)(idx, *args, *prev, *behind)
    return [o.reshape(shape) for o in outs]


_BIG = [("w_in", 2), ("mla_w_q_up", 2), ("mla_w_kv_up", 2), ("w_branch", 3), ("w_o", 1), ("ffn_w_up", 2), ("ffn_w_down", 1)]
_SMALL_SHARDED = [("meta_tokens", 1), ("ffn_conv_w", 2)]
_REPLICATED = ["norm1_g", "fox_forget_b", "fox_q_g", "fox_k_g", "mla_q_a_g", "mla_kv_a_g", "mla_q_g", "mla_k_g",
               "swa_q_g", "swa_k_g", "swa_sinks", "norm2_g", "ffn_conv_b"]
_ORDER = ["meta_tokens", "norm1_g", "w_in", "fox_forget_b", "fox_q_g", "fox_k_g", "mla_q_a_g", "mla_w_q_up",
          "mla_kv_a_g", "mla_w_kv_up", "mla_q_g", "mla_k_g", "swa_q_g", "swa_k_g", "swa_sinks", "w_branch", "w_o",
          "norm2_g", "ffn_w_up", "ffn_conv_w", "ffn_conv_b", "ffn_w_down"]


def _flat_rows(vecs, dtype, row_mult):
    flat = jnp.concatenate([a.reshape(-1).astype(dtype) for a in vecs])
    rows = -(-flat.shape[0] // (1024 * row_mult)) * row_mult
    return jnp.pad(flat, (0, rows * 1024 - flat.shape[0])).reshape(rows, 1024)


def _unflatten(flat, shapes):
    out, off = [], 0
    for s in shapes:
        n = math.prod(s)
        out.append(flat[off:off + n].reshape(s))
        off += n
    return out


def _to_full(blocks, axis):
    moved = jnp.moveaxis(blocks, 0, axis)
    s = moved.shape
    return moved.reshape(s[:axis] + (s[axis] * s[axis + 1],) + s[axis + 2:])


def _to_blocks(full, axis):
    s = full.shape
    split = full.reshape(s[:axis] + (N_DEV, s[axis] // N_DEV) + s[axis + 1:])
    return jnp.moveaxis(split, axis, 0)


def kernel(x, meta_tokens, norm1_g, w_in, fox_forget_b, fox_q_g, fox_k_g, mla_q_a_g, mla_w_q_up, mla_kv_a_g, mla_w_kv_up, mla_q_g, mla_k_g, swa_q_g, swa_k_g, swa_sinks, w_branch, w_o, norm2_g, ffn_w_up, ffn_conv_w, ffn_conv_b, ffn_w_down, loss_target, m_meta_tokens, m_norm1_g, m_w_in, m_fox_forget_b, m_fox_q_g, m_fox_k_g, m_mla_q_a_g, m_mla_w_q_up, m_mla_kv_a_g, m_mla_w_kv_up, m_mla_q_g, m_mla_k_g, m_swa_q_g, m_swa_k_g, m_swa_sinks, m_w_branch, m_w_o, m_norm2_g, m_ffn_w_up, m_ffn_conv_w, m_ffn_conv_b, m_ffn_w_down, v_meta_tokens, v_norm1_g, v_w_in, v_fox_forget_b, v_fox_q_g, v_fox_k_g, v_mla_q_a_g, v_mla_w_q_up, v_mla_kv_a_g, v_mla_w_kv_up, v_mla_q_g, v_mla_k_g, v_swa_q_g, v_swa_k_g, v_swa_sinks, v_w_branch, v_w_o, v_norm2_g, v_ffn_w_up, v_ffn_conv_w, v_ffn_conv_b, v_ffn_w_down):
    wl = dict(zip(_ORDER, (meta_tokens, norm1_g, w_in, fox_forget_b, fox_q_g, fox_k_g, mla_q_a_g, mla_w_q_up,
                           mla_kv_a_g, mla_w_kv_up, mla_q_g, mla_k_g, swa_q_g, swa_k_g, swa_sinks, w_branch, w_o,
                           norm2_g, ffn_w_up, ffn_conv_w, ffn_conv_b, ffn_w_down)))
    ml = dict(zip(_ORDER, (m_meta_tokens, m_norm1_g, m_w_in, m_fox_forget_b, m_fox_q_g, m_fox_k_g, m_mla_q_a_g,
                           m_mla_w_q_up, m_mla_kv_a_g, m_mla_w_kv_up, m_mla_q_g, m_mla_k_g, m_swa_q_g, m_swa_k_g,
                           m_swa_sinks, m_w_branch, m_w_o, m_norm2_g, m_ffn_w_up, m_ffn_conv_w, m_ffn_conv_b,
                           m_ffn_w_down)))
    vl = dict(zip(_ORDER, (v_meta_tokens, v_norm1_g, v_w_in, v_fox_forget_b, v_fox_q_g, v_fox_k_g, v_mla_q_a_g,
                           v_mla_w_q_up, v_mla_kv_a_g, v_mla_w_kv_up, v_mla_q_g, v_mla_k_g, v_swa_q_g, v_swa_k_g,
                           v_swa_sinks, v_w_branch, v_w_o, v_norm2_g, v_ffn_w_up, v_ffn_conv_w, v_ffn_conv_b,
                           v_ffn_w_down)))
    small_sh = [n for n, _ in _SMALL_SHARDED]
    big = [n for n, _ in _BIG]
    axis_of = dict(_BIG)
    idx = _index(_place())

    def to_full(n, blocks, own=None):
        if own is not None:
            sel = (jnp.arange(N_DEV) == idx).reshape((N_DEV,) + (1,) * own.ndim)
            blocks = jnp.where(sel, own[None], blocks)
        return blocks if n == "w_in" else _to_full(blocks, axis_of[n] - 1)

    local = {(n, l): wl[n][l].astype(BF16) for n in big for l in range(DEPTH)}
    got = _gather([local[(n, 0)] for n in _MIX_BIG] + [wl[n] for n in small_sh], "gather_weights_l0_mix")
    full = {n: wl[n] for n in _REPLICATED}
    for (n, axis), blocks in zip(_SMALL_SHARDED, got[len(_MIX_BIG):]):
        full[n] = _to_full(blocks, axis)
    ready = {(n, 0): to_full(n, blocks) for n, blocks in zip(_MIX_BIG, got)}
    later = {"l0_ffn": [(n, 0) for n in _FFN_BIG], "l1": [(n, 1) for n in big]}
    states = {}
    for key, names in later.items():
        states[key], tick = _exchange_start([local[e] for e in names], "gather_weights_" + key + "_start", True, got[0])
        full["norm1_g"] = full["norm1_g"] + tick

    def fetch(l, stage, after):
        key = "l0_ffn" if l == 0 else "l1"
        if (l, stage) != (0, "mix") and key in states:
            lands = _exchange_wait(states.pop(key), after, "gather_weights_" + key + "_wait", True)
            ready.update({e: to_full(e[0], blocks, local[e]) for e, blocks in zip(later[key], lands)})
        return {n: ready[(n, l)] for n in (_MIX_BIG if stage == "mix" else _FFN_BIG)}

    blocks_of = lambda g, names: [(g[n].blocks(N_DEV) if isinstance(g[n], _ColumnSegments)
                                   else _to_blocks(g[n], axis_of[n] - 1)).astype(BF16) for n in names]
    early = {}

    def hook(l, stage, g):
        if l == DEPTH - 1 and stage == "mix":
            key, names = "l1", big
        elif l == 0:
            merge = ("w_branch", "w_o")
            groups = {"ffn": _FFN_BIG, "merge": merge, "mix": tuple(n for n in _MIX_BIG if n not in merge)}
            key, names = "l0_" + stage, groups[stage]
        else:
            return None
        sends = blocks_of(g, names)
        if key == "l0_mix":
            early[key] = (names, l, sends)
            return None
        state, tick = _exchange_start(sends, "exchange_grads_" + key + "_start")
        early[key] = (names, l, sends, state)
        return tick

    loss, grad_x, grad_meta, grads = _local_step(x[0], loss_target[0], full, hook, fetch)
    result = {kind: {} for kind in ("grad", "delta", "new_m", "new_v")}
    small_grads = {k: jnp.stack([grads[l][k] for l in range(DEPTH)]) for k in grads[0] if k not in big}
    small_grads["meta_tokens"] = grad_meta
    small_full = _REPLICATED + small_sh
    mine_small = _flat_rows([small_grads[n] for n in small_full] + [loss], F32, 8)
    small_state, tick = _exchange_start([mine_small], "gather_small_grads_start", True)
    names, l, sends = early["l0_mix"]
    state, tick = _exchange_start(sends, "exchange_grads_l0_mix_start", after=jnp.reshape(tick, (1, 1)))
    early["l0_mix"] = (names, l, sends, state)
    started = jnp.reshape(tick, (1, 1))
    landed, sent = {}, {}
    after = sends[0]
    for key in ("l1", "l0_ffn", "l0_merge"):
        names, l, sends, state = early[key]
        got = _exchange_wait(state, after, "exchange_grads_" + key + "_wait")
        landed.update({(n, l): p for n, p in zip(names, got)})
        sent.update({(n, l): p for n, p in zip(names, sends)})

    def update(names):
        for n in names:
            outs = _adamw(wl[n], ml[n], vl[n], [landed[(n, l)] for l in range(DEPTH)], "adamw_" + n,
                          [sent[(n, l)] for l in range(DEPTH)], started)
            for kind, val in zip(result, outs):
                result[kind][n] = val

    update(_FFN_BIG)
    done = result["delta"]["ffn_w_up"][0, 0, :8] + result["delta"]["ffn_w_down"][0, 0, :8]
    got_small = _exchange_wait(small_state, done, "gather_small_grads_wait", True)[0]
    sel = (jnp.arange(N_DEV) == idx).reshape(N_DEV, 1, 1)
    total_small = _sum_slots(jnp.where(sel, mine_small[None], got_small), "sum_small_grads").reshape(-1)
    pieces = _unflatten(total_small, [small_grads[n].shape for n in small_full] + [()])
    loss_total = pieces[-1]
    g_small = dict(zip(small_full, pieces[:-1]))
    for n, axis in _SMALL_SHARDED:
        size = wl[n].shape[axis]
        g_small[n] = lax.dynamic_slice_in_dim(g_small[n], idx * size, size, axis)
    flat = lambda d: _flat_rows([d[n] for n in small_full], F32, 8)
    small_out = _adamw(flat(wl), flat(ml), flat(vl), flat(g_small)[None], "adamw_small")
    for kind, fs in zip(result, small_out):
        result[kind].update(zip(small_full, _unflatten(fs.reshape(-1), [wl[n].shape for n in small_full])))
    names, l, sends, state = early["l0_mix"]
    got = _exchange_wait(state, small_out[0], "exchange_grads_l0_mix_wait")
    landed.update({(n, l): p for n, p in zip(names, got)})
    sent.update({(n, l): p for n, p in zip(names, sends)})
    update(_MIX_BIG)
    outs = [loss_total, grad_x[None]]
    for kind in ("grad", "delta", "new_m", "new_v"):
        outs += [result[kind][n] for n in _ORDER]
    return tuple(outs)
```

```python
import functools
import math

import numpy as np
import jax
import jax.numpy as jnp
from jax import lax
from jax.experimental import pallas as pl
from jax.experimental.pallas import tpu as pltpu

F32, BF16 = jnp.float32, jnp.bfloat16
SDS = jax.ShapeDtypeStruct

D_MODEL = 1024
N_META = 16
EPS = 1e-6
WINDOW = 128
ROPE_THETA = 10000.0
HEADS = 8
D_FF = 2816
DEPTH = 2
N_DEV = 8
ADAM_LR, ADAM_B1, ADAM_B2, ADAM_EPS, ADAM_WD, ADAM_STEP = 0.001, 0.9, 0.999, 1e-08, 0.01, 10

ROW_ALIGN = 384
TILE_MM = 384
TILE_ROW = 192
TILE_CONV_BWD = 128
TILE_ATT = 384
TILE_POST = 384
PAIRS = 2
VMEM_LIMIT = 56 * 1024 * 1024

GATES_W = 3072
OTHER_W = 2816
IN_W = GATES_W + OTHER_W
O_FQ, O_FK, O_FV, O_SQ, O_SK, O_SV, O_CQ, O_CKV, O_MISC = 0, 512, 1024, 1536, 2048, 2176, 2304, 2560, 2688
FF_LANE = 32

NEG = -1e30


def _dot(a, b):
    return jnp.dot(a, b, preferred_element_type=F32)


def _dot_nt(a, b):
    return lax.dot_general(a, b, (((1,), (1,)), ((), ())), preferred_element_type=F32)


def _dot_tn(a, b):
    return lax.dot_general(a, b, (((0,), (0,)), ((), ())), preferred_element_type=F32)


def _params(sem):
    return pltpu.CompilerParams(dimension_semantics=sem, vmem_limit_bytes=VMEM_LIMIT)


def _rms(x, g):
    return x * lax.rsqrt(jnp.mean(x * x, axis=-1, keepdims=True) + EPS) * g


def _split_dot(x, m, pieces=2):
    acc, rest = None, x
    for _ in range(pieces):
        part = rest.astype(BF16)
        rest = rest - part.astype(F32)
        acc = _dot(part, m) if acc is None else acc + _dot(part, m)
    return acc


@jax.custom_vjp
def _sel(x, m, mt):
    return _split_dot(x, m)


_sel.defvjp(lambda x, m, mt: (_split_dot(x, m), (m, mt)), lambda res, dy: (_split_dot(dy, res[1]), None, None))


@jax.custom_vjp
def _mm(x, w):
    return _dot(x.astype(BF16), w.astype(BF16))


def _mm_bwd(res, dy):
    x, w = res
    dyb = dy.astype(BF16)
    return _dot_nt(dyb, w.astype(BF16)), _dot_tn(x.astype(BF16), dyb)


_mm.defvjp(lambda x, w: (_mm(x, w), (x, w)), _mm_bwd)


def _rot_impl(x):
    w = x.shape[1]
    lane = lax.broadcasted_iota(jnp.int32, x.shape, 1) % 128
    lo = (lane >= 64) & (lane < 80)
    hi = (lane >= 80) & (lane < 96)
    return jnp.where(hi, pltpu.roll(x, 16, 1), 0.0) - jnp.where(lo, pltpu.roll(x, w - 16, 1), 0.0)


@jax.custom_vjp
def _rot(x):
    return _rot_impl(x)


_rot.defvjp(lambda x: (_rot_impl(x), None), lambda _, dy: (-_rot_impl(dy),))


def _gnorm(x, g, e, et, dim):
    inv = lax.rsqrt(_sel(x * x, e, et) * (1.0 / dim) + EPS)
    return x * _sel(inv, et, e) * g


def _indicator(width, period):
    m = np.zeros((width, 128), np.float32)
    m[np.arange(width), np.arange(width) // period] = 1.0
    return m


def _consts():
    e64 = _indicator(512, 64)
    e128 = _indicator(1024, 128)
    sk = np.zeros((128, 1024), np.float32)
    for h in range(HEADS):
        sk[np.arange(32), 128 * h + 64 + np.arange(32)] = 1.0
    dup = np.zeros((128, 256), np.float32)
    for g in range(2):
        for r in range(2):
            dup[64 * g + np.arange(64), 128 * g + 64 * r + np.arange(64)] = 1.0
    mats = [e64, e64.T, e128, e128.T, sk, sk.T, dup, dup.T]
    return [jnp.asarray(m, BF16) for m in mats]


def _fold_matrix(width, period):
    m = np.zeros((width, 128), np.float32)
    m[np.arange(width), np.arange(width) % period] = 1.0
    return jnp.asarray(m, BF16)


def _rope_tables(lp):
    half = 16
    freqs = ROPE_THETA ** (-np.arange(half, dtype=np.float32) / half)
    ang = np.arange(lp, dtype=np.float32)[:, None] * freqs[None, :]
    cos = np.ones((lp, 128), np.float32)
    sin = np.zeros((lp, 128), np.float32)
    cos[:, 64:80] = np.cos(ang)
    cos[:, 80:96] = np.cos(ang)
    sin[:, 64:80] = np.sin(ang)
    sin[:, 80:96] = np.sin(ang)
    return jnp.asarray(cos), jnp.asarray(sin)


def _norm_matmul(h, g, w, tn, name):
    lp, d = h.shape
    n = w.shape[1]
    tb = TILE_MM

    def body(h_ref, g_ref, w_ref, xn_ref, y_ref):
        @pl.when(pl.program_id(1) == 0)
        def _():
            xn_ref[...] = _rms(h_ref[...], g_ref[...]).astype(BF16)

        y_ref[...] = _dot(xn_ref[...], w_ref[...])

    return pl.pallas_call(
        body, name=name, grid=(lp // tb, n // tn),
        in_specs=[pl.BlockSpec((tb, d), lambda i, j: (i, 0)), pl.BlockSpec((1, d), lambda i, j: (0, 0)),
                  pl.BlockSpec((d, tn), lambda i, j: (0, j))],
        out_specs=[pl.BlockSpec((tb, d), lambda i, j: (i, 0)), pl.BlockSpec((tb, tn), lambda i, j: (i, j))],
        out_shape=[SDS((lp, d), BF16), SDS((lp, n), F32)],
        compiler_params=_params(("parallel", "arbitrary")),
    )(h, g, w)


def _matmul_residual(a, w, res, name):
    m, k = a.shape
    n = w.shape[1]
    tb = TILE_MM

    def body(a_ref, w_ref, r_ref, o_ref):
        o_ref[...] = r_ref[...] + _dot(a_ref[...], w_ref[...])

    return pl.pallas_call(
        body, name=name, grid=(m // tb,),
        in_specs=[pl.BlockSpec((tb, k), lambda i: (i, 0)), pl.BlockSpec((k, n), lambda i: (0, 0)),
                  pl.BlockSpec((tb, n), lambda i: (i, 0))],
        out_specs=pl.BlockSpec((tb, n), lambda i: (i, 0)),
        out_shape=SDS((m, n), F32),
        compiler_params=_params(("parallel",)),
    )(a, w, res)


def _matmul_nt(dy, w, tn, name):
    m, k = dy.shape
    n = w.shape[0]
    tb = TILE_MM

    def body(dy_ref, w_ref, o_ref):
        o_ref[...] = _dot_nt(dy_ref[...].astype(BF16), w_ref[...])

    return pl.pallas_call(
        body, name=name, grid=(m // tb, n // tn),
        in_specs=[pl.BlockSpec((tb, k), lambda i, j: (i, 0)), pl.BlockSpec((tn, k), lambda i, j: (j, 0))],
        out_specs=pl.BlockSpec((tb, tn), lambda i, j: (i, j)),
        out_shape=SDS((m, n), F32),
        compiler_params=_params(("parallel", "arbitrary")),
    )(dy, w)


def _matmul_tn(x, dy, tn, name):
    m, k = x.shape
    n = dy.shape[1]
    tb = TILE_MM
    nb = m // tb

    def body(x_ref, dy_ref, o_ref, acc):
        i = pl.program_id(1)

        @pl.when(i == 0)
        def _():
            acc[...] = jnp.zeros_like(acc)

        acc[...] += _dot_tn(x_ref[...].astype(BF16), dy_ref[...].astype(BF16))

        @pl.when(i == nb - 1)
        def _():
            o_ref[...] = acc[...].astype(BF16)

    return pl.pallas_call(
        body, name=name, grid=(n // tn, nb),
        in_specs=[pl.BlockSpec((tb, k), lambda j, i: (i, 0)), pl.BlockSpec((tb, tn), lambda j, i: (i, j))],
        out_specs=pl.BlockSpec((k, tn), lambda j, i: (0, j)),
        out_shape=SDS((k, n), BF16),
        scratch_shapes=[pltpu.VMEM((k, tn), F32)],
        compiler_params=_params(("parallel", "arbitrary")),
    )(x, dy)


def _norm_matmul_bwd(dys, w, x, g, dres, name):
    m, d = x.shape
    tb = TILE_MM
    widths = [a.shape[1] for a in dys]
    n_dy = len(dys)

    def body(*refs):
        w_ref, x_ref, g_ref, r_ref, o_ref, dg_ref = refs[n_dy:]

        @pl.when(pl.program_id(0) == 0)
        def _():
            dg_ref[...] = jnp.zeros_like(dg_ref)

        dxn, off = None, 0
        for dy_ref, width in zip(refs[:n_dy], widths):
            part = _dot_nt(dy_ref[...], w_ref[:, off:off + width])
            dxn = part if dxn is None else dxn + part
            off += width
        _, vjp = jax.vjp(_rms, x_ref[...], g_ref[...])
        dx, dg = vjp(dxn)
        o_ref[...] = r_ref[...] + dx
        dg_ref[...] += dg

    row = pl.BlockSpec((tb, d), lambda i: (i, 0))
    vec = pl.BlockSpec((1, d), lambda i: (0, 0))
    return pl.pallas_call(
        body, name=name, grid=(m // tb,),
        in_specs=[pl.BlockSpec((tb, wd), lambda i: (i, 0)) for wd in widths]
        + [pl.BlockSpec(w.shape, lambda i: (0, 0)), row, vec, row],
        out_specs=[row, vec],
        out_shape=[SDS((m, d), F32), SDS((1, d), F32)],
        compiler_params=_params(("arbitrary",)),
    )(*dys, w, x, g, dres)


def _prep_math(pieces, prm, consts, cos, sin):
    fq, fk, sq, sk, sv, cq, ckv, misc = pieces
    gfq, gfk, gsq, gsk, fb, gqa, gkva, gmq, gmk, wq, wkk, wkv = prm
    e64, e64t, e128, e128t, skm, skt, dup, dupt = consts
    cos8 = jnp.concatenate([cos] * HEADS, axis=1)
    sin8 = jnp.concatenate([sin] * HEADS, axis=1)
    fq_n = _gnorm(fq, gfq, e64, e64t, 64)
    fk_n = _gnorm(fk, gfk, e64, e64t, 64)
    ls = jax.nn.log_sigmoid(misc + fb)
    q = _gnorm(_mm(_rms(cq, gqa), wq), gmq, e128, e128t, 96)
    mq = q * cos8 + _rot(q) * sin8
    kva = _rms(ckv, gkva)
    k = _gnorm(_mm(kva, wkk) + _sel(misc, skm, skt), gmk, e128, e128t, 96)
    mk = k * cos8 + _rot(k) * sin8
    mv = _mm(kva, wkv)
    sq_n = _gnorm(sq, gsq, e64, e64t, 64)
    sk_n = _gnorm(sk, gsk, e64[0:128], e64t[:, 0:128], 64)
    skd = _sel(sk_n, dup, dupt)
    svd = _sel(sv, dup, dupt)
    return fq_n, fk_n, ls, mq, mk, mv, sq_n, skd, svd


_PIECES = [(O_FQ, 512), (O_FK, 512), (O_SQ, 512), (O_SK, 128), (O_SV, 128), (O_CQ, 256), (O_CKV, 128), (O_MISC, 128)]
_PRM_SHAPES = [(1, 512), (1, 512), (1, 512), (1, 128), (1, 128), (1, 256), (1, 128), (1, 1024), (1, 1024),
               (256, 1024), (128, 1024), (128, 512)]
_CONST_SHAPES = [(512, 128), (128, 512), (1024, 128), (128, 1024), (128, 1024), (1024, 128), (128, 256), (256, 128)]


def _piece_specs(tb):
    def spec(off, width):
        blk = (GATES_W + off) // width
        return pl.BlockSpec((tb, width), lambda i, blk=blk: (i, blk))
    return [spec(o, w) for o, w in _PIECES] + [spec(O_FV, 512)]


def _full_specs(shapes):
    return [pl.BlockSpec(s, lambda i: (0, 0)) for s in shapes]


def _prep_fwd(proj, prm, consts, cos, sin, name):
    lp = proj.shape[0]
    tb = TILE_ROW
    row = lambda w: pl.BlockSpec((tb, w), lambda i: (i, 0))

    def body(*refs):
        pieces = [r[...] for r in refs[0:8]]
        fv = refs[8][...]
        prm_v = [r[...] for r in refs[9:21]]
        consts_v = [r[...] for r in refs[21:29]]
        cos_v, sin_v = refs[29][...], refs[30][...]
        outs = refs[31:]
        fq_n, fk_n, ls, mq, mk, mv, sq_n, skd, svd = _prep_math(pieces, prm_v, consts_v, cos_v, sin_v)
        for ref, val in zip(outs, (fq_n, fk_n, fv, mq, mk, mv, sq_n, skd, svd)):
            ref[...] = val.astype(BF16)
        outs[9][...] = ls

    widths = [512, 512, 512, 1024, 1024, 512, 512, 256, 256]
    return pl.pallas_call(
        body, name=name, grid=(lp // tb,),
        in_specs=_piece_specs(tb) + _full_specs(_PRM_SHAPES) + _full_specs(_CONST_SHAPES) + [row(128), row(128)],
        out_specs=[row(w) for w in widths] + [row(128)],
        out_shape=[SDS((lp, w), BF16) for w in widths] + [SDS((lp, 128), F32)],
        compiler_params=_params(("parallel",)),
    )(*([proj] * 9), *prm, *consts, cos, sin)


def _prep_bwd(proj, prm, consts, cos, sin, cots, folds, name):
    lp = proj.shape[0]
    tb = TILE_ROW
    row = lambda w: pl.BlockSpec((tb, w), lambda i: (i, 0))
    fold64, fold128 = folds

    def body(*refs):
        pieces = [r[...] for r in refs[0:8]]
        prm_v = [r[...] for r in refs[9:21]]
        consts_v = [r[...] for r in refs[21:29]]
        cos_v, sin_v = refs[29][...], refs[30][...]
        dfq, dfk, dfv, dmq, dmk, dmv, dsq, dskd, dsvd, dls = [r[...] for r in refs[31:41]]
        f64, f128 = refs[41][...], refs[42][...]
        d_ref = refs[43]
        g_refs = refs[44:]

        @pl.when(pl.program_id(0) == 0)
        def _():
            for r in g_refs:
                r[...] = jnp.zeros_like(r)

        f = lambda pc, pr: _prep_math(pc, pr, consts_v, cos_v, sin_v)
        _, vjp = jax.vjp(f, pieces, prm_v)
        dpc, dprm = vjp((dfq, dfk, dls, dmq, dmk, dmv, dsq, dskd, dsvd))
        d_fq, d_fk, d_sq, d_sk, d_sv, d_cq, d_ckv, d_misc = dpc
        for off, val in ((O_FQ, d_fq), (O_FK, d_fk), (O_FV, dfv), (O_SQ, d_sq), (O_SK, d_sk), (O_SV, d_sv),
                         (O_CQ, d_cq), (O_CKV, d_ckv), (O_MISC, d_misc)):
            d_ref[:, off:off + val.shape[1]] = val.astype(BF16)
        folded = {0: f64, 1: f64, 2: f64, 3: f64[0:128], 7: f128, 8: f128}
        for idx, (ref, val) in enumerate(zip(g_refs, dprm)):
            if idx in folded:
                ref[...] += _split_dot(jnp.broadcast_to(val, (8, val.shape[1])), folded[idx], 3)
            elif val.shape[0] == 1:
                ref[...] += jnp.broadcast_to(val, ref.shape)
            else:
                ref[...] += val

    g_shapes = [(8, 128), (8, 128), (8, 128), (8, 128), (8, 128), (8, 256), (8, 128), (8, 128), (8, 128),
                (256, 1024), (128, 1024), (128, 512)]
    cot_widths = [512, 512, 512, 1024, 1024, 512, 512, 256, 256, 128]
    return pl.pallas_call(
        body, name=name, grid=(lp // tb,),
        in_specs=(_piece_specs(tb) + _full_specs(_PRM_SHAPES) + _full_specs(_CONST_SHAPES) + [row(128), row(128)]
                  + [row(w) for w in cot_widths] + _full_specs([(512, 128), (1024, 128)])),
        out_specs=[row(OTHER_W)] + _full_specs(g_shapes),
        out_shape=[SDS((lp, OTHER_W), BF16)] + [SDS(s, F32) for s in g_shapes],
        compiler_params=_params(("arbitrary",)),
    )(*([proj] * 9), *prm, *consts, cos, sin, *cots, fold64, fold128)


def _cumsum(xs, reverse, name):
    lp = xs[0].shape[0]
    tb = TILE_MM
    nb = lp // tb
    n_in = len(xs)
    idx = (lambda i: (nb - 1 - i, 0)) if reverse else (lambda i: (i, 0))

    def body(*refs):
        o_ref, carry = refs[n_in], refs[n_in + 1]

        @pl.when(pl.program_id(0) == 0)
        def _():
            carry[...] = jnp.zeros_like(carry)

        x = refs[0][...]
        for r in refs[1:n_in]:
            x = x + r[...]
        r_i = lax.broadcasted_iota(jnp.int32, (tb, tb), 0)
        c_i = lax.broadcasted_iota(jnp.int32, (tb, tb), 1)
        tri = ((c_i >= r_i) if reverse else (c_i <= r_i)).astype(BF16)
        acc, rest = None, x
        for _ in range(3):
            part = rest.astype(BF16)
            rest = rest - part.astype(F32)
            acc = _dot(tri, part) if acc is None else acc + _dot(tri, part)
        o_ref[...] = acc + carry[...]
        carry[...] += jnp.sum(x, axis=0, keepdims=True)

    return pl.pallas_call(
        body, name=name, grid=(nb,),
        in_specs=[pl.BlockSpec((tb, 128), idx)] * n_in,
        out_specs=pl.BlockSpec((tb, 128), idx),
        out_shape=SDS((lp, 128), F32),
        scratch_shapes=[pltpu.VMEM((1, 128), F32)],
        compiler_params=_params(("arbitrary",)),
    )(*xs)


class _Att:
    def __init__(self, mode):
        self.mode = mode
        self.wide = mode == "mla"
        self.qw = 256 if self.wide else 128
        self.scale = (96 if mode == "mla" else 64) ** -0.5
        self.pairs = 2 * PAIRS if mode == "swa" else PAIRS

    def resident(self, x, lo, scaled):
        if self.wide:
            return x[:, 0:128], x[:, 128:256]
        if scaled:
            x = x * jnp.asarray(self.scale, x.dtype)
        zero = jnp.zeros_like(x)
        return jnp.where(lo, x, zero), jnp.where(lo, zero, x)

    def moving(self, x):
        return (x[:, 0:128], x[:, 128:256]) if self.wide else (x, x)

    def logits(self, a, b, qpos, kpos, key_decay, slope, masked):
        s = _dot_nt(a, b)
        if self.wide:
            s = s * self.scale
        if self.mode == "fox":
            s = s - key_decay
        if self.mode == "swa":
            s = s - slope * (qpos - kpos).astype(F32)
        if masked:
            ok = kpos <= qpos
            if self.mode == "swa":
                ok = ok & ((kpos < N_META) | (qpos - kpos < WINDOW))
            s = jnp.where(ok, s, NEG)
        return s


def _as_rows(col):
    return jnp.broadcast_to(col, (col.shape[0], 128)).T[0:8, :]


def _halves(x, lo):
    zero = jnp.zeros_like(x)
    return jnp.where(lo, x, zero), jnp.where(lo, zero, x)


def _kv_specs(att, lp, rows):
    row = lambda g, i: (i if rows != lp else 0, g)
    if att.mode == "swa":
        return (pl.BlockSpec((rows, 64 * att.pairs), row),) * 2
    return pl.BlockSpec((rows, att.pairs * att.qw), row), pl.BlockSpec((rows, att.pairs * 128), row)


def _pair_cols(att, x, pp, width):
    return x if x.shape[1] == width else x[:, pp * width:(pp + 1) * width]


def _kv_cols(att, x, pp, width):
    return _pair_cols(att, x, pp // 2 if att.mode == "swa" else pp, width)


def _att_fwd(att, q, k, v, extra, name):
    lp = q.shape[0]
    t = TILE_ATT
    nq = lp // t
    qw = att.qw
    mode = att.mode
    pairs = att.pairs
    nh = 2 * pairs

    def body(*refs):
        q_ref, k_ref, v_ref = refs[0:3]
        o_ref, lse_ref = refs[-2:]
        g, qi = pl.program_id(0), pl.program_id(1)
        lo = lax.broadcasted_iota(jnp.int32, (1, 128), 1) < 64
        q_all = q_ref[...]
        q_heads = [h for pp in range(pairs) for h in att.resident(_pair_cols(att, q_all, pp, qw), lo, True)]
        qpos = qi * t + lax.broadcasted_iota(jnp.int32, (t, 1), 0)

        def step(first, cols, carry, masked):
            ks = pl.multiple_of(first, 128)
            kc, vc = k_ref[pl.ds(ks, cols), :], v_ref[pl.ds(ks, cols), :]
            kpos = first + lax.broadcasted_iota(jnp.int32, (1, cols), 1)
            out = []
            for h in range(nh):
                pp = h // 2
                m, l, acc = carry[3 * h:3 * h + 3]
                k_h = att.moving(_kv_cols(att, kc, pp, qw))[h % 2]
                decay = refs[3][h, :, pl.ds(ks, cols)] if mode == "fox" else None
                slope = refs[4][nh * g + h] if mode == "swa" else None
                s = att.logits(q_heads[h], k_h, qpos, kpos, decay, slope, masked)
                m_new = jnp.maximum(m, jnp.max(s, axis=-1, keepdims=True))
                alpha = jnp.exp(m - m_new)
                pe = jnp.exp(s - m_new)
                l = alpha * l + jnp.sum(pe, axis=-1, keepdims=True)
                acc = alpha * acc + _dot(pe.astype(BF16), _kv_cols(att, vc, pp, 128))
                out += [m_new, l, acc]
            return tuple(out)

        init = []
        for h in range(nh):
            if mode == "swa":
                init += [jnp.full((t, 1), refs[3][nh * g + h], F32), jnp.ones((t, 1), F32)]
            else:
                init += [jnp.full((t, 1), NEG, F32), jnp.zeros((t, 1), F32)]
            init.append(jnp.zeros((t, 128), F32))
        if mode == "swa":
            band = jnp.maximum(qi * t - WINDOW, 0)
            carry = lax.fori_loop(0, (band >= 128).astype(jnp.int32), lambda j, c: step(0, 128, c, True), tuple(init))
            carry = step(band, t + WINDOW, carry, True)
        else:
            carry = lax.fori_loop(0, qi // 2, lambda j, c: step(2 * j * t, 2 * t, c, False), tuple(init))
            carry = lax.fori_loop(0, qi % 2, lambda j, c: step((qi - 1) * t, t, c, False), carry)
            carry = step(qi * t, t, carry, True)
        outs = []
        for pp in range(pairs):
            (ma, la, acca), (mb, lb, accb) = carry[6 * pp:6 * pp + 3], carry[6 * pp + 3:6 * pp + 6]
            outs.append(jnp.where(lo, acca / la, accb / lb).astype(BF16))
            lse_ref[2 * pp] = ma + jnp.log(la)
            lse_ref[2 * pp + 1] = mb + jnp.log(lb)
        o_ref[...] = jnp.concatenate(outs, axis=1)

    in_specs = [pl.BlockSpec((t, pairs * qw), lambda g, i: (i, g)), *_kv_specs(att, lp, lp)]
    if mode == "fox":
        in_specs += [pl.BlockSpec((nh, 1, lp), lambda g, i: (g, 0, 0))]
    if mode == "swa":
        in_specs += [pl.BlockSpec(memory_space=pltpu.SMEM)] * 2
    return pl.pallas_call(
        body, name=name, grid=(4 // pairs, nq), in_specs=in_specs,
        out_specs=[pl.BlockSpec((t, pairs * 128), lambda g, i: (i, g)), pl.BlockSpec((nh, t, 1), lambda g, i: (g, i, 0))],
        out_shape=[SDS((lp, 512), BF16), SDS((HEADS, lp, 1), F32)],
        compiler_params=_params(("parallel", "arbitrary")),
    )(q, k, v, *extra)


def _att_bwd(att, q, k, v, o, do, lse, extra, name):
    lp = q.shape[0]
    t = TILE_ATT
    nq = lp // t
    qw = att.qw
    mode = att.mode
    pairs = att.pairs
    nh = 2 * pairs
    kw = 64 * pairs if mode == "swa" else pairs * qw
    vw = 64 * pairs if mode == "swa" else pairs * 128

    def body(*refs):
        q_ref, k_ref, v_ref, o_ref, do_ref, lse_ref = refs[0:6]
        n_out = {"fox": 5, "mla": 3, "swa": 4}[mode]
        outs = refs[len(refs) - n_out:]
        dq_ref, dk_ref, dv_ref = outs[0:3]
        g, qi = pl.program_id(0), pl.program_id(1)

        @pl.when(qi == 0)
        def _():
            dk_ref[...] = jnp.zeros_like(dk_ref)
            dv_ref[...] = jnp.zeros_like(dv_ref)
            if mode == "fox":
                outs[4][...] = jnp.zeros_like(outs[4])

        lo = lax.broadcasted_iota(jnp.int32, (1, 128), 1) < 64
        q_all, do_all = q_ref[...], do_ref[...]
        prod = do_all.astype(F32) * o_ref[...].astype(F32)
        q_heads, q_plain, do_heads, do_pairs, delta = [], [], [], [], []
        for pp in range(pairs):
            q_pp = _pair_cols(att, q_all, pp, qw)
            q_heads += att.resident(q_pp, lo, True)
            q_plain += att.moving(q_pp)
            do_pp = _pair_cols(att, do_all, pp, 128)
            do_pairs.append(do_pp)
            do_heads += _halves(do_pp, lo)
            pr_pp = _pair_cols(att, prod, pp, 128)
            delta += [jnp.sum(jnp.where(lo, pr_pp, 0.0), axis=-1, keepdims=True),
                      jnp.sum(jnp.where(lo, 0.0, pr_pp), axis=-1, keepdims=True)]
        lse_v = [lse_ref[h] for h in range(nh)]
        qpos = qi * t + lax.broadcasted_iota(jnp.int32, (t, 1), 0)

        def step(first, cols, carry, masked):
            ks = pl.multiple_of(first, 128)
            kc, vc = k_ref[pl.ds(ks, cols), :], v_ref[pl.ds(ks, cols), :]
            kpos = first + lax.broadcasted_iota(jnp.int32, (1, cols), 1)
            out, dk_parts, dv_parts = [], [], []
            for h in range(nh):
                pp = h // 2
                k_h = att.moving(_kv_cols(att, kc, pp, qw))[h % 2]
                decay = refs[6][h, :, pl.ds(ks, cols)] if mode == "fox" else None
                slope = refs[7][nh * g + h] if mode == "swa" else None
                s = att.logits(q_heads[h], k_h, qpos, kpos, decay, slope, masked)
                pr = jnp.exp(s - lse_v[h])
                ds = pr * (_dot_nt(do_heads[h], _kv_cols(att, vc, pp, 128)) - delta[h])
                dsb = ds.astype(BF16)
                out.append(carry[2 * h] + _dot(dsb, k_h))
                out.append(carry[2 * h + 1] + jnp.sum(ds, axis=-1, keepdims=True) if mode == "fox" else carry[2 * h + 1])
                dk_parts.append(_dot_tn(dsb, q_plain[h]))
                dv_parts.append(_dot_tn(pr.astype(BF16), do_pairs[pp]))
                if mode == "fox":
                    outs[4][h, 0:1, pl.ds(ks, cols)] -= jnp.sum(ds, axis=0, keepdims=True)
            rows = pl.ds(ks, cols)
            for pp in range(pairs):
                dv_pp = jnp.where(lo, dv_parts[2 * pp], dv_parts[2 * pp + 1])
                if att.wide:
                    dk_pp = jnp.concatenate(dk_parts[2 * pp:2 * pp + 2], axis=1) * att.scale
                else:
                    dk_pp = jnp.where(lo, dk_parts[2 * pp], dk_parts[2 * pp + 1]) * att.scale
                if mode == "swa":
                    dk_ref[rows, (pp // 2) * 128:(pp // 2 + 1) * 128] += dk_pp
                    dv_ref[rows, (pp // 2) * 128:(pp // 2 + 1) * 128] += dv_pp
                else:
                    dk_ref[rows, pp * qw:(pp + 1) * qw] += dk_pp
                    dv_ref[rows, pp * 128:(pp + 1) * 128] += dv_pp
            return tuple(out)

        init = (jnp.zeros((t, 128), F32), jnp.zeros((t, 1), F32)) * nh
        if mode == "swa":
            band = jnp.maximum(qi * t - WINDOW, 0)
            carry = lax.fori_loop(0, (band >= 128).astype(jnp.int32), lambda j, c: step(0, 128, c, True), init)
            carry = step(band, t + WINDOW, carry, True)
        else:
            carry = lax.fori_loop(0, qi // 2, lambda j, c: step(2 * j * t, 2 * t, c, False), init)
            carry = lax.fori_loop(0, qi % 2, lambda j, c: step((qi - 1) * t, t, c, False), carry)
            carry = step(qi * t, t, carry, True)
        dq = []
        for pp in range(pairs):
            dqa, dca, dqb, dcb = carry[4 * pp:4 * pp + 4]
            dq += [dqa, dqb] if att.wide else [jnp.where(lo, dqa, dqb)]
            if mode == "fox":
                outs[3][2 * pp] = _as_rows(dca)
                outs[3][2 * pp + 1] = _as_rows(dcb)
        dq_ref[...] = jnp.concatenate(dq, axis=1) * att.scale
        if mode == "swa":
            ds_ref = outs[3]

            @pl.when(qi == 0)
            def _():
                ds_ref[...] = jnp.zeros_like(ds_ref)

            lane = lax.broadcasted_iota(jnp.int32, (8, 128), 1)
            acc = jnp.zeros((8, 128), F32)
            for h in range(nh):
                tot = -jnp.sum(jnp.exp(refs[6][nh * g + h] - lse_v[h]) * delta[h])
                acc = acc + jnp.where(lane == h, tot, 0.0)
            ds_ref[0] += acc

    col = pl.BlockSpec((nh, t, 1), lambda g, i: (g, i, 0))
    in_specs = [pl.BlockSpec((t, pairs * qw), lambda g, i: (i, g)), *_kv_specs(att, lp, lp),
                pl.BlockSpec((t, pairs * 128), lambda g, i: (i, g)), pl.BlockSpec((t, pairs * 128), lambda g, i: (i, g)), col]
    out_specs = [pl.BlockSpec((t, pairs * qw), lambda g, i: (i, g)), pl.BlockSpec((lp, kw), lambda g, i: (0, g)),
                 pl.BlockSpec((lp, vw), lambda g, i: (0, g))]
    n_groups = 4 // pairs
    out_shape = [SDS((lp, 4 * qw), F32), SDS((lp, n_groups * kw), F32), SDS((lp, n_groups * vw), F32)]
    if mode == "fox":
        in_specs += [pl.BlockSpec((nh, 1, lp), lambda g, i: (g, 0, 0))]
        out_specs += [pl.BlockSpec((nh, 8, t), lambda g, i: (g, 0, i)), pl.BlockSpec((nh, 8, lp), lambda g, i: (g, 0, 0))]
        out_shape += [SDS((HEADS, 8, lp), F32)] * 2
    if mode == "swa":
        in_specs += [pl.BlockSpec(memory_space=pltpu.SMEM)] * 2
        out_specs.append(pl.BlockSpec((1, 8, 128), lambda g, i: (g, 0, 0)))
        out_shape.append(SDS((n_groups, 8, 128), F32))
    return pl.pallas_call(
        body, name=name, grid=(n_groups, nq), in_specs=in_specs, out_specs=out_specs, out_shape=out_shape,
        compiler_params=_params(("parallel", "arbitrary")),
    )(q, k, v, o, do, lse, *extra)


def _post_fwd(h, proj, outs, wb, wo, name):
    lp, d = h.shape
    tb = TILE_POST
    row = lambda w: pl.BlockSpec((tb, w), lambda i: (i, 0))

    def body(h_ref, g0, g1, g2, oa, ob, oc, wb_ref, wo_ref, o_ref):
        merged = jnp.zeros((tb, d), F32)
        for n, (g_ref, br) in enumerate(((g0, oa), (g1, ob), (g2, oc))):
            merged = merged + jax.nn.sigmoid(g_ref[...]) * _dot(br[...], wb_ref[n])
        o_ref[...] = h_ref[...] + _dot(merged.astype(BF16), wo_ref[...])

    gate = lambda n: pl.BlockSpec((tb, d), lambda i, n=n: (i, n))
    return pl.pallas_call(
        body, name=name, grid=(lp // tb,),
        in_specs=[row(d), gate(0), gate(1), gate(2), row(512), row(512), row(512),
                  pl.BlockSpec((3, 512, d), lambda i: (0, 0, 0)), pl.BlockSpec((d, d), lambda i: (0, 0))],
        out_specs=row(d), out_shape=SDS((lp, d), F32),
        compiler_params=_params(("parallel",)),
    )(h, proj, proj, proj, *outs, wb, wo)


def _post_bwd(dh, proj, outs, wb, wo, name):
    lp, d = dh.shape
    tb = TILE_POST
    row = lambda w: pl.BlockSpec((tb, w), lambda i: (i, 0))

    def body(dh_ref, g0, g1, g2, oa, ob, oc, wb_ref, wo_ref, dg_ref, doa, dob, doc, dwb_ref, dwo_ref):
        @pl.when(pl.program_id(0) == 0)
        def _():
            dwb_ref[...] = jnp.zeros_like(dwb_ref)
            dwo_ref[...] = jnp.zeros_like(dwo_ref)

        dhb = dh_ref[...].astype(BF16)
        dm = _dot_nt(dhb, wo_ref[...])
        merged = jnp.zeros((tb, d), F32)
        for n, (g_ref, br, do_ref) in enumerate(((g0, oa, doa), (g1, ob, dob), (g2, oc, doc))):
            gate = jax.nn.sigmoid(g_ref[...])
            o_n = br[...]
            y = _dot(o_n, wb_ref[n])
            merged = merged + gate * y
            dy = (dm * gate).astype(BF16)
            dg_ref[:, n * d:(n + 1) * d] = (dm * y * gate * (1.0 - gate)).astype(BF16)
            do_ref[...] = _dot_nt(dy, wb_ref[n]).astype(BF16)
            dwb_ref[n] += _dot_tn(o_n, dy)
        dwo_ref[...] += _dot_tn(merged.astype(BF16), dhb)

    gate = lambda n: pl.BlockSpec((tb, d), lambda i, n=n: (i, n))
    wb_spec = pl.BlockSpec((3, 512, d), lambda i: (0, 0, 0))
    wo_spec = pl.BlockSpec((d, d), lambda i: (0, 0))
    return pl.pallas_call(
        body, name=name, grid=(lp // tb,),
        in_specs=[row(d), gate(0), gate(1), gate(2), row(512), row(512), row(512), wb_spec, wo_spec],
        out_specs=[row(GATES_W), row(512), row(512), row(512), wb_spec, wo_spec],
        out_shape=[SDS((lp, GATES_W), BF16)] + [SDS((lp, 512), BF16)] * 3 + [SDS((3, 512, d), F32), SDS((d, d), F32)],
        compiler_params=_params(("arbitrary",)),
    )(dh, proj, proj, proj, *outs, wb, wo)


def _shift_down(x, halo, n, first):
    rows = lax.broadcasted_iota(jnp.int32, x.shape, 0)
    halo = jnp.where(first, 0.0, halo)
    edge = jnp.concatenate([pltpu.roll(halo, n, 0), jnp.zeros((x.shape[0] - 8, x.shape[1]), F32)], axis=0)
    return jnp.where(rows < n, edge, pltpu.roll(x, n, 0))


def _shift_up(x, halo, n, last):
    tb = x.shape[0]
    rows = lax.broadcasted_iota(jnp.int32, x.shape, 0)
    halo = jnp.where(last, 0.0, halo)
    edge = jnp.concatenate([jnp.zeros((tb - 8, x.shape[1]), F32), pltpu.roll(halo, 8 - n, 0)], axis=0)
    return jnp.where(rows >= tb - n, edge, pltpu.roll(x, tb - n, 0))


def _conv(u, halo, w_ref, b_ref, first):
    taps = (_shift_down(u, halo, 2, first), _shift_down(u, halo, 1, first), u)
    c = b_ref[...] + w_ref[0:1, :] * taps[0] + w_ref[1:2, :] * taps[1] + w_ref[2:3, :] * taps[2]
    return c, taps


def _ffn_specs(tb, f):
    hb = tb // 8
    cur = lambda c: pl.BlockSpec((tb, f), lambda i, c=c: (i, c))
    prev = lambda c: pl.BlockSpec((8, f), lambda i, c=c: (jnp.maximum(i * hb - 1, 0), c))
    vec = lambda r, c: pl.BlockSpec((r, f), lambda i, c=c: (0, c))
    return cur, prev, vec


def _ffn_act_fwd(u, cw, cb, name):
    lp = u.shape[0]
    f = D_FF
    tb = TILE_ROW
    cur, prev, vec = _ffn_specs(tb, f)

    def body(ug, uv, hg, hv, wg, wv, bg, bv, o_ref):
        first = pl.program_id(0) == 0
        cg, _ = _conv(ug[...], hg[...], wg, bg, first)
        cv, _ = _conv(uv[...], hv[...], wv, bv, first)
        o_ref[...] = (cg * jax.nn.sigmoid(cg) * cv).astype(BF16)

    return pl.pallas_call(
        body, name=name, grid=(lp // tb,),
        in_specs=[cur(0), cur(1), prev(0), prev(1), vec(8, 0), vec(8, 1), vec(1, 0), vec(1, 1)],
        out_specs=pl.BlockSpec((tb, f), lambda i: (i, 0)), out_shape=SDS((lp, f), BF16),
        compiler_params=_params(("parallel",)),
    )(u, u, u, u, cw, cw, cb, cb)


def _ffn_act_bwd_conv(u, dact, cw, cb, name):
    lp = u.shape[0]
    f = D_FF
    tb = TILE_CONV_BWD
    cur, prev, vec = _ffn_specs(tb, f)

    def body(ug, uv, hg, hv, wg, wv, bg, bv, da_ref, dcg_ref, dcv_ref, dwg, dwv, dbg, dbv):
        first = pl.program_id(0) == 0

        @pl.when(first)
        def _():
            for r in (dwg, dwv, dbg, dbv):
                r[...] = jnp.zeros_like(r)

        cg, tg = _conv(ug[...], hg[...], wg, bg, first)
        cv, tv = _conv(uv[...], hv[...], wv, bv, first)
        da = da_ref[...]
        sg = jax.nn.sigmoid(cg)
        dcg = da * cv * sg * (1.0 + cg * (1.0 - sg))
        dcv = da * cg * sg
        dcg_ref[...] = dcg
        dcv_ref[...] = dcv
        for dc, taps, dw, db in ((dcg, tg, dwg, dbg), (dcv, tv, dwv, dbv)):
            for n in range(3):
                dw[n:n + 1, :] += jnp.sum(dc * taps[n], axis=0, keepdims=True)
            db[0:1, :] += jnp.sum(dc, axis=0, keepdims=True)

    row = pl.BlockSpec((tb, f), lambda i: (i, 0))
    acc = pl.BlockSpec((8, f), lambda i: (0, 0))
    return pl.pallas_call(
        body, name=name, grid=(lp // tb,),
        in_specs=[cur(0), cur(1), prev(0), prev(1), vec(8, 0), vec(8, 1), vec(1, 0), vec(1, 1), row],
        out_specs=[row, row, acc, acc, acc, acc],
        out_shape=[SDS((lp, f), F32)] * 2 + [SDS((8, f), F32)] * 4,
        compiler_params=_params(("arbitrary",)),
    )(u, u, u, u, cw, cw, cb, cb, dact)


def _ffn_act_bwd_in(dcg, dcv, cw, name):
    lp = dcg.shape[0]
    f = D_FF
    tb = TILE_ROW
    nb = lp // tb
    hb = tb // 8
    cur = pl.BlockSpec((tb, f), lambda i: (i, 0))
    nxt = pl.BlockSpec((8, f), lambda i: (jnp.minimum((i + 1) * hb, nb * hb - 1), 0))
    vec = lambda c: pl.BlockSpec((8, f), lambda i, c=c: (0, c))

    def body(dg, dv, ng, nv, wg, wv, o_ref):
        last = pl.program_id(0) == nb - 1
        for c, (dc_ref, n_ref, w_ref) in enumerate(((dg, ng, wg), (dv, nv, wv))):
            dc, halo = dc_ref[...], n_ref[...]
            du = (w_ref[2:3, :] * dc + w_ref[1:2, :] * _shift_up(dc, halo, 1, last)
                  + w_ref[0:1, :] * _shift_up(dc, halo, 2, last))
            o_ref[:, c * f:(c + 1) * f] = du.astype(BF16)

    return pl.pallas_call(
        body, name=name, grid=(nb,),
        in_specs=[cur, cur, nxt, nxt, vec(0), vec(1)],
        out_specs=pl.BlockSpec((tb, 2 * f), lambda i: (i, 0)), out_shape=SDS((lp, 2 * f), BF16),
        compiler_params=_params(("parallel",)),
    )(dcg, dcv, dcg, dcv, cw, cw)


def _matmul_residual_loss(a, w, res, target, n_real, name):
    m, k = a.shape
    d = w.shape[1]
    tb = TILE_MM

    def body(a_ref, w_ref, r_ref, t_ref, dy_ref, loss_ref):
        i = pl.program_id(0)

        @pl.when(i == 0)
        def _():
            loss_ref[...] = jnp.zeros_like(loss_ref)

        y = r_ref[...] + _dot(a_ref[...], w_ref[...])
        rows = i * tb + lax.broadcasted_iota(jnp.int32, (tb, 1), 0)
        real = (rows >= N_META) & (rows < N_META + n_real)
        diff = jnp.where(real, y - t_ref[...], 0.0)
        dy_ref[...] = diff * (1.0 / d)
        loss_ref[...] += (0.5 / d) * jnp.sum(diff * diff).reshape(1, 1)

    row = pl.BlockSpec((tb, d), lambda i: (i, 0))
    return pl.pallas_call(
        body, name=name, grid=(m // tb,),
        in_specs=[pl.BlockSpec((tb, k), lambda i: (i, 0)), pl.BlockSpec((k, d), lambda i: (0, 0)), row, row],
        out_specs=[row, pl.BlockSpec((1, 1), lambda i: (0, 0))],
        out_shape=[SDS((m, d), F32), SDS((1, 1), F32)],
        compiler_params=_params(("arbitrary",)),
    )(a, w, res, target)


def _pad_lanes(v, width, at=0):
    return jnp.pad(v.astype(F32), (at, width - at - v.shape[0]))[None, :]


_IN_COLS = dict(fq=(0, 512), fk=(512, 512), fv=(1024, 512), ff=(1536, 8), cq=(1544, 256), ckv=(1800, 128),
                kr=(1928, 32), sq=(1960, 512), sk=(2472, 128), sv=(2600, 128), gates=(2728, 3072))


def _orig_cols(src, start, width):
    if src.ndim == 2:
        return [src[:, start:start + width]]
    per, out, pos = src.shape[2], [], start
    while pos < start + width:
        d, off = divmod(pos, per)
        take = min(per - off, start + width - pos)
        out.append(src[d, :, off:off + take])
        pos += take
    return out


class _ColumnSegments:
    def __init__(self, segments):
        self.segments = segments

    def full(self):
        return jnp.concatenate([a[:, s:s + w] for a, s, w in self.segments], axis=1)

    def blocks(self, n):
        per = sum(w for _, _, w in self.segments) // n
        out, seg, used = [], 0, 0
        for _ in range(n):
            pieces, need = [], per
            while need:
                a, s, w = self.segments[seg]
                take = min(w - used, need)
                pieces.append(a[:, s + used:s + used + take])
                used, need = used + take, need - take
                if used == w:
                    seg, used = seg + 1, 0
            out.append(jnp.concatenate(pieces, axis=1))
        return jnp.stack(out)


def _mix_params(w, big, l):
    b = lambda a: a.astype(BF16)
    win = big["w_in"]
    order = ("gates", "fq", "fk", "fv", "sq", "sk", "sv", "cq", "ckv", "kr", "ff")
    pieces = [p for name in order for p in _orig_cols(win, *_IN_COLS[name])]
    w_in = b(jnp.concatenate(pieces + [jnp.zeros((D_MODEL, 88), win.dtype)], axis=1))
    wq = jnp.pad(big["mla_w_q_up"].reshape(256, HEADS, 96), ((0, 0), (0, 0), (0, 32))).reshape(256, 1024)
    wkv = big["mla_w_kv_up"].reshape(128, HEADS, 128)
    wkk = jnp.pad(wkv[:, :, :64], ((0, 0), (0, 0), (0, 64))).reshape(128, 1024)
    wkvv = wkv[:, :, 64:].reshape(128, 512)
    tile = lambda g, n: jnp.tile(g.astype(F32), n)[None, :]
    prm = [tile(w["fox_q_g"][l], 8), tile(w["fox_k_g"][l], 8), tile(w["swa_q_g"][l], 8), tile(w["swa_k_g"][l], 2),
           _pad_lanes(w["fox_forget_b"][l], 128, FF_LANE), w["mla_q_a_g"][l][None, :], w["mla_kv_a_g"][l][None, :],
           tile(jnp.pad(w["mla_q_g"][l], (0, 32)), 8), tile(jnp.pad(w["mla_k_g"][l], (0, 32)), 8),
           wq.astype(F32), wkk.astype(F32), wkvv.astype(F32)]
    return dict(g1=w["norm1_g"][l][None, :], w_in=w_in, prm=prm, sinks=w["swa_sinks"][l].astype(F32),
                wb=b(big["w_branch"]), wo=b(big["w_o"]))


def _ffn_params(w, big, l):
    cw = jnp.pad(w["ffn_conv_w"][l].astype(F32), ((0, 5), (0, 0)))
    return dict(g2=w["norm2_g"][l][None, :], w_up=big["ffn_w_up"].astype(BF16), cw=cw,
                cb=w["ffn_conv_b"][l][None, :].astype(F32), w_down=big["ffn_w_down"].astype(BF16))


def _decay_rows(c):
    return c[:, FF_LANE:FF_LANE + HEADS].T[:, None, :]


def _from_rows(row):
    return jnp.pad(row[:, 0, :].T, ((0, 0), (FF_LANE, 128 - FF_LANE - HEADS)))


def _layer_fwd_mix(h, lw, consts, cos, sin, slopes, l):
    tag = f"l{l}_"
    xn, proj = _norm_matmul(h, lw["g1"], lw["w_in"], IN_W, tag + "in_proj")
    fq, fk, fv, mq, mk, mv, sq, skd, svd, ls = _prep_fwd(proj, lw["prm"], consts, cos, sin, tag + "prep")
    c = _cumsum([ls], False, tag + "decay_cumsum")
    c_row = _decay_rows(c)
    oa, lse_a = _att_fwd(_Att("fox"), fq, fk, fv, (c_row,), tag + "fox_fwd")
    ob, lse_b = _att_fwd(_Att("mla"), mq, mk, mv, (), tag + "mla_fwd")
    oc, lse_c = _att_fwd(_Att("swa"), sq, skd, svd, (lw["sinks"], slopes), tag + "swa_fwd")
    h2 = _post_fwd(h, proj, (oa, ob, oc), lw["wb"], lw["wo"], tag + "merge")
    saved = dict(h=h, xn=xn, proj=proj, q=(fq, mq, sq), k=(fk, mk, skd), v=(fv, mv, svd), c=c_row,
                 o=(oa, ob, oc), lse=(lse_a, lse_b, lse_c), h2=h2)
    return h2, saved


def _layer_fwd_ffn(h2, lw, l, loss_of=None):
    tag = f"l{l}_"
    xn2, u = _norm_matmul(h2, lw["g2"], lw["w_up"], 2 * D_FF, tag + "ffn_up")
    act = _ffn_act_fwd(u, lw["cw"], lw["cb"], tag + "ffn_act")
    if loss_of is None:
        out = _matmul_residual(act, lw["w_down"], h2, tag + "ffn_down")
    else:
        out = _matmul_residual_loss(act, lw["w_down"], h2, *loss_of, tag + "ffn_down_loss")
    return out, dict(xn2=xn2, u=u, act=act)


def _layer_bwd_ffn(dh3, lw, sv, l):
    tag = f"l{l}_"
    f = D_FF
    dact = _matmul_nt(dh3, lw["w_down"], f, tag + "ffn_down_dx")
    dw_down = _matmul_tn(sv["act"], dh3, D_MODEL, tag + "ffn_down_dw")
    dcg, dcv, dwg, dwv, dbg, dbv = _ffn_act_bwd_conv(sv["u"], dact, lw["cw"], lw["cb"], tag + "ffn_act_dc")
    du = _ffn_act_bwd_in(dcg, dcv, lw["cw"], tag + "ffn_act_du")
    dw_up = _matmul_tn(sv["xn2"], du, f, tag + "ffn_up_dw")
    dh2, dg2 = _norm_matmul_bwd([du], lw["w_up"], sv["h2"], lw["g2"], dh3, tag + "ffn_up_dx")
    g = dict(norm2_g=dg2[0], ffn_w_up=dw_up, ffn_conv_w=jnp.concatenate([dwg[0:3], dwv[0:3]], axis=1),
             ffn_conv_b=jnp.concatenate([dbg[0], dbv[0]]), ffn_w_down=dw_down)
    return dh2, g


def _layer_bwd_mix(dh2, lw, sv, consts, folds, cos, sin, slopes, l, hook=None, merge_hook=None):
    tag = f"l{l}_"
    dgates, doa, dob, doc, dwb, dwo = _post_bwd(dh2, sv["proj"], sv["o"], lw["wb"], lw["wo"], tag + "merge_bwd")
    c_row = sv["c"]
    tick = merge_hook({"w_branch": dwb, "w_o": dwo}) if merge_hook else None
    if tick is not None:
        c_row = c_row + tick
    extras = ((c_row,), (), (lw["sinks"], slopes))
    grads = []
    for n, (mode, do) in enumerate((("fox", doa), ("mla", dob), ("swa", doc))):
        res = _att_bwd(_Att(mode), sv["q"][n], sv["k"][n], sv["v"][n], sv["o"][n], do, sv["lse"][n], extras[n],
                       tag + mode + "_bwd")
        grads.append((res[0], res[1], res[2], res[3:]))
    (dfq, dfk, dfv, (dcq, dck)), (dmq, dmk, dmv, _), (dsq, dskd, dsvd, (dsink,)) = grads
    dls = _cumsum([_from_rows(dcq), _from_rows(dck)], True, tag + "decay_cumsum_bwd")
    res = _prep_bwd(sv["proj"], lw["prm"], consts, cos, sin,
                    (dfq, dfk, dfv, dmq, dmk, dmv, dsq, dskd, dsvd, dls), folds, tag + "prep_bwd")
    dother, pg = res[0], res[1:]
    dh, dg1 = _norm_matmul_bwd([dgates, dother], lw["w_in"], sv["h"], lw["g1"], dh2, tag + "in_proj_dx")
    dw_g = _matmul_tn(sv["xn"], dgates, GATES_W, tag + "in_proj_dw_gates")
    dw_o = _matmul_tn(sv["xn"], dother, OTHER_W, tag + "in_proj_dw_other")
    d_in = _ColumnSegments([
        (dw_o, O_FQ, 1536), (dw_o, O_MISC + FF_LANE, 8), (dw_o, O_CQ, 256), (dw_o, O_CKV, 128), (dw_o, O_MISC, 32),
        (dw_o, O_SQ, 512), (dw_o, O_SK, 128), (dw_o, O_SV, 128), (dw_g, 0, GATES_W)])
    d_wq = pg[9].reshape(256, HEADS, 128)[:, :, :96].reshape(256, 768)
    d_wkv = jnp.concatenate([pg[10].reshape(128, HEADS, 128)[:, :, :64], pg[11].reshape(128, HEADS, 64)],
                            axis=2).reshape(128, 1024)
    g = dict(
        w_in=d_in, fox_forget_b=pg[4][0, FF_LANE:FF_LANE + 8], fox_q_g=pg[0][0, :64],
        fox_k_g=pg[1][0, :64], mla_q_a_g=pg[5][0], mla_w_q_up=d_wq, mla_kv_a_g=pg[6][0], mla_w_kv_up=d_wkv,
        mla_q_g=pg[7][0, :96], mla_k_g=pg[8][0, :96], swa_q_g=pg[2][0, :64], swa_k_g=pg[3][0, :64],
        swa_sinks=dsink[:, 0, :HEADS // dsink.shape[0]].reshape(HEADS), w_branch=dwb, w_o=dwo)
    tick = hook(g) if hook else None
    g["norm1_g"] = dg1[0]
    return dh, g, tick


_MIX_BIG = ("w_in", "mla_w_q_up", "mla_w_kv_up", "w_branch", "w_o")
_FFN_BIG = ("ffn_w_up", "ffn_w_down")


def _local_step(x, target, w, hook=None, fetch=None):
    if fetch is None:
        fetch = lambda l, stage, after: {n: w[n][l] for n in (_MIX_BIG if stage == "mix" else _FFN_BIG)}
    seq = x.shape[0]
    length = N_META + seq
    lp = -(-length // ROW_ALIGN) * ROW_ALIGN
    pad = lp - length
    h = jnp.concatenate([w["meta_tokens"].astype(F32), x, jnp.zeros((pad, D_MODEL), F32)], axis=0)
    tgt = jnp.pad(target, ((N_META, pad), (0, 0)))
    consts = _consts()
    folds = (_fold_matrix(512, 64), _fold_matrix(1024, 128))
    cos, sin = _rope_tables(lp)
    slopes = jnp.asarray(2.0 ** (-8.0 * np.arange(1, HEADS + 1, dtype=np.float32) / HEADS), F32)
    lws, saved = [], []
    for l in range(DEPTH):
        lw = _mix_params(w, fetch(l, "mix", h), l)
        h, sv = _layer_fwd_mix(h, lw, consts, cos, sin, slopes, l)
        lw.update(_ffn_params(w, fetch(l, "ffn", h), l))
        h, sv_ffn = _layer_fwd_ffn(h, lw, l, (tgt, seq) if l == DEPTH - 1 else None)
        lws.append(lw)
        saved.append({**sv, **sv_ffn})
    dh, loss = h
    grads = [None] * DEPTH
    for l in reversed(range(DEPTH)):
        dh, g_ffn = _layer_bwd_ffn(dh, lws[l], saved[l], l)
        tick = hook(l, "ffn", g_ffn) if hook else None
        if tick is not None:
            lws[l]["sinks"] = lws[l]["sinks"] + tick
        mix_hook = (lambda g, l=l, g_ffn=g_ffn: hook(l, "mix", {**g_ffn, **g})) if hook else None
        merge_hook = (lambda g, l=l: hook(l, "merge", g)) if hook else None
        dh, g_mix, tick = _layer_bwd_mix(dh, lws[l], saved[l], consts, folds, cos, sin, slopes, l, mix_hook, merge_hook)
        grads[l] = {**g_ffn, **g_mix}
        if tick is not None and l > 0:
            lws[l - 1]["cw"] = lws[l - 1]["cw"] + tick
    return loss, dh[N_META:length], dh[:N_META], grads


def _place():
    return lax.axis_index("x"), lax.axis_index("y"), lax.axis_index("c")


def _flip(pos, k):
    x, y, c = pos
    return (1 - x if k & 4 else x, 1 - y if k & 2 else y, 1 - c if k & 1 else c)


def _index(pos):
    return 4 * pos[0] + 2 * pos[1] + pos[2]


def _gather(tensors, name):
    n_t = len(tensors)

    def body(*refs):
        ins, outs = refs[:n_t], refs[n_t:2 * n_t]
        send_sems, recv_sems, local_sems = refs[2 * n_t:]
        x, y, c = _place()
        me, sibling = (x, y, c), (x, y, 1 - c)
        chips = [(1 - x, y), (x, 1 - y), (1 - x, 1 - y)]

        def copy(t, k, block, to, src=None):
            dst = outs[t].at[_index(block)]
            return pltpu.make_async_remote_copy(
                src_ref=dst if src is None else src, dst_ref=dst, send_sem=send_sems.at[t, k],
                recv_sem=recv_sems.at[t, k], device_id=to, device_id_type=pl.DeviceIdType.MESH)

        local, sent = [], []
        for t in range(n_t):
            local.append(pltpu.make_async_copy(ins[t], outs[t].at[_index(me)], local_sems.at[t]))
            local[-1].start()
            sent.append(copy(t, 0, me, sibling, src=ins[t]))
            sent += [copy(t, 1 + j, me, (*chip, c), src=ins[t]) for j, chip in enumerate(chips)]
        for cp in sent:
            cp.start()
        for j, chip in enumerate(chips):
            for t in range(n_t):
                copy(t, 1 + j, (*chip, c), me).wait_recv()
                sent.append(copy(t, 4 + j, (*chip, c), sibling))
                sent[-1].start()
        for t in range(n_t):
            copy(t, 0, sibling, me).wait_recv()
            for j, chip in enumerate(chips):
                copy(t, 4 + j, (*chip, 1 - c), me).wait_recv()
        for cp in sent:
            cp.wait_send()
        for cp in local:
            cp.wait()

    any_spec = pl.BlockSpec(memory_space=pl.ANY)
    return pl.pallas_call(
        body, name=name, in_specs=[any_spec] * n_t, out_specs=[any_spec] * n_t,
        out_shape=[SDS((N_DEV,) + a.shape, a.dtype) for a in tensors],
        scratch_shapes=[pltpu.SemaphoreType.DMA((n_t, N_DEV - 1)), pltpu.SemaphoreType.DMA((n_t, N_DEV - 1)),
                        pltpu.SemaphoreType.DMA((n_t,))],
    )(*tensors)


def _exchange_start(tensors, name, gather=False, after=None):
    n_t = len(tensors)

    def body(*refs):
        ins, lands = refs[:n_t], refs[n_t:2 * n_t]
        send_sem, recv_sem = refs[2 * n_t + 1:2 * n_t + 3]
        token = refs[-1]
        me = _place()
        mine = _index(me)
        for t in range(n_t):
            for k in range(1, N_DEV):
                peer = _flip(me, k)
                pltpu.make_async_remote_copy(
                    src_ref=ins[t] if gather else ins[t].at[_index(peer)], dst_ref=lands[t].at[mine],
                    send_sem=send_sem, recv_sem=recv_sem, device_id=peer, device_id_type=pl.DeviceIdType.MESH).start()
        token[...] = jnp.zeros_like(token)

    hbm = pl.BlockSpec(memory_space=pltpu.HBM)
    sem = pl.BlockSpec(memory_space=pltpu.SEMAPHORE)
    one = pltpu.SemaphoreType.DMA(())
    land_shape = lambda a: ((N_DEV,) + a.shape) if gather else a.shape
    bufs = ([pltpu.HBM(a.shape, a.dtype) for a in tensors] + [pltpu.HBM(land_shape(a), a.dtype) for a in tensors])
    after = jnp.zeros((8, 128), F32) if after is None else after
    outs = pl.pallas_call(
        body, name=name, in_specs=[hbm] * (2 * n_t) + [pl.BlockSpec(memory_space=pl.ANY)],
        out_specs=[sem, sem] + [hbm] * (2 * n_t) + [pl.BlockSpec(memory_space=pltpu.VMEM)],
        out_shape=[one, one] + bufs + [SDS((8, 128), F32)],
        input_output_aliases={i: 2 + i for i in range(2 * n_t)},
        compiler_params=pltpu.CompilerParams(has_side_effects=pltpu.SideEffectType.DATAFLOW_SIDE_EFFECTING),
    )(*[pltpu.with_memory_space_constraint(a, pltpu.HBM) for a in tensors],
      *[pltpu.with_memory_space_constraint(lax.empty(land_shape(a), a.dtype), pltpu.HBM) for a in tensors], after)
    return outs[:-1], outs[-1][0, 0]


def _exchange_wait(state, after, name, gather=False):
    n_t = (len(state) - 2) // 2

    def body(*refs):
        send_sem, recv_sem = refs[0:2]
        ins, lands = refs[2:2 + n_t], refs[2 + n_t:2 + 2 * n_t]
        me = _place()
        for t in range(n_t):
            for k in range(1, N_DEV):
                peer = _flip(me, k)
                copy = pltpu.make_async_remote_copy(
                    src_ref=ins[t] if gather else ins[t].at[_index(peer)], dst_ref=lands[t].at[_index(peer)],
                    send_sem=send_sem, recv_sem=recv_sem, device_id=peer, device_id_type=pl.DeviceIdType.MESH)
                copy.wait_send()
                copy.wait_recv()

    hbm = pl.BlockSpec(memory_space=pltpu.HBM)
    sem = pl.BlockSpec(memory_space=pltpu.SEMAPHORE)
    bufs = [pltpu.HBM(a.shape, a.dtype) for a in state[2:]]
    outs = pl.pallas_call(
        body, name=name, in_specs=[sem, sem] + [hbm] * (2 * n_t) + [pl.BlockSpec(memory_space=pl.ANY)],
        out_specs=[hbm] * (2 * n_t), out_shape=bufs,
        input_output_aliases={2 + i: i for i in range(2 * n_t)},
        compiler_params=pltpu.CompilerParams(has_side_effects=pltpu.SideEffectType.DATAFLOW_SIDE_EFFECTING),
    )(*state, after)
    return outs[n_t:]


def _sum_slots(parts, name):
    n, rows, w = parts.shape
    tb = 8

    def body(p_ref, o_ref):
        acc = p_ref[0].astype(F32)
        for s in range(1, n):
            acc = acc + p_ref[s].astype(F32)
        o_ref[...] = acc

    return pl.pallas_call(
        body, name=name, grid=(rows // tb,),
        in_specs=[pl.BlockSpec((n, tb, w), lambda i: (0, i, 0))], out_specs=pl.BlockSpec((tb, w), lambda i: (i, 0)),
        out_shape=SDS((rows, w), F32), compiler_params=_params(("parallel",)),
    )(parts)


def _adamw(wt, m, v, parts, name, own=None, after=None):
    shape = wt.shape
    parts = parts if isinstance(parts, (list, tuple)) else [parts]
    n, w = parts[0].shape[0], shape[-1]
    rows = math.prod(shape[:-1])
    per = rows // len(parts)
    step = 16 if parts[0].dtype == BF16 else 8
    tb = max([t for t in range(step, 257, step) if per % t == 0] or [per])
    nb = per // tb
    c1 = 1.0 / (1.0 - ADAM_B1 ** ADAM_STEP)
    c2 = 1.0 / (1.0 - ADAM_B2 ** ADAM_STEP)
    state = [a.reshape(rows, w) for a in (wt, m, v)]
    n_in = 4 if own is None else 5
    outs = None
    for l in reversed(range(len(parts))):
        def body(*refs):
            idx_ref = None if own is None else refs[0]
            w_ref, m_ref, v_ref, p_ref = refs[n_in - 4:n_in] if own is None else refs[1:5]
            g_out, d_out, m_out, v_out = refs[-4:]
            g = None
            for s in range(n):
                term = p_ref[s] if own is None else jnp.where(idx_ref[0] == s, refs[5][0], p_ref[s])
                g = term.astype(F32) if g is None else g + term.astype(F32)
            m_new = ADAM_B1 * m_ref[...] + (1.0 - ADAM_B1) * g
            v_new = ADAM_B2 * v_ref[...] + (1.0 - ADAM_B2) * (g * g)
            g_out[...] = g
            m_out[...] = m_new
            v_out[...] = v_new
            d_out[...] = -ADAM_LR * ((m_new * c1) / (jnp.sqrt(v_new * c2) + ADAM_EPS) + ADAM_WD * w_ref[...])

        row = pl.BlockSpec((tb, w), lambda i, *_, l=l: (l * nb + i, 0))
        in_specs = [row, row, row, pl.BlockSpec((n, tb, w), lambda i, *_: (0, i, 0))]
        args = [*state, parts[l].reshape(n, per, w)]
        if own is not None:
            in_specs.append(pl.BlockSpec((1, tb, w), lambda i, idx: (idx[0], i, 0)))
            args.append(own[l].reshape(n, per, w))
        prev = [] if outs is None else list(outs)
        behind = [] if after is None else [after]
        in_specs += [pl.BlockSpec(memory_space=pl.ANY)] * (len(prev) + len(behind))
        n_pre = 0 if own is None else 1
        call = dict(name=f"{name}_{l}", out_shape=[SDS((rows, w), F32)] * 4,
                    input_output_aliases={n_pre + len(args) + k: k for k in range(len(prev))},
                    compiler_params=_params(("parallel",)))
        if own is None:
            outs = pl.pallas_call(body, grid=(nb,), in_specs=in_specs, out_specs=[row] * 4, **call)(*args, *prev, *behind)
        else:
            spec = pltpu.PrefetchScalarGridSpec(num_scalar_prefetch=1, grid=(nb,), in_specs=in_specs, out_specs=[row] * 4)
            idx = jnp.reshape(_index(_place()), (1,)).astype(jnp.int32)
            outs = pl.pallas_call(body, grid_spec=spec, **call)(idx, *args, *prev, *behind)
    return [o.reshape(shape) for o in outs]


_BIG = [("w_in", 2), ("mla_w_q_up", 2), ("mla_w_kv_up", 2), ("w_branch", 3), ("w_o", 1), ("ffn_w_up", 2), ("ffn_w_down", 1)]
_SMALL_SHARDED = [("meta_tokens", 1), ("ffn_conv_w", 2)]
_REPLICATED = ["norm1_g", "fox_forget_b", "fox_q_g", "fox_k_g", "mla_q_a_g", "mla_kv_a_g", "mla_q_g", "mla_k_g",
               "swa_q_g", "swa_k_g", "swa_sinks", "norm2_g", "ffn_conv_b"]
_ORDER = ["meta_tokens", "norm1_g", "w_in", "fox_forget_b", "fox_q_g", "fox_k_g", "mla_q_a_g", "mla_w_q_up",
          "mla_kv_a_g", "mla_w_kv_up", "mla_q_g", "mla_k_g", "swa_q_g", "swa_k_g", "swa_sinks", "w_branch", "w_o",
          "norm2_g", "ffn_w_up", "ffn_conv_w", "ffn_conv_b", "ffn_w_down"]


def _flat_rows(vecs, dtype, row_mult):
    flat = jnp.concatenate([a.reshape(-1).astype(dtype) for a in vecs])
    rows = -(-flat.shape[0] // (1024 * row_mult)) * row_mult
    return jnp.pad(flat, (0, rows * 1024 - flat.shape[0])).reshape(rows, 1024)


def _unflatten(flat, shapes):
    out, off = [], 0
    for s in shapes:
        n = math.prod(s)
        out.append(flat[off:off + n].reshape(s))
        off += n
    return out


def _to_full(blocks, axis):
    moved = jnp.moveaxis(blocks, 0, axis)
    s = moved.shape
    return moved.reshape(s[:axis] + (s[axis] * s[axis + 1],) + s[axis + 2:])


def _to_blocks(full, axis):
    s = full.shape
    split = full.reshape(s[:axis] + (N_DEV, s[axis] // N_DEV) + s[axis + 1:])
    return jnp.moveaxis(split, axis, 0)


def kernel(x, meta_tokens, norm1_g, w_in, fox_forget_b, fox_q_g, fox_k_g, mla_q_a_g, mla_w_q_up, mla_kv_a_g, mla_w_kv_up, mla_q_g, mla_k_g, swa_q_g, swa_k_g, swa_sinks, w_branch, w_o, norm2_g, ffn_w_up, ffn_conv_w, ffn_conv_b, ffn_w_down, loss_target, m_meta_tokens, m_norm1_g, m_w_in, m_fox_forget_b, m_fox_q_g, m_fox_k_g, m_mla_q_a_g, m_mla_w_q_up, m_mla_kv_a_g, m_mla_w_kv_up, m_mla_q_g, m_mla_k_g, m_swa_q_g, m_swa_k_g, m_swa_sinks, m_w_branch, m_w_o, m_norm2_g, m_ffn_w_up, m_ffn_conv_w, m_ffn_conv_b, m_ffn_w_down, v_meta_tokens, v_norm1_g, v_w_in, v_fox_forget_b, v_fox_q_g, v_fox_k_g, v_mla_q_a_g, v_mla_w_q_up, v_mla_kv_a_g, v_mla_w_kv_up, v_mla_q_g, v_mla_k_g, v_swa_q_g, v_swa_k_g, v_swa_sinks, v_w_branch, v_w_o, v_norm2_g, v_ffn_w_up, v_ffn_conv_w, v_ffn_conv_b, v_ffn_w_down):
    wl = dict(zip(_ORDER, (meta_tokens, norm1_g, w_in, fox_forget_b, fox_q_g, fox_k_g, mla_q_a_g, mla_w_q_up,
                           mla_kv_a_g, mla_w_kv_up, mla_q_g, mla_k_g, swa_q_g, swa_k_g, swa_sinks, w_branch, w_o,
                           norm2_g, ffn_w_up, ffn_conv_w, ffn_conv_b, ffn_w_down)))
    ml = dict(zip(_ORDER, (m_meta_tokens, m_norm1_g, m_w_in, m_fox_forget_b, m_fox_q_g, m_fox_k_g, m_mla_q_a_g,
                           m_mla_w_q_up, m_mla_kv_a_g, m_mla_w_kv_up, m_mla_q_g, m_mla_k_g, m_swa_q_g, m_swa_k_g,
                           m_swa_sinks, m_w_branch, m_w_o, m_norm2_g, m_ffn_w_up, m_ffn_conv_w, m_ffn_conv_b,
                           m_ffn_w_down)))
    vl = dict(zip(_ORDER, (v_meta_tokens, v_norm1_g, v_w_in, v_fox_forget_b, v_fox_q_g, v_fox_k_g, v_mla_q_a_g,
                           v_mla_w_q_up, v_mla_kv_a_g, v_mla_w_kv_up, v_mla_q_g, v_mla_k_g, v_swa_q_g, v_swa_k_g,
                           v_swa_sinks, v_w_branch, v_w_o, v_norm2_g, v_ffn_w_up, v_ffn_conv_w, v_ffn_conv_b,
                           v_ffn_w_down)))
    small_sh = [n for n, _ in _SMALL_SHARDED]
    big = [n for n, _ in _BIG]
    axis_of = dict(_BIG)
    idx = _index(_place())

    def to_full(n, blocks, own=None):
        if own is not None:
            sel = (jnp.arange(N_DEV) == idx).reshape((N_DEV,) + (1,) * own.ndim)
            blocks = jnp.where(sel, own[None], blocks)
        return blocks if n == "w_in" else _to_full(blocks, axis_of[n] - 1)

    local = {(n, l): wl[n][l].astype(BF16) for n in big for l in range(DEPTH)}
    got = _gather([local[(n, 0)] for n in _MIX_BIG] + [wl[n] for n in small_sh], "gather_weights_l0_mix")
    full = {n: wl[n] for n in _REPLICATED}
    for (n, axis), blocks in zip(_SMALL_SHARDED, got[len(_MIX_BIG):]):
        full[n] = _to_full(blocks, axis)
    ready = {(n, 0): to_full(n, blocks) for n, blocks in zip(_MIX_BIG, got)}
    later = {"l0_ffn": [(n, 0) for n in _FFN_BIG], "l1": [(n, 1) for n in big]}
    states = {}
    for key, names in later.items():
        states[key], tick = _exchange_start([local[e] for e in names], "gather_weights_" + key + "_start", True, got[0])
        full["norm1_g"] = full["norm1_g"] + tick

    def fetch(l, stage, after):
        key = "l0_ffn" if l == 0 else "l1"
        if (l, stage) != (0, "mix") and key in states:
            lands = _exchange_wait(states.pop(key), after, "gather_weights_" + key + "_wait", True)
            ready.update({e: to_full(e[0], blocks, local[e]) for e, blocks in zip(later[key], lands)})
        return {n: ready[(n, l)] for n in (_MIX_BIG if stage == "mix" else _FFN_BIG)}

    blocks_of = lambda g, names: [(g[n].blocks(N_DEV) if isinstance(g[n], _ColumnSegments)
                                   else _to_blocks(g[n], axis_of[n] - 1)).astype(BF16) for n in names]
    early = {}

    def hook(l, stage, g):
        if l == DEPTH - 1 and stage == "mix":
            key, names = "l1", big
        elif l == 0:
            merge = ("w_branch", "w_o")
            groups = {"ffn": _FFN_BIG, "merge": merge, "mix": tuple(n for n in _MIX_BIG if n not in merge)}
            key, names = "l0_" + stage, groups[stage]
        else:
            return None
        sends = blocks_of(g, names)
        if key == "l0_mix":
            early[key] = (names, l, sends)
            return None
        state, tick = _exchange_start(sends, "exchange_grads_" + key + "_start")
        early[key] = (names, l, sends, state)
        return tick

    loss, grad_x, grad_meta, grads = _local_step(x[0], loss_target[0], full, hook, fetch)
    result = {kind: {} for kind in ("grad", "delta", "new_m", "new_v")}
    small_grads = {k: jnp.stack([grads[l][k] for l in range(DEPTH)]) for k in grads[0] if k not in big}
    small_grads["meta_tokens"] = grad_meta
    small_full = _REPLICATED + small_sh
    mine_small = _flat_rows([small_grads[n] for n in small_full] + [loss], F32, 8)
    small_state, tick = _exchange_start([mine_small], "gather_small_grads_start", True)
    names, l, sends = early["l0_mix"]
    state, tick = _exchange_start(sends, "exchange_grads_l0_mix_start", after=jnp.reshape(tick, (1, 1)))
    early["l0_mix"] = (names, l, sends, state)
    started = jnp.reshape(tick, (1, 1))
    landed, sent = {}, {}
    after = sends[0]
    for key in ("l1", "l0_ffn", "l0_merge"):
        names, l, sends, state = early[key]
        got = _exchange_wait(state, after, "exchange_grads_" + key + "_wait")
        landed.update({(n, l): p for n, p in zip(names, got)})
        sent.update({(n, l): p for n, p in zip(names, sends)})

    def update(names):
        for n in names:
            outs = _adamw(wl[n], ml[n], vl[n], [landed[(n, l)] for l in range(DEPTH)], "adamw_" + n,
                          [sent[(n, l)] for l in range(DEPTH)], started)
            for kind, val in zip(result, outs):
                result[kind][n] = val

    update(_FFN_BIG)
    done = result["delta"]["ffn_w_up"][0, 0, :8] + result["delta"]["ffn_w_down"][0, 0, :8]
    got_small = _exchange_wait(small_state, done, "gather_small_grads_wait", True)[0]
    sel = (jnp.arange(N_DEV) == idx).reshape(N_DEV, 1, 1)
    total_small = _sum_slots(jnp.where(sel, mine_small[None], got_small), "sum_small_grads").reshape(-1)
    pieces = _unflatten(total_small, [small_grads[n].shape for n in small_full] + [()])
    loss_total = pieces[-1]
    g_small = dict(zip(small_full, pieces[:-1]))
    for n, axis in _SMALL_SHARDED:
        size = wl[n].shape[axis]
        g_small[n] = lax.dynamic_slice_in_dim(g_small[n], idx * size, size, axis)
    flat = lambda d: _flat_rows([d[n] for n in small_full], F32, 8)
    small_out = _adamw(flat(wl), flat(ml), flat(vl), flat(g_small)[None], "adamw_small")
    for kind, fs in zip(result, small_out):
        result[kind].update(zip(small_full, _unflatten(fs.reshape(-1), [wl[n].shape for n in small_full])))
    names, l, sends, state = early["l0_mix"]
    got = _exchange_wait(state, small_out[0], "exchange_grads_l0_mix_wait")
    landed.update({(n, l): p for n, p in zip(names, got)})
    sent.update({(n, l): p for n, p in zip(names, sends)})
    update(_MIX_BIG)
    outs = [loss_total, grad_x[None]]
    for kind in ("grad", "delta", "new_m", "new_v"):
        outs += [result[kind][n] for n in _ORDER]
    return tuple(outs)
```

```python
import functools
import math

import numpy as np
import jax
import jax.numpy as jnp
from jax import lax
from jax.experimental import pallas as pl
from jax.experimental.pallas import tpu as pltpu

F32, BF16 = jnp.float32, jnp.bfloat16
SDS = jax.ShapeDtypeStruct

D_MODEL = 1024
N_META = 16
EPS = 1e-6
WINDOW = 128
ROPE_THETA = 10000.0
HEADS = 8
D_FF = 2816
DEPTH = 2
N_DEV = 8
ADAM_LR, ADAM_B1, ADAM_B2, ADAM_EPS, ADAM_WD, ADAM_STEP = 0.001, 0.9, 0.999, 1e-08, 0.01, 10

ROW_ALIGN = 384
TILE_MM = 384
TILE_ROW = 192
TILE_CONV_BWD = 128
TILE_ATT = 384
TILE_POST = 384
PAIRS = 2
VMEM_LIMIT = 56 * 1024 * 1024

GATES_W = 3072
OTHER_W = 2816
IN_W = GATES_W + OTHER_W
O_FQ, O_FK, O_FV, O_SQ, O_SK, O_SV, O_CQ, O_CKV, O_MISC = 0, 512, 1024, 1536, 2048, 2176, 2304, 2560, 2688
FF_LANE = 32

NEG = -1e30


def _dot(a, b):
    return jnp.dot(a, b, preferred_element_type=F32)


def _dot_nt(a, b):
    return lax.dot_general(a, b, (((1,), (1,)), ((), ())), preferred_element_type=F32)


def _dot_tn(a, b):
    return lax.dot_general(a, b, (((0,), (0,)), ((), ())), preferred_element_type=F32)


def _params(sem):
    return pltpu.CompilerParams(dimension_semantics=sem, vmem_limit_bytes=VMEM_LIMIT)


def _rms(x, g):
    return x * lax.rsqrt(jnp.mean(x * x, axis=-1, keepdims=True) + EPS) * g


def _split_dot(x, m, pieces=2):
    acc, rest = None, x
    for _ in range(pieces):
        part = rest.astype(BF16)
        rest = rest - part.astype(F32)
        acc = _dot(part, m) if acc is None else acc + _dot(part, m)
    return acc


@jax.custom_vjp
def _sel(x, m, mt):
    return _split_dot(x, m)


_sel.defvjp(lambda x, m, mt: (_split_dot(x, m), (m, mt)), lambda res, dy: (_split_dot(dy, res[1]), None, None))


@jax.custom_vjp
def _mm(x, w):
    return _dot(x.astype(BF16), w.astype(BF16))


def _mm_bwd(res, dy):
    x, w = res
    dyb = dy.astype(BF16)
    return _dot_nt(dyb, w.astype(BF16)), _dot_tn(x.astype(BF16), dyb)


_mm.defvjp(lambda x, w: (_mm(x, w), (x, w)), _mm_bwd)


def _rot_impl(x):
    w = x.shape[1]
    lane = lax.broadcasted_iota(jnp.int32, x.shape, 1) % 128
    lo = (lane >= 64) & (lane < 80)
    hi = (lane >= 80) & (lane < 96)
    return jnp.where(hi, pltpu.roll(x, 16, 1), 0.0) - jnp.where(lo, pltpu.roll(x, w - 16, 1), 0.0)


@jax.custom_vjp
def _rot(x):
    return _rot_impl(x)


_rot.defvjp(lambda x: (_rot_impl(x), None), lambda _, dy: (-_rot_impl(dy),))


def _gnorm(x, g, e, et, dim):
    inv = lax.rsqrt(_sel(x * x, e, et) * (1.0 / dim) + EPS)
    return x * _sel(inv, et, e) * g


def _indicator(width, period):
    m = np.zeros((width, 128), np.float32)
    m[np.arange(width), np.arange(width) // period] = 1.0
    return m


def _consts():
    e64 = _indicator(512, 64)
    e128 = _indicator(1024, 128)
    sk = np.zeros((128, 1024), np.float32)
    for h in range(HEADS):
        sk[np.arange(32), 128 * h + 64 + np.arange(32)] = 1.0
    dup = np.zeros((128, 256), np.float32)
    for g in range(2):
        for r in range(2):
            dup[64 * g + np.arange(64), 128 * g + 64 * r + np.arange(64)] = 1.0
    mats = [e64, e64.T, e128, e128.T, sk, sk.T, dup, dup.T]
    return [jnp.asarray(m, BF16) for m in mats]


def _fold_matrix(width, period):
    m = np.zeros((width, 128), np.float32)
    m[np.arange(width), np.arange(width) % period] = 1.0
    return jnp.asarray(m, BF16)


def _rope_tables(lp):
    half = 16
    freqs = ROPE_THETA ** (-np.arange(half, dtype=np.float32) / half)
    ang = np.arange(lp, dtype=np.float32)[:, None] * freqs[None, :]
    cos = np.ones((lp, 128), np.float32)
    sin = np.zeros((lp, 128), np.float32)
    cos[:, 64:80] = np.cos(ang)
    cos[:, 80:96] = np.cos(ang)
    sin[:, 64:80] = np.sin(ang)
    sin[:, 80:96] = np.sin(ang)
    return jnp.asarray(cos), jnp.asarray(sin)


def _norm_matmul(h, g, w, tn, name):
    lp, d = h.shape
    n = w.shape[1]
    tb = TILE_MM

    def body(h_ref, g_ref, w_ref, xn_ref, y_ref):
        @pl.when(pl.program_id(1) == 0)
        def _():
            xn_ref[...] = _rms(h_ref[...], g_ref[...]).astype(BF16)

        y_ref[...] = _dot(xn_ref[...], w_ref[...])

    return pl.pallas_call(
        body, name=name, grid=(lp // tb, n // tn),
        in_specs=[pl.BlockSpec((tb, d), lambda i, j: (i, 0)), pl.BlockSpec((1, d), lambda i, j: (0, 0)),
                  pl.BlockSpec((d, tn), lambda i, j: (0, j))],
        out_specs=[pl.BlockSpec((tb, d), lambda i, j: (i, 0)), pl.BlockSpec((tb, tn), lambda i, j: (i, j))],
        out_shape=[SDS((lp, d), BF16), SDS((lp, n), F32)],
        compiler_params=_params(("parallel", "arbitrary")),
    )(h, g, w)


def _matmul_residual(a, w, res, name):
    m, k = a.shape
    n = w.shape[1]
    tb = TILE_MM

    def body(a_ref, w_ref, r_ref, o_ref):
        o_ref[...] = r_ref[...] + _dot(a_ref[...], w_ref[...])

    return pl.pallas_call(
        body, name=name, grid=(m // tb,),
        in_specs=[pl.BlockSpec((tb, k), lambda i: (i, 0)), pl.BlockSpec((k, n), lambda i: (0, 0)),
                  pl.BlockSpec((tb, n), lambda i: (i, 0))],
        out_specs=pl.BlockSpec((tb, n), lambda i: (i, 0)),
        out_shape=SDS((m, n), F32),
        compiler_params=_params(("parallel",)),
    )(a, w, res)


def _matmul_nt(dy, w, tn, name):
    m, k = dy.shape
    n = w.shape[0]
    tb = TILE_MM

    def body(dy_ref, w_ref, o_ref):
        o_ref[...] = _dot_nt(dy_ref[...].astype(BF16), w_ref[...])

    return pl.pallas_call(
        body, name=name, grid=(m // tb, n // tn),
        in_specs=[pl.BlockSpec((tb, k), lambda i, j: (i, 0)), pl.BlockSpec((tn, k), lambda i, j: (j, 0))],
        out_specs=pl.BlockSpec((tb, tn), lambda i, j: (i, j)),
        out_shape=SDS((m, n), F32),
        compiler_params=_params(("parallel", "arbitrary")),
    )(dy, w)


def _matmul_tn(x, dy, tn, name):
    m, k = x.shape
    n = dy.shape[1]
    tb = TILE_MM
    nb = m // tb

    def body(x_ref, dy_ref, o_ref, acc):
        i = pl.program_id(1)

        @pl.when(i == 0)
        def _():
            acc[...] = jnp.zeros_like(acc)

        acc[...] += _dot_tn(x_ref[...].astype(BF16), dy_ref[...].astype(BF16))

        @pl.when(i == nb - 1)
        def _():
            o_ref[...] = acc[...].astype(BF16)

    return pl.pallas_call(
        body, name=name, grid=(n // tn, nb),
        in_specs=[pl.BlockSpec((tb, k), lambda j, i: (i, 0)), pl.BlockSpec((tb, tn), lambda j, i: (i, j))],
        out_specs=pl.BlockSpec((k, tn), lambda j, i: (0, j)),
        out_shape=SDS((k, n), BF16),
        scratch_shapes=[pltpu.VMEM((k, tn), F32)],
        compiler_params=_params(("parallel", "arbitrary")),
    )(x, dy)


def _norm_matmul_bwd(dys, w, x, g, dres, name):
    m, d = x.shape
    tb = TILE_MM
    widths = [a.shape[1] for a in dys]
    n_dy = len(dys)

    def body(*refs):
        w_ref, x_ref, g_ref, r_ref, o_ref, dg_ref = refs[n_dy:]

        @pl.when(pl.program_id(0) == 0)
        def _():
            dg_ref[...] = jnp.zeros_like(dg_ref)

        dxn, off = None, 0
        for dy_ref, width in zip(refs[:n_dy], widths):
            part = _dot_nt(dy_ref[...], w_ref[:, off:off + width])
            dxn = part if dxn is None else dxn + part
            off += width
        _, vjp = jax.vjp(_rms, x_ref[...], g_ref[...])
        dx, dg = vjp(dxn)
        o_ref[...] = r_ref[...] + dx
        dg_ref[...] += dg

    row = pl.BlockSpec((tb, d), lambda i: (i, 0))
    vec = pl.BlockSpec((1, d), lambda i: (0, 0))
    return pl.pallas_call(
        body, name=name, grid=(m // tb,),
        in_specs=[pl.BlockSpec((tb, wd), lambda i: (i, 0)) for wd in widths]
        + [pl.BlockSpec(w.shape, lambda i: (0, 0)), row, vec, row],
        out_specs=[row, vec],
        out_shape=[SDS((m, d), F32), SDS((1, d), F32)],
        compiler_params=_params(("arbitrary",)),
    )(*dys, w, x, g, dres)


def _prep_math(pieces, prm, consts, cos, sin):
    fq, fk, sq, sk, sv, cq, ckv, misc = pieces
    gfq, gfk, gsq, gsk, fb, gqa, gkva, gmq, gmk, wq, wkk, wkv = prm
    e64, e64t, e128, e128t, skm, skt, dup, dupt = consts
    cos8 = jnp.concatenate([cos] * HEADS, axis=1)
    sin8 = jnp.concatenate([sin] * HEADS, axis=1)
    fq_n = _gnorm(fq, gfq, e64, e64t, 64)
    fk_n = _gnorm(fk, gfk, e64, e64t, 64)
    ls = jax.nn.log_sigmoid(misc + fb)
    q = _gnorm(_mm(_rms(cq, gqa), wq), gmq, e128, e128t, 96)
    mq = q * cos8 + _rot(q) * sin8
    kva = _rms(ckv, gkva)
    k = _gnorm(_mm(kva, wkk) + _sel(misc, skm, skt), gmk, e128, e128t, 96)
    mk = k * cos8 + _rot(k) * sin8
    mv = _mm(kva, wkv)
    sq_n = _gnorm(sq, gsq, e64, e64t, 64)
    sk_n = _gnorm(sk, gsk, e64[0:128], e64t[:, 0:128], 64)
    skd = _sel(sk_n, dup, dupt)
    svd = _sel(sv, dup, dupt)
    return fq_n, fk_n, ls, mq, mk, mv, sq_n, skd, svd


_PIECES = [(O_FQ, 512), (O_FK, 512), (O_SQ, 512), (O_SK, 128), (O_SV, 128), (O_CQ, 256), (O_CKV, 128), (O_MISC, 128)]
_PRM_SHAPES = [(1, 512), (1, 512), (1, 512), (1, 128), (1, 128), (1, 256), (1, 128), (1, 1024), (1, 1024),
               (256, 1024), (128, 1024), (128, 512)]
_CONST_SHAPES = [(512, 128), (128, 512), (1024, 128), (128, 1024), (128, 1024), (1024, 128), (128, 256), (256, 128)]


def _piece_specs(tb):
    def spec(off, width):
        blk = (GATES_W + off) // width
        return pl.BlockSpec((tb, width), lambda i, blk=blk: (i, blk))
    return [spec(o, w) for o, w in _PIECES] + [spec(O_FV, 512)]


def _full_specs(shapes):
    return [pl.BlockSpec(s, lambda i: (0, 0)) for s in shapes]


def _prep_fwd(proj, prm, consts, cos, sin, name):
    lp = proj.shape[0]
    tb = TILE_ROW
    row = lambda w: pl.BlockSpec((tb, w), lambda i: (i, 0))

    def body(*refs):
        pieces = [r[...] for r in refs[0:8]]
        fv = refs[8][...]
        prm_v = [r[...] for r in refs[9:21]]
        consts_v = [r[...] for r in refs[21:29]]
        cos_v, sin_v = refs[29][...], refs[30][...]
        outs = refs[31:]
        fq_n, fk_n, ls, mq, mk, mv, sq_n, skd, svd = _prep_math(pieces, prm_v, consts_v, cos_v, sin_v)
        for ref, val in zip(outs, (fq_n, fk_n, fv, mq, mk, mv, sq_n, skd, svd)):
            ref[...] = val.astype(BF16)
        outs[9][...] = ls

    widths = [512, 512, 512, 1024, 1024, 512, 512, 256, 256]
    return pl.pallas_call(
        body, name=name, grid=(lp // tb,),
        in_specs=_piece_specs(tb) + _full_specs(_PRM_SHAPES) + _full_specs(_CONST_SHAPES) + [row(128), row(128)],
        out_specs=[row(w) for w in widths] + [row(128)],
        out_shape=[SDS((lp, w), BF16) for w in widths] + [SDS((lp, 128), F32)],
        compiler_params=_params(("parallel",)),
    )(*([proj] * 9), *prm, *consts, cos, sin)


def _prep_bwd(proj, prm, consts, cos, sin, cots, folds, name):
    lp = proj.shape[0]
    tb = TILE_ROW
    row = lambda w: pl.BlockSpec((tb, w), lambda i: (i, 0))
    fold64, fold128 = folds

    def body(*refs):
        pieces = [r[...] for r in refs[0:8]]
        prm_v = [r[...] for r in refs[9:21]]
        consts_v = [r[...] for r in refs[21:29]]
        cos_v, sin_v = refs[29][...], refs[30][...]
        dfq, dfk, dfv, dmq, dmk, dmv, dsq, dskd, dsvd, dls = [r[...] for r in refs[31:41]]
        f64, f128 = refs[41][...], refs[42][...]
        d_ref = refs[43]
        g_refs = refs[44:]

        @pl.when(pl.program_id(0) == 0)
        def _():
            for r in g_refs:
                r[...] = jnp.zeros_like(r)

        f = lambda pc, pr: _prep_math(pc, pr, consts_v, cos_v, sin_v)
        _, vjp = jax.vjp(f, pieces, prm_v)
        dpc, dprm = vjp((dfq, dfk, dls, dmq, dmk, dmv, dsq, dskd, dsvd))
        d_fq, d_fk, d_sq, d_sk, d_sv, d_cq, d_ckv, d_misc = dpc
        for off, val in ((O_FQ, d_fq), (O_FK, d_fk), (O_FV, dfv), (O_SQ, d_sq), (O_SK, d_sk), (O_SV, d_sv),
                         (O_CQ, d_cq), (O_CKV, d_ckv), (O_MISC, d_misc)):
            d_ref[:, off:off + val.shape[1]] = val.astype(BF16)
        folded = {0: f64, 1: f64, 2: f64, 3: f64[0:128], 7: f128, 8: f128}
        for idx, (ref, val) in enumerate(zip(g_refs, dprm)):
            if idx in folded:
                ref[...] += _split_dot(jnp.broadcast_to(val, (8, val.shape[1])), folded[idx], 3)
            elif val.shape[0] == 1:
                ref[...] += jnp.broadcast_to(val, ref.shape)
            else:
                ref[...] += val

    g_shapes = [(8, 128), (8, 128), (8, 128), (8, 128), (8, 128), (8, 256), (8, 128), (8, 128), (8, 128),
                (256, 1024), (128, 1024), (128, 512)]
    cot_widths = [512, 512, 512, 1024, 1024, 512, 512, 256, 256, 128]
    return pl.pallas_call(
        body, name=name, grid=(lp // tb,),
        in_specs=(_piece_specs(tb) + _full_specs(_PRM_SHAPES) + _full_specs(_CONST_SHAPES) + [row(128), row(128)]
                  + [row(w) for w in cot_widths] + _full_specs([(512, 128), (1024, 128)])),
        out_specs=[row(OTHER_W)] + _full_specs(g_shapes),
        out_shape=[SDS((lp, OTHER_W), BF16)] + [SDS(s, F32) for s in g_shapes],
        compiler_params=_params(("arbitrary",)),
    )(*([proj] * 9), *prm, *consts, cos, sin, *cots, fold64, fold128)


def _cumsum(xs, reverse, name):
    lp = xs[0].shape[0]
    tb = TILE_MM
    nb = lp // tb
    n_in = len(xs)
    idx = (lambda i: (nb - 1 - i, 0)) if reverse else (lambda i: (i, 0))

    def body(*refs):
        o_ref, carry = refs[n_in], refs[n_in + 1]

        @pl.when(pl.program_id(0) == 0)
        def _():
            carry[...] = jnp.zeros_like(carry)

        x = refs[0][...]
        for r in refs[1:n_in]:
            x = x + r[...]
        r_i = lax.broadcasted_iota(jnp.int32, (tb, tb), 0)
        c_i = lax.broadcasted_iota(jnp.int32, (tb, tb), 1)
        tri = ((c_i >= r_i) if reverse else (c_i <= r_i)).astype(BF16)
        acc, rest = None, x
        for _ in range(3):
            part = rest.astype(BF16)
            rest = rest - part.astype(F32)
            acc = _dot(tri, part) if acc is None else acc + _dot(tri, part)
        o_ref[...] = acc + carry[...]
        carry[...] += jnp.sum(x, axis=0, keepdims=True)

    return pl.pallas_call(
        body, name=name, grid=(nb,),
        in_specs=[pl.BlockSpec((tb, 128), idx)] * n_in,
        out_specs=pl.BlockSpec((tb, 128), idx),
        out_shape=SDS((lp, 128), F32),
        scratch_shapes=[pltpu.VMEM((1, 128), F32)],
        compiler_params=_params(("arbitrary",)),
    )(*xs)


class _Att:
    def __init__(self, mode):
        self.mode = mode
        self.wide = mode == "mla"
        self.qw = 256 if self.wide else 128
        self.scale = (96 if mode == "mla" else 64) ** -0.5
        self.pairs = 2 * PAIRS if mode == "swa" else PAIRS

    def resident(self, x, lo, scaled):
        if self.wide:
            return x[:, 0:128], x[:, 128:256]
        if scaled:
            x = x * jnp.asarray(self.scale, x.dtype)
        zero = jnp.zeros_like(x)
        return jnp.where(lo, x, zero), jnp.where(lo, zero, x)

    def moving(self, x):
        return (x[:, 0:128], x[:, 128:256]) if self.wide else (x, x)

    def logits(self, a, b, qpos, kpos, key_decay, slope, masked):
        s = _dot_nt(a, b)
        if self.wide:
            s = s * self.scale
        if self.mode == "fox":
            s = s - key_decay
        if self.mode == "swa":
            s = s - slope * (qpos - kpos).astype(F32)
        if masked:
            ok = kpos <= qpos
            if self.mode == "swa":
                ok = ok & ((kpos < N_META) | (qpos - kpos < WINDOW))
            s = jnp.where(ok, s, NEG)
        return s


def _as_rows(col):
    return jnp.broadcast_to(col, (col.shape[0], 128)).T[0:8, :]


def _halves(x, lo):
    zero = jnp.zeros_like(x)
    return jnp.where(lo, x, zero), jnp.where(lo, zero, x)


def _kv_specs(att, lp, rows):
    row = lambda g, i: (i if rows != lp else 0, g)
    if att.mode == "swa":
        return (pl.BlockSpec((rows, 64 * att.pairs), row),) * 2
    return pl.BlockSpec((rows, att.pairs * att.qw), row), pl.BlockSpec((rows, att.pairs * 128), row)


def _pair_cols(att, x, pp, width):
    return x if x.shape[1] == width else x[:, pp * width:(pp + 1) * width]


def _kv_cols(att, x, pp, width):
    return _pair_cols(att, x, pp // 2 if att.mode == "swa" else pp, width)


def _att_fwd(att, q, k, v, extra, name):
    lp = q.shape[0]
    t = TILE_ATT
    nq = lp // t
    qw = att.qw
    mode = att.mode
    pairs = att.pairs
    nh = 2 * pairs

    def body(*refs):
        q_ref, k_ref, v_ref = refs[0:3]
        o_ref, lse_ref = refs[-2:]
        g, qi = pl.program_id(0), pl.program_id(1)
        lo = lax.broadcasted_iota(jnp.int32, (1, 128), 1) < 64
        q_all = q_ref[...]
        q_heads = [h for pp in range(pairs) for h in att.resident(_pair_cols(att, q_all, pp, qw), lo, True)]
        qpos = qi * t + lax.broadcasted_iota(jnp.int32, (t, 1), 0)

        def step(first, cols, carry, masked):
            ks = pl.multiple_of(first, 128)
            kc, vc = k_ref[pl.ds(ks, cols), :], v_ref[pl.ds(ks, cols), :]
            kpos = first + lax.broadcasted_iota(jnp.int32, (1, cols), 1)
            out = []
            for h in range(nh):
                pp = h // 2
                m, l, acc = carry[3 * h:3 * h + 3]
                k_h = att.moving(_kv_cols(att, kc, pp, qw))[h % 2]
                decay = refs[3][h, :, pl.ds(ks, cols)] if mode == "fox" else None
                slope = refs[4][nh * g + h] if mode == "swa" else None
                s = att.logits(q_heads[h], k_h, qpos, kpos, decay, slope, masked)
                m_new = jnp.maximum(m, jnp.max(s, axis=-1, keepdims=True))
                alpha = jnp.exp(m - m_new)
                pe = jnp.exp(s - m_new)
                l = alpha * l + jnp.sum(pe, axis=-1, keepdims=True)
                acc = alpha * acc + _dot(pe.astype(BF16), _kv_cols(att, vc, pp, 128))
                out += [m_new, l, acc]
            return tuple(out)

        init = []
        for h in range(nh):
            if mode == "swa":
                init += [jnp.full((t, 1), refs[3][nh * g + h], F32), jnp.ones((t, 1), F32)]
            else:
                init += [jnp.full((t, 1), NEG, F32), jnp.zeros((t, 1), F32)]
            init.append(jnp.zeros((t, 128), F32))
        if mode == "swa":
            band = jnp.maximum(qi * t - WINDOW, 0)
            carry = lax.fori_loop(0, (band >= 128).astype(jnp.int32), lambda j, c: step(0, 128, c, True), tuple(init))
            carry = step(band, t + WINDOW, carry, True)
        else:
            carry = lax.fori_loop(0, qi // 2, lambda j, c: step(2 * j * t, 2 * t, c, False), tuple(init))
            carry = lax.fori_loop(0, qi % 2, lambda j, c: step((qi - 1) * t, t, c, False), carry)
            carry = step(qi * t, t, carry, True)
        outs = []
        for pp in range(pairs):
            (ma, la, acca), (mb, lb, accb) = carry[6 * pp:6 * pp + 3], carry[6 * pp + 3:6 * pp + 6]
            outs.append(jnp.where(lo, acca / la, accb / lb).astype(BF16))
            lse_ref[2 * pp] = ma + jnp.log(la)
            lse_ref[2 * pp + 1] = mb + jnp.log(lb)
        o_ref[...] = jnp.concatenate(outs, axis=1)

    in_specs = [pl.BlockSpec((t, pairs * qw), lambda g, i: (i, g)), *_kv_specs(att, lp, lp)]
    if mode == "fox":
        in_specs += [pl.BlockSpec((nh, 1, lp), lambda g, i: (g, 0, 0))]
    if mode == "swa":
        in_specs += [pl.BlockSpec(memory_space=pltpu.SMEM)] * 2
    return pl.pallas_call(
        body, name=name, grid=(4 // pairs, nq), in_specs=in_specs,
        out_specs=[pl.BlockSpec((t, pairs * 128), lambda g, i: (i, g)), pl.BlockSpec((nh, t, 1), lambda g, i: (g, i, 0))],
        out_shape=[SDS((lp, 512), BF16), SDS((HEADS, lp, 1), F32)],
        compiler_params=_params(("parallel", "arbitrary")),
    )(q, k, v, *extra)


def _att_bwd(att, q, k, v, o, do, lse, extra, name):
    lp = q.shape[0]
    t = TILE_ATT
    nq = lp // t
    qw = att.qw
    mode = att.mode
    pairs = att.pairs
    nh = 2 * pairs
    kw = 64 * pairs if mode == "swa" else pairs * qw
    vw = 64 * pairs if mode == "swa" else pairs * 128

    def body(*refs):
        q_ref, k_ref, v_ref, o_ref, do_ref, lse_ref = refs[0:6]
        n_out = {"fox": 5, "mla": 3, "swa": 4}[mode]
        outs = refs[len(refs) - n_out:]
        dq_ref, dk_ref, dv_ref = outs[0:3]
        g, qi = pl.program_id(0), pl.program_id(1)

        @pl.when(qi == 0)
        def _():
            dk_ref[...] = jnp.zeros_like(dk_ref)
            dv_ref[...] = jnp.zeros_like(dv_ref)
            if mode == "fox":
                outs[4][...] = jnp.zeros_like(outs[4])

        lo = lax.broadcasted_iota(jnp.int32, (1, 128), 1) < 64
        q_all, do_all = q_ref[...], do_ref[...]
        prod = do_all.astype(F32) * o_ref[...].astype(F32)
        q_heads, q_plain, do_heads, do_pairs, delta = [], [], [], [], []
        for pp in range(pairs):
            q_pp = _pair_cols(att, q_all, pp, qw)
            q_heads += att.resident(q_pp, lo, True)
            q_plain += att.moving(q_pp)
            do_pp = _pair_cols(att, do_all, pp, 128)
            do_pairs.append(do_pp)
            do_heads += _halves(do_pp, lo)
            pr_pp = _pair_cols(att, prod, pp, 128)
            delta += [jnp.sum(jnp.where(lo, pr_pp, 0.0), axis=-1, keepdims=True),
                      jnp.sum(jnp.where(lo, 0.0, pr_pp), axis=-1, keepdims=True)]
        lse_v = [lse_ref[h] for h in range(nh)]
        qpos = qi * t + lax.broadcasted_iota(jnp.int32, (t, 1), 0)

        def step(first, cols, carry, masked):
            ks = pl.multiple_of(first, 128)
            kc, vc = k_ref[pl.ds(ks, cols), :], v_ref[pl.ds(ks, cols), :]
            kpos = first + lax.broadcasted_iota(jnp.int32, (1, cols), 1)
            out, dk_parts, dv_parts = [], [], []
            for h in range(nh):
                pp = h // 2
                k_h = att.moving(_kv_cols(att, kc, pp, qw))[h % 2]
                decay = refs[6][h, :, pl.ds(ks, cols)] if mode == "fox" else None
                slope = refs[7][nh * g + h] if mode == "swa" else None
                s = att.logits(q_heads[h], k_h, qpos, kpos, decay, slope, masked)
                pr = jnp.exp(s - lse_v[h])
                ds = pr * (_dot_nt(do_heads[h], _kv_cols(att, vc, pp, 128)) - delta[h])
                dsb = ds.astype(BF16)
                out.append(carry[2 * h] + _dot(dsb, k_h))
                out.append(carry[2 * h + 1] + jnp.sum(ds, axis=-1, keepdims=True) if mode == "fox" else carry[2 * h + 1])
                dk_parts.append(_dot_tn(dsb, q_plain[h]))
                dv_parts.append(_dot_tn(pr.astype(BF16), do_pairs[pp]))
                if mode == "fox":
                    outs[4][h, 0:1, pl.ds(ks, cols)] -= jnp.sum(ds, axis=0, keepdims=True)
            rows = pl.ds(ks, cols)
            for pp in range(pairs):
                dv_pp = jnp.where(lo, dv_parts[2 * pp], dv_parts[2 * pp + 1])
                if att.wide:
                    dk_pp = jnp.concatenate(dk_parts[2 * pp:2 * pp + 2], axis=1) * att.scale
                else:
                    dk_pp = jnp.where(lo, dk_parts[2 * pp], dk_parts[2 * pp + 1]) * att.scale
                if mode == "swa":
                    dk_ref[rows, (pp // 2) * 128:(pp // 2 + 1) * 128] += dk_pp
                    dv_ref[rows, (pp // 2) * 128:(pp // 2 + 1) * 128] += dv_pp
                else:
                    dk_ref[rows, pp * qw:(pp + 1) * qw] += dk_pp
                    dv_ref[rows, pp * 128:(pp + 1) * 128] += dv_pp
            return tuple(out)

        init = (jnp.zeros((t, 128), F32), jnp.zeros((t, 1), F32)) * nh
        if mode == "swa":
            band = jnp.maximum(qi * t - WINDOW, 0)
            carry = lax.fori_loop(0, (band >= 128).astype(jnp.int32), lambda j, c: step(0, 128, c, True), init)
            carry = step(band, t + WINDOW, carry, True)
        else:
            carry = lax.fori_loop(0, qi // 2, lambda j, c: step(2 * j * t, 2 * t, c, False), init)
            carry = lax.fori_loop(0, qi % 2, lambda j, c: step((qi - 1) * t, t, c, False), carry)
            carry = step(qi * t, t, carry, True)
        dq = []
        for pp in range(pairs):
            dqa, dca, dqb, dcb = carry[4 * pp:4 * pp + 4]
            dq += [dqa, dqb] if att.wide else [jnp.where(lo, dqa, dqb)]
            if mode == "fox":
                outs[3][2 * pp] = _as_rows(dca)
                outs[3][2 * pp + 1] = _as_rows(dcb)
        dq_ref[...] = jnp.concatenate(dq, axis=1) * att.scale
        if mode == "swa":
            ds_ref = outs[3]

            @pl.when(qi == 0)
            def _():
                ds_ref[...] = jnp.zeros_like(ds_ref)

            lane = lax.broadcasted_iota(jnp.int32, (8, 128), 1)
            acc = jnp.zeros((8, 128), F32)
            for h in range(nh):
                tot = -jnp.sum(jnp.exp(refs[6][nh * g + h] - lse_v[h]) * delta[h])
                acc = acc + jnp.where(lane == h, tot, 0.0)
            ds_ref[0] += acc

    col = pl.BlockSpec((nh, t, 1), lambda g, i: (g, i, 0))
    in_specs = [pl.BlockSpec((t, pairs * qw), lambda g, i: (i, g)), *_kv_specs(att, lp, lp),
                pl.BlockSpec((t, pairs * 128), lambda g, i: (i, g)), pl.BlockSpec((t, pairs * 128), lambda g, i: (i, g)), col]
    out_specs = [pl.BlockSpec((t, pairs * qw), lambda g, i: (i, g)), pl.BlockSpec((lp, kw), lambda g, i: (0, g)),
                 pl.BlockSpec((lp, vw), lambda g, i: (0, g))]
    n_groups = 4 // pairs
    out_shape = [SDS((lp, 4 * qw), F32), SDS((lp, n_groups * kw), F32), SDS((lp, n_groups * vw), F32)]
    if mode == "fox":
        in_specs += [pl.BlockSpec((nh, 1, lp), lambda g, i: (g, 0, 0))]
        out_specs += [pl.BlockSpec((nh, 8, t), lambda g, i: (g, 0, i)), pl.BlockSpec((nh, 8, lp), lambda g, i: (g, 0, 0))]
        out_shape += [SDS((HEADS, 8, lp), F32)] * 2
    if mode == "swa":
        in_specs += [pl.BlockSpec(memory_space=pltpu.SMEM)] * 2
        out_specs.append(pl.BlockSpec((1, 8, 128), lambda g, i: (g, 0, 0)))
        out_shape.append(SDS((n_groups, 8, 128), F32))
    return pl.pallas_call(
        body, name=name, grid=(n_groups, nq), in_specs=in_specs, out_specs=out_specs, out_shape=out_shape,
        compiler_params=_params(("parallel", "arbitrary")),
    )(q, k, v, o, do, lse, *extra)


def _post_fwd(h, proj, outs, wb, wo, name):
    lp, d = h.shape
    tb = TILE_POST
    row = lambda w: pl.BlockSpec((tb, w), lambda i: (i, 0))

    def body(h_ref, g0, g1, g2, oa, ob, oc, wb_ref, wo_ref, o_ref):
        merged = jnp.zeros((tb, d), F32)
        for n, (g_ref, br) in enumerate(((g0, oa), (g1, ob), (g2, oc))):
            merged = merged + jax.nn.sigmoid(g_ref[...]) * _dot(br[...], wb_ref[n])
        o_ref[...] = h_ref[...] + _dot(merged.astype(BF16), wo_ref[...])

    gate = lambda n: pl.BlockSpec((tb, d), lambda i, n=n: (i, n))
    return pl.pallas_call(
        body, name=name, grid=(lp // tb,),
        in_specs=[row(d), gate(0), gate(1), gate(2), row(512), row(512), row(512),
                  pl.BlockSpec((3, 512, d), lambda i: (0, 0, 0)), pl.BlockSpec((d, d), lambda i: (0, 0))],
        out_specs=row(d), out_shape=SDS((lp, d), F32),
        compiler_params=_params(("parallel",)),
    )(h, proj, proj, proj, *outs, wb, wo)


def _post_bwd(dh, proj, outs, wb, wo, name):
    lp, d = dh.shape
    tb = TILE_POST
    row = lambda w: pl.BlockSpec((tb, w), lambda i: (i, 0))

    def body(dh_ref, g0, g1, g2, oa, ob, oc, wb_ref, wo_ref, dg_ref, doa, dob, doc, dwb_ref, dwo_ref):
        @pl.when(pl.program_id(0) == 0)
        def _():
            dwb_ref[...] = jnp.zeros_like(dwb_ref)
            dwo_ref[...] = jnp.zeros_like(dwo_ref)

        dhb = dh_ref[...].astype(BF16)
        dm = _dot_nt(dhb, wo_ref[...])
        merged = jnp.zeros((tb, d), F32)
        for n, (g_ref, br, do_ref) in enumerate(((g0, oa, doa), (g1, ob, dob), (g2, oc, doc))):
            gate = jax.nn.sigmoid(g_ref[...])
            o_n = br[...]
            y = _dot(o_n, wb_ref[n])
            merged = merged + gate * y
            dy = (dm * gate).astype(BF16)
            dg_ref[:, n * d:(n + 1) * d] = (dm * y * gate * (1.0 - gate)).astype(BF16)
            do_ref[...] = _dot_nt(dy, wb_ref[n]).astype(BF16)
            dwb_ref[n] += _dot_tn(o_n, dy)
        dwo_ref[...] += _dot_tn(merged.astype(BF16), dhb)

    gate = lambda n: pl.BlockSpec((tb, d), lambda i, n=n: (i, n))
    wb_spec = pl.BlockSpec((3, 512, d), lambda i: (0, 0, 0))
    wo_spec = pl.BlockSpec((d, d), lambda i: (0, 0))
    return pl.pallas_call(
        body, name=name, grid=(lp // tb,),
        in_specs=[row(d), gate(0), gate(1), gate(2), row(512), row(512), row(512), wb_spec, wo_spec],
        out_specs=[row(GATES_W), row(512), row(512), row(512), wb_spec, wo_spec],
        out_shape=[SDS((lp, GATES_W), BF16)] + [SDS((lp, 512), BF16)] * 3 + [SDS((3, 512, d), F32), SDS((d, d), F32)],
        compiler_params=_params(("arbitrary",)),
    )(dh, proj, proj, proj, *outs, wb, wo)


def _shift_down(x, halo, n, first):
    rows = lax.broadcasted_iota(jnp.int32, x.shape, 0)
    halo = jnp.where(first, 0.0, halo)
    edge = jnp.concatenate([pltpu.roll(halo, n, 0), jnp.zeros((x.shape[0] - 8, x.shape[1]), F32)], axis=0)
    return jnp.where(rows < n, edge, pltpu.roll(x, n, 0))


def _shift_up(x, halo, n, last):
    tb = x.shape[0]
    rows = lax.broadcasted_iota(jnp.int32, x.shape, 0)
    halo = jnp.where(last, 0.0, halo)
    edge = jnp.concatenate([jnp.zeros((tb - 8, x.shape[1]), F32), pltpu.roll(halo, 8 - n, 0)], axis=0)
    return jnp.where(rows >= tb - n, edge, pltpu.roll(x, tb - n, 0))


def _conv(u, halo, w_ref, b_ref, first):
    taps = (_shift_down(u, halo, 2, first), _shift_down(u, halo, 1, first), u)
    c = b_ref[...] + w_ref[0:1, :] * taps[0] + w_ref[1:2, :] * taps[1] + w_ref[2:3, :] * taps[2]
    return c, taps


def _ffn_specs(tb, f):
    hb = tb // 8
    cur = lambda c: pl.BlockSpec((tb, f), lambda i, c=c: (i, c))
    prev = lambda c: pl.BlockSpec((8, f), lambda i, c=c: (jnp.maximum(i * hb - 1, 0), c))
    vec = lambda r, c: pl.BlockSpec((r, f), lambda i, c=c: (0, c))
    return cur, prev, vec


def _ffn_act_fwd(u, cw, cb, name):
    lp = u.shape[0]
    f = D_FF
    tb = TILE_ROW
    cur, prev, vec = _ffn_specs(tb, f)

    def body(ug, uv, hg, hv, wg, wv, bg, bv, o_ref):
        first = pl.program_id(0) == 0
        cg, _ = _conv(ug[...], hg[...], wg, bg, first)
        cv, _ = _conv(uv[...], hv[...], wv, bv, first)
        o_ref[...] = (cg * jax.nn.sigmoid(cg) * cv).astype(BF16)

    return pl.pallas_call(
        body, name=name, grid=(lp // tb,),
        in_specs=[cur(0), cur(1), prev(0), prev(1), vec(8, 0), vec(8, 1), vec(1, 0), vec(1, 1)],
        out_specs=pl.BlockSpec((tb, f), lambda i: (i, 0)), out_shape=SDS((lp, f), BF16),
        compiler_params=_params(("parallel",)),
    )(u, u, u, u, cw, cw, cb, cb)


def _ffn_act_bwd_conv(u, dh, w_down, cw, cb, name):
    lp = u.shape[0]
    f = D_FF
    tb = TILE_CONV_BWD
    cur, prev, vec = _ffn_specs(tb, f)

    def body(ug, uv, hg, hv, wg, wv, bg, bv, dh_ref, wd_ref, dcg_ref, dcv_ref, dwg, dwv, dbg, dbv):
        first = pl.program_id(0) == 0

        @pl.when(first)
        def _():
            for r in (dwg, dwv, dbg, dbv):
                r[...] = jnp.zeros_like(r)

        cg, tg = _conv(ug[...], hg[...], wg, bg, first)
        cv, tv = _conv(uv[...], hv[...], wv, bv, first)
        da = _dot_nt(dh_ref[...].astype(BF16), wd_ref[...])
        sg = jax.nn.sigmoid(cg)
        dcg = da * cv * sg * (1.0 + cg * (1.0 - sg))
        dcv = da * cg * sg
        dcg_ref[...] = dcg
        dcv_ref[...] = dcv
        for dc, taps, dw, db in ((dcg, tg, dwg, dbg), (dcv, tv, dwv, dbv)):
            for n in range(3):
                dw[n:n + 1, :] += jnp.sum(dc * taps[n], axis=0, keepdims=True)
            db[0:1, :] += jnp.sum(dc, axis=0, keepdims=True)

    row = pl.BlockSpec((tb, f), lambda i: (i, 0))
    acc = pl.BlockSpec((8, f), lambda i: (0, 0))
    return pl.pallas_call(
        body, name=name, grid=(lp // tb,),
        in_specs=[cur(0), cur(1), prev(0), prev(1), vec(8, 0), vec(8, 1), vec(1, 0), vec(1, 1),
                  pl.BlockSpec((tb, dh.shape[1]), lambda i: (i, 0)), pl.BlockSpec(w_down.shape, lambda i: (0, 0))],
        out_specs=[row, row, acc, acc, acc, acc],
        out_shape=[SDS((lp, f), F32)] * 2 + [SDS((8, f), F32)] * 4,
        compiler_params=_params(("arbitrary",)),
    )(u, u, u, u, cw, cw, cb, cb, dh, w_down)


def _ffn_act_bwd_in(dcg, dcv, cw, name):
    lp = dcg.shape[0]
    f = D_FF
    tb = TILE_ROW
    nb = lp // tb
    hb = tb // 8
    cur = pl.BlockSpec((tb, f), lambda i: (i, 0))
    nxt = pl.BlockSpec((8, f), lambda i: (jnp.minimum((i + 1) * hb, nb * hb - 1), 0))
    vec = lambda c: pl.BlockSpec((8, f), lambda i, c=c: (0, c))

    def body(dg, dv, ng, nv, wg, wv, o_ref):
        last = pl.program_id(0) == nb - 1
        for c, (dc_ref, n_ref, w_ref) in enumerate(((dg, ng, wg), (dv, nv, wv))):
            dc, halo = dc_ref[...], n_ref[...]
            du = (w_ref[2:3, :] * dc + w_ref[1:2, :] * _shift_up(dc, halo, 1, last)
                  + w_ref[0:1, :] * _shift_up(dc, halo, 2, last))
            o_ref[:, c * f:(c + 1) * f] = du.astype(BF16)

    return pl.pallas_call(
        body, name=name, grid=(nb,),
        in_specs=[cur, cur, nxt, nxt, vec(0), vec(1)],
        out_specs=pl.BlockSpec((tb, 2 * f), lambda i: (i, 0)), out_shape=SDS((lp, 2 * f), BF16),
        compiler_params=_params(("parallel",)),
    )(dcg, dcv, dcg, dcv, cw, cw)


def _matmul_residual_loss(a, w, res, target, n_real, name):
    m, k = a.shape
    d = w.shape[1]
    tb = TILE_MM

    def body(a_ref, w_ref, r_ref, t_ref, dy_ref, loss_ref):
        i = pl.program_id(0)

        @pl.when(i == 0)
        def _():
            loss_ref[...] = jnp.zeros_like(loss_ref)

        y = r_ref[...] + _dot(a_ref[...], w_ref[...])
        rows = i * tb + lax.broadcasted_iota(jnp.int32, (tb, 1), 0)
        real = (rows >= N_META) & (rows < N_META + n_real)
        diff = jnp.where(real, y - t_ref[...], 0.0)
        dy_ref[...] = diff * (1.0 / d)
        loss_ref[...] += (0.5 / d) * jnp.sum(diff * diff).reshape(1, 1)

    row = pl.BlockSpec((tb, d), lambda i: (i, 0))
    return pl.pallas_call(
        body, name=name, grid=(m // tb,),
        in_specs=[pl.BlockSpec((tb, k), lambda i: (i, 0)), pl.BlockSpec((k, d), lambda i: (0, 0)), row, row],
        out_specs=[row, pl.BlockSpec((1, 1), lambda i: (0, 0))],
        out_shape=[SDS((m, d), F32), SDS((1, 1), F32)],
        compiler_params=_params(("arbitrary",)),
    )(a, w, res, target)


def _pad_lanes(v, width, at=0):
    return jnp.pad(v.astype(F32), (at, width - at - v.shape[0]))[None, :]


_IN_COLS = dict(fq=(0, 512), fk=(512, 512), fv=(1024, 512), ff=(1536, 8), cq=(1544, 256), ckv=(1800, 128),
                kr=(1928, 32), sq=(1960, 512), sk=(2472, 128), sv=(2600, 128), gates=(2728, 3072))


def _orig_cols(src, start, width):
    if src.ndim == 2:
        return [src[:, start:start + width]]
    per, out, pos = src.shape[2], [], start
    while pos < start + width:
        d, off = divmod(pos, per)
        take = min(per - off, start + width - pos)
        out.append(src[d, :, off:off + take])
        pos += take
    return out


class _ColumnSegments:
    def __init__(self, segments):
        self.segments = segments

    def full(self):
        return jnp.concatenate([a[:, s:s + w] for a, s, w in self.segments], axis=1)

    def blocks(self, n):
        per = sum(w for _, _, w in self.segments) // n
        out, seg, used = [], 0, 0
        for _ in range(n):
            pieces, need = [], per
            while need:
                a, s, w = self.segments[seg]
                take = min(w - used, need)
                pieces.append(a[:, s + used:s + used + take])
                used, need = used + take, need - take
                if used == w:
                    seg, used = seg + 1, 0
            out.append(jnp.concatenate(pieces, axis=1))
        return jnp.stack(out)


def _mix_params(w, big, l):
    b = lambda a: a.astype(BF16)
    win = big["w_in"]
    order = ("gates", "fq", "fk", "fv", "sq", "sk", "sv", "cq", "ckv", "kr", "ff")
    pieces = [p for name in order for p in _orig_cols(win, *_IN_COLS[name])]
    w_in = b(jnp.concatenate(pieces + [jnp.zeros((D_MODEL, 88), win.dtype)], axis=1))
    wq = jnp.pad(big["mla_w_q_up"].reshape(256, HEADS, 96), ((0, 0), (0, 0), (0, 32))).reshape(256, 1024)
    wkv = big["mla_w_kv_up"].reshape(128, HEADS, 128)
    wkk = jnp.pad(wkv[:, :, :64], ((0, 0), (0, 0), (0, 64))).reshape(128, 1024)
    wkvv = wkv[:, :, 64:].reshape(128, 512)
    tile = lambda g, n: jnp.tile(g.astype(F32), n)[None, :]
    prm = [tile(w["fox_q_g"][l], 8), tile(w["fox_k_g"][l], 8), tile(w["swa_q_g"][l], 8), tile(w["swa_k_g"][l], 2),
           _pad_lanes(w["fox_forget_b"][l], 128, FF_LANE), w["mla_q_a_g"][l][None, :], w["mla_kv_a_g"][l][None, :],
           tile(jnp.pad(w["mla_q_g"][l], (0, 32)), 8), tile(jnp.pad(w["mla_k_g"][l], (0, 32)), 8),
           wq.astype(F32), wkk.astype(F32), wkvv.astype(F32)]
    return dict(g1=w["norm1_g"][l][None, :], w_in=w_in, prm=prm, sinks=w["swa_sinks"][l].astype(F32),
                wb=b(big["w_branch"]), wo=b(big["w_o"]))


def _ffn_params(w, big, l):
    cw = jnp.pad(w["ffn_conv_w"][l].astype(F32), ((0, 5), (0, 0)))
    return dict(g2=w["norm2_g"][l][None, :], w_up=big["ffn_w_up"].astype(BF16), cw=cw,
                cb=w["ffn_conv_b"][l][None, :].astype(F32), w_down=big["ffn_w_down"].astype(BF16))


def _decay_rows(c):
    return c[:, FF_LANE:FF_LANE + HEADS].T[:, None, :]


def _from_rows(row):
    return jnp.pad(row[:, 0, :].T, ((0, 0), (FF_LANE, 128 - FF_LANE - HEADS)))


def _layer_fwd_mix(h, lw, consts, cos, sin, slopes, l):
    tag = f"l{l}_"
    xn, proj = _norm_matmul(h, lw["g1"], lw["w_in"], IN_W, tag + "in_proj")
    fq, fk, fv, mq, mk, mv, sq, skd, svd, ls = _prep_fwd(proj, lw["prm"], consts, cos, sin, tag + "prep")
    c = _cumsum([ls], False, tag + "decay_cumsum")
    c_row = _decay_rows(c)
    oa, lse_a = _att_fwd(_Att("fox"), fq, fk, fv, (c_row,), tag + "fox_fwd")
    ob, lse_b = _att_fwd(_Att("mla"), mq, mk, mv, (), tag + "mla_fwd")
    oc, lse_c = _att_fwd(_Att("swa"), sq, skd, svd, (lw["sinks"], slopes), tag + "swa_fwd")
    h2 = _post_fwd(h, proj, (oa, ob, oc), lw["wb"], lw["wo"], tag + "merge")
    saved = dict(h=h, xn=xn, proj=proj, q=(fq, mq, sq), k=(fk, mk, skd), v=(fv, mv, svd), c=c_row,
                 o=(oa, ob, oc), lse=(lse_a, lse_b, lse_c), h2=h2)
    return h2, saved


def _layer_fwd_ffn(h2, lw, l, loss_of=None):
    tag = f"l{l}_"
    xn2, u = _norm_matmul(h2, lw["g2"], lw["w_up"], 2 * D_FF, tag + "ffn_up")
    act = _ffn_act_fwd(u, lw["cw"], lw["cb"], tag + "ffn_act")
    if loss_of is None:
        out = _matmul_residual(act, lw["w_down"], h2, tag + "ffn_down")
    else:
        out = _matmul_residual_loss(act, lw["w_down"], h2, *loss_of, tag + "ffn_down_loss")
    return out, dict(xn2=xn2, u=u, act=act)


def _layer_bwd_ffn(dh3, lw, sv, l):
    tag = f"l{l}_"
    f = D_FF
    dw_down = _matmul_tn(sv["act"], dh3, D_MODEL, tag + "ffn_down_dw")
    dcg, dcv, dwg, dwv, dbg, dbv = _ffn_act_bwd_conv(sv["u"], dh3, lw["w_down"], lw["cw"], lw["cb"], tag + "ffn_act_dc")
    du = _ffn_act_bwd_in(dcg, dcv, lw["cw"], tag + "ffn_act_du")
    dw_up = _matmul_tn(sv["xn2"], du, f, tag + "ffn_up_dw")
    dh2, dg2 = _norm_matmul_bwd([du], lw["w_up"], sv["h2"], lw["g2"], dh3, tag + "ffn_up_dx")
    g = dict(norm2_g=dg2[0], ffn_w_up=dw_up, ffn_conv_w=jnp.concatenate([dwg[0:3], dwv[0:3]], axis=1),
             ffn_conv_b=jnp.concatenate([dbg[0], dbv[0]]), ffn_w_down=dw_down)
    return dh2, g


def _layer_bwd_mix(dh2, lw, sv, consts, folds, cos, sin, slopes, l, hook=None, merge_hook=None):
    tag = f"l{l}_"
    dgates, doa, dob, doc, dwb, dwo = _post_bwd(dh2, sv["proj"], sv["o"], lw["wb"], lw["wo"], tag + "merge_bwd")
    c_row = sv["c"]
    tick = merge_hook({"w_branch": dwb, "w_o": dwo}) if merge_hook else None
    if tick is not None:
        c_row = c_row + tick
    extras = ((c_row,), (), (lw["sinks"], slopes))
    grads = []
    for n, (mode, do) in enumerate((("fox", doa), ("mla", dob), ("swa", doc))):
        res = _att_bwd(_Att(mode), sv["q"][n], sv["k"][n], sv["v"][n], sv["o"][n], do, sv["lse"][n], extras[n],
                       tag + mode + "_bwd")
        grads.append((res[0], res[1], res[2], res[3:]))
    (dfq, dfk, dfv, (dcq, dck)), (dmq, dmk, dmv, _), (dsq, dskd, dsvd, (dsink,)) = grads
    dls = _cumsum([_from_rows(dcq), _from_rows(dck)], True, tag + "decay_cumsum_bwd")
    res = _prep_bwd(sv["proj"], lw["prm"], consts, cos, sin,
                    (dfq, dfk, dfv, dmq, dmk, dmv, dsq, dskd, dsvd, dls), folds, tag + "prep_bwd")
    dother, pg = res[0], res[1:]
    dh, dg1 = _norm_matmul_bwd([dgates, dother], lw["w_in"], sv["h"], lw["g1"], dh2, tag + "in_proj_dx")
    dw_g = _matmul_tn(sv["xn"], dgates, GATES_W, tag + "in_proj_dw_gates")
    dw_o = _matmul_tn(sv["xn"], dother, OTHER_W, tag + "in_proj_dw_other")
    d_in = _ColumnSegments([
        (dw_o, O_FQ, 1536), (dw_o, O_MISC + FF_LANE, 8), (dw_o, O_CQ, 256), (dw_o, O_CKV, 128), (dw_o, O_MISC, 32),
        (dw_o, O_SQ, 512), (dw_o, O_SK, 128), (dw_o, O_SV, 128), (dw_g, 0, GATES_W)])
    d_wq = pg[9].reshape(256, HEADS, 128)[:, :, :96].reshape(256, 768)
    d_wkv = jnp.concatenate([pg[10].reshape(128, HEADS, 128)[:, :, :64], pg[11].reshape(128, HEADS, 64)],
                            axis=2).reshape(128, 1024)
    g = dict(
        w_in=d_in, fox_forget_b=pg[4][0, FF_LANE:FF_LANE + 8], fox_q_g=pg[0][0, :64],
        fox_k_g=pg[1][0, :64], mla_q_a_g=pg[5][0], mla_w_q_up=d_wq, mla_kv_a_g=pg[6][0], mla_w_kv_up=d_wkv,
        mla_q_g=pg[7][0, :96], mla_k_g=pg[8][0, :96], swa_q_g=pg[2][0, :64], swa_k_g=pg[3][0, :64],
        swa_sinks=dsink[:, 0, :HEADS // dsink.shape[0]].reshape(HEADS), w_branch=dwb, w_o=dwo)
    tick = hook(g) if hook else None
    g["norm1_g"] = dg1[0]
    return dh, g, tick


_MIX_BIG = ("w_in", "mla_w_q_up", "mla_w_kv_up", "w_branch", "w_o")
_FFN_BIG = ("ffn_w_up", "ffn_w_down")


def _local_step(x, target, w, hook=None, fetch=None):
    if fetch is None:
        fetch = lambda l, stage, after: {n: w[n][l] for n in (_MIX_BIG if stage == "mix" else _FFN_BIG)}
    seq = x.shape[0]
    length = N_META + seq
    lp = -(-length // ROW_ALIGN) * ROW_ALIGN
    pad = lp - length
    h = jnp.concatenate([w["meta_tokens"].astype(F32), x, jnp.zeros((pad, D_MODEL), F32)], axis=0)
    tgt = jnp.pad(target, ((N_META, pad), (0, 0)))
    consts = _consts()
    folds = (_fold_matrix(512, 64), _fold_matrix(1024, 128))
    cos, sin = _rope_tables(lp)
    slopes = jnp.asarray(2.0 ** (-8.0 * np.arange(1, HEADS + 1, dtype=np.float32) / HEADS), F32)
    lws, saved = [], []
    for l in range(DEPTH):
        lw = _mix_params(w, fetch(l, "mix", h), l)
        h, sv = _layer_fwd_mix(h, lw, consts, cos, sin, slopes, l)
        lw.update(_ffn_params(w, fetch(l, "ffn", h), l))
        h, sv_ffn = _layer_fwd_ffn(h, lw, l, (tgt, seq) if l == DEPTH - 1 else None)
        lws.append(lw)
        saved.append({**sv, **sv_ffn})
    dh, loss = h
    grads = [None] * DEPTH
    for l in reversed(range(DEPTH)):
        dh, g_ffn = _layer_bwd_ffn(dh, lws[l], saved[l], l)
        tick = hook(l, "ffn", g_ffn) if hook else None
        if tick is not None:
            lws[l]["sinks"] = lws[l]["sinks"] + tick
        mix_hook = (lambda g, l=l, g_ffn=g_ffn: hook(l, "mix", {**g_ffn, **g})) if hook else None
        merge_hook = (lambda g, l=l: hook(l, "merge", g)) if hook else None
        dh, g_mix, tick = _layer_bwd_mix(dh, lws[l], saved[l], consts, folds, cos, sin, slopes, l, mix_hook, merge_hook)
        grads[l] = {**g_ffn, **g_mix}
        if tick is not None and l > 0:
            lws[l - 1]["cw"] = lws[l - 1]["cw"] + tick
    return loss, dh[N_META:length], dh[:N_META], grads


def _place():
    return lax.axis_index("x"), lax.axis_index("y"), lax.axis_index("c")


def _flip(pos, k):
    x, y, c = pos
    return (1 - x if k & 4 else x, 1 - y if k & 2 else y, 1 - c if k & 1 else c)


def _index(pos):
    return 4 * pos[0] + 2 * pos[1] + pos[2]


def _gather(tensors, name):
    n_t = len(tensors)

    def body(*refs):
        ins, outs = refs[:n_t], refs[n_t:2 * n_t]
        send_sems, recv_sems, local_sems = refs[2 * n_t:]
        x, y, c = _place()
        me, sibling = (x, y, c), (x, y, 1 - c)
        chips = [(1 - x, y), (x, 1 - y), (1 - x, 1 - y)]

        def copy(t, k, block, to, src=None):
            dst = outs[t].at[_index(block)]
            return pltpu.make_async_remote_copy(
                src_ref=dst if src is None else src, dst_ref=dst, send_sem=send_sems.at[t, k],
                recv_sem=recv_sems.at[t, k], device_id=to, device_id_type=pl.DeviceIdType.MESH)

        local, sent = [], []
        for t in range(n_t):
            local.append(pltpu.make_async_copy(ins[t], outs[t].at[_index(me)], local_sems.at[t]))
            local[-1].start()
            sent.append(copy(t, 0, me, sibling, src=ins[t]))
            sent += [copy(t, 1 + j, me, (*chip, c), src=ins[t]) for j, chip in enumerate(chips)]
        for cp in sent:
            cp.start()
        for j, chip in enumerate(chips):
            for t in range(n_t):
                copy(t, 1 + j, (*chip, c), me).wait_recv()
                sent.append(copy(t, 4 + j, (*chip, c), sibling))
                sent[-1].start()
        for t in range(n_t):
            copy(t, 0, sibling, me).wait_recv()
            for j, chip in enumerate(chips):
                copy(t, 4 + j, (*chip, 1 - c), me).wait_recv()
        for cp in sent:
            cp.wait_send()
        for cp in local:
            cp.wait()

    any_spec = pl.BlockSpec(memory_space=pl.ANY)
    return pl.pallas_call(
        body, name=name, in_specs=[any_spec] * n_t, out_specs=[any_spec] * n_t,
        out_shape=[SDS((N_DEV,) + a.shape, a.dtype) for a in tensors],
        scratch_shapes=[pltpu.SemaphoreType.DMA((n_t, N_DEV - 1)), pltpu.SemaphoreType.DMA((n_t, N_DEV - 1)),
                        pltpu.SemaphoreType.DMA((n_t,))],
    )(*tensors)


def _exchange_start(tensors, name, gather=False, after=None):
    n_t = len(tensors)

    def body(*refs):
        ins, lands = refs[:n_t], refs[n_t:2 * n_t]
        send_sem, recv_sem = refs[2 * n_t + 1:2 * n_t + 3]
        token = refs[-1]
        me = _place()
        mine = _index(me)
        for t in range(n_t):
            for k in range(1, N_DEV):
                peer = _flip(me, k)
                pltpu.make_async_remote_copy(
                    src_ref=ins[t] if gather else ins[t].at[_index(peer)], dst_ref=lands[t].at[mine],
                    send_sem=send_sem, recv_sem=recv_sem, device_id=peer, device_id_type=pl.DeviceIdType.MESH).start()
        token[...] = jnp.zeros_like(token)

    hbm = pl.BlockSpec(memory_space=pltpu.HBM)
    sem = pl.BlockSpec(memory_space=pltpu.SEMAPHORE)
    one = pltpu.SemaphoreType.DMA(())
    land_shape = lambda a: ((N_DEV,) + a.shape) if gather else a.shape
    bufs = ([pltpu.HBM(a.shape, a.dtype) for a in tensors] + [pltpu.HBM(land_shape(a), a.dtype) for a in tensors])
    after = jnp.zeros((8, 128), F32) if after is None else after
    outs = pl.pallas_call(
        body, name=name, in_specs=[hbm] * (2 * n_t) + [pl.BlockSpec(memory_space=pl.ANY)],
        out_specs=[sem, sem] + [hbm] * (2 * n_t) + [pl.BlockSpec(memory_space=pltpu.VMEM)],
        out_shape=[one, one] + bufs + [SDS((8, 128), F32)],
        input_output_aliases={i: 2 + i for i in range(2 * n_t)},
        compiler_params=pltpu.CompilerParams(has_side_effects=pltpu.SideEffectType.DATAFLOW_SIDE_EFFECTING),
    )(*[pltpu.with_memory_space_constraint(a, pltpu.HBM) for a in tensors],
      *[pltpu.with_memory_space_constraint(lax.empty(land_shape(a), a.dtype), pltpu.HBM) for a in tensors], after)
    return outs[:-1], outs[-1][0, 0]


def _exchange_wait(state, after, name, gather=False):
    n_t = (len(state) - 2) // 2

    def body(*refs):
        send_sem, recv_sem = refs[0:2]
        ins, lands = refs[2:2 + n_t], refs[2 + n_t:2 + 2 * n_t]
        me = _place()
        for t in range(n_t):
            for k in range(1, N_DEV):
                peer = _flip(me, k)
                copy = pltpu.make_async_remote_copy(
                    src_ref=ins[t] if gather else ins[t].at[_index(peer)], dst_ref=lands[t].at[_index(peer)],
                    send_sem=send_sem, recv_sem=recv_sem, device_id=peer, device_id_type=pl.DeviceIdType.MESH)
                copy.wait_send()
                copy.wait_recv()

    hbm = pl.BlockSpec(memory_space=pltpu.HBM)
    sem = pl.BlockSpec(memory_space=pltpu.SEMAPHORE)
    bufs = [pltpu.HBM(a.shape, a.dtype) for a in state[2:]]
    outs = pl.pallas_call(
        body, name=name, in_specs=[sem, sem] + [hbm] * (2 * n_t) + [pl.BlockSpec(memory_space=pl.ANY)],
        out_specs=[hbm] * (2 * n_t), out_shape=bufs,
        input_output_aliases={2 + i: i for i in range(2 * n_t)},
        compiler_params=pltpu.CompilerParams(has_side_effects=pltpu.SideEffectType.DATAFLOW_SIDE_EFFECTING),
    )(*state, after)
    return outs[n_t:]


def _sum_slots(parts, name):
    n, rows, w = parts.shape
    tb = 8

    def body(p_ref, o_ref):
        acc = p_ref[0].astype(F32)
        for s in range(1, n):
            acc = acc + p_ref[s].astype(F32)
        o_ref[...] = acc

    return pl.pallas_call(
        body, name=name, grid=(rows // tb,),
        in_specs=[pl.BlockSpec((n, tb, w), lambda i: (0, i, 0))], out_specs=pl.BlockSpec((tb, w), lambda i: (i, 0)),
        out_shape=SDS((rows, w), F32), compiler_params=_params(("parallel",)),
    )(parts)


def _adamw(wt, m, v, parts, name, own=None, after=None):
    shape = wt.shape
    parts = parts if isinstance(parts, (list, tuple)) else [parts]
    n, w = parts[0].shape[0], shape[-1]
    rows = math.prod(shape[:-1])
    per = rows // len(parts)
    step = 16 if parts[0].dtype == BF16 else 8
    tb = max([t for t in range(step, 257, step) if per % t == 0] or [per])
    nb = per // tb
    c1 = 1.0 / (1.0 - ADAM_B1 ** ADAM_STEP)
    c2 = 1.0 / (1.0 - ADAM_B2 ** ADAM_STEP)
    state = [a.reshape(rows, w) for a in (wt, m, v)]
    n_in = 4 if own is None else 5
    outs = None
    for l in reversed(range(len(parts))):
        def body(*refs):
            idx_ref = None if own is None else refs[0]
            w_ref, m_ref, v_ref, p_ref = refs[n_in - 4:n_in] if own is None else refs[1:5]
            g_out, d_out, m_out, v_out = refs[-4:]
            g = None
            for s in range(n):
                term = p_ref[s] if own is None else jnp.where(idx_ref[0] == s, refs[5][0], p_ref[s])
                g = term.astype(F32) if g is None else g + term.astype(F32)
            m_new = ADAM_B1 * m_ref[...] + (1.0 - ADAM_B1) * g
            v_new = ADAM_B2 * v_ref[...] + (1.0 - ADAM_B2) * (g * g)
            g_out[...] = g
            m_out[...] = m_new
            v_out[...] = v_new
            d_out[...] = -ADAM_LR * ((m_new * c1) / (jnp.sqrt(v_new * c2) + ADAM_EPS) + ADAM_WD * w_ref[...])

        row = pl.BlockSpec((tb, w), lambda i, *_, l=l: (l * nb + i, 0))
        in_specs = [row, row, row, pl.BlockSpec((n, tb, w), lambda i, *_: (0, i, 0))]
        args = [*state, parts[l].reshape(n, per, w)]
        if own is not None:
            in_specs.append(pl.BlockSpec((1, tb, w), lambda i, idx: (idx[0], i, 0)))
            args.append(own[l].reshape(n, per, w))
        prev = [] if outs is None else list(outs)
        behind = [] if after is None else [after]
        in_specs += [pl.BlockSpec(memory_space=pl.ANY)] * (len(prev) + len(behind))
        n_pre = 0 if own is None else 1
        call = dict(name=f"{name}_{l}", out_shape=[SDS((rows, w), F32)] * 4,
                    input_output_aliases={n_pre + len(args) + k: k for k in range(len(prev))},
                    compiler_params=_params(("parallel",)))
        if own is None:
            outs = pl.pallas_call(body, grid=(nb,), in_specs=in_specs, out_specs=[row] * 4, **call)(*args, *prev, *behind)
        else:
            spec = pltpu.PrefetchScalarGridSpec(num_scalar_prefetch=1, grid=(nb,), in_specs=in_specs, out_specs=[row] * 4)
            idx = jnp.reshape(_index(_place()), (1,)).astype(jnp.int32)
            outs = pl.pallas_call(body, grid_spec=spec, **call)(idx, *args, *prev, *behind)
    return [o.reshape(shape) for o in outs]


_BIG = [("w_in", 2), ("mla_w_q_up", 2), ("mla_w_kv_up", 2), ("w_branch", 3), ("w_o", 1), ("ffn_w_up", 2), ("ffn_w_down", 1)]
_SMALL_SHARDED = [("meta_tokens", 1), ("ffn_conv_w", 2)]
_REPLICATED = ["norm1_g", "fox_forget_b", "fox_q_g", "fox_k_g", "mla_q_a_g", "mla_kv_a_g", "mla_q_g", "mla_k_g",
               "swa_q_g", "swa_k_g", "swa_sinks", "norm2_g", "ffn_conv_b"]
_ORDER = ["meta_tokens", "norm1_g", "w_in", "fox_forget_b", "fox_q_g", "fox_k_g", "mla_q_a_g", "mla_w_q_up",
          "mla_kv_a_g", "mla_w_kv_up", "mla_q_g", "mla_k_g", "swa_q_g", "swa_k_g", "swa_sinks", "w_branch", "w_o",
          "norm2_g", "ffn_w_up", "ffn_conv_w", "ffn_conv_b", "ffn_w_down"]


def _flat_rows(vecs, dtype, row_mult):
    flat = jnp.concatenate([a.reshape(-1).astype(dtype) for a in vecs])
    rows = -(-flat.shape[0] // (1024 * row_mult)) * row_mult
    return jnp.pad(flat, (0, rows * 1024 - flat.shape[0])).reshape(rows, 1024)


def _unflatten(flat, shapes):
    out, off = [], 0
    for s in shapes:
        n = math.prod(s)
        out.append(flat[off:off + n].reshape(s))
        off += n
    return out


def _to_full(blocks, axis):
    moved = jnp.moveaxis(blocks, 0, axis)
    s = moved.shape
    return moved.reshape(s[:axis] + (s[axis] * s[axis + 1],) + s[axis + 2:])


def _to_blocks(full, axis):
    s = full.shape
    split = full.reshape(s[:axis] + (N_DEV, s[axis] // N_DEV) + s[axis + 1:])
    return jnp.moveaxis(split, axis, 0)


def kernel(x, meta_tokens, norm1_g, w_in, fox_forget_b, fox_q_g, fox_k_g, mla_q_a_g, mla_w_q_up, mla_kv_a_g, mla_w_kv_up, mla_q_g, mla_k_g, swa_q_g, swa_k_g, swa_sinks, w_branch, w_o, norm2_g, ffn_w_up, ffn_conv_w, ffn_conv_b, ffn_w_down, loss_target, m_meta_tokens, m_norm1_g, m_w_in, m_fox_forget_b, m_fox_q_g, m_fox_k_g, m_mla_q_a_g, m_mla_w_q_up, m_mla_kv_a_g, m_mla_w_kv_up, m_mla_q_g, m_mla_k_g, m_swa_q_g, m_swa_k_g, m_swa_sinks, m_w_branch, m_w_o, m_norm2_g, m_ffn_w_up, m_ffn_conv_w, m_ffn_conv_b, m_ffn_w_down, v_meta_tokens, v_norm1_g, v_w_in, v_fox_forget_b, v_fox_q_g, v_fox_k_g, v_mla_q_a_g, v_mla_w_q_up, v_mla_kv_a_g, v_mla_w_kv_up, v_mla_q_g, v_mla_k_g, v_swa_q_g, v_swa_k_g, v_swa_sinks, v_w_branch, v_w_o, v_norm2_g, v_ffn_w_up, v_ffn_conv_w, v_ffn_conv_b, v_ffn_w_down):
    wl = dict(zip(_ORDER, (meta_tokens, norm1_g, w_in, fox_forget_b, fox_q_g, fox_k_g, mla_q_a_g, mla_w_q_up,
                           mla_kv_a_g, mla_w_kv_up, mla_q_g, mla_k_g, swa_q_g, swa_k_g, swa_sinks, w_branch, w_o,
                           norm2_g, ffn_w_up, ffn_conv_w, ffn_conv_b, ffn_w_down)))
    ml = dict(zip(_ORDER, (m_meta_tokens, m_norm1_g, m_w_in, m_fox_forget_b, m_fox_q_g, m_fox_k_g, m_mla_q_a_g,
                           m_mla_w_q_up, m_mla_kv_a_g, m_mla_w_kv_up, m_mla_q_g, m_mla_k_g, m_swa_q_g, m_swa_k_g,
                           m_swa_sinks, m_w_branch, m_w_o, m_norm2_g, m_ffn_w_up, m_ffn_conv_w, m_ffn_conv_b,
                           m_ffn_w_down)))
    vl = dict(zip(_ORDER, (v_meta_tokens, v_norm1_g, v_w_in, v_fox_forget_b, v_fox_q_g, v_fox_k_g, v_mla_q_a_g,
                           v_mla_w_q_up, v_mla_kv_a_g, v_mla_w_kv_up, v_mla_q_g, v_mla_k_g, v_swa_q_g, v_swa_k_g,
                           v_swa_sinks, v_w_branch, v_w_o, v_norm2_g, v_ffn_w_up, v_ffn_conv_w, v_ffn_conv_b,
                           v_ffn_w_down)))
    small_sh = [n for n, _ in _SMALL_SHARDED]
    big = [n for n, _ in _BIG]
    axis_of = dict(_BIG)
    idx = _index(_place())

    def to_full(n, blocks, own=None):
        if own is not None:
            sel = (jnp.arange(N_DEV) == idx).reshape((N_DEV,) + (1,) * own.ndim)
            blocks = jnp.where(sel, own[None], blocks)
        return blocks if n == "w_in" else _to_full(blocks, axis_of[n] - 1)

    local = {(n, l): wl[n][l].astype(BF16) for n in big for l in range(DEPTH)}
    got = _gather([local[(n, 0)] for n in _MIX_BIG] + [wl[n] for n in small_sh], "gather_weights_l0_mix")
    full = {n: wl[n] for n in _REPLICATED}
    for (n, axis), blocks in zip(_SMALL_SHARDED, got[len(_MIX_BIG):]):
        full[n] = _to_full(blocks, axis)
    ready = {(n, 0): to_full(n, blocks) for n, blocks in zip(_MIX_BIG, got)}
    later = {"l0_ffn": [(n, 0) for n in _FFN_BIG], "l1": [(n, 1) for n in big]}
    states = {}
    for key, names in later.items():
        states[key], tick = _exchange_start([local[e] for e in names], "gather_weights_" + key + "_start", True, got[0])
        full["norm1_g"] = full["norm1_g"] + tick

    def fetch(l, stage, after):
        key = "l0_ffn" if l == 0 else "l1"
        if (l, stage) != (0, "mix") and key in states:
            lands = _exchange_wait(states.pop(key), after, "gather_weights_" + key + "_wait", True)
            ready.update({e: to_full(e[0], blocks, local[e]) for e, blocks in zip(later[key], lands)})
        return {n: ready[(n, l)] for n in (_MIX_BIG if stage == "mix" else _FFN_BIG)}

    blocks_of = lambda g, names: [(g[n].blocks(N_DEV) if isinstance(g[n], _ColumnSegments)
                                   else _to_blocks(g[n], axis_of[n] - 1)).astype(BF16) for n in names]
    early = {}

    def hook(l, stage, g):
        if l == DEPTH - 1 and stage == "mix":
            key, names = "l1", big
        elif l == 0:
            merge = ("w_branch", "w_o")
            groups = {"ffn": _FFN_BIG, "merge": merge, "mix": tuple(n for n in _MIX_BIG if n not in merge)}
            key, names = "l0_" + stage, groups[stage]
        else:
            return None
        sends = blocks_of(g, names)
        if key == "l0_mix":
            early[key] = (names, l, sends)
            return None
        state, tick = _exchange_start(sends, "exchange_grads_" + key + "_start")
        early[key] = (names, l, sends, state)
        return tick

    loss, grad_x, grad_meta, grads = _local_step(x[0], loss_target[0], full, hook, fetch)
    result = {kind: {} for kind in ("grad", "delta", "new_m", "new_v")}
    small_grads = {k: jnp.stack([grads[l][k] for l in range(DEPTH)]) for k in grads[0] if k not in big}
    small_grads["meta_tokens"] = grad_meta
    small_full = _REPLICATED + small_sh
    mine_small = _flat_rows([small_grads[n] for n in small_full] + [loss], F32, 8)
    small_state, tick = _exchange_start([mine_small], "gather_small_grads_start", True)
    names, l, sends = early["l0_mix"]
    state, tick = _exchange_start(sends, "exchange_grads_l0_mix_start", after=jnp.reshape(tick, (1, 1)))
    early["l0_mix"] = (names, l, sends, state)
    started = jnp.reshape(tick, (1, 1))
    landed, sent = {}, {}
    after = sends[0]
    for key in ("l1", "l0_ffn", "l0_merge"):
        names, l, sends, state = early[key]
        got = _exchange_wait(state, after, "exchange_grads_" + key + "_wait")
        landed.update({(n, l): p for n, p in zip(names, got)})
        sent.update({(n, l): p for n, p in zip(names, sends)})

    def update(names):
        for n in names:
            outs = _adamw(wl[n], ml[n], vl[n], [landed[(n, l)] for l in range(DEPTH)], "adamw_" + n,
                          [sent[(n, l)] for l in range(DEPTH)], started)
            for kind, val in zip(result, outs):
                result[kind][n] = val

    update(_FFN_BIG)
    done = result["delta"]["ffn_w_up"][0, 0, :8] + result["delta"]["ffn_w_down"][0, 0, :8]
    got_small = _exchange_wait(small_state, done, "gather_small_grads_wait", True)[0]
    sel = (jnp.arange(N_DEV) == idx).reshape(N_DEV, 1, 1)
    total_small = _sum_slots(jnp.where(sel, mine_small[None], got_small), "sum_small_grads").reshape(-1)
    pieces = _unflatten(total_small, [small_grads[n].shape for n in small_full] + [()])
    loss_total = pieces[-1]
    g_small = dict(zip(small_full, pieces[:-1]))
    for n, axis in _SMALL_SHARDED:
        size = wl[n].shape[axis]
        g_small[n] = lax.dynamic_slice_in_dim(g_small[n], idx * size, size, axis)
    flat = lambda d: _flat_rows([d[n] for n in small_full], F32, 8)
    small_out = _adamw(flat(wl), flat(ml), flat(vl), flat(g_small)[None], "adamw_small")
    for kind, fs in zip(result, small_out):
        result[kind].update(zip(small_full, _unflatten(fs.reshape(-1), [wl[n].shape for n in small_full])))
    names, l, sends, state = early["l0_mix"]
    got = _exchange_wait(state, small_out[0], "exchange_grads_l0_mix_wait")
    landed.update({(n, l): p for n, p in zip(names, got)})
    sent.update({(n, l): p for n, p in zip(names, sends)})
    update(_MIX_BIG)
    outs = [loss_total, grad_x[None]]
    for kind in ("grad", "delta", "new_m", "new_v"):
        outs += [result[kind][n] for n in _ORDER]
    return tuple(outs)
```
